```python
import math
import jax, jax.numpy as jnp
from jax import lax
import numpy as np

D_MODEL = 1024
BATCH = 8
SEQ = 4096
DEPTH = 2

CHUNK = 64
PLE_DIM = 256
D_MIX = D_MODEL
SSM_WIDTH = D_MIX // 2
SSM_GROUP = 16
SSM_GROUPS = SSM_WIDTH // SSM_GROUP
SSM_STATE = 64
ATTN_WIDTH = D_MIX - SSM_WIDTH
ATTN_HEADS = 8
HEAD_DIM = ATTN_WIDTH // ATTN_HEADS
Q_BLOCK = 128
RMS_EPS = 1e-6
DT_MIN = 1e-3
DT_MAX = 1e-1
IN_COLS = 2 * SSM_WIDTH + 4 * ATTN_WIDTH
SPLITS = [SSM_WIDTH, 2 * SSM_WIDTH, 2 * SSM_WIDTH + ATTN_WIDTH,
          2 * SSM_WIDTH + 2 * ATTN_WIDTH, 2 * SSM_WIDTH + 3 * ATTN_WIDTH]

kernel_name = "hymba_s5_stickbreaking_ple_block"


def rmsnorm(x, g):
    xf = x.astype(jnp.float32)
    xf = xf * lax.rsqrt(jnp.mean(xf * xf, axis=-1, keepdims=True) + RMS_EPS)
    return (xf * g.astype(jnp.float32)).astype(x.dtype)


def _complex_scan_combine(left, right):
    a1r, a1i, x1r, x1i = left
    a2r, a2i, x2r, x2i = right
    ar = a2r * a1r - a2i * a1i
    ai = a2r * a1i + a2i * a1r
    xr = a2r * x1r - a2i * x1i + x2r
    xi = a2r * x1i + a2i * x1r + x2i
    return ar, ai, xr, xi


def s5_branch(u, a_re, a_im, log_dt, b_re, b_im, c_re, c_im, d_skip, w_glu, b_glu):
    f32 = jnp.float32
    bsz, seq, _ = u.shape
    uf = u.astype(f32).reshape(bsz, seq, SSM_GROUPS, SSM_GROUP)
    dt = jnp.exp(log_dt.astype(f32))[:, None]
    lr = a_re.astype(f32)
    li = a_im.astype(f32)
    mag = jnp.exp(lr * dt)
    ab_re = mag * jnp.cos(li * dt)
    ab_im = mag * jnp.sin(li * dt)
    num_re = ab_re - 1.0
    num_im = ab_im
    den = lr * lr + li * li
    f_re = (num_re * lr + num_im * li) / den
    f_im = (num_im * lr - num_re * li) / den
    br = b_re.astype(f32)
    bi = b_im.astype(f32)
    bb_re = f_re[..., None] * br - f_im[..., None] * bi
    bb_im = f_re[..., None] * bi + f_im[..., None] * br
    cr = c_re.astype(f32)
    ci = c_im.astype(f32)

    n_chunks = seq // CHUNK
    u_chunks = uf.reshape(bsz, n_chunks, CHUNK, SSM_GROUPS, SSM_GROUP).transpose(1, 2, 0, 3, 4)
    a_re_l = jnp.broadcast_to(ab_re[None, None], (CHUNK, 1, SSM_GROUPS, SSM_STATE))
    a_im_l = jnp.broadcast_to(ab_im[None, None], (CHUNK, 1, SSM_GROUPS, SSM_STATE))

    def chunk_step(carry, u_c):
        h_re, h_im = carry
        bu_re = jnp.einsum('lbgh,gph->lbgp', u_c, bb_re)
        bu_im = jnp.einsum('lbgh,gph->lbgp', u_c, bb_im)
        p_re, p_im, x_re, x_im = lax.associative_scan(
            _complex_scan_combine, (a_re_l, a_im_l, bu_re, bu_im), axis=0)
        x_re = x_re + p_re * h_re - p_im * h_im
        x_im = x_im + p_re * h_im + p_im * h_re
        y_c = jnp.einsum('lbgp,ghp->lbgh', x_re, cr) - jnp.einsum('lbgp,ghp->lbgh', x_im, ci)
        return (x_re[-1], x_im[-1]), y_c

    h0 = jnp.zeros((bsz, SSM_GROUPS, SSM_STATE), f32)
    _, y = lax.scan(chunk_step, (h0, h0), u_chunks)
    y = y.transpose(2, 0, 1, 3, 4).reshape(bsz, seq, SSM_GROUPS, SSM_GROUP)
    y = (y + d_skip.astype(f32) * uf).reshape(bsz, seq, SSM_WIDTH)
    z = jax.nn.gelu(y)
    zz = z @ w_glu.astype(f32) + b_glu.astype(f32)
    val, gate = jnp.split(zz, 2, axis=-1)
    return (val * jax.nn.sigmoid(gate)).astype(u.dtype)


def stick_breaking_branch(q, k, v, q_g, k_g):
    f32 = jnp.float32
    bsz, seq, _ = q.shape

    def heads(t):
        return t.reshape(bsz, seq, ATTN_HEADS, HEAD_DIM).transpose(0, 2, 1, 3)

    qh = rmsnorm(heads(q), q_g).astype(f32)
    kh = rmsnorm(heads(k), k_g).astype(f32)
    vh = heads(v).astype(f32)
    scale = HEAD_DIM ** -0.5
    outs = []
    for blk in range(seq // Q_BLOCK):
        q0 = blk * Q_BLOCK
        kv_len = q0 + Q_BLOCK
        qb = qh[:, :, q0:kv_len]
        kb = kh[:, :, :kv_len]
        vb = vh[:, :, :kv_len]
        z = jnp.einsum('bhqd,bhkd->bhqk', qb, kb) * scale
        t_pos = q0 + jnp.arange(Q_BLOCK)[:, None]
        s_pos = jnp.arange(kv_len)[None, :]
        strict = s_pos < t_pos
        log_stay = jnp.where(strict, jax.nn.log_sigmoid(-z), 0.0)
        later = lax.cumsum(log_stay, axis=3, reverse=True) - log_stay
        weights = jnp.where(strict, jnp.exp(jax.nn.log_sigmoid(z) + later), 0.0)
        outs.append(jnp.einsum('bhqk,bhkd->bhqd', weights, vb))
    o = jnp.concatenate(outs, axis=2)
    return o.transpose(0, 2, 1, 3).reshape(bsz, seq, ATTN_WIDTH).astype(q.dtype)


def _fwd_setup_inputs(seed: int = 0) -> dict:
    key = jax.random.key(seed)
    ks = jax.random.split(key, 20)
    f32 = jnp.float32
    G, P, H = SSM_GROUPS, SSM_STATE, SSM_GROUP
    n_idx = jnp.arange(P, dtype=f32)
    inputs = {
        "x": jax.random.normal(ks[0], (BATCH, SEQ, D_MODEL), f32),
        "p": jax.random.normal(ks[1], (DEPTH, BATCH, SEQ, PLE_DIM), f32),
        "mix_norm_g": 1.0 + 0.02 * jax.random.normal(ks[2], (DEPTH, D_MODEL), f32),
        "w_in": jax.random.normal(ks[3], (DEPTH, D_MODEL, IN_COLS), f32) * D_MODEL ** -0.5,
        "ssm_a_re": -0.5 + 0.01 * jax.random.normal(ks[4], (DEPTH, G, P), f32),
        "ssm_a_im": math.pi * n_idx + 0.01 * jax.random.normal(ks[5], (DEPTH, G, P), f32),
        "ssm_log_dt": jax.random.uniform(ks[6], (DEPTH, G), f32, math.log(DT_MIN), math.log(DT_MAX)),
        "ssm_b_re": jax.random.normal(ks[7], (DEPTH, G, P, H), f32) * (2.0 * H) ** -0.5,
        "ssm_b_im": jax.random.normal(ks[8], (DEPTH, G, P, H), f32) * (2.0 * H) ** -0.5,
        "ssm_c_re": jax.random.normal(ks[9], (DEPTH, G, H, P), f32) * (2.0 * P) ** -0.5,
        "ssm_c_im": jax.random.normal(ks[10], (DEPTH, G, H, P), f32) * (2.0 * P) ** -0.5,
        "ssm_d": jax.random.normal(ks[11], (DEPTH, G, H), f32),
        "ssm_w_glu": jax.random.normal(ks[12], (DEPTH, SSM_WIDTH, 2 * SSM_WIDTH), f32) * SSM_WIDTH ** -0.5,
        "ssm_b_glu": 0.01 * jax.random.normal(ks[13], (DEPTH, 2 * SSM_WIDTH), f32),
        "q_norm_g": 1.0 + 0.02 * jax.random.normal(ks[14], (DEPTH, HEAD_DIM), f32),
        "k_norm_g": 1.0 + 0.02 * jax.random.normal(ks[15], (DEPTH, HEAD_DIM), f32),
        "w_out": jax.random.normal(ks[16], (DEPTH, D_MIX, D_MODEL), f32) * D_MIX ** -0.5,
        "ple_norm_g": 1.0 + 0.02 * jax.random.normal(ks[17], (DEPTH, D_MODEL), f32),
        "w_ple_gate": jax.random.normal(ks[18], (DEPTH, D_MODEL, D_MODEL), f32) * D_MODEL ** -0.5,
        "w_ple_proj": jax.random.normal(ks[19], (DEPTH, PLE_DIM, D_MODEL), f32) * PLE_DIM ** -0.5,
    }
    return inputs


def _fwd_reference(x, p, mix_norm_g, w_in, ssm_a_re, ssm_a_im, ssm_log_dt, ssm_b_re, ssm_b_im,
              ssm_c_re, ssm_c_im, ssm_d, ssm_w_glu, ssm_b_glu, q_norm_g, k_norm_g, w_out,
              ple_norm_g, w_ple_gate, w_ple_proj):
    h = x
    for i in range(DEPTH):
        hn = rmsnorm(h, mix_norm_g[i])
        proj = hn @ w_in[i]
        u, g_ssm, q, k, v, g_attn = jnp.split(proj, SPLITS, axis=-1)
        y_ssm = s5_branch(u, ssm_a_re[i], ssm_a_im[i], ssm_log_dt[i], ssm_b_re[i], ssm_b_im[i],
                          ssm_c_re[i], ssm_c_im[i], ssm_d[i], ssm_w_glu[i], ssm_b_glu[i])
        y_ssm = y_ssm * jax.nn.silu(g_ssm)
        y_att = stick_breaking_branch(q, k, v, q_norm_g[i], k_norm_g[i]) * jax.nn.silu(g_attn)
        h = h + jnp.concatenate([y_ssm, y_att], axis=-1) @ w_out[i]
        ple_gate = jax.nn.sigmoid(rmsnorm(h, ple_norm_g[i]) @ w_ple_gate[i])
        h = h + ple_gate * (p[i] @ w_ple_proj[i])
    return h


import jax as _jax
import jax.numpy as _jnp

TWIN_FORMAT = 'train_step'
FWD_PARAMS = ['x', 'p', 'mix_norm_g', 'w_in', 'ssm_a_re', 'ssm_a_im', 'ssm_log_dt', 'ssm_b_re', 'ssm_b_im', 'ssm_c_re', 'ssm_c_im', 'ssm_d', 'ssm_w_glu', 'ssm_b_glu', 'q_norm_g', 'k_norm_g', 'w_out', 'ple_norm_g', 'w_ple_gate', 'w_ple_proj']
TWIN_WEIGHTS = ['mix_norm_g', 'w_in', 'ssm_a_re', 'ssm_a_im', 'ssm_log_dt', 'ssm_b_re', 'ssm_b_im', 'ssm_c_re', 'ssm_c_im', 'ssm_d', 'ssm_w_glu', 'ssm_b_glu', 'q_norm_g', 'k_norm_g', 'w_out', 'ple_norm_g', 'w_ple_gate', 'w_ple_proj']
TWIN_DIFF_INPUT = 'x'
TWIN_INPUTS = ['x', 'p', 'mix_norm_g', 'w_in', 'ssm_a_re', 'ssm_a_im', 'ssm_log_dt', 'ssm_b_re', 'ssm_b_im', 'ssm_c_re', 'ssm_c_im', 'ssm_d', 'ssm_w_glu', 'ssm_b_glu', 'q_norm_g', 'k_norm_g', 'w_out', 'ple_norm_g', 'w_ple_gate', 'w_ple_proj', 'loss_target', 'm_mix_norm_g', 'm_w_in', 'm_ssm_a_re', 'm_ssm_a_im', 'm_ssm_log_dt', 'm_ssm_b_re', 'm_ssm_b_im', 'm_ssm_c_re', 'm_ssm_c_im', 'm_ssm_d', 'm_ssm_w_glu', 'm_ssm_b_glu', 'm_q_norm_g', 'm_k_norm_g', 'm_w_out', 'm_ple_norm_g', 'm_w_ple_gate', 'm_w_ple_proj', 'v_mix_norm_g', 'v_w_in', 'v_ssm_a_re', 'v_ssm_a_im', 'v_ssm_log_dt', 'v_ssm_b_re', 'v_ssm_b_im', 'v_ssm_c_re', 'v_ssm_c_im', 'v_ssm_d', 'v_ssm_w_glu', 'v_ssm_b_glu', 'v_q_norm_g', 'v_k_norm_g', 'v_w_out', 'v_ple_norm_g', 'v_w_ple_gate', 'v_w_ple_proj']
TWIN_OUTPUTS = ['loss', 'grad_x', 'grad_mix_norm_g', 'grad_w_in', 'grad_ssm_a_re', 'grad_ssm_a_im', 'grad_ssm_log_dt', 'grad_ssm_b_re', 'grad_ssm_b_im', 'grad_ssm_c_re', 'grad_ssm_c_im', 'grad_ssm_d', 'grad_ssm_w_glu', 'grad_ssm_b_glu', 'grad_q_norm_g', 'grad_k_norm_g', 'grad_w_out', 'grad_ple_norm_g', 'grad_w_ple_gate', 'grad_w_ple_proj', 'delta_mix_norm_g', 'delta_w_in', 'delta_ssm_a_re', 'delta_ssm_a_im', 'delta_ssm_log_dt', 'delta_ssm_b_re', 'delta_ssm_b_im', 'delta_ssm_c_re', 'delta_ssm_c_im', 'delta_ssm_d', 'delta_ssm_w_glu', 'delta_ssm_b_glu', 'delta_q_norm_g', 'delta_k_norm_g', 'delta_w_out', 'delta_ple_norm_g', 'delta_w_ple_gate', 'delta_w_ple_proj', 'new_m_mix_norm_g', 'new_m_w_in', 'new_m_ssm_a_re', 'new_m_ssm_a_im', 'new_m_ssm_log_dt', 'new_m_ssm_b_re', 'new_m_ssm_b_im', 'new_m_ssm_c_re', 'new_m_ssm_c_im', 'new_m_ssm_d', 'new_m_ssm_w_glu', 'new_m_ssm_b_glu', 'new_m_q_norm_g', 'new_m_k_norm_g', 'new_m_w_out', 'new_m_ple_norm_g', 'new_m_w_ple_gate', 'new_m_w_ple_proj', 'new_v_mix_norm_g', 'new_v_w_in', 'new_v_ssm_a_re', 'new_v_ssm_a_im', 'new_v_ssm_log_dt', 'new_v_ssm_b_re', 'new_v_ssm_b_im', 'new_v_ssm_c_re', 'new_v_ssm_c_im', 'new_v_ssm_d', 'new_v_ssm_w_glu', 'new_v_ssm_b_glu', 'new_v_q_norm_g', 'new_v_k_norm_g', 'new_v_w_out', 'new_v_ple_norm_g', 'new_v_w_ple_gate', 'new_v_w_ple_proj']
TWIN_LEAF_KINDS = {'loss': 'loss', 'grad_x': 'grad_x', 'grad_mix_norm_g': 'grad_w', 'grad_w_in': 'grad_w', 'grad_ssm_a_re': 'grad_w', 'grad_ssm_a_im': 'grad_w', 'grad_ssm_log_dt': 'grad_w', 'grad_ssm_b_re': 'grad_w', 'grad_ssm_b_im': 'grad_w', 'grad_ssm_c_re': 'grad_w', 'grad_ssm_c_im': 'grad_w', 'grad_ssm_d': 'grad_w', 'grad_ssm_w_glu': 'grad_w', 'grad_ssm_b_glu': 'grad_w', 'grad_q_norm_g': 'grad_w', 'grad_k_norm_g': 'grad_w', 'grad_w_out': 'grad_w', 'grad_ple_norm_g': 'grad_w', 'grad_w_ple_gate': 'grad_w', 'grad_w_ple_proj': 'grad_w', 'delta_mix_norm_g': 'delta_w', 'delta_w_in': 'delta_w', 'delta_ssm_a_re': 'delta_w', 'delta_ssm_a_im': 'delta_w', 'delta_ssm_log_dt': 'delta_w', 'delta_ssm_b_re': 'delta_w', 'delta_ssm_b_im': 'delta_w', 'delta_ssm_c_re': 'delta_w', 'delta_ssm_c_im': 'delta_w', 'delta_ssm_d': 'delta_w', 'delta_ssm_w_glu': 'delta_w', 'delta_ssm_b_glu': 'delta_w', 'delta_q_norm_g': 'delta_w', 'delta_k_norm_g': 'delta_w', 'delta_w_out': 'delta_w', 'delta_ple_norm_g': 'delta_w', 'delta_w_ple_gate': 'delta_w', 'delta_w_ple_proj': 'delta_w', 'new_m_mix_norm_g': 'new_m', 'new_m_w_in': 'new_m', 'new_m_ssm_a_re': 'new_m', 'new_m_ssm_a_im': 'new_m', 'new_m_ssm_log_dt': 'new_m', 'new_m_ssm_b_re': 'new_m', 'new_m_ssm_b_im': 'new_m', 'new_m_ssm_c_re': 'new_m', 'new_m_ssm_c_im': 'new_m', 'new_m_ssm_d': 'new_m', 'new_m_ssm_w_glu': 'new_m', 'new_m_ssm_b_glu': 'new_m', 'new_m_q_norm_g': 'new_m', 'new_m_k_norm_g': 'new_m', 'new_m_w_out': 'new_m', 'new_m_ple_norm_g': 'new_m', 'new_m_w_ple_gate': 'new_m', 'new_m_w_ple_proj': 'new_m', 'new_v_mix_norm_g': 'new_v', 'new_v_w_in': 'new_v', 'new_v_ssm_a_re': 'new_v', 'new_v_ssm_a_im': 'new_v', 'new_v_ssm_log_dt': 'new_v', 'new_v_ssm_b_re': 'new_v', 'new_v_ssm_b_im': 'new_v', 'new_v_ssm_c_re': 'new_v', 'new_v_ssm_c_im': 'new_v', 'new_v_ssm_d': 'new_v', 'new_v_ssm_w_glu': 'new_v', 'new_v_ssm_b_glu': 'new_v', 'new_v_q_norm_g': 'new_v', 'new_v_k_norm_g': 'new_v', 'new_v_w_out': 'new_v', 'new_v_ple_norm_g': 'new_v', 'new_v_w_ple_gate': 'new_v', 'new_v_w_ple_proj': 'new_v'}


def _forward(args):
    return _fwd_reference(*[args[k] for k in FWD_PARAMS])


def _output_shape():
    out = _jax.eval_shape(lambda: _forward(_fwd_setup_inputs(0)))
    return out.shape, out.dtype

N_MICROBATCH = 1
ADAM_LR = 0.001
ADAM_B1 = 0.9
ADAM_B2 = 0.999
ADAM_EPS = 1e-08
ADAM_WD = 0.01
ADAM_STEP = 10
PER_EXAMPLE_BATCH_AXIS = {'x': 0, 'p': 1, 'loss_target': 0}
SHARED_INPUTS = []
_WEIGHT_DTYPES = {'mix_norm_g': _jnp.float32, 'w_in': _jnp.float32, 'ssm_a_re': _jnp.float32, 'ssm_a_im': _jnp.float32, 'ssm_log_dt': _jnp.float32, 'ssm_b_re': _jnp.float32, 'ssm_b_im': _jnp.float32, 'ssm_c_re': _jnp.float32, 'ssm_c_im': _jnp.float32, 'ssm_d': _jnp.float32, 'ssm_w_glu': _jnp.float32, 'ssm_b_glu': _jnp.float32, 'q_norm_g': _jnp.float32, 'k_norm_g': _jnp.float32, 'w_out': _jnp.float32, 'ple_norm_g': _jnp.float32, 'w_ple_gate': _jnp.float32, 'w_ple_proj': _jnp.float32}
MOMENT_SCALE = {'mix_norm_g': 6.758749e+00, 'w_in': 1.273279e-01, 'ssm_a_re': 5.547057e-03, 'ssm_a_im': 4.937925e-03, 'ssm_log_dt': 1.727148e+00, 'ssm_b_re': 3.366141e-03, 'ssm_b_im': 3.408226e-03, 'ssm_c_re': 7.018219e-03, 'ssm_c_im': 6.919373e-03, 'ssm_d': 1.685525e+00, 'ssm_w_glu': 2.652259e-01, 'ssm_b_glu': 8.981405e-01, 'q_norm_g': 5.739757e+00, 'k_norm_g': 5.746797e+00, 'w_out': 1.402934e-01, 'ple_norm_g': 9.478171e-01, 'w_ple_gate': 8.903263e-02, 'w_ple_proj': 4.206466e-01}


def _to_microbatches(a, axis):
    t = _jnp.moveaxis(a, axis, 0)
    t = t.reshape((N_MICROBATCH, t.shape[0] // N_MICROBATCH) + t.shape[1:])
    return _jnp.moveaxis(t, 1, axis + 1)


def setup_inputs(seed: int = 0) -> dict:
    inp = _fwd_setup_inputs(seed)
    key = _jax.random.fold_in(_jax.random.key(seed), 7919)
    shape, _ = _output_shape()
    out = dict(inp)
    out["loss_target"] = _jax.random.normal(_jax.random.fold_in(key, 0), shape, _jnp.float32)
    for i, name in enumerate(TWIN_WEIGHTS):
        w = inp[name].astype(_jnp.float32)
        if MOMENT_SCALE is None:
            s = _jnp.sqrt(_jnp.mean(_jnp.square(w)) + 1e-30)
        else:
            s = MOMENT_SCALE[name]
        km, kv = _jax.random.split(_jax.random.fold_in(key, i + 1))
        out[name] = w
        out["m_" + name] = s * _jax.random.normal(km, w.shape, _jnp.float32)
        out["v_" + name] = (s * s) * _jax.random.uniform(kv, w.shape, _jnp.float32, 0.5, 1.5)
    if N_MICROBATCH > 1:
        for name, axis in PER_EXAMPLE_BATCH_AXIS.items():
            out[name] = _to_microbatches(out[name], axis)
    return {'x': out['x'], 'p': out['p'], 'mix_norm_g': out['mix_norm_g'], 'w_in': out['w_in'], 'ssm_a_re': out['ssm_a_re'], 'ssm_a_im': out['ssm_a_im'], 'ssm_log_dt': out['ssm_log_dt'], 'ssm_b_re': out['ssm_b_re'], 'ssm_b_im': out['ssm_b_im'], 'ssm_c_re': out['ssm_c_re'], 'ssm_c_im': out['ssm_c_im'], 'ssm_d': out['ssm_d'], 'ssm_w_glu': out['ssm_w_glu'], 'ssm_b_glu': out['ssm_b_glu'], 'q_norm_g': out['q_norm_g'], 'k_norm_g': out['k_norm_g'], 'w_out': out['w_out'], 'ple_norm_g': out['ple_norm_g'], 'w_ple_gate': out['w_ple_gate'], 'w_ple_proj': out['w_ple_proj'], 'loss_target': out['loss_target'], 'm_mix_norm_g': out['m_mix_norm_g'], 'm_w_in': out['m_w_in'], 'm_ssm_a_re': out['m_ssm_a_re'], 'm_ssm_a_im': out['m_ssm_a_im'], 'm_ssm_log_dt': out['m_ssm_log_dt'], 'm_ssm_b_re': out['m_ssm_b_re'], 'm_ssm_b_im': out['m_ssm_b_im'], 'm_ssm_c_re': out['m_ssm_c_re'], 'm_ssm_c_im': out['m_ssm_c_im'], 'm_ssm_d': out['m_ssm_d'], 'm_ssm_w_glu': out['m_ssm_w_glu'], 'm_ssm_b_glu': out['m_ssm_b_glu'], 'm_q_norm_g': out['m_q_norm_g'], 'm_k_norm_g': out['m_k_norm_g'], 'm_w_out': out['m_w_out'], 'm_ple_norm_g': out['m_ple_norm_g'], 'm_w_ple_gate': out['m_w_ple_gate'], 'm_w_ple_proj': out['m_w_ple_proj'], 'v_mix_norm_g': out['v_mix_norm_g'], 'v_w_in': out['v_w_in'], 'v_ssm_a_re': out['v_ssm_a_re'], 'v_ssm_a_im': out['v_ssm_a_im'], 'v_ssm_log_dt': out['v_ssm_log_dt'], 'v_ssm_b_re': out['v_ssm_b_re'], 'v_ssm_b_im': out['v_ssm_b_im'], 'v_ssm_c_re': out['v_ssm_c_re'], 'v_ssm_c_im': out['v_ssm_c_im'], 'v_ssm_d': out['v_ssm_d'], 'v_ssm_w_glu': out['v_ssm_w_glu'], 'v_ssm_b_glu': out['v_ssm_b_glu'], 'v_q_norm_g': out['v_q_norm_g'], 'v_k_norm_g': out['v_k_norm_g'], 'v_w_out': out['v_w_out'], 'v_ple_norm_g': out['v_ple_norm_g'], 'v_w_ple_gate': out['v_w_ple_gate'], 'v_w_ple_proj': out['v_w_ple_proj']}


def _loss(weights, diff, rest, loss_target):
    with _jax.named_scope("forward"):
        args = {**rest, TWIN_DIFF_INPUT: diff, **{k: w.astype(_WEIGHT_DTYPES[k]) for k, w in weights.items()}}
        y = _forward(args)
    with _jax.named_scope("loss_head"):
        err = _jnp.square(y.astype(_jnp.float32) - loss_target)
        return 0.5 * _jnp.sum(_jnp.mean(err, axis=-1)) if err.ndim else 0.5 * err


def _adamw(w, g, m, v):
    m = ADAM_B1 * m + (1.0 - ADAM_B1) * g
    v = ADAM_B2 * v + (1.0 - ADAM_B2) * _jnp.square(g)
    m_hat = m / (1.0 - ADAM_B1 ** ADAM_STEP)
    v_hat = v / (1.0 - ADAM_B2 ** ADAM_STEP)
    delta = -ADAM_LR * (m_hat / (_jnp.sqrt(v_hat) + ADAM_EPS) + ADAM_WD * w)
    return delta, m, v


def reference(x, p, mix_norm_g, w_in, ssm_a_re, ssm_a_im, ssm_log_dt, ssm_b_re, ssm_b_im, ssm_c_re, ssm_c_im, ssm_d, ssm_w_glu, ssm_b_glu, q_norm_g, k_norm_g, w_out, ple_norm_g, w_ple_gate, w_ple_proj, loss_target, m_mix_norm_g, m_w_in, m_ssm_a_re, m_ssm_a_im, m_ssm_log_dt, m_ssm_b_re, m_ssm_b_im, m_ssm_c_re, m_ssm_c_im, m_ssm_d, m_ssm_w_glu, m_ssm_b_glu, m_q_norm_g, m_k_norm_g, m_w_out, m_ple_norm_g, m_w_ple_gate, m_w_ple_proj, v_mix_norm_g, v_w_in, v_ssm_a_re, v_ssm_a_im, v_ssm_log_dt, v_ssm_b_re, v_ssm_b_im, v_ssm_c_re, v_ssm_c_im, v_ssm_d, v_ssm_w_glu, v_ssm_b_glu, v_q_norm_g, v_k_norm_g, v_w_out, v_ple_norm_g, v_w_ple_gate, v_w_ple_proj):
    given = dict(x=x, p=p, mix_norm_g=mix_norm_g, w_in=w_in, ssm_a_re=ssm_a_re, ssm_a_im=ssm_a_im, ssm_log_dt=ssm_log_dt, ssm_b_re=ssm_b_re, ssm_b_im=ssm_b_im, ssm_c_re=ssm_c_re, ssm_c_im=ssm_c_im, ssm_d=ssm_d, ssm_w_glu=ssm_w_glu, ssm_b_glu=ssm_b_glu, q_norm_g=q_norm_g, k_norm_g=k_norm_g, w_out=w_out, ple_norm_g=ple_norm_g, w_ple_gate=w_ple_gate, w_ple_proj=w_ple_proj, loss_target=loss_target, m_mix_norm_g=m_mix_norm_g, m_w_in=m_w_in, m_ssm_a_re=m_ssm_a_re, m_ssm_a_im=m_ssm_a_im, m_ssm_log_dt=m_ssm_log_dt, m_ssm_b_re=m_ssm_b_re, m_ssm_b_im=m_ssm_b_im, m_ssm_c_re=m_ssm_c_re, m_ssm_c_im=m_ssm_c_im, m_ssm_d=m_ssm_d, m_ssm_w_glu=m_ssm_w_glu, m_ssm_b_glu=m_ssm_b_glu, m_q_norm_g=m_q_norm_g, m_k_norm_g=m_k_norm_g, m_w_out=m_w_out, m_ple_norm_g=m_ple_norm_g, m_w_ple_gate=m_w_ple_gate, m_w_ple_proj=m_w_ple_proj, v_mix_norm_g=v_mix_norm_g, v_w_in=v_w_in, v_ssm_a_re=v_ssm_a_re, v_ssm_a_im=v_ssm_a_im, v_ssm_log_dt=v_ssm_log_dt, v_ssm_b_re=v_ssm_b_re, v_ssm_b_im=v_ssm_b_im, v_ssm_c_re=v_ssm_c_re, v_ssm_c_im=v_ssm_c_im, v_ssm_d=v_ssm_d, v_ssm_w_glu=v_ssm_w_glu, v_ssm_b_glu=v_ssm_b_glu, v_q_norm_g=v_q_norm_g, v_k_norm_g=v_k_norm_g, v_w_out=v_w_out, v_ple_norm_g=v_ple_norm_g, v_w_ple_gate=v_w_ple_gate, v_w_ple_proj=v_w_ple_proj)
    weights = {n: given[n] for n in TWIN_WEIGHTS}
    shared = {n: given[n] for n in SHARED_INPUTS}
    per_example = {n: given[n] for n in ['x', 'p']}
    grad_fn = _jax.value_and_grad(_loss, argnums=(0, 1))

    def one_microbatch(ex, loss_target):
        ex = dict(ex)
        diff = ex.pop(TWIN_DIFF_INPUT)
        return grad_fn(weights, diff, {**shared, **ex}, loss_target)

    if N_MICROBATCH == 1:
        loss, (grad_w, grad_x) = one_microbatch(per_example, given["loss_target"])
    else:
        def body(carry, xs):
            loss_sum, grad_sum = carry
            l_k, (gw_k, gx_k) = one_microbatch(xs[0], xs[1])
            with _jax.named_scope("update"):
                return (loss_sum + l_k, _jax.tree.map(_jnp.add, grad_sum, gw_k)), gx_k

        init = (_jnp.zeros((), _jnp.float32), _jax.tree.map(_jnp.zeros_like, weights))
        (loss, grad_w), grad_x = _jax.lax.scan(body, init, (per_example, given["loss_target"]))
    with _jax.named_scope("update"):
        delta_w, new_m, new_v = {}, {}, {}
        for n in TWIN_WEIGHTS:
            delta_w[n], new_m[n], new_v[n] = _adamw(weights[n], grad_w[n], given["m_" + n], given["v_" + n])
    return (loss, grad_x, *[grad_w[n] for n in TWIN_WEIGHTS], *[delta_w[n] for n in TWIN_WEIGHTS],
            *[new_m[n] for n in TWIN_WEIGHTS], *[new_v[n] for n in TWIN_WEIGHTS])
```

```python
import math

import jax
import jax.numpy as jnp
from jax import lax
from jax.experimental import pallas as pl
from jax.experimental.pallas import tpu as pltpu

F32 = jnp.float32
BF16 = jnp.bfloat16
MESH_IDS = pl.DeviceIdType.MESH

N_DEV = 8
D_MODEL = 1024
DEPTH = 2
PLE_DIM = 256
SSM_WIDTH = 512
SSM_GROUP = 16
SSM_GROUPS = 32
SSM_STATE = 64
SSM_LANES = SSM_GROUPS * SSM_STATE
ATTN_WIDTH = 512
ATTN_HEADS = 8
HEAD_DIM = 64
RMS_EPS = 1e-6
ADAM_LR = 0.001
ADAM_B1 = 0.9
ADAM_B2 = 0.999
ADAM_EPS = 1e-08
ADAM_WD = 0.01
ADAM_STEP = 10

VMEM_LIMIT = 56 * 1024 * 1024
ATT_T = 128
SCAN_LANES = 128
SUBLANES = 8
PACK_COLS = 1024

BIG = ("w_in", "ssm_w_glu", "w_out", "w_ple_gate", "w_ple_proj")
SMALL = ("mix_norm_g", "ssm_a_re", "ssm_a_im", "ssm_log_dt", "ssm_b_re", "ssm_b_im", "ssm_c_re",
         "ssm_c_im", "ssm_d", "ssm_b_glu", "q_norm_g", "k_norm_g", "ple_norm_g")
WEIGHTS = ("mix_norm_g", "w_in", "ssm_a_re", "ssm_a_im", "ssm_log_dt", "ssm_b_re", "ssm_b_im",
           "ssm_c_re", "ssm_c_im", "ssm_d", "ssm_w_glu", "ssm_b_glu", "q_norm_g", "k_norm_g",
           "w_out", "ple_norm_g", "w_ple_gate", "w_ple_proj")
BIG_SPLIT_AXIS = {"w_in": 1, "ssm_w_glu": 1, "w_out": 0, "w_ple_gate": 0, "w_ple_proj": 1}


def _params(sem=None):
    return pltpu.CompilerParams(dimension_semantics=sem, vmem_limit_bytes=VMEM_LIMIT)


def _sigmoid(x):
    return 1.0 / (1.0 + jnp.exp(-x))


_GELU_K = math.sqrt(2.0 / math.pi)
_GELU_C = 0.044715


def _gelu(x):
    return 0.5 * x * (1.0 + jnp.tanh(_GELU_K * (x + _GELU_C * x * x * x)))


def _gelu_grad(x):
    th = jnp.tanh(_GELU_K * (x + _GELU_C * x * x * x))
    return 0.5 * (1.0 + th) + 0.5 * x * (1.0 - th * th) * _GELU_K * (1.0 + 3.0 * _GELU_C * x * x)


def _mm(a, b, *, name, ta=False, tb=False, add=None, out_dtype=F32, tm=1024, tn=1024, tk=512):
    m, k = (a.shape[1], a.shape[0]) if ta else a.shape
    n = b.shape[0] if tb else b.shape[1]
    tm, tn, tk = min(tm, m), min(tn, n), min(tk, k)
    assert m % tm == 0 and n % tn == 0 and k % tk == 0, (name, a.shape, b.shape)
    nk = k // tk
    dims = (((0 if ta else 1,), (1 if tb else 0,)), ((), ()))

    def body(*refs):
        if add is None:
            a_ref, b_ref, o_ref, acc_ref = refs
        else:
            a_ref, b_ref, r_ref, o_ref, acc_ref = refs
        kk = pl.program_id(2)

        @pl.when(kk == 0)
        def _():
            acc_ref[...] = jnp.zeros_like(acc_ref)

        acc_ref[...] += lax.dot_general(a_ref[...].astype(BF16), b_ref[...].astype(BF16), dims,
                                        preferred_element_type=F32)

        @pl.when(kk == nk - 1)
        def _():
            res = acc_ref[...]
            if add is not None:
                res = res + r_ref[...].astype(F32)
            o_ref[...] = res.astype(out_dtype)

    a_spec = (pl.BlockSpec((tk, tm), lambda i, j, kk: (kk, i)) if ta
              else pl.BlockSpec((tm, tk), lambda i, j, kk: (i, kk)))
    b_spec = (pl.BlockSpec((tn, tk), lambda i, j, kk: (j, kk)) if tb
              else pl.BlockSpec((tk, tn), lambda i, j, kk: (kk, j)))
    o_spec = pl.BlockSpec((tm, tn), lambda i, j, kk: (i, j))
    ins, specs = [a, b], [a_spec, b_spec]
    if add is not None:
        ins.append(add)
        specs.append(o_spec)
    return pl.pallas_call(
        body, name=name, grid=(m // tm, n // tn, nk), in_specs=specs, out_specs=o_spec,
        out_shape=jax.ShapeDtypeStruct((m, n), out_dtype),
        scratch_shapes=[pltpu.VMEM((tm, tn), F32)],
        compiler_params=_params(("parallel", "parallel", "arbitrary")),
    )(*ins)


def _rows_tile(rows, want=512):
    t = min(rows, want)
    assert rows % t == 0
    return t


def _rowwise(body, name, rows, ins, outs):
    tm = _rows_tile(rows)
    in_specs = []
    for _, cols, cb in ins:
        if cb is None:
            in_specs.append(pl.BlockSpec((1, cols), lambda i: (0, 0)))
        else:
            in_specs.append(pl.BlockSpec((tm, cols), lambda i, cb=cb: (i, cb)))
    out_specs, out_shapes = [], []
    for cols, dtype, is_acc in outs:
        if is_acc:
            out_specs.append(pl.BlockSpec((1, cols), lambda i: (0, 0)))
            out_shapes.append(jax.ShapeDtypeStruct((1, cols), dtype))
        else:
            out_specs.append(pl.BlockSpec((tm, cols), lambda i: (i, 0)))
            out_shapes.append(jax.ShapeDtypeStruct((rows, cols), dtype))
    any_acc = any(o[2] for o in outs)
    return pl.pallas_call(
        body, name=name, grid=(rows // tm,), in_specs=in_specs, out_specs=tuple(out_specs),
        out_shape=tuple(out_shapes),
        compiler_params=_params(("arbitrary",) if any_acc else ("parallel",)))(*[a for a, _, _ in ins])


def _accumulate(ref, value):
    @pl.when(pl.program_id(0) == 0)
    def _():
        ref[...] = jnp.zeros_like(ref)

    ref[...] += value


def _rms_fwd(h, g, *, name):
    rows, d = h.shape

    def body(h_ref, g_ref, o_ref):
        hv = h_ref[...]
        r = lax.rsqrt(jnp.mean(hv * hv, axis=-1, keepdims=True) + RMS_EPS)
        o_ref[...] = (hv * r * g_ref[...]).astype(BF16)

    return _rowwise(body, name, rows, [(h, d, 0), (g, d, None)], [(d, BF16, False)])[0]


def _rms_bwd(h, g, dhn, *, name, dres=None):
    rows, d = h.shape

    def body(*refs):
        if dres is None:
            h_ref, g_ref, dn_ref, dh_ref, dg_ref = refs
        else:
            h_ref, g_ref, dn_ref, dr_ref, dh_ref, dg_ref = refs
        hv = h_ref[...]
        dn = dn_ref[...].astype(F32)
        r = lax.rsqrt(jnp.mean(hv * hv, axis=-1, keepdims=True) + RMS_EPS)
        a = dn * g_ref[...]
        dot = jnp.mean(a * hv, axis=-1, keepdims=True)
        dh = r * a - hv * (r * r * r * dot)
        if dres is not None:
            dh = dh + dr_ref[...]
        dh_ref[...] = dh
        _accumulate(dg_ref, jnp.sum(dn * hv * r, axis=0, keepdims=True))

    ins = [(h, d, 0), (g, d, None), (dhn, d, 0)]
    if dres is not None:
        ins.append((dres, d, 0))
    return _rowwise(body, name, rows, ins, [(d, F32, False), (d, F32, True)])


def _ssm_mid_fwd(ycx, proj, dvec):
    rows = ycx.shape[0]

    def body(yc_ref, u_ref, d_ref, y_ref, z_ref):
        y = yc_ref[...] + d_ref[...] * u_ref[...]
        y_ref[...] = y
        z_ref[...] = _gelu(y).astype(BF16)

    return _rowwise(body, "ssm_mid_fwd", rows,
                    [(ycx, SSM_WIDTH, 0), (proj, SSM_WIDTH, 0), (dvec, SSM_WIDTH, None)],
                    [(SSM_WIDTH, F32, False), (SSM_WIDTH, BF16, False)])


def _ssm_mid_bwd(dz, y, proj, dvec):
    rows = dz.shape[0]

    def body(dz_ref, y_ref, u_ref, d_ref, dy_ref, du_ref, dd_ref):
        dy = dz_ref[...] * _gelu_grad(y_ref[...])
        dy_ref[...] = dy
        du_ref[...] = dy * d_ref[...]
        _accumulate(dd_ref, jnp.sum(dy * u_ref[...], axis=0, keepdims=True))

    return _rowwise(body, "ssm_mid_bwd", rows,
                    [(dz, SSM_WIDTH, 0), (y, SSM_WIDTH, 0), (proj, SSM_WIDTH, 0), (dvec, SSM_WIDTH, None)],
                    [(SSM_WIDTH, F32, False), (SSM_WIDTH, F32, False), (SSM_WIDTH, F32, True)])


def _gate_fwd(zz, bglu, proj, o):
    rows = zz.shape[0]
    w = SSM_WIDTH

    def body(zz_ref, b_ref, gs_ref, o_ref, ga_ref, y_ref):
        zz_v = zz_ref[...] + b_ref[...]
        val, gate = zz_v[:, :w], zz_v[:, w:]
        gs, ga = gs_ref[...], ga_ref[...]
        y_ref[:, :w] = (val * _sigmoid(gate) * (gs * _sigmoid(gs))).astype(BF16)
        y_ref[:, w:] = (o_ref[...] * (ga * _sigmoid(ga))).astype(BF16)

    return _rowwise(body, "gate_fwd", rows,
                    [(zz, 2 * w, 0), (bglu, 2 * w, None), (proj, w, 1), (o, w, 0), (proj, w, 5)],
                    [(2 * w, BF16, False)])[0]


def _gate_bwd(dyc, zz, bglu, proj, o):
    rows = zz.shape[0]
    w = SSM_WIDTH

    def body(dy_ref, zz_ref, b_ref, gs_ref, o_ref, ga_ref, dzz_ref, dgs_ref, do_ref, dga_ref, db_ref):
        zz_v = zz_ref[...] + b_ref[...]
        val, gate = zz_v[:, :w], zz_v[:, w:]
        gs, ga = gs_ref[...], ga_ref[...]
        dys, dya = dy_ref[:, :w], dy_ref[:, w:]
        sg, ss, sa = _sigmoid(gate), _sigmoid(gs), _sigmoid(ga)
        glu = val * sg
        dglu = dys * (gs * ss)
        dgs_ref[...] = dys * glu * (ss * (1.0 + gs * (1.0 - ss)))
        dval = dglu * sg
        dgate = dglu * val * sg * (1.0 - sg)
        dzz_ref[:, :w] = dval.astype(BF16)
        dzz_ref[:, w:] = dgate.astype(BF16)
        do_ref[...] = dya * (ga * sa)
        dga_ref[...] = dya * o_ref[...] * (sa * (1.0 + ga * (1.0 - sa)))
        _accumulate(db_ref, jnp.concatenate([jnp.sum(dval, axis=0, keepdims=True),
                                             jnp.sum(dgate, axis=0, keepdims=True)], axis=1))

    return _rowwise(body, "gate_bwd", rows,
                    [(dyc, 2 * w, 0), (zz, 2 * w, 0), (bglu, 2 * w, None), (proj, w, 1), (o, w, 0), (proj, w, 5)],
                    [(2 * w, BF16, False), (w, F32, False), (w, F32, False), (w, F32, False), (2 * w, F32, True)])


def _ple_out(h2, pgl, pp):
    rows, d = h2.shape

    def body(h_ref, g_ref, p_ref, o_ref):
        o_ref[...] = h_ref[...] + _sigmoid(g_ref[...]) * p_ref[...]

    return _rowwise(body, "ple_out", rows, [(h2, d, 0), (pgl, d, 0), (pp, d, 0)], [(d, F32, False)])[0]


def _ple_bwd(dh3, pgl, pp):
    rows, d = dh3.shape

    def body(dh_ref, g_ref, p_ref, dg_ref, dp_ref):
        sg = _sigmoid(g_ref[...])
        dh = dh_ref[...]
        dg_ref[...] = (dh * p_ref[...] * sg * (1.0 - sg)).astype(BF16)
        dp_ref[...] = (dh * sg).astype(BF16)

    return _rowwise(body, "ple_bwd", rows, [(dh3, d, 0), (pgl, d, 0), (pp, d, 0)],
                    [(d, BF16, False), (d, BF16, False)])


def _loss_head(h, target):
    rows, d = h.shape

    def body(h_ref, t_ref, dh_ref, l_ref):
        err = h_ref[...] - t_ref[...]
        dh_ref[...] = err * (1.0 / d)
        _accumulate(l_ref, jnp.sum(err * err, axis=0, keepdims=True) * (0.5 / d))

    return _rowwise(body, "loss_head", rows, [(h, d, 0), (target, d, 0)], [(d, F32, False), (d, F32, True)])


def _zoh(lr, li, dt):
    mag = jnp.exp(lr * dt)
    ab_re = mag * jnp.cos(li * dt)
    ab_im = mag * jnp.sin(li * dt)
    num_re = ab_re - 1.0
    den = lr * lr + li * li
    f_re = (num_re * lr + ab_im * li) / den
    f_im = (ab_im * lr - num_re * li) / den
    return ab_re, ab_im, f_re, f_im, den


def _ssm_param_fwd(a_re, a_im, log_dt, bt_re, bt_im):
    g, _, p = a_re.shape
    h = bt_re.shape[1]

    def body(lr_ref, li_ref, ldt_ref, br_ref, bi_ref, pr_ref, pi_ref, bbr_ref, bbi_ref):
        dt = jnp.exp(ldt_ref[...])
        ab_re, ab_im, f_re, f_im, _ = _zoh(lr_ref[...], li_ref[...], dt)
        cr, ci = ab_re, ab_im
        for k in range(SUBLANES):
            pr_ref[:, k:k + 1, :] = cr
            pi_ref[:, k:k + 1, :] = ci
            cr, ci = cr * ab_re - ci * ab_im, cr * ab_im + ci * ab_re
        br, bi = br_ref[...], bi_ref[...]
        bbr_ref[...] = f_re * br - f_im * bi
        bbi_ref[...] = f_re * bi + f_im * br

    pw = jax.ShapeDtypeStruct((g, SUBLANES, p), F32)
    large = jax.ShapeDtypeStruct((g, h, p), F32)
    return pl.pallas_call(body, name="ssm_param_fwd", out_shape=(pw, pw, large, large),
                          compiler_params=_params())(a_re, a_im, log_dt, bt_re, bt_im)


def _ssm_param_bwd(a_re, a_im, log_dt, bt_re, bt_im, gab_re, gab_im, gbb_re, gbb_im):
    g, _, p = a_re.shape
    h = bt_re.shape[1]

    def body(lr_ref, li_ref, ldt_ref, br_ref, bi_ref, gar_ref, gai_ref, gbr_ref, gbi_ref,
             dlr_ref, dli_ref, dldt_ref, dbr_ref, dbi_ref):
        lr, li = lr_ref[...], li_ref[...]
        dt = jnp.exp(ldt_ref[...])
        ab_re, ab_im, f_re, f_im, den = _zoh(lr, li, dt)
        br, bi = br_ref[...], bi_ref[...]
        gbr, gbi = gbr_ref[...], gbi_ref[...]
        dbr_ref[...] = f_re * gbr + f_im * gbi
        dbi_ref[...] = f_re * gbi - f_im * gbr
        gf_re = jnp.sum(br * gbr + bi * gbi, axis=1, keepdims=True)
        gf_im = jnp.sum(br * gbi - bi * gbr, axis=1, keepdims=True)
        il_re, il_im = lr / den, -li / den
        ga_re = gar_ref[...] + il_re * gf_re + il_im * gf_im
        ga_im = gai_ref[...] + il_re * gf_im - il_im * gf_re
        q_re = f_re * il_re - f_im * il_im
        q_im = f_re * il_im + f_im * il_re
        gl_re = -(q_re * gf_re + q_im * gf_im)
        gl_im = -(q_re * gf_im - q_im * gf_re)
        dlr_ref[...] = gl_re + dt * (ab_re * ga_re + ab_im * ga_im)
        dli_ref[...] = gl_im + dt * (ab_re * ga_im - ab_im * ga_re)
        la_re = lr * ab_re - li * ab_im
        la_im = lr * ab_im + li * ab_re
        dldt_ref[...] = jnp.sum((la_re * ga_re + la_im * ga_im) * dt, axis=2, keepdims=True)

    small = jax.ShapeDtypeStruct((g, 1, p), F32)
    one = jax.ShapeDtypeStruct((g, 1, 1), F32)
    large = jax.ShapeDtypeStruct((g, h, p), F32)
    return pl.pallas_call(body, name="ssm_param_bwd", out_shape=(small, small, one, large, large),
                          compiler_params=_params())(
                              a_re, a_im, log_dt, bt_re, bt_im, gab_re, gab_im, gbb_re, gbb_im)


def _block_diag(m):
    g, h, p = m.shape
    eye = jnp.eye(g, dtype=m.dtype)
    return (m[:, :, None, :] * eye[:, None, :, None]).reshape(g * h, g * p)


def _block_diag_take(m, g, h, p):
    eye = jnp.eye(g, dtype=m.dtype)
    return jnp.sum(m.reshape(g, h, g, p) * eye[:, None, :, None], axis=2)


def _scan_tile(br, bi, pw_re, pw_im, reverse):
    row = lax.broadcasted_iota(jnp.int32, br.shape, 0)
    xr, xi = br, bi
    for d in (1, 2, 4):
        ar, ai = pw_re[d - 1:d, :], pw_im[d - 1:d, :]
        if reverse:
            keep = row < SUBLANES - d
            sr = jnp.where(keep, pltpu.roll(xr, SUBLANES - d, 0), 0.0)
            si = jnp.where(keep, pltpu.roll(xi, SUBLANES - d, 0), 0.0)
        else:
            keep = row >= d
            sr = jnp.where(keep, pltpu.roll(xr, d, 0), 0.0)
            si = jnp.where(keep, pltpu.roll(xi, d, 0), 0.0)
        xr, xi = xr + ar * sr - ai * si, xi + ar * si + ai * sr
    return xr, xi


def _ssm_scan_fwd(bu, pw_re, pw_im):
    s, two_l = bu.shape
    lanes = two_l // 2
    lb = min(SCAN_LANES, lanes)
    nlb = lanes // lb
    n_tiles = s // SUBLANES

    def body(br_ref, bi_ref, pr_ref, pi_ref, xr_ref, xi_ref, qr_ref, qi_ref):
        pw_r, pw_i = pr_ref[...], pi_ref[...]
        row = lax.broadcasted_iota(jnp.int32, (SUBLANES, lb), 0)

        def step(t, carry):
            cr, ci = carry
            rows = pl.ds(pl.multiple_of(t * SUBLANES, SUBLANES), SUBLANES)
            xr, xi = _scan_tile(br_ref[rows, :], bi_ref[rows, :], pw_r, pw_i, False)
            xr, xi = xr + pw_r * cr - pw_i * ci, xi + pw_r * ci + pw_i * cr
            xr_ref[rows, :] = xr
            xi_ref[rows, :] = xi
            qr_ref[rows, :] = jnp.where(row >= 1, pltpu.roll(xr, 1, 0), cr)
            qi_ref[rows, :] = jnp.where(row >= 1, pltpu.roll(xi, 1, 0), ci)
            last_r = jnp.broadcast_to(xr[SUBLANES - 1:SUBLANES, :], (SUBLANES, lb))
            last_i = jnp.broadcast_to(xi[SUBLANES - 1:SUBLANES, :], (SUBLANES, lb))
            return last_r, last_i

        zero = jnp.zeros((SUBLANES, lb), F32)
        lax.fori_loop(0, n_tiles, step, (zero, zero))

    re_in = pl.BlockSpec((s, lb), lambda j: (0, j))
    im_in = pl.BlockSpec((s, lb), lambda j: (0, j + nlb))
    pw = pl.BlockSpec((SUBLANES, lb), lambda j: (0, j))
    out = jax.ShapeDtypeStruct((s, lanes), F32)
    return pl.pallas_call(
        body, name="ssm_scan_fwd", grid=(nlb,), in_specs=[re_in, im_in, pw, pw],
        out_specs=(re_in,) * 4, out_shape=(out,) * 4,
        compiler_params=_params(("parallel",)))(bu, bu, pw_re, pw_im)


def _ssm_scan_bwd(gx, xprev_re, xprev_im, pw_re, pw_im):
    s, two_l = gx.shape
    lanes = two_l // 2
    lb = min(SCAN_LANES, lanes)
    nlb = lanes // lb
    n_tiles = s // SUBLANES

    def body(gr_ref, gi_ref, qr_ref, qi_ref, pr_ref, pi_ref, lr_ref, li_ref, ar_ref, ai_ref):
        pw_r, pw_i = pr_ref[...], -pi_ref[...]
        pwc_r = jnp.concatenate([pw_r[SUBLANES - 1 - i:SUBLANES - i, :] for i in range(SUBLANES)], axis=0)
        pwc_i = jnp.concatenate([pw_i[SUBLANES - 1 - i:SUBLANES - i, :] for i in range(SUBLANES)], axis=0)

        def step(t, carry):
            cr, ci, acc_r, acc_i = carry
            tile = n_tiles - 1 - t
            rows = pl.ds(pl.multiple_of(tile * SUBLANES, SUBLANES), SUBLANES)
            lr, li = _scan_tile(gr_ref[rows, :], gi_ref[rows, :], pw_r, pw_i, True)
            lr, li = lr + pwc_r * cr - pwc_i * ci, li + pwc_r * ci + pwc_i * cr
            lr_ref[rows, :] = lr
            li_ref[rows, :] = li
            qr, qi = qr_ref[rows, :], qi_ref[rows, :]
            acc_r = acc_r + qr * lr + qi * li
            acc_i = acc_i + qr * li - qi * lr
            first_r = jnp.broadcast_to(lr[0:1, :], (SUBLANES, lb))
            first_i = jnp.broadcast_to(li[0:1, :], (SUBLANES, lb))
            return first_r, first_i, acc_r, acc_i

        zero = jnp.zeros((SUBLANES, lb), F32)
        _, _, acc_r, acc_i = lax.fori_loop(0, n_tiles, step, (zero, zero, zero, zero))
        ar_ref[...] = jnp.sum(acc_r, axis=0, keepdims=True)
        ai_ref[...] = jnp.sum(acc_i, axis=0, keepdims=True)

    re_in = pl.BlockSpec((s, lb), lambda j: (0, j))
    im_in = pl.BlockSpec((s, lb), lambda j: (0, j + nlb))
    pw = pl.BlockSpec((SUBLANES, lb), lambda j: (0, j))
    vec = pl.BlockSpec((1, lb), lambda j: (0, j))
    out = jax.ShapeDtypeStruct((s, lanes), F32)
    small = jax.ShapeDtypeStruct((1, lanes), F32)
    return pl.pallas_call(
        body, name="ssm_scan_bwd", grid=(nlb,), in_specs=[re_in, im_in, re_in, re_in, pw, pw],
        out_specs=(re_in, re_in, vec, vec), out_shape=(out, out, small, small),
        compiler_params=_params(("parallel",)))(gx, gx, xprev_re, xprev_im, pw_re, pw_im)


def _split_dot(x, tri):
    hi = x.astype(BF16)
    lo = (x - hi.astype(F32)).astype(BF16)
    return (jnp.dot(hi, tri, preferred_element_type=F32) + jnp.dot(lo, tri, preferred_element_type=F32))


def _attn_tile(q, kb, t0, s0):
    z = lax.dot_general(q, kb, (((1,), (1,)), ((), ())), preferred_element_type=F32) * (HEAD_DIM ** -0.5)
    t_pos = t0 + lax.broadcasted_iota(jnp.int32, z.shape, 0)
    s_pos = s0 + lax.broadcasted_iota(jnp.int32, z.shape, 1)
    mask = s_pos < t_pos
    l = jnp.log(1.0 + jnp.exp(-jnp.abs(z)))
    log_stay = jnp.where(mask, -(jnp.maximum(z, 0.0) + l), 0.0)
    return z, mask, log_stay, l


def _attn_fwd(qn, kn, v):
    nh, s, d = qn.shape
    t = min(ATT_T, s)

    def body(q_ref, k_ref, v_ref, o_ref, b_ref):
        qi = pl.program_id(1)
        q = q_ref[0]
        r_i = lax.broadcasted_iota(jnp.int32, (t, t), 0)
        c_i = lax.broadcasted_iota(jnp.int32, (t, t), 1)
        tri_after = (r_i > c_i).astype(BF16)

        def step(it, carry):
            o_acc, acc = carry
            jb = qi - it
            ks = pl.ds(pl.multiple_of(jb * t, t), t)
            kb, vb = k_ref[0, ks, :], v_ref[0, ks, :]
            z, mask, log_stay, l = _attn_tile(q, kb, qi * t, jb * t)
            later = acc + _split_dot(log_stay, tri_after)
            w = jnp.where(mask, jnp.exp(jnp.minimum(z, 0.0) - l + later), 0.0)
            o_acc = o_acc + jnp.dot(w.astype(BF16), vb, preferred_element_type=F32)
            return o_acc, acc + jnp.sum(log_stay, axis=1, keepdims=True)

        o_acc, acc = lax.fori_loop(0, qi + 1, step, (jnp.zeros((t, d), F32), jnp.zeros((t, 1), F32)))
        o_ref[0] = o_acc
        b_ref[0] = acc

    q_spec = pl.BlockSpec((1, t, d), lambda h, i: (h, i, 0))
    kv_spec = pl.BlockSpec((1, s, d), lambda h, i: (h, 0, 0))
    return pl.pallas_call(
        body, name="attn_fwd", grid=(nh, s // t), in_specs=[q_spec, kv_spec, kv_spec],
        out_specs=(q_spec, pl.BlockSpec((1, t, 1), lambda h, i: (h, i, 0))),
        out_shape=(jax.ShapeDtypeStruct((nh, s, d), F32), jax.ShapeDtypeStruct((nh, s, 1), F32)),
        compiler_params=_params(("parallel", "parallel")))(qn, kn, v)


def _attn_bwd(qn, kn, v, bsum, do):
    nh, s, d = qn.shape
    t = min(ATT_T, s)

    def body(q_ref, k_ref, v_ref, b_ref, do_ref, dq_ref, dk_ref, dv_ref):
        qi = pl.program_id(1)

        @pl.when(qi == 0)
        def _():
            dk_ref[...] = jnp.zeros_like(dk_ref)
            dv_ref[...] = jnp.zeros_like(dv_ref)

        q = q_ref[0]
        dob = do_ref[0].astype(BF16)
        total = b_ref[0]
        r_i = lax.broadcasted_iota(jnp.int32, (t, t), 0)
        c_i = lax.broadcasted_iota(jnp.int32, (t, t), 1)
        tri_upto = (r_i <= c_i).astype(BF16)
        tri_before = (r_i < c_i).astype(BF16)

        def step(jb, carry):
            dq_acc, stay_prefix, g_prefix = carry
            ks = pl.ds(pl.multiple_of(jb * t, t), t)
            kb, vb = k_ref[0, ks, :], v_ref[0, ks, :]
            z, mask, log_stay, l = _attn_tile(q, kb, qi * t, jb * t)
            later = total - stay_prefix - _split_dot(log_stay, tri_upto)
            log_beta = jnp.minimum(z, 0.0) - l
            w = jnp.where(mask, jnp.exp(log_beta + later), 0.0)
            dw = lax.dot_general(dob, vb, (((1,), (1,)), ((), ())), preferred_element_type=F32)
            g = dw * w
            g_before = g_prefix + _split_dot(g, tri_before)
            beta = jnp.exp(log_beta)
            dz = jnp.where(mask, (g * (1.0 - beta) - beta * g_before) * (HEAD_DIM ** -0.5), 0.0)
            dzb = dz.astype(BF16)
            dq_acc = dq_acc + jnp.dot(dzb, kb, preferred_element_type=F32)
            dk_ref[0, ks, :] += lax.dot_general(dzb, q, (((0,), (0,)), ((), ())), preferred_element_type=F32)
            dv_ref[0, ks, :] += lax.dot_general(w.astype(BF16), dob, (((0,), (0,)), ((), ())),
                                                preferred_element_type=F32)
            return (dq_acc, stay_prefix + jnp.sum(log_stay, axis=1, keepdims=True),
                    g_prefix + jnp.sum(g, axis=1, keepdims=True))

        zero1 = jnp.zeros((t, 1), F32)
        dq_acc, _, _ = lax.fori_loop(0, qi + 1, step, (jnp.zeros((t, d), F32), zero1, zero1))
        dq_ref[0] = dq_acc

    q_spec = pl.BlockSpec((1, t, d), lambda h, i: (h, i, 0))
    kv_spec = pl.BlockSpec((1, s, d), lambda h, i: (h, 0, 0))
    b_spec = pl.BlockSpec((1, t, 1), lambda h, i: (h, i, 0))
    full = jax.ShapeDtypeStruct((nh, s, d), F32)
    return pl.pallas_call(
        body, name="attn_bwd", grid=(nh, s // t), in_specs=[q_spec, kv_spec, kv_spec, b_spec, q_spec],
        out_specs=(q_spec, kv_spec, kv_spec), out_shape=(full, full, full),
        compiler_params=_params(("parallel", "arbitrary")))(qn, kn, v, bsum, do)


def _reduce_adamw(parts, w, m, v, *, name):
    n, rows, cols = parts.shape
    tr = max(t for t in range(SUBLANES, min(rows, 256) + 1, SUBLANES) if rows % t == 0)
    c1 = 1.0 - ADAM_B1 ** ADAM_STEP
    c2 = 1.0 - ADAM_B2 ** ADAM_STEP

    def body(p_ref, w_ref, m_ref, v_ref, g_ref, d_ref, nm_ref, nv_ref):
        g = p_ref[0]
        for i in range(1, n):
            g = g + p_ref[i]
        nm = ADAM_B1 * m_ref[...] + (1.0 - ADAM_B1) * g
        nv = ADAM_B2 * v_ref[...] + (1.0 - ADAM_B2) * (g * g)
        g_ref[...] = g
        nm_ref[...] = nm
        nv_ref[...] = nv
        d_ref[...] = -ADAM_LR * ((nm / c1) / (jnp.sqrt(nv / c2) + ADAM_EPS) + ADAM_WD * w_ref[...])

    row = pl.BlockSpec((tr, cols), lambda i: (i, 0))
    out = jax.ShapeDtypeStruct((rows, cols), F32)
    return pl.pallas_call(
        body, name=name, grid=(rows // tr,),
        in_specs=[pl.BlockSpec((n, tr, cols), lambda i: (0, i, 0)), row, row, row],
        out_specs=(row,) * 4, out_shape=(out,) * 4, compiler_params=_params(("parallel",)))(parts, w, m, v)


def _all_gather(x_shard, *, name):
    m_per, n = x_shard.shape

    def body(x_ref, out_ref, send_sems, recv_sems, local_sem):
        x, y, c = lax.axis_index("x"), lax.axis_index("y"), lax.axis_index("c")
        me, sibling = (x, y, c), (x, y, 1 - c)
        chips = [(1 - x, y), (x, 1 - y), (1 - x, 1 - y)]

        def rows(px, py, pc):
            return out_ref.at[pl.ds((4 * px + 2 * py + pc) * m_per, m_per), :]

        def copy(k, block, to, src=None):
            return pltpu.make_async_remote_copy(
                src_ref=rows(*block) if src is None else src, dst_ref=rows(*block),
                send_sem=send_sems.at[k], recv_sem=recv_sems.at[k], device_id=to, device_id_type=MESH_IDS)

        mine = pltpu.make_async_copy(x_ref, rows(*me), local_sem)
        mine.start()
        first = [copy(0, me, sibling, src=x_ref)]
        first += [copy(1 + j, me, (*chip, c), src=x_ref) for j, chip in enumerate(chips)]
        for cp in first:
            cp.start()
        passed = [copy(4 + j, (*chip, c), sibling) for j, chip in enumerate(chips)]
        for j, chip in enumerate(chips):
            copy(1 + j, (*chip, c), me).wait_recv()
            passed[j].start()
        copy(0, sibling, me).wait_recv()
        for j, chip in enumerate(chips):
            copy(4 + j, (*chip, 1 - c), me).wait_recv()
        for cp in first + passed:
            cp.wait_send()
        mine.wait()

    return pl.pallas_call(
        body, name=name, out_shape=jax.ShapeDtypeStruct((N_DEV * m_per, n), x_shard.dtype),
        in_specs=[pl.BlockSpec(memory_space=pl.ANY)], out_specs=pl.BlockSpec(memory_space=pl.ANY),
        scratch_shapes=[pltpu.SemaphoreType.DMA((7,)), pltpu.SemaphoreType.DMA((7,)), pltpu.SemaphoreType.DMA],
    )(x_shard)


def _all_to_all(x, *, name):
    def body(x_ref, out_ref, send_sems, recv_sems, local_sem):
        x, y, c = lax.axis_index("x"), lax.axis_index("y"), lax.axis_index("c")
        me = 4 * x + 2 * y + c
        mine = pltpu.make_async_copy(x_ref.at[me], out_ref.at[me], local_sem)
        mine.start()
        copies = []
        for k in range(1, N_DEV):
            px, py, pc = x ^ (k >> 2), y ^ ((k >> 1) & 1), c ^ (k & 1)
            peer = 4 * px + 2 * py + pc
            copies.append(pltpu.make_async_remote_copy(
                src_ref=x_ref.at[peer], dst_ref=out_ref.at[me], send_sem=send_sems.at[k - 1],
                recv_sem=recv_sems.at[k - 1], device_id=(px, py, pc), device_id_type=MESH_IDS))
        for cp in copies:
            cp.start()
        for k in range(1, N_DEV):
            px, py, pc = x ^ (k >> 2), y ^ ((k >> 1) & 1), c ^ (k & 1)
            peer = 4 * px + 2 * py + pc
            pltpu.make_async_remote_copy(
                src_ref=x_ref.at[peer], dst_ref=out_ref.at[peer], send_sem=send_sems.at[k - 1],
                recv_sem=recv_sems.at[k - 1], device_id=(px, py, pc), device_id_type=MESH_IDS).wait_recv()
        for cp in copies:
            cp.wait_send()
        mine.wait()

    return pl.pallas_call(
        body, name=name, out_shape=jax.ShapeDtypeStruct(x.shape, x.dtype),
        in_specs=[pl.BlockSpec(memory_space=pl.ANY)], out_specs=pl.BlockSpec(memory_space=pl.ANY),
        scratch_shapes=[pltpu.SemaphoreType.DMA((7,)), pltpu.SemaphoreType.DMA((7,)), pltpu.SemaphoreType.DMA],
    )(x)


def _chunks_first(full, axis):
    if axis == 0:
        return full.reshape(N_DEV, -1)
    r, c = full.shape
    return full.reshape(r, N_DEV, c // N_DEV).transpose(1, 0, 2).reshape(N_DEV, -1)


def _from_chunks(chunks, name, shard_shape):
    r, c = shard_shape
    if BIG_SPLIT_AXIS[name] == 0:
        return chunks.reshape(N_DEV * r, c)
    return chunks.reshape(N_DEV, r, c).transpose(1, 0, 2).reshape(r, N_DEV * c)


def _pack_big_shards(shards):
    parts = [shards[n][l].reshape(-1) for l in range(DEPTH) for n in BIG]
    return jnp.concatenate(parts).reshape(-1, PACK_COLS)


def _unpack_big_shards(flat, shapes):
    flat = flat.reshape(-1)
    out = {n: [] for n in BIG}
    off = 0
    for _ in range(DEPTH):
        for n in BIG:
            r, c = shapes[n]
            out[n].append(flat[off:off + r * c].reshape(r, c))
            off += r * c
    return {n: jnp.stack(v) for n, v in out.items()}


def _pack_small(vals):
    flat = jnp.concatenate([vals[n].reshape(-1) for n in SMALL])
    rows = -(-flat.shape[0] // (SUBLANES * PACK_COLS)) * SUBLANES
    return jnp.pad(flat, (0, rows * PACK_COLS - flat.shape[0])).reshape(rows, PACK_COLS)


def _unpack_small(flat, shapes):
    flat = flat.reshape(-1)
    out, off = {}, 0
    for n in SMALL:
        size = math.prod(shapes[n])
        out[n] = flat[off:off + size].reshape(shapes[n])
        off += size
    return out


def _heads(t2d):
    s = t2d.shape[0]
    return t2d.reshape(s, ATTN_HEADS, HEAD_DIM).transpose(1, 0, 2)


def _unheads(t3d):
    return t3d.transpose(1, 0, 2).reshape(t3d.shape[1], ATTN_WIDTH)


def _ssm_setup(sp):
    g, p = SSM_GROUPS, SSM_STATE
    a_re = sp["ssm_a_re"][:, None, :]
    a_im = sp["ssm_a_im"][:, None, :]
    log_dt = jnp.broadcast_to(sp["ssm_log_dt"][:, None, None], (g, 1, p))
    bt_re = sp["ssm_b_re"].transpose(0, 2, 1)
    bt_im = sp["ssm_b_im"].transpose(0, 2, 1)
    return a_re, a_im, log_dt, bt_re, bt_im


def _layer_fwd(h, p_l, w, sp):
    s = h.shape[0]
    hn = _rms_fwd(h, sp["mix_norm_g"][None], name="rms_mix")
    proj = _mm(hn, w["w_in"], name="mm_proj")
    u = proj[:, :SSM_WIDTH]

    pw_re, pw_im, bbt_re, bbt_im = _ssm_param_fwd(*_ssm_setup(sp))
    pw_re = pw_re.transpose(1, 0, 2).reshape(SUBLANES, SSM_LANES)
    pw_im = pw_im.transpose(1, 0, 2).reshape(SUBLANES, SSM_LANES)
    bw_re, bw_im = _block_diag(bbt_re).astype(BF16), _block_diag(bbt_im).astype(BF16)
    cw_re, cw_im = _block_diag(sp["ssm_c_re"]).astype(BF16), _block_diag(-sp["ssm_c_im"]).astype(BF16)
    bu = _mm(u, jnp.concatenate([bw_re, bw_im], axis=1), name="mm_bu")
    x_re, x_im, xp_re, xp_im = _ssm_scan_fwd(bu, pw_re, pw_im)
    ycx = _mm(x_im, cw_im, tb=True, add=_mm(x_re, cw_re, tb=True, name="mm_cx_re"), name="mm_cx_im")
    dvec = sp["ssm_d"].reshape(1, SSM_WIDTH)
    y, z = _ssm_mid_fwd(ycx, proj, dvec)
    zz = _mm(z, w["ssm_w_glu"], name="mm_glu")

    q3 = _heads(proj[:, 2 * SSM_WIDTH:2 * SSM_WIDTH + ATTN_WIDTH]).reshape(ATTN_HEADS * s, HEAD_DIM)
    k3 = _heads(proj[:, 2 * SSM_WIDTH + ATTN_WIDTH:2 * SSM_WIDTH + 2 * ATTN_WIDTH]).reshape(ATTN_HEADS * s, HEAD_DIM)
    v3 = _heads(proj[:, 2 * SSM_WIDTH + 2 * ATTN_WIDTH:2 * SSM_WIDTH + 3 * ATTN_WIDTH]).astype(BF16)
    qn = _rms_fwd(q3, sp["q_norm_g"][None], name="rms_q").reshape(ATTN_HEADS, s, HEAD_DIM)
    kn = _rms_fwd(k3, sp["k_norm_g"][None], name="rms_k").reshape(ATTN_HEADS, s, HEAD_DIM)
    o3, bsum = _attn_fwd(qn, kn, v3)
    o = _unheads(o3)

    bglu = sp["ssm_b_glu"][None]
    ycat = _gate_fwd(zz, bglu, proj, o)
    h2 = _mm(ycat, w["w_out"], add=h, name="mm_out")
    hn2 = _rms_fwd(h2, sp["ple_norm_g"][None], name="rms_ple")
    pgl = _mm(hn2, w["w_ple_gate"], name="mm_ple_gate")
    pp = _mm(p_l, w["w_ple_proj"], name="mm_ple_proj")
    h3 = _ple_out(h2, pgl, pp)
    saved = dict(h=h, hn=hn, proj=proj, u=u, pw_re=pw_re, pw_im=pw_im, bw_re=bw_re, bw_im=bw_im,
                 cw_re=cw_re, cw_im=cw_im, x_re=x_re, x_im=x_im, xp_re=xp_re, xp_im=xp_im, y=y, z=z,
                 zz=zz, q3=q3, k3=k3, v3=v3, qn=qn, kn=kn, bsum=bsum, o=o, ycat=ycat, h2=h2, hn2=hn2,
                 pgl=pgl, pp=pp, p_l=p_l, dvec=dvec, bglu=bglu)
    return h3, saved


def _layer_bwd(dh3, sv, w, sp):
    s = dh3.shape[0]
    g, hh, p = SSM_GROUPS, SSM_GROUP, SSM_STATE
    grads = {}
    dgp, dpp = _ple_bwd(dh3, sv["pgl"], sv["pp"])
    grads["w_ple_gate"] = _mm(sv["hn2"], dgp, ta=True, name="mm_d_ple_gate")
    grads["w_ple_proj"] = _mm(sv["p_l"], dpp, ta=True, name="mm_d_ple_proj")
    dhn2 = _mm(dgp, w["w_ple_gate"], tb=True, name="mm_dhn2")
    dh2, dg2 = _rms_bwd(sv["h2"], sp["ple_norm_g"][None], dhn2, dres=dh3, name="rms_ple_bwd")
    grads["ple_norm_g"] = dg2[0]
    grads["w_out"] = _mm(sv["ycat"], dh2, ta=True, name="mm_d_out")
    dyc = _mm(dh2, w["w_out"], tb=True, name="mm_dycat")
    dzz, dgs, do, dga, dbglu = _gate_bwd(dyc, sv["zz"], sv["bglu"], sv["proj"], sv["o"])
    grads["ssm_b_glu"] = dbglu[0]

    dqn, dkn, dv3 = _attn_bwd(sv["qn"], sv["kn"], sv["v3"], sv["bsum"], _heads(do))
    dq3, dgq = _rms_bwd(sv["q3"], sp["q_norm_g"][None], dqn.reshape(ATTN_HEADS * s, HEAD_DIM), name="rms_q_bwd")
    dk3, dgk = _rms_bwd(sv["k3"], sp["k_norm_g"][None], dkn.reshape(ATTN_HEADS * s, HEAD_DIM), name="rms_k_bwd")
    grads["q_norm_g"], grads["k_norm_g"] = dgq[0], dgk[0]
    dq = _unheads(dq3.reshape(ATTN_HEADS, s, HEAD_DIM))
    dk = _unheads(dk3.reshape(ATTN_HEADS, s, HEAD_DIM))
    dv = _unheads(dv3)

    grads["ssm_w_glu"] = _mm(sv["z"], dzz, ta=True, name="mm_d_glu")
    dz = _mm(dzz, w["ssm_w_glu"], tb=True, name="mm_dz")
    dy, du_skip, dd = _ssm_mid_bwd(dz, sv["y"], sv["proj"], sv["dvec"])
    grads["ssm_d"] = dd.reshape(g, hh)
    gx = _mm(dy, jnp.concatenate([sv["cw_re"], sv["cw_im"]], axis=1), name="mm_gx")
    lam_re, lam_im, gab_re, gab_im = _ssm_scan_bwd(gx, sv["xp_re"], sv["xp_im"], sv["pw_re"], sv["pw_im"])
    du = _mm(lam_im, sv["bw_im"], tb=True, name="mm_du_im",
             add=_mm(lam_re, sv["bw_re"], tb=True, add=du_skip, name="mm_du_re"))
    gbb_re = _block_diag_take(_mm(sv["u"], lam_re, ta=True, name="mm_d_bw_re"), g, hh, p)
    gbb_im = _block_diag_take(_mm(sv["u"], lam_im, ta=True, name="mm_d_bw_im"), g, hh, p)
    grads["ssm_c_re"] = _block_diag_take(_mm(dy, sv["x_re"], ta=True, name="mm_d_cw_re"), g, hh, p)
    grads["ssm_c_im"] = -_block_diag_take(_mm(dy, sv["x_im"], ta=True, name="mm_d_cw_im"), g, hh, p)
    da_re, da_im, dldt, dbt_re, dbt_im = _ssm_param_bwd(
        *_ssm_setup(sp), gab_re.reshape(g, 1, p), gab_im.reshape(g, 1, p), gbb_re, gbb_im)
    grads["ssm_a_re"], grads["ssm_a_im"] = da_re[:, 0, :], da_im[:, 0, :]
    grads["ssm_log_dt"] = dldt[:, 0, 0]
    grads["ssm_b_re"], grads["ssm_b_im"] = dbt_re.transpose(0, 2, 1), dbt_im.transpose(0, 2, 1)

    dproj = jnp.concatenate([du, dgs, dq, dk, dv, dga], axis=1)
    grads["w_in"] = _mm(sv["hn"], dproj, ta=True, name="mm_d_in")
    dhn = _mm(dproj, w["w_in"], tb=True, name="mm_dhn")
    dh, dg1 = _rms_bwd(sv["h"], sp["mix_norm_g"][None], dhn, dres=dh2, name="rms_mix_bwd")
    grads["mix_norm_g"] = dg1[0]
    return dh, grads


def _local_step(x, p, target, big, small):
    h = x
    saved = []
    for l in range(DEPTH):
        sp = {n: small[n][l] for n in SMALL}
        h, sv = _layer_fwd(h, p[l], big[l], sp)
        saved.append(sv)
    dh, loss_parts = _loss_head(h, target)
    grads = [None] * DEPTH
    for l in reversed(range(DEPTH)):
        sp = {n: small[n][l] for n in SMALL}
        dh, grads[l] = _layer_bwd(dh, saved[l], big[l], sp)
    return jnp.sum(loss_parts), dh, grads


def kernel(x, p, mix_norm_g, w_in, ssm_a_re, ssm_a_im, ssm_log_dt, ssm_b_re, ssm_b_im, ssm_c_re, ssm_c_im, ssm_d, ssm_w_glu, ssm_b_glu, q_norm_g, k_norm_g, w_out, ple_norm_g, w_ple_gate, w_ple_proj, loss_target, m_mix_norm_g, m_w_in, m_ssm_a_re, m_ssm_a_im, m_ssm_log_dt, m_ssm_b_re, m_ssm_b_im, m_ssm_c_re, m_ssm_c_im, m_ssm_d, m_ssm_w_glu, m_ssm_b_glu, m_q_norm_g, m_k_norm_g, m_w_out, m_ple_norm_g, m_w_ple_gate, m_w_ple_proj, v_mix_norm_g, v_w_in, v_ssm_a_re, v_ssm_a_im, v_ssm_log_dt, v_ssm_b_re, v_ssm_b_im, v_ssm_c_re, v_ssm_c_im, v_ssm_d, v_ssm_w_glu, v_ssm_b_glu, v_q_norm_g, v_k_norm_g, v_w_out, v_ple_norm_g, v_w_ple_gate, v_w_ple_proj):
    given = dict(locals())
    wts = {n: given[n] for n in WEIGHTS}
    mom = {n: given["m_" + n] for n in WEIGHTS}
    var = {n: given["v_" + n] for n in WEIGHTS}
    shard_shapes = {n: wts[n].shape[1:] for n in BIG}

    packed_w = _pack_big_shards(wts)
    gathered = _all_gather(packed_w.astype(BF16), name="gather_weights")
    gathered = gathered.reshape(N_DEV, -1)
    big, off = [], 0
    for _ in range(DEPTH):
        layer = {}
        for n in BIG:
            size = math.prod(shard_shapes[n])
            layer[n] = _from_chunks(gathered[:, off:off + size], n, shard_shapes[n])
            off += size
        big.append(layer)

    small = {n: wts[n] for n in SMALL}
    loss, dx, grads = _local_step(x[0], p[:, 0], loss_target[0], big, small)
    loss = lax.psum(loss, ("x", "y", "c"))

    send = jnp.concatenate([_chunks_first(grads[l][n], BIG_SPLIT_AXIS[n]) for l in range(DEPTH) for n in BIG], axis=1)
    rows = send.shape[1] // PACK_COLS
    parts = _all_to_all(send.reshape(N_DEV, rows, PACK_COLS), name="exchange_weight_grads")
    big_out = _reduce_adamw(parts, packed_w, _pack_big_shards(mom), _pack_big_shards(var), name="adamw_sharded")
    big_out = [_unpack_big_shards(a, shard_shapes) for a in big_out]

    small_shapes = {n: wts[n].shape for n in SMALL}
    local_small = _pack_small({n: jnp.stack([grads[l][n] for l in range(DEPTH)]) for n in SMALL})
    all_small = _all_gather(local_small, name="gather_small_grads")
    all_small = all_small.reshape(N_DEV, local_small.shape[0], PACK_COLS)
    small_out = _reduce_adamw(all_small, _pack_small(small), _pack_small({n: mom[n] for n in SMALL}),
                              _pack_small({n: var[n] for n in SMALL}), name="adamw_replicated")
    small_out = [_unpack_small(a, small_shapes) for a in small_out]

    outs = [loss, dx[None]]
    for k in range(4):
        outs += [big_out[k][n] if n in BIG else small_out[k][n] for n in WEIGHTS]
    return tuple(outs)
```

```python
import math

import jax
import jax.numpy as jnp
from jax import lax
from jax.experimental import pallas as pl
from jax.experimental.pallas import tpu as pltpu

F32 = jnp.float32
BF16 = jnp.bfloat16
MESH_IDS = pl.DeviceIdType.MESH

N_DEV = 8
D_MODEL = 1024
DEPTH = 2
PLE_DIM = 256
SSM_WIDTH = 512
SSM_GROUP = 16
SSM_GROUPS = 32
SSM_STATE = 64
SSM_LANES = SSM_GROUPS * SSM_STATE
ATTN_WIDTH = 512
ATTN_HEADS = 8
HEAD_DIM = 64
RMS_EPS = 1e-6
ADAM_LR = 0.001
ADAM_B1 = 0.9
ADAM_B2 = 0.999
ADAM_EPS = 1e-08
ADAM_WD = 0.01
ADAM_STEP = 10

VMEM_LIMIT = 56 * 1024 * 1024
ATT_TQ = 512
ATT_TK = 128
SCAN_LANES = 128
SUBLANES = 8
PACK_COLS = 1024

BIG = ("w_in", "ssm_w_glu", "w_out", "w_ple_gate", "w_ple_proj")
SMALL = ("mix_norm_g", "ssm_a_re", "ssm_a_im", "ssm_log_dt", "ssm_b_re", "ssm_b_im", "ssm_c_re",
         "ssm_c_im", "ssm_d", "ssm_b_glu", "q_norm_g", "k_norm_g", "ple_norm_g")
WEIGHTS = ("mix_norm_g", "w_in", "ssm_a_re", "ssm_a_im", "ssm_log_dt", "ssm_b_re", "ssm_b_im",
           "ssm_c_re", "ssm_c_im", "ssm_d", "ssm_w_glu", "ssm_b_glu", "q_norm_g", "k_norm_g",
           "w_out", "ple_norm_g", "w_ple_gate", "w_ple_proj")
BIG_SPLIT_AXIS = {"w_in": 1, "ssm_w_glu": 1, "w_out": 0, "w_ple_gate": 0, "w_ple_proj": 1}


def _params(sem=None):
    return pltpu.CompilerParams(dimension_semantics=sem, vmem_limit_bytes=VMEM_LIMIT)


def _sigmoid(x):
    return 1.0 / (1.0 + jnp.exp(-x))


_GELU_K = math.sqrt(2.0 / math.pi)
_GELU_C = 0.044715


def _gelu(x):
    return 0.5 * x * (1.0 + jnp.tanh(_GELU_K * (x + _GELU_C * x * x * x)))


def _gelu_grad(x):
    th = jnp.tanh(_GELU_K * (x + _GELU_C * x * x * x))
    return 0.5 * (1.0 + th) + 0.5 * x * (1.0 - th * th) * _GELU_K * (1.0 + 3.0 * _GELU_C * x * x)


def _mm(a, b, *, name, ta=False, tb=False, add=None, out_dtype=F32, tm=1024, tn=1024, tk=512):
    m, k = (a.shape[1], a.shape[0]) if ta else a.shape
    n = b.shape[0] if tb else b.shape[1]
    tm, tn, tk = min(tm, m), min(tn, n), min(tk, k)
    assert m % tm == 0 and n % tn == 0 and k % tk == 0, (name, a.shape, b.shape)
    nk = k // tk
    dims = (((0 if ta else 1,), (1 if tb else 0,)), ((), ()))

    def body(*refs):
        if add is None:
            a_ref, b_ref, o_ref, acc_ref = refs
        else:
            a_ref, b_ref, r_ref, o_ref, acc_ref = refs
        kk = pl.program_id(2)

        @pl.when(kk == 0)
        def _():
            acc_ref[...] = jnp.zeros_like(acc_ref)

        acc_ref[...] += lax.dot_general(a_ref[...].astype(BF16), b_ref[...].astype(BF16), dims,
                                        preferred_element_type=F32)

        @pl.when(kk == nk - 1)
        def _():
            res = acc_ref[...]
            if add is not None:
                res = res + r_ref[...].astype(F32)
            o_ref[...] = res.astype(out_dtype)

    a_spec = (pl.BlockSpec((tk, tm), lambda i, j, kk: (kk, i)) if ta
              else pl.BlockSpec((tm, tk), lambda i, j, kk: (i, kk)))
    b_spec = (pl.BlockSpec((tn, tk), lambda i, j, kk: (j, kk)) if tb
              else pl.BlockSpec((tk, tn), lambda i, j, kk: (kk, j)))
    o_spec = pl.BlockSpec((tm, tn), lambda i, j, kk: (i, j))
    ins, specs = [a, b], [a_spec, b_spec]
    if add is not None:
        ins.append(add)
        specs.append(o_spec)
    return pl.pallas_call(
        body, name=name, grid=(m // tm, n // tn, nk), in_specs=specs, out_specs=o_spec,
        out_shape=jax.ShapeDtypeStruct((m, n), out_dtype),
        scratch_shapes=[pltpu.VMEM((tm, tn), F32)],
        compiler_params=_params(("parallel", "parallel", "arbitrary")),
    )(*ins)


def _rows_tile(rows, want=512):
    t = min(rows, want)
    assert rows % t == 0
    return t


def _rowwise(body, name, rows, ins, outs):
    tm = _rows_tile(rows)
    in_specs = []
    for _, cols, cb in ins:
        if cb is None:
            in_specs.append(pl.BlockSpec((1, cols), lambda i: (0, 0)))
        else:
            in_specs.append(pl.BlockSpec((tm, cols), lambda i, cb=cb: (i, cb)))
    out_specs, out_shapes = [], []
    for cols, dtype, is_acc in outs:
        if is_acc:
            out_specs.append(pl.BlockSpec((1, cols), lambda i: (0, 0)))
            out_shapes.append(jax.ShapeDtypeStruct((1, cols), dtype))
        else:
            out_specs.append(pl.BlockSpec((tm, cols), lambda i: (i, 0)))
            out_shapes.append(jax.ShapeDtypeStruct((rows, cols), dtype))
    any_acc = any(o[2] for o in outs)
    return pl.pallas_call(
        body, name=name, grid=(rows // tm,), in_specs=in_specs, out_specs=tuple(out_specs),
        out_shape=tuple(out_shapes),
        compiler_params=_params(("arbitrary",) if any_acc else ("parallel",)))(*[a for a, _, _ in ins])


def _accumulate(ref, value):
    @pl.when(pl.program_id(0) == 0)
    def _():
        ref[...] = jnp.zeros_like(ref)

    ref[...] += value


def _rms_fwd(h, g, *, name):
    rows, d = h.shape

    def body(h_ref, g_ref, o_ref):
        hv = h_ref[...]
        r = lax.rsqrt(jnp.mean(hv * hv, axis=-1, keepdims=True) + RMS_EPS)
        o_ref[...] = (hv * r * g_ref[...]).astype(BF16)

    return _rowwise(body, name, rows, [(h, d, 0), (g, d, None)], [(d, BF16, False)])[0]


def _rms_bwd(h, g, dhn, *, name, dres=None):
    rows, d = h.shape

    def body(*refs):
        if dres is None:
            h_ref, g_ref, dn_ref, dh_ref, dg_ref = refs
        else:
            h_ref, g_ref, dn_ref, dr_ref, dh_ref, dg_ref = refs
        hv = h_ref[...]
        dn = dn_ref[...].astype(F32)
        r = lax.rsqrt(jnp.mean(hv * hv, axis=-1, keepdims=True) + RMS_EPS)
        a = dn * g_ref[...]
        dot = jnp.mean(a * hv, axis=-1, keepdims=True)
        dh = r * a - hv * (r * r * r * dot)
        if dres is not None:
            dh = dh + dr_ref[...]
        dh_ref[...] = dh
        _accumulate(dg_ref, jnp.sum(dn * hv * r, axis=0, keepdims=True))

    ins = [(h, d, 0), (g, d, None), (dhn, d, 0)]
    if dres is not None:
        ins.append((dres, d, 0))
    return _rowwise(body, name, rows, ins, [(d, F32, False), (d, F32, True)])


def _ssm_mid_fwd(ycx, proj, dvec):
    rows = ycx.shape[0]

    def body(yc_ref, u_ref, d_ref, y_ref, z_ref):
        y = yc_ref[...] + d_ref[...] * u_ref[...]
        y_ref[...] = y
        z_ref[...] = _gelu(y).astype(BF16)

    return _rowwise(body, "ssm_mid_fwd", rows,
                    [(ycx, SSM_WIDTH, 0), (proj, SSM_WIDTH, 0), (dvec, SSM_WIDTH, None)],
                    [(SSM_WIDTH, F32, False), (SSM_WIDTH, BF16, False)])


def _ssm_mid_bwd(dz, y, proj, dvec):
    rows = dz.shape[0]

    def body(dz_ref, y_ref, u_ref, d_ref, dy_ref, du_ref, dd_ref):
        dy = dz_ref[...] * _gelu_grad(y_ref[...])
        dy_ref[...] = dy
        du_ref[...] = dy * d_ref[...]
        _accumulate(dd_ref, jnp.sum(dy * u_ref[...], axis=0, keepdims=True))

    return _rowwise(body, "ssm_mid_bwd", rows,
                    [(dz, SSM_WIDTH, 0), (y, SSM_WIDTH, 0), (proj, SSM_WIDTH, 0), (dvec, SSM_WIDTH, None)],
                    [(SSM_WIDTH, F32, False), (SSM_WIDTH, F32, False), (SSM_WIDTH, F32, True)])


def _gate_fwd(zz, bglu, proj, o):
    rows = zz.shape[0]
    w = SSM_WIDTH

    def body(zz_ref, b_ref, gs_ref, o_ref, ga_ref, y_ref):
        zz_v = zz_ref[...] + b_ref[...]
        val, gate = zz_v[:, :w], zz_v[:, w:]
        gs, ga = gs_ref[...], ga_ref[...]
        y_ref[:, :w] = (val * _sigmoid(gate) * (gs * _sigmoid(gs))).astype(BF16)
        y_ref[:, w:] = (o_ref[...] * (ga * _sigmoid(ga))).astype(BF16)

    return _rowwise(body, "gate_fwd", rows,
                    [(zz, 2 * w, 0), (bglu, 2 * w, None), (proj, w, 1), (o, w, 0), (proj, w, 5)],
                    [(2 * w, BF16, False)])[0]


def _gate_bwd(dyc, zz, bglu, proj, o):
    rows = zz.shape[0]
    w = SSM_WIDTH

    def body(dy_ref, zz_ref, b_ref, gs_ref, o_ref, ga_ref, dzz_ref, dgs_ref, do_ref, dga_ref, db_ref):
        zz_v = zz_ref[...] + b_ref[...]
        val, gate = zz_v[:, :w], zz_v[:, w:]
        gs, ga = gs_ref[...], ga_ref[...]
        dys, dya = dy_ref[:, :w], dy_ref[:, w:]
        sg, ss, sa = _sigmoid(gate), _sigmoid(gs), _sigmoid(ga)
        glu = val * sg
        dglu = dys * (gs * ss)
        dgs_ref[...] = dys * glu * (ss * (1.0 + gs * (1.0 - ss)))
        dval = dglu * sg
        dgate = dglu * val * sg * (1.0 - sg)
        dzz_ref[:, :w] = dval.astype(BF16)
        dzz_ref[:, w:] = dgate.astype(BF16)
        do_ref[...] = dya * (ga * sa)
        dga_ref[...] = dya * o_ref[...] * (sa * (1.0 + ga * (1.0 - sa)))
        _accumulate(db_ref, jnp.concatenate([jnp.sum(dval, axis=0, keepdims=True),
                                             jnp.sum(dgate, axis=0, keepdims=True)], axis=1))

    return _rowwise(body, "gate_bwd", rows,
                    [(dyc, 2 * w, 0), (zz, 2 * w, 0), (bglu, 2 * w, None), (proj, w, 1), (o, w, 0), (proj, w, 5)],
                    [(2 * w, BF16, False), (w, F32, False), (w, F32, False), (w, F32, False), (2 * w, F32, True)])


def _ple_out(h2, pgl, pp):
    rows, d = h2.shape

    def body(h_ref, g_ref, p_ref, o_ref):
        o_ref[...] = h_ref[...] + _sigmoid(g_ref[...]) * p_ref[...]

    return _rowwise(body, "ple_out", rows, [(h2, d, 0), (pgl, d, 0), (pp, d, 0)], [(d, F32, False)])[0]


def _ple_bwd(dh3, pgl, pp):
    rows, d = dh3.shape

    def body(dh_ref, g_ref, p_ref, dg_ref, dp_ref):
        sg = _sigmoid(g_ref[...])
        dh = dh_ref[...]
        dg_ref[...] = (dh * p_ref[...] * sg * (1.0 - sg)).astype(BF16)
        dp_ref[...] = (dh * sg).astype(BF16)

    return _rowwise(body, "ple_bwd", rows, [(dh3, d, 0), (pgl, d, 0), (pp, d, 0)],
                    [(d, BF16, False), (d, BF16, False)])


def _loss_head(h, target):
    rows, d = h.shape

    def body(h_ref, t_ref, dh_ref, l_ref):
        err = h_ref[...] - t_ref[...]
        dh_ref[...] = err * (1.0 / d)
        _accumulate(l_ref, jnp.sum(err * err, axis=0, keepdims=True) * (0.5 / d))

    return _rowwise(body, "loss_head", rows, [(h, d, 0), (target, d, 0)], [(d, F32, False), (d, F32, True)])


def _zoh(lr, li, dt):
    mag = jnp.exp(lr * dt)
    ab_re = mag * jnp.cos(li * dt)
    ab_im = mag * jnp.sin(li * dt)
    num_re = ab_re - 1.0
    den = lr * lr + li * li
    f_re = (num_re * lr + ab_im * li) / den
    f_im = (ab_im * lr - num_re * li) / den
    return ab_re, ab_im, f_re, f_im, den


def _ssm_param_fwd(a_re, a_im, log_dt, bt_re, bt_im):
    g, _, p = a_re.shape
    h = bt_re.shape[1]

    def body(lr_ref, li_ref, ldt_ref, br_ref, bi_ref, pr_ref, pi_ref, bbr_ref, bbi_ref):
        dt = jnp.exp(ldt_ref[...])
        ab_re, ab_im, f_re, f_im, _ = _zoh(lr_ref[...], li_ref[...], dt)
        cr, ci = ab_re, ab_im
        for k in range(SUBLANES):
            pr_ref[:, k:k + 1, :] = cr
            pi_ref[:, k:k + 1, :] = ci
            cr, ci = cr * ab_re - ci * ab_im, cr * ab_im + ci * ab_re
        br, bi = br_ref[...], bi_ref[...]
        bbr_ref[...] = f_re * br - f_im * bi
        bbi_ref[...] = f_re * bi + f_im * br

    pw = jax.ShapeDtypeStruct((g, SUBLANES, p), F32)
    large = jax.ShapeDtypeStruct((g, h, p), F32)
    return pl.pallas_call(body, name="ssm_param_fwd", out_shape=(pw, pw, large, large),
                          compiler_params=_params())(a_re, a_im, log_dt, bt_re, bt_im)


def _ssm_param_bwd(a_re, a_im, log_dt, bt_re, bt_im, gab_re, gab_im, gbb_re, gbb_im):
    g, _, p = a_re.shape
    h = bt_re.shape[1]

    def body(lr_ref, li_ref, ldt_ref, br_ref, bi_ref, gar_ref, gai_ref, gbr_ref, gbi_ref,
             dlr_ref, dli_ref, dldt_ref, dbr_ref, dbi_ref):
        lr, li = lr_ref[...], li_ref[...]
        dt = jnp.exp(ldt_ref[...])
        ab_re, ab_im, f_re, f_im, den = _zoh(lr, li, dt)
        br, bi = br_ref[...], bi_ref[...]
        gbr, gbi = gbr_ref[...], gbi_ref[...]
        dbr_ref[...] = f_re * gbr + f_im * gbi
        dbi_ref[...] = f_re * gbi - f_im * gbr
        gf_re = jnp.sum(br * gbr + bi * gbi, axis=1, keepdims=True)
        gf_im = jnp.sum(br * gbi - bi * gbr, axis=1, keepdims=True)
        il_re, il_im = lr / den, -li / den
        ga_re = gar_ref[...] + il_re * gf_re + il_im * gf_im
        ga_im = gai_ref[...] + il_re * gf_im - il_im * gf_re
        q_re = f_re * il_re - f_im * il_im
        q_im = f_re * il_im + f_im * il_re
        gl_re = -(q_re * gf_re + q_im * gf_im)
        gl_im = -(q_re * gf_im - q_im * gf_re)
        dlr_ref[...] = gl_re + dt * (ab_re * ga_re + ab_im * ga_im)
        dli_ref[...] = gl_im + dt * (ab_re * ga_im - ab_im * ga_re)
        la_re = lr * ab_re - li * ab_im
        la_im = lr * ab_im + li * ab_re
        dldt_ref[...] = jnp.sum((la_re * ga_re + la_im * ga_im) * dt, axis=2, keepdims=True)

    small = jax.ShapeDtypeStruct((g, 1, p), F32)
    one = jax.ShapeDtypeStruct((g, 1, 1), F32)
    large = jax.ShapeDtypeStruct((g, h, p), F32)
    return pl.pallas_call(body, name="ssm_param_bwd", out_shape=(small, small, one, large, large),
                          compiler_params=_params())(
                              a_re, a_im, log_dt, bt_re, bt_im, gab_re, gab_im, gbb_re, gbb_im)


def _block_diag(m):
    g, h, p = m.shape
    eye = jnp.eye(g, dtype=m.dtype)
    return (m[:, :, None, :] * eye[:, None, :, None]).reshape(g * h, g * p)


def _block_diag_take(m, g, h, p):
    eye = jnp.eye(g, dtype=m.dtype)
    return jnp.sum(m.reshape(g, h, g, p) * eye[:, None, :, None], axis=2)


def _scan_tile(br, bi, pw_re, pw_im, reverse):
    row = lax.broadcasted_iota(jnp.int32, br.shape, 0)
    xr, xi = br, bi
    for d in (1, 2, 4):
        ar, ai = pw_re[d - 1:d, :], pw_im[d - 1:d, :]
        if reverse:
            keep = row < SUBLANES - d
            sr = jnp.where(keep, pltpu.roll(xr, SUBLANES - d, 0), 0.0)
            si = jnp.where(keep, pltpu.roll(xi, SUBLANES - d, 0), 0.0)
        else:
            keep = row >= d
            sr = jnp.where(keep, pltpu.roll(xr, d, 0), 0.0)
            si = jnp.where(keep, pltpu.roll(xi, d, 0), 0.0)
        xr, xi = xr + ar * sr - ai * si, xi + ar * si + ai * sr
    return xr, xi


def _ssm_scan_fwd(bu, pw_re, pw_im):
    s, two_l = bu.shape
    lanes = two_l // 2
    lb = min(SCAN_LANES, lanes)
    nlb = lanes // lb
    n_tiles = s // SUBLANES

    def body(br_ref, bi_ref, pr_ref, pi_ref, xr_ref, xi_ref, qr_ref, qi_ref):
        pw_r, pw_i = pr_ref[...], pi_ref[...]
        row = lax.broadcasted_iota(jnp.int32, (SUBLANES, lb), 0)

        def step(t, carry):
            cr, ci = carry
            rows = pl.ds(pl.multiple_of(t * SUBLANES, SUBLANES), SUBLANES)
            xr, xi = _scan_tile(br_ref[rows, :], bi_ref[rows, :], pw_r, pw_i, False)
            xr, xi = xr + pw_r * cr - pw_i * ci, xi + pw_r * ci + pw_i * cr
            xr_ref[rows, :] = xr
            xi_ref[rows, :] = xi
            qr_ref[rows, :] = jnp.where(row >= 1, pltpu.roll(xr, 1, 0), cr)
            qi_ref[rows, :] = jnp.where(row >= 1, pltpu.roll(xi, 1, 0), ci)
            last_r = jnp.broadcast_to(xr[SUBLANES - 1:SUBLANES, :], (SUBLANES, lb))
            last_i = jnp.broadcast_to(xi[SUBLANES - 1:SUBLANES, :], (SUBLANES, lb))
            return last_r, last_i

        zero = jnp.zeros((SUBLANES, lb), F32)
        lax.fori_loop(0, n_tiles, step, (zero, zero))

    re_in = pl.BlockSpec((s, lb), lambda j: (0, j))
    im_in = pl.BlockSpec((s, lb), lambda j: (0, j + nlb))
    pw = pl.BlockSpec((SUBLANES, lb), lambda j: (0, j))
    out = jax.ShapeDtypeStruct((s, lanes), F32)
    return pl.pallas_call(
        body, name="ssm_scan_fwd", grid=(nlb,), in_specs=[re_in, im_in, pw, pw],
        out_specs=(re_in,) * 4, out_shape=(out,) * 4,
        compiler_params=_params(("parallel",)))(bu, bu, pw_re, pw_im)


def _ssm_scan_bwd(gx, xprev_re, xprev_im, pw_re, pw_im):
    s, two_l = gx.shape
    lanes = two_l // 2
    lb = min(SCAN_LANES, lanes)
    nlb = lanes // lb
    n_tiles = s // SUBLANES

    def body(gr_ref, gi_ref, qr_ref, qi_ref, pr_ref, pi_ref, lr_ref, li_ref, ar_ref, ai_ref):
        pw_r, pw_i = pr_ref[...], -pi_ref[...]
        pwc_r = jnp.concatenate([pw_r[SUBLANES - 1 - i:SUBLANES - i, :] for i in range(SUBLANES)], axis=0)
        pwc_i = jnp.concatenate([pw_i[SUBLANES - 1 - i:SUBLANES - i, :] for i in range(SUBLANES)], axis=0)

        def step(t, carry):
            cr, ci, acc_r, acc_i = carry
            tile = n_tiles - 1 - t
            rows = pl.ds(pl.multiple_of(tile * SUBLANES, SUBLANES), SUBLANES)
            lr, li = _scan_tile(gr_ref[rows, :], gi_ref[rows, :], pw_r, pw_i, True)
            lr, li = lr + pwc_r * cr - pwc_i * ci, li + pwc_r * ci + pwc_i * cr
            lr_ref[rows, :] = lr
            li_ref[rows, :] = li
            qr, qi = qr_ref[rows, :], qi_ref[rows, :]
            acc_r = acc_r + qr * lr + qi * li
            acc_i = acc_i + qr * li - qi * lr
            first_r = jnp.broadcast_to(lr[0:1, :], (SUBLANES, lb))
            first_i = jnp.broadcast_to(li[0:1, :], (SUBLANES, lb))
            return first_r, first_i, acc_r, acc_i

        zero = jnp.zeros((SUBLANES, lb), F32)
        _, _, acc_r, acc_i = lax.fori_loop(0, n_tiles, step, (zero, zero, zero, zero))
        ar_ref[...] = jnp.sum(acc_r, axis=0, keepdims=True)
        ai_ref[...] = jnp.sum(acc_i, axis=0, keepdims=True)

    re_in = pl.BlockSpec((s, lb), lambda j: (0, j))
    im_in = pl.BlockSpec((s, lb), lambda j: (0, j + nlb))
    pw = pl.BlockSpec((SUBLANES, lb), lambda j: (0, j))
    vec = pl.BlockSpec((1, lb), lambda j: (0, j))
    out = jax.ShapeDtypeStruct((s, lanes), F32)
    small = jax.ShapeDtypeStruct((1, lanes), F32)
    return pl.pallas_call(
        body, name="ssm_scan_bwd", grid=(nlb,), in_specs=[re_in, im_in, re_in, re_in, pw, pw],
        out_specs=(re_in, re_in, vec, vec), out_shape=(out, out, small, small),
        compiler_params=_params(("parallel",)))(gx, gx, xprev_re, xprev_im, pw_re, pw_im)


def _split_dot(x, tri2):
    hi = x.astype(BF16)
    lo = (x - hi.astype(F32)).astype(BF16)
    return jnp.dot(jnp.concatenate([hi, lo], axis=1), tri2, preferred_element_type=F32)


def _tri_and_ones(tk, rel):
    r_i = lax.broadcasted_iota(jnp.int32, (2 * tk, 2 * tk), 0) % tk
    c_i = lax.broadcasted_iota(jnp.int32, (2 * tk, 2 * tk), 1)
    return jnp.where((c_i >= tk) | rel(r_i, c_i), 1.0, 0.0).astype(BF16)


def _attn_scores(qs, kb, t0, s0, masked):
    z = lax.dot_general(qs, kb, (((1,), (1,)), ((), ())), preferred_element_type=F32)
    log_beta = jnp.minimum(z, 0.0) - jnp.log(1.0 + jnp.exp(-jnp.abs(z)))
    log_stay = log_beta - z
    if not masked:
        return log_beta, log_stay, None
    mask = (s0 + lax.broadcasted_iota(jnp.int32, z.shape, 1)) < (t0 + lax.broadcasted_iota(jnp.int32, z.shape, 0))
    return log_beta, jnp.where(mask, log_stay, 0.0), mask


def _attn_blocks(s):
    tq, tk = min(ATT_TQ, s), min(ATT_TK, s)
    assert s % tq == 0 and tq % tk == 0
    return tq, tk, tq // tk


def _attn_fwd(qn, kn, v):
    nh, s, d = qn.shape
    tq, tk, r = _attn_blocks(s)

    def body(q_ref, k_ref, v_ref, o_ref, b_ref):
        qi = pl.program_id(1)
        qs = (q_ref[0].astype(F32) * (HEAD_DIM ** -0.5)).astype(BF16)
        tri = _tri_and_ones(tk, lambda row, col: row > col)

        def tiles(kq, carry, masked):
            o_acc, acc = carry
            ks = pl.ds(pl.multiple_of(kq * tq, tq), tq)
            log_beta, log_stay, mask = _attn_scores(qs, k_ref[0, ks, :], qi * tq, kq * tq, masked)
            sums = [_split_dot(log_stay[:, u * tk:(u + 1) * tk], tri) for u in range(r)]
            later = [None] * r
            for u in reversed(range(r)):
                later[u] = acc + sums[u][:, :tk]
                acc = acc + sums[u][:, tk:]
            w = jnp.exp(log_beta + jnp.concatenate(later, axis=1))
            if masked:
                w = jnp.where(mask, w, 0.0)
            o_acc = o_acc + jnp.dot(w.astype(BF16), v_ref[0, ks, :], preferred_element_type=F32)
            return o_acc, acc

        carry = tiles(qi, (jnp.zeros((tq, d), F32), jnp.zeros((tq, tk), F32)), True)
        o_acc, acc = lax.fori_loop(0, qi, lambda it, c: tiles(qi - 1 - it, c, False), carry)
        o_ref[0] = o_acc
        b_ref[0] = acc[:, 0:1]

    q_spec = pl.BlockSpec((1, tq, d), lambda h, i: (h, i, 0))
    kv_spec = pl.BlockSpec((1, s, d), lambda h, i: (h, 0, 0))
    return pl.pallas_call(
        body, name="attn_fwd", grid=(nh, s // tq), in_specs=[q_spec, kv_spec, kv_spec],
        out_specs=(q_spec, pl.BlockSpec((1, tq, 1), lambda h, i: (h, i, 0))),
        out_shape=(jax.ShapeDtypeStruct((nh, s, d), F32), jax.ShapeDtypeStruct((nh, s, 1), F32)),
        compiler_params=_params(("parallel", "parallel")))(qn, kn, v)


def _attn_bwd(qn, kn, v, bsum, do):
    nh, s, d = qn.shape
    tq, tk, r = _attn_blocks(s)

    def body(q_ref, k_ref, v_ref, b_ref, do_ref, dq_ref, dkt_ref, dvt_ref):
        qi = pl.program_id(1)

        @pl.when(qi == 0)
        def _():
            dkt_ref[...] = jnp.zeros_like(dkt_ref)
            dvt_ref[...] = jnp.zeros_like(dvt_ref)

        scale = HEAD_DIM ** -0.5
        qf = q_ref[0].astype(F32) * scale
        qs, qst = qf.astype(BF16), qf.T.astype(BF16)
        dof = do_ref[0]
        dob, dobt = dof.astype(BF16), dof.T.astype(BF16)
        tri_upto = _tri_and_ones(tk, lambda row, col: row <= col)
        tri_before = _tri_and_ones(tk, lambda row, col: row < col)

        def tiles(kq, carry, masked):
            dq_acc, rest, g_prefix = carry
            ks = pl.ds(pl.multiple_of(kq * tq, tq), tq)
            kb, vb = k_ref[0, ks, :], v_ref[0, ks, :]
            log_beta, log_stay, mask = _attn_scores(qs, kb, qi * tq, kq * tq, masked)
            dw = lax.dot_general(dob, vb, (((1,), (1,)), ((), ())), preferred_element_type=F32)
            sums = [_split_dot(log_stay[:, u * tk:(u + 1) * tk], tri_upto) for u in range(r)]
            later = [None] * r
            for u in range(r):
                later[u] = rest - sums[u][:, :tk]
                rest = rest - sums[u][:, tk:]
            w = jnp.exp(log_beta + jnp.concatenate(later, axis=1))
            if masked:
                w = jnp.where(mask, w, 0.0)
            g = w * dw
            gsums = [_split_dot(g[:, u * tk:(u + 1) * tk], tri_before) for u in range(r)]
            before = [None] * r
            for u in range(r):
                before[u] = g_prefix + gsums[u][:, :tk]
                g_prefix = g_prefix + gsums[u][:, tk:]
            dz = g - jnp.exp(log_beta) * (g + jnp.concatenate(before, axis=1))
            if masked:
                dz = jnp.where(mask, dz, 0.0)
            dzb = dz.astype(BF16)
            dq_acc = dq_acc + jnp.dot(dzb, kb, preferred_element_type=F32)
            dkt_ref[0, :, ks] += jnp.dot(qst, dzb, preferred_element_type=F32)
            dvt_ref[0, :, ks] += jnp.dot(dobt, w.astype(BF16), preferred_element_type=F32)
            return dq_acc, rest, g_prefix

        carry = (jnp.zeros((tq, d), F32), jnp.broadcast_to(b_ref[0], (tq, tk)), jnp.zeros((tq, tk), F32))
        carry = lax.fori_loop(0, qi, lambda kq, c: tiles(kq, c, False), carry)
        dq_acc, _, _ = tiles(qi, carry, True)
        dq_ref[0] = dq_acc * scale

    q_spec = pl.BlockSpec((1, tq, d), lambda h, i: (h, i, 0))
    kv_spec = pl.BlockSpec((1, s, d), lambda h, i: (h, 0, 0))
    t_spec = pl.BlockSpec((1, d, s), lambda h, i: (h, 0, 0))
    b_spec = pl.BlockSpec((1, tq, 1), lambda h, i: (h, i, 0))
    full = jax.ShapeDtypeStruct((nh, s, d), F32)
    full_t = jax.ShapeDtypeStruct((nh, d, s), F32)
    return pl.pallas_call(
        body, name="attn_bwd", grid=(nh, s // tq), in_specs=[q_spec, kv_spec, kv_spec, b_spec, q_spec],
        out_specs=(q_spec, t_spec, t_spec), out_shape=(full, full_t, full_t),
        compiler_params=_params(("parallel", "arbitrary")))(qn, kn, v, bsum, do)


def _reduce_adamw(parts, w, m, v, *, name):
    n, rows, cols = parts.shape
    tr = max(t for t in range(SUBLANES, min(rows, 256) + 1, SUBLANES) if rows % t == 0)
    c1 = 1.0 - ADAM_B1 ** ADAM_STEP
    c2 = 1.0 - ADAM_B2 ** ADAM_STEP

    def body(p_ref, w_ref, m_ref, v_ref, g_ref, d_ref, nm_ref, nv_ref):
        g = p_ref[0]
        for i in range(1, n):
            g = g + p_ref[i]
        nm = ADAM_B1 * m_ref[...] + (1.0 - ADAM_B1) * g
        nv = ADAM_B2 * v_ref[...] + (1.0 - ADAM_B2) * (g * g)
        g_ref[...] = g
        nm_ref[...] = nm
        nv_ref[...] = nv
        d_ref[...] = -ADAM_LR * ((nm / c1) / (jnp.sqrt(nv / c2) + ADAM_EPS) + ADAM_WD * w_ref[...])

    row = pl.BlockSpec((tr, cols), lambda i: (i, 0))
    out = jax.ShapeDtypeStruct((rows, cols), F32)
    return pl.pallas_call(
        body, name=name, grid=(rows // tr,),
        in_specs=[pl.BlockSpec((n, tr, cols), lambda i: (0, i, 0)), row, row, row],
        out_specs=(row,) * 4, out_shape=(out,) * 4, compiler_params=_params(("parallel",)))(parts, w, m, v)


def _all_gather(x_shard, *, name):
    m_per, n = x_shard.shape

    def body(x_ref, out_ref, send_sems, recv_sems, local_sem):
        x, y, c = lax.axis_index("x"), lax.axis_index("y"), lax.axis_index("c")
        me, sibling = (x, y, c), (x, y, 1 - c)
        chips = [(1 - x, y), (x, 1 - y), (1 - x, 1 - y)]

        def rows(px, py, pc):
            return out_ref.at[pl.ds((4 * px + 2 * py + pc) * m_per, m_per), :]

        def copy(k, block, to, src=None):
            return pltpu.make_async_remote_copy(
                src_ref=rows(*block) if src is None else src, dst_ref=rows(*block),
                send_sem=send_sems.at[k], recv_sem=recv_sems.at[k], device_id=to, device_id_type=MESH_IDS)

        mine = pltpu.make_async_copy(x_ref, rows(*me), local_sem)
        mine.start()
        first = [copy(0, me, sibling, src=x_ref)]
        first += [copy(1 + j, me, (*chip, c), src=x_ref) for j, chip in enumerate(chips)]
        for cp in first:
            cp.start()
        passed = [copy(4 + j, (*chip, c), sibling) for j, chip in enumerate(chips)]
        for j, chip in enumerate(chips):
            copy(1 + j, (*chip, c), me).wait_recv()
            passed[j].start()
        copy(0, sibling, me).wait_recv()
        for j, chip in enumerate(chips):
            copy(4 + j, (*chip, 1 - c), me).wait_recv()
        for cp in first + passed:
            cp.wait_send()
        mine.wait()

    return pl.pallas_call(
        body, name=name, out_shape=jax.ShapeDtypeStruct((N_DEV * m_per, n), x_shard.dtype),
        in_specs=[pl.BlockSpec(memory_space=pl.ANY)], out_specs=pl.BlockSpec(memory_space=pl.ANY),
        scratch_shapes=[pltpu.SemaphoreType.DMA((7,)), pltpu.SemaphoreType.DMA((7,)), pltpu.SemaphoreType.DMA],
    )(x_shard)


def _all_to_all(x, *, name):
    def body(x_ref, out_ref, send_sems, recv_sems, local_sem):
        x, y, c = lax.axis_index("x"), lax.axis_index("y"), lax.axis_index("c")
        me = 4 * x + 2 * y + c
        mine = pltpu.make_async_copy(x_ref.at[me], out_ref.at[me], local_sem)
        mine.start()
        copies = []
        for k in range(1, N_DEV):
            px, py, pc = x ^ (k >> 2), y ^ ((k >> 1) & 1), c ^ (k & 1)
            peer = 4 * px + 2 * py + pc
            copies.append(pltpu.make_async_remote_copy(
                src_ref=x_ref.at[peer], dst_ref=out_ref.at[me], send_sem=send_sems.at[k - 1],
                recv_sem=recv_sems.at[k - 1], device_id=(px, py, pc), device_id_type=MESH_IDS))
        for cp in copies:
            cp.start()
        for k in range(1, N_DEV):
            px, py, pc = x ^ (k >> 2), y ^ ((k >> 1) & 1), c ^ (k & 1)
            peer = 4 * px + 2 * py + pc
            pltpu.make_async_remote_copy(
                src_ref=x_ref.at[peer], dst_ref=out_ref.at[peer], send_sem=send_sems.at[k - 1],
                recv_sem=recv_sems.at[k - 1], device_id=(px, py, pc), device_id_type=MESH_IDS).wait_recv()
        for cp in copies:
            cp.wait_send()
        mine.wait()

    return pl.pallas_call(
        body, name=name, out_shape=jax.ShapeDtypeStruct(x.shape, x.dtype),
        in_specs=[pl.BlockSpec(memory_space=pl.ANY)], out_specs=pl.BlockSpec(memory_space=pl.ANY),
        scratch_shapes=[pltpu.SemaphoreType.DMA((7,)), pltpu.SemaphoreType.DMA((7,)), pltpu.SemaphoreType.DMA],
    )(x)


def _chunks_first(full, axis):
    if axis == 0:
        return full.reshape(N_DEV, -1)
    r, c = full.shape
    return full.reshape(r, N_DEV, c // N_DEV).transpose(1, 0, 2).reshape(N_DEV, -1)


def _from_chunks(chunks, name, shard_shape):
    r, c = shard_shape
    if BIG_SPLIT_AXIS[name] == 0:
        return chunks.reshape(N_DEV * r, c)
    return chunks.reshape(N_DEV, r, c).transpose(1, 0, 2).reshape(r, N_DEV * c)


def _pack_big_shards(shards):
    parts = [shards[n][l].reshape(-1) for l in range(DEPTH) for n in BIG]
    return jnp.concatenate(parts).reshape(-1, PACK_COLS)


def _unpack_big_shards(flat, shapes):
    flat = flat.reshape(-1)
    out = {n: [] for n in BIG}
    off = 0
    for _ in range(DEPTH):
        for n in BIG:
            r, c = shapes[n]
            out[n].append(flat[off:off + r * c].reshape(r, c))
            off += r * c
    return {n: jnp.stack(v) for n, v in out.items()}


def _pack_small(vals):
    flat = jnp.concatenate([vals[n].reshape(-1) for n in SMALL])
    rows = -(-flat.shape[0] // (SUBLANES * PACK_COLS)) * SUBLANES
    return jnp.pad(flat, (0, rows * PACK_COLS - flat.shape[0])).reshape(rows, PACK_COLS)


def _unpack_small(flat, shapes):
    flat = flat.reshape(-1)
    out, off = {}, 0
    for n in SMALL:
        size = math.prod(shapes[n])
        out[n] = flat[off:off + size].reshape(shapes[n])
        off += size
    return out


def _heads(t2d):
    s = t2d.shape[0]
    return t2d.reshape(s, ATTN_HEADS, HEAD_DIM).transpose(1, 0, 2)


def _unheads(t3d):
    return t3d.transpose(1, 0, 2).reshape(t3d.shape[1], ATTN_WIDTH)


def _ssm_setup(sp):
    g, p = SSM_GROUPS, SSM_STATE
    a_re = sp["ssm_a_re"][:, None, :]
    a_im = sp["ssm_a_im"][:, None, :]
    log_dt = jnp.broadcast_to(sp["ssm_log_dt"][:, None, None], (g, 1, p))
    bt_re = sp["ssm_b_re"].transpose(0, 2, 1)
    bt_im = sp["ssm_b_im"].transpose(0, 2, 1)
    return a_re, a_im, log_dt, bt_re, bt_im


def _layer_fwd(h, p_l, w, sp):
    s = h.shape[0]
    hn = _rms_fwd(h, sp["mix_norm_g"][None], name="rms_mix")
    proj = _mm(hn, w["w_in"], name="mm_proj")
    u = proj[:, :SSM_WIDTH]

    pw_re, pw_im, bbt_re, bbt_im = _ssm_param_fwd(*_ssm_setup(sp))
    pw_re = pw_re.transpose(1, 0, 2).reshape(SUBLANES, SSM_LANES)
    pw_im = pw_im.transpose(1, 0, 2).reshape(SUBLANES, SSM_LANES)
    bw_re, bw_im = _block_diag(bbt_re).astype(BF16), _block_diag(bbt_im).astype(BF16)
    cw_re, cw_im = _block_diag(sp["ssm_c_re"]).astype(BF16), _block_diag(-sp["ssm_c_im"]).astype(BF16)
    bu = _mm(u, jnp.concatenate([bw_re, bw_im], axis=1), name="mm_bu")
    x_re, x_im, xp_re, xp_im = _ssm_scan_fwd(bu, pw_re, pw_im)
    ycx = _mm(x_im, cw_im, tb=True, add=_mm(x_re, cw_re, tb=True, name="mm_cx_re"), name="mm_cx_im")
    dvec = sp["ssm_d"].reshape(1, SSM_WIDTH)
    y, z = _ssm_mid_fwd(ycx, proj, dvec)
    zz = _mm(z, w["ssm_w_glu"], name="mm_glu")

    q3 = _heads(proj[:, 2 * SSM_WIDTH:2 * SSM_WIDTH + ATTN_WIDTH]).reshape(ATTN_HEADS * s, HEAD_DIM)
    k3 = _heads(proj[:, 2 * SSM_WIDTH + ATTN_WIDTH:2 * SSM_WIDTH + 2 * ATTN_WIDTH]).reshape(ATTN_HEADS * s, HEAD_DIM)
    v3 = _heads(proj[:, 2 * SSM_WIDTH + 2 * ATTN_WIDTH:2 * SSM_WIDTH + 3 * ATTN_WIDTH]).astype(BF16)
    qn = _rms_fwd(q3, sp["q_norm_g"][None], name="rms_q").reshape(ATTN_HEADS, s, HEAD_DIM)
    kn = _rms_fwd(k3, sp["k_norm_g"][None], name="rms_k").reshape(ATTN_HEADS, s, HEAD_DIM)
    o3, bsum = _attn_fwd(qn, kn, v3)
    o = _unheads(o3)

    bglu = sp["ssm_b_glu"][None]
    ycat = _gate_fwd(zz, bglu, proj, o)
    h2 = _mm(ycat, w["w_out"], add=h, name="mm_out")
    hn2 = _rms_fwd(h2, sp["ple_norm_g"][None], name="rms_ple")
    pgl = _mm(hn2, w["w_ple_gate"], name="mm_ple_gate")
    pp = _mm(p_l, w["w_ple_proj"], name="mm_ple_proj")
    h3 = _ple_out(h2, pgl, pp)
    saved = dict(h=h, hn=hn, proj=proj, u=u, pw_re=pw_re, pw_im=pw_im, bw_re=bw_re, bw_im=bw_im,
                 cw_re=cw_re, cw_im=cw_im, x_re=x_re, x_im=x_im, xp_re=xp_re, xp_im=xp_im, y=y, z=z,
                 zz=zz, q3=q3, k3=k3, v3=v3, qn=qn, kn=kn, bsum=bsum, o=o, ycat=ycat, h2=h2, hn2=hn2,
                 pgl=pgl, pp=pp, p_l=p_l, dvec=dvec, bglu=bglu)
    return h3, saved


def _layer_bwd(dh3, sv, w, sp):
    s = dh3.shape[0]
    g, hh, p = SSM_GROUPS, SSM_GROUP, SSM_STATE
    grads = {}
    dgp, dpp = _ple_bwd(dh3, sv["pgl"], sv["pp"])
    grads["w_ple_gate"] = _mm(sv["hn2"], dgp, ta=True, name="mm_d_ple_gate")
    grads["w_ple_proj"] = _mm(sv["p_l"], dpp, ta=True, name="mm_d_ple_proj")
    dhn2 = _mm(dgp, w["w_ple_gate"], tb=True, name="mm_dhn2")
    dh2, dg2 = _rms_bwd(sv["h2"], sp["ple_norm_g"][None], dhn2, dres=dh3, name="rms_ple_bwd")
    grads["ple_norm_g"] = dg2[0]
    grads["w_out"] = _mm(sv["ycat"], dh2, ta=True, name="mm_d_out")
    dyc = _mm(dh2, w["w_out"], tb=True, name="mm_dycat")
    dzz, dgs, do, dga, dbglu = _gate_bwd(dyc, sv["zz"], sv["bglu"], sv["proj"], sv["o"])
    grads["ssm_b_glu"] = dbglu[0]

    dqn, dkn_t, dv_t = _attn_bwd(sv["qn"], sv["kn"], sv["v3"], sv["bsum"], _heads(do))
    dkn = dkn_t.transpose(0, 2, 1)
    dq3, dgq = _rms_bwd(sv["q3"], sp["q_norm_g"][None], dqn.reshape(ATTN_HEADS * s, HEAD_DIM), name="rms_q_bwd")
    dk3, dgk = _rms_bwd(sv["k3"], sp["k_norm_g"][None], dkn.reshape(ATTN_HEADS * s, HEAD_DIM), name="rms_k_bwd")
    grads["q_norm_g"], grads["k_norm_g"] = dgq[0], dgk[0]
    dq = _unheads(dq3.reshape(ATTN_HEADS, s, HEAD_DIM))
    dk = _unheads(dk3.reshape(ATTN_HEADS, s, HEAD_DIM))
    dv = dv_t.transpose(2, 0, 1).reshape(s, ATTN_WIDTH)

    grads["ssm_w_glu"] = _mm(sv["z"], dzz, ta=True, name="mm_d_glu")
    dz = _mm(dzz, w["ssm_w_glu"], tb=True, name="mm_dz")
    dy, du_skip, dd = _ssm_mid_bwd(dz, sv["y"], sv["proj"], sv["dvec"])
    grads["ssm_d"] = dd.reshape(g, hh)
    gx = _mm(dy, jnp.concatenate([sv["cw_re"], sv["cw_im"]], axis=1), name="mm_gx")
    lam_re, lam_im, gab_re, gab_im = _ssm_scan_bwd(gx, sv["xp_re"], sv["xp_im"], sv["pw_re"], sv["pw_im"])
    du = _mm(lam_im, sv["bw_im"], tb=True, name="mm_du_im",
             add=_mm(lam_re, sv["bw_re"], tb=True, add=du_skip, name="mm_du_re"))
    gbb_re = _block_diag_take(_mm(sv["u"], lam_re, ta=True, name="mm_d_bw_re"), g, hh, p)
    gbb_im = _block_diag_take(_mm(sv["u"], lam_im, ta=True, name="mm_d_bw_im"), g, hh, p)
    grads["ssm_c_re"] = _block_diag_take(_mm(dy, sv["x_re"], ta=True, name="mm_d_cw_re"), g, hh, p)
    grads["ssm_c_im"] = -_block_diag_take(_mm(dy, sv["x_im"], ta=True, name="mm_d_cw_im"), g, hh, p)
    da_re, da_im, dldt, dbt_re, dbt_im = _ssm_param_bwd(
        *_ssm_setup(sp), gab_re.reshape(g, 1, p), gab_im.reshape(g, 1, p), gbb_re, gbb_im)
    grads["ssm_a_re"], grads["ssm_a_im"] = da_re[:, 0, :], da_im[:, 0, :]
    grads["ssm_log_dt"] = dldt[:, 0, 0]
    grads["ssm_b_re"], grads["ssm_b_im"] = dbt_re.transpose(0, 2, 1), dbt_im.transpose(0, 2, 1)

    dproj = jnp.concatenate([du, dgs, dq, dk, dv, dga], axis=1)
    grads["w_in"] = _mm(sv["hn"], dproj, ta=True, name="mm_d_in")
    dhn = _mm(dproj, w["w_in"], tb=True, name="mm_dhn")
    dh, dg1 = _rms_bwd(sv["h"], sp["mix_norm_g"][None], dhn, dres=dh2, name="rms_mix_bwd")
    grads["mix_norm_g"] = dg1[0]
    return dh, grads


def _local_step(x, p, target, big, small):
    h = x
    saved = []
    for l in range(DEPTH):
        sp = {n: small[n][l] for n in SMALL}
        h, sv = _layer_fwd(h, p[l], big[l], sp)
        saved.append(sv)
    dh, loss_parts = _loss_head(h, target)
    grads = [None] * DEPTH
    for l in reversed(range(DEPTH)):
        sp = {n: small[n][l] for n in SMALL}
        dh, grads[l] = _layer_bwd(dh, saved[l], big[l], sp)
    return jnp.sum(loss_parts), dh, grads


def kernel(x, p, mix_norm_g, w_in, ssm_a_re, ssm_a_im, ssm_log_dt, ssm_b_re, ssm_b_im, ssm_c_re, ssm_c_im, ssm_d, ssm_w_glu, ssm_b_glu, q_norm_g, k_norm_g, w_out, ple_norm_g, w_ple_gate, w_ple_proj, loss_target, m_mix_norm_g, m_w_in, m_ssm_a_re, m_ssm_a_im, m_ssm_log_dt, m_ssm_b_re, m_ssm_b_im, m_ssm_c_re, m_ssm_c_im, m_ssm_d, m_ssm_w_glu, m_ssm_b_glu, m_q_norm_g, m_k_norm_g, m_w_out, m_ple_norm_g, m_w_ple_gate, m_w_ple_proj, v_mix_norm_g, v_w_in, v_ssm_a_re, v_ssm_a_im, v_ssm_log_dt, v_ssm_b_re, v_ssm_b_im, v_ssm_c_re, v_ssm_c_im, v_ssm_d, v_ssm_w_glu, v_ssm_b_glu, v_q_norm_g, v_k_norm_g, v_w_out, v_ple_norm_g, v_w_ple_gate, v_w_ple_proj):
    given = dict(locals())
    wts = {n: given[n] for n in WEIGHTS}
    mom = {n: given["m_" + n] for n in WEIGHTS}
    var = {n: given["v_" + n] for n in WEIGHTS}
    shard_shapes = {n: wts[n].shape[1:] for n in BIG}

    packed_w = _pack_big_shards(wts)
    gathered = _all_gather(packed_w.astype(BF16), name="gather_weights")
    gathered = gathered.reshape(N_DEV, -1)
    big, off = [], 0
    for _ in range(DEPTH):
        layer = {}
        for n in BIG:
            size = math.prod(shard_shapes[n])
            layer[n] = _from_chunks(gathered[:, off:off + size], n, shard_shapes[n])
            off += size
        big.append(layer)

    small = {n: wts[n] for n in SMALL}
    loss, dx, grads = _local_step(x[0], p[:, 0], loss_target[0], big, small)
    loss = lax.psum(loss, ("x", "y", "c"))

    send = jnp.concatenate([_chunks_first(grads[l][n], BIG_SPLIT_AXIS[n]) for l in range(DEPTH) for n in BIG], axis=1)
    rows = send.shape[1] // PACK_COLS
    parts = _all_to_all(send.reshape(N_DEV, rows, PACK_COLS), name="exchange_weight_grads")
    big_out = _reduce_adamw(parts, packed_w, _pack_big_shards(mom), _pack_big_shards(var), name="adamw_sharded")
    big_out = [_unpack_big_shards(a, shard_shapes) for a in big_out]

    small_shapes = {n: wts[n].shape for n in SMALL}
    local_small = _pack_small({n: jnp.stack([grads[l][n] for l in range(DEPTH)]) for n in SMALL})
    all_small = _all_gather(local_small, name="gather_small_grads")
    all_small = all_small.reshape(N_DEV, local_small.shape[0], PACK_COLS)
    small_out = _reduce_adamw(all_small, _pack_small(small), _pack_small({n: mom[n] for n in SMALL}),
                              _pack_small({n: var[n] for n in SMALL}), name="adamw_replicated")
    small_out = [_unpack_small(a, small_shapes) for a in small_out]

    outs = [loss, dx[None]]
    for k in range(4):
        outs += [big_out[k][n] if n in BIG else small_out[k][n] for n in WEIGHTS]
    return tuple(outs)
```

```python
import math

import jax
import jax.numpy as jnp
from jax import lax
from jax.experimental import pallas as pl
from jax.experimental.pallas import tpu as pltpu

F32 = jnp.float32
BF16 = jnp.bfloat16
MESH_IDS = pl.DeviceIdType.MESH

N_DEV = 8
D_MODEL = 1024
DEPTH = 2
PLE_DIM = 256
SSM_WIDTH = 512
SSM_GROUP = 16
SSM_GROUPS = 32
SSM_STATE = 64
SSM_LANES = SSM_GROUPS * SSM_STATE
ATTN_WIDTH = 512
ATTN_HEADS = 8
HEAD_DIM = 64
RMS_EPS = 1e-6
ADAM_LR = 0.001
ADAM_B1 = 0.9
ADAM_B2 = 0.999
ADAM_EPS = 1e-08
ADAM_WD = 0.01
ADAM_STEP = 10

VMEM_LIMIT = 56 * 1024 * 1024
ATT_TQ = 512
ATT_TK = 128
SSM_BLOCK_GROUPS = 8
SSM_TIME_BLOCK = 512
SUBLANES = 8
PACK_COLS = 1024

BIG = ("w_in", "ssm_w_glu", "w_out", "w_ple_gate", "w_ple_proj")
SMALL = ("mix_norm_g", "ssm_a_re", "ssm_a_im", "ssm_log_dt", "ssm_b_re", "ssm_b_im", "ssm_c_re",
         "ssm_c_im", "ssm_d", "ssm_b_glu", "q_norm_g", "k_norm_g", "ple_norm_g")
WEIGHTS = ("mix_norm_g", "w_in", "ssm_a_re", "ssm_a_im", "ssm_log_dt", "ssm_b_re", "ssm_b_im",
           "ssm_c_re", "ssm_c_im", "ssm_d", "ssm_w_glu", "ssm_b_glu", "q_norm_g", "k_norm_g",
           "w_out", "ple_norm_g", "w_ple_gate", "w_ple_proj")
BIG_SPLIT_AXIS = {"w_in": 1, "ssm_w_glu": 1, "w_out": 0, "w_ple_gate": 0, "w_ple_proj": 1}


def _params(sem=None):
    return pltpu.CompilerParams(dimension_semantics=sem, vmem_limit_bytes=VMEM_LIMIT)


def _sigmoid(x):
    return 1.0 / (1.0 + jnp.exp(-x))


_GELU_K = math.sqrt(2.0 / math.pi)
_GELU_C = 0.044715


def _gelu(x):
    return 0.5 * x * (1.0 + jnp.tanh(_GELU_K * (x + _GELU_C * x * x * x)))


def _gelu_grad(x):
    th = jnp.tanh(_GELU_K * (x + _GELU_C * x * x * x))
    return 0.5 * (1.0 + th) + 0.5 * x * (1.0 - th * th) * _GELU_K * (1.0 + 3.0 * _GELU_C * x * x)


def _mm(a, b, *, name, ta=False, tb=False, add=None, out_dtype=F32, tm=1024, tn=1024, tk=512):
    m, k = (a.shape[1], a.shape[0]) if ta else a.shape
    n = b.shape[0] if tb else b.shape[1]
    tm, tn, tk = min(tm, m), min(tn, n), min(tk, k)
    assert m % tm == 0 and n % tn == 0 and k % tk == 0, (name, a.shape, b.shape)
    nk = k // tk
    dims = (((0 if ta else 1,), (1 if tb else 0,)), ((), ()))

    def body(*refs):
        if add is None:
            a_ref, b_ref, o_ref, acc_ref = refs
        else:
            a_ref, b_ref, r_ref, o_ref, acc_ref = refs
        kk = pl.program_id(2)

        @pl.when(kk == 0)
        def _():
            acc_ref[...] = jnp.zeros_like(acc_ref)

        acc_ref[...] += lax.dot_general(a_ref[...].astype(BF16), b_ref[...].astype(BF16), dims,
                                        preferred_element_type=F32)

        @pl.when(kk == nk - 1)
        def _():
            res = acc_ref[...]
            if add is not None:
                res = res + r_ref[...].astype(F32)
            o_ref[...] = res.astype(out_dtype)

    a_spec = (pl.BlockSpec((tk, tm), lambda i, j, kk: (kk, i)) if ta
              else pl.BlockSpec((tm, tk), lambda i, j, kk: (i, kk)))
    b_spec = (pl.BlockSpec((tn, tk), lambda i, j, kk: (j, kk)) if tb
              else pl.BlockSpec((tk, tn), lambda i, j, kk: (kk, j)))
    o_spec = pl.BlockSpec((tm, tn), lambda i, j, kk: (i, j))
    ins, specs = [a, b], [a_spec, b_spec]
    if add is not None:
        ins.append(add)
        specs.append(o_spec)
    return pl.pallas_call(
        body, name=name, grid=(m // tm, n // tn, nk), in_specs=specs, out_specs=o_spec,
        out_shape=jax.ShapeDtypeStruct((m, n), out_dtype),
        scratch_shapes=[pltpu.VMEM((tm, tn), F32)],
        compiler_params=_params(("parallel", "parallel", "arbitrary")),
    )(*ins)


def _rows_tile(rows, want=512):
    t = min(rows, want)
    assert rows % t == 0
    return t


def _rowwise(body, name, rows, ins, outs):
    widest = max(cols for _, cols, _ in ins)
    tm = _rows_tile(rows, 512 if widest > 128 else 4096)
    in_specs = []
    for _, cols, cb in ins:
        if cb is None:
            in_specs.append(pl.BlockSpec((1, cols), lambda i: (0, 0)))
        else:
            in_specs.append(pl.BlockSpec((tm, cols), lambda i, cb=cb: (i, cb)))
    out_specs, out_shapes = [], []
    for cols, dtype, is_acc in outs:
        if is_acc:
            out_specs.append(pl.BlockSpec((1, cols), lambda i: (0, 0)))
            out_shapes.append(jax.ShapeDtypeStruct((1, cols), dtype))
        else:
            out_specs.append(pl.BlockSpec((tm, cols), lambda i: (i, 0)))
            out_shapes.append(jax.ShapeDtypeStruct((rows, cols), dtype))
    any_acc = any(o[2] for o in outs)
    return pl.pallas_call(
        body, name=name, grid=(rows // tm,), in_specs=in_specs, out_specs=tuple(out_specs),
        out_shape=tuple(out_shapes),
        compiler_params=_params(("arbitrary",) if any_acc else ("parallel",)))(*[a for a, _, _ in ins])


def _accumulate(ref, value):
    @pl.when(pl.program_id(0) == 0)
    def _():
        ref[...] = jnp.zeros_like(ref)

    ref[...] += value


def _rms_fwd(h, g, *, name):
    rows, d = h.shape

    def body(h_ref, g_ref, o_ref):
        hv = h_ref[...]
        r = lax.rsqrt(jnp.mean(hv * hv, axis=-1, keepdims=True) + RMS_EPS)
        o_ref[...] = (hv * r * g_ref[...]).astype(BF16)

    return _rowwise(body, name, rows, [(h, d, 0), (g, d, None)], [(d, BF16, False)])[0]


def _rms_bwd(h, g, dhn, *, name, dres=None):
    rows, d = h.shape

    def body(*refs):
        if dres is None:
            h_ref, g_ref, dn_ref, dh_ref, dg_ref = refs
        else:
            h_ref, g_ref, dn_ref, dr_ref, dh_ref, dg_ref = refs
        hv = h_ref[...]
        dn = dn_ref[...].astype(F32)
        r = lax.rsqrt(jnp.mean(hv * hv, axis=-1, keepdims=True) + RMS_EPS)
        a = dn * g_ref[...]
        dot = jnp.mean(a * hv, axis=-1, keepdims=True)
        dh = r * a - hv * (r * r * r * dot)
        if dres is not None:
            dh = dh + dr_ref[...]
        dh_ref[...] = dh
        _accumulate(dg_ref, jnp.sum(dn * hv * r, axis=0, keepdims=True))

    ins = [(h, d, 0), (g, d, None), (dhn, d, 0)]
    if dres is not None:
        ins.append((dres, d, 0))
    return _rowwise(body, name, rows, ins, [(d, F32, False), (d, F32, True)])


def _gate_fwd(zz, bglu, proj, o):
    rows = zz.shape[0]
    w = SSM_WIDTH

    def body(zz_ref, b_ref, gs_ref, o_ref, ga_ref, y_ref):
        zz_v = zz_ref[...] + b_ref[...]
        val, gate = zz_v[:, :w], zz_v[:, w:]
        gs, ga = gs_ref[...], ga_ref[...]
        y_ref[:, :w] = (val * _sigmoid(gate) * (gs * _sigmoid(gs))).astype(BF16)
        y_ref[:, w:] = (o_ref[...] * (ga * _sigmoid(ga))).astype(BF16)

    return _rowwise(body, "gate_fwd", rows,
                    [(zz, 2 * w, 0), (bglu, 2 * w, None), (proj, w, 1), (o, w, 0), (proj, w, 5)],
                    [(2 * w, BF16, False)])[0]


def _gate_bwd(dyc, zz, bglu, proj, o):
    rows = zz.shape[0]
    w = SSM_WIDTH

    def body(dy_ref, zz_ref, b_ref, gs_ref, o_ref, ga_ref, dzz_ref, dgs_ref, do_ref, dga_ref, db_ref):
        zz_v = zz_ref[...] + b_ref[...]
        val, gate = zz_v[:, :w], zz_v[:, w:]
        gs, ga = gs_ref[...], ga_ref[...]
        dys, dya = dy_ref[:, :w], dy_ref[:, w:]
        sg, ss, sa = _sigmoid(gate), _sigmoid(gs), _sigmoid(ga)
        glu = val * sg
        dglu = dys * (gs * ss)
        dgs_ref[...] = dys * glu * (ss * (1.0 + gs * (1.0 - ss)))
        dval = dglu * sg
        dgate = dglu * val * sg * (1.0 - sg)
        dzz_ref[:, :w] = dval.astype(BF16)
        dzz_ref[:, w:] = dgate.astype(BF16)
        do_ref[...] = dya * (ga * sa)
        dga_ref[...] = dya * o_ref[...] * (sa * (1.0 + ga * (1.0 - sa)))
        _accumulate(db_ref, jnp.concatenate([jnp.sum(dval, axis=0, keepdims=True),
                                             jnp.sum(dgate, axis=0, keepdims=True)], axis=1))

    return _rowwise(body, "gate_bwd", rows,
                    [(dyc, 2 * w, 0), (zz, 2 * w, 0), (bglu, 2 * w, None), (proj, w, 1), (o, w, 0), (proj, w, 5)],
                    [(2 * w, BF16, False), (w, F32, False), (w, F32, False), (w, F32, False), (2 * w, F32, True)])


def _ple_out(h2, pgl, pp):
    rows, d = h2.shape

    def body(h_ref, g_ref, p_ref, o_ref):
        o_ref[...] = h_ref[...] + _sigmoid(g_ref[...]) * p_ref[...]

    return _rowwise(body, "ple_out", rows, [(h2, d, 0), (pgl, d, 0), (pp, d, 0)], [(d, F32, False)])[0]


def _ple_bwd(dh3, pgl, pp):
    rows, d = dh3.shape

    def body(dh_ref, g_ref, p_ref, dg_ref, dp_ref):
        sg = _sigmoid(g_ref[...])
        dh = dh_ref[...]
        dg_ref[...] = (dh * p_ref[...] * sg * (1.0 - sg)).astype(BF16)
        dp_ref[...] = (dh * sg).astype(BF16)

    return _rowwise(body, "ple_bwd", rows, [(dh3, d, 0), (pgl, d, 0), (pp, d, 0)],
                    [(d, BF16, False), (d, BF16, False)])


def _loss_head(h, target):
    rows, d = h.shape

    def body(h_ref, t_ref, dh_ref, l_ref):
        err = h_ref[...] - t_ref[...]
        dh_ref[...] = err * (1.0 / d)
        _accumulate(l_ref, jnp.sum(err * err, axis=0, keepdims=True) * (0.5 / d))

    return _rowwise(body, "loss_head", rows, [(h, d, 0), (target, d, 0)], [(d, F32, False), (d, F32, True)])


def _zoh(lr, li, dt):
    mag = jnp.exp(lr * dt)
    ab_re = mag * jnp.cos(li * dt)
    ab_im = mag * jnp.sin(li * dt)
    num_re = ab_re - 1.0
    den = lr * lr + li * li
    f_re = (num_re * lr + ab_im * li) / den
    f_im = (ab_im * lr - num_re * li) / den
    return ab_re, ab_im, f_re, f_im, den


def _ssm_param_fwd(a_re, a_im, log_dt, bt_re, bt_im):
    g, _, p = a_re.shape
    h = bt_re.shape[1]

    def body(lr_ref, li_ref, ldt_ref, br_ref, bi_ref, pr_ref, pi_ref, bbr_ref, bbi_ref):
        dt = jnp.exp(ldt_ref[...])
        ab_re, ab_im, f_re, f_im, _ = _zoh(lr_ref[...], li_ref[...], dt)
        cr, ci = ab_re, ab_im
        for k in range(SUBLANES):
            pr_ref[:, k:k + 1, :] = cr
            pi_ref[:, k:k + 1, :] = ci
            cr, ci = cr * ab_re - ci * ab_im, cr * ab_im + ci * ab_re
        br, bi = br_ref[...], bi_ref[...]
        bbr_ref[...] = f_re * br - f_im * bi
        bbi_ref[...] = f_re * bi + f_im * br

    pw = jax.ShapeDtypeStruct((g, SUBLANES, p), F32)
    large = jax.ShapeDtypeStruct((g, h, p), F32)
    return pl.pallas_call(body, name="ssm_param_fwd", out_shape=(pw, pw, large, large),
                          compiler_params=_params())(a_re, a_im, log_dt, bt_re, bt_im)


def _ssm_param_bwd(a_re, a_im, log_dt, bt_re, bt_im, gab_re, gab_im, gbb_re, gbb_im):
    g, _, p = a_re.shape
    h = bt_re.shape[1]

    def body(lr_ref, li_ref, ldt_ref, br_ref, bi_ref, gar_ref, gai_ref, gbr_ref, gbi_ref,
             dlr_ref, dli_ref, dldt_ref, dbr_ref, dbi_ref):
        lr, li = lr_ref[...], li_ref[...]
        dt = jnp.exp(ldt_ref[...])
        ab_re, ab_im, f_re, f_im, den = _zoh(lr, li, dt)
        br, bi = br_ref[...], bi_ref[...]
        gbr, gbi = gbr_ref[...], gbi_ref[...]
        dbr_ref[...] = f_re * gbr + f_im * gbi
        dbi_ref[...] = f_re * gbi - f_im * gbr
        gf_re = jnp.sum(br * gbr + bi * gbi, axis=1, keepdims=True)
        gf_im = jnp.sum(br * gbi - bi * gbr, axis=1, keepdims=True)
        il_re, il_im = lr / den, -li / den
        ga_re = gar_ref[...] + il_re * gf_re + il_im * gf_im
        ga_im = gai_ref[...] + il_re * gf_im - il_im * gf_re
        q_re = f_re * il_re - f_im * il_im
        q_im = f_re * il_im + f_im * il_re
        gl_re = -(q_re * gf_re + q_im * gf_im)
        gl_im = -(q_re * gf_im - q_im * gf_re)
        dlr_ref[...] = gl_re + dt * (ab_re * ga_re + ab_im * ga_im)
        dli_ref[...] = gl_im + dt * (ab_re * ga_im - ab_im * ga_re)
        la_re = lr * ab_re - li * ab_im
        la_im = lr * ab_im + li * ab_re
        dldt_ref[...] = jnp.sum((la_re * ga_re + la_im * ga_im) * dt, axis=2, keepdims=True)

    small = jax.ShapeDtypeStruct((g, 1, p), F32)
    one = jax.ShapeDtypeStruct((g, 1, 1), F32)
    large = jax.ShapeDtypeStruct((g, h, p), F32)
    return pl.pallas_call(body, name="ssm_param_bwd", out_shape=(small, small, one, large, large),
                          compiler_params=_params())(
                              a_re, a_im, log_dt, bt_re, bt_im, gab_re, gab_im, gbb_re, gbb_im)


def _block_diag(m):
    g, h, p = m.shape
    nb = g // SSM_BLOCK_GROUPS
    eye = jnp.eye(SSM_BLOCK_GROUPS, dtype=m.dtype)
    m4 = m.reshape(nb, SSM_BLOCK_GROUPS, h, p)
    return (m4[:, :, :, None, :] * eye[None, :, None, :, None]).reshape(nb, SSM_BLOCK_GROUPS * h, SSM_BLOCK_GROUPS * p)


def _block_diag_take(m, h, p):
    nb = m.shape[0]
    eye = jnp.eye(SSM_BLOCK_GROUPS, dtype=m.dtype)
    m5 = m.reshape(nb, SSM_BLOCK_GROUPS, h, SSM_BLOCK_GROUPS, p)
    return jnp.sum(m5 * eye[None, :, None, :, None], axis=3).reshape(nb * SSM_BLOCK_GROUPS, h, p)


def _scan_tile(br, bi, pw_re, pw_im, reverse):
    row = lax.broadcasted_iota(jnp.int32, br.shape, 0)
    xr, xi = br, bi
    for d in (1, 2, 4):
        ar, ai = pw_re[d - 1:d, :], pw_im[d - 1:d, :]
        if reverse:
            keep = row < SUBLANES - d
            sr = jnp.where(keep, pltpu.roll(xr, SUBLANES - d, 0), 0.0)
            si = jnp.where(keep, pltpu.roll(xi, SUBLANES - d, 0), 0.0)
        else:
            keep = row >= d
            sr = jnp.where(keep, pltpu.roll(xr, d, 0), 0.0)
            si = jnp.where(keep, pltpu.roll(xi, d, 0), 0.0)
        xr, xi = xr + ar * sr - ai * si, xi + ar * si + ai * sr
    return xr, xi


_NT = (((1,), (1,)), ((), ()))
_TN = (((0,), (0,)), ((), ()))


def _ssm_blocks(s):
    tt = min(SSM_TIME_BLOCK, s)
    assert s % tt == 0 and tt % SUBLANES == 0
    ch = SSM_BLOCK_GROUPS * SSM_GROUP
    st = SSM_BLOCK_GROUPS * SSM_STATE
    return tt, s // tt, SSM_WIDTH // ch, ch, st


def _ssm_fwd(proj, bc_re, bc_im, cc_re, cc_im, pw_re, pw_im, dvec):
    s = proj.shape[0]
    tt, nt, nb, ch, st = _ssm_blocks(s)
    n_tiles = tt // SUBLANES

    def body(u_ref, bre_ref, bim_ref, cre_ref, cim_ref, pr_ref, pi_ref, d_ref,
             xr_ref, xi_ref, y_ref, z_ref, carry_ref):
        @pl.when(pl.program_id(1) == 0)
        def _():
            carry_ref[...] = jnp.zeros_like(carry_ref)

        u = u_ref[...]
        ub = u.astype(BF16)
        xr_ref[...] = jnp.dot(ub, bre_ref[0], preferred_element_type=F32)
        xi_ref[...] = jnp.dot(ub, bim_ref[0], preferred_element_type=F32)
        pw_r, pw_i = pr_ref[...], pi_ref[...]

        def step(t, carry):
            cr, ci = carry
            rows = pl.ds(pl.multiple_of(t * SUBLANES, SUBLANES), SUBLANES)
            xr, xi = _scan_tile(xr_ref[rows, :], xi_ref[rows, :], pw_r, pw_i, False)
            xr, xi = xr + pw_r * cr - pw_i * ci, xi + pw_r * ci + pw_i * cr
            xr_ref[rows, :] = xr
            xi_ref[rows, :] = xi
            return (jnp.broadcast_to(xr[SUBLANES - 1:SUBLANES, :], (SUBLANES, st)),
                    jnp.broadcast_to(xi[SUBLANES - 1:SUBLANES, :], (SUBLANES, st)))

        cr, ci = lax.fori_loop(0, n_tiles, step, (carry_ref[0], carry_ref[1]), unroll=2)
        carry_ref[0] = cr
        carry_ref[1] = ci
        y = (lax.dot_general(xr_ref[...].astype(BF16), cre_ref[0], _NT, preferred_element_type=F32)
             + lax.dot_general(xi_ref[...].astype(BF16), cim_ref[0], _NT, preferred_element_type=F32)
             + d_ref[...] * u)
        y_ref[...] = y
        z_ref[...] = _gelu(y).astype(BF16)

    chan = pl.BlockSpec((tt, ch), lambda b, t: (t, b))
    state = pl.BlockSpec((tt, st), lambda b, t: (t, b))
    mat = pl.BlockSpec((1, ch, st), lambda b, t: (b, 0, 0))
    pw = pl.BlockSpec((SUBLANES, st), lambda b, t: (0, b))
    vec = pl.BlockSpec((1, ch), lambda b, t: (0, b))
    x_shape = jax.ShapeDtypeStruct((s, nb * st), F32)
    return pl.pallas_call(
        body, name="ssm_fwd", grid=(nb, nt), in_specs=[chan, mat, mat, mat, mat, pw, pw, vec],
        out_specs=(state, state, chan, chan),
        out_shape=(x_shape, x_shape, jax.ShapeDtypeStruct((s, nb * ch), F32), jax.ShapeDtypeStruct((s, nb * ch), BF16)),
        scratch_shapes=[pltpu.VMEM((2, SUBLANES, st), F32)],
        compiler_params=_params(("parallel", "arbitrary")))(proj, bc_re, bc_im, cc_re, cc_im, pw_re, pw_im, dvec)


def _ssm_bwd(dz, y, proj, x_re, x_im, bc_re, bc_im, cc_re, cc_im, pw_re, pw_im, dvec):
    s = proj.shape[0]
    tt, nt, nb, ch, st = _ssm_blocks(s)
    n_tiles = tt // SUBLANES

    def body(dz_ref, y_ref, u_ref, xr_ref, xi_ref, bre_ref, bim_ref, cre_ref, cim_ref, pr_ref, pi_ref, d_ref,
             du_ref, dbr_ref, dbi_ref, dcr_ref, dci_ref, gar_ref, gai_ref, dd_ref,
             lr_ref, li_ref, carry_ref, acc_ref):
        @pl.when(pl.program_id(1) == 0)
        def _():
            for ref in (carry_ref, acc_ref, dbr_ref, dbi_ref, dcr_ref, dci_ref, dd_ref):
                ref[...] = jnp.zeros_like(ref)

        u = u_ref[...]
        dy = dz_ref[...] * _gelu_grad(y_ref[...])
        ub, dyb = u.astype(BF16), dy.astype(BF16)
        lr_ref[...] = jnp.dot(dyb, cre_ref[0], preferred_element_type=F32)
        li_ref[...] = jnp.dot(dyb, cim_ref[0], preferred_element_type=F32)
        pw_r, pw_i = pr_ref[...], -pi_ref[...]
        pwc_r = jnp.concatenate([pw_r[SUBLANES - 1 - i:SUBLANES - i, :] for i in range(SUBLANES)], axis=0)
        pwc_i = jnp.concatenate([pw_i[SUBLANES - 1 - i:SUBLANES - i, :] for i in range(SUBLANES)], axis=0)
        row = lax.broadcasted_iota(jnp.int32, (SUBLANES, st), 0)

        def step(t, carry):
            cr, ci, acc_r, acc_i = carry
            rows = pl.ds(pl.multiple_of((n_tiles - 1 - t) * SUBLANES, SUBLANES), SUBLANES)
            lr, li = _scan_tile(lr_ref[rows, :], li_ref[rows, :], pw_r, pw_i, True)
            lr, li = lr + pwc_r * cr - pwc_i * ci, li + pwc_r * ci + pwc_i * cr
            lr_ref[rows, :] = lr
            li_ref[rows, :] = li
            nr = jnp.where(row < SUBLANES - 1, pltpu.roll(lr, SUBLANES - 1, 0), cr)
            ni = jnp.where(row < SUBLANES - 1, pltpu.roll(li, SUBLANES - 1, 0), ci)
            xr, xi = xr_ref[rows, :], xi_ref[rows, :]
            return (jnp.broadcast_to(lr[0:1, :], (SUBLANES, st)), jnp.broadcast_to(li[0:1, :], (SUBLANES, st)),
                    acc_r + xr * nr + xi * ni, acc_i + xr * ni - xi * nr)

        cr, ci, acc_r, acc_i = lax.fori_loop(
            0, n_tiles, step, (carry_ref[0], carry_ref[1], acc_ref[0], acc_ref[1]), unroll=2)
        carry_ref[0], carry_ref[1] = cr, ci
        acc_ref[0], acc_ref[1] = acc_r, acc_i
        lrb, lib = lr_ref[...].astype(BF16), li_ref[...].astype(BF16)
        du_ref[...] = (lax.dot_general(lrb, bre_ref[0], _NT, preferred_element_type=F32)
                       + lax.dot_general(lib, bim_ref[0], _NT, preferred_element_type=F32) + dy * d_ref[...])
        dbr_ref[0] += lax.dot_general(ub, lrb, _TN, preferred_element_type=F32)
        dbi_ref[0] += lax.dot_general(ub, lib, _TN, preferred_element_type=F32)
        dcr_ref[0] += lax.dot_general(dyb, xr_ref[...].astype(BF16), _TN, preferred_element_type=F32)
        dci_ref[0] += lax.dot_general(dyb, xi_ref[...].astype(BF16), _TN, preferred_element_type=F32)
        dd_ref[...] += jnp.sum(dy * u, axis=0, keepdims=True)

        @pl.when(pl.program_id(1) == nt - 1)
        def _():
            gar_ref[...] = jnp.sum(acc_r, axis=0, keepdims=True)
            gai_ref[...] = jnp.sum(acc_i, axis=0, keepdims=True)

    chan = pl.BlockSpec((tt, ch), lambda b, t: (nt - 1 - t, b))
    state = pl.BlockSpec((tt, st), lambda b, t: (nt - 1 - t, b))
    mat = pl.BlockSpec((1, ch, st), lambda b, t: (b, 0, 0))
    pw = pl.BlockSpec((SUBLANES, st), lambda b, t: (0, b))
    vec = pl.BlockSpec((1, ch), lambda b, t: (0, b))
    svec = pl.BlockSpec((1, st), lambda b, t: (0, b))
    mat_shape = jax.ShapeDtypeStruct((nb, ch, st), F32)
    return pl.pallas_call(
        body, name="ssm_bwd", grid=(nb, nt),
        in_specs=[chan, chan, chan, state, state, mat, mat, mat, mat, pw, pw, vec],
        out_specs=(chan, mat, mat, mat, mat, svec, svec, vec),
        out_shape=(jax.ShapeDtypeStruct((s, nb * ch), F32), mat_shape, mat_shape, mat_shape, mat_shape,
                   jax.ShapeDtypeStruct((1, nb * st), F32), jax.ShapeDtypeStruct((1, nb * st), F32),
                   jax.ShapeDtypeStruct((1, nb * ch), F32)),
        scratch_shapes=[pltpu.VMEM((tt, st), F32), pltpu.VMEM((tt, st), F32),
                        pltpu.VMEM((2, SUBLANES, st), F32), pltpu.VMEM((2, SUBLANES, st), F32)],
        compiler_params=_params(("parallel", "arbitrary")))(
            dz, y, proj, x_re, x_im, bc_re, bc_im, cc_re, cc_im, pw_re, pw_im, dvec)


def _split_dot(x, tri2):
    hi = x.astype(BF16)
    lo = (x - hi.astype(F32)).astype(BF16)
    return jnp.dot(jnp.concatenate([hi, lo], axis=1), tri2, preferred_element_type=F32)


def _tri_and_ones(tk, rel):
    r_i = lax.broadcasted_iota(jnp.int32, (2 * tk, 2 * tk), 0) % tk
    c_i = lax.broadcasted_iota(jnp.int32, (2 * tk, 2 * tk), 1)
    return jnp.where((c_i >= tk) | rel(r_i, c_i), 1.0, 0.0).astype(BF16)


def _attn_scores(qs, kb, t0, s0, masked):
    z = lax.dot_general(qs, kb, (((1,), (1,)), ((), ())), preferred_element_type=F32)
    log_beta = jnp.minimum(z, 0.0) - jnp.log(1.0 + jnp.exp(-jnp.abs(z)))
    log_stay = log_beta - z
    if not masked:
        return log_beta, log_stay, None
    mask = (s0 + lax.broadcasted_iota(jnp.int32, z.shape, 1)) < (t0 + lax.broadcasted_iota(jnp.int32, z.shape, 0))
    return log_beta, jnp.where(mask, log_stay, 0.0), mask


def _attn_blocks(s):
    tq, tk = min(ATT_TQ, s), min(ATT_TK, s)
    assert s % tq == 0 and tq % tk == 0
    return tq, tk, tq // tk


def _attn_fwd(qn, kn, v):
    nh, s, d = qn.shape
    tq, tk, r = _attn_blocks(s)

    def body(q_ref, k_ref, v_ref, o_ref, b_ref):
        qi = pl.program_id(1)
        qs = (q_ref[0].astype(F32) * (HEAD_DIM ** -0.5)).astype(BF16)
        tri = _tri_and_ones(tk, lambda row, col: row > col)

        def tiles(kq, carry, masked):
            o_acc, acc = carry
            ks = pl.ds(pl.multiple_of(kq * tq, tq), tq)
            log_beta, log_stay, mask = _attn_scores(qs, k_ref[0, ks, :], qi * tq, kq * tq, masked)
            sums = [_split_dot(log_stay[:, u * tk:(u + 1) * tk], tri) for u in range(r)]
            later = [None] * r
            for u in reversed(range(r)):
                later[u] = acc + sums[u][:, :tk]
                acc = acc + sums[u][:, tk:]
            w = jnp.exp(log_beta + jnp.concatenate(later, axis=1))
            if masked:
                w = jnp.where(mask, w, 0.0)
            o_acc = o_acc + jnp.dot(w.astype(BF16), v_ref[0, ks, :], preferred_element_type=F32)
            return o_acc, acc

        carry = tiles(qi, (jnp.zeros((tq, d), F32), jnp.zeros((tq, tk), F32)), True)
        o_acc, acc = lax.fori_loop(0, qi, lambda it, c: tiles(qi - 1 - it, c, False), carry)
        o_ref[0] = o_acc
        b_ref[0] = acc[:, 0:1]

    q_spec = pl.BlockSpec((1, tq, d), lambda h, i: (h, i, 0))
    kv_spec = pl.BlockSpec((1, s, d), lambda h, i: (h, 0, 0))
    return pl.pallas_call(
        body, name="attn_fwd", grid=(nh, s // tq), in_specs=[q_spec, kv_spec, kv_spec],
        out_specs=(q_spec, pl.BlockSpec((1, tq, 1), lambda h, i: (h, i, 0))),
        out_shape=(jax.ShapeDtypeStruct((nh, s, d), F32), jax.ShapeDtypeStruct((nh, s, 1), F32)),
        compiler_params=_params(("parallel", "parallel")))(qn, kn, v)


def _attn_bwd(qn, kn, v, bsum, do):
    nh, s, d = qn.shape
    tq, tk, r = _attn_blocks(s)

    def body(q_ref, k_ref, v_ref, b_ref, do_ref, dq_ref, dkt_ref, dvt_ref):
        qi = pl.program_id(1)

        @pl.when(qi == 0)
        def _():
            dkt_ref[...] = jnp.zeros_like(dkt_ref)
            dvt_ref[...] = jnp.zeros_like(dvt_ref)

        scale = HEAD_DIM ** -0.5
        qf = q_ref[0].astype(F32) * scale
        qs, qst = qf.astype(BF16), qf.T.astype(BF16)
        dof = do_ref[0]
        dob, dobt = dof.astype(BF16), dof.T.astype(BF16)
        tri_upto = _tri_and_ones(tk, lambda row, col: row <= col)
        tri_before = _tri_and_ones(tk, lambda row, col: row < col)

        def tiles(kq, carry, masked):
            dq_acc, rest, g_prefix = carry
            ks = pl.ds(pl.multiple_of(kq * tq, tq), tq)
            kb, vb = k_ref[0, ks, :], v_ref[0, ks, :]
            log_beta, log_stay, mask = _attn_scores(qs, kb, qi * tq, kq * tq, masked)
            dw = lax.dot_general(dob, vb, (((1,), (1,)), ((), ())), preferred_element_type=F32)
            sums = [_split_dot(log_stay[:, u * tk:(u + 1) * tk], tri_upto) for u in range(r)]
            later = [None] * r
            for u in range(r):
                later[u] = rest - sums[u][:, :tk]
                rest = rest - sums[u][:, tk:]
            w = jnp.exp(log_beta + jnp.concatenate(later, axis=1))
            if masked:
                w = jnp.where(mask, w, 0.0)
            g = w * dw
            gsums = [_split_dot(g[:, u * tk:(u + 1) * tk], tri_before) for u in range(r)]
            before = [None] * r
            for u in range(r):
                before[u] = g_prefix + gsums[u][:, :tk]
                g_prefix = g_prefix + gsums[u][:, tk:]
            dz = g - jnp.exp(log_beta) * (g + jnp.concatenate(before, axis=1))
            if masked:
                dz = jnp.where(mask, dz, 0.0)
            dzb = dz.astype(BF16)
            dq_acc = dq_acc + jnp.dot(dzb, kb, preferred_element_type=F32)
            dkt_ref[0, :, ks] += jnp.dot(qst, dzb, preferred_element_type=F32)
            dvt_ref[0, :, ks] += jnp.dot(dobt, w.astype(BF16), preferred_element_type=F32)
            return dq_acc, rest, g_prefix

        carry = (jnp.zeros((tq, d), F32), jnp.broadcast_to(b_ref[0], (tq, tk)), jnp.zeros((tq, tk), F32))
        carry = lax.fori_loop(0, qi, lambda kq, c: tiles(kq, c, False), carry)
        dq_acc, _, _ = tiles(qi, carry, True)
        dq_ref[0] = dq_acc * scale

    q_spec = pl.BlockSpec((1, tq, d), lambda h, i: (h, i, 0))
    kv_spec = pl.BlockSpec((1, s, d), lambda h, i: (h, 0, 0))
    t_spec = pl.BlockSpec((1, d, s), lambda h, i: (h, 0, 0))
    b_spec = pl.BlockSpec((1, tq, 1), lambda h, i: (h, i, 0))
    full = jax.ShapeDtypeStruct((nh, s, d), F32)
    full_t = jax.ShapeDtypeStruct((nh, d, s), F32)
    return pl.pallas_call(
        body, name="attn_bwd", grid=(nh, s // tq), in_specs=[q_spec, kv_spec, kv_spec, b_spec, q_spec],
        out_specs=(q_spec, t_spec, t_spec), out_shape=(full, full_t, full_t),
        compiler_params=_params(("parallel", "arbitrary")))(qn, kn, v, bsum, do)


def _reduce_adamw(parts, w, m, v, *, name):
    n, rows, cols = parts.shape
    tile_rows = 2 * SUBLANES
    tr = max(t for t in range(tile_rows, min(rows, 512) + 1, tile_rows) if rows % t == 0)
    c1 = 1.0 - ADAM_B1 ** ADAM_STEP
    c2 = 1.0 - ADAM_B2 ** ADAM_STEP

    def body(p_ref, w_ref, m_ref, v_ref, g_ref, d_ref, nm_ref, nv_ref):
        g = p_ref[0].astype(F32)
        for i in range(1, n):
            g = g + p_ref[i].astype(F32)
        nm = ADAM_B1 * m_ref[...] + (1.0 - ADAM_B1) * g
        nv = ADAM_B2 * v_ref[...] + (1.0 - ADAM_B2) * (g * g)
        g_ref[...] = g
        nm_ref[...] = nm
        nv_ref[...] = nv
        d_ref[...] = -ADAM_LR * ((nm / c1) / (jnp.sqrt(nv / c2) + ADAM_EPS) + ADAM_WD * w_ref[...])

    row = pl.BlockSpec((tr, cols), lambda i: (i, 0))
    out = jax.ShapeDtypeStruct((rows, cols), F32)
    return pl.pallas_call(
        body, name=name, grid=(rows // tr,),
        in_specs=[pl.BlockSpec((n, tr, cols), lambda i: (0, i, 0)), row, row, row],
        out_specs=(row,) * 4, out_shape=(out,) * 4, compiler_params=_params(("parallel",)))(parts, w, m, v)


def _all_gather(x_shard, *, name):
    m_per, n = x_shard.shape

    def body(x_ref, out_ref, send_sems, recv_sems, local_sem):
        x, y, c = lax.axis_index("x"), lax.axis_index("y"), lax.axis_index("c")
        me, sibling = (x, y, c), (x, y, 1 - c)
        chips = [(1 - x, y), (x, 1 - y), (1 - x, 1 - y)]

        def rows(px, py, pc):
            return out_ref.at[pl.ds((4 * px + 2 * py + pc) * m_per, m_per), :]

        def copy(k, block, to, src=None):
            return pltpu.make_async_remote_copy(
                src_ref=rows(*block) if src is None else src, dst_ref=rows(*block),
                send_sem=send_sems.at[k], recv_sem=recv_sems.at[k], device_id=to, device_id_type=MESH_IDS)

        mine = pltpu.make_async_copy(x_ref, rows(*me), local_sem)
        mine.start()
        first = [copy(0, me, sibling, src=x_ref)]
        first += [copy(1 + j, me, (*chip, c), src=x_ref) for j, chip in enumerate(chips)]
        for cp in first:
            cp.start()
        passed = [copy(4 + j, (*chip, c), sibling) for j, chip in enumerate(chips)]
        for j, chip in enumerate(chips):
            copy(1 + j, (*chip, c), me).wait_recv()
            passed[j].start()
        copy(0, sibling, me).wait_recv()
        for j, chip in enumerate(chips):
            copy(4 + j, (*chip, 1 - c), me).wait_recv()
        for cp in first + passed:
            cp.wait_send()
        mine.wait()

    return pl.pallas_call(
        body, name=name, out_shape=jax.ShapeDtypeStruct((N_DEV * m_per, n), x_shard.dtype),
        in_specs=[pl.BlockSpec(memory_space=pl.ANY)], out_specs=pl.BlockSpec(memory_space=pl.ANY),
        scratch_shapes=[pltpu.SemaphoreType.DMA((7,)), pltpu.SemaphoreType.DMA((7,)), pltpu.SemaphoreType.DMA],
    )(x_shard)


def _all_to_all(x, *, name):
    def body(x_ref, out_ref, send_sems, recv_sems, local_sem):
        x, y, c = lax.axis_index("x"), lax.axis_index("y"), lax.axis_index("c")
        me = 4 * x + 2 * y + c
        mine = pltpu.make_async_copy(x_ref.at[me], out_ref.at[me], local_sem)
        mine.start()
        copies = []
        for k in range(1, N_DEV):
            px, py, pc = x ^ (k >> 2), y ^ ((k >> 1) & 1), c ^ (k & 1)
            peer = 4 * px + 2 * py + pc
            copies.append(pltpu.make_async_remote_copy(
                src_ref=x_ref.at[peer], dst_ref=out_ref.at[me], send_sem=send_sems.at[k - 1],
                recv_sem=recv_sems.at[k - 1], device_id=(px, py, pc), device_id_type=MESH_IDS))
        for cp in copies:
            cp.start()
        for k in range(1, N_DEV):
            px, py, pc = x ^ (k >> 2), y ^ ((k >> 1) & 1), c ^ (k & 1)
            peer = 4 * px + 2 * py + pc
            pltpu.make_async_remote_copy(
                src_ref=x_ref.at[peer], dst_ref=out_ref.at[peer], send_sem=send_sems.at[k - 1],
                recv_sem=recv_sems.at[k - 1], device_id=(px, py, pc), device_id_type=MESH_IDS).wait_recv()
        for cp in copies:
            cp.wait_send()
        mine.wait()

    return pl.pallas_call(
        body, name=name, out_shape=jax.ShapeDtypeStruct(x.shape, x.dtype),
        in_specs=[pl.BlockSpec(memory_space=pl.ANY)], out_specs=pl.BlockSpec(memory_space=pl.ANY),
        scratch_shapes=[pltpu.SemaphoreType.DMA((7,)), pltpu.SemaphoreType.DMA((7,)), pltpu.SemaphoreType.DMA],
    )(x)


def _chunks_first(full, axis):
    if axis == 0:
        return full.reshape(N_DEV, -1)
    r, c = full.shape
    return full.reshape(r, N_DEV, c // N_DEV).transpose(1, 0, 2).reshape(N_DEV, -1)


def _from_chunks(chunks, name, shard_shape):
    r, c = shard_shape
    if BIG_SPLIT_AXIS[name] == 0:
        return chunks.reshape(N_DEV * r, c)
    return chunks.reshape(N_DEV, r, c).transpose(1, 0, 2).reshape(r, N_DEV * c)


def _pack_big_shards(shards):
    parts = [shards[n][l].reshape(-1) for l in range(DEPTH) for n in BIG]
    return jnp.concatenate(parts).reshape(-1, PACK_COLS)


def _unpack_big_shards(flat, shapes):
    flat = flat.reshape(-1)
    out = {n: [] for n in BIG}
    off = 0
    for _ in range(DEPTH):
        for n in BIG:
            r, c = shapes[n]
            out[n].append(flat[off:off + r * c].reshape(r, c))
            off += r * c
    return {n: jnp.stack(v) for n, v in out.items()}


def _pack_small(vals):
    flat = jnp.concatenate([vals[n].reshape(-1) for n in SMALL])
    rows = -(-flat.shape[0] // (SUBLANES * PACK_COLS)) * SUBLANES
    return jnp.pad(flat, (0, rows * PACK_COLS - flat.shape[0])).reshape(rows, PACK_COLS)


def _unpack_small(flat, shapes):
    flat = flat.reshape(-1)
    out, off = {}, 0
    for n in SMALL:
        size = math.prod(shapes[n])
        out[n] = flat[off:off + size].reshape(shapes[n])
        off += size
    return out


def _heads(t2d):
    s = t2d.shape[0]
    return t2d.reshape(s, ATTN_HEADS, HEAD_DIM).transpose(1, 0, 2)


def _unheads(t3d):
    return t3d.transpose(1, 0, 2).reshape(t3d.shape[1], ATTN_WIDTH)


def _ssm_setup(sp):
    g, p = SSM_GROUPS, SSM_STATE
    a_re = sp["ssm_a_re"][:, None, :]
    a_im = sp["ssm_a_im"][:, None, :]
    log_dt = jnp.broadcast_to(sp["ssm_log_dt"][:, None, None], (g, 1, p))
    bt_re = sp["ssm_b_re"].transpose(0, 2, 1)
    bt_im = sp["ssm_b_im"].transpose(0, 2, 1)
    return a_re, a_im, log_dt, bt_re, bt_im


def _layer_fwd(h, p_l, w, sp):
    s = h.shape[0]
    hn = _rms_fwd(h, sp["mix_norm_g"][None], name="rms_mix")
    proj = _mm(hn, w["w_in"], name="mm_proj")

    pw_re, pw_im, bbt_re, bbt_im = _ssm_param_fwd(*_ssm_setup(sp))
    pw_re = pw_re.transpose(1, 0, 2).reshape(SUBLANES, SSM_LANES)
    pw_im = pw_im.transpose(1, 0, 2).reshape(SUBLANES, SSM_LANES)
    ssm_mats = (_block_diag(bbt_re).astype(BF16), _block_diag(bbt_im).astype(BF16),
                _block_diag(sp["ssm_c_re"]).astype(BF16), _block_diag(-sp["ssm_c_im"]).astype(BF16),
                pw_re, pw_im, sp["ssm_d"].reshape(1, SSM_WIDTH))
    x_re, x_im, y, z = _ssm_fwd(proj, *ssm_mats)
    zz = _mm(z, w["ssm_w_glu"], name="mm_glu")

    q3 = _heads(proj[:, 2 * SSM_WIDTH:2 * SSM_WIDTH + ATTN_WIDTH]).reshape(ATTN_HEADS * s, HEAD_DIM)
    k3 = _heads(proj[:, 2 * SSM_WIDTH + ATTN_WIDTH:2 * SSM_WIDTH + 2 * ATTN_WIDTH]).reshape(ATTN_HEADS * s, HEAD_DIM)
    v3 = _heads(proj[:, 2 * SSM_WIDTH + 2 * ATTN_WIDTH:2 * SSM_WIDTH + 3 * ATTN_WIDTH]).astype(BF16)
    qn = _rms_fwd(q3, sp["q_norm_g"][None], name="rms_q").reshape(ATTN_HEADS, s, HEAD_DIM)
    kn = _rms_fwd(k3, sp["k_norm_g"][None], name="rms_k").reshape(ATTN_HEADS, s, HEAD_DIM)
    o3, bsum = _attn_fwd(qn, kn, v3)
    o = _unheads(o3)

    bglu = sp["ssm_b_glu"][None]
    ycat = _gate_fwd(zz, bglu, proj, o)
    h2 = _mm(ycat, w["w_out"], add=h, name="mm_out")
    hn2 = _rms_fwd(h2, sp["ple_norm_g"][None], name="rms_ple")
    pgl = _mm(hn2, w["w_ple_gate"], name="mm_ple_gate")
    pp = _mm(p_l, w["w_ple_proj"], name="mm_ple_proj")
    h3 = _ple_out(h2, pgl, pp)
    saved = dict(h=h, hn=hn, proj=proj, ssm_mats=ssm_mats, x_re=x_re, x_im=x_im, y=y, z=z,
                 zz=zz, q3=q3, k3=k3, v3=v3, qn=qn, kn=kn, bsum=bsum, o=o, ycat=ycat, h2=h2, hn2=hn2,
                 pgl=pgl, pp=pp, p_l=p_l, bglu=bglu)
    return h3, saved


def _layer_bwd(dh3, sv, w, sp):
    s = dh3.shape[0]
    g, hh, p = SSM_GROUPS, SSM_GROUP, SSM_STATE
    grads = {}
    dgp, dpp = _ple_bwd(dh3, sv["pgl"], sv["pp"])
    grads["w_ple_gate"] = _mm(sv["hn2"], dgp, ta=True, out_dtype=BF16, name="mm_d_ple_gate")
    grads["w_ple_proj"] = _mm(sv["p_l"], dpp, ta=True, out_dtype=BF16, name="mm_d_ple_proj")
    dhn2 = _mm(dgp, w["w_ple_gate"], tb=True, name="mm_dhn2")
    dh2, dg2 = _rms_bwd(sv["h2"], sp["ple_norm_g"][None], dhn2, dres=dh3, name="rms_ple_bwd")
    grads["ple_norm_g"] = dg2[0]
    grads["w_out"] = _mm(sv["ycat"], dh2, ta=True, out_dtype=BF16, name="mm_d_out")
    dyc = _mm(dh2, w["w_out"], tb=True, name="mm_dycat")
    dzz, dgs, do, dga, dbglu = _gate_bwd(dyc, sv["zz"], sv["bglu"], sv["proj"], sv["o"])
    grads["ssm_b_glu"] = dbglu[0]

    dqn, dkn_t, dv_t = _attn_bwd(sv["qn"], sv["kn"], sv["v3"], sv["bsum"], _heads(do))
    dkn = dkn_t.transpose(0, 2, 1)
    dq3, dgq = _rms_bwd(sv["q3"], sp["q_norm_g"][None], dqn.reshape(ATTN_HEADS * s, HEAD_DIM), name="rms_q_bwd")
    dk3, dgk = _rms_bwd(sv["k3"], sp["k_norm_g"][None], dkn.reshape(ATTN_HEADS * s, HEAD_DIM), name="rms_k_bwd")
    grads["q_norm_g"], grads["k_norm_g"] = dgq[0], dgk[0]
    dq = _unheads(dq3.reshape(ATTN_HEADS, s, HEAD_DIM))
    dk = _unheads(dk3.reshape(ATTN_HEADS, s, HEAD_DIM))
    dv = dv_t.transpose(2, 0, 1).reshape(s, ATTN_WIDTH)

    grads["ssm_w_glu"] = _mm(sv["z"], dzz, ta=True, out_dtype=BF16, name="mm_d_glu")
    dz = _mm(dzz, w["ssm_w_glu"], tb=True, name="mm_dz")
    du, dbc_re, dbc_im, dcc_re, dcc_im, gab_re, gab_im, dd = _ssm_bwd(
        dz, sv["y"], sv["proj"], sv["x_re"], sv["x_im"], *sv["ssm_mats"])
    grads["ssm_d"] = dd.reshape(g, hh)
    grads["ssm_c_re"] = _block_diag_take(dcc_re, hh, p)
    grads["ssm_c_im"] = -_block_diag_take(dcc_im, hh, p)
    da_re, da_im, dldt, dbt_re, dbt_im = _ssm_param_bwd(
        *_ssm_setup(sp), gab_re.reshape(g, 1, p), gab_im.reshape(g, 1, p),
        _block_diag_take(dbc_re, hh, p), _block_diag_take(dbc_im, hh, p))
    grads["ssm_a_re"], grads["ssm_a_im"] = da_re[:, 0, :], da_im[:, 0, :]
    grads["ssm_log_dt"] = dldt[:, 0, 0]
    grads["ssm_b_re"], grads["ssm_b_im"] = dbt_re.transpose(0, 2, 1), dbt_im.transpose(0, 2, 1)

    dproj = jnp.concatenate([du, dgs, dq, dk, dv, dga], axis=1)
    grads["w_in"] = _mm(sv["hn"], dproj, ta=True, out_dtype=BF16, name="mm_d_in")
    dhn = _mm(dproj, w["w_in"], tb=True, name="mm_dhn")
    dh, dg1 = _rms_bwd(sv["h"], sp["mix_norm_g"][None], dhn, dres=dh2, name="rms_mix_bwd")
    grads["mix_norm_g"] = dg1[0]
    return dh, grads


def _local_step(x, p, target, big, small):
    h = x
    saved = []
    for l in range(DEPTH):
        sp = {n: small[n][l] for n in SMALL}
        h, sv = _layer_fwd(h, p[l], big[l], sp)
        saved.append(sv)
    dh, loss_parts = _loss_head(h, target)
    grads = [None] * DEPTH
    for l in reversed(range(DEPTH)):
        sp = {n: small[n][l] for n in SMALL}
        dh, grads[l] = _layer_bwd(dh, saved[l], big[l], sp)
    return jnp.sum(loss_parts), dh, grads


def kernel(x, p, mix_norm_g, w_in, ssm_a_re, ssm_a_im, ssm_log_dt, ssm_b_re, ssm_b_im, ssm_c_re, ssm_c_im, ssm_d, ssm_w_glu, ssm_b_glu, q_norm_g, k_norm_g, w_out, ple_norm_g, w_ple_gate, w_ple_proj, loss_target, m_mix_norm_g, m_w_in, m_ssm_a_re, m_ssm_a_im, m_ssm_log_dt, m_ssm_b_re, m_ssm_b_im, m_ssm_c_re, m_ssm_c_im, m_ssm_d, m_ssm_w_glu, m_ssm_b_glu, m_q_norm_g, m_k_norm_g, m_w_out, m_ple_norm_g, m_w_ple_gate, m_w_ple_proj, v_mix_norm_g, v_w_in, v_ssm_a_re, v_ssm_a_im, v_ssm_log_dt, v_ssm_b_re, v_ssm_b_im, v_ssm_c_re, v_ssm_c_im, v_ssm_d, v_ssm_w_glu, v_ssm_b_glu, v_q_norm_g, v_k_norm_g, v_w_out, v_ple_norm_g, v_w_ple_gate, v_w_ple_proj):
    given = dict(locals())
    wts = {n: given[n] for n in WEIGHTS}
    mom = {n: given["m_" + n] for n in WEIGHTS}
    var = {n: given["v_" + n] for n in WEIGHTS}
    shard_shapes = {n: wts[n].shape[1:] for n in BIG}

    packed_w = _pack_big_shards(wts)
    gathered = _all_gather(packed_w.astype(BF16), name="gather_weights")
    gathered = gathered.reshape(N_DEV, -1)
    big, off = [], 0
    for _ in range(DEPTH):
        layer = {}
        for n in BIG:
            size = math.prod(shard_shapes[n])
            layer[n] = _from_chunks(gathered[:, off:off + size], n, shard_shapes[n])
            off += size
        big.append(layer)

    small = {n: wts[n] for n in SMALL}
    loss, dx, grads = _local_step(x[0], p[:, 0], loss_target[0], big, small)
    loss = lax.psum(loss, ("x", "y", "c"))

    send = jnp.concatenate([_chunks_first(grads[l][n], BIG_SPLIT_AXIS[n]) for l in range(DEPTH) for n in BIG], axis=1)
    rows = send.shape[1] // PACK_COLS
    parts = _all_to_all(send.reshape(N_DEV, rows, PACK_COLS), name="exchange_weight_grads")
    big_out = _reduce_adamw(parts, packed_w, _pack_big_shards(mom), _pack_big_shards(var), name="adamw_sharded")
    big_out = [_unpack_big_shards(a, shard_shapes) for a in big_out]

    small_shapes = {n: wts[n].shape for n in SMALL}
    local_small = _pack_small({n: jnp.stack([grads[l][n] for l in range(DEPTH)]) for n in SMALL})
    all_small = _all_gather(local_small, name="gather_small_grads")
    all_small = all_small.reshape(N_DEV, local_small.shape[0], PACK_COLS)
    small_out = _reduce_adamw(all_small, _pack_small(small), _pack_small({n: mom[n] for n in SMALL}),
                              _pack_small({n: var[n] for n in SMALL}), name="adamw_replicated")
    small_out = [_unpack_small(a, small_shapes) for a in small_out]

    outs = [loss, dx[None]]
    for k in range(4):
        outs += [big_out[k][n] if n in BIG else small_out[k][n] for n in WEIGHTS]
    return tuple(outs)
```

```python
import math

import jax
import jax.numpy as jnp
from jax import lax
from jax.experimental import pallas as pl
from jax.experimental.pallas import tpu as pltpu

F32 = jnp.float32
BF16 = jnp.bfloat16
MESH_IDS = pl.DeviceIdType.MESH

N_DEV = 8
D_MODEL = 1024
DEPTH = 2
PLE_DIM = 256
SSM_WIDTH = 512
SSM_GROUP = 16
SSM_GROUPS = 32
SSM_STATE = 64
SSM_LANES = SSM_GROUPS * SSM_STATE
ATTN_WIDTH = 512
ATTN_HEADS = 8
HEAD_DIM = 64
RMS_EPS = 1e-6
ADAM_LR = 0.001
ADAM_B1 = 0.9
ADAM_B2 = 0.999
ADAM_EPS = 1e-08
ADAM_WD = 0.01
ADAM_STEP = 10

VMEM_LIMIT = 56 * 1024 * 1024
ATT_TQ = 512
ATT_TK = 128
SSM_BLOCK_GROUPS = 8
SSM_TIME_BLOCK = 512
SUBLANES = 8
PACK_COLS = 1024

BIG = ("w_in", "ssm_w_glu", "w_out", "w_ple_gate", "w_ple_proj")
SMALL = ("mix_norm_g", "ssm_a_re", "ssm_a_im", "ssm_log_dt", "ssm_b_re", "ssm_b_im", "ssm_c_re",
         "ssm_c_im", "ssm_d", "ssm_b_glu", "q_norm_g", "k_norm_g", "ple_norm_g")
WEIGHTS = ("mix_norm_g", "w_in", "ssm_a_re", "ssm_a_im", "ssm_log_dt", "ssm_b_re", "ssm_b_im",
           "ssm_c_re", "ssm_c_im", "ssm_d", "ssm_w_glu", "ssm_b_glu", "q_norm_g", "k_norm_g",
           "w_out", "ple_norm_g", "w_ple_gate", "w_ple_proj")
_NT = (((1,), (1,)), ((), ()))
_TN = (((0,), (0,)), ((), ()))


def _params(sem=None):
    return pltpu.CompilerParams(dimension_semantics=sem, vmem_limit_bytes=VMEM_LIMIT)


def _sigmoid(x):
    return 1.0 / (1.0 + jnp.exp(-x))


_GELU_K = math.sqrt(2.0 / math.pi)
_GELU_C = 0.044715


def _gelu(x):
    return 0.5 * x * (1.0 + jnp.tanh(_GELU_K * (x + _GELU_C * x * x * x)))


def _gelu_grad(x):
    th = jnp.tanh(_GELU_K * (x + _GELU_C * x * x * x))
    return 0.5 * (1.0 + th) + 0.5 * x * (1.0 - th * th) * _GELU_K * (1.0 + 3.0 * _GELU_C * x * x)


def _mm_call(a, b, add, a_spec, b_spec, o_spec, grid, out_shape, acc_shape, dims, name):
    n_steps = grid[2]
    out_dtype = out_shape.dtype

    def body(*refs):
        if add is None:
            a_ref, b_ref, o_ref, acc_ref = refs
        else:
            a_ref, b_ref, r_ref, o_ref, acc_ref = refs
        kk = pl.program_id(2)

        @pl.when(kk == 0)
        def _():
            acc_ref[...] = jnp.zeros_like(acc_ref)

        acc_ref[...] += lax.dot_general(a_ref[...].astype(BF16), b_ref[...].astype(BF16), dims,
                                        preferred_element_type=F32)

        @pl.when(kk == n_steps - 1)
        def _():
            res = acc_ref[...]
            if add is not None:
                res = res + r_ref[...].astype(F32)
            o_ref[...] = res.astype(out_dtype)

    ins, specs = [a, b], [a_spec, b_spec]
    if add is not None:
        ins.append(add)
        specs.append(o_spec)
    return pl.pallas_call(
        body, name=name, grid=grid, in_specs=specs, out_specs=o_spec, out_shape=out_shape,
        scratch_shapes=[pltpu.VMEM(acc_shape, F32)],
        compiler_params=_params(("parallel", "parallel", "arbitrary")),
    )(*ins)


def _mm(a, b, *, name, ta=False, tb=False, add=None, out_dtype=F32, tm=1024, tn=1024, tk=512):
    m, k = (a.shape[1], a.shape[0]) if ta else a.shape
    n = b.shape[0] if tb else b.shape[1]
    tm, tn, tk = min(tm, m), min(tn, n), min(tk, k)
    assert m % tm == 0 and n % tn == 0 and k % tk == 0, (name, a.shape, b.shape)
    dims = (((0 if ta else 1,), (1 if tb else 0,)), ((), ()))
    a_spec = (pl.BlockSpec((tk, tm), lambda i, j, kk: (kk, i)) if ta
              else pl.BlockSpec((tm, tk), lambda i, j, kk: (i, kk)))
    b_spec = (pl.BlockSpec((tn, tk), lambda i, j, kk: (j, kk)) if tb
              else pl.BlockSpec((tk, tn), lambda i, j, kk: (kk, j)))
    o_spec = pl.BlockSpec((tm, tn), lambda i, j, kk: (i, j))
    return _mm_call(a, b, add, a_spec, b_spec, o_spec, (m // tm, n // tn, k // tk),
                    jax.ShapeDtypeStruct((m, n), out_dtype), (tm, tn), dims, name)


def _mm_cols(x, wc, row0, k, *, name, tm=1024, tk=512):
    m = x.shape[0]
    c = wc.shape[2]
    tm, tk = min(tm, m), min(tk, k)
    assert m % tm == 0 and k % tk == 0 and row0 % tk == 0
    return _mm_call(
        x, wc, None, pl.BlockSpec((tm, tk), lambda i, j, kk: (i, kk)),
        pl.BlockSpec((None, tk, c), lambda i, j, kk: (j, row0 // tk + kk, 0)),
        pl.BlockSpec((tm, c), lambda i, j, kk: (i, j)), (m // tm, N_DEV, k // tk),
        jax.ShapeDtypeStruct((m, N_DEV * c), F32), (tm, c), (((1,), (0,)), ((), ())), name)


def _mm_cols_t(g, wc, row0, k, *, name, tm=1024, tn=1024):
    m = g.shape[0]
    c = wc.shape[2]
    tm, tn = min(tm, m), min(tn, k)
    assert m % tm == 0 and k % tn == 0 and row0 % tn == 0
    return _mm_call(
        g, wc, None, pl.BlockSpec((tm, c), lambda i, n, j: (i, j)),
        pl.BlockSpec((None, tn, c), lambda i, n, j: (j, row0 // tn + n, 0)),
        pl.BlockSpec((tm, tn), lambda i, n, j: (i, n)), (m // tm, k // tn, N_DEV),
        jax.ShapeDtypeStruct((m, k), F32), (tm, tn), _NT, name)


def _mm_cols_grad(x, g, *, name, tm=1024, tk=512):
    s, k = x.shape
    c = g.shape[1] // N_DEV
    tm, tk = min(tm, k), min(tk, s)
    assert k % tm == 0 and s % tk == 0
    return _mm_call(
        x, g, None, pl.BlockSpec((tk, tm), lambda j, i, kk: (kk, i)),
        pl.BlockSpec((tk, c), lambda j, i, kk: (kk, j)),
        pl.BlockSpec((None, tm, c), lambda j, i, kk: (j, i, 0)), (N_DEV, k // tm, s // tk),
        jax.ShapeDtypeStruct((N_DEV, k, c), BF16), (tm, c), _TN, name)


def _rows_tile(rows, want=512):
    t = min(rows, want)
    assert rows % t == 0
    return t


def _rowwise(body, name, rows, ins, outs):
    widest = max(cols for _, cols, _ in ins)
    tm = _rows_tile(rows, 512 if widest > 128 else 4096)
    in_specs = []
    for _, cols, cb in ins:
        if cb is None:
            in_specs.append(pl.BlockSpec((1, cols), lambda i: (0, 0)))
        else:
            in_specs.append(pl.BlockSpec((tm, cols), lambda i, cb=cb: (i, cb)))
    out_specs, out_shapes = [], []
    for cols, dtype, is_acc in outs:
        if is_acc:
            out_specs.append(pl.BlockSpec((1, cols), lambda i: (0, 0)))
            out_shapes.append(jax.ShapeDtypeStruct((1, cols), dtype))
        else:
            out_specs.append(pl.BlockSpec((tm, cols), lambda i: (i, 0)))
            out_shapes.append(jax.ShapeDtypeStruct((rows, cols), dtype))
    any_acc = any(o[2] for o in outs)
    return pl.pallas_call(
        body, name=name, grid=(rows // tm,), in_specs=in_specs, out_specs=tuple(out_specs),
        out_shape=tuple(out_shapes),
        compiler_params=_params(("arbitrary",) if any_acc else ("parallel",)))(*[a for a, _, _ in ins])


def _accumulate(ref, value):
    @pl.when(pl.program_id(0) == 0)
    def _():
        ref[...] = jnp.zeros_like(ref)

    ref[...] += value


def _rms_fwd(h, g, *, name):
    rows, d = h.shape

    def body(h_ref, g_ref, o_ref):
        hv = h_ref[...]
        r = lax.rsqrt(jnp.mean(hv * hv, axis=-1, keepdims=True) + RMS_EPS)
        o_ref[...] = (hv * r * g_ref[...]).astype(BF16)

    return _rowwise(body, name, rows, [(h, d, 0), (g, d, None)], [(d, BF16, False)])[0]


def _rms_bwd(h, g, dhn, *, name, dres=None):
    rows, d = h.shape

    def body(*refs):
        if dres is None:
            h_ref, g_ref, dn_ref, dh_ref, dg_ref = refs
        else:
            h_ref, g_ref, dn_ref, dr_ref, dh_ref, dg_ref = refs
        hv = h_ref[...]
        dn = dn_ref[...].astype(F32)
        r = lax.rsqrt(jnp.mean(hv * hv, axis=-1, keepdims=True) + RMS_EPS)
        a = dn * g_ref[...]
        dot = jnp.mean(a * hv, axis=-1, keepdims=True)
        dh = r * a - hv * (r * r * r * dot)
        if dres is not None:
            dh = dh + dr_ref[...]
        dh_ref[...] = dh
        _accumulate(dg_ref, jnp.sum(dn * hv * r, axis=0, keepdims=True))

    ins = [(h, d, 0), (g, d, None), (dhn, d, 0)]
    if dres is not None:
        ins.append((dres, d, 0))
    return _rowwise(body, name, rows, ins, [(d, F32, False), (d, F32, True)])


def _gate_fwd(zz, bglu, proj, o):
    rows = zz.shape[0]
    w = SSM_WIDTH

    def body(zz_ref, b_ref, gs_ref, o_ref, ga_ref, y_ref):
        zz_v = zz_ref[...] + b_ref[...]
        val, gate = zz_v[:, :w], zz_v[:, w:]
        gs, ga = gs_ref[...], ga_ref[...]
        y_ref[:, :w] = (val * _sigmoid(gate) * (gs * _sigmoid(gs))).astype(BF16)
        y_ref[:, w:] = (o_ref[...] * (ga * _sigmoid(ga))).astype(BF16)

    return _rowwise(body, "gate_fwd", rows,
                    [(zz, 2 * w, 0), (bglu, 2 * w, None), (proj, w, 1), (o, w, 0), (proj, w, 5)],
                    [(2 * w, BF16, False)])[0]


def _gate_bwd(dyc, zz, bglu, proj, o):
    rows = zz.shape[0]
    w = SSM_WIDTH

    def body(dy_ref, zz_ref, b_ref, gs_ref, o_ref, ga_ref, dzz_ref, dgs_ref, do_ref, dga_ref, db_ref):
        zz_v = zz_ref[...] + b_ref[...]
        val, gate = zz_v[:, :w], zz_v[:, w:]
        gs, ga = gs_ref[...], ga_ref[...]
        dys, dya = dy_ref[:, :w], dy_ref[:, w:]
        sg, ss, sa = _sigmoid(gate), _sigmoid(gs), _sigmoid(ga)
        glu = val * sg
        dglu = dys * (gs * ss)
        dgs_ref[...] = dys * glu * (ss * (1.0 + gs * (1.0 - ss)))
        dval = dglu * sg
        dgate = dglu * val * sg * (1.0 - sg)
        dzz_ref[:, :w] = dval.astype(BF16)
        dzz_ref[:, w:] = dgate.astype(BF16)
        do_ref[...] = dya * (ga * sa)
        dga_ref[...] = dya * o_ref[...] * (sa * (1.0 + ga * (1.0 - sa)))
        _accumulate(db_ref, jnp.concatenate([jnp.sum(dval, axis=0, keepdims=True),
                                             jnp.sum(dgate, axis=0, keepdims=True)], axis=1))

    return _rowwise(body, "gate_bwd", rows,
                    [(dyc, 2 * w, 0), (zz, 2 * w, 0), (bglu, 2 * w, None), (proj, w, 1), (o, w, 0), (proj, w, 5)],
                    [(2 * w, BF16, False), (w, F32, False), (w, F32, False), (w, F32, False), (2 * w, F32, True)])


def _ple_out(h2, pgl, pp):
    rows, d = h2.shape

    def body(h_ref, g_ref, p_ref, o_ref):
        o_ref[...] = h_ref[...] + _sigmoid(g_ref[...]) * p_ref[...]

    return _rowwise(body, "ple_out", rows, [(h2, d, 0), (pgl, d, 0), (pp, d, 0)], [(d, F32, False)])[0]


def _ple_bwd(dh3, pgl, pp):
    rows, d = dh3.shape

    def body(dh_ref, g_ref, p_ref, dg_ref, dp_ref):
        sg = _sigmoid(g_ref[...])
        dh = dh_ref[...]
        dg_ref[...] = (dh * p_ref[...] * sg * (1.0 - sg)).astype(BF16)
        dp_ref[...] = (dh * sg).astype(BF16)

    return _rowwise(body, "ple_bwd", rows, [(dh3, d, 0), (pgl, d, 0), (pp, d, 0)],
                    [(d, BF16, False), (d, BF16, False)])


def _loss_head(h, target):
    rows, d = h.shape

    def body(h_ref, t_ref, dh_ref, l_ref):
        err = h_ref[...] - t_ref[...]
        dh_ref[...] = err * (1.0 / d)
        _accumulate(l_ref, jnp.sum(err * err, axis=0, keepdims=True) * (0.5 / d))

    return _rowwise(body, "loss_head", rows, [(h, d, 0), (target, d, 0)], [(d, F32, False), (d, F32, True)])


def _zoh(lr, li, dt):
    mag = jnp.exp(lr * dt)
    ab_re = mag * jnp.cos(li * dt)
    ab_im = mag * jnp.sin(li * dt)
    num_re = ab_re - 1.0
    den = lr * lr + li * li
    f_re = (num_re * lr + ab_im * li) / den
    f_im = (ab_im * lr - num_re * li) / den
    return ab_re, ab_im, f_re, f_im, den


def _ssm_param_fwd(a_re, a_im, log_dt, bt_re, bt_im):
    g, _, p = a_re.shape
    h = bt_re.shape[1]

    def body(lr_ref, li_ref, ldt_ref, br_ref, bi_ref, pr_ref, pi_ref, bbr_ref, bbi_ref):
        dt = jnp.exp(ldt_ref[...])
        ab_re, ab_im, f_re, f_im, _ = _zoh(lr_ref[...], li_ref[...], dt)
        cr, ci = ab_re, ab_im
        for k in range(SUBLANES):
            pr_ref[:, k:k + 1, :] = cr
            pi_ref[:, k:k + 1, :] = ci
            cr, ci = cr * ab_re - ci * ab_im, cr * ab_im + ci * ab_re
        br, bi = br_ref[...], bi_ref[...]
        bbr_ref[...] = f_re * br - f_im * bi
        bbi_ref[...] = f_re * bi + f_im * br

    pw = jax.ShapeDtypeStruct((g, SUBLANES, p), F32)
    large = jax.ShapeDtypeStruct((g, h, p), F32)
    return pl.pallas_call(body, name="ssm_param_fwd", out_shape=(pw, pw, large, large),
                          compiler_params=_params())(a_re, a_im, log_dt, bt_re, bt_im)


def _ssm_param_bwd(a_re, a_im, log_dt, bt_re, bt_im, gab_re, gab_im, gbb_re, gbb_im):
    g, _, p = a_re.shape
    h = bt_re.shape[1]

    def body(lr_ref, li_ref, ldt_ref, br_ref, bi_ref, gar_ref, gai_ref, gbr_ref, gbi_ref,
             dlr_ref, dli_ref, dldt_ref, dbr_ref, dbi_ref):
        lr, li = lr_ref[...], li_ref[...]
        dt = jnp.exp(ldt_ref[...])
        ab_re, ab_im, f_re, f_im, den = _zoh(lr, li, dt)
        br, bi = br_ref[...], bi_ref[...]
        gbr, gbi = gbr_ref[...], gbi_ref[...]
        dbr_ref[...] = f_re * gbr + f_im * gbi
        dbi_ref[...] = f_re * gbi - f_im * gbr
        gf_re = jnp.sum(br * gbr + bi * gbi, axis=1, keepdims=True)
        gf_im = jnp.sum(br * gbi - bi * gbr, axis=1, keepdims=True)
        il_re, il_im = lr / den, -li / den
        ga_re = gar_ref[...] + il_re * gf_re + il_im * gf_im
        ga_im = gai_ref[...] + il_re * gf_im - il_im * gf_re
        q_re = f_re * il_re - f_im * il_im
        q_im = f_re * il_im + f_im * il_re
        gl_re = -(q_re * gf_re + q_im * gf_im)
        gl_im = -(q_re * gf_im - q_im * gf_re)
        dlr_ref[...] = gl_re + dt * (ab_re * ga_re + ab_im * ga_im)
        dli_ref[...] = gl_im + dt * (ab_re * ga_im - ab_im * ga_re)
        la_re = lr * ab_re - li * ab_im
        la_im = lr * ab_im + li * ab_re
        dldt_ref[...] = jnp.sum((la_re * ga_re + la_im * ga_im) * dt, axis=2, keepdims=True)

    small = jax.ShapeDtypeStruct((g, 1, p), F32)
    one = jax.ShapeDtypeStruct((g, 1, 1), F32)
    large = jax.ShapeDtypeStruct((g, h, p), F32)
    return pl.pallas_call(body, name="ssm_param_bwd", out_shape=(small, small, one, large, large),
                          compiler_params=_params())(
                              a_re, a_im, log_dt, bt_re, bt_im, gab_re, gab_im, gbb_re, gbb_im)


def _block_diag(m):
    g, h, p = m.shape
    nb = g // SSM_BLOCK_GROUPS
    eye = jnp.eye(SSM_BLOCK_GROUPS, dtype=m.dtype)
    m4 = m.reshape(nb, SSM_BLOCK_GROUPS, h, p)
    return (m4[:, :, :, None, :] * eye[None, :, None, :, None]).reshape(nb, SSM_BLOCK_GROUPS * h, SSM_BLOCK_GROUPS * p)


def _block_diag_take(m, h, p):
    nb = m.shape[0]
    eye = jnp.eye(SSM_BLOCK_GROUPS, dtype=m.dtype)
    m5 = m.reshape(nb, SSM_BLOCK_GROUPS, h, SSM_BLOCK_GROUPS, p)
    return jnp.sum(m5 * eye[None, :, None, :, None], axis=3).reshape(nb * SSM_BLOCK_GROUPS, h, p)


def _scan_tile(br, bi, pw_re, pw_im, reverse):
    row = lax.broadcasted_iota(jnp.int32, br.shape, 0)
    xr, xi = br, bi
    for d in (1, 2, 4):
        ar, ai = pw_re[d - 1:d, :], pw_im[d - 1:d, :]
        if reverse:
            keep = row < SUBLANES - d
            sr = jnp.where(keep, pltpu.roll(xr, SUBLANES - d, 0), 0.0)
            si = jnp.where(keep, pltpu.roll(xi, SUBLANES - d, 0), 0.0)
        else:
            keep = row >= d
            sr = jnp.where(keep, pltpu.roll(xr, d, 0), 0.0)
            si = jnp.where(keep, pltpu.roll(xi, d, 0), 0.0)
        xr, xi = xr + ar * sr - ai * si, xi + ar * si + ai * sr
    return xr, xi


def _ssm_blocks(s):
    tt = min(SSM_TIME_BLOCK, s)
    assert s % tt == 0 and tt % SUBLANES == 0
    ch = SSM_BLOCK_GROUPS * SSM_GROUP
    st = SSM_BLOCK_GROUPS * SSM_STATE
    return tt, s // tt, SSM_WIDTH // ch, ch, st


def _ssm_fwd(proj, bc_re, bc_im, cc_re, cc_im, pw_re, pw_im, dvec):
    s = proj.shape[0]
    tt, nt, nb, ch, st = _ssm_blocks(s)
    n_tiles = tt // SUBLANES

    def body(u_ref, bre_ref, bim_ref, cre_ref, cim_ref, pr_ref, pi_ref, d_ref,
             xr_ref, xi_ref, y_ref, z_ref, carry_ref):
        @pl.when(pl.program_id(1) == 0)
        def _():
            carry_ref[...] = jnp.zeros_like(carry_ref)

        u = u_ref[...]
        ub = u.astype(BF16)
        xr_ref[...] = jnp.dot(ub, bre_ref[0], preferred_element_type=F32)
        xi_ref[...] = jnp.dot(ub, bim_ref[0], preferred_element_type=F32)
        pw_r, pw_i = pr_ref[...], pi_ref[...]

        def step(t, carry):
            cr, ci = carry
            rows = pl.ds(pl.multiple_of(t * SUBLANES, SUBLANES), SUBLANES)
            xr, xi = _scan_tile(xr_ref[rows, :], xi_ref[rows, :], pw_r, pw_i, False)
            xr, xi = xr + pw_r * cr - pw_i * ci, xi + pw_r * ci + pw_i * cr
            xr_ref[rows, :] = xr
            xi_ref[rows, :] = xi
            return (jnp.broadcast_to(xr[SUBLANES - 1:SUBLANES, :], (SUBLANES, st)),
                    jnp.broadcast_to(xi[SUBLANES - 1:SUBLANES, :], (SUBLANES, st)))

        cr, ci = lax.fori_loop(0, n_tiles, step, (carry_ref[0], carry_ref[1]), unroll=2)
        carry_ref[0] = cr
        carry_ref[1] = ci
        y = (lax.dot_general(xr_ref[...].astype(BF16), cre_ref[0], _NT, preferred_element_type=F32)
             + lax.dot_general(xi_ref[...].astype(BF16), cim_ref[0], _NT, preferred_element_type=F32)
             + d_ref[...] * u)
        y_ref[...] = y
        z_ref[...] = _gelu(y).astype(BF16)

    chan = pl.BlockSpec((tt, ch), lambda b, t: (t, b))
    state = pl.BlockSpec((tt, st), lambda b, t: (t, b))
    mat = pl.BlockSpec((1, ch, st), lambda b, t: (b, 0, 0))
    pw = pl.BlockSpec((SUBLANES, st), lambda b, t: (0, b))
    vec = pl.BlockSpec((1, ch), lambda b, t: (0, b))
    x_shape = jax.ShapeDtypeStruct((s, nb * st), F32)
    return pl.pallas_call(
        body, name="ssm_fwd", grid=(nb, nt), in_specs=[chan, mat, mat, mat, mat, pw, pw, vec],
        out_specs=(state, state, chan, chan),
        out_shape=(x_shape, x_shape, jax.ShapeDtypeStruct((s, nb * ch), F32), jax.ShapeDtypeStruct((s, nb * ch), BF16)),
        scratch_shapes=[pltpu.VMEM((2, SUBLANES, st), F32)],
        compiler_params=_params(("parallel", "arbitrary")))(proj, bc_re, bc_im, cc_re, cc_im, pw_re, pw_im, dvec)


def _ssm_bwd(dz, y, proj, x_re, x_im, bc_re, bc_im, cc_re, cc_im, pw_re, pw_im, dvec):
    s = proj.shape[0]
    tt, nt, nb, ch, st = _ssm_blocks(s)
    n_tiles = tt // SUBLANES

    def body(dz_ref, y_ref, u_ref, xr_ref, xi_ref, bre_ref, bim_ref, cre_ref, cim_ref, pr_ref, pi_ref, d_ref,
             du_ref, dbr_ref, dbi_ref, dcr_ref, dci_ref, gar_ref, gai_ref, dd_ref,
             lr_ref, li_ref, carry_ref, acc_ref):
        @pl.when(pl.program_id(1) == 0)
        def _():
            for ref in (carry_ref, acc_ref, dbr_ref, dbi_ref, dcr_ref, dci_ref, dd_ref):
                ref[...] = jnp.zeros_like(ref)

        u = u_ref[...]
        dy = dz_ref[...] * _gelu_grad(y_ref[...])
        ub, dyb = u.astype(BF16), dy.astype(BF16)
        lr_ref[...] = jnp.dot(dyb, cre_ref[0], preferred_element_type=F32)
        li_ref[...] = jnp.dot(dyb, cim_ref[0], preferred_element_type=F32)
        pw_r, pw_i = pr_ref[...], -pi_ref[...]
        pwc_r = jnp.concatenate([pw_r[SUBLANES - 1 - i:SUBLANES - i, :] for i in range(SUBLANES)], axis=0)
        pwc_i = jnp.concatenate([pw_i[SUBLANES - 1 - i:SUBLANES - i, :] for i in range(SUBLANES)], axis=0)
        row = lax.broadcasted_iota(jnp.int32, (SUBLANES, st), 0)

        def step(t, carry):
            cr, ci, acc_r, acc_i = carry
            rows = pl.ds(pl.multiple_of((n_tiles - 1 - t) * SUBLANES, SUBLANES), SUBLANES)
            lr, li = _scan_tile(lr_ref[rows, :], li_ref[rows, :], pw_r, pw_i, True)
            lr, li = lr + pwc_r * cr - pwc_i * ci, li + pwc_r * ci + pwc_i * cr
            lr_ref[rows, :] = lr
            li_ref[rows, :] = li
            nr = jnp.where(row < SUBLANES - 1, pltpu.roll(lr, SUBLANES - 1, 0), cr)
            ni = jnp.where(row < SUBLANES - 1, pltpu.roll(li, SUBLANES - 1, 0), ci)
            xr, xi = xr_ref[rows, :], xi_ref[rows, :]
            return (jnp.broadcast_to(lr[0:1, :], (SUBLANES, st)), jnp.broadcast_to(li[0:1, :], (SUBLANES, st)),
                    acc_r + xr * nr + xi * ni, acc_i + xr * ni - xi * nr)

        cr, ci, acc_r, acc_i = lax.fori_loop(
            0, n_tiles, step, (carry_ref[0], carry_ref[1], acc_ref[0], acc_ref[1]), unroll=2)
        carry_ref[0], carry_ref[1] = cr, ci
        acc_ref[0], acc_ref[1] = acc_r, acc_i
        lrb, lib = lr_ref[...].astype(BF16), li_ref[...].astype(BF16)
        du_ref[...] = (lax.dot_general(lrb, bre_ref[0], _NT, preferred_element_type=F32)
                       + lax.dot_general(lib, bim_ref[0], _NT, preferred_element_type=F32) + dy * d_ref[...])
        dbr_ref[0] += lax.dot_general(ub, lrb, _TN, preferred_element_type=F32)
        dbi_ref[0] += lax.dot_general(ub, lib, _TN, preferred_element_type=F32)
        dcr_ref[0] += lax.dot_general(dyb, xr_ref[...].astype(BF16), _TN, preferred_element_type=F32)
        dci_ref[0] += lax.dot_general(dyb, xi_ref[...].astype(BF16), _TN, preferred_element_type=F32)
        dd_ref[...] += jnp.sum(dy * u, axis=0, keepdims=True)

        @pl.when(pl.program_id(1) == nt - 1)
        def _():
            gar_ref[...] = jnp.sum(acc_r, axis=0, keepdims=True)
            gai_ref[...] = jnp.sum(acc_i, axis=0, keepdims=True)

    chan = pl.BlockSpec((tt, ch), lambda b, t: (nt - 1 - t, b))
    state = pl.BlockSpec((tt, st), lambda b, t: (nt - 1 - t, b))
    mat = pl.BlockSpec((1, ch, st), lambda b, t: (b, 0, 0))
    pw = pl.BlockSpec((SUBLANES, st), lambda b, t: (0, b))
    vec = pl.BlockSpec((1, ch), lambda b, t: (0, b))
    svec = pl.BlockSpec((1, st), lambda b, t: (0, b))
    mat_shape = jax.ShapeDtypeStruct((nb, ch, st), F32)
    return pl.pallas_call(
        body, name="ssm_bwd", grid=(nb, nt),
        in_specs=[chan, chan, chan, state, state, mat, mat, mat, mat, pw, pw, vec],
        out_specs=(chan, mat, mat, mat, mat, svec, svec, vec),
        out_shape=(jax.ShapeDtypeStruct((s, nb * ch), F32), mat_shape, mat_shape, mat_shape, mat_shape,
                   jax.ShapeDtypeStruct((1, nb * st), F32), jax.ShapeDtypeStruct((1, nb * st), F32),
                   jax.ShapeDtypeStruct((1, nb * ch), F32)),
        scratch_shapes=[pltpu.VMEM((tt, st), F32), pltpu.VMEM((tt, st), F32),
                        pltpu.VMEM((2, SUBLANES, st), F32), pltpu.VMEM((2, SUBLANES, st), F32)],
        compiler_params=_params(("parallel", "arbitrary")))(
            dz, y, proj, x_re, x_im, bc_re, bc_im, cc_re, cc_im, pw_re, pw_im, dvec)


def _split_dot(x, tri2):
    hi = x.astype(BF16)
    lo = (x - hi.astype(F32)).astype(BF16)
    return jnp.dot(jnp.concatenate([hi, lo], axis=1), tri2, preferred_element_type=F32)


def _tri_and_ones(tk, rel):
    r_i = lax.broadcasted_iota(jnp.int32, (2 * tk, 2 * tk), 0) % tk
    c_i = lax.broadcasted_iota(jnp.int32, (2 * tk, 2 * tk), 1)
    return jnp.where((c_i >= tk) | rel(r_i, c_i), 1.0, 0.0).astype(BF16)


def _attn_scores(qs, kb, t0, s0, masked):
    z = lax.dot_general(qs, kb, (((1,), (1,)), ((), ())), preferred_element_type=F32)
    log_beta = jnp.minimum(z, 0.0) - jnp.log(1.0 + jnp.exp(-jnp.abs(z)))
    log_stay = log_beta - z
    if not masked:
        return log_beta, log_stay, None
    mask = (s0 + lax.broadcasted_iota(jnp.int32, z.shape, 1)) < (t0 + lax.broadcasted_iota(jnp.int32, z.shape, 0))
    return log_beta, jnp.where(mask, log_stay, 0.0), mask


def _attn_blocks(s):
    tq, tk = min(ATT_TQ, s), min(ATT_TK, s)
    assert s % tq == 0 and tq % tk == 0
    return tq, tk, tq // tk


def _attn_fwd(qn, kn, v):
    nh, s, d = qn.shape
    tq, tk, r = _attn_blocks(s)

    def body(q_ref, k_ref, v_ref, o_ref, b_ref):
        qi = pl.program_id(1)
        qs = (q_ref[0].astype(F32) * (HEAD_DIM ** -0.5)).astype(BF16)
        tri = _tri_and_ones(tk, lambda row, col: row > col)

        def tiles(kq, carry, masked):
            o_acc, acc = carry
            ks = pl.ds(pl.multiple_of(kq * tq, tq), tq)
            log_beta, log_stay, mask = _attn_scores(qs, k_ref[0, ks, :], qi * tq, kq * tq, masked)
            sums = [_split_dot(log_stay[:, u * tk:(u + 1) * tk], tri) for u in range(r)]
            later = [None] * r
            for u in reversed(range(r)):
                later[u] = acc + sums[u][:, :tk]
                acc = acc + sums[u][:, tk:]
            w = jnp.exp(log_beta + jnp.concatenate(later, axis=1))
            if masked:
                w = jnp.where(mask, w, 0.0)
            o_acc = o_acc + jnp.dot(w.astype(BF16), v_ref[0, ks, :], preferred_element_type=F32)
            return o_acc, acc

        carry = tiles(qi, (jnp.zeros((tq, d), F32), jnp.zeros((tq, tk), F32)), True)
        o_acc, acc = lax.fori_loop(0, qi, lambda it, c: tiles(qi - 1 - it, c, False), carry)
        o_ref[0] = o_acc
        b_ref[0] = acc[:, 0:1]

    q_spec = pl.BlockSpec((1, tq, d), lambda h, i: (h, i, 0))
    kv_spec = pl.BlockSpec((1, s, d), lambda h, i: (h, 0, 0))
    return pl.pallas_call(
        body, name="attn_fwd", grid=(nh, s // tq), in_specs=[q_spec, kv_spec, kv_spec],
        out_specs=(q_spec, pl.BlockSpec((1, tq, 1), lambda h, i: (h, i, 0))),
        out_shape=(jax.ShapeDtypeStruct((nh, s, d), F32), jax.ShapeDtypeStruct((nh, s, 1), F32)),
        compiler_params=_params(("parallel", "parallel")))(qn, kn, v)


def _attn_bwd(qn, kn, v, bsum, do):
    nh, s, d = qn.shape
    tq, tk, r = _attn_blocks(s)

    def body(q_ref, k_ref, v_ref, b_ref, do_ref, dq_ref, dkt_ref, dvt_ref):
        qi = pl.program_id(1)

        @pl.when(qi == 0)
        def _():
            dkt_ref[...] = jnp.zeros_like(dkt_ref)
            dvt_ref[...] = jnp.zeros_like(dvt_ref)

        scale = HEAD_DIM ** -0.5
        qf = q_ref[0].astype(F32) * scale
        qs, qst = qf.astype(BF16), qf.T.astype(BF16)
        dof = do_ref[0]
        dob, dobt = dof.astype(BF16), dof.T.astype(BF16)
        tri_upto = _tri_and_ones(tk, lambda row, col: row <= col)
        tri_before = _tri_and_ones(tk, lambda row, col: row < col)

        def tiles(kq, carry, masked):
            dq_acc, rest, g_prefix = carry
            ks = pl.ds(pl.multiple_of(kq * tq, tq), tq)
            kb, vb = k_ref[0, ks, :], v_ref[0, ks, :]
            log_beta, log_stay, mask = _attn_scores(qs, kb, qi * tq, kq * tq, masked)
            dw = lax.dot_general(dob, vb, (((1,), (1,)), ((), ())), preferred_element_type=F32)
            sums = [_split_dot(log_stay[:, u * tk:(u + 1) * tk], tri_upto) for u in range(r)]
            later = [None] * r
            for u in range(r):
                later[u] = rest - sums[u][:, :tk]
                rest = rest - sums[u][:, tk:]
            w = jnp.exp(log_beta + jnp.concatenate(later, axis=1))
            if masked:
                w = jnp.where(mask, w, 0.0)
            g = w * dw
            gsums = [_split_dot(g[:, u * tk:(u + 1) * tk], tri_before) for u in range(r)]
            before = [None] * r
            for u in range(r):
                before[u] = g_prefix + gsums[u][:, :tk]
                g_prefix = g_prefix + gsums[u][:, tk:]
            dz = g - jnp.exp(log_beta) * (g + jnp.concatenate(before, axis=1))
            if masked:
                dz = jnp.where(mask, dz, 0.0)
            dzb = dz.astype(BF16)
            dq_acc = dq_acc + jnp.dot(dzb, kb, preferred_element_type=F32)
            dkt_ref[0, :, ks] += jnp.dot(qst, dzb, preferred_element_type=F32)
            dvt_ref[0, :, ks] += jnp.dot(dobt, w.astype(BF16), preferred_element_type=F32)
            return dq_acc, rest, g_prefix

        carry = (jnp.zeros((tq, d), F32), jnp.broadcast_to(b_ref[0], (tq, tk)), jnp.zeros((tq, tk), F32))
        carry = lax.fori_loop(0, qi, lambda kq, c: tiles(kq, c, False), carry)
        dq_acc, _, _ = tiles(qi, carry, True)
        dq_ref[0] = dq_acc * scale

    q_spec = pl.BlockSpec((1, tq, d), lambda h, i: (h, i, 0))
    kv_spec = pl.BlockSpec((1, s, d), lambda h, i: (h, 0, 0))
    t_spec = pl.BlockSpec((1, d, s), lambda h, i: (h, 0, 0))
    b_spec = pl.BlockSpec((1, tq, 1), lambda h, i: (h, i, 0))
    full = jax.ShapeDtypeStruct((nh, s, d), F32)
    full_t = jax.ShapeDtypeStruct((nh, d, s), F32)
    return pl.pallas_call(
        body, name="attn_bwd", grid=(nh, s // tq), in_specs=[q_spec, kv_spec, kv_spec, b_spec, q_spec],
        out_specs=(q_spec, t_spec, t_spec), out_shape=(full, full_t, full_t),
        compiler_params=_params(("parallel", "arbitrary")))(qn, kn, v, bsum, do)


def _reduce_adamw(parts, w, m, v, *, name, row0=0):
    n, _, cols = parts.shape
    rows = w.shape[0]
    tile_rows = 2 * SUBLANES
    tr = max(t for t in range(tile_rows, min(rows, 512) + 1, tile_rows) if rows % t == 0 and row0 % t == 0)
    c1 = 1.0 - ADAM_B1 ** ADAM_STEP
    c2 = 1.0 - ADAM_B2 ** ADAM_STEP

    def body(p_ref, w_ref, m_ref, v_ref, g_ref, d_ref, nm_ref, nv_ref):
        g = p_ref[0].astype(F32)
        for i in range(1, n):
            g = g + p_ref[i].astype(F32)
        nm = ADAM_B1 * m_ref[...] + (1.0 - ADAM_B1) * g
        nv = ADAM_B2 * v_ref[...] + (1.0 - ADAM_B2) * (g * g)
        g_ref[...] = g
        nm_ref[...] = nm
        nv_ref[...] = nv
        d_ref[...] = -ADAM_LR * ((nm / c1) / (jnp.sqrt(nv / c2) + ADAM_EPS) + ADAM_WD * w_ref[...])

    row = pl.BlockSpec((tr, cols), lambda i: (i, 0))
    out = jax.ShapeDtypeStruct((rows, cols), F32)
    return pl.pallas_call(
        body, name=name, grid=(rows // tr,),
        in_specs=[pl.BlockSpec((n, tr, cols), lambda i: (0, row0 // tr + i, 0)), row, row, row],
        out_specs=(row,) * 4, out_shape=(out,) * 4, compiler_params=_params(("parallel",)))(parts, w, m, v)


def _all_gather(shards, *, name):
    n_arr = len(shards)

    def body(*refs):
        x_refs, out_refs = refs[:n_arr], refs[n_arr:2 * n_arr]
        send_sems, recv_sems, local_sems = refs[2 * n_arr:]
        x, y, c = lax.axis_index("x"), lax.axis_index("y"), lax.axis_index("c")
        me, sibling = (x, y, c), (x, y, 1 - c)
        chips = [(1 - x, y), (x, 1 - y), (1 - x, 1 - y)]

        def slot(a, px, py, pc):
            return out_refs[a].at[4 * px + 2 * py + pc]

        def copy(a, k, block, to, src=None):
            return pltpu.make_async_remote_copy(
                src_ref=slot(a, *block) if src is None else src, dst_ref=slot(a, *block),
                send_sem=send_sems.at[a, k], recv_sem=recv_sems.at[a, k], device_id=to, device_id_type=MESH_IDS)

        arrays = range(n_arr)
        mine = [pltpu.make_async_copy(x_refs[a], slot(a, *me), local_sems.at[a]) for a in arrays]
        for cp in mine:
            cp.start()
        first = [copy(a, 0, me, sibling, src=x_refs[a]) for a in arrays]
        first += [copy(a, 1 + j, me, (*chip, c), src=x_refs[a]) for j, chip in enumerate(chips) for a in arrays]
        for cp in first:
            cp.start()
        passed = []
        for j, chip in enumerate(chips):
            for a in arrays:
                copy(a, 1 + j, (*chip, c), me).wait_recv()
                passed.append(copy(a, 4 + j, (*chip, c), sibling))
                passed[-1].start()
        for a in arrays:
            copy(a, 0, sibling, me).wait_recv()
        for j, chip in enumerate(chips):
            for a in arrays:
                copy(a, 4 + j, (*chip, 1 - c), me).wait_recv()
        for cp in first + passed:
            cp.wait_send()
        for cp in mine:
            cp.wait()

    any_spec = pl.BlockSpec(memory_space=pl.ANY)
    return pl.pallas_call(
        body, name=name, out_shape=tuple(jax.ShapeDtypeStruct((N_DEV,) + s.shape, s.dtype) for s in shards),
        in_specs=[any_spec] * n_arr, out_specs=(any_spec,) * n_arr,
        scratch_shapes=[pltpu.SemaphoreType.DMA((n_arr, 7)), pltpu.SemaphoreType.DMA((n_arr, 7)),
                        pltpu.SemaphoreType.DMA((n_arr,))],
    )(*shards)


def _all_to_all(bufs, *, name):
    n_arr = len(bufs)

    def body(*refs):
        x_refs, out_refs = refs[:n_arr], refs[n_arr:2 * n_arr]
        send_sems, recv_sems, local_sems = refs[2 * n_arr:]
        x, y, c = lax.axis_index("x"), lax.axis_index("y"), lax.axis_index("c")
        me = 4 * x + 2 * y + c
        mine = [pltpu.make_async_copy(x_refs[a].at[me], out_refs[a].at[me], local_sems.at[a]) for a in range(n_arr)]
        for cp in mine:
            cp.start()

        def copy(a, k, landing):
            px, py, pc = x ^ (k >> 2), y ^ ((k >> 1) & 1), c ^ (k & 1)
            peer = 4 * px + 2 * py + pc
            return pltpu.make_async_remote_copy(
                src_ref=x_refs[a].at[peer], dst_ref=out_refs[a].at[peer if landing else me],
                send_sem=send_sems.at[a, k - 1], recv_sem=recv_sems.at[a, k - 1],
                device_id=(px, py, pc), device_id_type=MESH_IDS)

        copies = [copy(a, k, False) for k in range(1, N_DEV) for a in range(n_arr)]
        for cp in copies:
            cp.start()
        for k in range(1, N_DEV):
            for a in range(n_arr):
                copy(a, k, True).wait_recv()
        for cp in copies:
            cp.wait_send()
        for cp in mine:
            cp.wait()

    any_spec = pl.BlockSpec(memory_space=pl.ANY)
    return pl.pallas_call(
        body, name=name, out_shape=tuple(jax.ShapeDtypeStruct(b.shape, b.dtype) for b in bufs),
        in_specs=[any_spec] * n_arr, out_specs=(any_spec,) * n_arr,
        scratch_shapes=[pltpu.SemaphoreType.DMA((n_arr, 7)), pltpu.SemaphoreType.DMA((n_arr, 7)),
                        pltpu.SemaphoreType.DMA((n_arr,))],
    )(*bufs)


def _pack_small(vals):
    flat = jnp.concatenate([vals[n].reshape(-1) for n in SMALL])
    rows = -(-flat.shape[0] // (SUBLANES * PACK_COLS)) * SUBLANES
    return jnp.pad(flat, (0, rows * PACK_COLS - flat.shape[0])).reshape(rows, PACK_COLS)


def _unpack_small(flat, shapes):
    flat = flat.reshape(-1)
    out, off = {}, 0
    for n in SMALL:
        size = math.prod(shapes[n])
        out[n] = flat[off:off + size].reshape(shapes[n])
        off += size
    return out


def _heads(t2d):
    s = t2d.shape[0]
    return t2d.reshape(s, ATTN_HEADS, HEAD_DIM).transpose(1, 0, 2)


def _unheads(t3d):
    return t3d.transpose(1, 0, 2).reshape(t3d.shape[1], ATTN_WIDTH)


def _ssm_setup(sp):
    g, p = SSM_GROUPS, SSM_STATE
    a_re = sp["ssm_a_re"][:, None, :]
    a_im = sp["ssm_a_im"][:, None, :]
    log_dt = jnp.broadcast_to(sp["ssm_log_dt"][:, None, None], (g, 1, p))
    bt_re = sp["ssm_b_re"].transpose(0, 2, 1)
    bt_im = sp["ssm_b_im"].transpose(0, 2, 1)
    return a_re, a_im, log_dt, bt_re, bt_im


def _layer_fwd(h, p_l, w, sp):
    s = h.shape[0]
    hn = _rms_fwd(h, sp["mix_norm_g"][None], name="rms_mix")
    proj = _mm_cols(hn, w["cols_a"], w["in_row0"], D_MODEL, name="mm_proj")

    pw_re, pw_im, bbt_re, bbt_im = _ssm_param_fwd(*_ssm_setup(sp))
    pw_re = pw_re.transpose(1, 0, 2).reshape(SUBLANES, SSM_LANES)
    pw_im = pw_im.transpose(1, 0, 2).reshape(SUBLANES, SSM_LANES)
    ssm_mats = (_block_diag(bbt_re).astype(BF16), _block_diag(bbt_im).astype(BF16),
                _block_diag(sp["ssm_c_re"]).astype(BF16), _block_diag(-sp["ssm_c_im"]).astype(BF16),
                pw_re, pw_im, sp["ssm_d"].reshape(1, SSM_WIDTH))
    x_re, x_im, y, z = _ssm_fwd(proj, *ssm_mats)
    zz = _mm_cols(z, w["cols_b"], w["glu_row0"], SSM_WIDTH, name="mm_glu")

    q3 = _heads(proj[:, 2 * SSM_WIDTH:2 * SSM_WIDTH + ATTN_WIDTH]).reshape(ATTN_HEADS * s, HEAD_DIM)
    k3 = _heads(proj[:, 2 * SSM_WIDTH + ATTN_WIDTH:2 * SSM_WIDTH + 2 * ATTN_WIDTH]).reshape(ATTN_HEADS * s, HEAD_DIM)
    v3 = _heads(proj[:, 2 * SSM_WIDTH + 2 * ATTN_WIDTH:2 * SSM_WIDTH + 3 * ATTN_WIDTH]).astype(BF16)
    qn = _rms_fwd(q3, sp["q_norm_g"][None], name="rms_q").reshape(ATTN_HEADS, s, HEAD_DIM)
    kn = _rms_fwd(k3, sp["k_norm_g"][None], name="rms_k").reshape(ATTN_HEADS, s, HEAD_DIM)
    o3, bsum = _attn_fwd(qn, kn, v3)
    o = _unheads(o3)

    bglu = sp["ssm_b_glu"][None]
    ycat = _gate_fwd(zz, bglu, proj, o)
    h2 = _mm(ycat, w["w_out"], add=h, name="mm_out")
    hn2 = _rms_fwd(h2, sp["ple_norm_g"][None], name="rms_ple")
    pgl = _mm(hn2, w["w_ple_gate"], name="mm_ple_gate")
    pp = _mm_cols(p_l, w["cols_b"], w["ple_row0"], PLE_DIM, name="mm_ple_proj")
    h3 = _ple_out(h2, pgl, pp)
    saved = dict(h=h, hn=hn, proj=proj, ssm_mats=ssm_mats, x_re=x_re, x_im=x_im, y=y, z=z,
                 zz=zz, q3=q3, k3=k3, v3=v3, qn=qn, kn=kn, bsum=bsum, o=o, ycat=ycat, h2=h2, hn2=hn2,
                 pgl=pgl, pp=pp, p_l=p_l, bglu=bglu)
    return h3, saved


def _layer_bwd(dh3, sv, w, sp):
    s = dh3.shape[0]
    g, hh, p = SSM_GROUPS, SSM_GROUP, SSM_STATE
    grads = {}
    dgp, dpp = _ple_bwd(dh3, sv["pgl"], sv["pp"])
    grads["w_ple_gate"] = _mm(sv["hn2"], dgp, ta=True, out_dtype=BF16, name="mm_d_ple_gate")
    grads["w_ple_proj"] = _mm_cols_grad(sv["p_l"], dpp, name="mm_d_ple_proj")
    dhn2 = _mm(dgp, w["w_ple_gate"], tb=True, name="mm_dhn2")
    dh2, dg2 = _rms_bwd(sv["h2"], sp["ple_norm_g"][None], dhn2, dres=dh3, name="rms_ple_bwd")
    grads["ple_norm_g"] = dg2[0]
    grads["w_out"] = _mm(sv["ycat"], dh2, ta=True, out_dtype=BF16, name="mm_d_out")
    dyc = _mm(dh2, w["w_out"], tb=True, name="mm_dycat")
    dzz, dgs, do, dga, dbglu = _gate_bwd(dyc, sv["zz"], sv["bglu"], sv["proj"], sv["o"])
    grads["ssm_b_glu"] = dbglu[0]

    dqn, dkn_t, dv_t = _attn_bwd(sv["qn"], sv["kn"], sv["v3"], sv["bsum"], _heads(do))
    dkn = dkn_t.transpose(0, 2, 1)
    dq3, dgq = _rms_bwd(sv["q3"], sp["q_norm_g"][None], dqn.reshape(ATTN_HEADS * s, HEAD_DIM), name="rms_q_bwd")
    dk3, dgk = _rms_bwd(sv["k3"], sp["k_norm_g"][None], dkn.reshape(ATTN_HEADS * s, HEAD_DIM), name="rms_k_bwd")
    grads["q_norm_g"], grads["k_norm_g"] = dgq[0], dgk[0]
    dq = _unheads(dq3.reshape(ATTN_HEADS, s, HEAD_DIM))
    dk = _unheads(dk3.reshape(ATTN_HEADS, s, HEAD_DIM))
    dv = dv_t.transpose(2, 0, 1).reshape(s, ATTN_WIDTH)

    grads["ssm_w_glu"] = _mm_cols_grad(sv["z"], dzz, name="mm_d_glu")
    dz = _mm_cols_t(dzz, w["cols_b"], w["glu_row0"], SSM_WIDTH, name="mm_dz")
    du, dbc_re, dbc_im, dcc_re, dcc_im, gab_re, gab_im, dd = _ssm_bwd(
        dz, sv["y"], sv["proj"], sv["x_re"], sv["x_im"], *sv["ssm_mats"])
    grads["ssm_d"] = dd.reshape(g, hh)
    grads["ssm_c_re"] = _block_diag_take(dcc_re, hh, p)
    grads["ssm_c_im"] = -_block_diag_take(dcc_im, hh, p)
    da_re, da_im, dldt, dbt_re, dbt_im = _ssm_param_bwd(
        *_ssm_setup(sp), gab_re.reshape(g, 1, p), gab_im.reshape(g, 1, p),
        _block_diag_take(dbc_re, hh, p), _block_diag_take(dbc_im, hh, p))
    grads["ssm_a_re"], grads["ssm_a_im"] = da_re[:, 0, :], da_im[:, 0, :]
    grads["ssm_log_dt"] = dldt[:, 0, 0]
    grads["ssm_b_re"], grads["ssm_b_im"] = dbt_re.transpose(0, 2, 1), dbt_im.transpose(0, 2, 1)

    dproj = jnp.concatenate([du, dgs, dq, dk, dv, dga], axis=1)
    grads["w_in"] = _mm_cols_grad(sv["hn"], dproj, name="mm_d_in")
    dhn = _mm_cols_t(dproj, w["cols_a"], w["in_row0"], D_MODEL, name="mm_dhn")
    dh, dg1 = _rms_bwd(sv["h"], sp["mix_norm_g"][None], dhn, dres=dh2, name="rms_mix_bwd")
    grads["mix_norm_g"] = dg1[0]
    return dh, grads


def _local_step(x, p, target, cols_a, cols_b, rows_c, small):
    h = x
    saved, weights = [], []
    for l in range(DEPTH):
        weights.append(dict(cols_a=cols_a, in_row0=l * D_MODEL, cols_b=cols_b, glu_row0=l * SSM_WIDTH,
                            ple_row0=DEPTH * SSM_WIDTH + l * PLE_DIM, w_out=rows_c[l], w_ple_gate=rows_c[DEPTH + l]))
        sp = {n: small[n][l] for n in SMALL}
        h, sv = _layer_fwd(h, p[l], weights[l], sp)
        saved.append(sv)
    dh, loss_parts = _loss_head(h, target)
    grads = [None] * DEPTH
    for l in reversed(range(DEPTH)):
        sp = {n: small[n][l] for n in SMALL}
        dh, grads[l] = _layer_bwd(dh, saved[l], weights[l], sp)
    return jnp.sum(loss_parts), dh, grads


def kernel(x, p, mix_norm_g, w_in, ssm_a_re, ssm_a_im, ssm_log_dt, ssm_b_re, ssm_b_im, ssm_c_re, ssm_c_im, ssm_d, ssm_w_glu, ssm_b_glu, q_norm_g, k_norm_g, w_out, ple_norm_g, w_ple_gate, w_ple_proj, loss_target, m_mix_norm_g, m_w_in, m_ssm_a_re, m_ssm_a_im, m_ssm_log_dt, m_ssm_b_re, m_ssm_b_im, m_ssm_c_re, m_ssm_c_im, m_ssm_d, m_ssm_w_glu, m_ssm_b_glu, m_q_norm_g, m_k_norm_g, m_w_out, m_ple_norm_g, m_w_ple_gate, m_w_ple_proj, v_mix_norm_g, v_w_in, v_ssm_a_re, v_ssm_a_im, v_ssm_log_dt, v_ssm_b_re, v_ssm_b_im, v_ssm_c_re, v_ssm_c_im, v_ssm_d, v_ssm_w_glu, v_ssm_b_glu, v_q_norm_g, v_k_norm_g, v_w_out, v_ple_norm_g, v_w_ple_gate, v_w_ple_proj):
    given = dict(locals())
    wts = {n: given[n] for n in WEIGHTS}
    mom = {n: given["m_" + n] for n in WEIGHTS}
    var = {n: given["v_" + n] for n in WEIGHTS}

    def rows2d(a):
        return a.reshape(-1, a.shape[-1])

    groups = (("w_in",), ("ssm_w_glu", "w_ple_proj"), ("w_out", "w_ple_gate"))
    shards = [jnp.concatenate([rows2d(wts[n]).astype(BF16) for n in names], axis=0) for names in groups]
    cols_a, cols_b, rows_c = _all_gather(shards, name="gather_weights")
    rows_c = rows_c.reshape(N_DEV, 2 * DEPTH, -1, D_MODEL).transpose(1, 0, 2, 3).reshape(2 * DEPTH, D_MODEL, D_MODEL)

    small = {n: wts[n] for n in SMALL}
    loss, dx, grads = _local_step(x[0], p[:, 0], loss_target[0], cols_a, cols_b, rows_c, small)
    loss = lax.psum(loss, ("x", "y", "c"))

    def blocks(n, l):
        g = grads[l][n]
        return g if g.ndim == 3 else g.reshape(N_DEV, -1, g.shape[-1])

    send = [jnp.concatenate([blocks(n, l) for n in names for l in range(DEPTH)], axis=1) for names in groups]
    parts = _all_to_all(send, name="exchange_weight_grads")
    big_out = {}
    for names, part in zip(groups, parts):
        row0 = 0
        for n in names:
            res = _reduce_adamw(part, rows2d(wts[n]), rows2d(mom[n]), rows2d(var[n]), row0=row0, name="adamw_" + n)
            big_out[n] = [a.reshape(wts[n].shape) for a in res]
            row0 += rows2d(wts[n]).shape[0]

    small_shapes = {n: wts[n].shape for n in SMALL}
    local_small = _pack_small({n: jnp.stack([grads[l][n] for l in range(DEPTH)]) for n in SMALL})
    (all_small,) = _all_gather([local_small], name="gather_small_grads")
    small_out = _reduce_adamw(all_small, _pack_small(small), _pack_small({n: mom[n] for n in SMALL}),
                              _pack_small({n: var[n] for n in SMALL}), name="adamw_replicated")
    small_out = [_unpack_small(a, small_shapes) for a in small_out]

    outs = [loss, dx[None]]
    for k in range(4):
        outs += [big_out[n][k] if n in BIG else small_out[k][n] for n in WEIGHTS]
    return tuple(outs)
```

```python
import math

import jax
import jax.numpy as jnp
from jax import lax
from jax.experimental import pallas as pl
from jax.experimental.pallas import tpu as pltpu

F32 = jnp.float32
BF16 = jnp.bfloat16
MESH_IDS = pl.DeviceIdType.MESH

N_DEV = 8
D_MODEL = 1024
DEPTH = 2
PLE_DIM = 256
SSM_WIDTH = 512
SSM_GROUP = 16
SSM_GROUPS = 32
SSM_STATE = 64
SSM_LANES = SSM_GROUPS * SSM_STATE
ATTN_WIDTH = 512
ATTN_HEADS = 8
HEAD_DIM = 64
RMS_EPS = 1e-6
ADAM_LR = 0.001
ADAM_B1 = 0.9
ADAM_B2 = 0.999
ADAM_EPS = 1e-08
ADAM_WD = 0.01
ADAM_STEP = 10

VMEM_LIMIT = 56 * 1024 * 1024
ATT_TQ = 512
ATT_TK = 128
SSM_BLOCK_GROUPS = 8
SSM_TIME_BLOCK = 512
SUBLANES = 8
PACK_COLS = 1024

BIG = ("w_in", "ssm_w_glu", "w_out", "w_ple_gate", "w_ple_proj")
ROW_SHARDED = ("w_out", "w_ple_gate")
SMALL = ("mix_norm_g", "ssm_a_re", "ssm_a_im", "ssm_log_dt", "ssm_b_re", "ssm_b_im", "ssm_c_re",
         "ssm_c_im", "ssm_d", "ssm_b_glu", "q_norm_g", "k_norm_g", "ple_norm_g")
WEIGHTS = ("mix_norm_g", "w_in", "ssm_a_re", "ssm_a_im", "ssm_log_dt", "ssm_b_re", "ssm_b_im",
           "ssm_c_re", "ssm_c_im", "ssm_d", "ssm_w_glu", "ssm_b_glu", "q_norm_g", "k_norm_g",
           "w_out", "ple_norm_g", "w_ple_gate", "w_ple_proj")
_NT = (((1,), (1,)), ((), ()))
_TN = (((0,), (0,)), ((), ()))


def _params(sem=None):
    return pltpu.CompilerParams(dimension_semantics=sem, vmem_limit_bytes=VMEM_LIMIT)


def _sigmoid(x):
    return 1.0 / (1.0 + jnp.exp(-x))


_GELU_K = math.sqrt(2.0 / math.pi)
_GELU_C = 0.044715


def _gelu(x):
    return 0.5 * x * (1.0 + jnp.tanh(_GELU_K * (x + _GELU_C * x * x * x)))


def _gelu_grad(x):
    th = jnp.tanh(_GELU_K * (x + _GELU_C * x * x * x))
    return 0.5 * (1.0 + th) + 0.5 * x * (1.0 - th * th) * _GELU_K * (1.0 + 3.0 * _GELU_C * x * x)


def _mm_call(a, b, add, a_spec, b_spec, o_spec, grid, out_shape, acc_shape, dims, name):
    n_steps = grid[2]
    out_dtype = out_shape.dtype

    def body(*refs):
        if add is None:
            a_ref, b_ref, o_ref, acc_ref = refs
        else:
            a_ref, b_ref, r_ref, o_ref, acc_ref = refs
        kk = pl.program_id(2)

        @pl.when(kk == 0)
        def _():
            acc_ref[...] = jnp.zeros_like(acc_ref)

        acc_ref[...] += lax.dot_general(a_ref[...].astype(BF16), b_ref[...].astype(BF16), dims,
                                        preferred_element_type=F32)

        @pl.when(kk == n_steps - 1)
        def _():
            res = acc_ref[...]
            if add is not None:
                res = res + r_ref[...].astype(F32)
            o_ref[...] = res.astype(out_dtype)

    ins, specs = [a, b], [a_spec, b_spec]
    if add is not None:
        ins.append(add)
        specs.append(o_spec)
    return pl.pallas_call(
        body, name=name, grid=grid, in_specs=specs, out_specs=o_spec, out_shape=out_shape,
        scratch_shapes=[pltpu.VMEM(acc_shape, F32)],
        compiler_params=_params(("parallel", "parallel", "arbitrary")),
    )(*ins)


def _mm(a, b, *, name, ta=False, tb=False, add=None, out_dtype=F32, tm=1024, tn=1024, tk=512):
    m, k = (a.shape[1], a.shape[0]) if ta else a.shape
    n = b.shape[0] if tb else b.shape[1]
    tm, tn, tk = min(tm, m), min(tn, n), min(tk, k)
    assert m % tm == 0 and n % tn == 0 and k % tk == 0, (name, a.shape, b.shape)
    dims = (((0 if ta else 1,), (1 if tb else 0,)), ((), ()))
    a_spec = (pl.BlockSpec((tk, tm), lambda i, j, kk: (kk, i)) if ta
              else pl.BlockSpec((tm, tk), lambda i, j, kk: (i, kk)))
    b_spec = (pl.BlockSpec((tn, tk), lambda i, j, kk: (j, kk)) if tb
              else pl.BlockSpec((tk, tn), lambda i, j, kk: (kk, j)))
    o_spec = pl.BlockSpec((tm, tn), lambda i, j, kk: (i, j))
    return _mm_call(a, b, add, a_spec, b_spec, o_spec, (m // tm, n // tn, k // tk),
                    jax.ShapeDtypeStruct((m, n), out_dtype), (tm, tn), dims, name)


def _rows_tile(rows, want=512):
    t = min(rows, want)
    assert rows % t == 0
    return t


def _rowwise(body, name, rows, ins, outs):
    widest = max(cols for _, cols, _ in ins)
    tm = _rows_tile(rows, 512 if widest > 128 else 4096)
    in_specs = []
    for _, cols, cb in ins:
        if cb is None:
            in_specs.append(pl.BlockSpec((1, cols), lambda i: (0, 0)))
        else:
            in_specs.append(pl.BlockSpec((tm, cols), lambda i, cb=cb: (i, cb)))
    out_specs, out_shapes = [], []
    for cols, dtype, is_acc in outs:
        if is_acc:
            out_specs.append(pl.BlockSpec((1, cols), lambda i: (0, 0)))
            out_shapes.append(jax.ShapeDtypeStruct((1, cols), dtype))
        else:
            out_specs.append(pl.BlockSpec((tm, cols), lambda i: (i, 0)))
            out_shapes.append(jax.ShapeDtypeStruct((rows, cols), dtype))
    any_acc = any(o[2] for o in outs)
    return pl.pallas_call(
        body, name=name, grid=(rows // tm,), in_specs=in_specs, out_specs=tuple(out_specs),
        out_shape=tuple(out_shapes),
        compiler_params=_params(("arbitrary",) if any_acc else ("parallel",)))(*[a for a, _, _ in ins])


def _accumulate(ref, value):
    @pl.when(pl.program_id(0) == 0)
    def _():
        ref[...] = jnp.zeros_like(ref)

    ref[...] += value


def _rms_fwd(h, g, *, name):
    rows, d = h.shape

    def body(h_ref, g_ref, o_ref):
        hv = h_ref[...]
        r = lax.rsqrt(jnp.mean(hv * hv, axis=-1, keepdims=True) + RMS_EPS)
        o_ref[...] = (hv * r * g_ref[...]).astype(BF16)

    return _rowwise(body, name, rows, [(h, d, 0), (g, d, None)], [(d, BF16, False)])[0]


def _rms_bwd(h, g, dhn, *, name, dres=None):
    rows, d = h.shape

    def body(*refs):
        if dres is None:
            h_ref, g_ref, dn_ref, dh_ref, dg_ref = refs
        else:
            h_ref, g_ref, dn_ref, dr_ref, dh_ref, dg_ref = refs
        hv = h_ref[...]
        dn = dn_ref[...].astype(F32)
        r = lax.rsqrt(jnp.mean(hv * hv, axis=-1, keepdims=True) + RMS_EPS)
        a = dn * g_ref[...]
        dot = jnp.mean(a * hv, axis=-1, keepdims=True)
        dh = r * a - hv * (r * r * r * dot)
        if dres is not None:
            dh = dh + dr_ref[...]
        dh_ref[...] = dh
        _accumulate(dg_ref, jnp.sum(dn * hv * r, axis=0, keepdims=True))

    ins = [(h, d, 0), (g, d, None), (dhn, d, 0)]
    if dres is not None:
        ins.append((dres, d, 0))
    return _rowwise(body, name, rows, ins, [(d, F32, False), (d, F32, True)])


def _gate_fwd(zz, bglu, proj, o):
    rows = zz.shape[0]
    w = SSM_WIDTH

    def body(zz_ref, b_ref, gs_ref, o_ref, ga_ref, y_ref):
        zz_v = zz_ref[...] + b_ref[...]
        val, gate = zz_v[:, :w], zz_v[:, w:]
        gs, ga = gs_ref[...], ga_ref[...]
        y_ref[:, :w] = (val * _sigmoid(gate) * (gs * _sigmoid(gs))).astype(BF16)
        y_ref[:, w:] = (o_ref[...] * (ga * _sigmoid(ga))).astype(BF16)

    return _rowwise(body, "gate_fwd", rows,
                    [(zz, 2 * w, 0), (bglu, 2 * w, None), (proj, w, 1), (o, w, 0), (proj, w, 5)],
                    [(2 * w, BF16, False)])[0]


def _gate_bwd(dyc, zz, bglu, proj, o):
    rows = zz.shape[0]
    w = SSM_WIDTH

    def body(dy_ref, zz_ref, b_ref, gs_ref, o_ref, ga_ref, dzz_ref, dgs_ref, do_ref, dga_ref, db_ref):
        zz_v = zz_ref[...] + b_ref[...]
        val, gate = zz_v[:, :w], zz_v[:, w:]
        gs, ga = gs_ref[...], ga_ref[...]
        dys, dya = dy_ref[:, :w], dy_ref[:, w:]
        sg, ss, sa = _sigmoid(gate), _sigmoid(gs), _sigmoid(ga)
        glu = val * sg
        dglu = dys * (gs * ss)
        dgs_ref[...] = dys * glu * (ss * (1.0 + gs * (1.0 - ss)))
        dval = dglu * sg
        dgate = dglu * val * sg * (1.0 - sg)
        dzz_ref[:, :w] = dval.astype(BF16)
        dzz_ref[:, w:] = dgate.astype(BF16)
        do_ref[...] = dya * (ga * sa)
        dga_ref[...] = dya * o_ref[...] * (sa * (1.0 + ga * (1.0 - sa)))
        _accumulate(db_ref, jnp.concatenate([jnp.sum(dval, axis=0, keepdims=True),
                                             jnp.sum(dgate, axis=0, keepdims=True)], axis=1))

    return _rowwise(body, "gate_bwd", rows,
                    [(dyc, 2 * w, 0), (zz, 2 * w, 0), (bglu, 2 * w, None), (proj, w, 1), (o, w, 0), (proj, w, 5)],
                    [(2 * w, BF16, False), (w, F32, False), (w, F32, False), (w, F32, False), (2 * w, F32, True)])


def _ple_out(h2, pgl, pp):
    rows, d = h2.shape

    def body(h_ref, g_ref, p_ref, o_ref):
        o_ref[...] = h_ref[...] + _sigmoid(g_ref[...]) * p_ref[...]

    return _rowwise(body, "ple_out", rows, [(h2, d, 0), (pgl, d, 0), (pp, d, 0)], [(d, F32, False)])[0]


def _ple_bwd(dh3, pgl, pp):
    rows, d = dh3.shape

    def body(dh_ref, g_ref, p_ref, dg_ref, dp_ref):
        sg = _sigmoid(g_ref[...])
        dh = dh_ref[...]
        dg_ref[...] = (dh * p_ref[...] * sg * (1.0 - sg)).astype(BF16)
        dp_ref[...] = (dh * sg).astype(BF16)

    return _rowwise(body, "ple_bwd", rows, [(dh3, d, 0), (pgl, d, 0), (pp, d, 0)],
                    [(d, BF16, False), (d, BF16, False)])


def _loss_head(h, target):
    rows, d = h.shape

    def body(h_ref, t_ref, dh_ref, l_ref):
        err = h_ref[...] - t_ref[...]
        dh_ref[...] = err * (1.0 / d)
        _accumulate(l_ref, jnp.sum(err * err, axis=0, keepdims=True) * (0.5 / d))

    return _rowwise(body, "loss_head", rows, [(h, d, 0), (target, d, 0)], [(d, F32, False), (d, F32, True)])


def _zoh(lr, li, dt):
    mag = jnp.exp(lr * dt)
    ab_re = mag * jnp.cos(li * dt)
    ab_im = mag * jnp.sin(li * dt)
    num_re = ab_re - 1.0
    den = lr * lr + li * li
    f_re = (num_re * lr + ab_im * li) / den
    f_im = (ab_im * lr - num_re * li) / den
    return ab_re, ab_im, f_re, f_im, den


def _ssm_param_fwd(a_re, a_im, log_dt, bt_re, bt_im):
    g, _, p = a_re.shape
    h = bt_re.shape[1]

    def body(lr_ref, li_ref, ldt_ref, br_ref, bi_ref, pr_ref, pi_ref, bbr_ref, bbi_ref):
        dt = jnp.exp(ldt_ref[...])
        ab_re, ab_im, f_re, f_im, _ = _zoh(lr_ref[...], li_ref[...], dt)
        cr, ci = ab_re, ab_im
        for k in range(SUBLANES):
            pr_ref[:, k:k + 1, :] = cr
            pi_ref[:, k:k + 1, :] = ci
            cr, ci = cr * ab_re - ci * ab_im, cr * ab_im + ci * ab_re
        br, bi = br_ref[...], bi_ref[...]
        bbr_ref[...] = f_re * br - f_im * bi
        bbi_ref[...] = f_re * bi + f_im * br

    pw = jax.ShapeDtypeStruct((g, SUBLANES, p), F32)
    large = jax.ShapeDtypeStruct((g, h, p), F32)
    return pl.pallas_call(body, name="ssm_param_fwd", out_shape=(pw, pw, large, large),
                          compiler_params=_params())(a_re, a_im, log_dt, bt_re, bt_im)


def _ssm_param_bwd(a_re, a_im, log_dt, bt_re, bt_im, gab_re, gab_im, gbb_re, gbb_im):
    g, _, p = a_re.shape
    h = bt_re.shape[1]

    def body(lr_ref, li_ref, ldt_ref, br_ref, bi_ref, gar_ref, gai_ref, gbr_ref, gbi_ref,
             dlr_ref, dli_ref, dldt_ref, dbr_ref, dbi_ref):
        lr, li = lr_ref[...], li_ref[...]
        dt = jnp.exp(ldt_ref[...])
        ab_re, ab_im, f_re, f_im, den = _zoh(lr, li, dt)
        br, bi = br_ref[...], bi_ref[...]
        gbr, gbi = gbr_ref[...], gbi_ref[...]
        dbr_ref[...] = f_re * gbr + f_im * gbi
        dbi_ref[...] = f_re * gbi - f_im * gbr
        gf_re = jnp.sum(br * gbr + bi * gbi, axis=1, keepdims=True)
        gf_im = jnp.sum(br * gbi - bi * gbr, axis=1, keepdims=True)
        il_re, il_im = lr / den, -li / den
        ga_re = gar_ref[...] + il_re * gf_re + il_im * gf_im
        ga_im = gai_ref[...] + il_re * gf_im - il_im * gf_re
        q_re = f_re * il_re - f_im * il_im
        q_im = f_re * il_im + f_im * il_re
        gl_re = -(q_re * gf_re + q_im * gf_im)
        gl_im = -(q_re * gf_im - q_im * gf_re)
        dlr_ref[...] = gl_re + dt * (ab_re * ga_re + ab_im * ga_im)
        dli_ref[...] = gl_im + dt * (ab_re * ga_im - ab_im * ga_re)
        la_re = lr * ab_re - li * ab_im
        la_im = lr * ab_im + li * ab_re
        dldt_ref[...] = jnp.sum((la_re * ga_re + la_im * ga_im) * dt, axis=2, keepdims=True)

    small = jax.ShapeDtypeStruct((g, 1, p), F32)
    one = jax.ShapeDtypeStruct((g, 1, 1), F32)
    large = jax.ShapeDtypeStruct((g, h, p), F32)
    return pl.pallas_call(body, name="ssm_param_bwd", out_shape=(small, small, one, large, large),
                          compiler_params=_params())(
                              a_re, a_im, log_dt, bt_re, bt_im, gab_re, gab_im, gbb_re, gbb_im)


def _block_diag(m):
    g, h, p = m.shape
    nb = g // SSM_BLOCK_GROUPS
    eye = jnp.eye(SSM_BLOCK_GROUPS, dtype=m.dtype)
    m4 = m.reshape(nb, SSM_BLOCK_GROUPS, h, p)
    return (m4[:, :, :, None, :] * eye[None, :, None, :, None]).reshape(nb, SSM_BLOCK_GROUPS * h, SSM_BLOCK_GROUPS * p)


def _block_diag_take(m, h, p):
    nb = m.shape[0]
    eye = jnp.eye(SSM_BLOCK_GROUPS, dtype=m.dtype)
    m5 = m.reshape(nb, SSM_BLOCK_GROUPS, h, SSM_BLOCK_GROUPS, p)
    return jnp.sum(m5 * eye[None, :, None, :, None], axis=3).reshape(nb * SSM_BLOCK_GROUPS, h, p)


def _scan_tile(br, bi, pw_re, pw_im, reverse):
    row = lax.broadcasted_iota(jnp.int32, br.shape, 0)
    xr, xi = br, bi
    for d in (1, 2, 4):
        ar, ai = pw_re[d - 1:d, :], pw_im[d - 1:d, :]
        if reverse:
            keep = row < SUBLANES - d
            sr = jnp.where(keep, pltpu.roll(xr, SUBLANES - d, 0), 0.0)
            si = jnp.where(keep, pltpu.roll(xi, SUBLANES - d, 0), 0.0)
        else:
            keep = row >= d
            sr = jnp.where(keep, pltpu.roll(xr, d, 0), 0.0)
            si = jnp.where(keep, pltpu.roll(xi, d, 0), 0.0)
        xr, xi = xr + ar * sr - ai * si, xi + ar * si + ai * sr
    return xr, xi


def _ssm_blocks(s):
    tt = min(SSM_TIME_BLOCK, s)
    assert s % tt == 0 and tt % SUBLANES == 0
    ch = SSM_BLOCK_GROUPS * SSM_GROUP
    st = SSM_BLOCK_GROUPS * SSM_STATE
    return tt, s // tt, SSM_WIDTH // ch, ch, st


def _ssm_fwd(proj, bc_re, bc_im, cc_re, cc_im, pw_re, pw_im, dvec):
    s = proj.shape[0]
    tt, nt, nb, ch, st = _ssm_blocks(s)
    n_tiles = tt // SUBLANES

    def body(u_ref, bre_ref, bim_ref, cre_ref, cim_ref, pr_ref, pi_ref, d_ref,
             xr_ref, xi_ref, y_ref, z_ref, carry_ref):
        @pl.when(pl.program_id(1) == 0)
        def _():
            carry_ref[...] = jnp.zeros_like(carry_ref)

        u = u_ref[...]
        ub = u.astype(BF16)
        xr_ref[...] = jnp.dot(ub, bre_ref[0], preferred_element_type=F32)
        xi_ref[...] = jnp.dot(ub, bim_ref[0], preferred_element_type=F32)
        pw_r, pw_i = pr_ref[...], pi_ref[...]

        def step(t, carry):
            cr, ci = carry
            rows = pl.ds(pl.multiple_of(t * SUBLANES, SUBLANES), SUBLANES)
            xr, xi = _scan_tile(xr_ref[rows, :], xi_ref[rows, :], pw_r, pw_i, False)
            xr, xi = xr + pw_r * cr - pw_i * ci, xi + pw_r * ci + pw_i * cr
            xr_ref[rows, :] = xr
            xi_ref[rows, :] = xi
            return (jnp.broadcast_to(xr[SUBLANES - 1:SUBLANES, :], (SUBLANES, st)),
                    jnp.broadcast_to(xi[SUBLANES - 1:SUBLANES, :], (SUBLANES, st)))

        cr, ci = lax.fori_loop(0, n_tiles, step, (carry_ref[0], carry_ref[1]), unroll=2)
        carry_ref[0] = cr
        carry_ref[1] = ci
        y = (lax.dot_general(xr_ref[...].astype(BF16), cre_ref[0], _NT, preferred_element_type=F32)
             + lax.dot_general(xi_ref[...].astype(BF16), cim_ref[0], _NT, preferred_element_type=F32)
             + d_ref[...] * u)
        y_ref[...] = y
        z_ref[...] = _gelu(y).astype(BF16)

    chan = pl.BlockSpec((tt, ch), lambda b, t: (t, b))
    state = pl.BlockSpec((tt, st), lambda b, t: (t, b))
    mat = pl.BlockSpec((1, ch, st), lambda b, t: (b, 0, 0))
    pw = pl.BlockSpec((SUBLANES, st), lambda b, t: (0, b))
    vec = pl.BlockSpec((1, ch), lambda b, t: (0, b))
    x_shape = jax.ShapeDtypeStruct((s, nb * st), F32)
    return pl.pallas_call(
        body, name="ssm_fwd", grid=(nb, nt), in_specs=[chan, mat, mat, mat, mat, pw, pw, vec],
        out_specs=(state, state, chan, chan),
        out_shape=(x_shape, x_shape, jax.ShapeDtypeStruct((s, nb * ch), F32), jax.ShapeDtypeStruct((s, nb * ch), BF16)),
        scratch_shapes=[pltpu.VMEM((2, SUBLANES, st), F32)],
        compiler_params=_params(("parallel", "arbitrary")))(proj, bc_re, bc_im, cc_re, cc_im, pw_re, pw_im, dvec)


def _ssm_bwd(dz, y, proj, x_re, x_im, bc_re, bc_im, cc_re, cc_im, pw_re, pw_im, dvec):
    s = proj.shape[0]
    tt, nt, nb, ch, st = _ssm_blocks(s)
    n_tiles = tt // SUBLANES

    def body(dz_ref, y_ref, u_ref, xr_ref, xi_ref, bre_ref, bim_ref, cre_ref, cim_ref, pr_ref, pi_ref, d_ref,
             du_ref, dbr_ref, dbi_ref, dcr_ref, dci_ref, gar_ref, gai_ref, dd_ref,
             lr_ref, li_ref, carry_ref, acc_ref):
        @pl.when(pl.program_id(1) == 0)
        def _():
            for ref in (carry_ref, acc_ref, dbr_ref, dbi_ref, dcr_ref, dci_ref, dd_ref):
                ref[...] = jnp.zeros_like(ref)

        u = u_ref[...]
        dy = dz_ref[...] * _gelu_grad(y_ref[...])
        ub, dyb = u.astype(BF16), dy.astype(BF16)
        lr_ref[...] = jnp.dot(dyb, cre_ref[0], preferred_element_type=F32)
        li_ref[...] = jnp.dot(dyb, cim_ref[0], preferred_element_type=F32)
        pw_r, pw_i = pr_ref[...], -pi_ref[...]
        pwc_r = jnp.concatenate([pw_r[SUBLANES - 1 - i:SUBLANES - i, :] for i in range(SUBLANES)], axis=0)
        pwc_i = jnp.concatenate([pw_i[SUBLANES - 1 - i:SUBLANES - i, :] for i in range(SUBLANES)], axis=0)
        row = lax.broadcasted_iota(jnp.int32, (SUBLANES, st), 0)

        def step(t, carry):
            cr, ci, acc_r, acc_i = carry
            rows = pl.ds(pl.multiple_of((n_tiles - 1 - t) * SUBLANES, SUBLANES), SUBLANES)
            lr, li = _scan_tile(lr_ref[rows, :], li_ref[rows, :], pw_r, pw_i, True)
            lr, li = lr + pwc_r * cr - pwc_i * ci, li + pwc_r * ci + pwc_i * cr
            lr_ref[rows, :] = lr
            li_ref[rows, :] = li
            nr = jnp.where(row < SUBLANES - 1, pltpu.roll(lr, SUBLANES - 1, 0), cr)
            ni = jnp.where(row < SUBLANES - 1, pltpu.roll(li, SUBLANES - 1, 0), ci)
            xr, xi = xr_ref[rows, :], xi_ref[rows, :]
            return (jnp.broadcast_to(lr[0:1, :], (SUBLANES, st)), jnp.broadcast_to(li[0:1, :], (SUBLANES, st)),
                    acc_r + xr * nr + xi * ni, acc_i + xr * ni - xi * nr)

        cr, ci, acc_r, acc_i = lax.fori_loop(
            0, n_tiles, step, (carry_ref[0], carry_ref[1], acc_ref[0], acc_ref[1]), unroll=2)
        carry_ref[0], carry_ref[1] = cr, ci
        acc_ref[0], acc_ref[1] = acc_r, acc_i
        lrb, lib = lr_ref[...].astype(BF16), li_ref[...].astype(BF16)
        du_ref[...] = (lax.dot_general(lrb, bre_ref[0], _NT, preferred_element_type=F32)
                       + lax.dot_general(lib, bim_ref[0], _NT, preferred_element_type=F32) + dy * d_ref[...])
        dbr_ref[0] += lax.dot_general(ub, lrb, _TN, preferred_element_type=F32)
        dbi_ref[0] += lax.dot_general(ub, lib, _TN, preferred_element_type=F32)
        dcr_ref[0] += lax.dot_general(dyb, xr_ref[...].astype(BF16), _TN, preferred_element_type=F32)
        dci_ref[0] += lax.dot_general(dyb, xi_ref[...].astype(BF16), _TN, preferred_element_type=F32)
        dd_ref[...] += jnp.sum(dy * u, axis=0, keepdims=True)

        @pl.when(pl.program_id(1) == nt - 1)
        def _():
            gar_ref[...] = jnp.sum(acc_r, axis=0, keepdims=True)
            gai_ref[...] = jnp.sum(acc_i, axis=0, keepdims=True)

    chan = pl.BlockSpec((tt, ch), lambda b, t: (nt - 1 - t, b))
    state = pl.BlockSpec((tt, st), lambda b, t: (nt - 1 - t, b))
    mat = pl.BlockSpec((1, ch, st), lambda b, t: (b, 0, 0))
    pw = pl.BlockSpec((SUBLANES, st), lambda b, t: (0, b))
    vec = pl.BlockSpec((1, ch), lambda b, t: (0, b))
    svec = pl.BlockSpec((1, st), lambda b, t: (0, b))
    mat_shape = jax.ShapeDtypeStruct((nb, ch, st), F32)
    return pl.pallas_call(
        body, name="ssm_bwd", grid=(nb, nt),
        in_specs=[chan, chan, chan, state, state, mat, mat, mat, mat, pw, pw, vec],
        out_specs=(chan, mat, mat, mat, mat, svec, svec, vec),
        out_shape=(jax.ShapeDtypeStruct((s, nb * ch), F32), mat_shape, mat_shape, mat_shape, mat_shape,
                   jax.ShapeDtypeStruct((1, nb * st), F32), jax.ShapeDtypeStruct((1, nb * st), F32),
                   jax.ShapeDtypeStruct((1, nb * ch), F32)),
        scratch_shapes=[pltpu.VMEM((tt, st), F32), pltpu.VMEM((tt, st), F32),
                        pltpu.VMEM((2, SUBLANES, st), F32), pltpu.VMEM((2, SUBLANES, st), F32)],
        compiler_params=_params(("parallel", "arbitrary")))(
            dz, y, proj, x_re, x_im, bc_re, bc_im, cc_re, cc_im, pw_re, pw_im, dvec)


def _split_dot(x, tri2):
    hi = x.astype(BF16)
    lo = (x - hi.astype(F32)).astype(BF16)
    return jnp.dot(jnp.concatenate([hi, lo], axis=1), tri2, preferred_element_type=F32)


def _tri_and_ones(tk, rel):
    r_i = lax.broadcasted_iota(jnp.int32, (2 * tk, 2 * tk), 0) % tk
    c_i = lax.broadcasted_iota(jnp.int32, (2 * tk, 2 * tk), 1)
    return jnp.where((c_i >= tk) | rel(r_i, c_i), 1.0, 0.0).astype(BF16)


def _attn_scores(qs, kb, t0, s0, masked):
    z = lax.dot_general(qs, kb, _NT, preferred_element_type=F32)
    log_beta = jnp.minimum(z, 0.0) - jnp.log(1.0 + jnp.exp(-jnp.abs(z)))
    log_stay = log_beta - z
    if not masked:
        return log_beta, log_stay, None
    mask = (s0 + lax.broadcasted_iota(jnp.int32, z.shape, 1)) < (t0 + lax.broadcasted_iota(jnp.int32, z.shape, 0))
    return log_beta, jnp.where(mask, log_stay, 0.0), mask


def _attn_blocks(s):
    tq, tk = min(ATT_TQ, s), min(ATT_TK, s)
    assert s % tq == 0 and tq % tk == 0
    return tq, tk, tq // tk


def _attn_fwd(qn, kn, v):
    nh, s, d = qn.shape
    tq, tk, r = _attn_blocks(s)

    def body(q_ref, k_ref, v_ref, o_ref, b_ref):
        qi = pl.program_id(1)
        qs = (q_ref[0].astype(F32) * (HEAD_DIM ** -0.5)).astype(BF16)
        tri = _tri_and_ones(tk, lambda row, col: row > col)

        def tiles(kq, carry, masked):
            o_acc, acc = carry
            ks = pl.ds(pl.multiple_of(kq * tq, tq), tq)
            log_beta, log_stay, mask = _attn_scores(qs, k_ref[0, ks, :], qi * tq, kq * tq, masked)
            sums = [_split_dot(log_stay[:, u * tk:(u + 1) * tk], tri) for u in range(r)]
            later = [None] * r
            for u in reversed(range(r)):
                later[u] = acc + sums[u][:, :tk]
                acc = acc + sums[u][:, tk:]
            w = jnp.exp(log_beta + jnp.concatenate(later, axis=1))
            if masked:
                w = jnp.where(mask, w, 0.0)
            o_acc = o_acc + jnp.dot(w.astype(BF16), v_ref[0, ks, :], preferred_element_type=F32)
            return o_acc, acc

        carry = tiles(qi, (jnp.zeros((tq, d), F32), jnp.zeros((tq, tk), F32)), True)
        o_acc, acc = lax.fori_loop(0, qi, lambda it, c: tiles(qi - 1 - it, c, False), carry)
        o_ref[0] = o_acc
        b_ref[0] = acc[:, 0:1]

    q_spec = pl.BlockSpec((1, tq, d), lambda h, i: (h, i, 0))
    kv_spec = pl.BlockSpec((1, s, d), lambda h, i: (h, 0, 0))
    return pl.pallas_call(
        body, name="attn_fwd", grid=(nh, s // tq), in_specs=[q_spec, kv_spec, kv_spec],
        out_specs=(q_spec, pl.BlockSpec((1, tq, 1), lambda h, i: (h, i, 0))),
        out_shape=(jax.ShapeDtypeStruct((nh, s, d), F32), jax.ShapeDtypeStruct((nh, s, 1), F32)),
        compiler_params=_params(("parallel", "parallel")))(qn, kn, v)


def _attn_bwd(qn, kn, v, bsum, do):
    nh, s, d = qn.shape
    tq, tk, r = _attn_blocks(s)

    def body(q_ref, k_ref, v_ref, b_ref, do_ref, dq_ref, dkt_ref, dvt_ref):
        qi = pl.program_id(1)

        @pl.when(qi == 0)
        def _():
            dkt_ref[...] = jnp.zeros_like(dkt_ref)
            dvt_ref[...] = jnp.zeros_like(dvt_ref)

        scale = HEAD_DIM ** -0.5
        qf = q_ref[0].astype(F32) * scale
        qs, qst = qf.astype(BF16), qf.T.astype(BF16)
        dof = do_ref[0]
        dob, dobt = dof.astype(BF16), dof.T.astype(BF16)
        tri_upto = _tri_and_ones(tk, lambda row, col: row <= col)
        tri_before = _tri_and_ones(tk, lambda row, col: row < col)

        def tiles(kq, carry, masked):
            dq_acc, rest, g_prefix = carry
            ks = pl.ds(pl.multiple_of(kq * tq, tq), tq)
            kb, vb = k_ref[0, ks, :], v_ref[0, ks, :]
            log_beta, log_stay, mask = _attn_scores(qs, kb, qi * tq, kq * tq, masked)
            dw = lax.dot_general(dob, vb, _NT, preferred_element_type=F32)
            sums = [_split_dot(log_stay[:, u * tk:(u + 1) * tk], tri_upto) for u in range(r)]
            later = [None] * r
            for u in range(r):
                later[u] = rest - sums[u][:, :tk]
                rest = rest - sums[u][:, tk:]
            w = jnp.exp(log_beta + jnp.concatenate(later, axis=1))
            if masked:
                w = jnp.where(mask, w, 0.0)
            g = w * dw
            gsums = [_split_dot(g[:, u * tk:(u + 1) * tk], tri_before) for u in range(r)]
            before = [None] * r
            for u in range(r):
                before[u] = g_prefix + gsums[u][:, :tk]
                g_prefix = g_prefix + gsums[u][:, tk:]
            dz = g - jnp.exp(log_beta) * (g + jnp.concatenate(before, axis=1))
            if masked:
                dz = jnp.where(mask, dz, 0.0)
            dzb = dz.astype(BF16)
            dq_acc = dq_acc + jnp.dot(dzb, kb, preferred_element_type=F32)
            dkt_ref[0, :, ks] += jnp.dot(qst, dzb, preferred_element_type=F32)
            dvt_ref[0, :, ks] += jnp.dot(dobt, w.astype(BF16), preferred_element_type=F32)
            return dq_acc, rest, g_prefix

        carry = (jnp.zeros((tq, d), F32), jnp.broadcast_to(b_ref[0], (tq, tk)), jnp.zeros((tq, tk), F32))
        carry = lax.fori_loop(0, qi, lambda kq, c: tiles(kq, c, False), carry)
        dq_acc, _, _ = tiles(qi, carry, True)
        dq_ref[0] = dq_acc * scale

    q_spec = pl.BlockSpec((1, tq, d), lambda h, i: (h, i, 0))
    kv_spec = pl.BlockSpec((1, s, d), lambda h, i: (h, 0, 0))
    t_spec = pl.BlockSpec((1, d, s), lambda h, i: (h, 0, 0))
    b_spec = pl.BlockSpec((1, tq, 1), lambda h, i: (h, i, 0))
    full = jax.ShapeDtypeStruct((nh, s, d), F32)
    full_t = jax.ShapeDtypeStruct((nh, d, s), F32)
    return pl.pallas_call(
        body, name="attn_bwd", grid=(nh, s // tq), in_specs=[q_spec, kv_spec, kv_spec, b_spec, q_spec],
        out_specs=(q_spec, t_spec, t_spec), out_shape=(full, full_t, full_t),
        compiler_params=_params(("parallel", "arbitrary")))(qn, kn, v, bsum, do)


def _reduce_adamw(parts, w, m, v, *, name, row0=0):
    n, _, cols = parts.shape
    rows = w.shape[0]
    tile_rows = 2 * SUBLANES
    tr = max(t for t in range(tile_rows, min(rows, 512) + 1, tile_rows) if rows % t == 0 and row0 % t == 0)
    c1 = 1.0 - ADAM_B1 ** ADAM_STEP
    c2 = 1.0 - ADAM_B2 ** ADAM_STEP

    def body(p_ref, w_ref, m_ref, v_ref, g_ref, d_ref, nm_ref, nv_ref):
        g = p_ref[0].astype(F32)
        for i in range(1, n):
            g = g + p_ref[i].astype(F32)
        nm = ADAM_B1 * m_ref[...] + (1.0 - ADAM_B1) * g
        nv = ADAM_B2 * v_ref[...] + (1.0 - ADAM_B2) * (g * g)
        g_ref[...] = g
        nm_ref[...] = nm
        nv_ref[...] = nv
        d_ref[...] = -ADAM_LR * ((nm / c1) / (jnp.sqrt(nv / c2) + ADAM_EPS) + ADAM_WD * w_ref[...])

    row = pl.BlockSpec((tr, cols), lambda i: (i, 0))
    out = jax.ShapeDtypeStruct((rows, cols), F32)
    return pl.pallas_call(
        body, name=name, grid=(rows // tr,),
        in_specs=[pl.BlockSpec((n, tr, cols), lambda i: (0, row0 // tr + i, 0)), row, row, row],
        out_specs=(row,) * 4, out_shape=(out,) * 4, compiler_params=_params(("parallel",)))(parts, w, m, v)


def _all_gather(shards, *, name):
    n_arr = len(shards)

    def body(*refs):
        x_refs, out_refs = refs[:n_arr], refs[n_arr:2 * n_arr]
        send_sems, recv_sems, local_sems = refs[2 * n_arr:]
        x, y, c = lax.axis_index("x"), lax.axis_index("y"), lax.axis_index("c")
        me, sibling = (x, y, c), (x, y, 1 - c)
        chips = [(1 - x, y), (x, 1 - y), (1 - x, 1 - y)]

        def slot(a, px, py, pc):
            return out_refs[a].at[4 * px + 2 * py + pc]

        def copy(a, k, block, to, src=None):
            return pltpu.make_async_remote_copy(
                src_ref=slot(a, *block) if src is None else src, dst_ref=slot(a, *block),
                send_sem=send_sems.at[a, k], recv_sem=recv_sems.at[a, k], device_id=to, device_id_type=MESH_IDS)

        arrays = range(n_arr)
        mine = [pltpu.make_async_copy(x_refs[a], slot(a, *me), local_sems.at[a]) for a in arrays]
        for cp in mine:
            cp.start()
        first = [copy(a, 0, me, sibling, src=x_refs[a]) for a in arrays]
        first += [copy(a, 1 + j, me, (*chip, c), src=x_refs[a]) for j, chip in enumerate(chips) for a in arrays]
        for cp in first:
            cp.start()
        passed = []
        for j, chip in enumerate(chips):
            for a in arrays:
                copy(a, 1 + j, (*chip, c), me).wait_recv()
                passed.append(copy(a, 4 + j, (*chip, c), sibling))
                passed[-1].start()
        for a in arrays:
            copy(a, 0, sibling, me).wait_recv()
        for j, chip in enumerate(chips):
            for a in arrays:
                copy(a, 4 + j, (*chip, 1 - c), me).wait_recv()
        for cp in first + passed:
            cp.wait_send()
        for cp in mine:
            cp.wait()

    any_spec = pl.BlockSpec(memory_space=pl.ANY)
    return pl.pallas_call(
        body, name=name, out_shape=tuple(jax.ShapeDtypeStruct((N_DEV,) + s.shape, s.dtype) for s in shards),
        in_specs=[any_spec] * n_arr, out_specs=(any_spec,) * n_arr,
        scratch_shapes=[pltpu.SemaphoreType.DMA((n_arr, 7)), pltpu.SemaphoreType.DMA((n_arr, 7)),
                        pltpu.SemaphoreType.DMA((n_arr,))],
    )(*shards)


def _all_to_all(bufs, *, name):
    n_arr = len(bufs)

    def body(*refs):
        x_refs, out_refs = refs[:n_arr], refs[n_arr:2 * n_arr]
        send_sems, recv_sems, local_sems = refs[2 * n_arr:]
        x, y, c = lax.axis_index("x"), lax.axis_index("y"), lax.axis_index("c")
        me = 4 * x + 2 * y + c
        mine = [pltpu.make_async_copy(x_refs[a].at[me], out_refs[a].at[me], local_sems.at[a]) for a in range(n_arr)]
        for cp in mine:
            cp.start()

        def copy(a, k, landing):
            px, py, pc = x ^ (k >> 2), y ^ ((k >> 1) & 1), c ^ (k & 1)
            peer = 4 * px + 2 * py + pc
            return pltpu.make_async_remote_copy(
                src_ref=x_refs[a].at[peer], dst_ref=out_refs[a].at[peer if landing else me],
                send_sem=send_sems.at[a, k - 1], recv_sem=recv_sems.at[a, k - 1],
                device_id=(px, py, pc), device_id_type=MESH_IDS)

        copies = [copy(a, k, False) for k in range(1, N_DEV) for a in range(n_arr)]
        for cp in copies:
            cp.start()
        for k in range(1, N_DEV):
            for a in range(n_arr):
                copy(a, k, True).wait_recv()
        for cp in copies:
            cp.wait_send()
        for cp in mine:
            cp.wait()

    any_spec = pl.BlockSpec(memory_space=pl.ANY)
    return pl.pallas_call(
        body, name=name, out_shape=tuple(jax.ShapeDtypeStruct(b.shape, b.dtype) for b in bufs),
        in_specs=[any_spec] * n_arr, out_specs=(any_spec,) * n_arr,
        scratch_shapes=[pltpu.SemaphoreType.DMA((n_arr, 7)), pltpu.SemaphoreType.DMA((n_arr, 7)),
                        pltpu.SemaphoreType.DMA((n_arr,))],
    )(*bufs)


def _pack_small(vals):
    flat = jnp.concatenate([vals[n].reshape(-1) for n in SMALL])
    rows = -(-flat.shape[0] // (SUBLANES * PACK_COLS)) * SUBLANES
    return jnp.pad(flat, (0, rows * PACK_COLS - flat.shape[0])).reshape(rows, PACK_COLS)


def _unpack_small(flat, shapes):
    flat = flat.reshape(-1)
    out, off = {}, 0
    for n in SMALL:
        size = math.prod(shapes[n])
        out[n] = flat[off:off + size].reshape(shapes[n])
        off += size
    return out


def _heads(t2d):
    s = t2d.shape[0]
    return t2d.reshape(s, ATTN_HEADS, HEAD_DIM).transpose(1, 0, 2)


def _unheads(t3d):
    return t3d.transpose(1, 0, 2).reshape(t3d.shape[1], ATTN_WIDTH)


def _ssm_setup(sp):
    g, p = SSM_GROUPS, SSM_STATE
    a_re = sp["ssm_a_re"][:, None, :]
    a_im = sp["ssm_a_im"][:, None, :]
    log_dt = jnp.broadcast_to(sp["ssm_log_dt"][:, None, None], (g, 1, p))
    bt_re = sp["ssm_b_re"].transpose(0, 2, 1)
    bt_im = sp["ssm_b_im"].transpose(0, 2, 1)
    return a_re, a_im, log_dt, bt_re, bt_im


def _layer_fwd(h, p_l, w, sp):
    s = h.shape[0]
    hn = _rms_fwd(h, sp["mix_norm_g"][None], name="rms_mix")
    proj = _mm(hn, w["w_in"], name="mm_proj")

    pw_re, pw_im, bbt_re, bbt_im = _ssm_param_fwd(*_ssm_setup(sp))
    pw_re = pw_re.transpose(1, 0, 2).reshape(SUBLANES, SSM_LANES)
    pw_im = pw_im.transpose(1, 0, 2).reshape(SUBLANES, SSM_LANES)
    ssm_mats = (_block_diag(bbt_re).astype(BF16), _block_diag(bbt_im).astype(BF16),
                _block_diag(sp["ssm_c_re"]).astype(BF16), _block_diag(-sp["ssm_c_im"]).astype(BF16),
                pw_re, pw_im, sp["ssm_d"].reshape(1, SSM_WIDTH))
    x_re, x_im, y, z = _ssm_fwd(proj, *ssm_mats)
    zz = _mm(z, w["ssm_w_glu"], name="mm_glu")

    q3 = _heads(proj[:, 2 * SSM_WIDTH:2 * SSM_WIDTH + ATTN_WIDTH]).reshape(ATTN_HEADS * s, HEAD_DIM)
    k3 = _heads(proj[:, 2 * SSM_WIDTH + ATTN_WIDTH:2 * SSM_WIDTH + 2 * ATTN_WIDTH]).reshape(ATTN_HEADS * s, HEAD_DIM)
    v3 = _heads(proj[:, 2 * SSM_WIDTH + 2 * ATTN_WIDTH:2 * SSM_WIDTH + 3 * ATTN_WIDTH]).astype(BF16)
    qn = _rms_fwd(q3, sp["q_norm_g"][None], name="rms_q").reshape(ATTN_HEADS, s, HEAD_DIM)
    kn = _rms_fwd(k3, sp["k_norm_g"][None], name="rms_k").reshape(ATTN_HEADS, s, HEAD_DIM)
    o3, bsum = _attn_fwd(qn, kn, v3)
    o = _unheads(o3)

    bglu = sp["ssm_b_glu"][None]
    ycat = _gate_fwd(zz, bglu, proj, o)
    h2 = _mm(ycat, w["w_out"], add=h, name="mm_out")
    hn2 = _rms_fwd(h2, sp["ple_norm_g"][None], name="rms_ple")
    pgl = _mm(hn2, w["w_ple_gate"], name="mm_ple_gate")
    pp = _mm(p_l, w["w_ple_proj"], name="mm_ple_proj")
    h3 = _ple_out(h2, pgl, pp)
    saved = dict(h=h, hn=hn, proj=proj, ssm_mats=ssm_mats, x_re=x_re, x_im=x_im, y=y, z=z,
                 zz=zz, q3=q3, k3=k3, v3=v3, qn=qn, kn=kn, bsum=bsum, o=o, ycat=ycat, h2=h2, hn2=hn2,
                 pgl=pgl, pp=pp, p_l=p_l, bglu=bglu)
    return h3, saved


def _layer_bwd(dh3, sv, w, sp):
    s = dh3.shape[0]
    g, hh, p = SSM_GROUPS, SSM_GROUP, SSM_STATE
    grads = {}
    dgp, dpp = _ple_bwd(dh3, sv["pgl"], sv["pp"])
    grads["w_ple_gate"] = _mm(sv["hn2"], dgp, ta=True, out_dtype=BF16, name="mm_d_ple_gate")
    grads["w_ple_proj"] = _mm(sv["p_l"], dpp, ta=True, out_dtype=BF16, name="mm_d_ple_proj")
    dhn2 = _mm(dgp, w["w_ple_gate"], tb=True, name="mm_dhn2")
    dh2, dg2 = _rms_bwd(sv["h2"], sp["ple_norm_g"][None], dhn2, dres=dh3, name="rms_ple_bwd")
    grads["ple_norm_g"] = dg2[0]
    grads["w_out"] = _mm(sv["ycat"], dh2, ta=True, out_dtype=BF16, name="mm_d_out")
    dyc = _mm(dh2, w["w_out"], tb=True, name="mm_dycat")
    dzz, dgs, do, dga, dbglu = _gate_bwd(dyc, sv["zz"], sv["bglu"], sv["proj"], sv["o"])
    grads["ssm_b_glu"] = dbglu[0]

    dqn, dkn_t, dv_t = _attn_bwd(sv["qn"], sv["kn"], sv["v3"], sv["bsum"], _heads(do))
    dkn = dkn_t.transpose(0, 2, 1)
    dq3, dgq = _rms_bwd(sv["q3"], sp["q_norm_g"][None], dqn.reshape(ATTN_HEADS * s, HEAD_DIM), name="rms_q_bwd")
    dk3, dgk = _rms_bwd(sv["k3"], sp["k_norm_g"][None], dkn.reshape(ATTN_HEADS * s, HEAD_DIM), name="rms_k_bwd")
    grads["q_norm_g"], grads["k_norm_g"] = dgq[0], dgk[0]
    dq = _unheads(dq3.reshape(ATTN_HEADS, s, HEAD_DIM))
    dk = _unheads(dk3.reshape(ATTN_HEADS, s, HEAD_DIM))
    dv = dv_t.transpose(2, 0, 1).reshape(s, ATTN_WIDTH)

    grads["ssm_w_glu"] = _mm(sv["z"], dzz, ta=True, out_dtype=BF16, name="mm_d_glu")
    dz = _mm(dzz, w["ssm_w_glu"], tb=True, name="mm_dz")
    du, dbc_re, dbc_im, dcc_re, dcc_im, gab_re, gab_im, dd = _ssm_bwd(
        dz, sv["y"], sv["proj"], sv["x_re"], sv["x_im"], *sv["ssm_mats"])
    grads["ssm_d"] = dd.reshape(g, hh)
    grads["ssm_c_re"] = _block_diag_take(dcc_re, hh, p)
    grads["ssm_c_im"] = -_block_diag_take(dcc_im, hh, p)
    da_re, da_im, dldt, dbt_re, dbt_im = _ssm_param_bwd(
        *_ssm_setup(sp), gab_re.reshape(g, 1, p), gab_im.reshape(g, 1, p),
        _block_diag_take(dbc_re, hh, p), _block_diag_take(dbc_im, hh, p))
    grads["ssm_a_re"], grads["ssm_a_im"] = da_re[:, 0, :], da_im[:, 0, :]
    grads["ssm_log_dt"] = dldt[:, 0, 0]
    grads["ssm_b_re"], grads["ssm_b_im"] = dbt_re.transpose(0, 2, 1), dbt_im.transpose(0, 2, 1)

    dproj = jnp.concatenate([du, dgs, dq, dk, dv, dga], axis=1)
    grads["w_in"] = _mm(sv["hn"], dproj, ta=True, out_dtype=BF16, name="mm_d_in")
    dhn = _mm(dproj, w["w_in"], tb=True, name="mm_dhn")
    dh, dg1 = _rms_bwd(sv["h"], sp["mix_norm_g"][None], dhn, dres=dh2, name="rms_mix_bwd")
    grads["mix_norm_g"] = dg1[0]
    return dh, grads


def _local_step(x, p, target, big, small):
    h = x
    saved = []
    for l in range(DEPTH):
        sp = {n: small[n][l] for n in SMALL}
        h, sv = _layer_fwd(h, p[l], big[l], sp)
        saved.append(sv)
    dh, loss_parts = _loss_head(h, target)
    grads = [None] * DEPTH
    for l in reversed(range(DEPTH)):
        sp = {n: small[n][l] for n in SMALL}
        dh, grads[l] = _layer_bwd(dh, saved[l], big[l], sp)
    return jnp.sum(loss_parts), dh, grads


def kernel(x, p, mix_norm_g, w_in, ssm_a_re, ssm_a_im, ssm_log_dt, ssm_b_re, ssm_b_im, ssm_c_re, ssm_c_im, ssm_d, ssm_w_glu, ssm_b_glu, q_norm_g, k_norm_g, w_out, ple_norm_g, w_ple_gate, w_ple_proj, loss_target, m_mix_norm_g, m_w_in, m_ssm_a_re, m_ssm_a_im, m_ssm_log_dt, m_ssm_b_re, m_ssm_b_im, m_ssm_c_re, m_ssm_c_im, m_ssm_d, m_ssm_w_glu, m_ssm_b_glu, m_q_norm_g, m_k_norm_g, m_w_out, m_ple_norm_g, m_w_ple_gate, m_w_ple_proj, v_mix_norm_g, v_w_in, v_ssm_a_re, v_ssm_a_im, v_ssm_log_dt, v_ssm_b_re, v_ssm_b_im, v_ssm_c_re, v_ssm_c_im, v_ssm_d, v_ssm_w_glu, v_ssm_b_glu, v_q_norm_g, v_k_norm_g, v_w_out, v_ple_norm_g, v_w_ple_gate, v_w_ple_proj):
    given = dict(locals())
    wts = {n: given[n] for n in WEIGHTS}
    mom = {n: given["m_" + n] for n in WEIGHTS}
    var = {n: given["v_" + n] for n in WEIGHTS}

    def rows2d(a):
        return a.reshape(-1, a.shape[-1])

    groups = (("w_in",), ("ssm_w_glu", "w_ple_proj"), ("w_out", "w_ple_gate"))
    shards = [jnp.concatenate([rows2d(wts[n]).astype(BF16) for n in names], axis=0) for names in groups]
    gathered = _all_gather(shards, name="gather_weights")
    big = [{} for _ in range(DEPTH)]
    for names, got in zip(groups, gathered):
        row0 = 0
        for n in names:
            r = wts[n].shape[1]
            for l in range(DEPTH):
                blocks = got[:, row0:row0 + r, :]
                if n in ROW_SHARDED:
                    big[l][n] = blocks.reshape(N_DEV * r, -1)
                else:
                    big[l][n] = blocks.transpose(1, 0, 2).reshape(r, -1)
                row0 += r

    small = {n: wts[n] for n in SMALL}
    loss, dx, grads = _local_step(x[0], p[:, 0], loss_target[0], big, small)
    loss = lax.psum(loss, ("x", "y", "c"))

    def blocks_of(n, l):
        g = grads[l][n]
        if n in ROW_SHARDED:
            return g.reshape(N_DEV, -1, g.shape[1])
        return g.reshape(g.shape[0], N_DEV, -1).transpose(1, 0, 2)

    send = [jnp.concatenate([blocks_of(n, l) for n in names for l in range(DEPTH)], axis=1) for names in groups]
    parts = _all_to_all(send, name="exchange_weight_grads")
    big_out = {}
    for names, part in zip(groups, parts):
        row0 = 0
        for n in names:
            res = _reduce_adamw(part, rows2d(wts[n]), rows2d(mom[n]), rows2d(var[n]), row0=row0, name="adamw_" + n)
            big_out[n] = [a.reshape(wts[n].shape) for a in res]
            row0 += rows2d(wts[n]).shape[0]

    small_shapes = {n: wts[n].shape for n in SMALL}
    local_small = _pack_small({n: jnp.stack([grads[l][n] for l in range(DEPTH)]) for n in SMALL})
    (all_small,) = _all_gather([local_small], name="gather_small_grads")
    small_out = _reduce_adamw(all_small, _pack_small(small), _pack_small({n: mom[n] for n in SMALL}),
                              _pack_small({n: var[n] for n in SMALL}), name="adamw_replicated")
    small_out = [_unpack_small(a, small_shapes) for a in small_out]

    outs = [loss, dx[None]]
    for k in range(4):
        outs += [big_out[n][k] if n in BIG else small_out[k][n] for n in WEIGHTS]
    return tuple(outs)
```

```python
import math

import jax
import jax.numpy as jnp
from jax import lax
from jax.experimental import pallas as pl
from jax.experimental.pallas import tpu as pltpu

F32 = jnp.float32
BF16 = jnp.bfloat16
MESH_IDS = pl.DeviceIdType.MESH

N_DEV = 8
D_MODEL = 1024
DEPTH = 2
PLE_DIM = 256
SSM_WIDTH = 512
SSM_GROUP = 16
SSM_GROUPS = 32
SSM_STATE = 64
SSM_LANES = SSM_GROUPS * SSM_STATE
ATTN_WIDTH = 512
ATTN_HEADS = 8
HEAD_DIM = 64
RMS_EPS = 1e-6
ADAM_LR = 0.001
ADAM_B1 = 0.9
ADAM_B2 = 0.999
ADAM_EPS = 1e-08
ADAM_WD = 0.01
ADAM_STEP = 10

VMEM_LIMIT = 56 * 1024 * 1024
ATT_TQ = 512
ATT_TK = 128
SSM_BLOCK_GROUPS = 8
SSM_TIME_BLOCK = 512
SUBLANES = 8
PACK_COLS = 1024

BIG = ("w_in", "ssm_w_glu", "w_out", "w_ple_gate", "w_ple_proj")
ROW_SHARDED = ("w_out", "w_ple_gate")
SMALL = ("mix_norm_g", "ssm_a_re", "ssm_a_im", "ssm_log_dt", "ssm_b_re", "ssm_b_im", "ssm_c_re",
         "ssm_c_im", "ssm_d", "ssm_b_glu", "q_norm_g", "k_norm_g", "ple_norm_g")
WEIGHTS = ("mix_norm_g", "w_in", "ssm_a_re", "ssm_a_im", "ssm_log_dt", "ssm_b_re", "ssm_b_im",
           "ssm_c_re", "ssm_c_im", "ssm_d", "ssm_w_glu", "ssm_b_glu", "q_norm_g", "k_norm_g",
           "w_out", "ple_norm_g", "w_ple_gate", "w_ple_proj")
_NT = (((1,), (1,)), ((), ()))
_TN = (((0,), (0,)), ((), ()))


def _params(sem=None):
    return pltpu.CompilerParams(dimension_semantics=sem, vmem_limit_bytes=VMEM_LIMIT)


def _sigmoid(x):
    return 1.0 / (1.0 + jnp.exp(-x))


_GELU_K = math.sqrt(2.0 / math.pi)
_GELU_C = 0.044715


def _gelu(x):
    return 0.5 * x * (1.0 + jnp.tanh(_GELU_K * (x + _GELU_C * x * x * x)))


def _gelu_grad(x):
    th = jnp.tanh(_GELU_K * (x + _GELU_C * x * x * x))
    return 0.5 * (1.0 + th) + 0.5 * x * (1.0 - th * th) * _GELU_K * (1.0 + 3.0 * _GELU_C * x * x)


def _mm_call(a, b, add, a_spec, b_spec, o_spec, grid, out_shape, acc_shape, dims, name):
    n_steps = grid[2]
    out_dtype = out_shape.dtype

    def body(*refs):
        if add is None:
            a_ref, b_ref, o_ref, acc_ref = refs
        else:
            a_ref, b_ref, r_ref, o_ref, acc_ref = refs
        kk = pl.program_id(2)

        @pl.when(kk == 0)
        def _():
            acc_ref[...] = jnp.zeros_like(acc_ref)

        acc_ref[...] += lax.dot_general(a_ref[...].astype(BF16), b_ref[...].astype(BF16), dims,
                                        preferred_element_type=F32)

        @pl.when(kk == n_steps - 1)
        def _():
            res = acc_ref[...]
            if add is not None:
                res = res + r_ref[...].astype(F32)
            o_ref[...] = res.astype(out_dtype)

    ins, specs = [a, b], [a_spec, b_spec]
    if add is not None:
        ins.append(add)
        specs.append(o_spec)
    return pl.pallas_call(
        body, name=name, grid=grid, in_specs=specs, out_specs=o_spec, out_shape=out_shape,
        scratch_shapes=[pltpu.VMEM(acc_shape, F32)],
        compiler_params=_params(("parallel", "parallel", "arbitrary")),
    )(*ins)


def _mm(a, b, *, name, ta=False, tb=False, add=None, out_dtype=F32, tm=1024, tn=1024, tk=512):
    m, k = (a.shape[1], a.shape[0]) if ta else a.shape
    n = b.shape[0] if tb else b.shape[1]
    tm, tn, tk = min(tm, m), min(tn, n), min(tk, k)
    assert m % tm == 0 and n % tn == 0 and k % tk == 0, (name, a.shape, b.shape)
    dims = (((0 if ta else 1,), (1 if tb else 0,)), ((), ()))
    a_spec = (pl.BlockSpec((tk, tm), lambda i, j, kk: (kk, i)) if ta
              else pl.BlockSpec((tm, tk), lambda i, j, kk: (i, kk)))
    b_spec = (pl.BlockSpec((tn, tk), lambda i, j, kk: (j, kk)) if tb
              else pl.BlockSpec((tk, tn), lambda i, j, kk: (kk, j)))
    o_spec = pl.BlockSpec((tm, tn), lambda i, j, kk: (i, j))
    return _mm_call(a, b, add, a_spec, b_spec, o_spec, (m // tm, n // tn, k // tk),
                    jax.ShapeDtypeStruct((m, n), out_dtype), (tm, tn), dims, name)


def _rows_tile(rows, want=512):
    t = min(rows, want)
    assert rows % t == 0
    return t


def _rowwise(body, name, rows, ins, outs):
    widest = max(cols for _, cols, _ in ins)
    tm = _rows_tile(rows, 512 if widest > 128 else 4096)
    in_specs = []
    for _, cols, cb in ins:
        if cb is None:
            in_specs.append(pl.BlockSpec((1, cols), lambda i: (0, 0)))
        else:
            in_specs.append(pl.BlockSpec((tm, cols), lambda i, cb=cb: (i, cb)))
    out_specs, out_shapes = [], []
    for cols, dtype, is_acc in outs:
        if is_acc:
            out_specs.append(pl.BlockSpec((1, cols), lambda i: (0, 0)))
            out_shapes.append(jax.ShapeDtypeStruct((1, cols), dtype))
        else:
            out_specs.append(pl.BlockSpec((tm, cols), lambda i: (i, 0)))
            out_shapes.append(jax.ShapeDtypeStruct((rows, cols), dtype))
    any_acc = any(o[2] for o in outs)
    return pl.pallas_call(
        body, name=name, grid=(rows // tm,), in_specs=in_specs, out_specs=tuple(out_specs),
        out_shape=tuple(out_shapes),
        compiler_params=_params(("arbitrary",) if any_acc else ("parallel",)))(*[a for a, _, _ in ins])


def _accumulate(ref, value):
    @pl.when(pl.program_id(0) == 0)
    def _():
        ref[...] = jnp.zeros_like(ref)

    ref[...] += value


def _rms_fwd(h, g, *, name):
    rows, d = h.shape

    def body(h_ref, g_ref, o_ref):
        hv = h_ref[...]
        r = lax.rsqrt(jnp.mean(hv * hv, axis=-1, keepdims=True) + RMS_EPS)
        o_ref[...] = (hv * r * g_ref[...]).astype(BF16)

    return _rowwise(body, name, rows, [(h, d, 0), (g, d, None)], [(d, BF16, False)])[0]


def _rms_bwd(h, g, dhn, *, name, dres=None):
    rows, d = h.shape

    def body(*refs):
        if dres is None:
            h_ref, g_ref, dn_ref, dh_ref, dg_ref = refs
        else:
            h_ref, g_ref, dn_ref, dr_ref, dh_ref, dg_ref = refs
        hv = h_ref[...]
        dn = dn_ref[...].astype(F32)
        r = lax.rsqrt(jnp.mean(hv * hv, axis=-1, keepdims=True) + RMS_EPS)
        a = dn * g_ref[...]
        dot = jnp.mean(a * hv, axis=-1, keepdims=True)
        dh = r * a - hv * (r * r * r * dot)
        if dres is not None:
            dh = dh + dr_ref[...]
        dh_ref[...] = dh
        _accumulate(dg_ref, jnp.sum(dn * hv * r, axis=0, keepdims=True))

    ins = [(h, d, 0), (g, d, None), (dhn, d, 0)]
    if dres is not None:
        ins.append((dres, d, 0))
    return _rowwise(body, name, rows, ins, [(d, F32, False), (d, F32, True)])


def _gate_fwd(zz, bglu, proj, o):
    rows = zz.shape[0]
    w = SSM_WIDTH

    def body(zz_ref, b_ref, gs_ref, o_ref, ga_ref, y_ref):
        zz_v = zz_ref[...] + b_ref[...]
        val, gate = zz_v[:, :w], zz_v[:, w:]
        gs, ga = gs_ref[...], ga_ref[...]
        y_ref[:, :w] = (val * _sigmoid(gate) * (gs * _sigmoid(gs))).astype(BF16)
        y_ref[:, w:] = (o_ref[...] * (ga * _sigmoid(ga))).astype(BF16)

    return _rowwise(body, "gate_fwd", rows,
                    [(zz, 2 * w, 0), (bglu, 2 * w, None), (proj, w, 1), (o, w, 0), (proj, w, 5)],
                    [(2 * w, BF16, False)])[0]


def _gate_bwd(dyc, zz, bglu, proj, o):
    rows = zz.shape[0]
    w = SSM_WIDTH

    def body(dy_ref, zz_ref, b_ref, gs_ref, o_ref, ga_ref, dzz_ref, dgs_ref, do_ref, dga_ref, db_ref):
        zz_v = zz_ref[...] + b_ref[...]
        val, gate = zz_v[:, :w], zz_v[:, w:]
        gs, ga = gs_ref[...], ga_ref[...]
        dys, dya = dy_ref[:, :w], dy_ref[:, w:]
        sg, ss, sa = _sigmoid(gate), _sigmoid(gs), _sigmoid(ga)
        glu = val * sg
        dglu = dys * (gs * ss)
        dgs_ref[...] = dys * glu * (ss * (1.0 + gs * (1.0 - ss)))
        dval = dglu * sg
        dgate = dglu * val * sg * (1.0 - sg)
        dzz_ref[:, :w] = dval.astype(BF16)
        dzz_ref[:, w:] = dgate.astype(BF16)
        do_ref[...] = dya * (ga * sa)
        dga_ref[...] = dya * o_ref[...] * (sa * (1.0 + ga * (1.0 - sa)))
        _accumulate(db_ref, jnp.concatenate([jnp.sum(dval, axis=0, keepdims=True),
                                             jnp.sum(dgate, axis=0, keepdims=True)], axis=1))

    return _rowwise(body, "gate_bwd", rows,
                    [(dyc, 2 * w, 0), (zz, 2 * w, 0), (bglu, 2 * w, None), (proj, w, 1), (o, w, 0), (proj, w, 5)],
                    [(2 * w, BF16, False), (w, F32, False), (w, F32, False), (w, F32, False), (2 * w, F32, True)])


def _ple_out(h2, pgl, pp):
    rows, d = h2.shape

    def body(h_ref, g_ref, p_ref, o_ref):
        o_ref[...] = h_ref[...] + _sigmoid(g_ref[...]) * p_ref[...]

    return _rowwise(body, "ple_out", rows, [(h2, d, 0), (pgl, d, 0), (pp, d, 0)], [(d, F32, False)])[0]


def _ple_bwd(dh3, pgl, pp):
    rows, d = dh3.shape

    def body(dh_ref, g_ref, p_ref, dg_ref, dp_ref):
        sg = _sigmoid(g_ref[...])
        dh = dh_ref[...]
        dg_ref[...] = (dh * p_ref[...] * sg * (1.0 - sg)).astype(BF16)
        dp_ref[...] = (dh * sg).astype(BF16)

    return _rowwise(body, "ple_bwd", rows, [(dh3, d, 0), (pgl, d, 0), (pp, d, 0)],
                    [(d, BF16, False), (d, BF16, False)])


def _loss_head(h, target):
    rows, d = h.shape

    def body(h_ref, t_ref, dh_ref, l_ref):
        err = h_ref[...] - t_ref[...]
        dh_ref[...] = err * (1.0 / d)
        _accumulate(l_ref, jnp.sum(err * err, axis=0, keepdims=True) * (0.5 / d))

    return _rowwise(body, "loss_head", rows, [(h, d, 0), (target, d, 0)], [(d, F32, False), (d, F32, True)])


def _zoh(lr, li, dt):
    mag = jnp.exp(lr * dt)
    ab_re = mag * jnp.cos(li * dt)
    ab_im = mag * jnp.sin(li * dt)
    num_re = ab_re - 1.0
    den = lr * lr + li * li
    f_re = (num_re * lr + ab_im * li) / den
    f_im = (ab_im * lr - num_re * li) / den
    return ab_re, ab_im, f_re, f_im, den


def _ssm_param_fwd(a_re, a_im, log_dt, bt_re, bt_im):
    g, _, p = a_re.shape
    h = bt_re.shape[1]

    def body(lr_ref, li_ref, ldt_ref, br_ref, bi_ref, pr_ref, pi_ref, bbr_ref, bbi_ref):
        dt = jnp.exp(ldt_ref[...])
        ab_re, ab_im, f_re, f_im, _ = _zoh(lr_ref[...], li_ref[...], dt)
        cr, ci = ab_re, ab_im
        for k in range(SUBLANES):
            pr_ref[:, k:k + 1, :] = cr
            pi_ref[:, k:k + 1, :] = ci
            cr, ci = cr * ab_re - ci * ab_im, cr * ab_im + ci * ab_re
        br, bi = br_ref[...], bi_ref[...]
        bbr_ref[...] = f_re * br - f_im * bi
        bbi_ref[...] = f_re * bi + f_im * br

    pw = jax.ShapeDtypeStruct((g, SUBLANES, p), F32)
    large = jax.ShapeDtypeStruct((g, h, p), F32)
    return pl.pallas_call(body, name="ssm_param_fwd", out_shape=(pw, pw, large, large),
                          compiler_params=_params())(a_re, a_im, log_dt, bt_re, bt_im)


def _ssm_param_bwd(a_re, a_im, log_dt, bt_re, bt_im, gab_re, gab_im, gbb_re, gbb_im):
    g, _, p = a_re.shape
    h = bt_re.shape[1]

    def body(lr_ref, li_ref, ldt_ref, br_ref, bi_ref, gar_ref, gai_ref, gbr_ref, gbi_ref,
             dlr_ref, dli_ref, dldt_ref, dbr_ref, dbi_ref):
        lr, li = lr_ref[...], li_ref[...]
        dt = jnp.exp(ldt_ref[...])
        ab_re, ab_im, f_re, f_im, den = _zoh(lr, li, dt)
        br, bi = br_ref[...], bi_ref[...]
        gbr, gbi = gbr_ref[...], gbi_ref[...]
        dbr_ref[...] = f_re * gbr + f_im * gbi
        dbi_ref[...] = f_re * gbi - f_im * gbr
        gf_re = jnp.sum(br * gbr + bi * gbi, axis=1, keepdims=True)
        gf_im = jnp.sum(br * gbi - bi * gbr, axis=1, keepdims=True)
        il_re, il_im = lr / den, -li / den
        ga_re = gar_ref[...] + il_re * gf_re + il_im * gf_im
        ga_im = gai_ref[...] + il_re * gf_im - il_im * gf_re
        q_re = f_re * il_re - f_im * il_im
        q_im = f_re * il_im + f_im * il_re
        gl_re = -(q_re * gf_re + q_im * gf_im)
        gl_im = -(q_re * gf_im - q_im * gf_re)
        dlr_ref[...] = gl_re + dt * (ab_re * ga_re + ab_im * ga_im)
        dli_ref[...] = gl_im + dt * (ab_re * ga_im - ab_im * ga_re)
        la_re = lr * ab_re - li * ab_im
        la_im = lr * ab_im + li * ab_re
        dldt_ref[...] = jnp.sum((la_re * ga_re + la_im * ga_im) * dt, axis=2, keepdims=True)

    small = jax.ShapeDtypeStruct((g, 1, p), F32)
    one = jax.ShapeDtypeStruct((g, 1, 1), F32)
    large = jax.ShapeDtypeStruct((g, h, p), F32)
    return pl.pallas_call(body, name="ssm_param_bwd", out_shape=(small, small, one, large, large),
                          compiler_params=_params())(
                              a_re, a_im, log_dt, bt_re, bt_im, gab_re, gab_im, gbb_re, gbb_im)


def _block_diag(m):
    g, h, p = m.shape
    nb = g // SSM_BLOCK_GROUPS
    eye = jnp.eye(SSM_BLOCK_GROUPS, dtype=m.dtype)
    m4 = m.reshape(nb, SSM_BLOCK_GROUPS, h, p)
    return (m4[:, :, :, None, :] * eye[None, :, None, :, None]).reshape(nb, SSM_BLOCK_GROUPS * h, SSM_BLOCK_GROUPS * p)


def _block_diag_take(m, h, p):
    nb = m.shape[0]
    eye = jnp.eye(SSM_BLOCK_GROUPS, dtype=m.dtype)
    m5 = m.reshape(nb, SSM_BLOCK_GROUPS, h, SSM_BLOCK_GROUPS, p)
    return jnp.sum(m5 * eye[None, :, None, :, None], axis=3).reshape(nb * SSM_BLOCK_GROUPS, h, p)


def _scan_tile(br, bi, pw_re, pw_im, reverse):
    row = lax.broadcasted_iota(jnp.int32, br.shape, 0)
    xr, xi = br, bi
    for d in (1, 2, 4):
        ar, ai = pw_re[d - 1:d, :], pw_im[d - 1:d, :]
        if reverse:
            keep = row < SUBLANES - d
            sr = jnp.where(keep, pltpu.roll(xr, SUBLANES - d, 0), 0.0)
            si = jnp.where(keep, pltpu.roll(xi, SUBLANES - d, 0), 0.0)
        else:
            keep = row >= d
            sr = jnp.where(keep, pltpu.roll(xr, d, 0), 0.0)
            si = jnp.where(keep, pltpu.roll(xi, d, 0), 0.0)
        xr, xi = xr + ar * sr - ai * si, xi + ar * si + ai * sr
    return xr, xi


def _ssm_blocks(s):
    tt = min(SSM_TIME_BLOCK, s)
    assert s % tt == 0 and tt % SUBLANES == 0
    ch = SSM_BLOCK_GROUPS * SSM_GROUP
    st = SSM_BLOCK_GROUPS * SSM_STATE
    return tt, s // tt, SSM_WIDTH // ch, ch, st


def _ssm_fwd(proj, bc_re, bc_im, cc_re, cc_im, pw_re, pw_im, dvec):
    s = proj.shape[0]
    tt, nt, nb, ch, st = _ssm_blocks(s)
    n_tiles = tt // SUBLANES

    def body(u_ref, bre_ref, bim_ref, cre_ref, cim_ref, pr_ref, pi_ref, d_ref,
             xr_ref, xi_ref, y_ref, z_ref, carry_ref):
        @pl.when(pl.program_id(1) == 0)
        def _():
            carry_ref[...] = jnp.zeros_like(carry_ref)

        u = u_ref[...]
        ub = u.astype(BF16)
        xr_ref[...] = jnp.dot(ub, bre_ref[0], preferred_element_type=F32)
        xi_ref[...] = jnp.dot(ub, bim_ref[0], preferred_element_type=F32)
        pw_r, pw_i = pr_ref[...], pi_ref[...]

        def step(t, carry):
            cr, ci = carry
            rows = pl.ds(pl.multiple_of(t * SUBLANES, SUBLANES), SUBLANES)
            xr, xi = _scan_tile(xr_ref[rows, :], xi_ref[rows, :], pw_r, pw_i, False)
            xr, xi = xr + pw_r * cr - pw_i * ci, xi + pw_r * ci + pw_i * cr
            xr_ref[rows, :] = xr
            xi_ref[rows, :] = xi
            return (jnp.broadcast_to(xr[SUBLANES - 1:SUBLANES, :], (SUBLANES, st)),
                    jnp.broadcast_to(xi[SUBLANES - 1:SUBLANES, :], (SUBLANES, st)))

        cr, ci = lax.fori_loop(0, n_tiles, step, (carry_ref[0], carry_ref[1]), unroll=2)
        carry_ref[0] = cr
        carry_ref[1] = ci
        y = (lax.dot_general(xr_ref[...].astype(BF16), cre_ref[0], _NT, preferred_element_type=F32)
             + lax.dot_general(xi_ref[...].astype(BF16), cim_ref[0], _NT, preferred_element_type=F32)
             + d_ref[...] * u)
        y_ref[...] = y
        z_ref[...] = _gelu(y).astype(BF16)

    chan = pl.BlockSpec((tt, ch), lambda b, t: (t, b))
    state = pl.BlockSpec((tt, st), lambda b, t: (t, b))
    mat = pl.BlockSpec((1, ch, st), lambda b, t: (b, 0, 0))
    pw = pl.BlockSpec((SUBLANES, st), lambda b, t: (0, b))
    vec = pl.BlockSpec((1, ch), lambda b, t: (0, b))
    x_shape = jax.ShapeDtypeStruct((s, nb * st), F32)
    return pl.pallas_call(
        body, name="ssm_fwd", grid=(nb, nt), in_specs=[chan, mat, mat, mat, mat, pw, pw, vec],
        out_specs=(state, state, chan, chan),
        out_shape=(x_shape, x_shape, jax.ShapeDtypeStruct((s, nb * ch), F32), jax.ShapeDtypeStruct((s, nb * ch), BF16)),
        scratch_shapes=[pltpu.VMEM((2, SUBLANES, st), F32)],
        compiler_params=_params(("parallel", "arbitrary")))(proj, bc_re, bc_im, cc_re, cc_im, pw_re, pw_im, dvec)


def _ssm_bwd(dz, y, proj, x_re, x_im, bc_re, bc_im, cc_re, cc_im, pw_re, pw_im, dvec):
    s = proj.shape[0]
    tt, nt, nb, ch, st = _ssm_blocks(s)
    n_tiles = tt // SUBLANES

    def body(dz_ref, y_ref, u_ref, xr_ref, xi_ref, bre_ref, bim_ref, cre_ref, cim_ref, pr_ref, pi_ref, d_ref,
             du_ref, dbr_ref, dbi_ref, dcr_ref, dci_ref, gar_ref, gai_ref, dd_ref,
             lr_ref, li_ref, carry_ref, acc_ref):
        @pl.when(pl.program_id(1) == 0)
        def _():
            for ref in (carry_ref, acc_ref, dbr_ref, dbi_ref, dcr_ref, dci_ref, dd_ref):
                ref[...] = jnp.zeros_like(ref)

        u = u_ref[...]
        dy = dz_ref[...] * _gelu_grad(y_ref[...])
        ub, dyb = u.astype(BF16), dy.astype(BF16)
        lr_ref[...] = jnp.dot(dyb, cre_ref[0], preferred_element_type=F32)
        li_ref[...] = jnp.dot(dyb, cim_ref[0], preferred_element_type=F32)
        pw_r, pw_i = pr_ref[...], -pi_ref[...]
        pwc_r = jnp.concatenate([pw_r[SUBLANES - 1 - i:SUBLANES - i, :] for i in range(SUBLANES)], axis=0)
        pwc_i = jnp.concatenate([pw_i[SUBLANES - 1 - i:SUBLANES - i, :] for i in range(SUBLANES)], axis=0)
        row = lax.broadcasted_iota(jnp.int32, (SUBLANES, st), 0)

        def step(t, carry):
            cr, ci, acc_r, acc_i = carry
            rows = pl.ds(pl.multiple_of((n_tiles - 1 - t) * SUBLANES, SUBLANES), SUBLANES)
            lr, li = _scan_tile(lr_ref[rows, :], li_ref[rows, :], pw_r, pw_i, True)
            lr, li = lr + pwc_r * cr - pwc_i * ci, li + pwc_r * ci + pwc_i * cr
            lr_ref[rows, :] = lr
            li_ref[rows, :] = li
            nr = jnp.where(row < SUBLANES - 1, pltpu.roll(lr, SUBLANES - 1, 0), cr)
            ni = jnp.where(row < SUBLANES - 1, pltpu.roll(li, SUBLANES - 1, 0), ci)
            xr, xi = xr_ref[rows, :], xi_ref[rows, :]
            return (jnp.broadcast_to(lr[0:1, :], (SUBLANES, st)), jnp.broadcast_to(li[0:1, :], (SUBLANES, st)),
                    acc_r + xr * nr + xi * ni, acc_i + xr * ni - xi * nr)

        cr, ci, acc_r, acc_i = lax.fori_loop(
            0, n_tiles, step, (carry_ref[0], carry_ref[1], acc_ref[0], acc_ref[1]), unroll=2)
        carry_ref[0], carry_ref[1] = cr, ci
        acc_ref[0], acc_ref[1] = acc_r, acc_i
        lrb, lib = lr_ref[...].astype(BF16), li_ref[...].astype(BF16)
        du_ref[...] = (lax.dot_general(lrb, bre_ref[0], _NT, preferred_element_type=F32)
                       + lax.dot_general(lib, bim_ref[0], _NT, preferred_element_type=F32) + dy * d_ref[...])
        dbr_ref[0] += lax.dot_general(ub, lrb, _TN, preferred_element_type=F32)
        dbi_ref[0] += lax.dot_general(ub, lib, _TN, preferred_element_type=F32)
        dcr_ref[0] += lax.dot_general(dyb, xr_ref[...].astype(BF16), _TN, preferred_element_type=F32)
        dci_ref[0] += lax.dot_general(dyb, xi_ref[...].astype(BF16), _TN, preferred_element_type=F32)
        dd_ref[...] += jnp.sum(dy * u, axis=0, keepdims=True)

        @pl.when(pl.program_id(1) == nt - 1)
        def _():
            gar_ref[...] = jnp.sum(acc_r, axis=0, keepdims=True)
            gai_ref[...] = jnp.sum(acc_i, axis=0, keepdims=True)

    chan = pl.BlockSpec((tt, ch), lambda b, t: (nt - 1 - t, b))
    state = pl.BlockSpec((tt, st), lambda b, t: (nt - 1 - t, b))
    mat = pl.BlockSpec((1, ch, st), lambda b, t: (b, 0, 0))
    pw = pl.BlockSpec((SUBLANES, st), lambda b, t: (0, b))
    vec = pl.BlockSpec((1, ch), lambda b, t: (0, b))
    svec = pl.BlockSpec((1, st), lambda b, t: (0, b))
    mat_shape = jax.ShapeDtypeStruct((nb, ch, st), F32)
    return pl.pallas_call(
        body, name="ssm_bwd", grid=(nb, nt),
        in_specs=[chan, chan, chan, state, state, mat, mat, mat, mat, pw, pw, vec],
        out_specs=(chan, mat, mat, mat, mat, svec, svec, vec),
        out_shape=(jax.ShapeDtypeStruct((s, nb * ch), F32), mat_shape, mat_shape, mat_shape, mat_shape,
                   jax.ShapeDtypeStruct((1, nb * st), F32), jax.ShapeDtypeStruct((1, nb * st), F32),
                   jax.ShapeDtypeStruct((1, nb * ch), F32)),
        scratch_shapes=[pltpu.VMEM((tt, st), F32), pltpu.VMEM((tt, st), F32),
                        pltpu.VMEM((2, SUBLANES, st), F32), pltpu.VMEM((2, SUBLANES, st), F32)],
        compiler_params=_params(("parallel", "arbitrary")))(
            dz, y, proj, x_re, x_im, bc_re, bc_im, cc_re, cc_im, pw_re, pw_im, dvec)


def _split_dot(x, tri2):
    hi = x.astype(BF16)
    lo = (x - hi.astype(F32)).astype(BF16)
    return jnp.dot(jnp.concatenate([hi, lo], axis=1), tri2, preferred_element_type=F32)


def _tri_and_ones(tk, rel):
    r_i = lax.broadcasted_iota(jnp.int32, (2 * tk, 2 * tk), 0) % tk
    c_i = lax.broadcasted_iota(jnp.int32, (2 * tk, 2 * tk), 1)
    return jnp.where((c_i >= tk) | rel(r_i, c_i), 1.0, 0.0).astype(BF16)


def _attn_scores(qs, kb, own):
    z = lax.dot_general(qs, kb, _NT, preferred_element_type=F32)
    log_beta = jnp.minimum(z, 0.0) - jnp.log(1.0 + jnp.exp(-jnp.abs(z)))
    log_stay = log_beta - z
    if not own:
        return log_beta, log_stay, None
    mask = lax.broadcasted_iota(jnp.int32, z.shape, 1) < lax.broadcasted_iota(jnp.int32, z.shape, 0)
    return log_beta, jnp.where(mask, log_stay, 0.0), mask


def _attn_blocks(s):
    tq, tk = min(ATT_TQ, s), min(ATT_TK, s)
    assert s % tq == 0 and tq % tk == 0
    return tq, tk, tq // tk


def _first_head(shape, axis):
    return lax.broadcasted_iota(jnp.int32, shape, axis) < HEAD_DIM


def _pair_rms(x, g):
    first = _first_head(x.shape, 1)
    sq = x * x
    ms_a = jnp.sum(jnp.where(first, sq, 0.0), axis=1, keepdims=True) * (1.0 / HEAD_DIM)
    ms_b = jnp.sum(jnp.where(first, 0.0, sq), axis=1, keepdims=True) * (1.0 / HEAD_DIM)
    r = jnp.where(first, lax.rsqrt(ms_a + RMS_EPS), lax.rsqrt(ms_b + RMS_EPS))
    return x * r * g, r


def _pair_rms_bwd(x, g, r, dn):
    first = _first_head(x.shape, 1)
    a = dn * g
    ax = a * x
    dot_a = jnp.sum(jnp.where(first, ax, 0.0), axis=1, keepdims=True) * (1.0 / HEAD_DIM)
    dot_b = jnp.sum(jnp.where(first, 0.0, ax), axis=1, keepdims=True) * (1.0 / HEAD_DIM)
    dx = r * a - x * (r * r * r * jnp.where(first, dot_a, dot_b))
    return dx, jnp.sum(dn * x * r, axis=0, keepdims=True)


_Q_BLOCK0 = 2 * SSM_WIDTH // (2 * HEAD_DIM)
_K_BLOCK0 = _Q_BLOCK0 + ATTN_WIDTH // (2 * HEAD_DIM)
_V_BLOCK0 = _K_BLOCK0 + ATTN_WIDTH // (2 * HEAD_DIM)
HEAD_PAIRS = ATTN_HEADS // 2


def _fill_keys(k_ref, v_ref, gk_ref, kn_ref, vb_ref, rows_per_step):
    def fill(c, _):
        rows = pl.ds(pl.multiple_of(c * rows_per_step, rows_per_step), rows_per_step)
        kn, _ = _pair_rms(k_ref[rows, :], gk_ref[...])
        kn_ref[rows, :] = kn.astype(BF16)
        vb_ref[rows, :] = v_ref[rows, :].astype(BF16)
        return 0

    lax.fori_loop(0, k_ref.shape[0] // rows_per_step, fill, 0)


def _attn_fwd(proj, gq, gk):
    s = proj.shape[0]
    tq, tk, r = _attn_blocks(s)
    pw = 2 * HEAD_DIM

    def body(q_ref, k_ref, v_ref, gq_ref, gk_ref, o_ref, b_ref, kn_ref, vb_ref):
        qi = pl.program_id(1)

        @pl.when(qi == 0)
        def _():
            _fill_keys(k_ref, v_ref, gk_ref, kn_ref, vb_ref, tq)

        qn, _ = _pair_rms(q_ref[...], gq_ref[...])
        qn = qn * (HEAD_DIM ** -0.5)
        first = _first_head((tq, pw), 1)
        tri = _tri_and_ones(tk, lambda row, col: row > col)
        outs = []
        for head in range(2):
            qs = jnp.where(first if head == 0 else ~first, qn, 0.0).astype(BF16)

            def tiles(kq, carry, own, qs=qs):
                o_acc, acc = carry
                ks = pl.ds(pl.multiple_of(kq * tq, tq), tq)
                log_beta, log_stay, mask = _attn_scores(qs, kn_ref[ks, :], own)
                sums = [_split_dot(log_stay[:, u * tk:(u + 1) * tk], tri) for u in range(r)]
                later = [None] * r
                for u in reversed(range(r)):
                    later[u] = acc + sums[u][:, :tk]
                    acc = acc + sums[u][:, tk:]
                w = jnp.exp(log_beta + jnp.concatenate(later, axis=1))
                if own:
                    w = jnp.where(mask, w, 0.0)
                o_acc = o_acc + jnp.dot(w.astype(BF16), vb_ref[ks, :], preferred_element_type=F32)
                return o_acc, acc

            carry = tiles(qi, (jnp.zeros((tq, pw), F32), jnp.zeros((tq, tk), F32)), True)
            o_acc, acc = lax.fori_loop(0, qi, lambda it, c, tiles=tiles: tiles(qi - 1 - it, c, False), carry)
            outs.append(o_acc)
            b_ref[head] = acc[:, 0:1]
        o_ref[...] = jnp.where(first, outs[0], outs[1])

    def cols(block0):
        return pl.BlockSpec((s, pw), lambda hp, i: (0, block0 + hp))

    gain = pl.BlockSpec((1, pw), lambda hp, i: (0, 0))
    return pl.pallas_call(
        body, name="attn_fwd", grid=(HEAD_PAIRS, s // tq),
        in_specs=[pl.BlockSpec((tq, pw), lambda hp, i: (i, _Q_BLOCK0 + hp)), cols(_K_BLOCK0), cols(_V_BLOCK0),
                  gain, gain],
        out_specs=(pl.BlockSpec((tq, pw), lambda hp, i: (i, hp)), pl.BlockSpec((2, tq, 1), lambda hp, i: (hp, i, 0))),
        out_shape=(jax.ShapeDtypeStruct((s, ATTN_WIDTH), F32), jax.ShapeDtypeStruct((ATTN_HEADS, s, 1), F32)),
        scratch_shapes=[pltpu.VMEM((s, pw), BF16), pltpu.VMEM((s, pw), BF16)],
        compiler_params=_params(("parallel", "arbitrary")))(proj, proj, proj, gq, gk)


def _attn_bwd(proj, gq, gk, bsum, do):
    s = proj.shape[0]
    tq, tk, r = _attn_blocks(s)
    nq = s // tq
    pw = 2 * HEAD_DIM

    def body(q_ref, k_ref, v_ref, gq_ref, gk_ref, b_ref, do_ref,
             dq_ref, dk_ref, dv_ref, dgq_ref, dgk_ref, kn_ref, vb_ref, dkt_ref, dvt_ref):
        qi = pl.program_id(1)

        @pl.when(qi == 0)
        def _():
            _fill_keys(k_ref, v_ref, gk_ref, kn_ref, vb_ref, tq)
            for ref in (dkt_ref, dvt_ref, dgq_ref):
                ref[...] = jnp.zeros_like(ref)

        scale = HEAD_DIM ** -0.5
        q = q_ref[...]
        qn, rq = _pair_rms(q, gq_ref[...])
        qf = qn * scale
        qft = qf.T
        dof = do_ref[...]
        doft = dof.T
        first = _first_head((tq, pw), 1)
        first_t = _first_head((pw, tq), 0)
        tri_upto = _tri_and_ones(tk, lambda row, col: row <= col)
        tri_before = _tri_and_ones(tk, lambda row, col: row < col)
        dqs = []
        for head in range(2):
            sel, sel_t = (first, first_t) if head == 0 else (~first, ~first_t)
            qs, qst = jnp.where(sel, qf, 0.0).astype(BF16), jnp.where(sel_t, qft, 0.0).astype(BF16)
            dob, dobt = jnp.where(sel, dof, 0.0).astype(BF16), jnp.where(sel_t, doft, 0.0).astype(BF16)

            def tiles(kq, carry, own, qs=qs, qst=qst, dob=dob, dobt=dobt):
                dq_acc, rest, g_prefix = carry
                ks = pl.ds(pl.multiple_of(kq * tq, tq), tq)
                kb = kn_ref[ks, :]
                log_beta, log_stay, mask = _attn_scores(qs, kb, own)
                dw = lax.dot_general(dob, vb_ref[ks, :], _NT, preferred_element_type=F32)
                sums = [_split_dot(log_stay[:, u * tk:(u + 1) * tk], tri_upto) for u in range(r)]
                later = [None] * r
                for u in range(r):
                    later[u] = rest - sums[u][:, :tk]
                    rest = rest - sums[u][:, tk:]
                w = jnp.exp(log_beta + jnp.concatenate(later, axis=1))
                if own:
                    w = jnp.where(mask, w, 0.0)
                g = w * dw
                gsums = [_split_dot(g[:, u * tk:(u + 1) * tk], tri_before) for u in range(r)]
                before = [None] * r
                for u in range(r):
                    before[u] = g_prefix + gsums[u][:, :tk]
                    g_prefix = g_prefix + gsums[u][:, tk:]
                dz = g - jnp.exp(log_beta) * (g + jnp.concatenate(before, axis=1))
                if own:
                    dz = jnp.where(mask, dz, 0.0)
                dzb = dz.astype(BF16)
                dq_acc = dq_acc + jnp.dot(dzb, kb, preferred_element_type=F32)
                dkt_ref[:, ks] += jnp.dot(qst, dzb, preferred_element_type=F32)
                dvt_ref[:, ks] += jnp.dot(dobt, w.astype(BF16), preferred_element_type=F32)
                return dq_acc, rest, g_prefix

            carry = (jnp.zeros((tq, pw), F32), jnp.broadcast_to(b_ref[head], (tq, tk)), jnp.zeros((tq, tk), F32))
            carry = lax.fori_loop(0, qi, lambda kq, c, tiles=tiles: tiles(kq, c, False), carry)
            dqs.append(tiles(qi, carry, True)[0])
        dq, dgq = _pair_rms_bwd(q, gq_ref[...], rq, jnp.where(first, dqs[0], dqs[1]) * scale)
        dq_ref[...] = dq
        dgq_ref[0] += dgq

        @pl.when(qi == nq - 1)
        def _():
            def finish(c, dgk):
                rows = pl.ds(pl.multiple_of(c * tq, tq), tq)
                k = k_ref[rows, :]
                _, rk = _pair_rms(k, gk_ref[...])
                dk, dgk_c = _pair_rms_bwd(k, gk_ref[...], rk, dkt_ref[:, rows].T)
                dk_ref[rows, :] = dk
                dv_ref[rows, :] = dvt_ref[:, rows].T
                return dgk + dgk_c

            dgk_ref[0] = lax.fori_loop(0, nq, finish, jnp.zeros((1, pw), F32))

    def cols(block0):
        return pl.BlockSpec((s, pw), lambda hp, i: (0, block0 + hp))

    gain = pl.BlockSpec((1, pw), lambda hp, i: (0, 0))
    q_rows = pl.BlockSpec((tq, pw), lambda hp, i: (i, hp))
    all_rows = pl.BlockSpec((s, pw), lambda hp, i: (0, hp))
    pair_gain = pl.BlockSpec((1, 1, pw), lambda hp, i: (hp, 0, 0))
    wide = jax.ShapeDtypeStruct((s, ATTN_WIDTH), F32)
    gains = jax.ShapeDtypeStruct((HEAD_PAIRS, 1, pw), F32)
    return pl.pallas_call(
        body, name="attn_bwd", grid=(HEAD_PAIRS, nq),
        in_specs=[pl.BlockSpec((tq, pw), lambda hp, i: (i, _Q_BLOCK0 + hp)), cols(_K_BLOCK0), cols(_V_BLOCK0),
                  gain, gain, pl.BlockSpec((2, tq, 1), lambda hp, i: (hp, i, 0)), q_rows],
        out_specs=(q_rows, all_rows, all_rows, pair_gain, pair_gain),
        out_shape=(wide, wide, wide, gains, gains),
        scratch_shapes=[pltpu.VMEM((s, pw), BF16), pltpu.VMEM((s, pw), BF16),
                        pltpu.VMEM((pw, s), F32), pltpu.VMEM((pw, s), F32)],
        compiler_params=_params(("parallel", "arbitrary")))(proj, proj, proj, gq, gk, bsum, do)


def _reduce_adamw(parts, w, m, v, *, name, row0=0):
    n, _, cols = parts.shape
    rows = w.shape[0]
    tile_rows = 2 * SUBLANES
    tr = max(t for t in range(tile_rows, min(rows, 512) + 1, tile_rows) if rows % t == 0 and row0 % t == 0)
    c1 = 1.0 - ADAM_B1 ** ADAM_STEP
    c2 = 1.0 - ADAM_B2 ** ADAM_STEP

    def body(p_ref, w_ref, m_ref, v_ref, g_ref, d_ref, nm_ref, nv_ref):
        g = p_ref[0].astype(F32)
        for i in range(1, n):
            g = g + p_ref[i].astype(F32)
        nm = ADAM_B1 * m_ref[...] + (1.0 - ADAM_B1) * g
        nv = ADAM_B2 * v_ref[...] + (1.0 - ADAM_B2) * (g * g)
        g_ref[...] = g
        nm_ref[...] = nm
        nv_ref[...] = nv
        d_ref[...] = -ADAM_LR * ((nm / c1) / (jnp.sqrt(nv / c2) + ADAM_EPS) + ADAM_WD * w_ref[...])

    row = pl.BlockSpec((tr, cols), lambda i: (i, 0))
    out = jax.ShapeDtypeStruct((rows, cols), F32)
    return pl.pallas_call(
        body, name=name, grid=(rows // tr,),
        in_specs=[pl.BlockSpec((n, tr, cols), lambda i: (0, row0 // tr + i, 0)), row, row, row],
        out_specs=(row,) * 4, out_shape=(out,) * 4, compiler_params=_params(("parallel",)))(parts, w, m, v)


def _all_gather(shards, *, name):
    n_arr = len(shards)

    def body(*refs):
        x_refs, out_refs = refs[:n_arr], refs[n_arr:2 * n_arr]
        send_sems, recv_sems, local_sems = refs[2 * n_arr:]
        x, y, c = lax.axis_index("x"), lax.axis_index("y"), lax.axis_index("c")
        me, sibling = (x, y, c), (x, y, 1 - c)
        chips = [(1 - x, y), (x, 1 - y), (1 - x, 1 - y)]

        def slot(a, px, py, pc):
            return out_refs[a].at[4 * px + 2 * py + pc]

        def copy(a, k, block, to, src=None):
            return pltpu.make_async_remote_copy(
                src_ref=slot(a, *block) if src is None else src, dst_ref=slot(a, *block),
                send_sem=send_sems.at[a, k], recv_sem=recv_sems.at[a, k], device_id=to, device_id_type=MESH_IDS)

        arrays = range(n_arr)
        mine = [pltpu.make_async_copy(x_refs[a], slot(a, *me), local_sems.at[a]) for a in arrays]
        for cp in mine:
            cp.start()
        first = [copy(a, 0, me, sibling, src=x_refs[a]) for a in arrays]
        first += [copy(a, 1 + j, me, (*chip, c), src=x_refs[a]) for j, chip in enumerate(chips) for a in arrays]
        for cp in first:
            cp.start()
        passed = []
        for j, chip in enumerate(chips):
            for a in arrays:
                copy(a, 1 + j, (*chip, c), me).wait_recv()
                passed.append(copy(a, 4 + j, (*chip, c), sibling))
                passed[-1].start()
        for a in arrays:
            copy(a, 0, sibling, me).wait_recv()
        for j, chip in enumerate(chips):
            for a in arrays:
                copy(a, 4 + j, (*chip, 1 - c), me).wait_recv()
        for cp in first + passed:
            cp.wait_send()
        for cp in mine:
            cp.wait()

    any_spec = pl.BlockSpec(memory_space=pl.ANY)
    return pl.pallas_call(
        body, name=name, out_shape=tuple(jax.ShapeDtypeStruct((N_DEV,) + s.shape, s.dtype) for s in shards),
        in_specs=[any_spec] * n_arr, out_specs=(any_spec,) * n_arr,
        scratch_shapes=[pltpu.SemaphoreType.DMA((n_arr, 7)), pltpu.SemaphoreType.DMA((n_arr, 7)),
                        pltpu.SemaphoreType.DMA((n_arr,))],
    )(*shards)


def _all_to_all(bufs, *, name):
    n_arr = len(bufs)

    def body(*refs):
        x_refs, out_refs = refs[:n_arr], refs[n_arr:2 * n_arr]
        send_sems, recv_sems, local_sems = refs[2 * n_arr:]
        x, y, c = lax.axis_index("x"), lax.axis_index("y"), lax.axis_index("c")
        me = 4 * x + 2 * y + c
        mine = [pltpu.make_async_copy(x_refs[a].at[me], out_refs[a].at[me], local_sems.at[a]) for a in range(n_arr)]
        for cp in mine:
            cp.start()

        def copy(a, k, landing):
            px, py, pc = x ^ (k >> 2), y ^ ((k >> 1) & 1), c ^ (k & 1)
            peer = 4 * px + 2 * py + pc
            return pltpu.make_async_remote_copy(
                src_ref=x_refs[a].at[peer], dst_ref=out_refs[a].at[peer if landing else me],
                send_sem=send_sems.at[a, k - 1], recv_sem=recv_sems.at[a, k - 1],
                device_id=(px, py, pc), device_id_type=MESH_IDS)

        copies = [copy(a, k, False) for k in range(1, N_DEV) for a in range(n_arr)]
        for cp in copies:
            cp.start()
        for k in range(1, N_DEV):
            for a in range(n_arr):
                copy(a, k, True).wait_recv()
        for cp in copies:
            cp.wait_send()
        for cp in mine:
            cp.wait()

    any_spec = pl.BlockSpec(memory_space=pl.ANY)
    return pl.pallas_call(
        body, name=name, out_shape=tuple(jax.ShapeDtypeStruct(b.shape, b.dtype) for b in bufs),
        in_specs=[any_spec] * n_arr, out_specs=(any_spec,) * n_arr,
        scratch_shapes=[pltpu.SemaphoreType.DMA((n_arr, 7)), pltpu.SemaphoreType.DMA((n_arr, 7)),
                        pltpu.SemaphoreType.DMA((n_arr,))],
    )(*bufs)


def _pack_small(vals):
    flat = jnp.concatenate([vals[n].reshape(-1) for n in SMALL])
    rows = -(-flat.shape[0] // (SUBLANES * PACK_COLS)) * SUBLANES
    return jnp.pad(flat, (0, rows * PACK_COLS - flat.shape[0])).reshape(rows, PACK_COLS)


def _unpack_small(flat, shapes):
    flat = flat.reshape(-1)
    out, off = {}, 0
    for n in SMALL:
        size = math.prod(shapes[n])
        out[n] = flat[off:off + size].reshape(shapes[n])
        off += size
    return out


def _pair_gain(g):
    return jnp.tile(g, 2)[None]


def _ssm_setup(sp):
    g, p = SSM_GROUPS, SSM_STATE
    a_re = sp["ssm_a_re"][:, None, :]
    a_im = sp["ssm_a_im"][:, None, :]
    log_dt = jnp.broadcast_to(sp["ssm_log_dt"][:, None, None], (g, 1, p))
    bt_re = sp["ssm_b_re"].transpose(0, 2, 1)
    bt_im = sp["ssm_b_im"].transpose(0, 2, 1)
    return a_re, a_im, log_dt, bt_re, bt_im


def _layer_fwd(h, p_l, w, sp):
    s = h.shape[0]
    hn = _rms_fwd(h, sp["mix_norm_g"][None], name="rms_mix")
    proj = _mm(hn, w["w_in"], name="mm_proj")

    pw_re, pw_im, bbt_re, bbt_im = _ssm_param_fwd(*_ssm_setup(sp))
    pw_re = pw_re.transpose(1, 0, 2).reshape(SUBLANES, SSM_LANES)
    pw_im = pw_im.transpose(1, 0, 2).reshape(SUBLANES, SSM_LANES)
    ssm_mats = (_block_diag(bbt_re).astype(BF16), _block_diag(bbt_im).astype(BF16),
                _block_diag(sp["ssm_c_re"]).astype(BF16), _block_diag(-sp["ssm_c_im"]).astype(BF16),
                pw_re, pw_im, sp["ssm_d"].reshape(1, SSM_WIDTH))
    x_re, x_im, y, z = _ssm_fwd(proj, *ssm_mats)
    zz = _mm(z, w["ssm_w_glu"], name="mm_glu")

    o, bsum = _attn_fwd(proj, _pair_gain(sp["q_norm_g"]), _pair_gain(sp["k_norm_g"]))

    bglu = sp["ssm_b_glu"][None]
    ycat = _gate_fwd(zz, bglu, proj, o)
    h2 = _mm(ycat, w["w_out"], add=h, name="mm_out")
    hn2 = _rms_fwd(h2, sp["ple_norm_g"][None], name="rms_ple")
    pgl = _mm(hn2, w["w_ple_gate"], name="mm_ple_gate")
    pp = _mm(p_l, w["w_ple_proj"], name="mm_ple_proj")
    h3 = _ple_out(h2, pgl, pp)
    saved = dict(h=h, hn=hn, proj=proj, ssm_mats=ssm_mats, x_re=x_re, x_im=x_im, y=y, z=z,
                 zz=zz, bsum=bsum, o=o, ycat=ycat, h2=h2, hn2=hn2,
                 pgl=pgl, pp=pp, p_l=p_l, bglu=bglu)
    return h3, saved


def _layer_bwd(dh3, sv, w, sp):
    s = dh3.shape[0]
    g, hh, p = SSM_GROUPS, SSM_GROUP, SSM_STATE
    grads = {}
    dgp, dpp = _ple_bwd(dh3, sv["pgl"], sv["pp"])
    grads["w_ple_gate"] = _mm(sv["hn2"], dgp, ta=True, out_dtype=BF16, name="mm_d_ple_gate")
    grads["w_ple_proj"] = _mm(sv["p_l"], dpp, ta=True, out_dtype=BF16, name="mm_d_ple_proj")
    dhn2 = _mm(dgp, w["w_ple_gate"], tb=True, name="mm_dhn2")
    dh2, dg2 = _rms_bwd(sv["h2"], sp["ple_norm_g"][None], dhn2, dres=dh3, name="rms_ple_bwd")
    grads["ple_norm_g"] = dg2[0]
    grads["w_out"] = _mm(sv["ycat"], dh2, ta=True, out_dtype=BF16, name="mm_d_out")
    dyc = _mm(dh2, w["w_out"], tb=True, name="mm_dycat")
    dzz, dgs, do, dga, dbglu = _gate_bwd(dyc, sv["zz"], sv["bglu"], sv["proj"], sv["o"])
    grads["ssm_b_glu"] = dbglu[0]

    dq, dk, dv, dgq, dgk = _attn_bwd(sv["proj"], _pair_gain(sp["q_norm_g"]), _pair_gain(sp["k_norm_g"]),
                                     sv["bsum"], do)
    grads["q_norm_g"] = jnp.sum(dgq.reshape(ATTN_HEADS, HEAD_DIM), axis=0)
    grads["k_norm_g"] = jnp.sum(dgk.reshape(ATTN_HEADS, HEAD_DIM), axis=0)

    grads["ssm_w_glu"] = _mm(sv["z"], dzz, ta=True, out_dtype=BF16, name="mm_d_glu")
    dz = _mm(dzz, w["ssm_w_glu"], tb=True, name="mm_dz")
    du, dbc_re, dbc_im, dcc_re, dcc_im, gab_re, gab_im, dd = _ssm_bwd(
        dz, sv["y"], sv["proj"], sv["x_re"], sv["x_im"], *sv["ssm_mats"])
    grads["ssm_d"] = dd.reshape(g, hh)
    grads["ssm_c_re"] = _block_diag_take(dcc_re, hh, p)
    grads["ssm_c_im"] = -_block_diag_take(dcc_im, hh, p)
    da_re, da_im, dldt, dbt_re, dbt_im = _ssm_param_bwd(
        *_ssm_setup(sp), gab_re.reshape(g, 1, p), gab_im.reshape(g, 1, p),
        _block_diag_take(dbc_re, hh, p), _block_diag_take(dbc_im, hh, p))
    grads["ssm_a_re"], grads["ssm_a_im"] = da_re[:, 0, :], da_im[:, 0, :]
    grads["ssm_log_dt"] = dldt[:, 0, 0]
    grads["ssm_b_re"], grads["ssm_b_im"] = dbt_re.transpose(0, 2, 1), dbt_im.transpose(0, 2, 1)

    dproj = jnp.concatenate([du, dgs, dq, dk, dv, dga], axis=1)
    grads["w_in"] = _mm(sv["hn"], dproj, ta=True, out_dtype=BF16, name="mm_d_in")
    dhn = _mm(dproj, w["w_in"], tb=True, name="mm_dhn")
    dh, dg1 = _rms_bwd(sv["h"], sp["mix_norm_g"][None], dhn, dres=dh2, name="rms_mix_bwd")
    grads["mix_norm_g"] = dg1[0]
    return dh, grads


def _local_step(x, p, target, big, small):
    h = x
    saved = []
    for l in range(DEPTH):
        sp = {n: small[n][l] for n in SMALL}
        h, sv = _layer_fwd(h, p[l], big[l], sp)
        saved.append(sv)
    dh, loss_parts = _loss_head(h, target)
    grads = [None] * DEPTH
    for l in reversed(range(DEPTH)):
        sp = {n: small[n][l] for n in SMALL}
        dh, grads[l] = _layer_bwd(dh, saved[l], big[l], sp)
    return jnp.sum(loss_parts), dh, grads


def kernel(x, p, mix_norm_g, w_in, ssm_a_re, ssm_a_im, ssm_log_dt, ssm_b_re, ssm_b_im, ssm_c_re, ssm_c_im, ssm_d, ssm_w_glu, ssm_b_glu, q_norm_g, k_norm_g, w_out, ple_norm_g, w_ple_gate, w_ple_proj, loss_target, m_mix_norm_g, m_w_in, m_ssm_a_re, m_ssm_a_im, m_ssm_log_dt, m_ssm_b_re, m_ssm_b_im, m_ssm_c_re, m_ssm_c_im, m_ssm_d, m_ssm_w_glu, m_ssm_b_glu, m_q_norm_g, m_k_norm_g, m_w_out, m_ple_norm_g, m_w_ple_gate, m_w_ple_proj, v_mix_norm_g, v_w_in, v_ssm_a_re, v_ssm_a_im, v_ssm_log_dt, v_ssm_b_re, v_ssm_b_im, v_ssm_c_re, v_ssm_c_im, v_ssm_d, v_ssm_w_glu, v_ssm_b_glu, v_q_norm_g, v_k_norm_g, v_w_out, v_ple_norm_g, v_w_ple_gate, v_w_ple_proj):
    given = dict(locals())
    wts = {n: given[n] for n in WEIGHTS}
    mom = {n: given["m_" + n] for n in WEIGHTS}
    var = {n: given["v_" + n] for n in WEIGHTS}

    def rows2d(a):
        return a.reshape(-1, a.shape[-1])

    groups = (("w_in",), ("ssm_w_glu", "w_ple_proj"), ("w_out", "w_ple_gate"))
    shards = [jnp.concatenate([rows2d(wts[n]).astype(BF16) for n in names], axis=0) for names in groups]
    gathered = _all_gather(shards, name="gather_weights")
    big = [{} for _ in range(DEPTH)]
    for names, got in zip(groups, gathered):
        row0 = 0
        for n in names:
            r = wts[n].shape[1]
            for l in range(DEPTH):
                blocks = got[:, row0:row0 + r, :]
                if n in ROW_SHARDED:
                    big[l][n] = blocks.reshape(N_DEV * r, -1)
                else:
                    big[l][n] = blocks.transpose(1, 0, 2).reshape(r, -1)
                row0 += r

    small = {n: wts[n] for n in SMALL}
    loss, dx, grads = _local_step(x[0], p[:, 0], loss_target[0], big, small)
    loss = lax.psum(loss, ("x", "y", "c"))

    def blocks_of(n, l):
        g = grads[l][n]
        if n in ROW_SHARDED:
            return g.reshape(N_DEV, -1, g.shape[1])
        return g.reshape(g.shape[0], N_DEV, -1).transpose(1, 0, 2)

    send = [jnp.concatenate([blocks_of(n, l) for n in names for l in range(DEPTH)], axis=1) for names in groups]
    parts = _all_to_all(send, name="exchange_weight_grads")
    big_out = {}
    for names, part in zip(groups, parts):
        row0 = 0
        for n in names:
            res = _reduce_adamw(part, rows2d(wts[n]), rows2d(mom[n]), rows2d(var[n]), row0=row0, name="adamw_" + n)
            big_out[n] = [a.reshape(wts[n].shape) for a in res]
            row0 += rows2d(wts[n]).shape[0]

    small_shapes = {n: wts[n].shape for n in SMALL}
    local_small = _pack_small({n: jnp.stack([grads[l][n] for l in range(DEPTH)]) for n in SMALL})
    (all_small,) = _all_gather([local_small], name="gather_small_grads")
    small_out = _reduce_adamw(all_small, _pack_small(small), _pack_small({n: mom[n] for n in SMALL}),
                              _pack_small({n: var[n] for n in SMALL}), name="adamw_replicated")
    small_out = [_unpack_small(a, small_shapes) for a in small_out]

    outs = [loss, dx[None]]
    for k in range(4):
        outs += [big_out[n][k] if n in BIG else small_out[k][n] for n in WEIGHTS]
    return tuple(outs)
```

```python
import math

import jax
import jax.numpy as jnp
from jax import lax
from jax.experimental import pallas as pl
from jax.experimental.pallas import tpu as pltpu

F32 = jnp.float32
BF16 = jnp.bfloat16
MESH_IDS = pl.DeviceIdType.MESH

N_DEV = 8
D_MODEL = 1024
DEPTH = 2
PLE_DIM = 256
SSM_WIDTH = 512
SSM_GROUP = 16
SSM_GROUPS = 32
SSM_STATE = 64
SSM_LANES = SSM_GROUPS * SSM_STATE
ATTN_WIDTH = 512
ATTN_HEADS = 8
HEAD_DIM = 64
RMS_EPS = 1e-6
ADAM_LR = 0.001
ADAM_B1 = 0.9
ADAM_B2 = 0.999
ADAM_EPS = 1e-08
ADAM_WD = 0.01
ADAM_STEP = 10

VMEM_LIMIT = 56 * 1024 * 1024
ATT_TQ = 512
ATT_TK = 128
SSM_BLOCK_GROUPS = 8
SSM_TIME_BLOCK = 512
SUBLANES = 8
PACK_COLS = 1024

BIG = ("w_in", "ssm_w_glu", "w_out", "w_ple_gate", "w_ple_proj")
ROW_SHARDED = ("w_out", "w_ple_gate")
SMALL = ("mix_norm_g", "ssm_a_re", "ssm_a_im", "ssm_log_dt", "ssm_b_re", "ssm_b_im", "ssm_c_re",
         "ssm_c_im", "ssm_d", "ssm_b_glu", "q_norm_g", "k_norm_g", "ple_norm_g")
WEIGHTS = ("mix_norm_g", "w_in", "ssm_a_re", "ssm_a_im", "ssm_log_dt", "ssm_b_re", "ssm_b_im",
           "ssm_c_re", "ssm_c_im", "ssm_d", "ssm_w_glu", "ssm_b_glu", "q_norm_g", "k_norm_g",
           "w_out", "ple_norm_g", "w_ple_gate", "w_ple_proj")
_NT = (((1,), (1,)), ((), ()))
_TN = (((0,), (0,)), ((), ()))


def _params(sem=None):
    return pltpu.CompilerParams(dimension_semantics=sem, vmem_limit_bytes=VMEM_LIMIT)


def _sigmoid(x):
    return 1.0 / (1.0 + jnp.exp(-x))


_GELU_K = math.sqrt(2.0 / math.pi)
_GELU_C = 0.044715


def _gelu(x):
    return 0.5 * x * (1.0 + jnp.tanh(_GELU_K * (x + _GELU_C * x * x * x)))


def _gelu_grad(x):
    th = jnp.tanh(_GELU_K * (x + _GELU_C * x * x * x))
    return 0.5 * (1.0 + th) + 0.5 * x * (1.0 - th * th) * _GELU_K * (1.0 + 3.0 * _GELU_C * x * x)


def _mm_call(a, b, add, a_spec, b_spec, o_spec, grid, out_shape, acc_shape, dims, name):
    n_steps = grid[2]
    out_dtype = out_shape.dtype

    def body(*refs):
        if add is None:
            a_ref, b_ref, o_ref, acc_ref = refs
        else:
            a_ref, b_ref, r_ref, o_ref, acc_ref = refs
        kk = pl.program_id(2)

        @pl.when(kk == 0)
        def _():
            acc_ref[...] = jnp.zeros_like(acc_ref)

        acc_ref[...] += lax.dot_general(a_ref[...].astype(BF16), b_ref[...].astype(BF16), dims,
                                        preferred_element_type=F32)

        @pl.when(kk == n_steps - 1)
        def _():
            res = acc_ref[...]
            if add is not None:
                res = res + r_ref[...].astype(F32)
            o_ref[...] = res.astype(out_dtype)

    ins, specs = [a, b], [a_spec, b_spec]
    if add is not None:
        ins.append(add)
        specs.append(o_spec)
    return pl.pallas_call(
        body, name=name, grid=grid, in_specs=specs, out_specs=o_spec, out_shape=out_shape,
        scratch_shapes=[pltpu.VMEM(acc_shape, F32)],
        compiler_params=_params(("parallel", "parallel", "arbitrary")),
    )(*ins)


def _mm(a, b, *, name, ta=False, tb=False, add=None, out_dtype=F32, tm=1024, tn=1024, tk=512):
    m, k = (a.shape[1], a.shape[0]) if ta else a.shape
    n = b.shape[0] if tb else b.shape[1]
    tm, tn, tk = min(tm, m), min(tn, n), min(tk, k)
    assert m % tm == 0 and n % tn == 0 and k % tk == 0, (name, a.shape, b.shape)
    dims = (((0 if ta else 1,), (1 if tb else 0,)), ((), ()))
    a_spec = (pl.BlockSpec((tk, tm), lambda i, j, kk: (kk, i)) if ta
              else pl.BlockSpec((tm, tk), lambda i, j, kk: (i, kk)))
    b_spec = (pl.BlockSpec((tn, tk), lambda i, j, kk: (j, kk)) if tb
              else pl.BlockSpec((tk, tn), lambda i, j, kk: (kk, j)))
    o_spec = pl.BlockSpec((tm, tn), lambda i, j, kk: (i, j))
    return _mm_call(a, b, add, a_spec, b_spec, o_spec, (m // tm, n // tn, k // tk),
                    jax.ShapeDtypeStruct((m, n), out_dtype), (tm, tn), dims, name)


def _rows_tile(rows, want=512):
    t = min(rows, want)
    assert rows % t == 0
    return t


def _rowwise(body, name, rows, ins, outs):
    widest = max(cols for _, cols, _ in ins)
    tm = _rows_tile(rows, 512 if widest > 128 else 4096)
    in_specs = []
    for _, cols, cb in ins:
        if cb is None:
            in_specs.append(pl.BlockSpec((1, cols), lambda i: (0, 0)))
        else:
            in_specs.append(pl.BlockSpec((tm, cols), lambda i, cb=cb: (i, cb)))
    out_specs, out_shapes = [], []
    for cols, dtype, is_acc in outs:
        if is_acc:
            out_specs.append(pl.BlockSpec((1, cols), lambda i: (0, 0)))
            out_shapes.append(jax.ShapeDtypeStruct((1, cols), dtype))
        else:
            out_specs.append(pl.BlockSpec((tm, cols), lambda i: (i, 0)))
            out_shapes.append(jax.ShapeDtypeStruct((rows, cols), dtype))
    any_acc = any(o[2] for o in outs)
    return pl.pallas_call(
        body, name=name, grid=(rows // tm,), in_specs=in_specs, out_specs=tuple(out_specs),
        out_shape=tuple(out_shapes),
        compiler_params=_params(("arbitrary",) if any_acc else ("parallel",)))(*[a for a, _, _ in ins])


def _accumulate(ref, value):
    @pl.when(pl.program_id(0) == 0)
    def _():
        ref[...] = jnp.zeros_like(ref)

    ref[...] += value


def _rms_fwd(h, g, *, name):
    rows, d = h.shape

    def body(h_ref, g_ref, o_ref):
        hv = h_ref[...]
        r = lax.rsqrt(jnp.mean(hv * hv, axis=-1, keepdims=True) + RMS_EPS)
        o_ref[...] = (hv * r * g_ref[...]).astype(BF16)

    return _rowwise(body, name, rows, [(h, d, 0), (g, d, None)], [(d, BF16, False)])[0]


def _rms_bwd(h, g, dhn, *, name, dres=None):
    rows, d = h.shape

    def body(*refs):
        if dres is None:
            h_ref, g_ref, dn_ref, dh_ref, dg_ref = refs
        else:
            h_ref, g_ref, dn_ref, dr_ref, dh_ref, dg_ref = refs
        hv = h_ref[...]
        dn = dn_ref[...].astype(F32)
        r = lax.rsqrt(jnp.mean(hv * hv, axis=-1, keepdims=True) + RMS_EPS)
        a = dn * g_ref[...]
        dot = jnp.mean(a * hv, axis=-1, keepdims=True)
        dh = r * a - hv * (r * r * r * dot)
        if dres is not None:
            dh = dh + dr_ref[...]
        dh_ref[...] = dh
        _accumulate(dg_ref, jnp.sum(dn * hv * r, axis=0, keepdims=True))

    ins = [(h, d, 0), (g, d, None), (dhn, d, 0)]
    if dres is not None:
        ins.append((dres, d, 0))
    return _rowwise(body, name, rows, ins, [(d, F32, False), (d, F32, True)])


def _gate_fwd(zz, bglu, proj, o):
    rows = zz.shape[0]
    w = SSM_WIDTH

    def body(zz_ref, b_ref, gs_ref, o_ref, ga_ref, y_ref):
        zz_v = zz_ref[...] + b_ref[...]
        val, gate = zz_v[:, :w], zz_v[:, w:]
        gs, ga = gs_ref[...], ga_ref[...]
        y_ref[:, :w] = (val * _sigmoid(gate) * (gs * _sigmoid(gs))).astype(BF16)
        y_ref[:, w:] = (o_ref[...] * (ga * _sigmoid(ga))).astype(BF16)

    return _rowwise(body, "gate_fwd", rows,
                    [(zz, 2 * w, 0), (bglu, 2 * w, None), (proj, w, 1), (o, w, 0), (proj, w, 5)],
                    [(2 * w, BF16, False)])[0]


def _gate_bwd(dyc, zz, bglu, proj, o):
    rows = zz.shape[0]
    w = SSM_WIDTH

    def body(dy_ref, zz_ref, b_ref, gs_ref, o_ref, ga_ref, dzz_ref, dgs_ref, do_ref, dga_ref, db_ref):
        zz_v = zz_ref[...] + b_ref[...]
        val, gate = zz_v[:, :w], zz_v[:, w:]
        gs, ga = gs_ref[...], ga_ref[...]
        dys, dya = dy_ref[:, :w], dy_ref[:, w:]
        sg, ss, sa = _sigmoid(gate), _sigmoid(gs), _sigmoid(ga)
        glu = val * sg
        dglu = dys * (gs * ss)
        dgs_ref[...] = dys * glu * (ss * (1.0 + gs * (1.0 - ss)))
        dval = dglu * sg
        dgate = dglu * val * sg * (1.0 - sg)
        dzz_ref[:, :w] = dval.astype(BF16)
        dzz_ref[:, w:] = dgate.astype(BF16)
        do_ref[...] = dya * (ga * sa)
        dga_ref[...] = dya * o_ref[...] * (sa * (1.0 + ga * (1.0 - sa)))
        _accumulate(db_ref, jnp.concatenate([jnp.sum(dval, axis=0, keepdims=True),
                                             jnp.sum(dgate, axis=0, keepdims=True)], axis=1))

    return _rowwise(body, "gate_bwd", rows,
                    [(dyc, 2 * w, 0), (zz, 2 * w, 0), (bglu, 2 * w, None), (proj, w, 1), (o, w, 0), (proj, w, 5)],
                    [(2 * w, BF16, False), (w, F32, False), (w, F32, False), (w, F32, False), (2 * w, F32, True)])


def _ple_out(h2, pgl, pp):
    rows, d = h2.shape

    def body(h_ref, g_ref, p_ref, o_ref):
        o_ref[...] = h_ref[...] + _sigmoid(g_ref[...]) * p_ref[...]

    return _rowwise(body, "ple_out", rows, [(h2, d, 0), (pgl, d, 0), (pp, d, 0)], [(d, F32, False)])[0]


def _ple_bwd(dh3, pgl, pp):
    rows, d = dh3.shape

    def body(dh_ref, g_ref, p_ref, dg_ref, dp_ref):
        sg = _sigmoid(g_ref[...])
        dh = dh_ref[...]
        dg_ref[...] = (dh * p_ref[...] * sg * (1.0 - sg)).astype(BF16)
        dp_ref[...] = (dh * sg).astype(BF16)

    return _rowwise(body, "ple_bwd", rows, [(dh3, d, 0), (pgl, d, 0), (pp, d, 0)],
                    [(d, BF16, False), (d, BF16, False)])


def _loss_head(h, target):
    rows, d = h.shape

    def body(h_ref, t_ref, dh_ref, l_ref):
        err = h_ref[...] - t_ref[...]
        dh_ref[...] = err * (1.0 / d)
        _accumulate(l_ref, jnp.sum(err * err, axis=0, keepdims=True) * (0.5 / d))

    return _rowwise(body, "loss_head", rows, [(h, d, 0), (target, d, 0)], [(d, F32, False), (d, F32, True)])


def _zoh(lr, li, dt):
    mag = jnp.exp(lr * dt)
    ab_re = mag * jnp.cos(li * dt)
    ab_im = mag * jnp.sin(li * dt)
    num_re = ab_re - 1.0
    den = lr * lr + li * li
    f_re = (num_re * lr + ab_im * li) / den
    f_im = (ab_im * lr - num_re * li) / den
    return ab_re, ab_im, f_re, f_im, den


def _ssm_param_fwd(a_re, a_im, log_dt, bt_re, bt_im):
    g, _, p = a_re.shape
    h = bt_re.shape[1]

    def body(lr_ref, li_ref, ldt_ref, br_ref, bi_ref, pr_ref, pi_ref, bbr_ref, bbi_ref):
        dt = jnp.exp(ldt_ref[...])
        ab_re, ab_im, f_re, f_im, _ = _zoh(lr_ref[...], li_ref[...], dt)
        cr, ci = ab_re, ab_im
        for k in range(SUBLANES):
            pr_ref[:, k:k + 1, :] = cr
            pi_ref[:, k:k + 1, :] = ci
            cr, ci = cr * ab_re - ci * ab_im, cr * ab_im + ci * ab_re
        br, bi = br_ref[...], bi_ref[...]
        bbr_ref[...] = f_re * br - f_im * bi
        bbi_ref[...] = f_re * bi + f_im * br

    pw = jax.ShapeDtypeStruct((g, SUBLANES, p), F32)
    large = jax.ShapeDtypeStruct((g, h, p), F32)
    return pl.pallas_call(body, name="ssm_param_fwd", out_shape=(pw, pw, large, large),
                          compiler_params=_params())(a_re, a_im, log_dt, bt_re, bt_im)


def _ssm_param_bwd(a_re, a_im, log_dt, bt_re, bt_im, gab_re, gab_im, gbb_re, gbb_im):
    g, _, p = a_re.shape
    h = bt_re.shape[1]

    def body(lr_ref, li_ref, ldt_ref, br_ref, bi_ref, gar_ref, gai_ref, gbr_ref, gbi_ref,
             dlr_ref, dli_ref, dldt_ref, dbr_ref, dbi_ref):
        lr, li = lr_ref[...], li_ref[...]
        dt = jnp.exp(ldt_ref[...])
        ab_re, ab_im, f_re, f_im, den = _zoh(lr, li, dt)
        br, bi = br_ref[...], bi_ref[...]
        gbr, gbi = gbr_ref[...], gbi_ref[...]
        dbr_ref[...] = f_re * gbr + f_im * gbi
        dbi_ref[...] = f_re * gbi - f_im * gbr
        gf_re = jnp.sum(br * gbr + bi * gbi, axis=1, keepdims=True)
        gf_im = jnp.sum(br * gbi - bi * gbr, axis=1, keepdims=True)
        il_re, il_im = lr / den, -li / den
        ga_re = gar_ref[...] + il_re * gf_re + il_im * gf_im
        ga_im = gai_ref[...] + il_re * gf_im - il_im * gf_re
        q_re = f_re * il_re - f_im * il_im
        q_im = f_re * il_im + f_im * il_re
        gl_re = -(q_re * gf_re + q_im * gf_im)
        gl_im = -(q_re * gf_im - q_im * gf_re)
        dlr_ref[...] = gl_re + dt * (ab_re * ga_re + ab_im * ga_im)
        dli_ref[...] = gl_im + dt * (ab_re * ga_im - ab_im * ga_re)
        la_re = lr * ab_re - li * ab_im
        la_im = lr * ab_im + li * ab_re
        dldt_ref[...] = jnp.sum((la_re * ga_re + la_im * ga_im) * dt, axis=2, keepdims=True)

    small = jax.ShapeDtypeStruct((g, 1, p), F32)
    one = jax.ShapeDtypeStruct((g, 1, 1), F32)
    large = jax.ShapeDtypeStruct((g, h, p), F32)
    return pl.pallas_call(body, name="ssm_param_bwd", out_shape=(small, small, one, large, large),
                          compiler_params=_params())(
                              a_re, a_im, log_dt, bt_re, bt_im, gab_re, gab_im, gbb_re, gbb_im)


def _block_diag(m):
    g, h, p = m.shape
    nb = g // SSM_BLOCK_GROUPS
    eye = jnp.eye(SSM_BLOCK_GROUPS, dtype=m.dtype)
    m4 = m.reshape(nb, SSM_BLOCK_GROUPS, h, p)
    return (m4[:, :, :, None, :] * eye[None, :, None, :, None]).reshape(nb, SSM_BLOCK_GROUPS * h, SSM_BLOCK_GROUPS * p)


def _block_diag_take(m, h, p):
    nb = m.shape[0]
    eye = jnp.eye(SSM_BLOCK_GROUPS, dtype=m.dtype)
    m5 = m.reshape(nb, SSM_BLOCK_GROUPS, h, SSM_BLOCK_GROUPS, p)
    return jnp.sum(m5 * eye[None, :, None, :, None], axis=3).reshape(nb * SSM_BLOCK_GROUPS, h, p)


def _scan_tile(br, bi, pw_re, pw_im, reverse):
    row = lax.broadcasted_iota(jnp.int32, br.shape, 0)
    xr, xi = br, bi
    for d in (1, 2, 4):
        ar, ai = pw_re[d - 1:d, :], pw_im[d - 1:d, :]
        if reverse:
            keep = row < SUBLANES - d
            sr = jnp.where(keep, pltpu.roll(xr, SUBLANES - d, 0), 0.0)
            si = jnp.where(keep, pltpu.roll(xi, SUBLANES - d, 0), 0.0)
        else:
            keep = row >= d
            sr = jnp.where(keep, pltpu.roll(xr, d, 0), 0.0)
            si = jnp.where(keep, pltpu.roll(xi, d, 0), 0.0)
        xr, xi = xr + ar * sr - ai * si, xi + ar * si + ai * sr
    return xr, xi


def _ssm_blocks(s):
    tt = min(SSM_TIME_BLOCK, s)
    assert s % tt == 0 and tt % SUBLANES == 0
    ch = SSM_BLOCK_GROUPS * SSM_GROUP
    st = SSM_BLOCK_GROUPS * SSM_STATE
    return tt, s // tt, SSM_WIDTH // ch, ch, st


def _ssm_fwd(proj, bc_re, bc_im, cc_re, cc_im, pw_re, pw_im, dvec):
    s = proj.shape[0]
    tt, nt, nb, ch, st = _ssm_blocks(s)
    n_tiles = tt // SUBLANES

    def body(u_ref, bre_ref, bim_ref, cre_ref, cim_ref, pr_ref, pi_ref, d_ref,
             xr_ref, xi_ref, y_ref, z_ref, carry_ref):
        @pl.when(pl.program_id(1) == 0)
        def _():
            carry_ref[...] = jnp.zeros_like(carry_ref)

        u = u_ref[...]
        ub = u.astype(BF16)
        xr_ref[...] = jnp.dot(ub, bre_ref[0], preferred_element_type=F32)
        xi_ref[...] = jnp.dot(ub, bim_ref[0], preferred_element_type=F32)
        pw_r, pw_i = pr_ref[...], pi_ref[...]

        def step(t, carry):
            cr, ci = carry
            rows = pl.ds(pl.multiple_of(t * SUBLANES, SUBLANES), SUBLANES)
            xr, xi = _scan_tile(xr_ref[rows, :], xi_ref[rows, :], pw_r, pw_i, False)
            xr, xi = xr + pw_r * cr - pw_i * ci, xi + pw_r * ci + pw_i * cr
            xr_ref[rows, :] = xr
            xi_ref[rows, :] = xi
            return (jnp.broadcast_to(xr[SUBLANES - 1:SUBLANES, :], (SUBLANES, st)),
                    jnp.broadcast_to(xi[SUBLANES - 1:SUBLANES, :], (SUBLANES, st)))

        cr, ci = lax.fori_loop(0, n_tiles, step, (carry_ref[0], carry_ref[1]), unroll=2)
        carry_ref[0] = cr
        carry_ref[1] = ci
        y = (lax.dot_general(xr_ref[...].astype(BF16), cre_ref[0], _NT, preferred_element_type=F32)
             + lax.dot_general(xi_ref[...].astype(BF16), cim_ref[0], _NT, preferred_element_type=F32)
             + d_ref[...] * u)
        y_ref[...] = y
        z_ref[...] = _gelu(y).astype(BF16)

    chan = pl.BlockSpec((tt, ch), lambda b, t: (t, b))
    state = pl.BlockSpec((tt, st), lambda b, t: (t, b))
    mat = pl.BlockSpec((1, ch, st), lambda b, t: (b, 0, 0))
    pw = pl.BlockSpec((SUBLANES, st), lambda b, t: (0, b))
    vec = pl.BlockSpec((1, ch), lambda b, t: (0, b))
    x_shape = jax.ShapeDtypeStruct((s, nb * st), F32)
    return pl.pallas_call(
        body, name="ssm_fwd", grid=(nb, nt), in_specs=[chan, mat, mat, mat, mat, pw, pw, vec],
        out_specs=(state, state, chan, chan),
        out_shape=(x_shape, x_shape, jax.ShapeDtypeStruct((s, nb * ch), F32), jax.ShapeDtypeStruct((s, nb * ch), BF16)),
        scratch_shapes=[pltpu.VMEM((2, SUBLANES, st), F32)],
        compiler_params=_params(("parallel", "arbitrary")))(proj, bc_re, bc_im, cc_re, cc_im, pw_re, pw_im, dvec)


def _ssm_bwd(dz, y, proj, x_re, x_im, bc_re, bc_im, cc_re, cc_im, pw_re, pw_im, dvec):
    s = proj.shape[0]
    tt, nt, nb, ch, st = _ssm_blocks(s)
    n_tiles = tt // SUBLANES

    def body(dz_ref, y_ref, u_ref, xr_ref, xi_ref, bre_ref, bim_ref, cre_ref, cim_ref, pr_ref, pi_ref, d_ref,
             du_ref, dbr_ref, dbi_ref, dcr_ref, dci_ref, gar_ref, gai_ref, dd_ref,
             lr_ref, li_ref, carry_ref, acc_ref):
        @pl.when(pl.program_id(1) == 0)
        def _():
            for ref in (carry_ref, acc_ref, dbr_ref, dbi_ref, dcr_ref, dci_ref, dd_ref):
                ref[...] = jnp.zeros_like(ref)

        u = u_ref[...]
        dy = dz_ref[...] * _gelu_grad(y_ref[...])
        ub, dyb = u.astype(BF16), dy.astype(BF16)
        lr_ref[...] = jnp.dot(dyb, cre_ref[0], preferred_element_type=F32)
        li_ref[...] = jnp.dot(dyb, cim_ref[0], preferred_element_type=F32)
        pw_r, pw_i = pr_ref[...], -pi_ref[...]
        pwc_r = jnp.concatenate([pw_r[SUBLANES - 1 - i:SUBLANES - i, :] for i in range(SUBLANES)], axis=0)
        pwc_i = jnp.concatenate([pw_i[SUBLANES - 1 - i:SUBLANES - i, :] for i in range(SUBLANES)], axis=0)
        row = lax.broadcasted_iota(jnp.int32, (SUBLANES, st), 0)

        def step(t, carry):
            cr, ci, acc_r, acc_i = carry
            rows = pl.ds(pl.multiple_of((n_tiles - 1 - t) * SUBLANES, SUBLANES), SUBLANES)
            lr, li = _scan_tile(lr_ref[rows, :], li_ref[rows, :], pw_r, pw_i, True)
            lr, li = lr + pwc_r * cr - pwc_i * ci, li + pwc_r * ci + pwc_i * cr
            lr_ref[rows, :] = lr
            li_ref[rows, :] = li
            nr = jnp.where(row < SUBLANES - 1, pltpu.roll(lr, SUBLANES - 1, 0), cr)
            ni = jnp.where(row < SUBLANES - 1, pltpu.roll(li, SUBLANES - 1, 0), ci)
            xr, xi = xr_ref[rows, :], xi_ref[rows, :]
            return (jnp.broadcast_to(lr[0:1, :], (SUBLANES, st)), jnp.broadcast_to(li[0:1, :], (SUBLANES, st)),
                    acc_r + xr * nr + xi * ni, acc_i + xr * ni - xi * nr)

        cr, ci, acc_r, acc_i = lax.fori_loop(
            0, n_tiles, step, (carry_ref[0], carry_ref[1], acc_ref[0], acc_ref[1]), unroll=2)
        carry_ref[0], carry_ref[1] = cr, ci
        acc_ref[0], acc_ref[1] = acc_r, acc_i
        lrb, lib = lr_ref[...].astype(BF16), li_ref[...].astype(BF16)
        du_ref[...] = (lax.dot_general(lrb, bre_ref[0], _NT, preferred_element_type=F32)
                       + lax.dot_general(lib, bim_ref[0], _NT, preferred_element_type=F32) + dy * d_ref[...])
        dbr_ref[0] += lax.dot_general(ub, lrb, _TN, preferred_element_type=F32)
        dbi_ref[0] += lax.dot_general(ub, lib, _TN, preferred_element_type=F32)
        dcr_ref[0] += lax.dot_general(dyb, xr_ref[...].astype(BF16), _TN, preferred_element_type=F32)
        dci_ref[0] += lax.dot_general(dyb, xi_ref[...].astype(BF16), _TN, preferred_element_type=F32)
        dd_ref[...] += jnp.sum(dy * u, axis=0, keepdims=True)

        @pl.when(pl.program_id(1) == nt - 1)
        def _():
            gar_ref[...] = jnp.sum(acc_r, axis=0, keepdims=True)
            gai_ref[...] = jnp.sum(acc_i, axis=0, keepdims=True)

    chan = pl.BlockSpec((tt, ch), lambda b, t: (nt - 1 - t, b))
    state = pl.BlockSpec((tt, st), lambda b, t: (nt - 1 - t, b))
    mat = pl.BlockSpec((1, ch, st), lambda b, t: (b, 0, 0))
    pw = pl.BlockSpec((SUBLANES, st), lambda b, t: (0, b))
    vec = pl.BlockSpec((1, ch), lambda b, t: (0, b))
    svec = pl.BlockSpec((1, st), lambda b, t: (0, b))
    mat_shape = jax.ShapeDtypeStruct((nb, ch, st), F32)
    return pl.pallas_call(
        body, name="ssm_bwd", grid=(nb, nt),
        in_specs=[chan, chan, chan, state, state, mat, mat, mat, mat, pw, pw, vec],
        out_specs=(chan, mat, mat, mat, mat, svec, svec, vec),
        out_shape=(jax.ShapeDtypeStruct((s, nb * ch), F32), mat_shape, mat_shape, mat_shape, mat_shape,
                   jax.ShapeDtypeStruct((1, nb * st), F32), jax.ShapeDtypeStruct((1, nb * st), F32),
                   jax.ShapeDtypeStruct((1, nb * ch), F32)),
        scratch_shapes=[pltpu.VMEM((tt, st), F32), pltpu.VMEM((tt, st), F32),
                        pltpu.VMEM((2, SUBLANES, st), F32), pltpu.VMEM((2, SUBLANES, st), F32)],
        compiler_params=_params(("parallel", "arbitrary")))(
            dz, y, proj, x_re, x_im, bc_re, bc_im, cc_re, cc_im, pw_re, pw_im, dvec)


def _split_dot(x, tri2):
    hi = x.astype(BF16)
    lo = (x - hi.astype(F32)).astype(BF16)
    return jnp.dot(jnp.concatenate([hi, lo], axis=1), tri2, preferred_element_type=F32)


def _tri_and_ones(tk, rel):
    r_i = lax.broadcasted_iota(jnp.int32, (2 * tk, 2 * tk), 0) % tk
    c_i = lax.broadcasted_iota(jnp.int32, (2 * tk, 2 * tk), 1)
    return jnp.where((c_i >= tk) | rel(r_i, c_i), 1.0, 0.0).astype(BF16)


def _attn_scores(qs, kb, own):
    z = lax.dot_general(qs, kb, _NT, preferred_element_type=F32)
    log_beta = jnp.minimum(z, 0.0) - jnp.log(1.0 + jnp.exp(-jnp.abs(z)))
    log_stay = log_beta - z
    if not own:
        return log_beta, log_stay, None
    mask = lax.broadcasted_iota(jnp.int32, z.shape, 1) < lax.broadcasted_iota(jnp.int32, z.shape, 0)
    return log_beta, jnp.where(mask, log_stay, 0.0), mask


def _attn_blocks(s):
    tq, tk = min(ATT_TQ, s), min(ATT_TK, s)
    assert s % tq == 0 and tq % tk == 0
    return tq, tk, tq // tk


def _first_head(shape, axis):
    return lax.broadcasted_iota(jnp.int32, shape, axis) < HEAD_DIM


def _pair_rms(x, g):
    first = _first_head(x.shape, 1)
    sq = x * x
    ms_a = jnp.sum(jnp.where(first, sq, 0.0), axis=1, keepdims=True) * (1.0 / HEAD_DIM)
    ms_b = jnp.sum(jnp.where(first, 0.0, sq), axis=1, keepdims=True) * (1.0 / HEAD_DIM)
    r = jnp.where(first, lax.rsqrt(ms_a + RMS_EPS), lax.rsqrt(ms_b + RMS_EPS))
    return x * r * g, r


def _pair_rms_bwd(x, g, r, dn):
    first = _first_head(x.shape, 1)
    a = dn * g
    ax = a * x
    dot_a = jnp.sum(jnp.where(first, ax, 0.0), axis=1, keepdims=True) * (1.0 / HEAD_DIM)
    dot_b = jnp.sum(jnp.where(first, 0.0, ax), axis=1, keepdims=True) * (1.0 / HEAD_DIM)
    dx = r * a - x * (r * r * r * jnp.where(first, dot_a, dot_b))
    return dx, jnp.sum(dn * x * r, axis=0, keepdims=True)


_Q_BLOCK0 = 2 * SSM_WIDTH // (2 * HEAD_DIM)
_K_BLOCK0 = _Q_BLOCK0 + ATTN_WIDTH // (2 * HEAD_DIM)
_V_BLOCK0 = _K_BLOCK0 + ATTN_WIDTH // (2 * HEAD_DIM)
HEAD_PAIRS = ATTN_HEADS // 2


def _fill_keys(k_ref, v_ref, gk_ref, kn_ref, vb_ref, rows_per_step):
    def fill(c, _):
        rows = pl.ds(pl.multiple_of(c * rows_per_step, rows_per_step), rows_per_step)
        kn, _ = _pair_rms(k_ref[rows, :], gk_ref[...])
        kn_ref[rows, :] = kn.astype(BF16)
        vb_ref[rows, :] = v_ref[rows, :].astype(BF16)
        return 0

    lax.fori_loop(0, k_ref.shape[0] // rows_per_step, fill, 0)


def _attn_fwd(proj, gq, gk):
    s = proj.shape[0]
    tq, tk, r = _attn_blocks(s)
    pw = 2 * HEAD_DIM

    def body(q_ref, k_ref, v_ref, gq_ref, gk_ref, o_ref, b_ref, kn_ref, vb_ref):
        qi = pl.program_id(1)

        @pl.when(qi == 0)
        def _():
            _fill_keys(k_ref, v_ref, gk_ref, kn_ref, vb_ref, tq)

        qn, _ = _pair_rms(q_ref[...], gq_ref[...])
        qn = qn * (HEAD_DIM ** -0.5)
        first = _first_head((tq, pw), 1)
        tri = _tri_and_ones(tk, lambda row, col: row > col)
        qs = [jnp.where(first, qn, 0.0).astype(BF16), jnp.where(first, 0.0, qn).astype(BF16)]

        def tiles(kq, carry, own):
            o_acc, accs = carry[0], list(carry[1:])
            ks = pl.ds(pl.multiple_of(kq * tq, tq), tq)
            kb, vb = kn_ref[ks, :], vb_ref[ks, :]
            scores = [_attn_scores(q, kb, own) for q in qs]
            sums = [[_split_dot(sc[1][:, u * tk:(u + 1) * tk], tri) for u in range(r)] for sc in scores]
            ws = []
            for h in range(2):
                later = [None] * r
                for u in reversed(range(r)):
                    later[u] = accs[h] + sums[h][u][:, :tk]
                    accs[h] = accs[h] + sums[h][u][:, tk:]
                w = jnp.exp(scores[h][0] + jnp.concatenate(later, axis=1))
                if own:
                    w = jnp.where(scores[h][2], w, 0.0)
                ws.append(w.astype(BF16))
            po = [jnp.dot(w, vb, preferred_element_type=F32) for w in ws]
            return (o_acc + jnp.where(first, po[0], po[1]), *accs)

        zero = jnp.zeros((tq, tk), F32)
        carry = tiles(qi, (jnp.zeros((tq, pw), F32), zero, zero), True)
        o_acc, acc_a, acc_b = lax.fori_loop(0, qi, lambda it, c: tiles(qi - 1 - it, c, False), carry)
        o_ref[...] = o_acc
        b_ref[0] = acc_a[:, 0:1]
        b_ref[1] = acc_b[:, 0:1]

    def cols(block0):
        return pl.BlockSpec((s, pw), lambda hp, i: (0, block0 + hp))

    gain = pl.BlockSpec((1, pw), lambda hp, i: (0, 0))
    return pl.pallas_call(
        body, name="attn_fwd", grid=(HEAD_PAIRS, s // tq),
        in_specs=[pl.BlockSpec((tq, pw), lambda hp, i: (i, _Q_BLOCK0 + hp)), cols(_K_BLOCK0), cols(_V_BLOCK0),
                  gain, gain],
        out_specs=(pl.BlockSpec((tq, pw), lambda hp, i: (i, hp)), pl.BlockSpec((2, tq, 1), lambda hp, i: (hp, i, 0))),
        out_shape=(jax.ShapeDtypeStruct((s, ATTN_WIDTH), F32), jax.ShapeDtypeStruct((ATTN_HEADS, s, 1), F32)),
        scratch_shapes=[pltpu.VMEM((s, pw), BF16), pltpu.VMEM((s, pw), BF16)],
        compiler_params=_params(("parallel", "arbitrary")))(proj, proj, proj, gq, gk)


def _attn_bwd(proj, gq, gk, bsum, do):
    s = proj.shape[0]
    tq, tk, r = _attn_blocks(s)
    nq = s // tq
    pw = 2 * HEAD_DIM

    def body(q_ref, k_ref, v_ref, gq_ref, gk_ref, b_ref, do_ref,
             dq_ref, dk_ref, dv_ref, dgq_ref, dgk_ref, kn_ref, vb_ref, dkt_ref, dvt_ref):
        qi = pl.program_id(1)

        @pl.when(qi == 0)
        def _():
            _fill_keys(k_ref, v_ref, gk_ref, kn_ref, vb_ref, tq)
            for ref in (dkt_ref, dvt_ref, dgq_ref):
                ref[...] = jnp.zeros_like(ref)

        scale = HEAD_DIM ** -0.5
        q = q_ref[...]
        qn, rq = _pair_rms(q, gq_ref[...])
        qf = qn * scale
        qft = qf.T
        dof = do_ref[...]
        doft = dof.T
        first = _first_head((tq, pw), 1)
        first_t = _first_head((pw, tq), 0)
        tri_upto = _tri_and_ones(tk, lambda row, col: row <= col)
        tri_before = _tri_and_ones(tk, lambda row, col: row < col)
        heads = ((first, first_t), (~first, ~first_t))
        qs = [jnp.where(sel, qf, 0.0).astype(BF16) for sel, _ in heads]
        qst = [jnp.where(sel_t, qft, 0.0).astype(BF16) for _, sel_t in heads]
        dob = [jnp.where(sel, dof, 0.0).astype(BF16) for sel, _ in heads]
        dobt = [jnp.where(sel_t, doft, 0.0).astype(BF16) for _, sel_t in heads]

        def tiles(kq, carry, own):
            dq_acc = carry[0]
            rest, g_prefix = list(carry[1:3]), list(carry[3:5])
            ks = pl.ds(pl.multiple_of(kq * tq, tq), tq)
            kb, vb = kn_ref[ks, :], vb_ref[ks, :]
            scores = [_attn_scores(q, kb, own) for q in qs]
            dws = [lax.dot_general(d, vb, _NT, preferred_element_type=F32) for d in dob]
            sums = [[_split_dot(sc[1][:, u * tk:(u + 1) * tk], tri_upto) for u in range(r)] for sc in scores]
            ws, gs = [], []
            for h in range(2):
                later = [None] * r
                for u in range(r):
                    later[u] = rest[h] - sums[h][u][:, :tk]
                    rest[h] = rest[h] - sums[h][u][:, tk:]
                w = jnp.exp(scores[h][0] + jnp.concatenate(later, axis=1))
                if own:
                    w = jnp.where(scores[h][2], w, 0.0)
                ws.append(w)
                gs.append(w * dws[h])
            gsums = [[_split_dot(g[:, u * tk:(u + 1) * tk], tri_before) for u in range(r)] for g in gs]
            dzs = []
            for h in range(2):
                before = [None] * r
                for u in range(r):
                    before[u] = g_prefix[h] + gsums[h][u][:, :tk]
                    g_prefix[h] = g_prefix[h] + gsums[h][u][:, tk:]
                dz = gs[h] - jnp.exp(scores[h][0]) * (gs[h] + jnp.concatenate(before, axis=1))
                if own:
                    dz = jnp.where(scores[h][2], dz, 0.0)
                dzs.append(dz.astype(BF16))
            pq = [jnp.dot(dz, kb, preferred_element_type=F32) for dz in dzs]
            dkt_ref[:, ks] += (jnp.dot(qst[0], dzs[0], preferred_element_type=F32)
                               + jnp.dot(qst[1], dzs[1], preferred_element_type=F32))
            dvt_ref[:, ks] += (jnp.dot(dobt[0], ws[0].astype(BF16), preferred_element_type=F32)
                               + jnp.dot(dobt[1], ws[1].astype(BF16), preferred_element_type=F32))
            return (dq_acc + jnp.where(first, pq[0], pq[1]), *rest, *g_prefix)

        zero = jnp.zeros((tq, tk), F32)
        carry = (jnp.zeros((tq, pw), F32), jnp.broadcast_to(b_ref[0], (tq, tk)), jnp.broadcast_to(b_ref[1], (tq, tk)),
                 zero, zero)
        carry = lax.fori_loop(0, qi, lambda kq, c: tiles(kq, c, False), carry)
        dq, dgq = _pair_rms_bwd(q, gq_ref[...], rq, tiles(qi, carry, True)[0] * scale)
        dq_ref[...] = dq
        dgq_ref[0] += dgq

        @pl.when(qi == nq - 1)
        def _():
            def finish(c, dgk):
                rows = pl.ds(pl.multiple_of(c * tq, tq), tq)
                k = k_ref[rows, :]
                _, rk = _pair_rms(k, gk_ref[...])
                dk, dgk_c = _pair_rms_bwd(k, gk_ref[...], rk, dkt_ref[:, rows].T)
                dk_ref[rows, :] = dk
                dv_ref[rows, :] = dvt_ref[:, rows].T
                return dgk + dgk_c

            dgk_ref[0] = lax.fori_loop(0, nq, finish, jnp.zeros((1, pw), F32))

    def cols(block0):
        return pl.BlockSpec((s, pw), lambda hp, i: (0, block0 + hp))

    gain = pl.BlockSpec((1, pw), lambda hp, i: (0, 0))
    q_rows = pl.BlockSpec((tq, pw), lambda hp, i: (i, hp))
    all_rows = pl.BlockSpec((s, pw), lambda hp, i: (0, hp))
    pair_gain = pl.BlockSpec((1, 1, pw), lambda hp, i: (hp, 0, 0))
    wide = jax.ShapeDtypeStruct((s, ATTN_WIDTH), F32)
    gains = jax.ShapeDtypeStruct((HEAD_PAIRS, 1, pw), F32)
    return pl.pallas_call(
        body, name="attn_bwd", grid=(HEAD_PAIRS, nq),
        in_specs=[pl.BlockSpec((tq, pw), lambda hp, i: (i, _Q_BLOCK0 + hp)), cols(_K_BLOCK0), cols(_V_BLOCK0),
                  gain, gain, pl.BlockSpec((2, tq, 1), lambda hp, i: (hp, i, 0)), q_rows],
        out_specs=(q_rows, all_rows, all_rows, pair_gain, pair_gain),
        out_shape=(wide, wide, wide, gains, gains),
        scratch_shapes=[pltpu.VMEM((s, pw), BF16), pltpu.VMEM((s, pw), BF16),
                        pltpu.VMEM((pw, s), F32), pltpu.VMEM((pw, s), F32)],
        compiler_params=_params(("parallel", "arbitrary")))(proj, proj, proj, gq, gk, bsum, do)


def _reduce_adamw(parts, w, m, v, *, name, row0=0):
    n, _, cols = parts.shape
    rows = w.shape[0]
    tile_rows = 2 * SUBLANES
    tr = max(t for t in range(tile_rows, min(rows, 512) + 1, tile_rows) if rows % t == 0 and row0 % t == 0)
    c1 = 1.0 - ADAM_B1 ** ADAM_STEP
    c2 = 1.0 - ADAM_B2 ** ADAM_STEP

    def body(p_ref, w_ref, m_ref, v_ref, g_ref, d_ref, nm_ref, nv_ref):
        g = p_ref[0].astype(F32)
        for i in range(1, n):
            g = g + p_ref[i].astype(F32)
        nm = ADAM_B1 * m_ref[...] + (1.0 - ADAM_B1) * g
        nv = ADAM_B2 * v_ref[...] + (1.0 - ADAM_B2) * (g * g)
        g_ref[...] = g
        nm_ref[...] = nm
        nv_ref[...] = nv
        d_ref[...] = -ADAM_LR * ((nm / c1) / (jnp.sqrt(nv / c2) + ADAM_EPS) + ADAM_WD * w_ref[...])

    row = pl.BlockSpec((tr, cols), lambda i: (i, 0))
    out = jax.ShapeDtypeStruct((rows, cols), F32)
    return pl.pallas_call(
        body, name=name, grid=(rows // tr,),
        in_specs=[pl.BlockSpec((n, tr, cols), lambda i: (0, row0 // tr + i, 0)), row, row, row],
        out_specs=(row,) * 4, out_shape=(out,) * 4, compiler_params=_params(("parallel",)))(parts, w, m, v)


def _all_gather(shards, *, name):
    n_arr = len(shards)

    def body(*refs):
        x_refs, out_refs = refs[:n_arr], refs[n_arr:2 * n_arr]
        send_sems, recv_sems, local_sems = refs[2 * n_arr:]
        x, y, c = lax.axis_index("x"), lax.axis_index("y"), lax.axis_index("c")
        me, sibling = (x, y, c), (x, y, 1 - c)
        chips = [(1 - x, y), (x, 1 - y), (1 - x, 1 - y)]

        def slot(a, px, py, pc):
            return out_refs[a].at[4 * px + 2 * py + pc]

        def copy(a, k, block, to, src=None):
            return pltpu.make_async_remote_copy(
                src_ref=slot(a, *block) if src is None else src, dst_ref=slot(a, *block),
                send_sem=send_sems.at[a, k], recv_sem=recv_sems.at[a, k], device_id=to, device_id_type=MESH_IDS)

        arrays = range(n_arr)
        mine = [pltpu.make_async_copy(x_refs[a], slot(a, *me), local_sems.at[a]) for a in arrays]
        for cp in mine:
            cp.start()
        first = [copy(a, 0, me, sibling, src=x_refs[a]) for a in arrays]
        first += [copy(a, 1 + j, me, (*chip, c), src=x_refs[a]) for j, chip in enumerate(chips) for a in arrays]
        for cp in first:
            cp.start()
        passed = []
        for j, chip in enumerate(chips):
            for a in arrays:
                copy(a, 1 + j, (*chip, c), me).wait_recv()
                passed.append(copy(a, 4 + j, (*chip, c), sibling))
                passed[-1].start()
        for a in arrays:
            copy(a, 0, sibling, me).wait_recv()
        for j, chip in enumerate(chips):
            for a in arrays:
                copy(a, 4 + j, (*chip, 1 - c), me).wait_recv()
        for cp in first + passed:
            cp.wait_send()
        for cp in mine:
            cp.wait()

    any_spec = pl.BlockSpec(memory_space=pl.ANY)
    return pl.pallas_call(
        body, name=name, out_shape=tuple(jax.ShapeDtypeStruct((N_DEV,) + s.shape, s.dtype) for s in shards),
        in_specs=[any_spec] * n_arr, out_specs=(any_spec,) * n_arr,
        scratch_shapes=[pltpu.SemaphoreType.DMA((n_arr, 7)), pltpu.SemaphoreType.DMA((n_arr, 7)),
                        pltpu.SemaphoreType.DMA((n_arr,))],
    )(*shards)


def _all_to_all(bufs, *, name):
    n_arr = len(bufs)

    def body(*refs):
        x_refs, out_refs = refs[:n_arr], refs[n_arr:2 * n_arr]
        send_sems, recv_sems, local_sems = refs[2 * n_arr:]
        x, y, c = lax.axis_index("x"), lax.axis_index("y"), lax.axis_index("c")
        me = 4 * x + 2 * y + c
        mine = [pltpu.make_async_copy(x_refs[a].at[me], out_refs[a].at[me], local_sems.at[a]) for a in range(n_arr)]
        for cp in mine:
            cp.start()

        def copy(a, k, landing):
            px, py, pc = x ^ (k >> 2), y ^ ((k >> 1) & 1), c ^ (k & 1)
            peer = 4 * px + 2 * py + pc
            return pltpu.make_async_remote_copy(
                src_ref=x_refs[a].at[peer], dst_ref=out_refs[a].at[peer if landing else me],
                send_sem=send_sems.at[a, k - 1], recv_sem=recv_sems.at[a, k - 1],
                device_id=(px, py, pc), device_id_type=MESH_IDS)

        copies = [copy(a, k, False) for k in range(1, N_DEV) for a in range(n_arr)]
        for cp in copies:
            cp.start()
        for k in range(1, N_DEV):
            for a in range(n_arr):
                copy(a, k, True).wait_recv()
        for cp in copies:
            cp.wait_send()
        for cp in mine:
            cp.wait()

    any_spec = pl.BlockSpec(memory_space=pl.ANY)
    return pl.pallas_call(
        body, name=name, out_shape=tuple(jax.ShapeDtypeStruct(b.shape, b.dtype) for b in bufs),
        in_specs=[any_spec] * n_arr, out_specs=(any_spec,) * n_arr,
        scratch_shapes=[pltpu.SemaphoreType.DMA((n_arr, 7)), pltpu.SemaphoreType.DMA((n_arr, 7)),
                        pltpu.SemaphoreType.DMA((n_arr,))],
    )(*bufs)


def _pack_small(vals):
    flat = jnp.concatenate([vals[n].reshape(-1) for n in SMALL])
    rows = -(-flat.shape[0] // (SUBLANES * PACK_COLS)) * SUBLANES
    return jnp.pad(flat, (0, rows * PACK_COLS - flat.shape[0])).reshape(rows, PACK_COLS)


def _unpack_small(flat, shapes):
    flat = flat.reshape(-1)
    out, off = {}, 0
    for n in SMALL:
        size = math.prod(shapes[n])
        out[n] = flat[off:off + size].reshape(shapes[n])
        off += size
    return out


def _pair_gain(g):
    return jnp.tile(g, 2)[None]


def _ssm_setup(sp):
    g, p = SSM_GROUPS, SSM_STATE
    a_re = sp["ssm_a_re"][:, None, :]
    a_im = sp["ssm_a_im"][:, None, :]
    log_dt = jnp.broadcast_to(sp["ssm_log_dt"][:, None, None], (g, 1, p))
    bt_re = sp["ssm_b_re"].transpose(0, 2, 1)
    bt_im = sp["ssm_b_im"].transpose(0, 2, 1)
    return a_re, a_im, log_dt, bt_re, bt_im


def _layer_fwd(h, p_l, w, sp):
    s = h.shape[0]
    hn = _rms_fwd(h, sp["mix_norm_g"][None], name="rms_mix")
    proj = _mm(hn, w["w_in"], name="mm_proj")

    pw_re, pw_im, bbt_re, bbt_im = _ssm_param_fwd(*_ssm_setup(sp))
    pw_re = pw_re.transpose(1, 0, 2).reshape(SUBLANES, SSM_LANES)
    pw_im = pw_im.transpose(1, 0, 2).reshape(SUBLANES, SSM_LANES)
    ssm_mats = (_block_diag(bbt_re).astype(BF16), _block_diag(bbt_im).astype(BF16),
                _block_diag(sp["ssm_c_re"]).astype(BF16), _block_diag(-sp["ssm_c_im"]).astype(BF16),
                pw_re, pw_im, sp["ssm_d"].reshape(1, SSM_WIDTH))
    x_re, x_im, y, z = _ssm_fwd(proj, *ssm_mats)
    zz = _mm(z, w["ssm_w_glu"], name="mm_glu")

    o, bsum = _attn_fwd(proj, _pair_gain(sp["q_norm_g"]), _pair_gain(sp["k_norm_g"]))

    bglu = sp["ssm_b_glu"][None]
    ycat = _gate_fwd(zz, bglu, proj, o)
    h2 = _mm(ycat, w["w_out"], add=h, name="mm_out")
    hn2 = _rms_fwd(h2, sp["ple_norm_g"][None], name="rms_ple")
    pgl = _mm(hn2, w["w_ple_gate"], name="mm_ple_gate")
    pp = _mm(p_l, w["w_ple_proj"], name="mm_ple_proj")
    h3 = _ple_out(h2, pgl, pp)
    saved = dict(h=h, hn=hn, proj=proj, ssm_mats=ssm_mats, x_re=x_re, x_im=x_im, y=y, z=z,
                 zz=zz, bsum=bsum, o=o, ycat=ycat, h2=h2, hn2=hn2,
                 pgl=pgl, pp=pp, p_l=p_l, bglu=bglu)
    return h3, saved


def _layer_bwd(dh3, sv, w, sp):
    s = dh3.shape[0]
    g, hh, p = SSM_GROUPS, SSM_GROUP, SSM_STATE
    grads = {}
    dgp, dpp = _ple_bwd(dh3, sv["pgl"], sv["pp"])
    grads["w_ple_gate"] = _mm(sv["hn2"], dgp, ta=True, out_dtype=BF16, name="mm_d_ple_gate")
    grads["w_ple_proj"] = _mm(sv["p_l"], dpp, ta=True, out_dtype=BF16, name="mm_d_ple_proj")
    dhn2 = _mm(dgp, w["w_ple_gate"], tb=True, name="mm_dhn2")
    dh2, dg2 = _rms_bwd(sv["h2"], sp["ple_norm_g"][None], dhn2, dres=dh3, name="rms_ple_bwd")
    grads["ple_norm_g"] = dg2[0]
    grads["w_out"] = _mm(sv["ycat"], dh2, ta=True, out_dtype=BF16, name="mm_d_out")
    dyc = _mm(dh2, w["w_out"], tb=True, name="mm_dycat")
    dzz, dgs, do, dga, dbglu = _gate_bwd(dyc, sv["zz"], sv["bglu"], sv["proj"], sv["o"])
    grads["ssm_b_glu"] = dbglu[0]

    dq, dk, dv, dgq, dgk = _attn_bwd(sv["proj"], _pair_gain(sp["q_norm_g"]), _pair_gain(sp["k_norm_g"]),
                                     sv["bsum"], do)
    grads["q_norm_g"] = jnp.sum(dgq.reshape(ATTN_HEADS, HEAD_DIM), axis=0)
    grads["k_norm_g"] = jnp.sum(dgk.reshape(ATTN_HEADS, HEAD_DIM), axis=0)

    grads["ssm_w_glu"] = _mm(sv["z"], dzz, ta=True, out_dtype=BF16, name="mm_d_glu")
    dz = _mm(dzz, w["ssm_w_glu"], tb=True, name="mm_dz")
    du, dbc_re, dbc_im, dcc_re, dcc_im, gab_re, gab_im, dd = _ssm_bwd(
        dz, sv["y"], sv["proj"], sv["x_re"], sv["x_im"], *sv["ssm_mats"])
    grads["ssm_d"] = dd.reshape(g, hh)
    grads["ssm_c_re"] = _block_diag_take(dcc_re, hh, p)
    grads["ssm_c_im"] = -_block_diag_take(dcc_im, hh, p)
    da_re, da_im, dldt, dbt_re, dbt_im = _ssm_param_bwd(
        *_ssm_setup(sp), gab_re.reshape(g, 1, p), gab_im.reshape(g, 1, p),
        _block_diag_take(dbc_re, hh, p), _block_diag_take(dbc_im, hh, p))
    grads["ssm_a_re"], grads["ssm_a_im"] = da_re[:, 0, :], da_im[:, 0, :]
    grads["ssm_log_dt"] = dldt[:, 0, 0]
    grads["ssm_b_re"], grads["ssm_b_im"] = dbt_re.transpose(0, 2, 1), dbt_im.transpose(0, 2, 1)

    dproj = jnp.concatenate([du, dgs, dq, dk, dv, dga], axis=1)
    grads["w_in"] = _mm(sv["hn"], dproj, ta=True, out_dtype=BF16, name="mm_d_in")
    dhn = _mm(dproj, w["w_in"], tb=True, name="mm_dhn")
    dh, dg1 = _rms_bwd(sv["h"], sp["mix_norm_g"][None], dhn, dres=dh2, name="rms_mix_bwd")
    grads["mix_norm_g"] = dg1[0]
    return dh, grads


def _local_step(x, p, target, big, small):
    h = x
    saved = []
    for l in range(DEPTH):
        sp = {n: small[n][l] for n in SMALL}
        h, sv = _layer_fwd(h, p[l], big[l], sp)
        saved.append(sv)
    dh, loss_parts = _loss_head(h, target)
    grads = [None] * DEPTH
    for l in reversed(range(DEPTH)):
        sp = {n: small[n][l] for n in SMALL}
        dh, grads[l] = _layer_bwd(dh, saved[l], big[l], sp)
    return jnp.sum(loss_parts), dh, grads


def kernel(x, p, mix_norm_g, w_in, ssm_a_re, ssm_a_im, ssm_log_dt, ssm_b_re, ssm_b_im, ssm_c_re, ssm_c_im, ssm_d, ssm_w_glu, ssm_b_glu, q_norm_g, k_norm_g, w_out, ple_norm_g, w_ple_gate, w_ple_proj, loss_target, m_mix_norm_g, m_w_in, m_ssm_a_re, m_ssm_a_im, m_ssm_log_dt, m_ssm_b_re, m_ssm_b_im, m_ssm_c_re, m_ssm_c_im, m_ssm_d, m_ssm_w_glu, m_ssm_b_glu, m_q_norm_g, m_k_norm_g, m_w_out, m_ple_norm_g, m_w_ple_gate, m_w_ple_proj, v_mix_norm_g, v_w_in, v_ssm_a_re, v_ssm_a_im, v_ssm_log_dt, v_ssm_b_re, v_ssm_b_im, v_ssm_c_re, v_ssm_c_im, v_ssm_d, v_ssm_w_glu, v_ssm_b_glu, v_q_norm_g, v_k_norm_g, v_w_out, v_ple_norm_g, v_w_ple_gate, v_w_ple_proj):
    given = dict(locals())
    wts = {n: given[n] for n in WEIGHTS}
    mom = {n: given["m_" + n] for n in WEIGHTS}
    var = {n: given["v_" + n] for n in WEIGHTS}

    def rows2d(a):
        return a.reshape(-1, a.shape[-1])

    groups = (("w_in",), ("ssm_w_glu", "w_ple_proj"), ("w_out", "w_ple_gate"))
    shards = [jnp.concatenate([rows2d(wts[n]).astype(BF16) for n in names], axis=0) for names in groups]
    gathered = _all_gather(shards, name="gather_weights")
    big = [{} for _ in range(DEPTH)]
    for names, got in zip(groups, gathered):
        row0 = 0
        for n in names:
            r = wts[n].shape[1]
            for l in range(DEPTH):
                blocks = got[:, row0:row0 + r, :]
                if n in ROW_SHARDED:
                    big[l][n] = blocks.reshape(N_DEV * r, -1)
                else:
                    big[l][n] = blocks.transpose(1, 0, 2).reshape(r, -1)
                row0 += r

    small = {n: wts[n] for n in SMALL}
    loss, dx, grads = _local_step(x[0], p[:, 0], loss_target[0], big, small)
    loss = lax.psum(loss, ("x", "y", "c"))

    def blocks_of(n, l):
        g = grads[l][n]
        if n in ROW_SHARDED:
            return g.reshape(N_DEV, -1, g.shape[1])
        return g.reshape(g.shape[0], N_DEV, -1).transpose(1, 0, 2)

    send = [jnp.concatenate([blocks_of(n, l) for n in names for l in range(DEPTH)], axis=1) for names in groups]
    parts = _all_to_all(send, name="exchange_weight_grads")
    big_out = {}
    for names, part in zip(groups, parts):
        row0 = 0
        for n in names:
            res = _reduce_adamw(part, rows2d(wts[n]), rows2d(mom[n]), rows2d(var[n]), row0=row0, name="adamw_" + n)
            big_out[n] = [a.reshape(wts[n].shape) for a in res]
            row0 += rows2d(wts[n]).shape[0]

    small_shapes = {n: wts[n].shape for n in SMALL}
    local_small = _pack_small({n: jnp.stack([grads[l][n] for l in range(DEPTH)]) for n in SMALL})
    (all_small,) = _all_gather([local_small], name="gather_small_grads")
    small_out = _reduce_adamw(all_small, _pack_small(small), _pack_small({n: mom[n] for n in SMALL}),
                              _pack_small({n: var[n] for n in SMALL}), name="adamw_replicated")
    small_out = [_unpack_small(a, small_shapes) for a in small_out]

    outs = [loss, dx[None]]
    for k in range(4):
        outs += [big_out[n][k] if n in BIG else small_out[k][n] for n in WEIGHTS]
    return tuple(outs)
```

```python
import math

import jax
import jax.numpy as jnp
from jax import lax
from jax.experimental import pallas as pl
from jax.experimental.pallas import tpu as pltpu

F32 = jnp.float32
BF16 = jnp.bfloat16
MESH_IDS = pl.DeviceIdType.MESH

N_DEV = 8
D_MODEL = 1024
DEPTH = 2
PLE_DIM = 256
SSM_WIDTH = 512
SSM_GROUP = 16
SSM_GROUPS = 32
SSM_STATE = 64
SSM_LANES = SSM_GROUPS * SSM_STATE
ATTN_WIDTH = 512
ATTN_HEADS = 8
HEAD_DIM = 64
RMS_EPS = 1e-6
ADAM_LR = 0.001
ADAM_B1 = 0.9
ADAM_B2 = 0.999
ADAM_EPS = 1e-08
ADAM_WD = 0.01
ADAM_STEP = 10

VMEM_LIMIT = 56 * 1024 * 1024
ATT_TQ = 512
ATT_TK = 128
SSM_BLOCK_GROUPS = 8
SSM_TIME_BLOCK = 512
SUBLANES = 8
PACK_COLS = 1024

BIG = ("w_in", "ssm_w_glu", "w_out", "w_ple_gate", "w_ple_proj")
ROW_SHARDED = ("w_out", "w_ple_gate")
SMALL = ("mix_norm_g", "ssm_a_re", "ssm_a_im", "ssm_log_dt", "ssm_b_re", "ssm_b_im", "ssm_c_re",
         "ssm_c_im", "ssm_d", "ssm_b_glu", "q_norm_g", "k_norm_g", "ple_norm_g")
WEIGHTS = ("mix_norm_g", "w_in", "ssm_a_re", "ssm_a_im", "ssm_log_dt", "ssm_b_re", "ssm_b_im",
           "ssm_c_re", "ssm_c_im", "ssm_d", "ssm_w_glu", "ssm_b_glu", "q_norm_g", "k_norm_g",
           "w_out", "ple_norm_g", "w_ple_gate", "w_ple_proj")
_NT = (((1,), (1,)), ((), ()))
_TN = (((0,), (0,)), ((), ()))


def _params(sem=None):
    return pltpu.CompilerParams(dimension_semantics=sem, vmem_limit_bytes=VMEM_LIMIT)


def _sigmoid(x):
    return 1.0 / (1.0 + jnp.exp(-x))


_GELU_K = math.sqrt(2.0 / math.pi)
_GELU_C = 0.044715


def _gelu(x):
    return 0.5 * x * (1.0 + jnp.tanh(_GELU_K * (x + _GELU_C * x * x * x)))


def _gelu_grad(x):
    th = jnp.tanh(_GELU_K * (x + _GELU_C * x * x * x))
    return 0.5 * (1.0 + th) + 0.5 * x * (1.0 - th * th) * _GELU_K * (1.0 + 3.0 * _GELU_C * x * x)


def _mm_call(a, b, add, a_spec, b_spec, o_spec, grid, out_shape, acc_shape, dims, name):
    n_steps = grid[2]
    out_dtype = out_shape.dtype

    def body(*refs):
        if add is None:
            a_ref, b_ref, o_ref, acc_ref = refs
        else:
            a_ref, b_ref, r_ref, o_ref, acc_ref = refs
        kk = pl.program_id(2)

        @pl.when(kk == 0)
        def _():
            acc_ref[...] = jnp.zeros_like(acc_ref)

        acc_ref[...] += lax.dot_general(a_ref[...].astype(BF16), b_ref[...].astype(BF16), dims,
                                        preferred_element_type=F32)

        @pl.when(kk == n_steps - 1)
        def _():
            res = acc_ref[...]
            if add is not None:
                res = res + r_ref[...].astype(F32)
            o_ref[...] = res.astype(out_dtype)

    ins, specs = [a, b], [a_spec, b_spec]
    if add is not None:
        ins.append(add)
        specs.append(o_spec)
    return pl.pallas_call(
        body, name=name, grid=grid, in_specs=specs, out_specs=o_spec, out_shape=out_shape,
        scratch_shapes=[pltpu.VMEM(acc_shape, F32)],
        compiler_params=_params(("parallel", "parallel", "arbitrary")),
    )(*ins)


def _mm(a, b, *, name, ta=False, tb=False, add=None, out_dtype=F32, tm=1024, tn=1024, tk=512):
    m, k = (a.shape[1], a.shape[0]) if ta else a.shape
    n = b.shape[0] if tb else b.shape[1]
    tm, tn, tk = min(tm, m), min(tn, n), min(tk, k)
    assert m % tm == 0 and n % tn == 0 and k % tk == 0, (name, a.shape, b.shape)
    dims = (((0 if ta else 1,), (1 if tb else 0,)), ((), ()))
    a_spec = (pl.BlockSpec((tk, tm), lambda i, j, kk: (kk, i)) if ta
              else pl.BlockSpec((tm, tk), lambda i, j, kk: (i, kk)))
    b_spec = (pl.BlockSpec((tn, tk), lambda i, j, kk: (j, kk)) if tb
              else pl.BlockSpec((tk, tn), lambda i, j, kk: (kk, j)))
    o_spec = pl.BlockSpec((tm, tn), lambda i, j, kk: (i, j))
    return _mm_call(a, b, add, a_spec, b_spec, o_spec, (m // tm, n // tn, k // tk),
                    jax.ShapeDtypeStruct((m, n), out_dtype), (tm, tn), dims, name)


def _rows_tile(rows, want=512):
    t = min(rows, want)
    assert rows % t == 0
    return t


def _rowwise(body, name, rows, ins, outs):
    widest = max(cols for _, cols, _ in ins)
    tm = _rows_tile(rows, 512 if widest > 128 else 4096)
    in_specs = []
    for _, cols, cb in ins:
        if cb is None:
            in_specs.append(pl.BlockSpec((1, cols), lambda i: (0, 0)))
        else:
            in_specs.append(pl.BlockSpec((tm, cols), lambda i, cb=cb: (i, cb)))
    out_specs, out_shapes = [], []
    for cols, dtype, is_acc in outs:
        if is_acc:
            out_specs.append(pl.BlockSpec((1, cols), lambda i: (0, 0)))
            out_shapes.append(jax.ShapeDtypeStruct((1, cols), dtype))
        else:
            out_specs.append(pl.BlockSpec((tm, cols), lambda i: (i, 0)))
            out_shapes.append(jax.ShapeDtypeStruct((rows, cols), dtype))
    any_acc = any(o[2] for o in outs)
    return pl.pallas_call(
        body, name=name, grid=(rows // tm,), in_specs=in_specs, out_specs=tuple(out_specs),
        out_shape=tuple(out_shapes),
        compiler_params=_params(("arbitrary",) if any_acc else ("parallel",)))(*[a for a, _, _ in ins])


def _accumulate(ref, value):
    @pl.when(pl.program_id(0) == 0)
    def _():
        ref[...] = jnp.zeros_like(ref)

    ref[...] += value


def _rms_fwd(h, g, *, name):
    rows, d = h.shape

    def body(h_ref, g_ref, o_ref):
        hv = h_ref[...]
        r = lax.rsqrt(jnp.mean(hv * hv, axis=-1, keepdims=True) + RMS_EPS)
        o_ref[...] = (hv * r * g_ref[...]).astype(BF16)

    return _rowwise(body, name, rows, [(h, d, 0), (g, d, None)], [(d, BF16, False)])[0]


def _rms_bwd(h, g, dhn, *, name, dres=None):
    rows, d = h.shape

    def body(*refs):
        if dres is None:
            h_ref, g_ref, dn_ref, dh_ref, dg_ref = refs
        else:
            h_ref, g_ref, dn_ref, dr_ref, dh_ref, dg_ref = refs
        hv = h_ref[...]
        dn = dn_ref[...].astype(F32)
        r = lax.rsqrt(jnp.mean(hv * hv, axis=-1, keepdims=True) + RMS_EPS)
        a = dn * g_ref[...]
        dot = jnp.mean(a * hv, axis=-1, keepdims=True)
        dh = r * a - hv * (r * r * r * dot)
        if dres is not None:
            dh = dh + dr_ref[...]
        dh_ref[...] = dh
        _accumulate(dg_ref, jnp.sum(dn * hv * r, axis=0, keepdims=True))

    ins = [(h, d, 0), (g, d, None), (dhn, d, 0)]
    if dres is not None:
        ins.append((dres, d, 0))
    return _rowwise(body, name, rows, ins, [(d, F32, False), (d, F32, True)])


def _gate_fwd(zz, bglu, proj, o):
    rows = zz.shape[0]
    w = SSM_WIDTH

    def body(zz_ref, b_ref, gs_ref, o_ref, ga_ref, y_ref):
        zz_v = zz_ref[...] + b_ref[...]
        val, gate = zz_v[:, :w], zz_v[:, w:]
        gs, ga = gs_ref[...], ga_ref[...]
        y_ref[:, :w] = (val * _sigmoid(gate) * (gs * _sigmoid(gs))).astype(BF16)
        y_ref[:, w:] = (o_ref[...] * (ga * _sigmoid(ga))).astype(BF16)

    return _rowwise(body, "gate_fwd", rows,
                    [(zz, 2 * w, 0), (bglu, 2 * w, None), (proj, w, 1), (o, w, 0), (proj, w, 5)],
                    [(2 * w, BF16, False)])[0]


def _gate_bwd(dyc, zz, bglu, proj, o):
    rows = zz.shape[0]
    w = SSM_WIDTH

    def body(dy_ref, zz_ref, b_ref, gs_ref, o_ref, ga_ref, dzz_ref, dgs_ref, do_ref, dga_ref, db_ref):
        zz_v = zz_ref[...] + b_ref[...]
        val, gate = zz_v[:, :w], zz_v[:, w:]
        gs, ga = gs_ref[...], ga_ref[...]
        dys, dya = dy_ref[:, :w], dy_ref[:, w:]
        sg, ss, sa = _sigmoid(gate), _sigmoid(gs), _sigmoid(ga)
        glu = val * sg
        dglu = dys * (gs * ss)
        dgs_ref[...] = dys * glu * (ss * (1.0 + gs * (1.0 - ss)))
        dval = dglu * sg
        dgate = dglu * val * sg * (1.0 - sg)
        dzz_ref[:, :w] = dval.astype(BF16)
        dzz_ref[:, w:] = dgate.astype(BF16)
        do_ref[...] = dya * (ga * sa)
        dga_ref[...] = dya * o_ref[...] * (sa * (1.0 + ga * (1.0 - sa)))
        _accumulate(db_ref, jnp.concatenate([jnp.sum(dval, axis=0, keepdims=True),
                                             jnp.sum(dgate, axis=0, keepdims=True)], axis=1))

    return _rowwise(body, "gate_bwd", rows,
                    [(dyc, 2 * w, 0), (zz, 2 * w, 0), (bglu, 2 * w, None), (proj, w, 1), (o, w, 0), (proj, w, 5)],
                    [(2 * w, BF16, False), (w, F32, False), (w, F32, False), (w, F32, False), (2 * w, F32, True)])


def _ple_out(h2, pgl, pp):
    rows, d = h2.shape

    def body(h_ref, g_ref, p_ref, o_ref):
        o_ref[...] = h_ref[...] + _sigmoid(g_ref[...]) * p_ref[...]

    return _rowwise(body, "ple_out", rows, [(h2, d, 0), (pgl, d, 0), (pp, d, 0)], [(d, F32, False)])[0]


def _ple_bwd(dh3, pgl, pp):
    rows, d = dh3.shape

    def body(dh_ref, g_ref, p_ref, dg_ref, dp_ref):
        sg = _sigmoid(g_ref[...])
        dh = dh_ref[...]
        dg_ref[...] = (dh * p_ref[...] * sg * (1.0 - sg)).astype(BF16)
        dp_ref[...] = (dh * sg).astype(BF16)

    return _rowwise(body, "ple_bwd", rows, [(dh3, d, 0), (pgl, d, 0), (pp, d, 0)],
                    [(d, BF16, False), (d, BF16, False)])


def _loss_head(h, target):
    rows, d = h.shape

    def body(h_ref, t_ref, dh_ref, l_ref):
        err = h_ref[...] - t_ref[...]
        dh_ref[...] = err * (1.0 / d)
        _accumulate(l_ref, jnp.sum(err * err, axis=0, keepdims=True) * (0.5 / d))

    return _rowwise(body, "loss_head", rows, [(h, d, 0), (target, d, 0)], [(d, F32, False), (d, F32, True)])


def _zoh(lr, li, dt):
    mag = jnp.exp(lr * dt)
    ab_re = mag * jnp.cos(li * dt)
    ab_im = mag * jnp.sin(li * dt)
    num_re = ab_re - 1.0
    den = lr * lr + li * li
    f_re = (num_re * lr + ab_im * li) / den
    f_im = (ab_im * lr - num_re * li) / den
    return ab_re, ab_im, f_re, f_im, den


def _ssm_param_fwd(a_re, a_im, log_dt, bt_re, bt_im):
    g, _, p = a_re.shape
    h = bt_re.shape[1]

    def body(lr_ref, li_ref, ldt_ref, br_ref, bi_ref, pr_ref, pi_ref, bbr_ref, bbi_ref):
        dt = jnp.exp(ldt_ref[...])
        ab_re, ab_im, f_re, f_im, _ = _zoh(lr_ref[...], li_ref[...], dt)
        cr, ci = ab_re, ab_im
        for k in range(SUBLANES):
            pr_ref[:, k:k + 1, :] = cr
            pi_ref[:, k:k + 1, :] = ci
            cr, ci = cr * ab_re - ci * ab_im, cr * ab_im + ci * ab_re
        br, bi = br_ref[...], bi_ref[...]
        bbr_ref[...] = f_re * br - f_im * bi
        bbi_ref[...] = f_re * bi + f_im * br

    pw = jax.ShapeDtypeStruct((g, SUBLANES, p), F32)
    large = jax.ShapeDtypeStruct((g, h, p), F32)
    return pl.pallas_call(body, name="ssm_param_fwd", out_shape=(pw, pw, large, large),
                          compiler_params=_params())(a_re, a_im, log_dt, bt_re, bt_im)


def _ssm_param_bwd(a_re, a_im, log_dt, bt_re, bt_im, gab_re, gab_im, gbb_re, gbb_im):
    g, _, p = a_re.shape
    h = bt_re.shape[1]

    def body(lr_ref, li_ref, ldt_ref, br_ref, bi_ref, gar_ref, gai_ref, gbr_ref, gbi_ref,
             dlr_ref, dli_ref, dldt_ref, dbr_ref, dbi_ref):
        lr, li = lr_ref[...], li_ref[...]
        dt = jnp.exp(ldt_ref[...])
        ab_re, ab_im, f_re, f_im, den = _zoh(lr, li, dt)
        br, bi = br_ref[...], bi_ref[...]
        gbr, gbi = gbr_ref[...], gbi_ref[...]
        dbr_ref[...] = f_re * gbr + f_im * gbi
        dbi_ref[...] = f_re * gbi - f_im * gbr
        gf_re = jnp.sum(br * gbr + bi * gbi, axis=1, keepdims=True)
        gf_im = jnp.sum(br * gbi - bi * gbr, axis=1, keepdims=True)
        il_re, il_im = lr / den, -li / den
        ga_re = gar_ref[...] + il_re * gf_re + il_im * gf_im
        ga_im = gai_ref[...] + il_re * gf_im - il_im * gf_re
        q_re = f_re * il_re - f_im * il_im
        q_im = f_re * il_im + f_im * il_re
        gl_re = -(q_re * gf_re + q_im * gf_im)
        gl_im = -(q_re * gf_im - q_im * gf_re)
        dlr_ref[...] = gl_re + dt * (ab_re * ga_re + ab_im * ga_im)
        dli_ref[...] = gl_im + dt * (ab_re * ga_im - ab_im * ga_re)
        la_re = lr * ab_re - li * ab_im
        la_im = lr * ab_im + li * ab_re
        dldt_ref[...] = jnp.sum((la_re * ga_re + la_im * ga_im) * dt, axis=2, keepdims=True)

    small = jax.ShapeDtypeStruct((g, 1, p), F32)
    one = jax.ShapeDtypeStruct((g, 1, 1), F32)
    large = jax.ShapeDtypeStruct((g, h, p), F32)
    return pl.pallas_call(body, name="ssm_param_bwd", out_shape=(small, small, one, large, large),
                          compiler_params=_params())(
                              a_re, a_im, log_dt, bt_re, bt_im, gab_re, gab_im, gbb_re, gbb_im)


def _block_diag(m):
    g, h, p = m.shape
    nb = g // SSM_BLOCK_GROUPS
    eye = jnp.eye(SSM_BLOCK_GROUPS, dtype=m.dtype)
    m4 = m.reshape(nb, SSM_BLOCK_GROUPS, h, p)
    return (m4[:, :, :, None, :] * eye[None, :, None, :, None]).reshape(nb, SSM_BLOCK_GROUPS * h, SSM_BLOCK_GROUPS * p)


def _block_diag_take(m, h, p):
    nb = m.shape[0]
    eye = jnp.eye(SSM_BLOCK_GROUPS, dtype=m.dtype)
    m5 = m.reshape(nb, SSM_BLOCK_GROUPS, h, SSM_BLOCK_GROUPS, p)
    return jnp.sum(m5 * eye[None, :, None, :, None], axis=3).reshape(nb * SSM_BLOCK_GROUPS, h, p)


def _scan_tile(br, bi, pw_re, pw_im, reverse):
    row = lax.broadcasted_iota(jnp.int32, br.shape, 0)
    xr, xi = br, bi
    for d in (1, 2, 4):
        ar, ai = pw_re[d - 1:d, :], pw_im[d - 1:d, :]
        if reverse:
            keep = row < SUBLANES - d
            sr = jnp.where(keep, pltpu.roll(xr, SUBLANES - d, 0), 0.0)
            si = jnp.where(keep, pltpu.roll(xi, SUBLANES - d, 0), 0.0)
        else:
            keep = row >= d
            sr = jnp.where(keep, pltpu.roll(xr, d, 0), 0.0)
            si = jnp.where(keep, pltpu.roll(xi, d, 0), 0.0)
        xr, xi = xr + ar * sr - ai * si, xi + ar * si + ai * sr
    return xr, xi


def _ssm_blocks(s):
    tt = min(SSM_TIME_BLOCK, s)
    assert s % tt == 0 and tt % SUBLANES == 0
    ch = SSM_BLOCK_GROUPS * SSM_GROUP
    st = SSM_BLOCK_GROUPS * SSM_STATE
    return tt, s // tt, SSM_WIDTH // ch, ch, st


def _ssm_fwd(proj, bc_re, bc_im, cc_re, cc_im, pw_re, pw_im, dvec):
    s = proj.shape[0]
    tt, nt, nb, ch, st = _ssm_blocks(s)
    n_tiles = tt // SUBLANES

    def body(u_ref, bre_ref, bim_ref, cre_ref, cim_ref, pr_ref, pi_ref, d_ref,
             xr_ref, xi_ref, y_ref, z_ref, carry_ref):
        @pl.when(pl.program_id(1) == 0)
        def _():
            carry_ref[...] = jnp.zeros_like(carry_ref)

        u = u_ref[...]
        ub = u.astype(BF16)
        xr_ref[...] = jnp.dot(ub, bre_ref[0], preferred_element_type=F32)
        xi_ref[...] = jnp.dot(ub, bim_ref[0], preferred_element_type=F32)
        pw_r, pw_i = pr_ref[...], pi_ref[...]

        def step(t, carry):
            cr, ci = carry
            rows = pl.ds(pl.multiple_of(t * SUBLANES, SUBLANES), SUBLANES)
            xr, xi = _scan_tile(xr_ref[rows, :], xi_ref[rows, :], pw_r, pw_i, False)
            xr, xi = xr + pw_r * cr - pw_i * ci, xi + pw_r * ci + pw_i * cr
            xr_ref[rows, :] = xr
            xi_ref[rows, :] = xi
            return (jnp.broadcast_to(xr[SUBLANES - 1:SUBLANES, :], (SUBLANES, st)),
                    jnp.broadcast_to(xi[SUBLANES - 1:SUBLANES, :], (SUBLANES, st)))

        cr, ci = lax.fori_loop(0, n_tiles, step, (carry_ref[0], carry_ref[1]), unroll=2)
        carry_ref[0] = cr
        carry_ref[1] = ci
        y = (lax.dot_general(xr_ref[...].astype(BF16), cre_ref[0], _NT, preferred_element_type=F32)
             + lax.dot_general(xi_ref[...].astype(BF16), cim_ref[0], _NT, preferred_element_type=F32)
             + d_ref[...] * u)
        y_ref[...] = y
        z_ref[...] = _gelu(y).astype(BF16)

    chan = pl.BlockSpec((tt, ch), lambda b, t: (t, b))
    state = pl.BlockSpec((tt, st), lambda b, t: (t, b))
    mat = pl.BlockSpec((1, ch, st), lambda b, t: (b, 0, 0))
    pw = pl.BlockSpec((SUBLANES, st), lambda b, t: (0, b))
    vec = pl.BlockSpec((1, ch), lambda b, t: (0, b))
    x_shape = jax.ShapeDtypeStruct((s, nb * st), F32)
    return pl.pallas_call(
        body, name="ssm_fwd", grid=(nb, nt), in_specs=[chan, mat, mat, mat, mat, pw, pw, vec],
        out_specs=(state, state, chan, chan),
        out_shape=(x_shape, x_shape, jax.ShapeDtypeStruct((s, nb * ch), F32), jax.ShapeDtypeStruct((s, nb * ch), BF16)),
        scratch_shapes=[pltpu.VMEM((2, SUBLANES, st), F32)],
        compiler_params=_params(("parallel", "arbitrary")))(proj, bc_re, bc_im, cc_re, cc_im, pw_re, pw_im, dvec)


def _ssm_bwd(dz, y, proj, x_re, x_im, bc_re, bc_im, cc_re, cc_im, pw_re, pw_im, dvec):
    s = proj.shape[0]
    tt, nt, nb, ch, st = _ssm_blocks(s)
    n_tiles = tt // SUBLANES

    def body(dz_ref, y_ref, u_ref, xr_ref, xi_ref, bre_ref, bim_ref, cre_ref, cim_ref, pr_ref, pi_ref, d_ref,
             du_ref, dbr_ref, dbi_ref, dcr_ref, dci_ref, gar_ref, gai_ref, dd_ref,
             lr_ref, li_ref, carry_ref, acc_ref):
        @pl.when(pl.program_id(1) == 0)
        def _():
            for ref in (carry_ref, acc_ref, dbr_ref, dbi_ref, dcr_ref, dci_ref, dd_ref):
                ref[...] = jnp.zeros_like(ref)

        u = u_ref[...]
        dy = dz_ref[...] * _gelu_grad(y_ref[...])
        ub, dyb = u.astype(BF16), dy.astype(BF16)
        lr_ref[...] = jnp.dot(dyb, cre_ref[0], preferred_element_type=F32)
        li_ref[...] = jnp.dot(dyb, cim_ref[0], preferred_element_type=F32)
        pw_r, pw_i = pr_ref[...], -pi_ref[...]
        pwc_r = jnp.concatenate([pw_r[SUBLANES - 1 - i:SUBLANES - i, :] for i in range(SUBLANES)], axis=0)
        pwc_i = jnp.concatenate([pw_i[SUBLANES - 1 - i:SUBLANES - i, :] for i in range(SUBLANES)], axis=0)
        row = lax.broadcasted_iota(jnp.int32, (SUBLANES, st), 0)

        def step(t, carry):
            cr, ci, acc_r, acc_i = carry
            rows = pl.ds(pl.multiple_of((n_tiles - 1 - t) * SUBLANES, SUBLANES), SUBLANES)
            lr, li = _scan_tile(lr_ref[rows, :], li_ref[rows, :], pw_r, pw_i, True)
            lr, li = lr + pwc_r * cr - pwc_i * ci, li + pwc_r * ci + pwc_i * cr
            lr_ref[rows, :] = lr
            li_ref[rows, :] = li
            nr = jnp.where(row < SUBLANES - 1, pltpu.roll(lr, SUBLANES - 1, 0), cr)
            ni = jnp.where(row < SUBLANES - 1, pltpu.roll(li, SUBLANES - 1, 0), ci)
            xr, xi = xr_ref[rows, :], xi_ref[rows, :]
            return (jnp.broadcast_to(lr[0:1, :], (SUBLANES, st)), jnp.broadcast_to(li[0:1, :], (SUBLANES, st)),
                    acc_r + xr * nr + xi * ni, acc_i + xr * ni - xi * nr)

        cr, ci, acc_r, acc_i = lax.fori_loop(
            0, n_tiles, step, (carry_ref[0], carry_ref[1], acc_ref[0], acc_ref[1]), unroll=2)
        carry_ref[0], carry_ref[1] = cr, ci
        acc_ref[0], acc_ref[1] = acc_r, acc_i
        lrb, lib = lr_ref[...].astype(BF16), li_ref[...].astype(BF16)
        du_ref[...] = (lax.dot_general(lrb, bre_ref[0], _NT, preferred_element_type=F32)
                       + lax.dot_general(lib, bim_ref[0], _NT, preferred_element_type=F32) + dy * d_ref[...])
        dbr_ref[0] += lax.dot_general(ub, lrb, _TN, preferred_element_type=F32)
        dbi_ref[0] += lax.dot_general(ub, lib, _TN, preferred_element_type=F32)
        dcr_ref[0] += lax.dot_general(dyb, xr_ref[...].astype(BF16), _TN, preferred_element_type=F32)
        dci_ref[0] += lax.dot_general(dyb, xi_ref[...].astype(BF16), _TN, preferred_element_type=F32)
        dd_ref[...] += jnp.sum(dy * u, axis=0, keepdims=True)

        @pl.when(pl.program_id(1) == nt - 1)
        def _():
            gar_ref[...] = jnp.sum(acc_r, axis=0, keepdims=True)
            gai_ref[...] = jnp.sum(acc_i, axis=0, keepdims=True)

    chan = pl.BlockSpec((tt, ch), lambda b, t: (nt - 1 - t, b))
    state = pl.BlockSpec((tt, st), lambda b, t: (nt - 1 - t, b))
    mat = pl.BlockSpec((1, ch, st), lambda b, t: (b, 0, 0))
    pw = pl.BlockSpec((SUBLANES, st), lambda b, t: (0, b))
    vec = pl.BlockSpec((1, ch), lambda b, t: (0, b))
    svec = pl.BlockSpec((1, st), lambda b, t: (0, b))
    mat_shape = jax.ShapeDtypeStruct((nb, ch, st), F32)
    return pl.pallas_call(
        body, name="ssm_bwd", grid=(nb, nt),
        in_specs=[chan, chan, chan, state, state, mat, mat, mat, mat, pw, pw, vec],
        out_specs=(chan, mat, mat, mat, mat, svec, svec, vec),
        out_shape=(jax.ShapeDtypeStruct((s, nb * ch), F32), mat_shape, mat_shape, mat_shape, mat_shape,
                   jax.ShapeDtypeStruct((1, nb * st), F32), jax.ShapeDtypeStruct((1, nb * st), F32),
                   jax.ShapeDtypeStruct((1, nb * ch), F32)),
        scratch_shapes=[pltpu.VMEM((tt, st), F32), pltpu.VMEM((tt, st), F32),
                        pltpu.VMEM((2, SUBLANES, st), F32), pltpu.VMEM((2, SUBLANES, st), F32)],
        compiler_params=_params(("parallel", "arbitrary")))(
            dz, y, proj, x_re, x_im, bc_re, bc_im, cc_re, cc_im, pw_re, pw_im, dvec)


def _split_dot(x, tri2):
    hi = x.astype(BF16)
    lo = (x - hi.astype(F32)).astype(BF16)
    return jnp.dot(jnp.concatenate([hi, lo], axis=1), tri2, preferred_element_type=F32)


def _tri_and_ones(tk, rel):
    r_i = lax.broadcasted_iota(jnp.int32, (2 * tk, 2 * tk), 0) % tk
    c_i = lax.broadcasted_iota(jnp.int32, (2 * tk, 2 * tk), 1)
    return jnp.where((c_i >= tk) | rel(r_i, c_i), 1.0, 0.0).astype(BF16)


def _attn_scores(qs, kb, own):
    z = lax.dot_general(qs, kb, _NT, preferred_element_type=F32)
    log_beta = jnp.minimum(z, 0.0) - jnp.log(1.0 + jnp.exp(-jnp.abs(z)))
    log_stay = log_beta - z
    if not own:
        return log_beta, log_stay, None
    mask = lax.broadcasted_iota(jnp.int32, z.shape, 1) < lax.broadcasted_iota(jnp.int32, z.shape, 0)
    return log_beta, jnp.where(mask, log_stay, 0.0), mask


def _attn_blocks(s):
    tq, tk = min(ATT_TQ, s), min(ATT_TK, s)
    assert s % tq == 0 and tq % tk == 0
    return tq, tk, tq // tk


def _first_head(shape, axis):
    return lax.broadcasted_iota(jnp.int32, shape, axis) < HEAD_DIM


def _pair_rms(x, g):
    first = _first_head(x.shape, 1)
    sq = x * x
    ms_a = jnp.sum(jnp.where(first, sq, 0.0), axis=1, keepdims=True) * (1.0 / HEAD_DIM)
    ms_b = jnp.sum(jnp.where(first, 0.0, sq), axis=1, keepdims=True) * (1.0 / HEAD_DIM)
    r = jnp.where(first, lax.rsqrt(ms_a + RMS_EPS), lax.rsqrt(ms_b + RMS_EPS))
    return x * r * g, r


def _pair_rms_bwd(x, g, r, dn):
    first = _first_head(x.shape, 1)
    a = dn * g
    ax = a * x
    dot_a = jnp.sum(jnp.where(first, ax, 0.0), axis=1, keepdims=True) * (1.0 / HEAD_DIM)
    dot_b = jnp.sum(jnp.where(first, 0.0, ax), axis=1, keepdims=True) * (1.0 / HEAD_DIM)
    dx = r * a - x * (r * r * r * jnp.where(first, dot_a, dot_b))
    return dx, jnp.sum(dn * x * r, axis=0, keepdims=True)


_Q_BLOCK0 = 2 * SSM_WIDTH // (2 * HEAD_DIM)
_K_BLOCK0 = _Q_BLOCK0 + ATTN_WIDTH // (2 * HEAD_DIM)
_V_BLOCK0 = _K_BLOCK0 + ATTN_WIDTH // (2 * HEAD_DIM)
HEAD_PAIRS = ATTN_HEADS // 2


def _fill_keys(k_ref, v_ref, gk_ref, kn_ref, vb_ref, rows_per_step):
    def fill(c, _):
        rows = pl.ds(pl.multiple_of(c * rows_per_step, rows_per_step), rows_per_step)
        kn, _ = _pair_rms(k_ref[rows, :], gk_ref[...])
        kn_ref[rows, :] = kn.astype(BF16)
        vb_ref[rows, :] = v_ref[rows, :].astype(BF16)
        return 0

    lax.fori_loop(0, k_ref.shape[0] // rows_per_step, fill, 0)


def _call_with_ride_along(body, ride, *, name, grid, in_specs, out_specs, out_shape, scratch_shapes, operands):
    if ride is None:
        return pl.pallas_call(body, name=name, grid=grid, in_specs=in_specs, out_specs=out_specs, out_shape=out_shape,
                              scratch_shapes=scratch_shapes,
                              compiler_params=_params(("parallel", "arbitrary")))(*operands)
    n, n_in, n_out = len(ride["bufs"]), len(in_specs), len(out_specs)
    last = grid[0] * grid[1] - 1

    def wrapped(*refs):
        ins, x_refs = refs[:n_in], refs[n_in:n_in + n]
        outs, out_refs = refs[n_in + n:n_in + n + n_out], refs[n_in + n + n_out:n_in + 2 * n + n_out]
        scratch, sems = refs[n_in + 2 * n + n_out:-3], refs[-3:]
        step = pl.program_id(0) * grid[1] + pl.program_id(1)

        def exchange():
            return _direct_exchange(x_refs, out_refs, *sems, scatter=ride["scatter"], row0s=ride["row0s"])

        @pl.when(step == 0)
        def _():
            exchange()[0]()

        body(*ins, *outs, *scratch)

        @pl.when(step == last)
        def _():
            exchange()[1]()

    any_spec = pl.BlockSpec(memory_space=pl.ANY)
    return pl.pallas_call(
        wrapped, name=name, grid=grid, in_specs=list(in_specs) + [any_spec] * n,
        out_specs=tuple(out_specs) + (any_spec,) * n, out_shape=tuple(out_shape) + tuple(ride["into_shapes"]),
        scratch_shapes=list(scratch_shapes) + _exchange_sems(n),
        compiler_params=_params(("arbitrary", "arbitrary")))(*operands, *ride["bufs"])


def _attn_fwd(proj, gq, gk, ride=None):
    s = proj.shape[0]
    tq, tk, r = _attn_blocks(s)
    pw = 2 * HEAD_DIM

    def body(q_ref, k_ref, v_ref, gq_ref, gk_ref, o_ref, b_ref, kn_ref, vb_ref):
        qi = pl.program_id(1)

        @pl.when(qi == 0)
        def _():
            _fill_keys(k_ref, v_ref, gk_ref, kn_ref, vb_ref, tq)

        qn, _ = _pair_rms(q_ref[...], gq_ref[...])
        qn = qn * (HEAD_DIM ** -0.5)
        first = _first_head((tq, pw), 1)
        tri = _tri_and_ones(tk, lambda row, col: row > col)
        qs = [jnp.where(first, qn, 0.0).astype(BF16), jnp.where(first, 0.0, qn).astype(BF16)]

        def tiles(kq, carry, own):
            o_acc, accs = carry[0], list(carry[1:])
            ks = pl.ds(pl.multiple_of(kq * tq, tq), tq)
            kb, vb = kn_ref[ks, :], vb_ref[ks, :]
            scores = [_attn_scores(q, kb, own) for q in qs]
            sums = [[_split_dot(sc[1][:, u * tk:(u + 1) * tk], tri) for u in range(r)] for sc in scores]
            ws = []
            for h in range(2):
                later = [None] * r
                for u in reversed(range(r)):
                    later[u] = accs[h] + sums[h][u][:, :tk]
                    accs[h] = accs[h] + sums[h][u][:, tk:]
                w = jnp.exp(scores[h][0] + jnp.concatenate(later, axis=1))
                if own:
                    w = jnp.where(scores[h][2], w, 0.0)
                ws.append(w.astype(BF16))
            po = [jnp.dot(w, vb, preferred_element_type=F32) for w in ws]
            return (o_acc + jnp.where(first, po[0], po[1]), *accs)

        zero = jnp.zeros((tq, tk), F32)
        carry = tiles(qi, (jnp.zeros((tq, pw), F32), zero, zero), True)
        o_acc, acc_a, acc_b = lax.fori_loop(0, qi, lambda it, c: tiles(qi - 1 - it, c, False), carry)
        o_ref[...] = o_acc
        b_ref[0] = acc_a[:, 0:1]
        b_ref[1] = acc_b[:, 0:1]

    def cols(block0):
        return pl.BlockSpec((s, pw), lambda hp, i: (0, block0 + hp))

    gain = pl.BlockSpec((1, pw), lambda hp, i: (0, 0))
    return _call_with_ride_along(
        body, ride, name="attn_fwd", grid=(HEAD_PAIRS, s // tq),
        in_specs=[pl.BlockSpec((tq, pw), lambda hp, i: (i, _Q_BLOCK0 + hp)), cols(_K_BLOCK0), cols(_V_BLOCK0),
                  gain, gain],
        out_specs=(pl.BlockSpec((tq, pw), lambda hp, i: (i, hp)), pl.BlockSpec((2, tq, 1), lambda hp, i: (hp, i, 0))),
        out_shape=(jax.ShapeDtypeStruct((s, ATTN_WIDTH), F32), jax.ShapeDtypeStruct((ATTN_HEADS, s, 1), F32)),
        scratch_shapes=[pltpu.VMEM((s, pw), BF16), pltpu.VMEM((s, pw), BF16)],
        operands=(proj, proj, proj, gq, gk))


def _attn_bwd(proj, gq, gk, bsum, do, ride=None):
    s = proj.shape[0]
    tq, tk, r = _attn_blocks(s)
    nq = s // tq
    pw = 2 * HEAD_DIM

    def body(q_ref, k_ref, v_ref, gq_ref, gk_ref, b_ref, do_ref,
             dq_ref, dk_ref, dv_ref, dgq_ref, dgk_ref, kn_ref, vb_ref, dkt_ref, dvt_ref):
        qi = pl.program_id(1)

        @pl.when(qi == 0)
        def _():
            _fill_keys(k_ref, v_ref, gk_ref, kn_ref, vb_ref, tq)
            for ref in (dkt_ref, dvt_ref, dgq_ref):
                ref[...] = jnp.zeros_like(ref)

        scale = HEAD_DIM ** -0.5
        q = q_ref[...]
        qn, rq = _pair_rms(q, gq_ref[...])
        qf = qn * scale
        qft = qf.T
        dof = do_ref[...]
        doft = dof.T
        first = _first_head((tq, pw), 1)
        first_t = _first_head((pw, tq), 0)
        tri_upto = _tri_and_ones(tk, lambda row, col: row <= col)
        tri_before = _tri_and_ones(tk, lambda row, col: row < col)
        heads = ((first, first_t), (~first, ~first_t))
        qs = [jnp.where(sel, qf, 0.0).astype(BF16) for sel, _ in heads]
        qst = [jnp.where(sel_t, qft, 0.0).astype(BF16) for _, sel_t in heads]
        dob = [jnp.where(sel, dof, 0.0).astype(BF16) for sel, _ in heads]
        dobt = [jnp.where(sel_t, doft, 0.0).astype(BF16) for _, sel_t in heads]

        def tiles(kq, carry, own):
            dq_acc = carry[0]
            rest, g_prefix = list(carry[1:3]), list(carry[3:5])
            ks = pl.ds(pl.multiple_of(kq * tq, tq), tq)
            kb, vb = kn_ref[ks, :], vb_ref[ks, :]
            scores = [_attn_scores(q, kb, own) for q in qs]
            dws = [lax.dot_general(d, vb, _NT, preferred_element_type=F32) for d in dob]
            sums = [[_split_dot(sc[1][:, u * tk:(u + 1) * tk], tri_upto) for u in range(r)] for sc in scores]
            ws, gs = [], []
            for h in range(2):
                later = [None] * r
                for u in range(r):
                    later[u] = rest[h] - sums[h][u][:, :tk]
                    rest[h] = rest[h] - sums[h][u][:, tk:]
                w = jnp.exp(scores[h][0] + jnp.concatenate(later, axis=1))
                if own:
                    w = jnp.where(scores[h][2], w, 0.0)
                ws.append(w)
                gs.append(w * dws[h])
            gsums = [[_split_dot(g[:, u * tk:(u + 1) * tk], tri_before) for u in range(r)] for g in gs]
            dzs = []
            for h in range(2):
                before = [None] * r
                for u in range(r):
                    before[u] = g_prefix[h] + gsums[h][u][:, :tk]
                    g_prefix[h] = g_prefix[h] + gsums[h][u][:, tk:]
                dz = gs[h] - jnp.exp(scores[h][0]) * (gs[h] + jnp.concatenate(before, axis=1))
                if own:
                    dz = jnp.where(scores[h][2], dz, 0.0)
                dzs.append(dz.astype(BF16))
            pq = [jnp.dot(dz, kb, preferred_element_type=F32) for dz in dzs]
            dkt_ref[:, ks] += (jnp.dot(qst[0], dzs[0], preferred_element_type=F32)
                               + jnp.dot(qst[1], dzs[1], preferred_element_type=F32))
            dvt_ref[:, ks] += (jnp.dot(dobt[0], ws[0].astype(BF16), preferred_element_type=F32)
                               + jnp.dot(dobt[1], ws[1].astype(BF16), preferred_element_type=F32))
            return (dq_acc + jnp.where(first, pq[0], pq[1]), *rest, *g_prefix)

        zero = jnp.zeros((tq, tk), F32)
        carry = (jnp.zeros((tq, pw), F32), jnp.broadcast_to(b_ref[0], (tq, tk)), jnp.broadcast_to(b_ref[1], (tq, tk)),
                 zero, zero)
        carry = lax.fori_loop(0, qi, lambda kq, c: tiles(kq, c, False), carry)
        dq, dgq = _pair_rms_bwd(q, gq_ref[...], rq, tiles(qi, carry, True)[0] * scale)
        dq_ref[...] = dq
        dgq_ref[0] += dgq

        @pl.when(qi == nq - 1)
        def _():
            def finish(c, dgk):
                rows = pl.ds(pl.multiple_of(c * tq, tq), tq)
                k = k_ref[rows, :]
                _, rk = _pair_rms(k, gk_ref[...])
                dk, dgk_c = _pair_rms_bwd(k, gk_ref[...], rk, dkt_ref[:, rows].T)
                dk_ref[rows, :] = dk
                dv_ref[rows, :] = dvt_ref[:, rows].T
                return dgk + dgk_c

            dgk_ref[0] = lax.fori_loop(0, nq, finish, jnp.zeros((1, pw), F32))

    def cols(block0):
        return pl.BlockSpec((s, pw), lambda hp, i: (0, block0 + hp))

    gain = pl.BlockSpec((1, pw), lambda hp, i: (0, 0))
    q_rows = pl.BlockSpec((tq, pw), lambda hp, i: (i, hp))
    all_rows = pl.BlockSpec((s, pw), lambda hp, i: (0, hp))
    pair_gain = pl.BlockSpec((1, 1, pw), lambda hp, i: (hp, 0, 0))
    wide = jax.ShapeDtypeStruct((s, ATTN_WIDTH), F32)
    gains = jax.ShapeDtypeStruct((HEAD_PAIRS, 1, pw), F32)
    return _call_with_ride_along(
        body, ride, name="attn_bwd", grid=(HEAD_PAIRS, nq),
        in_specs=[pl.BlockSpec((tq, pw), lambda hp, i: (i, _Q_BLOCK0 + hp)), cols(_K_BLOCK0), cols(_V_BLOCK0),
                  gain, gain, pl.BlockSpec((2, tq, 1), lambda hp, i: (hp, i, 0)), q_rows],
        out_specs=(q_rows, all_rows, all_rows, pair_gain, pair_gain),
        out_shape=(wide, wide, wide, gains, gains),
        scratch_shapes=[pltpu.VMEM((s, pw), BF16), pltpu.VMEM((s, pw), BF16),
                        pltpu.VMEM((pw, s), F32), pltpu.VMEM((pw, s), F32)],
        operands=(proj, proj, proj, gq, gk, bsum, do))


def _reduce_adamw(parts, w, m, v, *, name, row0s=(0,)):
    n, _, cols = parts.shape
    rows = w.shape[0]
    seg_rows = rows // len(row0s)
    tile_rows = 2 * SUBLANES
    tr = max(t for t in range(tile_rows, min(seg_rows, 512) + 1, tile_rows)
             if seg_rows % t == 0 and all(r0 % t == 0 for r0 in row0s))
    seg_tiles = seg_rows // tr

    def parts_tile(i):
        tile = row0s[0] // tr + i
        for l in range(1, len(row0s)):
            tile = jnp.where(i >= l * seg_tiles, row0s[l] // tr + i - l * seg_tiles, tile)
        return tile

    c1 = 1.0 - ADAM_B1 ** ADAM_STEP
    c2 = 1.0 - ADAM_B2 ** ADAM_STEP

    def body(p_ref, w_ref, m_ref, v_ref, g_ref, d_ref, nm_ref, nv_ref):
        g = p_ref[0].astype(F32)
        for i in range(1, n):
            g = g + p_ref[i].astype(F32)
        nm = ADAM_B1 * m_ref[...] + (1.0 - ADAM_B1) * g
        nv = ADAM_B2 * v_ref[...] + (1.0 - ADAM_B2) * (g * g)
        g_ref[...] = g
        nm_ref[...] = nm
        nv_ref[...] = nv
        d_ref[...] = -ADAM_LR * ((nm / c1) / (jnp.sqrt(nv / c2) + ADAM_EPS) + ADAM_WD * w_ref[...])

    row = pl.BlockSpec((tr, cols), lambda i: (i, 0))
    out = jax.ShapeDtypeStruct((rows, cols), F32)
    return pl.pallas_call(
        body, name=name, grid=(rows // tr,),
        in_specs=[pl.BlockSpec((n, tr, cols), lambda i: (0, parts_tile(i), 0)), row, row, row],
        out_specs=(row,) * 4, out_shape=(out,) * 4, compiler_params=_params(("parallel",)))(parts, w, m, v)


def _all_gather(shards, *, name):
    n_arr = len(shards)

    def body(*refs):
        x_refs, out_refs = refs[:n_arr], refs[n_arr:2 * n_arr]
        send_sems, recv_sems, local_sems = refs[2 * n_arr:]
        x, y, c = lax.axis_index("x"), lax.axis_index("y"), lax.axis_index("c")
        me, sibling = (x, y, c), (x, y, 1 - c)
        chips = [(1 - x, y), (x, 1 - y), (1 - x, 1 - y)]

        def slot(a, px, py, pc):
            return out_refs[a].at[4 * px + 2 * py + pc]

        def copy(a, k, block, to, src=None):
            return pltpu.make_async_remote_copy(
                src_ref=slot(a, *block) if src is None else src, dst_ref=slot(a, *block),
                send_sem=send_sems.at[a, k], recv_sem=recv_sems.at[a, k], device_id=to, device_id_type=MESH_IDS)

        arrays = range(n_arr)
        mine = [pltpu.make_async_copy(x_refs[a], slot(a, *me), local_sems.at[a]) for a in arrays]
        for cp in mine:
            cp.start()
        first = [copy(a, 0, me, sibling, src=x_refs[a]) for a in arrays]
        first += [copy(a, 1 + j, me, (*chip, c), src=x_refs[a]) for j, chip in enumerate(chips) for a in arrays]
        for cp in first:
            cp.start()
        passed = []
        for j, chip in enumerate(chips):
            for a in arrays:
                copy(a, 1 + j, (*chip, c), me).wait_recv()
                passed.append(copy(a, 4 + j, (*chip, c), sibling))
                passed[-1].start()
        for a in arrays:
            copy(a, 0, sibling, me).wait_recv()
        for j, chip in enumerate(chips):
            for a in arrays:
                copy(a, 4 + j, (*chip, 1 - c), me).wait_recv()
        for cp in first + passed:
            cp.wait_send()
        for cp in mine:
            cp.wait()

    any_spec = pl.BlockSpec(memory_space=pl.ANY)
    return pl.pallas_call(
        body, name=name, out_shape=tuple(jax.ShapeDtypeStruct((N_DEV,) + s.shape, s.dtype) for s in shards),
        in_specs=[any_spec] * n_arr, out_specs=(any_spec,) * n_arr,
        scratch_shapes=[pltpu.SemaphoreType.DMA((n_arr, 7)), pltpu.SemaphoreType.DMA((n_arr, 7)),
                        pltpu.SemaphoreType.DMA((n_arr,))],
    )(*shards)


def _direct_exchange(x_refs, out_refs, send_sems, recv_sems, local_sems, *, scatter, row0s):
    arrays = range(len(x_refs))
    x, y, c = lax.axis_index("x"), lax.axis_index("y"), lax.axis_index("c")
    me = 4 * x + 2 * y + c

    def src(a, slot):
        return x_refs[a].at[slot] if scatter else x_refs[a]

    def landing(a, sender):
        return out_refs[a].at[sender, pl.ds(row0s[a], x_refs[a].shape[-2]), :]

    def local(a):
        return pltpu.make_async_copy(src(a, me), landing(a, me), local_sems.at[a])

    def copy(a, k, arriving):
        px, py, pc = x ^ (k >> 2), y ^ ((k >> 1) & 1), c ^ (k & 1)
        peer = 4 * px + 2 * py + pc
        return pltpu.make_async_remote_copy(
            src_ref=src(a, peer), dst_ref=landing(a, peer if arriving else me),
            send_sem=send_sems.at[a, k - 1], recv_sem=recv_sems.at[a, k - 1],
            device_id=(px, py, pc), device_id_type=MESH_IDS)

    def start():
        for a in arrays:
            local(a).start()
        for k in range(1, N_DEV):
            for a in arrays:
                copy(a, k, False).start()

    def wait():
        for k in range(1, N_DEV):
            for a in arrays:
                copy(a, k, True).wait_recv()
        for k in range(1, N_DEV):
            for a in arrays:
                copy(a, k, False).wait_send()
        for a in arrays:
            local(a).wait()

    return start, wait


def _exchange_sems(n_arr):
    return [pltpu.SemaphoreType.DMA((n_arr, N_DEV - 1)), pltpu.SemaphoreType.DMA((n_arr, N_DEV - 1)),
            pltpu.SemaphoreType.DMA((n_arr,))]


def _all_to_all(bufs, into, row0s, *, name):
    n_arr = len(bufs)

    def body(*refs):
        x_refs, out_refs = refs[:n_arr], refs[2 * n_arr:3 * n_arr]
        start, wait = _direct_exchange(x_refs, out_refs, *refs[3 * n_arr:], scatter=True, row0s=row0s)
        start()
        wait()

    any_spec = pl.BlockSpec(memory_space=pl.ANY)
    return pl.pallas_call(
        body, name=name, out_shape=tuple(jax.ShapeDtypeStruct(b.shape, b.dtype) for b in into),
        in_specs=[any_spec] * (2 * n_arr), out_specs=(any_spec,) * n_arr,
        input_output_aliases={n_arr + a: a for a in range(n_arr)}, scratch_shapes=_exchange_sems(n_arr),
    )(*bufs, *into)


def _pack_small(vals):
    flat = jnp.concatenate([vals[n].reshape(-1) for n in SMALL])
    rows = -(-flat.shape[0] // (SUBLANES * PACK_COLS)) * SUBLANES
    return jnp.pad(flat, (0, rows * PACK_COLS - flat.shape[0])).reshape(rows, PACK_COLS)


def _unpack_small(flat, shapes):
    flat = flat.reshape(-1)
    out, off = {}, 0
    for n in SMALL:
        size = math.prod(shapes[n])
        out[n] = flat[off:off + size].reshape(shapes[n])
        off += size
    return out


def _pair_gain(g):
    return jnp.tile(g, 2)[None]


def _ssm_setup(sp):
    g, p = SSM_GROUPS, SSM_STATE
    a_re = sp["ssm_a_re"][:, None, :]
    a_im = sp["ssm_a_im"][:, None, :]
    log_dt = jnp.broadcast_to(sp["ssm_log_dt"][:, None, None], (g, 1, p))
    bt_re = sp["ssm_b_re"].transpose(0, 2, 1)
    bt_im = sp["ssm_b_im"].transpose(0, 2, 1)
    return a_re, a_im, log_dt, bt_re, bt_im


def _layer_fwd(h, p_l, w, sp, ride=None):
    hn = _rms_fwd(h, sp["mix_norm_g"][None], name="rms_mix")
    proj = _mm(hn, w["w_in"], name="mm_proj")

    pw_re, pw_im, bbt_re, bbt_im = _ssm_param_fwd(*_ssm_setup(sp))
    pw_re = pw_re.transpose(1, 0, 2).reshape(SUBLANES, SSM_LANES)
    pw_im = pw_im.transpose(1, 0, 2).reshape(SUBLANES, SSM_LANES)
    ssm_mats = (_block_diag(bbt_re).astype(BF16), _block_diag(bbt_im).astype(BF16),
                _block_diag(sp["ssm_c_re"]).astype(BF16), _block_diag(-sp["ssm_c_im"]).astype(BF16),
                pw_re, pw_im, sp["ssm_d"].reshape(1, SSM_WIDTH))
    x_re, x_im, y, z = _ssm_fwd(proj, *ssm_mats)
    zz = _mm(z, w["ssm_w_glu"], name="mm_glu")

    o, bsum, *landed = _attn_fwd(proj, _pair_gain(sp["q_norm_g"]), _pair_gain(sp["k_norm_g"]), ride)

    bglu = sp["ssm_b_glu"][None]
    ycat = _gate_fwd(zz, bglu, proj, o)
    h2 = _mm(ycat, w["w_out"], add=h, name="mm_out")
    hn2 = _rms_fwd(h2, sp["ple_norm_g"][None], name="rms_ple")
    pgl = _mm(hn2, w["w_ple_gate"], name="mm_ple_gate")
    pp = _mm(p_l, w["w_ple_proj"], name="mm_ple_proj")
    h3 = _ple_out(h2, pgl, pp)
    saved = dict(h=h, hn=hn, proj=proj, ssm_mats=ssm_mats, x_re=x_re, x_im=x_im, y=y, z=z,
                 zz=zz, bsum=bsum, o=o, ycat=ycat, h2=h2, hn2=hn2,
                 pgl=pgl, pp=pp, p_l=p_l, bglu=bglu)
    return h3, saved, landed


def _layer_bwd(dh3, sv, w, sp, ride=None):
    g, hh, p = SSM_GROUPS, SSM_GROUP, SSM_STATE
    grads = {}
    dgp, dpp = _ple_bwd(dh3, sv["pgl"], sv["pp"])
    grads["w_ple_gate"] = _mm(sv["hn2"], dgp, ta=True, out_dtype=BF16, name="mm_d_ple_gate")
    grads["w_ple_proj"] = _mm(sv["p_l"], dpp, ta=True, out_dtype=BF16, name="mm_d_ple_proj")
    dhn2 = _mm(dgp, w["w_ple_gate"], tb=True, name="mm_dhn2")
    dh2, dg2 = _rms_bwd(sv["h2"], sp["ple_norm_g"][None], dhn2, dres=dh3, name="rms_ple_bwd")
    grads["ple_norm_g"] = dg2[0]
    grads["w_out"] = _mm(sv["ycat"], dh2, ta=True, out_dtype=BF16, name="mm_d_out")
    dyc = _mm(dh2, w["w_out"], tb=True, name="mm_dycat")
    dzz, dgs, do, dga, dbglu = _gate_bwd(dyc, sv["zz"], sv["bglu"], sv["proj"], sv["o"])
    grads["ssm_b_glu"] = dbglu[0]

    dq, dk, dv, dgq, dgk, *landed = _attn_bwd(sv["proj"], _pair_gain(sp["q_norm_g"]), _pair_gain(sp["k_norm_g"]),
                                              sv["bsum"], do, ride)
    grads["q_norm_g"] = jnp.sum(dgq.reshape(ATTN_HEADS, HEAD_DIM), axis=0)
    grads["k_norm_g"] = jnp.sum(dgk.reshape(ATTN_HEADS, HEAD_DIM), axis=0)

    grads["ssm_w_glu"] = _mm(sv["z"], dzz, ta=True, out_dtype=BF16, name="mm_d_glu")
    dz = _mm(dzz, w["ssm_w_glu"], tb=True, name="mm_dz")
    du, dbc_re, dbc_im, dcc_re, dcc_im, gab_re, gab_im, dd = _ssm_bwd(
        dz, sv["y"], sv["proj"], sv["x_re"], sv["x_im"], *sv["ssm_mats"])
    grads["ssm_d"] = dd.reshape(g, hh)
    grads["ssm_c_re"] = _block_diag_take(dcc_re, hh, p)
    grads["ssm_c_im"] = -_block_diag_take(dcc_im, hh, p)
    da_re, da_im, dldt, dbt_re, dbt_im = _ssm_param_bwd(
        *_ssm_setup(sp), gab_re.reshape(g, 1, p), gab_im.reshape(g, 1, p),
        _block_diag_take(dbc_re, hh, p), _block_diag_take(dbc_im, hh, p))
    grads["ssm_a_re"], grads["ssm_a_im"] = da_re[:, 0, :], da_im[:, 0, :]
    grads["ssm_log_dt"] = dldt[:, 0, 0]
    grads["ssm_b_re"], grads["ssm_b_im"] = dbt_re.transpose(0, 2, 1), dbt_im.transpose(0, 2, 1)

    dproj = jnp.concatenate([du, dgs, dq, dk, dv, dga], axis=1)
    grads["w_in"] = _mm(sv["hn"], dproj, ta=True, out_dtype=BF16, name="mm_d_in")
    dhn = _mm(dproj, w["w_in"], tb=True, name="mm_dhn")
    dh, dg1 = _rms_bwd(sv["h"], sp["mix_norm_g"][None], dhn, dres=dh2, name="rms_mix_bwd")
    grads["mix_norm_g"] = dg1[0]
    return dh, grads, landed


def _local_step(x, p, target, big, small, fetch=None, ship=None):
    h = x
    big = list(big)
    saved = []
    for l in range(DEPTH):
        sp = {n: small[n][l] for n in SMALL}
        h, sv, landed = _layer_fwd(h, p[l], big[l], sp, fetch[0] if fetch and l == 0 else None)
        if landed:
            big[DEPTH - 1] = fetch[1](landed)
        saved.append(sv)
    dh, loss_parts = _loss_head(h, target)
    grads = [None] * DEPTH
    shipped = []
    for l in reversed(range(DEPTH)):
        sp = {n: small[n][l] for n in SMALL}
        ride = ship(grads[DEPTH - 1]) if ship and l == 0 else None
        dh, grads[l], landed = _layer_bwd(dh, saved[l], big[l], sp, ride)
        shipped = landed or shipped
    return jnp.sum(loss_parts), dh, grads, shipped


def kernel(x, p, mix_norm_g, w_in, ssm_a_re, ssm_a_im, ssm_log_dt, ssm_b_re, ssm_b_im, ssm_c_re, ssm_c_im, ssm_d, ssm_w_glu, ssm_b_glu, q_norm_g, k_norm_g, w_out, ple_norm_g, w_ple_gate, w_ple_proj, loss_target, m_mix_norm_g, m_w_in, m_ssm_a_re, m_ssm_a_im, m_ssm_log_dt, m_ssm_b_re, m_ssm_b_im, m_ssm_c_re, m_ssm_c_im, m_ssm_d, m_ssm_w_glu, m_ssm_b_glu, m_q_norm_g, m_k_norm_g, m_w_out, m_ple_norm_g, m_w_ple_gate, m_w_ple_proj, v_mix_norm_g, v_w_in, v_ssm_a_re, v_ssm_a_im, v_ssm_log_dt, v_ssm_b_re, v_ssm_b_im, v_ssm_c_re, v_ssm_c_im, v_ssm_d, v_ssm_w_glu, v_ssm_b_glu, v_q_norm_g, v_k_norm_g, v_w_out, v_ple_norm_g, v_w_ple_gate, v_w_ple_proj):
    given = dict(locals())
    wts = {n: given[n] for n in WEIGHTS}
    mom = {n: given["m_" + n] for n in WEIGHTS}
    var = {n: given["v_" + n] for n in WEIGHTS}

    def rows2d(a):
        return a.reshape(-1, a.shape[-1])

    assert DEPTH == 2
    groups = (("w_in",), ("ssm_w_glu", "w_ple_proj"), ("w_out", "w_ple_gate"))
    layer_rows = [sum(wts[n].shape[1] for n in names) for names in groups]
    widths = [wts[names[0]].shape[2] for names in groups]

    def shards_of(l):
        return [jnp.concatenate([wts[n][l].astype(BF16) for n in names], axis=0) for names in groups]

    def weights_of(gathered):
        full = {}
        for names, got in zip(groups, gathered):
            row0 = 0
            for n in names:
                r = wts[n].shape[1]
                blocks = got[:, row0:row0 + r, :]
                full[n] = blocks.reshape(N_DEV * r, -1) if n in ROW_SHARDED else blocks.transpose(1, 0, 2).reshape(r, -1)
                row0 += r
        return full

    def blocks_to_send(layer_grads):
        def blocks(n):
            g = layer_grads[n]
            if n in ROW_SHARDED:
                return g.reshape(N_DEV, -1, g.shape[1])
            return g.reshape(g.shape[0], N_DEV, -1).transpose(1, 0, 2)

        return [jnp.concatenate([blocks(n) for n in names], axis=1) for names in groups]

    fetch = (dict(bufs=shards_of(1), scatter=False, row0s=[0] * len(groups),
                  into_shapes=[jax.ShapeDtypeStruct((N_DEV, r, c), BF16) for r, c in zip(layer_rows, widths)]),
             weights_of)

    def ship(layer_grads):
        return dict(bufs=blocks_to_send(layer_grads), scatter=True, row0s=layer_rows,
                    into_shapes=[jax.ShapeDtypeStruct((N_DEV, DEPTH * r, c), BF16) for r, c in zip(layer_rows, widths)])

    first = weights_of(_all_gather(shards_of(0), name="gather_weights"))
    small = {n: wts[n] for n in SMALL}
    loss, dx, grads, parts = _local_step(x[0], p[:, 0], loss_target[0], [first, None], small, fetch, ship)
    loss = lax.psum(loss, ("x", "y", "c"))

    parts = _all_to_all(blocks_to_send(grads[0]), parts, [0] * len(groups), name="exchange_weight_grads")
    big_out = {}
    for names, part, rows in zip(groups, parts, layer_rows):
        row0 = 0
        for n in names:
            res = _reduce_adamw(part, rows2d(wts[n]), rows2d(mom[n]), rows2d(var[n]),
                                row0s=[l * rows + row0 for l in range(DEPTH)], name="adamw_" + n)
            big_out[n] = [a.reshape(wts[n].shape) for a in res]
            row0 += wts[n].shape[1]

    small_shapes = {n: wts[n].shape for n in SMALL}
    local_small = _pack_small({n: jnp.stack([grads[l][n] for l in range(DEPTH)]) for n in SMALL})
    (all_small,) = _all_gather([local_small], name="gather_small_grads")
    small_out = _reduce_adamw(all_small, _pack_small(small), _pack_small({n: mom[n] for n in SMALL}),
                              _pack_small({n: var[n] for n in SMALL}), name="adamw_replicated")
    small_out = [_unpack_small(a, small_shapes) for a in small_out]

    outs = [loss, dx[None]]
    for k in range(4):
        outs += [big_out[n][k] if n in BIG else small_out[k][n] for n in WEIGHTS]
    return tuple(outs)
```

```python
import math

import jax
import jax.numpy as jnp
from jax import lax
from jax.experimental import pallas as pl
from jax.experimental.pallas import tpu as pltpu

F32 = jnp.float32
BF16 = jnp.bfloat16
MESH_IDS = pl.DeviceIdType.MESH

N_DEV = 8
D_MODEL = 1024
DEPTH = 2
PLE_DIM = 256
SSM_WIDTH = 512
SSM_GROUP = 16
SSM_GROUPS = 32
SSM_STATE = 64
SSM_LANES = SSM_GROUPS * SSM_STATE
ATTN_WIDTH = 512
ATTN_HEADS = 8
HEAD_DIM = 64
RMS_EPS = 1e-6
ADAM_LR = 0.001
ADAM_B1 = 0.9
ADAM_B2 = 0.999
ADAM_EPS = 1e-08
ADAM_WD = 0.01
ADAM_STEP = 10

VMEM_LIMIT = 56 * 1024 * 1024
ATT_TQ = 512
ATT_TK = 128
SSM_BLOCK_GROUPS = 8
SSM_TIME_BLOCK = 512
SUBLANES = 8
PACK_COLS = 1024

BIG = ("w_in", "ssm_w_glu", "w_out", "w_ple_gate", "w_ple_proj")
ROW_SHARDED = ("w_out", "w_ple_gate")
SMALL = ("mix_norm_g", "ssm_a_re", "ssm_a_im", "ssm_log_dt", "ssm_b_re", "ssm_b_im", "ssm_c_re",
         "ssm_c_im", "ssm_d", "ssm_b_glu", "q_norm_g", "k_norm_g", "ple_norm_g")
WEIGHTS = ("mix_norm_g", "w_in", "ssm_a_re", "ssm_a_im", "ssm_log_dt", "ssm_b_re", "ssm_b_im",
           "ssm_c_re", "ssm_c_im", "ssm_d", "ssm_w_glu", "ssm_b_glu", "q_norm_g", "k_norm_g",
           "w_out", "ple_norm_g", "w_ple_gate", "w_ple_proj")
_NT = (((1,), (1,)), ((), ()))
_TN = (((0,), (0,)), ((), ()))


def _params(sem=None):
    return pltpu.CompilerParams(dimension_semantics=sem, vmem_limit_bytes=VMEM_LIMIT)


def _sigmoid(x):
    return 1.0 / (1.0 + jnp.exp(-x))


_GELU_K = math.sqrt(2.0 / math.pi)
_GELU_C = 0.044715


def _gelu(x):
    return 0.5 * x * (1.0 + jnp.tanh(_GELU_K * (x + _GELU_C * x * x * x)))


def _gelu_grad(x):
    th = jnp.tanh(_GELU_K * (x + _GELU_C * x * x * x))
    return 0.5 * (1.0 + th) + 0.5 * x * (1.0 - th * th) * _GELU_K * (1.0 + 3.0 * _GELU_C * x * x)


def _mm(a, b, *, name, ta=False, tb=False, add=None, out_dtype=F32, tm=1024, tn=1024, tk=512,
        finish=None, extras=(), outs=None):
    m, k = (a.shape[1], a.shape[0]) if ta else a.shape
    n = b.shape[0] if tb else b.shape[1]
    tm, tn, tk = min(tm, m), min(tn, n), min(tk, k)
    assert m % tm == 0 and n % tn == 0 and k % tk == 0, (name, a.shape, b.shape)
    n_steps = k // tk
    dims = (((0 if ta else 1,), (1 if tb else 0,)), ((), ()))
    a_spec = (pl.BlockSpec((tk, tm), lambda i, j, kk: (kk, i)) if ta
              else pl.BlockSpec((tm, tk), lambda i, j, kk: (i, kk)))
    b_spec = (pl.BlockSpec((tn, tk), lambda i, j, kk: (j, kk)) if tb
              else pl.BlockSpec((tk, tn), lambda i, j, kk: (kk, j)))
    o_spec = pl.BlockSpec((tm, tn), lambda i, j, kk: (i, j))
    ins, in_specs = [a, b], [a_spec, b_spec]
    if finish is None:
        if add is not None:
            ins.append(add)
            in_specs.append(o_spec)

        def finish(res, extra_refs, out_refs):
            if add is not None:
                res = res + extra_refs[0][...].astype(F32)
            out_refs[0][...] = res.astype(out_dtype)

        out_specs, out_shapes = [o_spec], [jax.ShapeDtypeStruct((m, n), out_dtype)]
    else:
        assert tn == n and add is None
        for arr, kind in extras:
            ins.append(arr)
            cols = arr.shape[1]
            in_specs.append(pl.BlockSpec((tm, cols), lambda i, j, kk: (i, 0)) if kind == "rows"
                            else pl.BlockSpec((1, cols), lambda i, j, kk: (0, 0)))
        out_specs = [pl.BlockSpec((tm, cols), lambda i, j, kk: (i, 0)) if kind == "rows"
                     else pl.BlockSpec((1, cols), lambda i, j, kk: (0, 0)) for cols, _, kind in outs]
        out_shapes = [jax.ShapeDtypeStruct((m if kind == "rows" else 1, cols), dtype) for cols, dtype, kind in outs]
    n_in = len(ins)

    def body(*refs):
        a_ref, b_ref, acc_ref = refs[0], refs[1], refs[-1]
        kk = pl.program_id(2)

        @pl.when(kk == 0)
        def _():
            acc_ref[...] = jnp.zeros_like(acc_ref)

        acc_ref[...] += lax.dot_general(a_ref[...].astype(BF16), b_ref[...].astype(BF16), dims,
                                        preferred_element_type=F32)

        @pl.when(kk == n_steps - 1)
        def _():
            finish(acc_ref[...], refs[2:n_in], refs[n_in:-1])

    sums = outs is not None and any(kind == "sum" for _, _, kind in outs)
    res = pl.pallas_call(
        body, name=name, grid=(m // tm, n // tn, n_steps), in_specs=in_specs, out_specs=tuple(out_specs),
        out_shape=tuple(out_shapes), scratch_shapes=[pltpu.VMEM((tm, tn), F32)],
        compiler_params=_params(("arbitrary",) * 3 if sums else ("parallel", "parallel", "arbitrary")),
    )(*ins)
    return res[0] if outs is None else res


def _rms(hv, g):
    r = lax.rsqrt(jnp.mean(hv * hv, axis=-1, keepdims=True) + RMS_EPS)
    return hv * r * g, r


def _rms_grad(hv, g, dn):
    _, r = _rms(hv, g)
    a = dn * g
    dot = jnp.mean(a * hv, axis=-1, keepdims=True)
    return r * a - hv * (r * r * r * dot), jnp.sum(dn * hv * r, axis=0, keepdims=True)


def _mm_residual_norm(a, b, h, g, *, name):
    def finish(res, extra_refs, out_refs):
        h_ref, g_ref = extra_refs
        h2 = res + h_ref[...]
        out_refs[0][...] = h2
        out_refs[1][...] = _rms(h2, g_ref[...])[0].astype(BF16)

    d = b.shape[1]
    return _mm(a, b, name=name, tm=512, finish=finish, extras=[(h, "rows"), (g, "vec")],
               outs=[(d, F32, "rows"), (d, BF16, "rows")])


def _mm_gated_add(a, b, h2, pp, *, name):
    def finish(res, extra_refs, out_refs):
        h_ref, p_ref = extra_refs
        out_refs[0][...] = res
        out_refs[1][...] = h_ref[...] + _sigmoid(res) * p_ref[...]

    d = b.shape[1]
    return _mm(a, b, name=name, tm=512, finish=finish, extras=[(h2, "rows"), (pp, "rows")],
               outs=[(d, F32, "rows"), (d, F32, "rows")])


def _mm_norm_grad(dy, w, h, g, dres, *, name):
    def finish(res, extra_refs, out_refs):
        h_ref, g_ref, r_ref = extra_refs
        dh, dg = _rms_grad(h_ref[...], g_ref[...], res)
        out_refs[0][...] = dh + r_ref[...]

        @pl.when(pl.program_id(0) == 0)
        def _():
            out_refs[1][...] = jnp.zeros_like(out_refs[1])

        out_refs[1][...] += dg

    d = w.shape[0]
    return _mm(dy, w, name=name, tb=True, tm=512, finish=finish, extras=[(h, "rows"), (g, "vec"), (dres, "rows")],
               outs=[(d, F32, "rows"), (d, F32, "sum")])


def _rows_tile(rows, want=512):
    t = min(rows, want)
    assert rows % t == 0
    return t


def _rowwise(body, name, rows, ins, outs):
    widest = max(cols for _, cols, _ in ins)
    tm = _rows_tile(rows, 512 if widest > 128 else 4096)
    in_specs = []
    for _, cols, cb in ins:
        if cb is None:
            in_specs.append(pl.BlockSpec((1, cols), lambda i: (0, 0)))
        else:
            in_specs.append(pl.BlockSpec((tm, cols), lambda i, cb=cb: (i, cb)))
    out_specs, out_shapes = [], []
    for cols, dtype, is_acc in outs:
        if is_acc:
            out_specs.append(pl.BlockSpec((1, cols), lambda i: (0, 0)))
            out_shapes.append(jax.ShapeDtypeStruct((1, cols), dtype))
        else:
            out_specs.append(pl.BlockSpec((tm, cols), lambda i: (i, 0)))
            out_shapes.append(jax.ShapeDtypeStruct((rows, cols), dtype))
    any_acc = any(o[2] for o in outs)
    return pl.pallas_call(
        body, name=name, grid=(rows // tm,), in_specs=in_specs, out_specs=tuple(out_specs),
        out_shape=tuple(out_shapes),
        compiler_params=_params(("arbitrary",) if any_acc else ("parallel",)))(*[a for a, _, _ in ins])


def _accumulate(ref, value):
    @pl.when(pl.program_id(0) == 0)
    def _():
        ref[...] = jnp.zeros_like(ref)

    ref[...] += value


def _rms_fwd(h, g, *, name):
    rows, d = h.shape

    def body(h_ref, g_ref, o_ref):
        o_ref[...] = _rms(h_ref[...], g_ref[...])[0].astype(BF16)

    return _rowwise(body, name, rows, [(h, d, 0), (g, d, None)], [(d, BF16, False)])[0]


def _gate_fwd(zz, bglu, proj, o):
    rows = zz.shape[0]
    w = SSM_WIDTH

    def body(zz_ref, b_ref, gs_ref, o_ref, ga_ref, y_ref):
        zz_v = zz_ref[...] + b_ref[...]
        val, gate = zz_v[:, :w], zz_v[:, w:]
        gs, ga = gs_ref[...], ga_ref[...]
        y_ref[:, :w] = (val * _sigmoid(gate) * (gs * _sigmoid(gs))).astype(BF16)
        y_ref[:, w:] = (o_ref[...] * (ga * _sigmoid(ga))).astype(BF16)

    return _rowwise(body, "gate_fwd", rows,
                    [(zz, 2 * w, 0), (bglu, 2 * w, None), (proj, w, 1), (o, w, 0), (proj, w, 5)],
                    [(2 * w, BF16, False)])[0]


def _gate_bwd(dyc, zz, bglu, proj, o):
    rows = zz.shape[0]
    w = SSM_WIDTH

    def body(dy_ref, zz_ref, b_ref, gs_ref, o_ref, ga_ref, dzz_ref, dgs_ref, do_ref, dga_ref, db_ref):
        zz_v = zz_ref[...] + b_ref[...]
        val, gate = zz_v[:, :w], zz_v[:, w:]
        gs, ga = gs_ref[...], ga_ref[...]
        dys, dya = dy_ref[:, :w], dy_ref[:, w:]
        sg, ss, sa = _sigmoid(gate), _sigmoid(gs), _sigmoid(ga)
        glu = val * sg
        dglu = dys * (gs * ss)
        dgs_ref[...] = dys * glu * (ss * (1.0 + gs * (1.0 - ss)))
        dval = dglu * sg
        dgate = dglu * val * sg * (1.0 - sg)
        dzz_ref[:, :w] = dval.astype(BF16)
        dzz_ref[:, w:] = dgate.astype(BF16)
        do_ref[...] = dya * (ga * sa)
        dga_ref[...] = dya * o_ref[...] * (sa * (1.0 + ga * (1.0 - sa)))
        _accumulate(db_ref, jnp.concatenate([jnp.sum(dval, axis=0, keepdims=True),
                                             jnp.sum(dgate, axis=0, keepdims=True)], axis=1))

    return _rowwise(body, "gate_bwd", rows,
                    [(dyc, 2 * w, 0), (zz, 2 * w, 0), (bglu, 2 * w, None), (proj, w, 1), (o, w, 0), (proj, w, 5)],
                    [(2 * w, BF16, False), (w, F32, False), (w, F32, False), (w, F32, False), (2 * w, F32, True)])


def _ple_bwd(dh3, pgl, pp):
    rows, d = dh3.shape

    def body(dh_ref, g_ref, p_ref, dg_ref, dp_ref):
        sg = _sigmoid(g_ref[...])
        dh = dh_ref[...]
        dg_ref[...] = (dh * p_ref[...] * sg * (1.0 - sg)).astype(BF16)
        dp_ref[...] = (dh * sg).astype(BF16)

    return _rowwise(body, "ple_bwd", rows, [(dh3, d, 0), (pgl, d, 0), (pp, d, 0)],
                    [(d, BF16, False), (d, BF16, False)])


def _loss_head(h, target):
    rows, d = h.shape

    def body(h_ref, t_ref, dh_ref, l_ref):
        err = h_ref[...] - t_ref[...]
        dh_ref[...] = err * (1.0 / d)
        _accumulate(l_ref, jnp.sum(err * err, axis=0, keepdims=True) * (0.5 / d))

    return _rowwise(body, "loss_head", rows, [(h, d, 0), (target, d, 0)], [(d, F32, False), (d, F32, True)])


def _zoh(lr, li, dt):
    mag = jnp.exp(lr * dt)
    ab_re = mag * jnp.cos(li * dt)
    ab_im = mag * jnp.sin(li * dt)
    num_re = ab_re - 1.0
    den = lr * lr + li * li
    f_re = (num_re * lr + ab_im * li) / den
    f_im = (ab_im * lr - num_re * li) / den
    return ab_re, ab_im, f_re, f_im, den


def _ssm_param_fwd(a_re, a_im, log_dt, bt_re, bt_im):
    g, _, p = a_re.shape
    h = bt_re.shape[1]

    def body(lr_ref, li_ref, ldt_ref, br_ref, bi_ref, pr_ref, pi_ref, bbr_ref, bbi_ref):
        dt = jnp.exp(ldt_ref[...])
        ab_re, ab_im, f_re, f_im, _ = _zoh(lr_ref[...], li_ref[...], dt)
        cr, ci = ab_re, ab_im
        for k in range(SUBLANES):
            pr_ref[:, k:k + 1, :] = cr
            pi_ref[:, k:k + 1, :] = ci
            cr, ci = cr * ab_re - ci * ab_im, cr * ab_im + ci * ab_re
        br, bi = br_ref[...], bi_ref[...]
        bbr_ref[...] = f_re * br - f_im * bi
        bbi_ref[...] = f_re * bi + f_im * br

    pw = jax.ShapeDtypeStruct((g, SUBLANES, p), F32)
    large = jax.ShapeDtypeStruct((g, h, p), F32)
    return pl.pallas_call(body, name="ssm_param_fwd", out_shape=(pw, pw, large, large),
                          compiler_params=_params())(a_re, a_im, log_dt, bt_re, bt_im)


def _ssm_param_bwd(a_re, a_im, log_dt, bt_re, bt_im, gab_re, gab_im, gbb_re, gbb_im):
    g, _, p = a_re.shape
    h = bt_re.shape[1]

    def body(lr_ref, li_ref, ldt_ref, br_ref, bi_ref, gar_ref, gai_ref, gbr_ref, gbi_ref,
             dlr_ref, dli_ref, dldt_ref, dbr_ref, dbi_ref):
        lr, li = lr_ref[...], li_ref[...]
        dt = jnp.exp(ldt_ref[...])
        ab_re, ab_im, f_re, f_im, den = _zoh(lr, li, dt)
        br, bi = br_ref[...], bi_ref[...]
        gbr, gbi = gbr_ref[...], gbi_ref[...]
        dbr_ref[...] = f_re * gbr + f_im * gbi
        dbi_ref[...] = f_re * gbi - f_im * gbr
        gf_re = jnp.sum(br * gbr + bi * gbi, axis=1, keepdims=True)
        gf_im = jnp.sum(br * gbi - bi * gbr, axis=1, keepdims=True)
        il_re, il_im = lr / den, -li / den
        ga_re = gar_ref[...] + il_re * gf_re + il_im * gf_im
        ga_im = gai_ref[...] + il_re * gf_im - il_im * gf_re
        q_re = f_re * il_re - f_im * il_im
        q_im = f_re * il_im + f_im * il_re
        gl_re = -(q_re * gf_re + q_im * gf_im)
        gl_im = -(q_re * gf_im - q_im * gf_re)
        dlr_ref[...] = gl_re + dt * (ab_re * ga_re + ab_im * ga_im)
        dli_ref[...] = gl_im + dt * (ab_re * ga_im - ab_im * ga_re)
        la_re = lr * ab_re - li * ab_im
        la_im = lr * ab_im + li * ab_re
        dldt_ref[...] = jnp.sum((la_re * ga_re + la_im * ga_im) * dt, axis=2, keepdims=True)

    small = jax.ShapeDtypeStruct((g, 1, p), F32)
    one = jax.ShapeDtypeStruct((g, 1, 1), F32)
    large = jax.ShapeDtypeStruct((g, h, p), F32)
    return pl.pallas_call(body, name="ssm_param_bwd", out_shape=(small, small, one, large, large),
                          compiler_params=_params())(
                              a_re, a_im, log_dt, bt_re, bt_im, gab_re, gab_im, gbb_re, gbb_im)


def _block_diag(m):
    g, h, p = m.shape
    nb = g // SSM_BLOCK_GROUPS
    eye = jnp.eye(SSM_BLOCK_GROUPS, dtype=m.dtype)
    m4 = m.reshape(nb, SSM_BLOCK_GROUPS, h, p)
    return (m4[:, :, :, None, :] * eye[None, :, None, :, None]).reshape(nb, SSM_BLOCK_GROUPS * h, SSM_BLOCK_GROUPS * p)


def _block_diag_take(m, h, p):
    nb = m.shape[0]
    eye = jnp.eye(SSM_BLOCK_GROUPS, dtype=m.dtype)
    m5 = m.reshape(nb, SSM_BLOCK_GROUPS, h, SSM_BLOCK_GROUPS, p)
    return jnp.sum(m5 * eye[None, :, None, :, None], axis=3).reshape(nb * SSM_BLOCK_GROUPS, h, p)


def _scan_tile(br, bi, pw_re, pw_im, reverse):
    row = lax.broadcasted_iota(jnp.int32, br.shape, 0)
    xr, xi = br, bi
    for d in (1, 2, 4):
        ar, ai = pw_re[d - 1:d, :], pw_im[d - 1:d, :]
        if reverse:
            keep = row < SUBLANES - d
            sr = jnp.where(keep, pltpu.roll(xr, SUBLANES - d, 0), 0.0)
            si = jnp.where(keep, pltpu.roll(xi, SUBLANES - d, 0), 0.0)
        else:
            keep = row >= d
            sr = jnp.where(keep, pltpu.roll(xr, d, 0), 0.0)
            si = jnp.where(keep, pltpu.roll(xi, d, 0), 0.0)
        xr, xi = xr + ar * sr - ai * si, xi + ar * si + ai * sr
    return xr, xi


def _ssm_blocks(s):
    tt = min(SSM_TIME_BLOCK, s)
    assert s % tt == 0 and tt % SUBLANES == 0
    ch = SSM_BLOCK_GROUPS * SSM_GROUP
    st = SSM_BLOCK_GROUPS * SSM_STATE
    return tt, s // tt, SSM_WIDTH // ch, ch, st


def _ssm_fwd(proj, bc_re, bc_im, cc_re, cc_im, pw_re, pw_im, dvec):
    s = proj.shape[0]
    tt, nt, nb, ch, st = _ssm_blocks(s)
    n_tiles = tt // SUBLANES

    def body(u_ref, bre_ref, bim_ref, cre_ref, cim_ref, pr_ref, pi_ref, d_ref,
             xr_ref, xi_ref, y_ref, z_ref, carry_ref):
        @pl.when(pl.program_id(1) == 0)
        def _():
            carry_ref[...] = jnp.zeros_like(carry_ref)

        u = u_ref[...]
        ub = u.astype(BF16)
        xr_ref[...] = jnp.dot(ub, bre_ref[0], preferred_element_type=F32)
        xi_ref[...] = jnp.dot(ub, bim_ref[0], preferred_element_type=F32)
        pw_r, pw_i = pr_ref[...], pi_ref[...]

        def step(t, carry):
            cr, ci = carry
            rows = pl.ds(pl.multiple_of(t * SUBLANES, SUBLANES), SUBLANES)
            xr, xi = _scan_tile(xr_ref[rows, :], xi_ref[rows, :], pw_r, pw_i, False)
            xr, xi = xr + pw_r * cr - pw_i * ci, xi + pw_r * ci + pw_i * cr
            xr_ref[rows, :] = xr
            xi_ref[rows, :] = xi
            return (jnp.broadcast_to(xr[SUBLANES - 1:SUBLANES, :], (SUBLANES, st)),
                    jnp.broadcast_to(xi[SUBLANES - 1:SUBLANES, :], (SUBLANES, st)))

        cr, ci = lax.fori_loop(0, n_tiles, step, (carry_ref[0], carry_ref[1]), unroll=2)
        carry_ref[0] = cr
        carry_ref[1] = ci
        y = (lax.dot_general(xr_ref[...].astype(BF16), cre_ref[0], _NT, preferred_element_type=F32)
             + lax.dot_general(xi_ref[...].astype(BF16), cim_ref[0], _NT, preferred_element_type=F32)
             + d_ref[...] * u)
        y_ref[...] = y
        z_ref[...] = _gelu(y).astype(BF16)

    chan = pl.BlockSpec((tt, ch), lambda b, t: (t, b))
    state = pl.BlockSpec((tt, st), lambda b, t: (t, b))
    mat = pl.BlockSpec((1, ch, st), lambda b, t: (b, 0, 0))
    pw = pl.BlockSpec((SUBLANES, st), lambda b, t: (0, b))
    vec = pl.BlockSpec((1, ch), lambda b, t: (0, b))
    x_shape = jax.ShapeDtypeStruct((s, nb * st), F32)
    return pl.pallas_call(
        body, name="ssm_fwd", grid=(nb, nt), in_specs=[chan, mat, mat, mat, mat, pw, pw, vec],
        out_specs=(state, state, chan, chan),
        out_shape=(x_shape, x_shape, jax.ShapeDtypeStruct((s, nb * ch), F32), jax.ShapeDtypeStruct((s, nb * ch), BF16)),
        scratch_shapes=[pltpu.VMEM((2, SUBLANES, st), F32)],
        compiler_params=_params(("parallel", "arbitrary")))(proj, bc_re, bc_im, cc_re, cc_im, pw_re, pw_im, dvec)


def _ssm_bwd(dz, y, proj, x_re, x_im, bc_re, bc_im, cc_re, cc_im, pw_re, pw_im, dvec):
    s = proj.shape[0]
    tt, nt, nb, ch, st = _ssm_blocks(s)
    n_tiles = tt // SUBLANES

    def body(dz_ref, y_ref, u_ref, xr_ref, xi_ref, bre_ref, bim_ref, cre_ref, cim_ref, pr_ref, pi_ref, d_ref,
             du_ref, dbr_ref, dbi_ref, dcr_ref, dci_ref, gar_ref, gai_ref, dd_ref,
             lr_ref, li_ref, carry_ref, acc_ref):
        @pl.when(pl.program_id(1) == 0)
        def _():
            for ref in (carry_ref, acc_ref, dbr_ref, dbi_ref, dcr_ref, dci_ref, dd_ref):
                ref[...] = jnp.zeros_like(ref)

        u = u_ref[...]
        dy = dz_ref[...] * _gelu_grad(y_ref[...])
        ub, dyb = u.astype(BF16), dy.astype(BF16)
        lr_ref[...] = jnp.dot(dyb, cre_ref[0], preferred_element_type=F32)
        li_ref[...] = jnp.dot(dyb, cim_ref[0], preferred_element_type=F32)
        pw_r, pw_i = pr_ref[...], -pi_ref[...]
        pwc_r = jnp.concatenate([pw_r[SUBLANES - 1 - i:SUBLANES - i, :] for i in range(SUBLANES)], axis=0)
        pwc_i = jnp.concatenate([pw_i[SUBLANES - 1 - i:SUBLANES - i, :] for i in range(SUBLANES)], axis=0)
        row = lax.broadcasted_iota(jnp.int32, (SUBLANES, st), 0)

        def step(t, carry):
            cr, ci, acc_r, acc_i = carry
            rows = pl.ds(pl.multiple_of((n_tiles - 1 - t) * SUBLANES, SUBLANES), SUBLANES)
            lr, li = _scan_tile(lr_ref[rows, :], li_ref[rows, :], pw_r, pw_i, True)
            lr, li = lr + pwc_r * cr - pwc_i * ci, li + pwc_r * ci + pwc_i * cr
            lr_ref[rows, :] = lr
            li_ref[rows, :] = li
            nr = jnp.where(row < SUBLANES - 1, pltpu.roll(lr, SUBLANES - 1, 0), cr)
            ni = jnp.where(row < SUBLANES - 1, pltpu.roll(li, SUBLANES - 1, 0), ci)
            xr, xi = xr_ref[rows, :], xi_ref[rows, :]
            return (jnp.broadcast_to(lr[0:1, :], (SUBLANES, st)), jnp.broadcast_to(li[0:1, :], (SUBLANES, st)),
                    acc_r + xr * nr + xi * ni, acc_i + xr * ni - xi * nr)

        cr, ci, acc_r, acc_i = lax.fori_loop(
            0, n_tiles, step, (carry_ref[0], carry_ref[1], acc_ref[0], acc_ref[1]), unroll=2)
        carry_ref[0], carry_ref[1] = cr, ci
        acc_ref[0], acc_ref[1] = acc_r, acc_i
        lrb, lib = lr_ref[...].astype(BF16), li_ref[...].astype(BF16)
        du_ref[...] = (lax.dot_general(lrb, bre_ref[0], _NT, preferred_element_type=F32)
                       + lax.dot_general(lib, bim_ref[0], _NT, preferred_element_type=F32) + dy * d_ref[...])
        dbr_ref[0] += lax.dot_general(ub, lrb, _TN, preferred_element_type=F32)
        dbi_ref[0] += lax.dot_general(ub, lib, _TN, preferred_element_type=F32)
        dcr_ref[0] += lax.dot_general(dyb, xr_ref[...].astype(BF16), _TN, preferred_element_type=F32)
        dci_ref[0] += lax.dot_general(dyb, xi_ref[...].astype(BF16), _TN, preferred_element_type=F32)
        dd_ref[...] += jnp.sum(dy * u, axis=0, keepdims=True)

        @pl.when(pl.program_id(1) == nt - 1)
        def _():
            gar_ref[...] = jnp.sum(acc_r, axis=0, keepdims=True)
            gai_ref[...] = jnp.sum(acc_i, axis=0, keepdims=True)

    chan = pl.BlockSpec((tt, ch), lambda b, t: (nt - 1 - t, b))
    state = pl.BlockSpec((tt, st), lambda b, t: (nt - 1 - t, b))
    mat = pl.BlockSpec((1, ch, st), lambda b, t: (b, 0, 0))
    pw = pl.BlockSpec((SUBLANES, st), lambda b, t: (0, b))
    vec = pl.BlockSpec((1, ch), lambda b, t: (0, b))
    svec = pl.BlockSpec((1, st), lambda b, t: (0, b))
    mat_shape = jax.ShapeDtypeStruct((nb, ch, st), F32)
    return pl.pallas_call(
        body, name="ssm_bwd", grid=(nb, nt),
        in_specs=[chan, chan, chan, state, state, mat, mat, mat, mat, pw, pw, vec],
        out_specs=(chan, mat, mat, mat, mat, svec, svec, vec),
        out_shape=(jax.ShapeDtypeStruct((s, nb * ch), F32), mat_shape, mat_shape, mat_shape, mat_shape,
                   jax.ShapeDtypeStruct((1, nb * st), F32), jax.ShapeDtypeStruct((1, nb * st), F32),
                   jax.ShapeDtypeStruct((1, nb * ch), F32)),
        scratch_shapes=[pltpu.VMEM((tt, st), F32), pltpu.VMEM((tt, st), F32),
                        pltpu.VMEM((2, SUBLANES, st), F32), pltpu.VMEM((2, SUBLANES, st), F32)],
        compiler_params=_params(("parallel", "arbitrary")))(
            dz, y, proj, x_re, x_im, bc_re, bc_im, cc_re, cc_im, pw_re, pw_im, dvec)


def _split_dot(x, tri2):
    hi = x.astype(BF16)
    lo = (x - hi.astype(F32)).astype(BF16)
    return jnp.dot(jnp.concatenate([hi, lo], axis=1), tri2, preferred_element_type=F32)


def _tri_and_ones(tk, rel):
    r_i = lax.broadcasted_iota(jnp.int32, (2 * tk, 2 * tk), 0) % tk
    c_i = lax.broadcasted_iota(jnp.int32, (2 * tk, 2 * tk), 1)
    return jnp.where((c_i >= tk) | rel(r_i, c_i), 1.0, 0.0).astype(BF16)


def _attn_scores(qs, kb, own):
    z = lax.dot_general(qs, kb, _NT, preferred_element_type=F32)
    log_beta = jnp.minimum(z, 0.0) - jnp.log(1.0 + jnp.exp(-jnp.abs(z)))
    log_stay = log_beta - z
    if not own:
        return log_beta, log_stay, None
    mask = lax.broadcasted_iota(jnp.int32, z.shape, 1) < lax.broadcasted_iota(jnp.int32, z.shape, 0)
    return log_beta, jnp.where(mask, log_stay, 0.0), mask


def _attn_blocks(s):
    tq, tk = min(ATT_TQ, s), min(ATT_TK, s)
    assert s % tq == 0 and tq % tk == 0
    return tq, tk, tq // tk


def _first_head(shape, axis):
    return lax.broadcasted_iota(jnp.int32, shape, axis) < HEAD_DIM


def _pair_rms(x, g):
    first = _first_head(x.shape, 1)
    sq = x * x
    ms_a = jnp.sum(jnp.where(first, sq, 0.0), axis=1, keepdims=True) * (1.0 / HEAD_DIM)
    ms_b = jnp.sum(jnp.where(first, 0.0, sq), axis=1, keepdims=True) * (1.0 / HEAD_DIM)
    r = jnp.where(first, lax.rsqrt(ms_a + RMS_EPS), lax.rsqrt(ms_b + RMS_EPS))
    return x * r * g, r


def _pair_rms_bwd(x, g, r, dn):
    first = _first_head(x.shape, 1)
    a = dn * g
    ax = a * x
    dot_a = jnp.sum(jnp.where(first, ax, 0.0), axis=1, keepdims=True) * (1.0 / HEAD_DIM)
    dot_b = jnp.sum(jnp.where(first, 0.0, ax), axis=1, keepdims=True) * (1.0 / HEAD_DIM)
    dx = r * a - x * (r * r * r * jnp.where(first, dot_a, dot_b))
    return dx, jnp.sum(dn * x * r, axis=0, keepdims=True)


_Q_BLOCK0 = 2 * SSM_WIDTH // (2 * HEAD_DIM)
_K_BLOCK0 = _Q_BLOCK0 + ATTN_WIDTH // (2 * HEAD_DIM)
_V_BLOCK0 = _K_BLOCK0 + ATTN_WIDTH // (2 * HEAD_DIM)
HEAD_PAIRS = ATTN_HEADS // 2


def _fill_keys(k_ref, v_ref, gk_ref, kn_ref, vb_ref, rows_per_step):
    def fill(c, _):
        rows = pl.ds(pl.multiple_of(c * rows_per_step, rows_per_step), rows_per_step)
        kn, _ = _pair_rms(k_ref[rows, :], gk_ref[...])
        kn_ref[rows, :] = kn.astype(BF16)
        vb_ref[rows, :] = v_ref[rows, :].astype(BF16)
        return 0

    lax.fori_loop(0, k_ref.shape[0] // rows_per_step, fill, 0)


def _call_with_ride_along(body, ride, *, name, grid, in_specs, out_specs, out_shape, scratch_shapes, operands):
    if ride is None:
        return pl.pallas_call(body, name=name, grid=grid, in_specs=in_specs, out_specs=out_specs, out_shape=out_shape,
                              scratch_shapes=scratch_shapes,
                              compiler_params=_params(("parallel", "arbitrary")))(*operands)
    n, n_in, n_out = len(ride["bufs"]), len(in_specs), len(out_specs)
    last = grid[0] * grid[1] - 1

    def wrapped(*refs):
        ins, x_refs = refs[:n_in], refs[n_in:n_in + n]
        outs, out_refs = refs[n_in + n:n_in + n + n_out], refs[n_in + n + n_out:n_in + 2 * n + n_out]
        scratch, sems = refs[n_in + 2 * n + n_out:-3], refs[-3:]
        step = pl.program_id(0) * grid[1] + pl.program_id(1)

        def exchange():
            return _direct_exchange(x_refs, out_refs, *sems, scatter=ride["scatter"], row0s=ride["row0s"])

        @pl.when(step == 0)
        def _():
            exchange()[0]()

        body(*ins, *outs, *scratch)

        @pl.when(step == last)
        def _():
            exchange()[1]()

    any_spec = pl.BlockSpec(memory_space=pl.ANY)
    return pl.pallas_call(
        wrapped, name=name, grid=grid, in_specs=list(in_specs) + [any_spec] * n,
        out_specs=tuple(out_specs) + (any_spec,) * n, out_shape=tuple(out_shape) + tuple(ride["into_shapes"]),
        scratch_shapes=list(scratch_shapes) + _exchange_sems(n),
        compiler_params=_params(("arbitrary", "arbitrary")))(*operands, *ride["bufs"])


def _attn_fwd(proj, gq, gk, ride=None):
    s = proj.shape[0]
    tq, tk, r = _attn_blocks(s)
    pw = 2 * HEAD_DIM

    def body(q_ref, k_ref, v_ref, gq_ref, gk_ref, o_ref, b_ref, kn_ref, vb_ref):
        qi = pl.program_id(1)

        @pl.when(qi == 0)
        def _():
            _fill_keys(k_ref, v_ref, gk_ref, kn_ref, vb_ref, tq)

        qn, _ = _pair_rms(q_ref[...], gq_ref[...])
        qn = qn * (HEAD_DIM ** -0.5)
        first = _first_head((tq, pw), 1)
        tri = _tri_and_ones(tk, lambda row, col: row > col)
        qs = [jnp.where(first, qn, 0.0).astype(BF16), jnp.where(first, 0.0, qn).astype(BF16)]

        def tiles(kq, carry, own):
            o_acc, accs = carry[0], list(carry[1:])
            ks = pl.ds(pl.multiple_of(kq * tq, tq), tq)
            kb, vb = kn_ref[ks, :], vb_ref[ks, :]
            scores = [_attn_scores(q, kb, own) for q in qs]
            sums = [[_split_dot(sc[1][:, u * tk:(u + 1) * tk], tri) for u in range(r)] for sc in scores]
            ws = []
            for h in range(2):
                later = [None] * r
                for u in reversed(range(r)):
                    later[u] = accs[h] + sums[h][u][:, :tk]
                    accs[h] = accs[h] + sums[h][u][:, tk:]
                w = jnp.exp(scores[h][0] + jnp.concatenate(later, axis=1))
                if own:
                    w = jnp.where(scores[h][2], w, 0.0)
                ws.append(w.astype(BF16))
            po = [jnp.dot(w, vb, preferred_element_type=F32) for w in ws]
            return (o_acc + jnp.where(first, po[0], po[1]), *accs)

        zero = jnp.zeros((tq, tk), F32)
        carry = tiles(qi, (jnp.zeros((tq, pw), F32), zero, zero), True)
        o_acc, acc_a, acc_b = lax.fori_loop(0, qi, lambda it, c: tiles(qi - 1 - it, c, False), carry)
        o_ref[...] = o_acc
        b_ref[0] = acc_a[:, 0:1]
        b_ref[1] = acc_b[:, 0:1]

    def cols(block0):
        return pl.BlockSpec((s, pw), lambda hp, i: (0, block0 + hp))

    gain = pl.BlockSpec((1, pw), lambda hp, i: (0, 0))
    return _call_with_ride_along(
        body, ride, name="attn_fwd", grid=(HEAD_PAIRS, s // tq),
        in_specs=[pl.BlockSpec((tq, pw), lambda hp, i: (i, _Q_BLOCK0 + hp)), cols(_K_BLOCK0), cols(_V_BLOCK0),
                  gain, gain],
        out_specs=(pl.BlockSpec((tq, pw), lambda hp, i: (i, hp)), pl.BlockSpec((2, tq, 1), lambda hp, i: (hp, i, 0))),
        out_shape=(jax.ShapeDtypeStruct((s, ATTN_WIDTH), F32), jax.ShapeDtypeStruct((ATTN_HEADS, s, 1), F32)),
        scratch_shapes=[pltpu.VMEM((s, pw), BF16), pltpu.VMEM((s, pw), BF16)],
        operands=(proj, proj, proj, gq, gk))


def _attn_bwd(proj, gq, gk, bsum, do, ride=None):
    s = proj.shape[0]
    tq, tk, r = _attn_blocks(s)
    nq = s // tq
    pw = 2 * HEAD_DIM

    def body(q_ref, k_ref, v_ref, gq_ref, gk_ref, b_ref, do_ref,
             dq_ref, dk_ref, dv_ref, dgq_ref, dgk_ref, kn_ref, vb_ref, dkt_ref, dvt_ref):
        qi = pl.program_id(1)

        @pl.when(qi == 0)
        def _():
            _fill_keys(k_ref, v_ref, gk_ref, kn_ref, vb_ref, tq)
            for ref in (dkt_ref, dvt_ref, dgq_ref):
                ref[...] = jnp.zeros_like(ref)

        scale = HEAD_DIM ** -0.5
        q = q_ref[...]
        qn, rq = _pair_rms(q, gq_ref[...])
        qf = qn * scale
        qft = qf.T
        dof = do_ref[...]
        doft = dof.T
        first = _first_head((tq, pw), 1)
        first_t = _first_head((pw, tq), 0)
        tri_upto = _tri_and_ones(tk, lambda row, col: row <= col)
        tri_before = _tri_and_ones(tk, lambda row, col: row < col)
        heads = ((first, first_t), (~first, ~first_t))
        qs = [jnp.where(sel, qf, 0.0).astype(BF16) for sel, _ in heads]
        qst = [jnp.where(sel_t, qft, 0.0).astype(BF16) for _, sel_t in heads]
        dob = [jnp.where(sel, dof, 0.0).astype(BF16) for sel, _ in heads]
        dobt = [jnp.where(sel_t, doft, 0.0).astype(BF16) for _, sel_t in heads]

        def tiles(kq, carry, own):
            dq_acc = carry[0]
            rest, g_prefix = list(carry[1:3]), list(carry[3:5])
            ks = pl.ds(pl.multiple_of(kq * tq, tq), tq)
            kb, vb = kn_ref[ks, :], vb_ref[ks, :]
            scores = [_attn_scores(q, kb, own) for q in qs]
            dws = [lax.dot_general(d, vb, _NT, preferred_element_type=F32) for d in dob]
            sums = [[_split_dot(sc[1][:, u * tk:(u + 1) * tk], tri_upto) for u in range(r)] for sc in scores]
            ws, gs = [], []
            for h in range(2):
                later = [None] * r
                for u in range(r):
                    later[u] = rest[h] - sums[h][u][:, :tk]
                    rest[h] = rest[h] - sums[h][u][:, tk:]
                w = jnp.exp(scores[h][0] + jnp.concatenate(later, axis=1))
                if own:
                    w = jnp.where(scores[h][2], w, 0.0)
                ws.append(w)
                gs.append(w * dws[h])
            gsums = [[_split_dot(g[:, u * tk:(u + 1) * tk], tri_before) for u in range(r)] for g in gs]
            dzs = []
            for h in range(2):
                before = [None] * r
                for u in range(r):
                    before[u] = g_prefix[h] + gsums[h][u][:, :tk]
                    g_prefix[h] = g_prefix[h] + gsums[h][u][:, tk:]
                dz = gs[h] - jnp.exp(scores[h][0]) * (gs[h] + jnp.concatenate(before, axis=1))
                if own:
                    dz = jnp.where(scores[h][2], dz, 0.0)
                dzs.append(dz.astype(BF16))
            pq = [jnp.dot(dz, kb, preferred_element_type=F32) for dz in dzs]
            dkt_ref[:, ks] += (jnp.dot(qst[0], dzs[0], preferred_element_type=F32)
                               + jnp.dot(qst[1], dzs[1], preferred_element_type=F32))
            dvt_ref[:, ks] += (jnp.dot(dobt[0], ws[0].astype(BF16), preferred_element_type=F32)
                               + jnp.dot(dobt[1], ws[1].astype(BF16), preferred_element_type=F32))
            return (dq_acc + jnp.where(first, pq[0], pq[1]), *rest, *g_prefix)

        zero = jnp.zeros((tq, tk), F32)
        carry = (jnp.zeros((tq, pw), F32), jnp.broadcast_to(b_ref[0], (tq, tk)), jnp.broadcast_to(b_ref[1], (tq, tk)),
                 zero, zero)
        carry = lax.fori_loop(0, qi, lambda kq, c: tiles(kq, c, False), carry)
        dq, dgq = _pair_rms_bwd(q, gq_ref[...], rq, tiles(qi, carry, True)[0] * scale)
        dq_ref[...] = dq
        dgq_ref[0] += dgq

        @pl.when(qi == nq - 1)
        def _():
            def finish(c, dgk):
                rows = pl.ds(pl.multiple_of(c * tq, tq), tq)
                k = k_ref[rows, :]
                _, rk = _pair_rms(k, gk_ref[...])
                dk, dgk_c = _pair_rms_bwd(k, gk_ref[...], rk, dkt_ref[:, rows].T)
                dk_ref[rows, :] = dk
                dv_ref[rows, :] = dvt_ref[:, rows].T
                return dgk + dgk_c

            dgk_ref[0] = lax.fori_loop(0, nq, finish, jnp.zeros((1, pw), F32))

    def cols(block0):
        return pl.BlockSpec((s, pw), lambda hp, i: (0, block0 + hp))

    gain = pl.BlockSpec((1, pw), lambda hp, i: (0, 0))
    q_rows = pl.BlockSpec((tq, pw), lambda hp, i: (i, hp))
    all_rows = pl.BlockSpec((s, pw), lambda hp, i: (0, hp))
    pair_gain = pl.BlockSpec((1, 1, pw), lambda hp, i: (hp, 0, 0))
    wide = jax.ShapeDtypeStruct((s, ATTN_WIDTH), F32)
    gains = jax.ShapeDtypeStruct((HEAD_PAIRS, 1, pw), F32)
    return _call_with_ride_along(
        body, ride, name="attn_bwd", grid=(HEAD_PAIRS, nq),
        in_specs=[pl.BlockSpec((tq, pw), lambda hp, i: (i, _Q_BLOCK0 + hp)), cols(_K_BLOCK0), cols(_V_BLOCK0),
                  gain, gain, pl.BlockSpec((2, tq, 1), lambda hp, i: (hp, i, 0)), q_rows],
        out_specs=(q_rows, all_rows, all_rows, pair_gain, pair_gain),
        out_shape=(wide, wide, wide, gains, gains),
        scratch_shapes=[pltpu.VMEM((s, pw), BF16), pltpu.VMEM((s, pw), BF16),
                        pltpu.VMEM((pw, s), F32), pltpu.VMEM((pw, s), F32)],
        operands=(proj, proj, proj, gq, gk, bsum, do))


def _reduce_adamw(parts, w, m, v, *, name, row0s=(0,)):
    n, _, cols = parts.shape
    rows = w.shape[0]
    seg_rows = rows // len(row0s)
    tile_rows = 2 * SUBLANES
    tr = max(t for t in range(tile_rows, min(seg_rows, 512) + 1, tile_rows)
             if seg_rows % t == 0 and all(r0 % t == 0 for r0 in row0s))
    seg_tiles = seg_rows // tr

    def parts_tile(i):
        tile = row0s[0] // tr + i
        for l in range(1, len(row0s)):
            tile = jnp.where(i >= l * seg_tiles, row0s[l] // tr + i - l * seg_tiles, tile)
        return tile

    c1 = 1.0 - ADAM_B1 ** ADAM_STEP
    c2 = 1.0 - ADAM_B2 ** ADAM_STEP

    def body(p_ref, w_ref, m_ref, v_ref, g_ref, d_ref, nm_ref, nv_ref):
        g = p_ref[0].astype(F32)
        for i in range(1, n):
            g = g + p_ref[i].astype(F32)
        nm = ADAM_B1 * m_ref[...] + (1.0 - ADAM_B1) * g
        nv = ADAM_B2 * v_ref[...] + (1.0 - ADAM_B2) * (g * g)
        g_ref[...] = g
        nm_ref[...] = nm
        nv_ref[...] = nv
        d_ref[...] = -ADAM_LR * ((nm / c1) / (jnp.sqrt(nv / c2) + ADAM_EPS) + ADAM_WD * w_ref[...])

    row = pl.BlockSpec((tr, cols), lambda i: (i, 0))
    out = jax.ShapeDtypeStruct((rows, cols), F32)
    return pl.pallas_call(
        body, name=name, grid=(rows // tr,),
        in_specs=[pl.BlockSpec((n, tr, cols), lambda i: (0, parts_tile(i), 0)), row, row, row],
        out_specs=(row,) * 4, out_shape=(out,) * 4, compiler_params=_params(("parallel",)))(parts, w, m, v)


def _all_gather(shards, *, name):
    n_arr = len(shards)

    def body(*refs):
        x_refs, out_refs = refs[:n_arr], refs[n_arr:2 * n_arr]
        send_sems, recv_sems, local_sems = refs[2 * n_arr:]
        x, y, c = lax.axis_index("x"), lax.axis_index("y"), lax.axis_index("c")
        me, sibling = (x, y, c), (x, y, 1 - c)
        chips = [(1 - x, y), (x, 1 - y), (1 - x, 1 - y)]

        def slot(a, px, py, pc):
            return out_refs[a].at[4 * px + 2 * py + pc]

        def copy(a, k, block, to, src=None):
            return pltpu.make_async_remote_copy(
                src_ref=slot(a, *block) if src is None else src, dst_ref=slot(a, *block),
                send_sem=send_sems.at[a, k], recv_sem=recv_sems.at[a, k], device_id=to, device_id_type=MESH_IDS)

        arrays = range(n_arr)
        mine = [pltpu.make_async_copy(x_refs[a], slot(a, *me), local_sems.at[a]) for a in arrays]
        for cp in mine:
            cp.start()
        first = [copy(a, 0, me, sibling, src=x_refs[a]) for a in arrays]
        first += [copy(a, 1 + j, me, (*chip, c), src=x_refs[a]) for j, chip in enumerate(chips) for a in arrays]
        for cp in first:
            cp.start()
        passed = []
        for j, chip in enumerate(chips):
            for a in arrays:
                copy(a, 1 + j, (*chip, c), me).wait_recv()
                passed.append(copy(a, 4 + j, (*chip, c), sibling))
                passed[-1].start()
        for a in arrays:
            copy(a, 0, sibling, me).wait_recv()
        for j, chip in enumerate(chips):
            for a in arrays:
                copy(a, 4 + j, (*chip, 1 - c), me).wait_recv()
        for cp in first + passed:
            cp.wait_send()
        for cp in mine:
            cp.wait()

    any_spec = pl.BlockSpec(memory_space=pl.ANY)
    return pl.pallas_call(
        body, name=name, out_shape=tuple(jax.ShapeDtypeStruct((N_DEV,) + s.shape, s.dtype) for s in shards),
        in_specs=[any_spec] * n_arr, out_specs=(any_spec,) * n_arr,
        scratch_shapes=[pltpu.SemaphoreType.DMA((n_arr, 7)), pltpu.SemaphoreType.DMA((n_arr, 7)),
                        pltpu.SemaphoreType.DMA((n_arr,))],
    )(*shards)


def _direct_exchange(x_refs, out_refs, send_sems, recv_sems, local_sems, *, scatter, row0s):
    arrays = range(len(x_refs))
    x, y, c = lax.axis_index("x"), lax.axis_index("y"), lax.axis_index("c")
    me = 4 * x + 2 * y + c

    def src(a, slot):
        return x_refs[a].at[slot] if scatter else x_refs[a]

    def landing(a, sender):
        return out_refs[a].at[sender, pl.ds(row0s[a], x_refs[a].shape[-2]), :]

    def local(a):
        return pltpu.make_async_copy(src(a, me), landing(a, me), local_sems.at[a])

    def copy(a, k, arriving):
        px, py, pc = x ^ (k >> 2), y ^ ((k >> 1) & 1), c ^ (k & 1)
        peer = 4 * px + 2 * py + pc
        return pltpu.make_async_remote_copy(
            src_ref=src(a, peer), dst_ref=landing(a, peer if arriving else me),
            send_sem=send_sems.at[a, k - 1], recv_sem=recv_sems.at[a, k - 1],
            device_id=(px, py, pc), device_id_type=MESH_IDS)

    def start():
        for a in arrays:
            local(a).start()
        for k in range(1, N_DEV):
            for a in arrays:
                copy(a, k, False).start()

    def wait():
        for k in range(1, N_DEV):
            for a in arrays:
                copy(a, k, True).wait_recv()
        for k in range(1, N_DEV):
            for a in arrays:
                copy(a, k, False).wait_send()
        for a in arrays:
            local(a).wait()

    return start, wait


def _exchange_sems(n_arr):
    return [pltpu.SemaphoreType.DMA((n_arr, N_DEV - 1)), pltpu.SemaphoreType.DMA((n_arr, N_DEV - 1)),
            pltpu.SemaphoreType.DMA((n_arr,))]


def _all_to_all(bufs, into, row0s, *, name):
    n_arr = len(bufs)

    def body(*refs):
        x_refs, out_refs = refs[:n_arr], refs[2 * n_arr:3 * n_arr]
        start, wait = _direct_exchange(x_refs, out_refs, *refs[3 * n_arr:], scatter=True, row0s=row0s)
        start()
        wait()

    any_spec = pl.BlockSpec(memory_space=pl.ANY)
    return pl.pallas_call(
        body, name=name, out_shape=tuple(jax.ShapeDtypeStruct(b.shape, b.dtype) for b in into),
        in_specs=[any_spec] * (2 * n_arr), out_specs=(any_spec,) * n_arr,
        input_output_aliases={n_arr + a: a for a in range(n_arr)}, scratch_shapes=_exchange_sems(n_arr),
    )(*bufs, *into)


def _pack_small(vals):
    flat = jnp.concatenate([vals[n].reshape(-1) for n in SMALL])
    rows = -(-flat.shape[0] // (SUBLANES * PACK_COLS)) * SUBLANES
    return jnp.pad(flat, (0, rows * PACK_COLS - flat.shape[0])).reshape(rows, PACK_COLS)


def _unpack_small(flat, shapes):
    flat = flat.reshape(-1)
    out, off = {}, 0
    for n in SMALL:
        size = math.prod(shapes[n])
        out[n] = flat[off:off + size].reshape(shapes[n])
        off += size
    return out


def _pair_gain(g):
    return jnp.tile(g, 2)[None]


def _ssm_setup(sp):
    g, p = SSM_GROUPS, SSM_STATE
    a_re = sp["ssm_a_re"][:, None, :]
    a_im = sp["ssm_a_im"][:, None, :]
    log_dt = jnp.broadcast_to(sp["ssm_log_dt"][:, None, None], (g, 1, p))
    bt_re = sp["ssm_b_re"].transpose(0, 2, 1)
    bt_im = sp["ssm_b_im"].transpose(0, 2, 1)
    return a_re, a_im, log_dt, bt_re, bt_im


def _layer_fwd(h, p_l, w, sp, ride=None):
    hn = _rms_fwd(h, sp["mix_norm_g"][None], name="rms_mix")
    proj = _mm(hn, w["w_in"], name="mm_proj")

    pw_re, pw_im, bbt_re, bbt_im = _ssm_param_fwd(*_ssm_setup(sp))
    pw_re = pw_re.transpose(1, 0, 2).reshape(SUBLANES, SSM_LANES)
    pw_im = pw_im.transpose(1, 0, 2).reshape(SUBLANES, SSM_LANES)
    ssm_mats = (_block_diag(bbt_re).astype(BF16), _block_diag(bbt_im).astype(BF16),
                _block_diag(sp["ssm_c_re"]).astype(BF16), _block_diag(-sp["ssm_c_im"]).astype(BF16),
                pw_re, pw_im, sp["ssm_d"].reshape(1, SSM_WIDTH))
    x_re, x_im, y, z = _ssm_fwd(proj, *ssm_mats)
    zz = _mm(z, w["ssm_w_glu"], name="mm_glu")

    o, bsum, *landed = _attn_fwd(proj, _pair_gain(sp["q_norm_g"]), _pair_gain(sp["k_norm_g"]), ride)

    bglu = sp["ssm_b_glu"][None]
    ycat = _gate_fwd(zz, bglu, proj, o)
    h2, hn2 = _mm_residual_norm(ycat, w["w_out"], h, sp["ple_norm_g"][None], name="mm_out")
    pp = _mm(p_l, w["w_ple_proj"], name="mm_ple_proj")
    pgl, h3 = _mm_gated_add(hn2, w["w_ple_gate"], h2, pp, name="mm_ple_gate")
    saved = dict(h=h, hn=hn, proj=proj, ssm_mats=ssm_mats, x_re=x_re, x_im=x_im, y=y, z=z,
                 zz=zz, bsum=bsum, o=o, ycat=ycat, h2=h2, hn2=hn2,
                 pgl=pgl, pp=pp, p_l=p_l, bglu=bglu)
    return h3, saved, landed


def _layer_bwd(dh3, sv, w, sp, ride=None):
    g, hh, p = SSM_GROUPS, SSM_GROUP, SSM_STATE
    grads = {}
    dgp, dpp = _ple_bwd(dh3, sv["pgl"], sv["pp"])
    grads["w_ple_gate"] = _mm(sv["hn2"], dgp, ta=True, out_dtype=BF16, name="mm_d_ple_gate")
    grads["w_ple_proj"] = _mm(sv["p_l"], dpp, ta=True, out_dtype=BF16, name="mm_d_ple_proj")
    dh2, dg2 = _mm_norm_grad(dgp, w["w_ple_gate"], sv["h2"], sp["ple_norm_g"][None], dh3, name="mm_dhn2")
    grads["ple_norm_g"] = dg2[0]
    grads["w_out"] = _mm(sv["ycat"], dh2, ta=True, out_dtype=BF16, name="mm_d_out")
    dyc = _mm(dh2, w["w_out"], tb=True, name="mm_dycat")
    dzz, dgs, do, dga, dbglu = _gate_bwd(dyc, sv["zz"], sv["bglu"], sv["proj"], sv["o"])
    grads["ssm_b_glu"] = dbglu[0]

    dq, dk, dv, dgq, dgk, *landed = _attn_bwd(sv["proj"], _pair_gain(sp["q_norm_g"]), _pair_gain(sp["k_norm_g"]),
                                              sv["bsum"], do, ride)
    grads["q_norm_g"] = jnp.sum(dgq.reshape(ATTN_HEADS, HEAD_DIM), axis=0)
    grads["k_norm_g"] = jnp.sum(dgk.reshape(ATTN_HEADS, HEAD_DIM), axis=0)

    grads["ssm_w_glu"] = _mm(sv["z"], dzz, ta=True, out_dtype=BF16, name="mm_d_glu")
    dz = _mm(dzz, w["ssm_w_glu"], tb=True, name="mm_dz")
    du, dbc_re, dbc_im, dcc_re, dcc_im, gab_re, gab_im, dd = _ssm_bwd(
        dz, sv["y"], sv["proj"], sv["x_re"], sv["x_im"], *sv["ssm_mats"])
    grads["ssm_d"] = dd.reshape(g, hh)
    grads["ssm_c_re"] = _block_diag_take(dcc_re, hh, p)
    grads["ssm_c_im"] = -_block_diag_take(dcc_im, hh, p)
    da_re, da_im, dldt, dbt_re, dbt_im = _ssm_param_bwd(
        *_ssm_setup(sp), gab_re.reshape(g, 1, p), gab_im.reshape(g, 1, p),
        _block_diag_take(dbc_re, hh, p), _block_diag_take(dbc_im, hh, p))
    grads["ssm_a_re"], grads["ssm_a_im"] = da_re[:, 0, :], da_im[:, 0, :]
    grads["ssm_log_dt"] = dldt[:, 0, 0]
    grads["ssm_b_re"], grads["ssm_b_im"] = dbt_re.transpose(0, 2, 1), dbt_im.transpose(0, 2, 1)

    dproj = jnp.concatenate([du, dgs, dq, dk, dv, dga], axis=1)
    grads["w_in"] = _mm(sv["hn"], dproj, ta=True, out_dtype=BF16, name="mm_d_in")
    dh, dg1 = _mm_norm_grad(dproj, w["w_in"], sv["h"], sp["mix_norm_g"][None], dh2, name="mm_dhn")
    grads["mix_norm_g"] = dg1[0]
    return dh, grads, landed


def _local_step(x, p, target, big, small, fetch=None, ship=None):
    h = x
    big = list(big)
    saved = []
    for l in range(DEPTH):
        sp = {n: small[n][l] for n in SMALL}
        h, sv, landed = _layer_fwd(h, p[l], big[l], sp, fetch[0] if fetch and l == 0 else None)
        if landed:
            big[DEPTH - 1] = fetch[1](landed)
        saved.append(sv)
    dh, loss_parts = _loss_head(h, target)
    grads = [None] * DEPTH
    shipped = []
    for l in reversed(range(DEPTH)):
        sp = {n: small[n][l] for n in SMALL}
        ride = ship(grads[DEPTH - 1]) if ship and l == 0 else None
        dh, grads[l], landed = _layer_bwd(dh, saved[l], big[l], sp, ride)
        shipped = landed or shipped
    return jnp.sum(loss_parts), dh, grads, shipped


def kernel(x, p, mix_norm_g, w_in, ssm_a_re, ssm_a_im, ssm_log_dt, ssm_b_re, ssm_b_im, ssm_c_re, ssm_c_im, ssm_d, ssm_w_glu, ssm_b_glu, q_norm_g, k_norm_g, w_out, ple_norm_g, w_ple_gate, w_ple_proj, loss_target, m_mix_norm_g, m_w_in, m_ssm_a_re, m_ssm_a_im, m_ssm_log_dt, m_ssm_b_re, m_ssm_b_im, m_ssm_c_re, m_ssm_c_im, m_ssm_d, m_ssm_w_glu, m_ssm_b_glu, m_q_norm_g, m_k_norm_g, m_w_out, m_ple_norm_g, m_w_ple_gate, m_w_ple_proj, v_mix_norm_g, v_w_in, v_ssm_a_re, v_ssm_a_im, v_ssm_log_dt, v_ssm_b_re, v_ssm_b_im, v_ssm_c_re, v_ssm_c_im, v_ssm_d, v_ssm_w_glu, v_ssm_b_glu, v_q_norm_g, v_k_norm_g, v_w_out, v_ple_norm_g, v_w_ple_gate, v_w_ple_proj):
    given = dict(locals())
    wts = {n: given[n] for n in WEIGHTS}
    mom = {n: given["m_" + n] for n in WEIGHTS}
    var = {n: given["v_" + n] for n in WEIGHTS}

    def rows2d(a):
        return a.reshape(-1, a.shape[-1])

    assert DEPTH == 2
    groups = (("w_in",), ("ssm_w_glu", "w_ple_proj"), ("w_out", "w_ple_gate"))
    layer_rows = [sum(wts[n].shape[1] for n in names) for names in groups]
    widths = [wts[names[0]].shape[2] for names in groups]

    def shards_of(l):
        return [jnp.concatenate([wts[n][l].astype(BF16) for n in names], axis=0) for names in groups]

    def weights_of(gathered):
        full = {}
        for names, got in zip(groups, gathered):
            row0 = 0
            for n in names:
                r = wts[n].shape[1]
                blocks = got[:, row0:row0 + r, :]
                full[n] = blocks.reshape(N_DEV * r, -1) if n in ROW_SHARDED else blocks.transpose(1, 0, 2).reshape(r, -1)
                row0 += r
        return full

    def blocks_to_send(layer_grads):
        def blocks(n):
            g = layer_grads[n]
            if n in ROW_SHARDED:
                return g.reshape(N_DEV, -1, g.shape[1])
            return g.reshape(g.shape[0], N_DEV, -1).transpose(1, 0, 2)

        return [jnp.concatenate([blocks(n) for n in names], axis=1) for names in groups]

    fetch = (dict(bufs=shards_of(1), scatter=False, row0s=[0] * len(groups),
                  into_shapes=[jax.ShapeDtypeStruct((N_DEV, r, c), BF16) for r, c in zip(layer_rows, widths)]),
             weights_of)

    def ship(layer_grads):
        return dict(bufs=blocks_to_send(layer_grads), scatter=True, row0s=layer_rows,
                    into_shapes=[jax.ShapeDtypeStruct((N_DEV, DEPTH * r, c), BF16) for r, c in zip(layer_rows, widths)])

    first = weights_of(_all_gather(shards_of(0), name="gather_weights"))
    small = {n: wts[n] for n in SMALL}
    loss, dx, grads, parts = _local_step(x[0], p[:, 0], loss_target[0], [first, None], small, fetch, ship)
    loss = lax.psum(loss, ("x", "y", "c"))

    parts = _all_to_all(blocks_to_send(grads[0]), parts, [0] * len(groups), name="exchange_weight_grads")
    big_out = {}
    for names, part, rows in zip(groups, parts, layer_rows):
        row0 = 0
        for n in names:
            res = _reduce_adamw(part, rows2d(wts[n]), rows2d(mom[n]), rows2d(var[n]),
                                row0s=[l * rows + row0 for l in range(DEPTH)], name="adamw_" + n)
            big_out[n] = [a.reshape(wts[n].shape) for a in res]
            row0 += wts[n].shape[1]

    small_shapes = {n: wts[n].shape for n in SMALL}
    local_small = _pack_small({n: jnp.stack([grads[l][n] for l in range(DEPTH)]) for n in SMALL})
    (all_small,) = _all_gather([local_small], name="gather_small_grads")
    small_out = _reduce_adamw(all_small, _pack_small(small), _pack_small({n: mom[n] for n in SMALL}),
                              _pack_small({n: var[n] for n in SMALL}), name="adamw_replicated")
    small_out = [_unpack_small(a, small_shapes) for a in small_out]

    outs = [loss, dx[None]]
    for k in range(4):
        outs += [big_out[n][k] if n in BIG else small_out[k][n] for n in WEIGHTS]
    return tuple(outs)
```

```python
import math

import jax
import jax.numpy as jnp
from jax import lax
from jax.experimental import pallas as pl
from jax.experimental.pallas import tpu as pltpu

F32 = jnp.float32
BF16 = jnp.bfloat16
MESH_IDS = pl.DeviceIdType.MESH

N_DEV = 8
D_MODEL = 1024
DEPTH = 2
PLE_DIM = 256
SSM_WIDTH = 512
SSM_GROUP = 16
SSM_GROUPS = 32
SSM_STATE = 64
SSM_LANES = SSM_GROUPS * SSM_STATE
ATTN_WIDTH = 512
ATTN_HEADS = 8
HEAD_DIM = 64
RMS_EPS = 1e-6
ADAM_LR = 0.001
ADAM_B1 = 0.9
ADAM_B2 = 0.999
ADAM_EPS = 1e-08
ADAM_WD = 0.01
ADAM_STEP = 10

VMEM_LIMIT = 56 * 1024 * 1024
ATT_TQ = 512
ATT_TK = 128
SSM_BLOCK_GROUPS = 8
SSM_TIME_BLOCK = 512
SUBLANES = 8
PACK_COLS = 1024

BIG = ("w_in", "ssm_w_glu", "w_out", "w_ple_gate", "w_ple_proj")
ROW_SHARDED = ("w_out", "w_ple_gate")
SMALL = ("mix_norm_g", "ssm_a_re", "ssm_a_im", "ssm_log_dt", "ssm_b_re", "ssm_b_im", "ssm_c_re",
         "ssm_c_im", "ssm_d", "ssm_b_glu", "q_norm_g", "k_norm_g", "ple_norm_g")
WEIGHTS = ("mix_norm_g", "w_in", "ssm_a_re", "ssm_a_im", "ssm_log_dt", "ssm_b_re", "ssm_b_im",
           "ssm_c_re", "ssm_c_im", "ssm_d", "ssm_w_glu", "ssm_b_glu", "q_norm_g", "k_norm_g",
           "w_out", "ple_norm_g", "w_ple_gate", "w_ple_proj")
_NT = (((1,), (1,)), ((), ()))
_TN = (((0,), (0,)), ((), ()))


def _params(sem=None):
    return pltpu.CompilerParams(dimension_semantics=sem, vmem_limit_bytes=VMEM_LIMIT)


def _sigmoid(x):
    return 1.0 / (1.0 + jnp.exp(-x))


_GELU_K = math.sqrt(2.0 / math.pi)
_GELU_C = 0.044715


def _gelu(x):
    return 0.5 * x * (1.0 + jnp.tanh(_GELU_K * (x + _GELU_C * x * x * x)))


def _gelu_grad(x):
    th = jnp.tanh(_GELU_K * (x + _GELU_C * x * x * x))
    return 0.5 * (1.0 + th) + 0.5 * x * (1.0 - th * th) * _GELU_K * (1.0 + 3.0 * _GELU_C * x * x)


def _mm(a, b, *, name, ta=False, tb=False, add=None, out_dtype=F32, tm=1024, tn=1024, tk=512,
        finish=None, extras=(), outs=None, ride=None):
    m, k = (a.shape[1], a.shape[0]) if ta else a.shape
    n = b.shape[0] if tb else b.shape[1]
    tm, tn, tk = min(tm, m), min(tn, n), min(tk, k)
    assert m % tm == 0 and n % tn == 0 and k % tk == 0, (name, a.shape, b.shape)
    n_steps = k // tk
    dims = (((0 if ta else 1,), (1 if tb else 0,)), ((), ()))
    a_spec = (pl.BlockSpec((tk, tm), lambda i, j, kk: (kk, i)) if ta
              else pl.BlockSpec((tm, tk), lambda i, j, kk: (i, kk)))
    b_spec = (pl.BlockSpec((tn, tk), lambda i, j, kk: (j, kk)) if tb
              else pl.BlockSpec((tk, tn), lambda i, j, kk: (kk, j)))
    o_spec = pl.BlockSpec((tm, tn), lambda i, j, kk: (i, j))
    ins, in_specs = [a, b], [a_spec, b_spec]
    if finish is None:
        if add is not None:
            ins.append(add)
            in_specs.append(o_spec)

        def finish(res, extra_refs, out_refs):
            if add is not None:
                res = res + extra_refs[0][...].astype(F32)
            out_refs[0][...] = res.astype(out_dtype)

        out_specs, out_shapes = [o_spec], [jax.ShapeDtypeStruct((m, n), out_dtype)]
    else:
        assert tn == n and add is None
        for arr, kind in extras:
            ins.append(arr)
            cols = arr.shape[1]
            in_specs.append(pl.BlockSpec((tm, cols), lambda i, j, kk: (i, 0)) if kind == "rows"
                            else pl.BlockSpec((1, cols), lambda i, j, kk: (0, 0)))
        out_specs = [pl.BlockSpec((tm, cols), lambda i, j, kk: (i, 0)) if kind == "rows"
                     else pl.BlockSpec((1, cols), lambda i, j, kk: (0, 0)) for cols, _, kind in outs]
        out_shapes = [jax.ShapeDtypeStruct((m if kind == "rows" else 1, cols), dtype) for cols, dtype, kind in outs]
    n_in = len(ins)

    def body(*refs):
        a_ref, b_ref, acc_ref = refs[0], refs[1], refs[-1]
        kk = pl.program_id(2)

        @pl.when(kk == 0)
        def _():
            acc_ref[...] = jnp.zeros_like(acc_ref)

        acc_ref[...] += lax.dot_general(a_ref[...].astype(BF16), b_ref[...].astype(BF16), dims,
                                        preferred_element_type=F32)

        @pl.when(kk == n_steps - 1)
        def _():
            finish(acc_ref[...], refs[2:n_in], refs[n_in:-1])

    sums = outs is not None and any(kind == "sum" for _, _, kind in outs)
    res = _call_with_ride_along(
        body, ride, name=name, grid=(m // tm, n // tn, n_steps),
        semantics=("arbitrary",) * 3 if sums else ("parallel", "parallel", "arbitrary"),
        in_specs=in_specs, out_specs=tuple(out_specs), out_shape=tuple(out_shapes),
        scratch_shapes=[pltpu.VMEM((tm, tn), F32)], operands=ins)
    return res[0] if outs is None and ride is None else res


def _rms(hv, g):
    r = lax.rsqrt(jnp.mean(hv * hv, axis=-1, keepdims=True) + RMS_EPS)
    return hv * r * g, r


def _rms_grad(hv, g, dn):
    _, r = _rms(hv, g)
    a = dn * g
    dot = jnp.mean(a * hv, axis=-1, keepdims=True)
    return r * a - hv * (r * r * r * dot), jnp.sum(dn * hv * r, axis=0, keepdims=True)


def _mm_residual_norm(a, b, h, g, *, name):
    def finish(res, extra_refs, out_refs):
        h_ref, g_ref = extra_refs
        h2 = res + h_ref[...]
        out_refs[0][...] = h2
        out_refs[1][...] = _rms(h2, g_ref[...])[0].astype(BF16)

    d = b.shape[1]
    return _mm(a, b, name=name, tm=512, finish=finish, extras=[(h, "rows"), (g, "vec")],
               outs=[(d, F32, "rows"), (d, BF16, "rows")])


def _mm_gated_add(a, b, h2, pp, *, name):
    def finish(res, extra_refs, out_refs):
        h_ref, p_ref = extra_refs
        out_refs[0][...] = res
        out_refs[1][...] = h_ref[...] + _sigmoid(res) * p_ref[...]

    d = b.shape[1]
    return _mm(a, b, name=name, tm=512, finish=finish, extras=[(h2, "rows"), (pp, "rows")],
               outs=[(d, F32, "rows"), (d, F32, "rows")])


def _mm_norm_grad(dy, w, h, g, dres, *, name, ride=None):
    def finish(res, extra_refs, out_refs):
        h_ref, g_ref, r_ref = extra_refs
        dh, dg = _rms_grad(h_ref[...], g_ref[...], res)
        out_refs[0][...] = dh + r_ref[...]

        @pl.when(pl.program_id(0) == 0)
        def _():
            out_refs[1][...] = jnp.zeros_like(out_refs[1])

        out_refs[1][...] += dg

    d = w.shape[0]
    return _mm(dy, w, name=name, tb=True, tm=512, finish=finish, extras=[(h, "rows"), (g, "vec"), (dres, "rows")],
               outs=[(d, F32, "rows"), (d, F32, "sum")], ride=ride)


def _rows_tile(rows, want=512):
    t = min(rows, want)
    assert rows % t == 0
    return t


def _rowwise(body, name, rows, ins, outs):
    widest = max(cols for _, cols, _ in ins)
    tm = _rows_tile(rows, 512 if widest > 128 else 4096)
    in_specs = []
    for _, cols, cb in ins:
        if cb is None:
            in_specs.append(pl.BlockSpec((1, cols), lambda i: (0, 0)))
        else:
            in_specs.append(pl.BlockSpec((tm, cols), lambda i, cb=cb: (i, cb)))
    out_specs, out_shapes = [], []
    for cols, dtype, is_acc in outs:
        if is_acc:
            out_specs.append(pl.BlockSpec((1, cols), lambda i: (0, 0)))
            out_shapes.append(jax.ShapeDtypeStruct((1, cols), dtype))
        else:
            out_specs.append(pl.BlockSpec((tm, cols), lambda i: (i, 0)))
            out_shapes.append(jax.ShapeDtypeStruct((rows, cols), dtype))
    any_acc = any(o[2] for o in outs)
    return pl.pallas_call(
        body, name=name, grid=(rows // tm,), in_specs=in_specs, out_specs=tuple(out_specs),
        out_shape=tuple(out_shapes),
        compiler_params=_params(("arbitrary",) if any_acc else ("parallel",)))(*[a for a, _, _ in ins])


def _accumulate(ref, value):
    @pl.when(pl.program_id(0) == 0)
    def _():
        ref[...] = jnp.zeros_like(ref)

    ref[...] += value


def _rms_fwd(h, g, *, name):
    rows, d = h.shape

    def body(h_ref, g_ref, o_ref):
        o_ref[...] = _rms(h_ref[...], g_ref[...])[0].astype(BF16)

    return _rowwise(body, name, rows, [(h, d, 0), (g, d, None)], [(d, BF16, False)])[0]


def _gate_fwd(zz, bglu, proj, o):
    rows = zz.shape[0]
    w = SSM_WIDTH

    def body(zz_ref, b_ref, gs_ref, o_ref, ga_ref, y_ref):
        zz_v = zz_ref[...] + b_ref[...]
        val, gate = zz_v[:, :w], zz_v[:, w:]
        gs, ga = gs_ref[...], ga_ref[...]
        y_ref[:, :w] = (val * _sigmoid(gate) * (gs * _sigmoid(gs))).astype(BF16)
        y_ref[:, w:] = (o_ref[...] * (ga * _sigmoid(ga))).astype(BF16)

    return _rowwise(body, "gate_fwd", rows,
                    [(zz, 2 * w, 0), (bglu, 2 * w, None), (proj, w, 1), (o, w, 0), (proj, w, 5)],
                    [(2 * w, BF16, False)])[0]


def _gate_bwd(dyc, zz, bglu, proj, o):
    rows = zz.shape[0]
    w = SSM_WIDTH

    def body(dy_ref, zz_ref, b_ref, gs_ref, o_ref, ga_ref, dzz_ref, dgs_ref, do_ref, dga_ref, db_ref):
        zz_v = zz_ref[...] + b_ref[...]
        val, gate = zz_v[:, :w], zz_v[:, w:]
        gs, ga = gs_ref[...], ga_ref[...]
        dys, dya = dy_ref[:, :w], dy_ref[:, w:]
        sg, ss, sa = _sigmoid(gate), _sigmoid(gs), _sigmoid(ga)
        glu = val * sg
        dglu = dys * (gs * ss)
        dgs_ref[...] = dys * glu * (ss * (1.0 + gs * (1.0 - ss)))
        dval = dglu * sg
        dgate = dglu * val * sg * (1.0 - sg)
        dzz_ref[:, :w] = dval.astype(BF16)
        dzz_ref[:, w:] = dgate.astype(BF16)
        do_ref[...] = dya * (ga * sa)
        dga_ref[...] = dya * o_ref[...] * (sa * (1.0 + ga * (1.0 - sa)))
        _accumulate(db_ref, jnp.concatenate([jnp.sum(dval, axis=0, keepdims=True),
                                             jnp.sum(dgate, axis=0, keepdims=True)], axis=1))

    return _rowwise(body, "gate_bwd", rows,
                    [(dyc, 2 * w, 0), (zz, 2 * w, 0), (bglu, 2 * w, None), (proj, w, 1), (o, w, 0), (proj, w, 5)],
                    [(2 * w, BF16, False), (w, F32, False), (w, F32, False), (w, F32, False), (2 * w, F32, True)])


def _ple_bwd(dh3, pgl, pp):
    rows, d = dh3.shape

    def body(dh_ref, g_ref, p_ref, dg_ref, dp_ref):
        sg = _sigmoid(g_ref[...])
        dh = dh_ref[...]
        dg_ref[...] = (dh * p_ref[...] * sg * (1.0 - sg)).astype(BF16)
        dp_ref[...] = (dh * sg).astype(BF16)

    return _rowwise(body, "ple_bwd", rows, [(dh3, d, 0), (pgl, d, 0), (pp, d, 0)],
                    [(d, BF16, False), (d, BF16, False)])


def _loss_head(h, target):
    rows, d = h.shape

    def body(h_ref, t_ref, dh_ref, l_ref):
        err = h_ref[...] - t_ref[...]
        dh_ref[...] = err * (1.0 / d)
        _accumulate(l_ref, jnp.sum(err * err, axis=0, keepdims=True) * (0.5 / d))

    return _rowwise(body, "loss_head", rows, [(h, d, 0), (target, d, 0)], [(d, F32, False), (d, F32, True)])


def _zoh(lr, li, dt):
    mag = jnp.exp(lr * dt)
    ab_re = mag * jnp.cos(li * dt)
    ab_im = mag * jnp.sin(li * dt)
    num_re = ab_re - 1.0
    den = lr * lr + li * li
    f_re = (num_re * lr + ab_im * li) / den
    f_im = (ab_im * lr - num_re * li) / den
    return ab_re, ab_im, f_re, f_im, den


def _ssm_param_fwd(a_re, a_im, log_dt, bt_re, bt_im):
    g, _, p = a_re.shape
    h = bt_re.shape[1]

    def body(lr_ref, li_ref, ldt_ref, br_ref, bi_ref, pr_ref, pi_ref, bbr_ref, bbi_ref):
        dt = jnp.exp(ldt_ref[...])
        ab_re, ab_im, f_re, f_im, _ = _zoh(lr_ref[...], li_ref[...], dt)
        cr, ci = ab_re, ab_im
        for k in range(SUBLANES):
            pr_ref[:, k:k + 1, :] = cr
            pi_ref[:, k:k + 1, :] = ci
            cr, ci = cr * ab_re - ci * ab_im, cr * ab_im + ci * ab_re
        br, bi = br_ref[...], bi_ref[...]
        bbr_ref[...] = f_re * br - f_im * bi
        bbi_ref[...] = f_re * bi + f_im * br

    pw = jax.ShapeDtypeStruct((g, SUBLANES, p), F32)
    large = jax.ShapeDtypeStruct((g, h, p), F32)
    return pl.pallas_call(body, name="ssm_param_fwd", out_shape=(pw, pw, large, large),
                          compiler_params=_params())(a_re, a_im, log_dt, bt_re, bt_im)


def _ssm_param_bwd(a_re, a_im, log_dt, bt_re, bt_im, gab_re, gab_im, gbb_re, gbb_im):
    g, _, p = a_re.shape
    h = bt_re.shape[1]

    def body(lr_ref, li_ref, ldt_ref, br_ref, bi_ref, gar_ref, gai_ref, gbr_ref, gbi_ref,
             dlr_ref, dli_ref, dldt_ref, dbr_ref, dbi_ref):
        lr, li = lr_ref[...], li_ref[...]
        dt = jnp.exp(ldt_ref[...])
        ab_re, ab_im, f_re, f_im, den = _zoh(lr, li, dt)
        br, bi = br_ref[...], bi_ref[...]
        gbr, gbi = gbr_ref[...], gbi_ref[...]
        dbr_ref[...] = f_re * gbr + f_im * gbi
        dbi_ref[...] = f_re * gbi - f_im * gbr
        gf_re = jnp.sum(br * gbr + bi * gbi, axis=1, keepdims=True)
        gf_im = jnp.sum(br * gbi - bi * gbr, axis=1, keepdims=True)
        il_re, il_im = lr / den, -li / den
        ga_re = gar_ref[...] + il_re * gf_re + il_im * gf_im
        ga_im = gai_ref[...] + il_re * gf_im - il_im * gf_re
        q_re = f_re * il_re - f_im * il_im
        q_im = f_re * il_im + f_im * il_re
        gl_re = -(q_re * gf_re + q_im * gf_im)
        gl_im = -(q_re * gf_im - q_im * gf_re)
        dlr_ref[...] = gl_re + dt * (ab_re * ga_re + ab_im * ga_im)
        dli_ref[...] = gl_im + dt * (ab_re * ga_im - ab_im * ga_re)
        la_re = lr * ab_re - li * ab_im
        la_im = lr * ab_im + li * ab_re
        dldt_ref[...] = jnp.sum((la_re * ga_re + la_im * ga_im) * dt, axis=2, keepdims=True)

    small = jax.ShapeDtypeStruct((g, 1, p), F32)
    one = jax.ShapeDtypeStruct((g, 1, 1), F32)
    large = jax.ShapeDtypeStruct((g, h, p), F32)
    return pl.pallas_call(body, name="ssm_param_bwd", out_shape=(small, small, one, large, large),
                          compiler_params=_params())(
                              a_re, a_im, log_dt, bt_re, bt_im, gab_re, gab_im, gbb_re, gbb_im)


def _block_diag(m):
    g, h, p = m.shape
    nb = g // SSM_BLOCK_GROUPS
    eye = jnp.eye(SSM_BLOCK_GROUPS, dtype=m.dtype)
    m4 = m.reshape(nb, SSM_BLOCK_GROUPS, h, p)
    return (m4[:, :, :, None, :] * eye[None, :, None, :, None]).reshape(nb, SSM_BLOCK_GROUPS * h, SSM_BLOCK_GROUPS * p)


def _block_diag_take(m, h, p):
    nb = m.shape[0]
    eye = jnp.eye(SSM_BLOCK_GROUPS, dtype=m.dtype)
    m5 = m.reshape(nb, SSM_BLOCK_GROUPS, h, SSM_BLOCK_GROUPS, p)
    return jnp.sum(m5 * eye[None, :, None, :, None], axis=3).reshape(nb * SSM_BLOCK_GROUPS, h, p)


def _scan_tile(br, bi, pw_re, pw_im, reverse):
    row = lax.broadcasted_iota(jnp.int32, br.shape, 0)
    xr, xi = br, bi
    for d in (1, 2, 4):
        ar, ai = pw_re[d - 1:d, :], pw_im[d - 1:d, :]
        if reverse:
            keep = row < SUBLANES - d
            sr = jnp.where(keep, pltpu.roll(xr, SUBLANES - d, 0), 0.0)
            si = jnp.where(keep, pltpu.roll(xi, SUBLANES - d, 0), 0.0)
        else:
            keep = row >= d
            sr = jnp.where(keep, pltpu.roll(xr, d, 0), 0.0)
            si = jnp.where(keep, pltpu.roll(xi, d, 0), 0.0)
        xr, xi = xr + ar * sr - ai * si, xi + ar * si + ai * sr
    return xr, xi


def _ssm_blocks(s):
    tt = min(SSM_TIME_BLOCK, s)
    assert s % tt == 0 and tt % SUBLANES == 0
    ch = SSM_BLOCK_GROUPS * SSM_GROUP
    st = SSM_BLOCK_GROUPS * SSM_STATE
    return tt, s // tt, SSM_WIDTH // ch, ch, st


def _ssm_fwd(proj, bc_re, bc_im, cc_re, cc_im, pw_re, pw_im, dvec):
    s = proj.shape[0]
    tt, nt, nb, ch, st = _ssm_blocks(s)
    n_tiles = tt // SUBLANES

    def body(u_ref, bre_ref, bim_ref, cre_ref, cim_ref, pr_ref, pi_ref, d_ref,
             xr_ref, xi_ref, y_ref, z_ref, carry_ref):
        @pl.when(pl.program_id(1) == 0)
        def _():
            carry_ref[...] = jnp.zeros_like(carry_ref)

        u = u_ref[...]
        ub = u.astype(BF16)
        xr_ref[...] = jnp.dot(ub, bre_ref[0], preferred_element_type=F32)
        xi_ref[...] = jnp.dot(ub, bim_ref[0], preferred_element_type=F32)
        pw_r, pw_i = pr_ref[...], pi_ref[...]

        def step(t, carry):
            cr, ci = carry
            rows = pl.ds(pl.multiple_of(t * SUBLANES, SUBLANES), SUBLANES)
            xr, xi = _scan_tile(xr_ref[rows, :], xi_ref[rows, :], pw_r, pw_i, False)
            xr, xi = xr + pw_r * cr - pw_i * ci, xi + pw_r * ci + pw_i * cr
            xr_ref[rows, :] = xr
            xi_ref[rows, :] = xi
            return (jnp.broadcast_to(xr[SUBLANES - 1:SUBLANES, :], (SUBLANES, st)),
                    jnp.broadcast_to(xi[SUBLANES - 1:SUBLANES, :], (SUBLANES, st)))

        cr, ci = lax.fori_loop(0, n_tiles, step, (carry_ref[0], carry_ref[1]), unroll=2)
        carry_ref[0] = cr
        carry_ref[1] = ci
        y = (lax.dot_general(xr_ref[...].astype(BF16), cre_ref[0], _NT, preferred_element_type=F32)
             + lax.dot_general(xi_ref[...].astype(BF16), cim_ref[0], _NT, preferred_element_type=F32)
             + d_ref[...] * u)
        y_ref[...] = y
        z_ref[...] = _gelu(y).astype(BF16)

    chan = pl.BlockSpec((tt, ch), lambda b, t: (t, b))
    state = pl.BlockSpec((tt, st), lambda b, t: (t, b))
    mat = pl.BlockSpec((1, ch, st), lambda b, t: (b, 0, 0))
    pw = pl.BlockSpec((SUBLANES, st), lambda b, t: (0, b))
    vec = pl.BlockSpec((1, ch), lambda b, t: (0, b))
    x_shape = jax.ShapeDtypeStruct((s, nb * st), F32)
    return pl.pallas_call(
        body, name="ssm_fwd", grid=(nb, nt), in_specs=[chan, mat, mat, mat, mat, pw, pw, vec],
        out_specs=(state, state, chan, chan),
        out_shape=(x_shape, x_shape, jax.ShapeDtypeStruct((s, nb * ch), F32), jax.ShapeDtypeStruct((s, nb * ch), BF16)),
        scratch_shapes=[pltpu.VMEM((2, SUBLANES, st), F32)],
        compiler_params=_params(("parallel", "arbitrary")))(proj, bc_re, bc_im, cc_re, cc_im, pw_re, pw_im, dvec)


def _ssm_bwd(dz, y, proj, x_re, x_im, bc_re, bc_im, cc_re, cc_im, pw_re, pw_im, dvec, ride=None):
    s = proj.shape[0]
    tt, nt, nb, ch, st = _ssm_blocks(s)
    n_tiles = tt // SUBLANES

    def body(dz_ref, y_ref, u_ref, xr_ref, xi_ref, bre_ref, bim_ref, cre_ref, cim_ref, pr_ref, pi_ref, d_ref,
             du_ref, dbr_ref, dbi_ref, dcr_ref, dci_ref, gar_ref, gai_ref, dd_ref,
             lr_ref, li_ref, carry_ref, acc_ref):
        @pl.when(pl.program_id(1) == 0)
        def _():
            for ref in (carry_ref, acc_ref, dbr_ref, dbi_ref, dcr_ref, dci_ref, dd_ref):
                ref[...] = jnp.zeros_like(ref)

        u = u_ref[...]
        dy = dz_ref[...] * _gelu_grad(y_ref[...])
        ub, dyb = u.astype(BF16), dy.astype(BF16)
        lr_ref[...] = jnp.dot(dyb, cre_ref[0], preferred_element_type=F32)
        li_ref[...] = jnp.dot(dyb, cim_ref[0], preferred_element_type=F32)
        pw_r, pw_i = pr_ref[...], -pi_ref[...]
        pwc_r = jnp.concatenate([pw_r[SUBLANES - 1 - i:SUBLANES - i, :] for i in range(SUBLANES)], axis=0)
        pwc_i = jnp.concatenate([pw_i[SUBLANES - 1 - i:SUBLANES - i, :] for i in range(SUBLANES)], axis=0)
        row = lax.broadcasted_iota(jnp.int32, (SUBLANES, st), 0)

        def step(t, carry):
            cr, ci, acc_r, acc_i = carry
            rows = pl.ds(pl.multiple_of((n_tiles - 1 - t) * SUBLANES, SUBLANES), SUBLANES)
            lr, li = _scan_tile(lr_ref[rows, :], li_ref[rows, :], pw_r, pw_i, True)
            lr, li = lr + pwc_r * cr - pwc_i * ci, li + pwc_r * ci + pwc_i * cr
            lr_ref[rows, :] = lr
            li_ref[rows, :] = li
            nr = jnp.where(row < SUBLANES - 1, pltpu.roll(lr, SUBLANES - 1, 0), cr)
            ni = jnp.where(row < SUBLANES - 1, pltpu.roll(li, SUBLANES - 1, 0), ci)
            xr, xi = xr_ref[rows, :], xi_ref[rows, :]
            return (jnp.broadcast_to(lr[0:1, :], (SUBLANES, st)), jnp.broadcast_to(li[0:1, :], (SUBLANES, st)),
                    acc_r + xr * nr + xi * ni, acc_i + xr * ni - xi * nr)

        cr, ci, acc_r, acc_i = lax.fori_loop(
            0, n_tiles, step, (carry_ref[0], carry_ref[1], acc_ref[0], acc_ref[1]), unroll=2)
        carry_ref[0], carry_ref[1] = cr, ci
        acc_ref[0], acc_ref[1] = acc_r, acc_i
        lrb, lib = lr_ref[...].astype(BF16), li_ref[...].astype(BF16)
        du_ref[...] = (lax.dot_general(lrb, bre_ref[0], _NT, preferred_element_type=F32)
                       + lax.dot_general(lib, bim_ref[0], _NT, preferred_element_type=F32) + dy * d_ref[...])
        dbr_ref[0] += lax.dot_general(ub, lrb, _TN, preferred_element_type=F32)
        dbi_ref[0] += lax.dot_general(ub, lib, _TN, preferred_element_type=F32)
        dcr_ref[0] += lax.dot_general(dyb, xr_ref[...].astype(BF16), _TN, preferred_element_type=F32)
        dci_ref[0] += lax.dot_general(dyb, xi_ref[...].astype(BF16), _TN, preferred_element_type=F32)
        dd_ref[...] += jnp.sum(dy * u, axis=0, keepdims=True)

        @pl.when(pl.program_id(1) == nt - 1)
        def _():
            gar_ref[...] = jnp.sum(acc_r, axis=0, keepdims=True)
            gai_ref[...] = jnp.sum(acc_i, axis=0, keepdims=True)

    chan = pl.BlockSpec((tt, ch), lambda b, t: (nt - 1 - t, b))
    state = pl.BlockSpec((tt, st), lambda b, t: (nt - 1 - t, b))
    mat = pl.BlockSpec((1, ch, st), lambda b, t: (b, 0, 0))
    pw = pl.BlockSpec((SUBLANES, st), lambda b, t: (0, b))
    vec = pl.BlockSpec((1, ch), lambda b, t: (0, b))
    svec = pl.BlockSpec((1, st), lambda b, t: (0, b))
    mat_shape = jax.ShapeDtypeStruct((nb, ch, st), F32)
    return _call_with_ride_along(
        body, ride, name="ssm_bwd", grid=(nb, nt), semantics=("parallel", "arbitrary"),
        in_specs=[chan, chan, chan, state, state, mat, mat, mat, mat, pw, pw, vec],
        out_specs=(chan, mat, mat, mat, mat, svec, svec, vec),
        out_shape=(jax.ShapeDtypeStruct((s, nb * ch), F32), mat_shape, mat_shape, mat_shape, mat_shape,
                   jax.ShapeDtypeStruct((1, nb * st), F32), jax.ShapeDtypeStruct((1, nb * st), F32),
                   jax.ShapeDtypeStruct((1, nb * ch), F32)),
        scratch_shapes=[pltpu.VMEM((tt, st), F32), pltpu.VMEM((tt, st), F32),
                        pltpu.VMEM((2, SUBLANES, st), F32), pltpu.VMEM((2, SUBLANES, st), F32)],
        operands=(dz, y, proj, x_re, x_im, bc_re, bc_im, cc_re, cc_im, pw_re, pw_im, dvec))


def _split_dot(x, tri2):
    hi = x.astype(BF16)
    lo = (x - hi.astype(F32)).astype(BF16)
    return jnp.dot(jnp.concatenate([hi, lo], axis=1), tri2, preferred_element_type=F32)


def _tri_and_ones(tk, rel):
    r_i = lax.broadcasted_iota(jnp.int32, (2 * tk, 2 * tk), 0) % tk
    c_i = lax.broadcasted_iota(jnp.int32, (2 * tk, 2 * tk), 1)
    return jnp.where((c_i >= tk) | rel(r_i, c_i), 1.0, 0.0).astype(BF16)


def _attn_scores(qs, kb, own):
    z = lax.dot_general(qs, kb, _NT, preferred_element_type=F32)
    log_beta = jnp.minimum(z, 0.0) - jnp.log(1.0 + jnp.exp(-jnp.abs(z)))
    log_stay = log_beta - z
    if not own:
        return log_beta, log_stay, None
    mask = lax.broadcasted_iota(jnp.int32, z.shape, 1) < lax.broadcasted_iota(jnp.int32, z.shape, 0)
    return log_beta, jnp.where(mask, log_stay, 0.0), mask


def _attn_blocks(s):
    tq, tk = min(ATT_TQ, s), min(ATT_TK, s)
    assert s % tq == 0 and tq % tk == 0
    return tq, tk, tq // tk


def _first_head(shape, axis):
    return lax.broadcasted_iota(jnp.int32, shape, axis) < HEAD_DIM


def _pair_rms(x, g):
    first = _first_head(x.shape, 1)
    sq = x * x
    ms_a = jnp.sum(jnp.where(first, sq, 0.0), axis=1, keepdims=True) * (1.0 / HEAD_DIM)
    ms_b = jnp.sum(jnp.where(first, 0.0, sq), axis=1, keepdims=True) * (1.0 / HEAD_DIM)
    r = jnp.where(first, lax.rsqrt(ms_a + RMS_EPS), lax.rsqrt(ms_b + RMS_EPS))
    return x * r * g, r


def _pair_rms_bwd(x, g, r, dn):
    first = _first_head(x.shape, 1)
    a = dn * g
    ax = a * x
    dot_a = jnp.sum(jnp.where(first, ax, 0.0), axis=1, keepdims=True) * (1.0 / HEAD_DIM)
    dot_b = jnp.sum(jnp.where(first, 0.0, ax), axis=1, keepdims=True) * (1.0 / HEAD_DIM)
    dx = r * a - x * (r * r * r * jnp.where(first, dot_a, dot_b))
    return dx, jnp.sum(dn * x * r, axis=0, keepdims=True)


_Q_BLOCK0 = 2 * SSM_WIDTH // (2 * HEAD_DIM)
_K_BLOCK0 = _Q_BLOCK0 + ATTN_WIDTH // (2 * HEAD_DIM)
_V_BLOCK0 = _K_BLOCK0 + ATTN_WIDTH // (2 * HEAD_DIM)
HEAD_PAIRS = ATTN_HEADS // 2


def _fill_keys(k_ref, v_ref, gk_ref, kn_ref, vb_ref, rows_per_step):
    def fill(c, _):
        rows = pl.ds(pl.multiple_of(c * rows_per_step, rows_per_step), rows_per_step)
        kn, _ = _pair_rms(k_ref[rows, :], gk_ref[...])
        kn_ref[rows, :] = kn.astype(BF16)
        vb_ref[rows, :] = v_ref[rows, :].astype(BF16)
        return 0

    lax.fori_loop(0, k_ref.shape[0] // rows_per_step, fill, 0)


def _call_with_ride_along(body, ride, *, name, grid, semantics, in_specs, out_specs, out_shape, scratch_shapes,
                          operands):
    if ride is None:
        return pl.pallas_call(body, name=name, grid=grid, in_specs=in_specs, out_specs=out_specs, out_shape=out_shape,
                              scratch_shapes=scratch_shapes, compiler_params=_params(semantics))(*operands)
    n, n_in, n_out = len(ride["bufs"]), len(in_specs), len(out_specs)
    into = list(ride.get("into", ()))
    first_out = n_in + n + len(into)
    last = math.prod(grid) - 1

    def wrapped(*refs):
        ins, x_refs = refs[:n_in], refs[n_in:n_in + n]
        outs, out_refs = refs[first_out:first_out + n_out], refs[first_out + n_out:first_out + n_out + n]
        scratch, sems = refs[first_out + n_out + n:-3], refs[-3:]
        step = 0
        for axis, size in enumerate(grid):
            step = step * size + pl.program_id(axis)

        def exchange():
            return _direct_exchange(x_refs, out_refs, *sems, scatter=ride["scatter"], row0s=ride["row0s"])

        @pl.when(step == 0)
        def _():
            exchange()[0]()

        body(*ins, *outs, *scratch)

        @pl.when(step == last)
        def _():
            exchange()[1]()

    any_spec = pl.BlockSpec(memory_space=pl.ANY)
    landing = [jax.ShapeDtypeStruct(b.shape, b.dtype) for b in into] or list(ride["into_shapes"])
    return pl.pallas_call(
        wrapped, name=name, grid=grid, in_specs=list(in_specs) + [any_spec] * (n + len(into)),
        out_specs=tuple(out_specs) + (any_spec,) * n, out_shape=tuple(out_shape) + tuple(landing),
        input_output_aliases={n_in + n + a: n_out + a for a in range(len(into))},
        scratch_shapes=list(scratch_shapes) + _exchange_sems(n),
        compiler_params=_params(("arbitrary",) * len(grid)))(*operands, *ride["bufs"], *into)


def _attn_fwd(proj, gq, gk, ride=None):
    s = proj.shape[0]
    tq, tk, r = _attn_blocks(s)
    pw = 2 * HEAD_DIM

    def body(q_ref, k_ref, v_ref, gq_ref, gk_ref, o_ref, b_ref, kn_ref, vb_ref):
        qi = pl.program_id(1)

        @pl.when(qi == 0)
        def _():
            _fill_keys(k_ref, v_ref, gk_ref, kn_ref, vb_ref, tq)

        qn, _ = _pair_rms(q_ref[...], gq_ref[...])
        qn = qn * (HEAD_DIM ** -0.5)
        first = _first_head((tq, pw), 1)
        tri = _tri_and_ones(tk, lambda row, col: row > col)
        qs = [jnp.where(first, qn, 0.0).astype(BF16), jnp.where(first, 0.0, qn).astype(BF16)]

        def tiles(kq, carry, own):
            o_acc, accs = carry[0], list(carry[1:])
            ks = pl.ds(pl.multiple_of(kq * tq, tq), tq)
            kb, vb = kn_ref[ks, :], vb_ref[ks, :]
            scores = [_attn_scores(q, kb, own) for q in qs]
            sums = [[_split_dot(sc[1][:, u * tk:(u + 1) * tk], tri) for u in range(r)] for sc in scores]
            ws = []
            for h in range(2):
                later = [None] * r
                for u in reversed(range(r)):
                    later[u] = accs[h] + sums[h][u][:, :tk]
                    accs[h] = accs[h] + sums[h][u][:, tk:]
                w = jnp.exp(scores[h][0] + jnp.concatenate(later, axis=1))
                if own:
                    w = jnp.where(scores[h][2], w, 0.0)
                ws.append(w.astype(BF16))
            po = [jnp.dot(w, vb, preferred_element_type=F32) for w in ws]
            return (o_acc + jnp.where(first, po[0], po[1]), *accs)

        zero = jnp.zeros((tq, tk), F32)
        carry = tiles(qi, (jnp.zeros((tq, pw), F32), zero, zero), True)
        o_acc, acc_a, acc_b = lax.fori_loop(0, qi, lambda it, c: tiles(qi - 1 - it, c, False), carry)
        o_ref[...] = o_acc
        b_ref[0] = acc_a[:, 0:1]
        b_ref[1] = acc_b[:, 0:1]

    def cols(block0):
        return pl.BlockSpec((s, pw), lambda hp, i: (0, block0 + hp))

    gain = pl.BlockSpec((1, pw), lambda hp, i: (0, 0))
    return _call_with_ride_along(
        body, ride, name="attn_fwd", grid=(HEAD_PAIRS, s // tq), semantics=("parallel", "arbitrary"),
        in_specs=[pl.BlockSpec((tq, pw), lambda hp, i: (i, _Q_BLOCK0 + hp)), cols(_K_BLOCK0), cols(_V_BLOCK0),
                  gain, gain],
        out_specs=(pl.BlockSpec((tq, pw), lambda hp, i: (i, hp)), pl.BlockSpec((2, tq, 1), lambda hp, i: (hp, i, 0))),
        out_shape=(jax.ShapeDtypeStruct((s, ATTN_WIDTH), F32), jax.ShapeDtypeStruct((ATTN_HEADS, s, 1), F32)),
        scratch_shapes=[pltpu.VMEM((s, pw), BF16), pltpu.VMEM((s, pw), BF16)],
        operands=(proj, proj, proj, gq, gk))


def _attn_bwd(proj, gq, gk, bsum, do, ride=None):
    s = proj.shape[0]
    tq, tk, r = _attn_blocks(s)
    nq = s // tq
    pw = 2 * HEAD_DIM

    def body(q_ref, k_ref, v_ref, gq_ref, gk_ref, b_ref, do_ref,
             dq_ref, dk_ref, dv_ref, dgq_ref, dgk_ref, kn_ref, vb_ref, dkt_ref, dvt_ref):
        qi = pl.program_id(1)

        @pl.when(qi == 0)
        def _():
            _fill_keys(k_ref, v_ref, gk_ref, kn_ref, vb_ref, tq)
            for ref in (dkt_ref, dvt_ref, dgq_ref):
                ref[...] = jnp.zeros_like(ref)

        scale = HEAD_DIM ** -0.5
        q = q_ref[...]
        qn, rq = _pair_rms(q, gq_ref[...])
        qf = qn * scale
        qft = qf.T
        dof = do_ref[...]
        doft = dof.T
        first = _first_head((tq, pw), 1)
        first_t = _first_head((pw, tq), 0)
        tri_upto = _tri_and_ones(tk, lambda row, col: row <= col)
        tri_before = _tri_and_ones(tk, lambda row, col: row < col)
        heads = ((first, first_t), (~first, ~first_t))
        qs = [jnp.where(sel, qf, 0.0).astype(BF16) for sel, _ in heads]
        qst = [jnp.where(sel_t, qft, 0.0).astype(BF16) for _, sel_t in heads]
        dob = [jnp.where(sel, dof, 0.0).astype(BF16) for sel, _ in heads]
        dobt = [jnp.where(sel_t, doft, 0.0).astype(BF16) for _, sel_t in heads]

        def tiles(kq, carry, own):
            dq_acc = carry[0]
            rest, g_prefix = list(carry[1:3]), list(carry[3:5])
            ks = pl.ds(pl.multiple_of(kq * tq, tq), tq)
            kb, vb = kn_ref[ks, :], vb_ref[ks, :]
            scores = [_attn_scores(q, kb, own) for q in qs]
            dws = [lax.dot_general(d, vb, _NT, preferred_element_type=F32) for d in dob]
            sums = [[_split_dot(sc[1][:, u * tk:(u + 1) * tk], tri_upto) for u in range(r)] for sc in scores]
            ws, gs = [], []
            for h in range(2):
                later = [None] * r
                for u in range(r):
                    later[u] = rest[h] - sums[h][u][:, :tk]
                    rest[h] = rest[h] - sums[h][u][:, tk:]
                w = jnp.exp(scores[h][0] + jnp.concatenate(later, axis=1))
                if own:
                    w = jnp.where(scores[h][2], w, 0.0)
                ws.append(w)
                gs.append(w * dws[h])
            gsums = [[_split_dot(g[:, u * tk:(u + 1) * tk], tri_before) for u in range(r)] for g in gs]
            dzs = []
            for h in range(2):
                before = [None] * r
                for u in range(r):
                    before[u] = g_prefix[h] + gsums[h][u][:, :tk]
                    g_prefix[h] = g_prefix[h] + gsums[h][u][:, tk:]
                dz = gs[h] - jnp.exp(scores[h][0]) * (gs[h] + jnp.concatenate(before, axis=1))
                if own:
                    dz = jnp.where(scores[h][2], dz, 0.0)
                dzs.append(dz.astype(BF16))
            pq = [jnp.dot(dz, kb, preferred_element_type=F32) for dz in dzs]
            dkt_ref[:, ks] += (jnp.dot(qst[0], dzs[0], preferred_element_type=F32)
                               + jnp.dot(qst[1], dzs[1], preferred_element_type=F32))
            dvt_ref[:, ks] += (jnp.dot(dobt[0], ws[0].astype(BF16), preferred_element_type=F32)
                               + jnp.dot(dobt[1], ws[1].astype(BF16), preferred_element_type=F32))
            return (dq_acc + jnp.where(first, pq[0], pq[1]), *rest, *g_prefix)

        zero = jnp.zeros((tq, tk), F32)
        carry = (jnp.zeros((tq, pw), F32), jnp.broadcast_to(b_ref[0], (tq, tk)), jnp.broadcast_to(b_ref[1], (tq, tk)),
                 zero, zero)
        carry = lax.fori_loop(0, qi, lambda kq, c: tiles(kq, c, False), carry)
        dq, dgq = _pair_rms_bwd(q, gq_ref[...], rq, tiles(qi, carry, True)[0] * scale)
        dq_ref[...] = dq
        dgq_ref[0] += dgq

        @pl.when(qi == nq - 1)
        def _():
            def finish(c, dgk):
                rows = pl.ds(pl.multiple_of(c * tq, tq), tq)
                k = k_ref[rows, :]
                _, rk = _pair_rms(k, gk_ref[...])
                dk, dgk_c = _pair_rms_bwd(k, gk_ref[...], rk, dkt_ref[:, rows].T)
                dk_ref[rows, :] = dk
                dv_ref[rows, :] = dvt_ref[:, rows].T
                return dgk + dgk_c

            dgk_ref[0] = lax.fori_loop(0, nq, finish, jnp.zeros((1, pw), F32))

    def cols(block0):
        return pl.BlockSpec((s, pw), lambda hp, i: (0, block0 + hp))

    gain = pl.BlockSpec((1, pw), lambda hp, i: (0, 0))
    q_rows = pl.BlockSpec((tq, pw), lambda hp, i: (i, hp))
    all_rows = pl.BlockSpec((s, pw), lambda hp, i: (0, hp))
    pair_gain = pl.BlockSpec((1, 1, pw), lambda hp, i: (hp, 0, 0))
    wide = jax.ShapeDtypeStruct((s, ATTN_WIDTH), F32)
    gains = jax.ShapeDtypeStruct((HEAD_PAIRS, 1, pw), F32)
    return _call_with_ride_along(
        body, ride, name="attn_bwd", grid=(HEAD_PAIRS, nq), semantics=("parallel", "arbitrary"),
        in_specs=[pl.BlockSpec((tq, pw), lambda hp, i: (i, _Q_BLOCK0 + hp)), cols(_K_BLOCK0), cols(_V_BLOCK0),
                  gain, gain, pl.BlockSpec((2, tq, 1), lambda hp, i: (hp, i, 0)), q_rows],
        out_specs=(q_rows, all_rows, all_rows, pair_gain, pair_gain),
        out_shape=(wide, wide, wide, gains, gains),
        scratch_shapes=[pltpu.VMEM((s, pw), BF16), pltpu.VMEM((s, pw), BF16),
                        pltpu.VMEM((pw, s), F32), pltpu.VMEM((pw, s), F32)],
        operands=(proj, proj, proj, gq, gk, bsum, do))


def _reduce_adamw(parts, w, m, v, *, name, row0s=(0,)):
    n, _, cols = parts.shape
    rows = w.shape[0]
    seg_rows = rows // len(row0s)
    tile_rows = 2 * SUBLANES
    tr = max(t for t in range(tile_rows, min(seg_rows, 512) + 1, tile_rows)
             if seg_rows % t == 0 and all(r0 % t == 0 for r0 in row0s))
    seg_tiles = seg_rows // tr

    def parts_tile(i):
        tile = row0s[0] // tr + i
        for l in range(1, len(row0s)):
            tile = jnp.where(i >= l * seg_tiles, row0s[l] // tr + i - l * seg_tiles, tile)
        return tile

    c1 = 1.0 - ADAM_B1 ** ADAM_STEP
    c2 = 1.0 - ADAM_B2 ** ADAM_STEP

    def body(p_ref, w_ref, m_ref, v_ref, g_ref, d_ref, nm_ref, nv_ref):
        g = p_ref[0].astype(F32)
        for i in range(1, n):
            g = g + p_ref[i].astype(F32)
        nm = ADAM_B1 * m_ref[...] + (1.0 - ADAM_B1) * g
        nv = ADAM_B2 * v_ref[...] + (1.0 - ADAM_B2) * (g * g)
        g_ref[...] = g
        nm_ref[...] = nm
        nv_ref[...] = nv
        d_ref[...] = -ADAM_LR * ((nm / c1) / (jnp.sqrt(nv / c2) + ADAM_EPS) + ADAM_WD * w_ref[...])

    row = pl.BlockSpec((tr, cols), lambda i: (i, 0))
    out = jax.ShapeDtypeStruct((rows, cols), F32)
    return pl.pallas_call(
        body, name=name, grid=(rows // tr,),
        in_specs=[pl.BlockSpec((n, tr, cols), lambda i: (0, parts_tile(i), 0)), row, row, row],
        out_specs=(row,) * 4, out_shape=(out,) * 4, compiler_params=_params(("parallel",)))(parts, w, m, v)


def _all_gather(shards, *, name):
    n_arr = len(shards)

    def body(*refs):
        x_refs, out_refs = refs[:n_arr], refs[n_arr:2 * n_arr]
        send_sems, recv_sems, local_sems = refs[2 * n_arr:]
        x, y, c = lax.axis_index("x"), lax.axis_index("y"), lax.axis_index("c")
        me, sibling = (x, y, c), (x, y, 1 - c)
        chips = [(1 - x, y), (x, 1 - y), (1 - x, 1 - y)]

        def slot(a, px, py, pc):
            return out_refs[a].at[4 * px + 2 * py + pc]

        def copy(a, k, block, to, src=None):
            return pltpu.make_async_remote_copy(
                src_ref=slot(a, *block) if src is None else src, dst_ref=slot(a, *block),
                send_sem=send_sems.at[a, k], recv_sem=recv_sems.at[a, k], device_id=to, device_id_type=MESH_IDS)

        arrays = range(n_arr)
        mine = [pltpu.make_async_copy(x_refs[a], slot(a, *me), local_sems.at[a]) for a in arrays]
        for cp in mine:
            cp.start()
        first = [copy(a, 0, me, sibling, src=x_refs[a]) for a in arrays]
        first += [copy(a, 1 + j, me, (*chip, c), src=x_refs[a]) for j, chip in enumerate(chips) for a in arrays]
        for cp in first:
            cp.start()
        passed = []
        for j, chip in enumerate(chips):
            for a in arrays:
                copy(a, 1 + j, (*chip, c), me).wait_recv()
                passed.append(copy(a, 4 + j, (*chip, c), sibling))
                passed[-1].start()
        for a in arrays:
            copy(a, 0, sibling, me).wait_recv()
        for j, chip in enumerate(chips):
            for a in arrays:
                copy(a, 4 + j, (*chip, 1 - c), me).wait_recv()
        for cp in first + passed:
            cp.wait_send()
        for cp in mine:
            cp.wait()

    any_spec = pl.BlockSpec(memory_space=pl.ANY)
    return pl.pallas_call(
        body, name=name, out_shape=tuple(jax.ShapeDtypeStruct((N_DEV,) + s.shape, s.dtype) for s in shards),
        in_specs=[any_spec] * n_arr, out_specs=(any_spec,) * n_arr,
        scratch_shapes=[pltpu.SemaphoreType.DMA((n_arr, 7)), pltpu.SemaphoreType.DMA((n_arr, 7)),
                        pltpu.SemaphoreType.DMA((n_arr,))],
    )(*shards)


def _direct_exchange(x_refs, out_refs, send_sems, recv_sems, local_sems, *, scatter, row0s):
    arrays = range(len(x_refs))
    x, y, c = lax.axis_index("x"), lax.axis_index("y"), lax.axis_index("c")
    me = 4 * x + 2 * y + c

    def src(a, slot):
        return x_refs[a].at[slot] if scatter else x_refs[a]

    def landing(a, sender):
        return out_refs[a].at[sender, pl.ds(row0s[a], x_refs[a].shape[-2]), :]

    def local(a):
        return pltpu.make_async_copy(src(a, me), landing(a, me), local_sems.at[a])

    def copy(a, k, arriving):
        px, py, pc = x ^ (k >> 2), y ^ ((k >> 1) & 1), c ^ (k & 1)
        peer = 4 * px + 2 * py + pc
        return pltpu.make_async_remote_copy(
            src_ref=src(a, peer), dst_ref=landing(a, peer if arriving else me),
            send_sem=send_sems.at[a, k - 1], recv_sem=recv_sems.at[a, k - 1],
            device_id=(px, py, pc), device_id_type=MESH_IDS)

    def start():
        for a in arrays:
            local(a).start()
        for k in range(1, N_DEV):
            for a in arrays:
                copy(a, k, False).start()

    def wait():
        for k in range(1, N_DEV):
            for a in arrays:
                copy(a, k, True).wait_recv()
        for k in range(1, N_DEV):
            for a in arrays:
                copy(a, k, False).wait_send()
        for a in arrays:
            local(a).wait()

    return start, wait


def _exchange_sems(n_arr):
    return [pltpu.SemaphoreType.DMA((n_arr, N_DEV - 1)), pltpu.SemaphoreType.DMA((n_arr, N_DEV - 1)),
            pltpu.SemaphoreType.DMA((n_arr,))]


def _pack_small(vals):
    flat = jnp.concatenate([vals[n].reshape(-1) for n in SMALL])
    rows = -(-flat.shape[0] // (SUBLANES * PACK_COLS)) * SUBLANES
    return jnp.pad(flat, (0, rows * PACK_COLS - flat.shape[0])).reshape(rows, PACK_COLS)


def _unpack_small(flat, shapes):
    flat = flat.reshape(-1)
    out, off = {}, 0
    for n in SMALL:
        size = math.prod(shapes[n])
        out[n] = flat[off:off + size].reshape(shapes[n])
        off += size
    return out


def _pair_gain(g):
    return jnp.tile(g, 2)[None]


def _ssm_setup(sp):
    g, p = SSM_GROUPS, SSM_STATE
    a_re = sp["ssm_a_re"][:, None, :]
    a_im = sp["ssm_a_im"][:, None, :]
    log_dt = jnp.broadcast_to(sp["ssm_log_dt"][:, None, None], (g, 1, p))
    bt_re = sp["ssm_b_re"].transpose(0, 2, 1)
    bt_im = sp["ssm_b_im"].transpose(0, 2, 1)
    return a_re, a_im, log_dt, bt_re, bt_im


def _layer_fwd(h, p_l, w, sp, ride=None):
    hn = _rms_fwd(h, sp["mix_norm_g"][None], name="rms_mix")
    proj = _mm(hn, w["w_in"], name="mm_proj")

    pw_re, pw_im, bbt_re, bbt_im = _ssm_param_fwd(*_ssm_setup(sp))
    pw_re = pw_re.transpose(1, 0, 2).reshape(SUBLANES, SSM_LANES)
    pw_im = pw_im.transpose(1, 0, 2).reshape(SUBLANES, SSM_LANES)
    ssm_mats = (_block_diag(bbt_re).astype(BF16), _block_diag(bbt_im).astype(BF16),
                _block_diag(sp["ssm_c_re"]).astype(BF16), _block_diag(-sp["ssm_c_im"]).astype(BF16),
                pw_re, pw_im, sp["ssm_d"].reshape(1, SSM_WIDTH))
    x_re, x_im, y, z = _ssm_fwd(proj, *ssm_mats)
    zz = _mm(z, w["ssm_w_glu"], name="mm_glu")

    o, bsum, *landed = _attn_fwd(proj, _pair_gain(sp["q_norm_g"]), _pair_gain(sp["k_norm_g"]), ride)

    bglu = sp["ssm_b_glu"][None]
    ycat = _gate_fwd(zz, bglu, proj, o)
    h2, hn2 = _mm_residual_norm(ycat, w["w_out"], h, sp["ple_norm_g"][None], name="mm_out")
    pp = _mm(p_l, w["w_ple_proj"], name="mm_ple_proj")
    pgl, h3 = _mm_gated_add(hn2, w["w_ple_gate"], h2, pp, name="mm_ple_gate")
    saved = dict(h=h, hn=hn, proj=proj, ssm_mats=ssm_mats, x_re=x_re, x_im=x_im, y=y, z=z,
                 zz=zz, bsum=bsum, o=o, ycat=ycat, h2=h2, hn2=hn2,
                 pgl=pgl, pp=pp, p_l=p_l, bglu=bglu)
    return h3, saved, landed


def _layer_bwd(dh3, sv, w, sp, ship=None):
    g, hh, p = SSM_GROUPS, SSM_GROUP, SSM_STATE
    grads = {}
    ship = ship or (lambda stage, grads_so_far, parts: None)
    dgp, dpp = _ple_bwd(dh3, sv["pgl"], sv["pp"])
    grads["w_ple_gate"] = _mm(sv["hn2"], dgp, ta=True, out_dtype=BF16, name="mm_d_ple_gate")
    grads["w_ple_proj"] = _mm(sv["p_l"], dpp, ta=True, out_dtype=BF16, name="mm_d_ple_proj")
    dh2, dg2 = _mm_norm_grad(dgp, w["w_ple_gate"], sv["h2"], sp["ple_norm_g"][None], dh3, name="mm_dhn2")
    grads["ple_norm_g"] = dg2[0]
    grads["w_out"] = _mm(sv["ycat"], dh2, ta=True, out_dtype=BF16, name="mm_d_out")
    dyc = _mm(dh2, w["w_out"], tb=True, name="mm_dycat")
    dzz, dgs, do, dga, dbglu = _gate_bwd(dyc, sv["zz"], sv["bglu"], sv["proj"], sv["o"])
    grads["ssm_b_glu"] = dbglu[0]

    dq, dk, dv, dgq, dgk, *parts = _attn_bwd(sv["proj"], _pair_gain(sp["q_norm_g"]), _pair_gain(sp["k_norm_g"]),
                                             sv["bsum"], do, ship("attn", grads, None))
    grads["q_norm_g"] = jnp.sum(dgq.reshape(ATTN_HEADS, HEAD_DIM), axis=0)
    grads["k_norm_g"] = jnp.sum(dgk.reshape(ATTN_HEADS, HEAD_DIM), axis=0)

    grads["ssm_w_glu"] = _mm(sv["z"], dzz, ta=True, out_dtype=BF16, name="mm_d_glu")
    dz = _mm(dzz, w["ssm_w_glu"], tb=True, name="mm_dz")
    du, dbc_re, dbc_im, dcc_re, dcc_im, gab_re, gab_im, dd, *landed = _ssm_bwd(
        dz, sv["y"], sv["proj"], sv["x_re"], sv["x_im"], *sv["ssm_mats"], ride=ship("ssm", grads, parts))
    parts = parts[:len(parts) - len(landed)] + landed
    grads["ssm_d"] = dd.reshape(g, hh)
    grads["ssm_c_re"] = _block_diag_take(dcc_re, hh, p)
    grads["ssm_c_im"] = -_block_diag_take(dcc_im, hh, p)
    da_re, da_im, dldt, dbt_re, dbt_im = _ssm_param_bwd(
        *_ssm_setup(sp), gab_re.reshape(g, 1, p), gab_im.reshape(g, 1, p),
        _block_diag_take(dbc_re, hh, p), _block_diag_take(dbc_im, hh, p))
    grads["ssm_a_re"], grads["ssm_a_im"] = da_re[:, 0, :], da_im[:, 0, :]
    grads["ssm_log_dt"] = dldt[:, 0, 0]
    grads["ssm_b_re"], grads["ssm_b_im"] = dbt_re.transpose(0, 2, 1), dbt_im.transpose(0, 2, 1)

    dproj = jnp.concatenate([du, dgs, dq, dk, dv, dga], axis=1)
    grads["w_in"] = _mm(sv["hn"], dproj, ta=True, out_dtype=BF16, name="mm_d_in")
    dh, dg1, *landed = _mm_norm_grad(dproj, w["w_in"], sv["h"], sp["mix_norm_g"][None], dh2, name="mm_dhn",
                                     ride=ship("dhn", grads, parts))
    parts = landed + parts[len(landed):]
    grads["mix_norm_g"] = dg1[0]
    return dh, grads, parts


def _local_step(x, p, target, big, small, fetch=None, ship=None):
    h = x
    big = list(big)
    saved = []
    for l in range(DEPTH):
        sp = {n: small[n][l] for n in SMALL}
        h, sv, landed = _layer_fwd(h, p[l], big[l], sp, fetch[0] if fetch and l == 0 else None)
        if landed:
            big[DEPTH - 1] = fetch[1](landed)
        saved.append(sv)
    dh, loss_parts = _loss_head(h, target)
    grads = [None] * DEPTH
    shipped = []
    for l in reversed(range(DEPTH)):
        sp = {n: small[n][l] for n in SMALL}
        carry = None
        if ship and l == 0:
            def carry(stage, grads_so_far, parts):
                return ship(stage, grads[DEPTH - 1], grads_so_far, parts)
        dh, grads[l], landed = _layer_bwd(dh, saved[l], big[l], sp, carry)
        shipped = landed or shipped
    return jnp.sum(loss_parts), dh, grads, shipped


def kernel(x, p, mix_norm_g, w_in, ssm_a_re, ssm_a_im, ssm_log_dt, ssm_b_re, ssm_b_im, ssm_c_re, ssm_c_im, ssm_d, ssm_w_glu, ssm_b_glu, q_norm_g, k_norm_g, w_out, ple_norm_g, w_ple_gate, w_ple_proj, loss_target, m_mix_norm_g, m_w_in, m_ssm_a_re, m_ssm_a_im, m_ssm_log_dt, m_ssm_b_re, m_ssm_b_im, m_ssm_c_re, m_ssm_c_im, m_ssm_d, m_ssm_w_glu, m_ssm_b_glu, m_q_norm_g, m_k_norm_g, m_w_out, m_ple_norm_g, m_w_ple_gate, m_w_ple_proj, v_mix_norm_g, v_w_in, v_ssm_a_re, v_ssm_a_im, v_ssm_log_dt, v_ssm_b_re, v_ssm_b_im, v_ssm_c_re, v_ssm_c_im, v_ssm_d, v_ssm_w_glu, v_ssm_b_glu, v_q_norm_g, v_k_norm_g, v_w_out, v_ple_norm_g, v_w_ple_gate, v_w_ple_proj):
    given = dict(locals())
    wts = {n: given[n] for n in WEIGHTS}
    mom = {n: given["m_" + n] for n in WEIGHTS}
    var = {n: given["v_" + n] for n in WEIGHTS}

    def rows2d(a):
        return a.reshape(-1, a.shape[-1])

    assert DEPTH == 2
    groups = (("w_in",), ("ssm_w_glu", "w_ple_proj"), ("w_out", "w_ple_gate"))
    layer_rows = [sum(wts[n].shape[1] for n in names) for names in groups]
    widths = [wts[names[0]].shape[2] for names in groups]

    def shards_of(l):
        return [jnp.concatenate([wts[n][l].astype(BF16) for n in names], axis=0) for names in groups]

    def weights_of(gathered):
        full = {}
        for names, got in zip(groups, gathered):
            row0 = 0
            for n in names:
                r = wts[n].shape[1]
                blocks = got[:, row0:row0 + r, :]
                full[n] = blocks.reshape(N_DEV * r, -1) if n in ROW_SHARDED else blocks.transpose(1, 0, 2).reshape(r, -1)
                row0 += r
        return full

    def blocks_to_send(layer_grads, which):
        def blocks(n):
            g = layer_grads[n]
            if n in ROW_SHARDED:
                return g.reshape(N_DEV, -1, g.shape[1])
            return g.reshape(g.shape[0], N_DEV, -1).transpose(1, 0, 2)

        return [jnp.concatenate([blocks(n) for n in groups[gi]], axis=1) for gi in which]

    fetch = (dict(bufs=shards_of(1), scatter=False, row0s=[0] * len(groups),
                  into_shapes=[jax.ShapeDtypeStruct((N_DEV, r, c), BF16) for r, c in zip(layer_rows, widths)]),
             weights_of)

    def ship(stage, last_grads, first_grads, parts):
        if stage == "attn":
            return dict(bufs=blocks_to_send(last_grads, (0, 1, 2)), scatter=True, row0s=layer_rows, into_shapes=[
                jax.ShapeDtypeStruct((N_DEV, DEPTH * r, c), BF16) for r, c in zip(layer_rows, widths)])
        which = (1, 2) if stage == "ssm" else (0,)
        return dict(bufs=blocks_to_send(first_grads, which), scatter=True, row0s=[0] * len(which),
                    into=[parts[gi] for gi in which])

    first = weights_of(_all_gather(shards_of(0), name="gather_weights"))
    small = {n: wts[n] for n in SMALL}
    loss, dx, grads, parts = _local_step(x[0], p[:, 0], loss_target[0], [first, None], small, fetch, ship)
    loss = lax.psum(loss, ("x", "y", "c"))

    big_out = {}
    for names, part, rows in zip(groups, parts, layer_rows):
        row0 = 0
        for n in names:
            res = _reduce_adamw(part, rows2d(wts[n]), rows2d(mom[n]), rows2d(var[n]),
                                row0s=[l * rows + row0 for l in range(DEPTH)], name="adamw_" + n)
            big_out[n] = [a.reshape(wts[n].shape) for a in res]
            row0 += wts[n].shape[1]

    small_shapes = {n: wts[n].shape for n in SMALL}
    local_small = _pack_small({n: jnp.stack([grads[l][n] for l in range(DEPTH)]) for n in SMALL})
    (all_small,) = _all_gather([local_small], name="gather_small_grads")
    small_out = _reduce_adamw(all_small, _pack_small(small), _pack_small({n: mom[n] for n in SMALL}),
                              _pack_small({n: var[n] for n in SMALL}), name="adamw_replicated")
    small_out = [_unpack_small(a, small_shapes) for a in small_out]

    outs = [loss, dx[None]]
    for k in range(4):
        outs += [big_out[n][k] if n in BIG else small_out[k][n] for n in WEIGHTS]
    return tuple(outs)
```

```python
import math

import jax
import jax.numpy as jnp
from jax import lax
from jax.experimental import pallas as pl
from jax.experimental.pallas import tpu as pltpu

F32 = jnp.float32
BF16 = jnp.bfloat16
MESH_IDS = pl.DeviceIdType.MESH

N_DEV = 8
D_MODEL = 1024
DEPTH = 2
PLE_DIM = 256
SSM_WIDTH = 512
SSM_GROUP = 16
SSM_GROUPS = 32
SSM_STATE = 64
SSM_LANES = SSM_GROUPS * SSM_STATE
ATTN_WIDTH = 512
ATTN_HEADS = 8
HEAD_DIM = 64
RMS_EPS = 1e-6
ADAM_LR = 0.001
ADAM_B1 = 0.9
ADAM_B2 = 0.999
ADAM_EPS = 1e-08
ADAM_WD = 0.01
ADAM_STEP = 10

VMEM_LIMIT = 56 * 1024 * 1024
ATT_TQ = 512
ATT_TK = 128
SSM_BLOCK_GROUPS = 8
SSM_TIME_BLOCK = 512
SUBLANES = 8
PACK_COLS = 1024

BIG = ("w_in", "ssm_w_glu", "w_out", "w_ple_gate", "w_ple_proj")
ROW_SHARDED = ("w_out", "w_ple_gate")
SMALL = ("mix_norm_g", "ssm_a_re", "ssm_a_im", "ssm_log_dt", "ssm_b_re", "ssm_b_im", "ssm_c_re",
         "ssm_c_im", "ssm_d", "ssm_b_glu", "q_norm_g", "k_norm_g", "ple_norm_g")
WEIGHTS = ("mix_norm_g", "w_in", "ssm_a_re", "ssm_a_im", "ssm_log_dt", "ssm_b_re", "ssm_b_im",
           "ssm_c_re", "ssm_c_im", "ssm_d", "ssm_w_glu", "ssm_b_glu", "q_norm_g", "k_norm_g",
           "w_out", "ple_norm_g", "w_ple_gate", "w_ple_proj")
_NT = (((1,), (1,)), ((), ()))
_TN = (((0,), (0,)), ((), ()))


def _params(sem=None):
    return pltpu.CompilerParams(dimension_semantics=sem, vmem_limit_bytes=VMEM_LIMIT)


def _sigmoid(x):
    return 0.5 * jnp.tanh(0.5 * x) + 0.5


_GELU_K = math.sqrt(2.0 / math.pi)
_GELU_C = 0.044715


def _gelu(x):
    return 0.5 * x * (1.0 + jnp.tanh(_GELU_K * (x + _GELU_C * x * x * x)))


def _gelu_grad(x):
    th = jnp.tanh(_GELU_K * (x + _GELU_C * x * x * x))
    return 0.5 * (1.0 + th) + 0.5 * x * (1.0 - th * th) * _GELU_K * (1.0 + 3.0 * _GELU_C * x * x)


def _mm(a, b, *, name, ta=False, tb=False, add=None, out_dtype=F32, tm=1024, tn=1024, tk=512,
        finish=None, extras=(), outs=None, ride=None):
    m, k = (a.shape[1], a.shape[0]) if ta else a.shape
    n = b.shape[0] if tb else b.shape[1]
    tm, tn, tk = min(tm, m), min(tn, n), min(tk, k)
    assert m % tm == 0 and n % tn == 0 and k % tk == 0, (name, a.shape, b.shape)
    n_steps = k // tk
    dims = (((0 if ta else 1,), (1 if tb else 0,)), ((), ()))
    a_spec = (pl.BlockSpec((tk, tm), lambda i, j, kk: (kk, i)) if ta
              else pl.BlockSpec((tm, tk), lambda i, j, kk: (i, kk)))
    b_spec = (pl.BlockSpec((tn, tk), lambda i, j, kk: (j, kk)) if tb
              else pl.BlockSpec((tk, tn), lambda i, j, kk: (kk, j)))
    o_spec = pl.BlockSpec((tm, tn), lambda i, j, kk: (i, j))
    ins, in_specs = [a, b], [a_spec, b_spec]
    if finish is None:
        if add is not None:
            ins.append(add)
            in_specs.append(o_spec)

        def finish(res, extra_refs, out_refs):
            if add is not None:
                res = res + extra_refs[0][...].astype(F32)
            out_refs[0][...] = res.astype(out_dtype)

        out_specs, out_shapes = [o_spec], [jax.ShapeDtypeStruct((m, n), out_dtype)]
    else:
        assert tn == n and add is None
        for arr, kind in extras:
            ins.append(arr)
            cols = arr.shape[1]
            in_specs.append(pl.BlockSpec((tm, cols), lambda i, j, kk: (i, 0)) if kind == "rows"
                            else pl.BlockSpec((1, cols), lambda i, j, kk: (0, 0)))
        out_specs = [pl.BlockSpec((tm, cols), lambda i, j, kk: (i, 0)) if kind == "rows"
                     else pl.BlockSpec((1, cols), lambda i, j, kk: (0, 0)) for cols, _, kind in outs]
        out_shapes = [jax.ShapeDtypeStruct((m if kind == "rows" else 1, cols), dtype) for cols, dtype, kind in outs]
    n_in = len(ins)

    def body(*refs):
        a_ref, b_ref, acc_ref = refs[0], refs[1], refs[-1]
        kk = pl.program_id(2)

        @pl.when(kk == 0)
        def _():
            acc_ref[...] = jnp.zeros_like(acc_ref)

        acc_ref[...] += lax.dot_general(a_ref[...].astype(BF16), b_ref[...].astype(BF16), dims,
                                        preferred_element_type=F32)

        @pl.when(kk == n_steps - 1)
        def _():
            finish(acc_ref[...], refs[2:n_in], refs[n_in:-1])

    sums = outs is not None and any(kind == "sum" for _, _, kind in outs)
    res = _call_with_ride_along(
        body, ride, name=name, grid=(m // tm, n // tn, n_steps),
        semantics=("arbitrary",) * 3 if sums else ("parallel", "parallel", "arbitrary"),
        in_specs=in_specs, out_specs=tuple(out_specs), out_shape=tuple(out_shapes),
        scratch_shapes=[pltpu.VMEM((tm, tn), F32)], operands=ins)
    return res[0] if outs is None and ride is None else res


def _rms(hv, g):
    r = lax.rsqrt(jnp.mean(hv * hv, axis=-1, keepdims=True) + RMS_EPS)
    return hv * r * g, r


def _rms_grad(hv, g, dn):
    _, r = _rms(hv, g)
    a = dn * g
    dot = jnp.mean(a * hv, axis=-1, keepdims=True)
    return r * a - hv * (r * r * r * dot), jnp.sum(dn * hv * r, axis=0, keepdims=True)


def _mm_residual_norm(a, b, h, g, *, name):
    def finish(res, extra_refs, out_refs):
        h_ref, g_ref = extra_refs
        h2 = res + h_ref[...]
        out_refs[0][...] = h2
        out_refs[1][...] = _rms(h2, g_ref[...])[0].astype(BF16)

    d = b.shape[1]
    return _mm(a, b, name=name, finish=finish, extras=[(h, "rows"), (g, "vec")],
               outs=[(d, F32, "rows"), (d, BF16, "rows")])


def _mm_gated_add(a, b, h2, pp, *, name):
    def finish(res, extra_refs, out_refs):
        h_ref, p_ref = extra_refs
        out_refs[0][...] = res
        out_refs[1][...] = h_ref[...] + _sigmoid(res) * p_ref[...]

    d = b.shape[1]
    return _mm(a, b, name=name, finish=finish, extras=[(h2, "rows"), (pp, "rows")],
               outs=[(d, F32, "rows"), (d, F32, "rows")])


def _mm_norm_grad(dy, w, h, g, dres, *, name, ride=None):
    def finish(res, extra_refs, out_refs):
        h_ref, g_ref, r_ref = extra_refs
        dh, dg = _rms_grad(h_ref[...], g_ref[...], res)
        out_refs[0][...] = dh + r_ref[...]

        @pl.when(pl.program_id(0) == 0)
        def _():
            out_refs[1][...] = jnp.zeros_like(out_refs[1])

        out_refs[1][...] += dg

    d = w.shape[0]
    return _mm(dy, w, name=name, tb=True, finish=finish, extras=[(h, "rows"), (g, "vec"), (dres, "rows")],
               outs=[(d, F32, "rows"), (d, F32, "sum")], ride=ride)


def _rows_tile(rows, want=512):
    t = min(rows, want)
    assert rows % t == 0
    return t


def _rowwise(body, name, rows, ins, outs):
    widest = max(cols for _, cols, _ in ins)
    tm = _rows_tile(rows, 512 if widest > 128 else 4096)
    in_specs = []
    for _, cols, cb in ins:
        if cb is None:
            in_specs.append(pl.BlockSpec((1, cols), lambda i: (0, 0)))
        else:
            in_specs.append(pl.BlockSpec((tm, cols), lambda i, cb=cb: (i, cb)))
    out_specs, out_shapes = [], []
    for cols, dtype, is_acc in outs:
        if is_acc:
            out_specs.append(pl.BlockSpec((1, cols), lambda i: (0, 0)))
            out_shapes.append(jax.ShapeDtypeStruct((1, cols), dtype))
        else:
            out_specs.append(pl.BlockSpec((tm, cols), lambda i: (i, 0)))
            out_shapes.append(jax.ShapeDtypeStruct((rows, cols), dtype))
    any_acc = any(o[2] for o in outs)
    return pl.pallas_call(
        body, name=name, grid=(rows // tm,), in_specs=in_specs, out_specs=tuple(out_specs),
        out_shape=tuple(out_shapes),
        compiler_params=_params(("arbitrary",) if any_acc else ("parallel",)))(*[a for a, _, _ in ins])


def _accumulate(ref, value):
    @pl.when(pl.program_id(0) == 0)
    def _():
        ref[...] = jnp.zeros_like(ref)

    ref[...] += value


def _rms_fwd(h, g, *, name):
    rows, d = h.shape

    def body(h_ref, g_ref, o_ref):
        o_ref[...] = _rms(h_ref[...], g_ref[...])[0].astype(BF16)

    return _rowwise(body, name, rows, [(h, d, 0), (g, d, None)], [(d, BF16, False)])[0]


def _gate_fwd(zz, bglu, proj, o):
    rows = zz.shape[0]
    w = SSM_WIDTH

    def body(zz_ref, b_ref, gs_ref, o_ref, ga_ref, y_ref):
        zz_v = zz_ref[...] + b_ref[...]
        val, gate = zz_v[:, :w], zz_v[:, w:]
        gs, ga = gs_ref[...], ga_ref[...]
        y_ref[:, :w] = (val * _sigmoid(gate) * (gs * _sigmoid(gs))).astype(BF16)
        y_ref[:, w:] = (o_ref[...] * (ga * _sigmoid(ga))).astype(BF16)

    return _rowwise(body, "gate_fwd", rows,
                    [(zz, 2 * w, 0), (bglu, 2 * w, None), (proj, w, 1), (o, w, 0), (proj, w, 5)],
                    [(2 * w, BF16, False)])[0]


def _gate_bwd(dyc, zz, bglu, proj, o):
    rows = zz.shape[0]
    w = SSM_WIDTH

    def body(dy_ref, zz_ref, b_ref, gs_ref, o_ref, ga_ref, dzz_ref, dgs_ref, do_ref, dga_ref, db_ref):
        zz_v = zz_ref[...] + b_ref[...]
        val, gate = zz_v[:, :w], zz_v[:, w:]
        gs, ga = gs_ref[...], ga_ref[...]
        dys, dya = dy_ref[:, :w], dy_ref[:, w:]
        sg, ss, sa = _sigmoid(gate), _sigmoid(gs), _sigmoid(ga)
        glu = val * sg
        dglu = dys * (gs * ss)
        dgs_ref[...] = dys * glu * (ss * (1.0 + gs * (1.0 - ss)))
        dval = dglu * sg
        dgate = dglu * val * sg * (1.0 - sg)
        dzz_ref[:, :w] = dval.astype(BF16)
        dzz_ref[:, w:] = dgate.astype(BF16)
        do_ref[...] = dya * (ga * sa)
        dga_ref[...] = dya * o_ref[...] * (sa * (1.0 + ga * (1.0 - sa)))
        _accumulate(db_ref, jnp.concatenate([jnp.sum(dval, axis=0, keepdims=True),
                                             jnp.sum(dgate, axis=0, keepdims=True)], axis=1))

    return _rowwise(body, "gate_bwd", rows,
                    [(dyc, 2 * w, 0), (zz, 2 * w, 0), (bglu, 2 * w, None), (proj, w, 1), (o, w, 0), (proj, w, 5)],
                    [(2 * w, BF16, False), (w, F32, False), (w, F32, False), (w, F32, False), (2 * w, F32, True)])


def _ple_bwd(dh3, pgl, pp):
    rows, d = dh3.shape

    def body(dh_ref, g_ref, p_ref, dg_ref, dp_ref):
        sg = _sigmoid(g_ref[...])
        dh = dh_ref[...]
        dg_ref[...] = (dh * p_ref[...] * sg * (1.0 - sg)).astype(BF16)
        dp_ref[...] = (dh * sg).astype(BF16)

    return _rowwise(body, "ple_bwd", rows, [(dh3, d, 0), (pgl, d, 0), (pp, d, 0)],
                    [(d, BF16, False), (d, BF16, False)])


def _loss_head(h, target):
    rows, d = h.shape

    def body(h_ref, t_ref, dh_ref, l_ref):
        err = h_ref[...] - t_ref[...]
        dh_ref[...] = err * (1.0 / d)
        _accumulate(l_ref, jnp.sum(err * err, axis=0, keepdims=True) * (0.5 / d))

    return _rowwise(body, "loss_head", rows, [(h, d, 0), (target, d, 0)], [(d, F32, False), (d, F32, True)])


def _zoh(lr, li, dt):
    mag = jnp.exp(lr * dt)
    ab_re = mag * jnp.cos(li * dt)
    ab_im = mag * jnp.sin(li * dt)
    num_re = ab_re - 1.0
    den = lr * lr + li * li
    f_re = (num_re * lr + ab_im * li) / den
    f_im = (ab_im * lr - num_re * li) / den
    return ab_re, ab_im, f_re, f_im, den


def _ssm_param_fwd(a_re, a_im, log_dt, bt_re, bt_im):
    g, _, p = a_re.shape
    h = bt_re.shape[1]

    def body(lr_ref, li_ref, ldt_ref, br_ref, bi_ref, pr_ref, pi_ref, bbr_ref, bbi_ref):
        dt = jnp.exp(ldt_ref[...])
        ab_re, ab_im, f_re, f_im, _ = _zoh(lr_ref[...], li_ref[...], dt)
        cr, ci = ab_re, ab_im
        for k in range(SUBLANES):
            pr_ref[:, k:k + 1, :] = cr
            pi_ref[:, k:k + 1, :] = ci
            cr, ci = cr * ab_re - ci * ab_im, cr * ab_im + ci * ab_re
        br, bi = br_ref[...], bi_ref[...]
        bbr_ref[...] = f_re * br - f_im * bi
        bbi_ref[...] = f_re * bi + f_im * br

    pw = jax.ShapeDtypeStruct((g, SUBLANES, p), F32)
    large = jax.ShapeDtypeStruct((g, h, p), F32)
    return pl.pallas_call(body, name="ssm_param_fwd", out_shape=(pw, pw, large, large),
                          compiler_params=_params())(a_re, a_im, log_dt, bt_re, bt_im)


def _ssm_param_bwd(a_re, a_im, log_dt, bt_re, bt_im, gab_re, gab_im, gbb_re, gbb_im):
    g, _, p = a_re.shape
    h = bt_re.shape[1]

    def body(lr_ref, li_ref, ldt_ref, br_ref, bi_ref, gar_ref, gai_ref, gbr_ref, gbi_ref,
             dlr_ref, dli_ref, dldt_ref, dbr_ref, dbi_ref):
        lr, li = lr_ref[...], li_ref[...]
        dt = jnp.exp(ldt_ref[...])
        ab_re, ab_im, f_re, f_im, den = _zoh(lr, li, dt)
        br, bi = br_ref[...], bi_ref[...]
        gbr, gbi = gbr_ref[...], gbi_ref[...]
        dbr_ref[...] = f_re * gbr + f_im * gbi
        dbi_ref[...] = f_re * gbi - f_im * gbr
        gf_re = jnp.sum(br * gbr + bi * gbi, axis=1, keepdims=True)
        gf_im = jnp.sum(br * gbi - bi * gbr, axis=1, keepdims=True)
        il_re, il_im = lr / den, -li / den
        ga_re = gar_ref[...] + il_re * gf_re + il_im * gf_im
        ga_im = gai_ref[...] + il_re * gf_im - il_im * gf_re
        q_re = f_re * il_re - f_im * il_im
        q_im = f_re * il_im + f_im * il_re
        gl_re = -(q_re * gf_re + q_im * gf_im)
        gl_im = -(q_re * gf_im - q_im * gf_re)
        dlr_ref[...] = gl_re + dt * (ab_re * ga_re + ab_im * ga_im)
        dli_ref[...] = gl_im + dt * (ab_re * ga_im - ab_im * ga_re)
        la_re = lr * ab_re - li * ab_im
        la_im = lr * ab_im + li * ab_re
        dldt_ref[...] = jnp.sum((la_re * ga_re + la_im * ga_im) * dt, axis=2, keepdims=True)

    small = jax.ShapeDtypeStruct((g, 1, p), F32)
    one = jax.ShapeDtypeStruct((g, 1, 1), F32)
    large = jax.ShapeDtypeStruct((g, h, p), F32)
    return pl.pallas_call(body, name="ssm_param_bwd", out_shape=(small, small, one, large, large),
                          compiler_params=_params())(
                              a_re, a_im, log_dt, bt_re, bt_im, gab_re, gab_im, gbb_re, gbb_im)


def _block_diag(m):
    g, h, p = m.shape
    nb = g // SSM_BLOCK_GROUPS
    eye = jnp.eye(SSM_BLOCK_GROUPS, dtype=m.dtype)
    m4 = m.reshape(nb, SSM_BLOCK_GROUPS, h, p)
    return (m4[:, :, :, None, :] * eye[None, :, None, :, None]).reshape(nb, SSM_BLOCK_GROUPS * h, SSM_BLOCK_GROUPS * p)


def _block_diag_take(m, h, p):
    nb = m.shape[0]
    eye = jnp.eye(SSM_BLOCK_GROUPS, dtype=m.dtype)
    m5 = m.reshape(nb, SSM_BLOCK_GROUPS, h, SSM_BLOCK_GROUPS, p)
    return jnp.sum(m5 * eye[None, :, None, :, None], axis=3).reshape(nb * SSM_BLOCK_GROUPS, h, p)


def _scan_tile(br, bi, pw_re, pw_im, reverse):
    row = lax.broadcasted_iota(jnp.int32, br.shape, 0)
    xr, xi = br, bi
    for d in (1, 2, 4):
        ar, ai = pw_re[d - 1:d, :], pw_im[d - 1:d, :]
        if reverse:
            keep = row < SUBLANES - d
            sr = jnp.where(keep, pltpu.roll(xr, SUBLANES - d, 0), 0.0)
            si = jnp.where(keep, pltpu.roll(xi, SUBLANES - d, 0), 0.0)
        else:
            keep = row >= d
            sr = jnp.where(keep, pltpu.roll(xr, d, 0), 0.0)
            si = jnp.where(keep, pltpu.roll(xi, d, 0), 0.0)
        xr, xi = xr + ar * sr - ai * si, xi + ar * si + ai * sr
    return xr, xi


def _ssm_blocks(s):
    tt = min(SSM_TIME_BLOCK, s)
    assert s % tt == 0 and tt % SUBLANES == 0
    ch = SSM_BLOCK_GROUPS * SSM_GROUP
    st = SSM_BLOCK_GROUPS * SSM_STATE
    return tt, s // tt, SSM_WIDTH // ch, ch, st


def _ssm_fwd(proj, bc_re, bc_im, cc_re, cc_im, pw_re, pw_im, dvec):
    s = proj.shape[0]
    tt, nt, nb, ch, st = _ssm_blocks(s)
    n_tiles = tt // SUBLANES

    def body(u_ref, bre_ref, bim_ref, cre_ref, cim_ref, pr_ref, pi_ref, d_ref,
             xr_ref, xi_ref, y_ref, z_ref, carry_ref):
        @pl.when(pl.program_id(1) == 0)
        def _():
            carry_ref[...] = jnp.zeros_like(carry_ref)

        u = u_ref[...]
        ub = u.astype(BF16)
        xr_ref[...] = jnp.dot(ub, bre_ref[0], preferred_element_type=F32)
        xi_ref[...] = jnp.dot(ub, bim_ref[0], preferred_element_type=F32)
        pw_r, pw_i = pr_ref[...], pi_ref[...]

        def step(t, carry):
            cr, ci = carry
            rows = pl.ds(pl.multiple_of(t * SUBLANES, SUBLANES), SUBLANES)
            xr, xi = _scan_tile(xr_ref[rows, :], xi_ref[rows, :], pw_r, pw_i, False)
            xr, xi = xr + pw_r * cr - pw_i * ci, xi + pw_r * ci + pw_i * cr
            xr_ref[rows, :] = xr
            xi_ref[rows, :] = xi
            return (jnp.broadcast_to(xr[SUBLANES - 1:SUBLANES, :], (SUBLANES, st)),
                    jnp.broadcast_to(xi[SUBLANES - 1:SUBLANES, :], (SUBLANES, st)))

        cr, ci = lax.fori_loop(0, n_tiles, step, (carry_ref[0], carry_ref[1]), unroll=2)
        carry_ref[0] = cr
        carry_ref[1] = ci
        y = (lax.dot_general(xr_ref[...].astype(BF16), cre_ref[0], _NT, preferred_element_type=F32)
             + lax.dot_general(xi_ref[...].astype(BF16), cim_ref[0], _NT, preferred_element_type=F32)
             + d_ref[...] * u)
        y_ref[...] = y
        z_ref[...] = _gelu(y).astype(BF16)

    chan = pl.BlockSpec((tt, ch), lambda b, t: (t, b))
    state = pl.BlockSpec((tt, st), lambda b, t: (t, b))
    mat = pl.BlockSpec((1, ch, st), lambda b, t: (b, 0, 0))
    pw = pl.BlockSpec((SUBLANES, st), lambda b, t: (0, b))
    vec = pl.BlockSpec((1, ch), lambda b, t: (0, b))
    x_shape = jax.ShapeDtypeStruct((s, nb * st), F32)
    return pl.pallas_call(
        body, name="ssm_fwd", grid=(nb, nt), in_specs=[chan, mat, mat, mat, mat, pw, pw, vec],
        out_specs=(state, state, chan, chan),
        out_shape=(x_shape, x_shape, jax.ShapeDtypeStruct((s, nb * ch), F32), jax.ShapeDtypeStruct((s, nb * ch), BF16)),
        scratch_shapes=[pltpu.VMEM((2, SUBLANES, st), F32)],
        compiler_params=_params(("parallel", "arbitrary")))(proj, bc_re, bc_im, cc_re, cc_im, pw_re, pw_im, dvec)


def _ssm_bwd(dz, y, proj, x_re, x_im, bc_re, bc_im, cc_re, cc_im, pw_re, pw_im, dvec, ride=None):
    s = proj.shape[0]
    tt, nt, nb, ch, st = _ssm_blocks(s)
    n_tiles = tt // SUBLANES

    def body(dz_ref, y_ref, u_ref, xr_ref, xi_ref, bre_ref, bim_ref, cre_ref, cim_ref, pr_ref, pi_ref, d_ref,
             du_ref, dbr_ref, dbi_ref, dcr_ref, dci_ref, gar_ref, gai_ref, dd_ref,
             lr_ref, li_ref, carry_ref, acc_ref):
        @pl.when(pl.program_id(1) == 0)
        def _():
            for ref in (carry_ref, acc_ref, dbr_ref, dbi_ref, dcr_ref, dci_ref, dd_ref):
                ref[...] = jnp.zeros_like(ref)

        u = u_ref[...]
        dy = dz_ref[...] * _gelu_grad(y_ref[...])
        ub, dyb = u.astype(BF16), dy.astype(BF16)
        lr_ref[...] = jnp.dot(dyb, cre_ref[0], preferred_element_type=F32)
        li_ref[...] = jnp.dot(dyb, cim_ref[0], preferred_element_type=F32)
        pw_r, pw_i = pr_ref[...], -pi_ref[...]
        pwc_r = jnp.concatenate([pw_r[SUBLANES - 1 - i:SUBLANES - i, :] for i in range(SUBLANES)], axis=0)
        pwc_i = jnp.concatenate([pw_i[SUBLANES - 1 - i:SUBLANES - i, :] for i in range(SUBLANES)], axis=0)
        row = lax.broadcasted_iota(jnp.int32, (SUBLANES, st), 0)

        def step(t, carry):
            cr, ci, acc_r, acc_i = carry
            rows = pl.ds(pl.multiple_of((n_tiles - 1 - t) * SUBLANES, SUBLANES), SUBLANES)
            lr, li = _scan_tile(lr_ref[rows, :], li_ref[rows, :], pw_r, pw_i, True)
            lr, li = lr + pwc_r * cr - pwc_i * ci, li + pwc_r * ci + pwc_i * cr
            lr_ref[rows, :] = lr
            li_ref[rows, :] = li
            nr = jnp.where(row < SUBLANES - 1, pltpu.roll(lr, SUBLANES - 1, 0), cr)
            ni = jnp.where(row < SUBLANES - 1, pltpu.roll(li, SUBLANES - 1, 0), ci)
            xr, xi = xr_ref[rows, :], xi_ref[rows, :]
            return (jnp.broadcast_to(lr[0:1, :], (SUBLANES, st)), jnp.broadcast_to(li[0:1, :], (SUBLANES, st)),
                    acc_r + xr * nr + xi * ni, acc_i + xr * ni - xi * nr)

        cr, ci, acc_r, acc_i = lax.fori_loop(
            0, n_tiles, step, (carry_ref[0], carry_ref[1], acc_ref[0], acc_ref[1]), unroll=2)
        carry_ref[0], carry_ref[1] = cr, ci
        acc_ref[0], acc_ref[1] = acc_r, acc_i
        lrb, lib = lr_ref[...].astype(BF16), li_ref[...].astype(BF16)
        du_ref[...] = (lax.dot_general(lrb, bre_ref[0], _NT, preferred_element_type=F32)
                       + lax.dot_general(lib, bim_ref[0], _NT, preferred_element_type=F32) + dy * d_ref[...])
        dbr_ref[0] += lax.dot_general(ub, lrb, _TN, preferred_element_type=F32)
        dbi_ref[0] += lax.dot_general(ub, lib, _TN, preferred_element_type=F32)
        dcr_ref[0] += lax.dot_general(dyb, xr_ref[...].astype(BF16), _TN, preferred_element_type=F32)
        dci_ref[0] += lax.dot_general(dyb, xi_ref[...].astype(BF16), _TN, preferred_element_type=F32)
        dd_ref[...] += jnp.sum(dy * u, axis=0, keepdims=True)

        @pl.when(pl.program_id(1) == nt - 1)
        def _():
            gar_ref[...] = jnp.sum(acc_r, axis=0, keepdims=True)
            gai_ref[...] = jnp.sum(acc_i, axis=0, keepdims=True)

    chan = pl.BlockSpec((tt, ch), lambda b, t: (nt - 1 - t, b))
    state = pl.BlockSpec((tt, st), lambda b, t: (nt - 1 - t, b))
    mat = pl.BlockSpec((1, ch, st), lambda b, t: (b, 0, 0))
    pw = pl.BlockSpec((SUBLANES, st), lambda b, t: (0, b))
    vec = pl.BlockSpec((1, ch), lambda b, t: (0, b))
    svec = pl.BlockSpec((1, st), lambda b, t: (0, b))
    mat_shape = jax.ShapeDtypeStruct((nb, ch, st), F32)
    return _call_with_ride_along(
        body, ride, name="ssm_bwd", grid=(nb, nt), semantics=("parallel", "arbitrary"),
        in_specs=[chan, chan, chan, state, state, mat, mat, mat, mat, pw, pw, vec],
        out_specs=(chan, mat, mat, mat, mat, svec, svec, vec),
        out_shape=(jax.ShapeDtypeStruct((s, nb * ch), F32), mat_shape, mat_shape, mat_shape, mat_shape,
                   jax.ShapeDtypeStruct((1, nb * st), F32), jax.ShapeDtypeStruct((1, nb * st), F32),
                   jax.ShapeDtypeStruct((1, nb * ch), F32)),
        scratch_shapes=[pltpu.VMEM((tt, st), F32), pltpu.VMEM((tt, st), F32),
                        pltpu.VMEM((2, SUBLANES, st), F32), pltpu.VMEM((2, SUBLANES, st), F32)],
        operands=(dz, y, proj, x_re, x_im, bc_re, bc_im, cc_re, cc_im, pw_re, pw_im, dvec))


def _split_dot(x, tri2):
    hi = x.astype(BF16)
    lo = (x - hi.astype(F32)).astype(BF16)
    return jnp.dot(jnp.concatenate([hi, lo], axis=1), tri2, preferred_element_type=F32)


def _tri_and_ones(tk, rel):
    r_i = lax.broadcasted_iota(jnp.int32, (2 * tk, 2 * tk), 0) % tk
    c_i = lax.broadcasted_iota(jnp.int32, (2 * tk, 2 * tk), 1)
    return jnp.where((c_i >= tk) | rel(r_i, c_i), 1.0, 0.0).astype(BF16)


def _attn_scores(qs, kb, own):
    z = lax.dot_general(qs, kb, _NT, preferred_element_type=F32)
    log_beta = jnp.minimum(z, 0.0) - jnp.log(1.0 + jnp.exp(-jnp.abs(z)))
    log_stay = log_beta - z
    if not own:
        return log_beta, log_stay, None
    mask = lax.broadcasted_iota(jnp.int32, z.shape, 1) < lax.broadcasted_iota(jnp.int32, z.shape, 0)
    return log_beta, jnp.where(mask, log_stay, 0.0), mask


def _attn_blocks(s):
    tq, tk = min(ATT_TQ, s), min(ATT_TK, s)
    assert s % tq == 0 and tq % tk == 0
    return tq, tk, tq // tk


def _first_head(shape, axis):
    return lax.broadcasted_iota(jnp.int32, shape, axis) < HEAD_DIM


def _pair_rms(x, g):
    first = _first_head(x.shape, 1)
    sq = x * x
    ms_a = jnp.sum(jnp.where(first, sq, 0.0), axis=1, keepdims=True) * (1.0 / HEAD_DIM)
    ms_b = jnp.sum(jnp.where(first, 0.0, sq), axis=1, keepdims=True) * (1.0 / HEAD_DIM)
    r = jnp.where(first, lax.rsqrt(ms_a + RMS_EPS), lax.rsqrt(ms_b + RMS_EPS))
    return x * r * g, r


def _pair_rms_bwd(x, g, r, dn):
    first = _first_head(x.shape, 1)
    a = dn * g
    ax = a * x
    dot_a = jnp.sum(jnp.where(first, ax, 0.0), axis=1, keepdims=True) * (1.0 / HEAD_DIM)
    dot_b = jnp.sum(jnp.where(first, 0.0, ax), axis=1, keepdims=True) * (1.0 / HEAD_DIM)
    dx = r * a - x * (r * r * r * jnp.where(first, dot_a, dot_b))
    return dx, jnp.sum(dn * x * r, axis=0, keepdims=True)


_Q_BLOCK0 = 2 * SSM_WIDTH // (2 * HEAD_DIM)
_K_BLOCK0 = _Q_BLOCK0 + ATTN_WIDTH // (2 * HEAD_DIM)
_V_BLOCK0 = _K_BLOCK0 + ATTN_WIDTH // (2 * HEAD_DIM)
HEAD_PAIRS = ATTN_HEADS // 2


def _fill_keys(k_ref, v_ref, gk_ref, kn_ref, vb_ref, rows_per_step):
    def fill(c, _):
        rows = pl.ds(pl.multiple_of(c * rows_per_step, rows_per_step), rows_per_step)
        kn, _ = _pair_rms(k_ref[rows, :], gk_ref[...])
        kn_ref[rows, :] = kn.astype(BF16)
        vb_ref[rows, :] = v_ref[rows, :].astype(BF16)
        return 0

    lax.fori_loop(0, k_ref.shape[0] // rows_per_step, fill, 0)


def _call_with_ride_along(body, ride, *, name, grid, semantics, in_specs, out_specs, out_shape, scratch_shapes,
                          operands):
    if ride is None:
        return pl.pallas_call(body, name=name, grid=grid, in_specs=in_specs, out_specs=out_specs, out_shape=out_shape,
                              scratch_shapes=scratch_shapes, compiler_params=_params(semantics))(*operands)
    n, n_in, n_out = len(ride["bufs"]), len(in_specs), len(out_specs)
    into = list(ride.get("into", ()))
    first_out = n_in + n + len(into)
    last = math.prod(grid) - 1

    def wrapped(*refs):
        ins, x_refs = refs[:n_in], refs[n_in:n_in + n]
        outs, out_refs = refs[first_out:first_out + n_out], refs[first_out + n_out:first_out + n_out + n]
        scratch, sems = refs[first_out + n_out + n:-3], refs[-3:]
        step = 0
        for axis, size in enumerate(grid):
            step = step * size + pl.program_id(axis)

        def exchange():
            return _direct_exchange(x_refs, out_refs, *sems, scatter=ride["scatter"], row0s=ride["row0s"])

        @pl.when(step == 0)
        def _():
            exchange()[0]()

        body(*ins, *outs, *scratch)

        @pl.when(step == last)
        def _():
            exchange()[1]()

    any_spec = pl.BlockSpec(memory_space=pl.ANY)
    landing = [jax.ShapeDtypeStruct(b.shape, b.dtype) for b in into] or list(ride["into_shapes"])
    return pl.pallas_call(
        wrapped, name=name, grid=grid, in_specs=list(in_specs) + [any_spec] * (n + len(into)),
        out_specs=tuple(out_specs) + (any_spec,) * n, out_shape=tuple(out_shape) + tuple(landing),
        input_output_aliases={n_in + n + a: n_out + a for a in range(len(into))},
        scratch_shapes=list(scratch_shapes) + _exchange_sems(n),
        compiler_params=_params(("arbitrary",) * len(grid)))(*operands, *ride["bufs"], *into)


def _attn_fwd(proj, gq, gk, ride=None):
    s = proj.shape[0]
    tq, tk, r = _attn_blocks(s)
    pw = 2 * HEAD_DIM

    def body(q_ref, k_ref, v_ref, gq_ref, gk_ref, o_ref, b_ref, kn_ref, vb_ref):
        qi = pl.program_id(1)

        @pl.when(qi == 0)
        def _():
            _fill_keys(k_ref, v_ref, gk_ref, kn_ref, vb_ref, tq)

        qn, _ = _pair_rms(q_ref[...], gq_ref[...])
        qn = qn * (HEAD_DIM ** -0.5)
        first = _first_head((tq, pw), 1)
        tri = _tri_and_ones(tk, lambda row, col: row > col)
        qs = [jnp.where(first, qn, 0.0).astype(BF16), jnp.where(first, 0.0, qn).astype(BF16)]

        def tiles(kq, carry, own):
            o_acc, accs = carry[0], list(carry[1:])
            ks = pl.ds(pl.multiple_of(kq * tq, tq), tq)
            kb, vb = kn_ref[ks, :], vb_ref[ks, :]
            scores = [_attn_scores(q, kb, own) for q in qs]
            sums = [[_split_dot(sc[1][:, u * tk:(u + 1) * tk], tri) for u in range(r)] for sc in scores]
            ws = []
            for h in range(2):
                later = [None] * r
                for u in reversed(range(r)):
                    later[u] = accs[h] + sums[h][u][:, :tk]
                    accs[h] = accs[h] + sums[h][u][:, tk:]
                w = jnp.exp(scores[h][0] + jnp.concatenate(later, axis=1))
                if own:
                    w = jnp.where(scores[h][2], w, 0.0)
                ws.append(w.astype(BF16))
            po = [jnp.dot(w, vb, preferred_element_type=F32) for w in ws]
            return (o_acc + jnp.where(first, po[0], po[1]), *accs)

        zero = jnp.zeros((tq, tk), F32)
        carry = tiles(qi, (jnp.zeros((tq, pw), F32), zero, zero), True)
        o_acc, acc_a, acc_b = lax.fori_loop(0, qi, lambda it, c: tiles(qi - 1 - it, c, False), carry)
        o_ref[...] = o_acc
        b_ref[0] = acc_a[:, 0:1]
        b_ref[1] = acc_b[:, 0:1]

    def cols(block0):
        return pl.BlockSpec((s, pw), lambda hp, i: (0, block0 + hp))

    gain = pl.BlockSpec((1, pw), lambda hp, i: (0, 0))
    return _call_with_ride_along(
        body, ride, name="attn_fwd", grid=(HEAD_PAIRS, s // tq), semantics=("parallel", "arbitrary"),
        in_specs=[pl.BlockSpec((tq, pw), lambda hp, i: (i, _Q_BLOCK0 + hp)), cols(_K_BLOCK0), cols(_V_BLOCK0),
                  gain, gain],
        out_specs=(pl.BlockSpec((tq, pw), lambda hp, i: (i, hp)), pl.BlockSpec((2, tq, 1), lambda hp, i: (hp, i, 0))),
        out_shape=(jax.ShapeDtypeStruct((s, ATTN_WIDTH), F32), jax.ShapeDtypeStruct((ATTN_HEADS, s, 1), F32)),
        scratch_shapes=[pltpu.VMEM((s, pw), BF16), pltpu.VMEM((s, pw), BF16)],
        operands=(proj, proj, proj, gq, gk))


def _attn_bwd(proj, gq, gk, bsum, do, ride=None):
    s = proj.shape[0]
    tq, tk, r = _attn_blocks(s)
    nq = s // tq
    pw = 2 * HEAD_DIM

    def body(q_ref, k_ref, v_ref, gq_ref, gk_ref, b_ref, do_ref,
             dq_ref, dk_ref, dv_ref, dgq_ref, dgk_ref, kn_ref, vb_ref, dkt_ref, dvt_ref):
        qi = pl.program_id(1)

        @pl.when(qi == 0)
        def _():
            _fill_keys(k_ref, v_ref, gk_ref, kn_ref, vb_ref, tq)
            for ref in (dkt_ref, dvt_ref, dgq_ref):
                ref[...] = jnp.zeros_like(ref)

        scale = HEAD_DIM ** -0.5
        q = q_ref[...]
        qn, rq = _pair_rms(q, gq_ref[...])
        qf = qn * scale
        qft = qf.T
        dof = do_ref[...]
        doft = dof.T
        first = _first_head((tq, pw), 1)
        first_t = _first_head((pw, tq), 0)
        tri_upto = _tri_and_ones(tk, lambda row, col: row <= col)
        tri_before = _tri_and_ones(tk, lambda row, col: row < col)
        heads = ((first, first_t), (~first, ~first_t))
        qs = [jnp.where(sel, qf, 0.0).astype(BF16) for sel, _ in heads]
        qst = [jnp.where(sel_t, qft, 0.0).astype(BF16) for _, sel_t in heads]
        dob = [jnp.where(sel, dof, 0.0).astype(BF16) for sel, _ in heads]
        dobt = [jnp.where(sel_t, doft, 0.0).astype(BF16) for _, sel_t in heads]

        def tiles(kq, carry, own):
            dq_acc = carry[0]
            rest, g_prefix = list(carry[1:3]), list(carry[3:5])
            ks = pl.ds(pl.multiple_of(kq * tq, tq), tq)
            kb, vb = kn_ref[ks, :], vb_ref[ks, :]
            scores = [_attn_scores(q, kb, own) for q in qs]
            dws = [lax.dot_general(d, vb, _NT, preferred_element_type=F32) for d in dob]
            sums = [[_split_dot(sc[1][:, u * tk:(u + 1) * tk], tri_upto) for u in range(r)] for sc in scores]
            ws, gs = [], []
            for h in range(2):
                later = [None] * r
                for u in range(r):
                    later[u] = rest[h] - sums[h][u][:, :tk]
                    rest[h] = rest[h] - sums[h][u][:, tk:]
                w = jnp.exp(scores[h][0] + jnp.concatenate(later, axis=1))
                if own:
                    w = jnp.where(scores[h][2], w, 0.0)
                ws.append(w)
                gs.append(w * dws[h])
            gsums = [[_split_dot(g[:, u * tk:(u + 1) * tk], tri_before) for u in range(r)] for g in gs]
            dzs = []
            for h in range(2):
                before = [None] * r
                for u in range(r):
                    before[u] = g_prefix[h] + gsums[h][u][:, :tk]
                    g_prefix[h] = g_prefix[h] + gsums[h][u][:, tk:]
                dz = gs[h] - jnp.exp(scores[h][0]) * (gs[h] + jnp.concatenate(before, axis=1))
                if own:
                    dz = jnp.where(scores[h][2], dz, 0.0)
                dzs.append(dz.astype(BF16))
            pq = [jnp.dot(dz, kb, preferred_element_type=F32) for dz in dzs]
            dkt_ref[:, ks] += (jnp.dot(qst[0], dzs[0], preferred_element_type=F32)
                               + jnp.dot(qst[1], dzs[1], preferred_element_type=F32))
            dvt_ref[:, ks] += (jnp.dot(dobt[0], ws[0].astype(BF16), preferred_element_type=F32)
                               + jnp.dot(dobt[1], ws[1].astype(BF16), preferred_element_type=F32))
            return (dq_acc + jnp.where(first, pq[0], pq[1]), *rest, *g_prefix)

        zero = jnp.zeros((tq, tk), F32)
        carry = (jnp.zeros((tq, pw), F32), jnp.broadcast_to(b_ref[0], (tq, tk)), jnp.broadcast_to(b_ref[1], (tq, tk)),
                 zero, zero)
        carry = lax.fori_loop(0, qi, lambda kq, c: tiles(kq, c, False), carry)
        dq, dgq = _pair_rms_bwd(q, gq_ref[...], rq, tiles(qi, carry, True)[0] * scale)
        dq_ref[...] = dq
        dgq_ref[0] += dgq

        @pl.when(qi == nq - 1)
        def _():
            def finish(c, dgk):
                rows = pl.ds(pl.multiple_of(c * tq, tq), tq)
                k = k_ref[rows, :]
                _, rk = _pair_rms(k, gk_ref[...])
                dk, dgk_c = _pair_rms_bwd(k, gk_ref[...], rk, dkt_ref[:, rows].T)
                dk_ref[rows, :] = dk
                dv_ref[rows, :] = dvt_ref[:, rows].T
                return dgk + dgk_c

            dgk_ref[0] = lax.fori_loop(0, nq, finish, jnp.zeros((1, pw), F32))

    def cols(block0):
        return pl.BlockSpec((s, pw), lambda hp, i: (0, block0 + hp))

    gain = pl.BlockSpec((1, pw), lambda hp, i: (0, 0))
    q_rows = pl.BlockSpec((tq, pw), lambda hp, i: (i, hp))
    all_rows = pl.BlockSpec((s, pw), lambda hp, i: (0, hp))
    pair_gain = pl.BlockSpec((1, 1, pw), lambda hp, i: (hp, 0, 0))
    wide = jax.ShapeDtypeStruct((s, ATTN_WIDTH), F32)
    gains = jax.ShapeDtypeStruct((HEAD_PAIRS, 1, pw), F32)
    return _call_with_ride_along(
        body, ride, name="attn_bwd", grid=(HEAD_PAIRS, nq), semantics=("parallel", "arbitrary"),
        in_specs=[pl.BlockSpec((tq, pw), lambda hp, i: (i, _Q_BLOCK0 + hp)), cols(_K_BLOCK0), cols(_V_BLOCK0),
                  gain, gain, pl.BlockSpec((2, tq, 1), lambda hp, i: (hp, i, 0)), q_rows],
        out_specs=(q_rows, all_rows, all_rows, pair_gain, pair_gain),
        out_shape=(wide, wide, wide, gains, gains),
        scratch_shapes=[pltpu.VMEM((s, pw), BF16), pltpu.VMEM((s, pw), BF16),
                        pltpu.VMEM((pw, s), F32), pltpu.VMEM((pw, s), F32)],
        operands=(proj, proj, proj, gq, gk, bsum, do))


def _reduce_adamw(parts, w, m, v, *, name, row0s=(0,)):
    n, _, cols = parts.shape
    rows = w.shape[0]
    seg_rows = rows // len(row0s)
    tile_rows = 2 * SUBLANES
    tr = max(t for t in range(tile_rows, min(seg_rows, 512) + 1, tile_rows)
             if seg_rows % t == 0 and all(r0 % t == 0 for r0 in row0s))
    seg_tiles = seg_rows // tr

    def parts_tile(i):
        tile = row0s[0] // tr + i
        for l in range(1, len(row0s)):
            tile = jnp.where(i >= l * seg_tiles, row0s[l] // tr + i - l * seg_tiles, tile)
        return tile

    c1 = 1.0 - ADAM_B1 ** ADAM_STEP
    c2 = 1.0 - ADAM_B2 ** ADAM_STEP

    def body(p_ref, w_ref, m_ref, v_ref, g_ref, d_ref, nm_ref, nv_ref):
        g = p_ref[0].astype(F32)
        for i in range(1, n):
            g = g + p_ref[i].astype(F32)
        nm = ADAM_B1 * m_ref[...] + (1.0 - ADAM_B1) * g
        nv = ADAM_B2 * v_ref[...] + (1.0 - ADAM_B2) * (g * g)
        g_ref[...] = g
        nm_ref[...] = nm
        nv_ref[...] = nv
        d_ref[...] = -ADAM_LR * ((nm / c1) / (jnp.sqrt(nv / c2) + ADAM_EPS) + ADAM_WD * w_ref[...])

    row = pl.BlockSpec((tr, cols), lambda i: (i, 0))
    out = jax.ShapeDtypeStruct((rows, cols), F32)
    return pl.pallas_call(
        body, name=name, grid=(rows // tr,),
        in_specs=[pl.BlockSpec((n, tr, cols), lambda i: (0, parts_tile(i), 0)), row, row, row],
        out_specs=(row,) * 4, out_shape=(out,) * 4, compiler_params=_params(("parallel",)))(parts, w, m, v)


def _all_gather(shards, *, name):
    n_arr = len(shards)

    def body(*refs):
        x_refs, out_refs = refs[:n_arr], refs[n_arr:2 * n_arr]
        send_sems, recv_sems, local_sems = refs[2 * n_arr:]
        x, y, c = lax.axis_index("x"), lax.axis_index("y"), lax.axis_index("c")
        me, sibling = (x, y, c), (x, y, 1 - c)
        chips = [(1 - x, y), (x, 1 - y), (1 - x, 1 - y)]

        def slot(a, px, py, pc):
            return out_refs[a].at[4 * px + 2 * py + pc]

        def copy(a, k, block, to, src=None):
            return pltpu.make_async_remote_copy(
                src_ref=slot(a, *block) if src is None else src, dst_ref=slot(a, *block),
                send_sem=send_sems.at[a, k], recv_sem=recv_sems.at[a, k], device_id=to, device_id_type=MESH_IDS)

        arrays = range(n_arr)
        mine = [pltpu.make_async_copy(x_refs[a], slot(a, *me), local_sems.at[a]) for a in arrays]
        for cp in mine:
            cp.start()
        first = [copy(a, 0, me, sibling, src=x_refs[a]) for a in arrays]
        first += [copy(a, 1 + j, me, (*chip, c), src=x_refs[a]) for j, chip in enumerate(chips) for a in arrays]
        for cp in first:
            cp.start()
        passed = []
        for j, chip in enumerate(chips):
            for a in arrays:
                copy(a, 1 + j, (*chip, c), me).wait_recv()
                passed.append(copy(a, 4 + j, (*chip, c), sibling))
                passed[-1].start()
        for a in arrays:
            copy(a, 0, sibling, me).wait_recv()
        for j, chip in enumerate(chips):
            for a in arrays:
                copy(a, 4 + j, (*chip, 1 - c), me).wait_recv()
        for cp in first + passed:
            cp.wait_send()
        for cp in mine:
            cp.wait()

    any_spec = pl.BlockSpec(memory_space=pl.ANY)
    return pl.pallas_call(
        body, name=name, out_shape=tuple(jax.ShapeDtypeStruct((N_DEV,) + s.shape, s.dtype) for s in shards),
        in_specs=[any_spec] * n_arr, out_specs=(any_spec,) * n_arr,
        scratch_shapes=[pltpu.SemaphoreType.DMA((n_arr, 7)), pltpu.SemaphoreType.DMA((n_arr, 7)),
                        pltpu.SemaphoreType.DMA((n_arr,))],
    )(*shards)


def _direct_exchange(x_refs, out_refs, send_sems, recv_sems, local_sems, *, scatter, row0s):
    arrays = range(len(x_refs))
    x, y, c = lax.axis_index("x"), lax.axis_index("y"), lax.axis_index("c")
    me = 4 * x + 2 * y + c

    def src(a, slot):
        return x_refs[a].at[slot] if scatter else x_refs[a]

    def landing(a, sender):
        return out_refs[a].at[sender, pl.ds(row0s[a], x_refs[a].shape[-2]), :]

    def local(a):
        return pltpu.make_async_copy(src(a, me), landing(a, me), local_sems.at[a])

    def copy(a, k, arriving):
        px, py, pc = x ^ (k >> 2), y ^ ((k >> 1) & 1), c ^ (k & 1)
        peer = 4 * px + 2 * py + pc
        return pltpu.make_async_remote_copy(
            src_ref=src(a, peer), dst_ref=landing(a, peer if arriving else me),
            send_sem=send_sems.at[a, k - 1], recv_sem=recv_sems.at[a, k - 1],
            device_id=(px, py, pc), device_id_type=MESH_IDS)

    def start():
        for a in arrays:
            local(a).start()
        for k in range(1, N_DEV):
            for a in arrays:
                copy(a, k, False).start()

    def wait():
        for k in range(1, N_DEV):
            for a in arrays:
                copy(a, k, True).wait_recv()
        for k in range(1, N_DEV):
            for a in arrays:
                copy(a, k, False).wait_send()
        for a in arrays:
            local(a).wait()

    return start, wait


def _exchange_sems(n_arr):
    return [pltpu.SemaphoreType.DMA((n_arr, N_DEV - 1)), pltpu.SemaphoreType.DMA((n_arr, N_DEV - 1)),
            pltpu.SemaphoreType.DMA((n_arr,))]


def _pack_small(vals):
    flat = jnp.concatenate([vals[n].reshape(-1) for n in SMALL])
    rows = -(-flat.shape[0] // (SUBLANES * PACK_COLS)) * SUBLANES
    return jnp.pad(flat, (0, rows * PACK_COLS - flat.shape[0])).reshape(rows, PACK_COLS)


def _unpack_small(flat, shapes):
    flat = flat.reshape(-1)
    out, off = {}, 0
    for n in SMALL:
        size = math.prod(shapes[n])
        out[n] = flat[off:off + size].reshape(shapes[n])
        off += size
    return out


def _pair_gain(g):
    return jnp.tile(g, 2)[None]


def _ssm_setup(sp):
    g, p = SSM_GROUPS, SSM_STATE
    a_re = sp["ssm_a_re"][:, None, :]
    a_im = sp["ssm_a_im"][:, None, :]
    log_dt = jnp.broadcast_to(sp["ssm_log_dt"][:, None, None], (g, 1, p))
    bt_re = sp["ssm_b_re"].transpose(0, 2, 1)
    bt_im = sp["ssm_b_im"].transpose(0, 2, 1)
    return a_re, a_im, log_dt, bt_re, bt_im


def _layer_fwd(h, p_l, w, sp, ride=None):
    hn = _rms_fwd(h, sp["mix_norm_g"][None], name="rms_mix")
    proj = _mm(hn, w["w_in"], name="mm_proj")

    pw_re, pw_im, bbt_re, bbt_im = _ssm_param_fwd(*_ssm_setup(sp))
    pw_re = pw_re.transpose(1, 0, 2).reshape(SUBLANES, SSM_LANES)
    pw_im = pw_im.transpose(1, 0, 2).reshape(SUBLANES, SSM_LANES)
    ssm_mats = (_block_diag(bbt_re).astype(BF16), _block_diag(bbt_im).astype(BF16),
                _block_diag(sp["ssm_c_re"]).astype(BF16), _block_diag(-sp["ssm_c_im"]).astype(BF16),
                pw_re, pw_im, sp["ssm_d"].reshape(1, SSM_WIDTH))
    x_re, x_im, y, z = _ssm_fwd(proj, *ssm_mats)
    zz = _mm(z, w["ssm_w_glu"], name="mm_glu")

    o, bsum, *landed = _attn_fwd(proj, _pair_gain(sp["q_norm_g"]), _pair_gain(sp["k_norm_g"]), ride)

    bglu = sp["ssm_b_glu"][None]
    ycat = _gate_fwd(zz, bglu, proj, o)
    h2, hn2 = _mm_residual_norm(ycat, w["w_out"], h, sp["ple_norm_g"][None], name="mm_out")
    pp = _mm(p_l, w["w_ple_proj"], name="mm_ple_proj")
    pgl, h3 = _mm_gated_add(hn2, w["w_ple_gate"], h2, pp, name="mm_ple_gate")
    saved = dict(h=h, hn=hn, proj=proj, ssm_mats=ssm_mats, x_re=x_re, x_im=x_im, y=y, z=z,
                 zz=zz, bsum=bsum, o=o, ycat=ycat, h2=h2, hn2=hn2,
                 pgl=pgl, pp=pp, p_l=p_l, bglu=bglu)
    return h3, saved, landed


def _layer_bwd(dh3, sv, w, sp, ship=None):
    g, hh, p = SSM_GROUPS, SSM_GROUP, SSM_STATE
    grads = {}
    ship = ship or (lambda stage, grads_so_far, parts: None)
    dgp, dpp = _ple_bwd(dh3, sv["pgl"], sv["pp"])
    grads["w_ple_gate"] = _mm(sv["hn2"], dgp, ta=True, out_dtype=BF16, name="mm_d_ple_gate")
    grads["w_ple_proj"] = _mm(sv["p_l"], dpp, ta=True, out_dtype=BF16, name="mm_d_ple_proj")
    dh2, dg2 = _mm_norm_grad(dgp, w["w_ple_gate"], sv["h2"], sp["ple_norm_g"][None], dh3, name="mm_dhn2")
    grads["ple_norm_g"] = dg2[0]
    grads["w_out"] = _mm(sv["ycat"], dh2, ta=True, out_dtype=BF16, name="mm_d_out")
    dyc = _mm(dh2, w["w_out"], tb=True, name="mm_dycat")
    dzz, dgs, do, dga, dbglu = _gate_bwd(dyc, sv["zz"], sv["bglu"], sv["proj"], sv["o"])
    grads["ssm_b_glu"] = dbglu[0]

    dq, dk, dv, dgq, dgk, *parts = _attn_bwd(sv["proj"], _pair_gain(sp["q_norm_g"]), _pair_gain(sp["k_norm_g"]),
                                             sv["bsum"], do, ship("attn", grads, None))
    grads["q_norm_g"] = jnp.sum(dgq.reshape(ATTN_HEADS, HEAD_DIM), axis=0)
    grads["k_norm_g"] = jnp.sum(dgk.reshape(ATTN_HEADS, HEAD_DIM), axis=0)

    grads["ssm_w_glu"] = _mm(sv["z"], dzz, ta=True, out_dtype=BF16, name="mm_d_glu")
    dz = _mm(dzz, w["ssm_w_glu"], tb=True, name="mm_dz")
    du, dbc_re, dbc_im, dcc_re, dcc_im, gab_re, gab_im, dd, *landed = _ssm_bwd(
        dz, sv["y"], sv["proj"], sv["x_re"], sv["x_im"], *sv["ssm_mats"], ride=ship("ssm", grads, parts))
    parts = parts[:len(parts) - len(landed)] + landed
    grads["ssm_d"] = dd.reshape(g, hh)
    grads["ssm_c_re"] = _block_diag_take(dcc_re, hh, p)
    grads["ssm_c_im"] = -_block_diag_take(dcc_im, hh, p)
    da_re, da_im, dldt, dbt_re, dbt_im = _ssm_param_bwd(
        *_ssm_setup(sp), gab_re.reshape(g, 1, p), gab_im.reshape(g, 1, p),
        _block_diag_take(dbc_re, hh, p), _block_diag_take(dbc_im, hh, p))
    grads["ssm_a_re"], grads["ssm_a_im"] = da_re[:, 0, :], da_im[:, 0, :]
    grads["ssm_log_dt"] = dldt[:, 0, 0]
    grads["ssm_b_re"], grads["ssm_b_im"] = dbt_re.transpose(0, 2, 1), dbt_im.transpose(0, 2, 1)

    dproj = jnp.concatenate([du, dgs, dq, dk, dv, dga], axis=1)
    grads["w_in"] = _mm(sv["hn"], dproj, ta=True, out_dtype=BF16, name="mm_d_in")
    dh, dg1, *landed = _mm_norm_grad(dproj, w["w_in"], sv["h"], sp["mix_norm_g"][None], dh2, name="mm_dhn",
                                     ride=ship("dhn", grads, parts))
    parts = landed + parts[len(landed):]
    grads["mix_norm_g"] = dg1[0]
    return dh, grads, parts


def _local_step(x, p, target, big, small, fetch=None, ship=None):
    h = x
    big = list(big)
    saved = []
    for l in range(DEPTH):
        sp = {n: small[n][l] for n in SMALL}
        h, sv, landed = _layer_fwd(h, p[l], big[l], sp, fetch[0] if fetch and l == 0 else None)
        if landed:
            big[DEPTH - 1] = fetch[1](landed)
        saved.append(sv)
    dh, loss_parts = _loss_head(h, target)
    grads = [None] * DEPTH
    shipped = []
    for l in reversed(range(DEPTH)):
        sp = {n: small[n][l] for n in SMALL}
        carry = None
        if ship and l == 0:
            def carry(stage, grads_so_far, parts):
                return ship(stage, grads[DEPTH - 1], grads_so_far, parts)
        dh, grads[l], landed = _layer_bwd(dh, saved[l], big[l], sp, carry)
        shipped = landed or shipped
    return jnp.sum(loss_parts), dh, grads, shipped


def kernel(x, p, mix_norm_g, w_in, ssm_a_re, ssm_a_im, ssm_log_dt, ssm_b_re, ssm_b_im, ssm_c_re, ssm_c_im, ssm_d, ssm_w_glu, ssm_b_glu, q_norm_g, k_norm_g, w_out, ple_norm_g, w_ple_gate, w_ple_proj, loss_target, m_mix_norm_g, m_w_in, m_ssm_a_re, m_ssm_a_im, m_ssm_log_dt, m_ssm_b_re, m_ssm_b_im, m_ssm_c_re, m_ssm_c_im, m_ssm_d, m_ssm_w_glu, m_ssm_b_glu, m_q_norm_g, m_k_norm_g, m_w_out, m_ple_norm_g, m_w_ple_gate, m_w_ple_proj, v_mix_norm_g, v_w_in, v_ssm_a_re, v_ssm_a_im, v_ssm_log_dt, v_ssm_b_re, v_ssm_b_im, v_ssm_c_re, v_ssm_c_im, v_ssm_d, v_ssm_w_glu, v_ssm_b_glu, v_q_norm_g, v_k_norm_g, v_w_out, v_ple_norm_g, v_w_ple_gate, v_w_ple_proj):
    given = dict(locals())
    wts = {n: given[n] for n in WEIGHTS}
    mom = {n: given["m_" + n] for n in WEIGHTS}
    var = {n: given["v_" + n] for n in WEIGHTS}

    def rows2d(a):
        return a.reshape(-1, a.shape[-1])

    assert DEPTH == 2
    groups = (("w_in",), ("ssm_w_glu", "w_ple_proj"), ("w_out", "w_ple_gate"))
    layer_rows = [sum(wts[n].shape[1] for n in names) for names in groups]
    widths = [wts[names[0]].shape[2] for names in groups]

    def shards_of(l):
        return [jnp.concatenate([wts[n][l].astype(BF16) for n in names], axis=0) for names in groups]

    def weights_of(gathered):
        full = {}
        for names, got in zip(groups, gathered):
            row0 = 0
            for n in names:
                r = wts[n].shape[1]
                blocks = got[:, row0:row0 + r, :]
                full[n] = blocks.reshape(N_DEV * r, -1) if n in ROW_SHARDED else blocks.transpose(1, 0, 2).reshape(r, -1)
                row0 += r
        return full

    def blocks_to_send(layer_grads, which):
        def blocks(n):
            g = layer_grads[n]
            if n in ROW_SHARDED:
                return g.reshape(N_DEV, -1, g.shape[1])
            return g.reshape(g.shape[0], N_DEV, -1).transpose(1, 0, 2)

        return [jnp.concatenate([blocks(n) for n in groups[gi]], axis=1) for gi in which]

    fetch = (dict(bufs=shards_of(1), scatter=False, row0s=[0] * len(groups),
                  into_shapes=[jax.ShapeDtypeStruct((N_DEV, r, c), BF16) for r, c in zip(layer_rows, widths)]),
             weights_of)

    def ship(stage, last_grads, first_grads, parts):
        if stage == "attn":
            return dict(bufs=blocks_to_send(last_grads, (0, 1, 2)), scatter=True, row0s=layer_rows, into_shapes=[
                jax.ShapeDtypeStruct((N_DEV, DEPTH * r, c), BF16) for r, c in zip(layer_rows, widths)])
        which = (1, 2) if stage == "ssm" else (0,)
        return dict(bufs=blocks_to_send(first_grads, which), scatter=True, row0s=[0] * len(which),
                    into=[parts[gi] for gi in which])

    first = weights_of(_all_gather(shards_of(0), name="gather_weights"))
    small = {n: wts[n] for n in SMALL}
    loss, dx, grads, parts = _local_step(x[0], p[:, 0], loss_target[0], [first, None], small, fetch, ship)
    loss = lax.psum(loss, ("x", "y", "c"))

    big_out = {}
    for names, part, rows in zip(groups, parts, layer_rows):
        row0 = 0
        for n in names:
            res = _reduce_adamw(part, rows2d(wts[n]), rows2d(mom[n]), rows2d(var[n]),
                                row0s=[l * rows + row0 for l in range(DEPTH)], name="adamw_" + n)
            big_out[n] = [a.reshape(wts[n].shape) for a in res]
            row0 += wts[n].shape[1]

    small_shapes = {n: wts[n].shape for n in SMALL}
    local_small = _pack_small({n: jnp.stack([grads[l][n] for l in range(DEPTH)]) for n in SMALL})
    (all_small,) = _all_gather([local_small], name="gather_small_grads")
    small_out = _reduce_adamw(all_small, _pack_small(small), _pack_small({n: mom[n] for n in SMALL}),
                              _pack_small({n: var[n] for n in SMALL}), name="adamw_replicated")
    small_out = [_unpack_small(a, small_shapes) for a in small_out]

    outs = [loss, dx[None]]
    for k in range(4):
        outs += [big_out[n][k] if n in BIG else small_out[k][n] for n in WEIGHTS]
    return tuple(outs)
```

```python
import math

import jax
import jax.numpy as jnp
from jax import lax
from jax.experimental import pallas as pl
from jax.experimental.pallas import tpu as pltpu

F32 = jnp.float32
BF16 = jnp.bfloat16
MESH_IDS = pl.DeviceIdType.MESH

N_DEV = 8
D_MODEL = 1024
DEPTH = 2
PLE_DIM = 256
SSM_WIDTH = 512
SSM_GROUP = 16
SSM_GROUPS = 32
SSM_STATE = 64
SSM_LANES = SSM_GROUPS * SSM_STATE
ATTN_WIDTH = 512
ATTN_HEADS = 8
HEAD_DIM = 64
RMS_EPS = 1e-6
ADAM_LR = 0.001
ADAM_B1 = 0.9
ADAM_B2 = 0.999
ADAM_EPS = 1e-08
ADAM_WD = 0.01
ADAM_STEP = 10

VMEM_LIMIT = 56 * 1024 * 1024
ATT_TQ = 512
ATT_TK = 128
SSM_BLOCK_GROUPS = 8
SSM_TIME_BLOCK = 512
SUBLANES = 8
PACK_COLS = 1024

BIG = ("w_in", "ssm_w_glu", "w_out", "w_ple_gate", "w_ple_proj")
ROW_SHARDED = ("w_out", "w_ple_gate")
SMALL = ("mix_norm_g", "ssm_a_re", "ssm_a_im", "ssm_log_dt", "ssm_b_re", "ssm_b_im", "ssm_c_re",
         "ssm_c_im", "ssm_d", "ssm_b_glu", "q_norm_g", "k_norm_g", "ple_norm_g")
WEIGHTS = ("mix_norm_g", "w_in", "ssm_a_re", "ssm_a_im", "ssm_log_dt", "ssm_b_re", "ssm_b_im",
           "ssm_c_re", "ssm_c_im", "ssm_d", "ssm_w_glu", "ssm_b_glu", "q_norm_g", "k_norm_g",
           "w_out", "ple_norm_g", "w_ple_gate", "w_ple_proj")
_NT = (((1,), (1,)), ((), ()))
_TN = (((0,), (0,)), ((), ()))


def _params(sem=None):
    return pltpu.CompilerParams(dimension_semantics=sem, vmem_limit_bytes=VMEM_LIMIT)


def _sigmoid(x):
    return 0.5 * jnp.tanh(0.5 * x) + 0.5


_GELU_K = math.sqrt(2.0 / math.pi)
_GELU_C = 0.044715


def _gelu(x):
    return 0.5 * x * (1.0 + jnp.tanh(_GELU_K * (x + _GELU_C * x * x * x)))


def _gelu_grad(x):
    th = jnp.tanh(_GELU_K * (x + _GELU_C * x * x * x))
    return 0.5 * (1.0 + th) + 0.5 * x * (1.0 - th * th) * _GELU_K * (1.0 + 3.0 * _GELU_C * x * x)


def _mm(a, b, *, name, ta=False, tb=False, add=None, out_dtype=F32, tm=1024, tn=1024, tk=512,
        finish=None, extras=(), outs=None, ride=None):
    m, k = (a.shape[1], a.shape[0]) if ta else a.shape
    n = b.shape[0] if tb else b.shape[1]
    tm, tn, tk = min(tm, m), min(tn, n), min(tk, k)
    assert m % tm == 0 and n % tn == 0 and k % tk == 0, (name, a.shape, b.shape)
    n_steps = k // tk
    dims = (((0 if ta else 1,), (1 if tb else 0,)), ((), ()))
    a_spec = (pl.BlockSpec((tk, tm), lambda i, j, kk: (kk, i)) if ta
              else pl.BlockSpec((tm, tk), lambda i, j, kk: (i, kk)))
    b_spec = (pl.BlockSpec((tn, tk), lambda i, j, kk: (j, kk)) if tb
              else pl.BlockSpec((tk, tn), lambda i, j, kk: (kk, j)))
    o_spec = pl.BlockSpec((tm, tn), lambda i, j, kk: (i, j))
    ins, in_specs = [a, b], [a_spec, b_spec]
    if finish is None:
        if add is not None:
            ins.append(add)
            in_specs.append(o_spec)

        def finish(res, extra_refs, out_refs):
            if add is not None:
                res = res + extra_refs[0][...].astype(F32)
            out_refs[0][...] = res.astype(out_dtype)

        out_specs, out_shapes = [o_spec], [jax.ShapeDtypeStruct((m, n), out_dtype)]
    else:
        assert tn == n and add is None
        for arr, kind in extras:
            ins.append(arr)
            cols = arr.shape[1]
            in_specs.append(pl.BlockSpec((tm, cols), lambda i, j, kk: (i, 0)) if kind == "rows"
                            else pl.BlockSpec((1, cols), lambda i, j, kk: (0, 0)))
        out_specs = [pl.BlockSpec((tm, cols), lambda i, j, kk: (i, 0)) if kind == "rows"
                     else pl.BlockSpec((1, cols), lambda i, j, kk: (0, 0)) for cols, _, kind in outs]
        out_shapes = [jax.ShapeDtypeStruct((m if kind == "rows" else 1, cols), dtype) for cols, dtype, kind in outs]
    n_in = len(ins)

    def body(*refs):
        a_ref, b_ref, acc_ref = refs[0], refs[1], refs[-1]
        kk = pl.program_id(2)

        @pl.when(kk == 0)
        def _():
            acc_ref[...] = jnp.zeros_like(acc_ref)

        acc_ref[...] += lax.dot_general(a_ref[...].astype(BF16), b_ref[...].astype(BF16), dims,
                                        preferred_element_type=F32)

        @pl.when(kk == n_steps - 1)
        def _():
            finish(acc_ref[...], refs[2:n_in], refs[n_in:-1])

    sums = outs is not None and any(kind == "sum" for _, _, kind in outs)
    res = _call_with_ride_along(
        body, ride, name=name, grid=(m // tm, n // tn, n_steps),
        semantics=("arbitrary",) * 3 if sums else ("parallel", "parallel", "arbitrary"),
        in_specs=in_specs, out_specs=tuple(out_specs), out_shape=tuple(out_shapes),
        scratch_shapes=[pltpu.VMEM((tm, tn), F32)], operands=ins)
    return res[0] if outs is None and ride is None else res


def _rms(hv, g):
    r = lax.rsqrt(jnp.mean(hv * hv, axis=-1, keepdims=True) + RMS_EPS)
    return hv * r * g, r


def _rms_grad(hv, g, dn):
    _, r = _rms(hv, g)
    a = dn * g
    dot = jnp.mean(a * hv, axis=-1, keepdims=True)
    return r * a - hv * (r * r * r * dot), jnp.sum(dn * hv * r, axis=0, keepdims=True)


def _mm_residual_norm(a, b, h, g, *, name):
    def finish(res, extra_refs, out_refs):
        h_ref, g_ref = extra_refs
        h2 = res + h_ref[...]
        out_refs[0][...] = h2
        out_refs[1][...] = _rms(h2, g_ref[...])[0].astype(BF16)

    d = b.shape[1]
    return _mm(a, b, name=name, finish=finish, extras=[(h, "rows"), (g, "vec")],
               outs=[(d, F32, "rows"), (d, BF16, "rows")])


def _mm_gated_add(a, b, h2, pp, *, name):
    def finish(res, extra_refs, out_refs):
        h_ref, p_ref = extra_refs
        out_refs[0][...] = res
        out_refs[1][...] = h_ref[...] + _sigmoid(res) * p_ref[...]

    d = b.shape[1]
    return _mm(a, b, name=name, finish=finish, extras=[(h2, "rows"), (pp, "rows")],
               outs=[(d, F32, "rows"), (d, F32, "rows")])


def _mm_norm_grad(dy, w, h, g, dres, *, name, ride=None):
    def finish(res, extra_refs, out_refs):
        h_ref, g_ref, r_ref = extra_refs
        dh, dg = _rms_grad(h_ref[...], g_ref[...], res)
        out_refs[0][...] = dh + r_ref[...]

        @pl.when(pl.program_id(0) == 0)
        def _():
            out_refs[1][...] = jnp.zeros_like(out_refs[1])

        out_refs[1][...] += dg

    d = w.shape[0]
    return _mm(dy, w, name=name, tb=True, finish=finish, extras=[(h, "rows"), (g, "vec"), (dres, "rows")],
               outs=[(d, F32, "rows"), (d, F32, "sum")], ride=ride)


def _rows_tile(rows, want=512):
    t = min(rows, want)
    assert rows % t == 0
    return t


def _rowwise(body, name, rows, ins, outs):
    widest = max(cols for _, cols, _ in ins)
    tm = _rows_tile(rows, 512 if widest > 128 else 4096)
    in_specs = []
    for _, cols, cb in ins:
        if cb is None:
            in_specs.append(pl.BlockSpec((1, cols), lambda i: (0, 0)))
        else:
            in_specs.append(pl.BlockSpec((tm, cols), lambda i, cb=cb: (i, cb)))
    out_specs, out_shapes = [], []
    for cols, dtype, is_acc in outs:
        if is_acc:
            out_specs.append(pl.BlockSpec((1, cols), lambda i: (0, 0)))
            out_shapes.append(jax.ShapeDtypeStruct((1, cols), dtype))
        else:
            out_specs.append(pl.BlockSpec((tm, cols), lambda i: (i, 0)))
            out_shapes.append(jax.ShapeDtypeStruct((rows, cols), dtype))
    any_acc = any(o[2] for o in outs)
    return pl.pallas_call(
        body, name=name, grid=(rows // tm,), in_specs=in_specs, out_specs=tuple(out_specs),
        out_shape=tuple(out_shapes),
        compiler_params=_params(("arbitrary",) if any_acc else ("parallel",)))(*[a for a, _, _ in ins])


def _accumulate(ref, value):
    @pl.when(pl.program_id(0) == 0)
    def _():
        ref[...] = jnp.zeros_like(ref)

    ref[...] += value


def _rms_fwd(h, g, *, name):
    rows, d = h.shape

    def body(h_ref, g_ref, o_ref):
        o_ref[...] = _rms(h_ref[...], g_ref[...])[0].astype(BF16)

    return _rowwise(body, name, rows, [(h, d, 0), (g, d, None)], [(d, BF16, False)])[0]


def _gate_fwd(zz, bglu, proj, o):
    rows = zz.shape[0]
    w = SSM_WIDTH

    def body(zz_ref, b_ref, gs_ref, o_ref, ga_ref, y_ref):
        zz_v = zz_ref[...] + b_ref[...]
        val, gate = zz_v[:, :w], zz_v[:, w:]
        gs, ga = gs_ref[...], ga_ref[...]
        y_ref[:, :w] = (val * _sigmoid(gate) * (gs * _sigmoid(gs))).astype(BF16)
        y_ref[:, w:] = (o_ref[...] * (ga * _sigmoid(ga))).astype(BF16)

    return _rowwise(body, "gate_fwd", rows,
                    [(zz, 2 * w, 0), (bglu, 2 * w, None), (proj, w, 1), (o, w, 0), (proj, w, 5)],
                    [(2 * w, BF16, False)])[0]


def _gate_bwd(dyc, zz, bglu, proj, o):
    rows = zz.shape[0]
    w = SSM_WIDTH

    def body(dy_ref, zz_ref, b_ref, gs_ref, o_ref, ga_ref, dzz_ref, dgs_ref, do_ref, dga_ref, db_ref):
        zz_v = zz_ref[...] + b_ref[...]
        val, gate = zz_v[:, :w], zz_v[:, w:]
        gs, ga = gs_ref[...], ga_ref[...]
        dys, dya = dy_ref[:, :w], dy_ref[:, w:]
        sg, ss, sa = _sigmoid(gate), _sigmoid(gs), _sigmoid(ga)
        glu = val * sg
        dglu = dys * (gs * ss)
        dgs_ref[...] = dys * glu * (ss * (1.0 + gs * (1.0 - ss)))
        dval = dglu * sg
        dgate = dglu * val * sg * (1.0 - sg)
        dzz_ref[:, :w] = dval.astype(BF16)
        dzz_ref[:, w:] = dgate.astype(BF16)
        do_ref[...] = dya * (ga * sa)
        dga_ref[...] = dya * o_ref[...] * (sa * (1.0 + ga * (1.0 - sa)))
        _accumulate(db_ref, jnp.concatenate([jnp.sum(dval, axis=0, keepdims=True),
                                             jnp.sum(dgate, axis=0, keepdims=True)], axis=1))

    return _rowwise(body, "gate_bwd", rows,
                    [(dyc, 2 * w, 0), (zz, 2 * w, 0), (bglu, 2 * w, None), (proj, w, 1), (o, w, 0), (proj, w, 5)],
                    [(2 * w, BF16, False), (w, F32, False), (w, F32, False), (w, F32, False), (2 * w, F32, True)])


def _ple_bwd(dh3, pgl, pp):
    rows, d = dh3.shape

    def body(dh_ref, g_ref, p_ref, dg_ref, dp_ref):
        sg = _sigmoid(g_ref[...])
        dh = dh_ref[...]
        dg_ref[...] = (dh * p_ref[...] * sg * (1.0 - sg)).astype(BF16)
        dp_ref[...] = (dh * sg).astype(BF16)

    return _rowwise(body, "ple_bwd", rows, [(dh3, d, 0), (pgl, d, 0), (pp, d, 0)],
                    [(d, BF16, False), (d, BF16, False)])


def _loss_head(h, target):
    rows, d = h.shape

    def body(h_ref, t_ref, dh_ref, l_ref):
        err = h_ref[...] - t_ref[...]
        dh_ref[...] = err * (1.0 / d)
        _accumulate(l_ref, jnp.sum(err * err, axis=0, keepdims=True) * (0.5 / d))

    return _rowwise(body, "loss_head", rows, [(h, d, 0), (target, d, 0)], [(d, F32, False), (d, F32, True)])


def _zoh(lr, li, dt):
    mag = jnp.exp(lr * dt)
    ab_re = mag * jnp.cos(li * dt)
    ab_im = mag * jnp.sin(li * dt)
    num_re = ab_re - 1.0
    den = lr * lr + li * li
    f_re = (num_re * lr + ab_im * li) / den
    f_im = (ab_im * lr - num_re * li) / den
    return ab_re, ab_im, f_re, f_im, den


def _ssm_param_fwd(a_re, a_im, log_dt, bt_re, bt_im):
    g, _, p = a_re.shape
    h = bt_re.shape[1]

    def body(lr_ref, li_ref, ldt_ref, br_ref, bi_ref, pr_ref, pi_ref, bbr_ref, bbi_ref):
        dt = jnp.exp(ldt_ref[...])
        ab_re, ab_im, f_re, f_im, _ = _zoh(lr_ref[...], li_ref[...], dt)
        cr, ci = ab_re, ab_im
        for k in range(SUBLANES):
            pr_ref[:, k:k + 1, :] = cr
            pi_ref[:, k:k + 1, :] = ci
            cr, ci = cr * ab_re - ci * ab_im, cr * ab_im + ci * ab_re
        br, bi = br_ref[...], bi_ref[...]
        bbr_ref[...] = f_re * br - f_im * bi
        bbi_ref[...] = f_re * bi + f_im * br

    pw = jax.ShapeDtypeStruct((g, SUBLANES, p), F32)
    large = jax.ShapeDtypeStruct((g, h, p), F32)
    return pl.pallas_call(body, name="ssm_param_fwd", out_shape=(pw, pw, large, large),
                          compiler_params=_params())(a_re, a_im, log_dt, bt_re, bt_im)


def _ssm_param_bwd(a_re, a_im, log_dt, bt_re, bt_im, gab_re, gab_im, gbb_re, gbb_im):
    g, _, p = a_re.shape
    h = bt_re.shape[1]

    def body(lr_ref, li_ref, ldt_ref, br_ref, bi_ref, gar_ref, gai_ref, gbr_ref, gbi_ref,
             dlr_ref, dli_ref, dldt_ref, dbr_ref, dbi_ref):
        lr, li = lr_ref[...], li_ref[...]
        dt = jnp.exp(ldt_ref[...])
        ab_re, ab_im, f_re, f_im, den = _zoh(lr, li, dt)
        br, bi = br_ref[...], bi_ref[...]
        gbr, gbi = gbr_ref[...], gbi_ref[...]
        dbr_ref[...] = f_re * gbr + f_im * gbi
        dbi_ref[...] = f_re * gbi - f_im * gbr
        gf_re = jnp.sum(br * gbr + bi * gbi, axis=1, keepdims=True)
        gf_im = jnp.sum(br * gbi - bi * gbr, axis=1, keepdims=True)
        il_re, il_im = lr / den, -li / den
        ga_re = gar_ref[...] + il_re * gf_re + il_im * gf_im
        ga_im = gai_ref[...] + il_re * gf_im - il_im * gf_re
        q_re = f_re * il_re - f_im * il_im
        q_im = f_re * il_im + f_im * il_re
        gl_re = -(q_re * gf_re + q_im * gf_im)
        gl_im = -(q_re * gf_im - q_im * gf_re)
        dlr_ref[...] = gl_re + dt * (ab_re * ga_re + ab_im * ga_im)
        dli_ref[...] = gl_im + dt * (ab_re * ga_im - ab_im * ga_re)
        la_re = lr * ab_re - li * ab_im
        la_im = lr * ab_im + li * ab_re
        dldt_ref[...] = jnp.sum((la_re * ga_re + la_im * ga_im) * dt, axis=2, keepdims=True)

    small = jax.ShapeDtypeStruct((g, 1, p), F32)
    one = jax.ShapeDtypeStruct((g, 1, 1), F32)
    large = jax.ShapeDtypeStruct((g, h, p), F32)
    return pl.pallas_call(body, name="ssm_param_bwd", out_shape=(small, small, one, large, large),
                          compiler_params=_params())(
                              a_re, a_im, log_dt, bt_re, bt_im, gab_re, gab_im, gbb_re, gbb_im)


def _block_diag(m):
    g, h, p = m.shape
    nb = g // SSM_BLOCK_GROUPS
    eye = jnp.eye(SSM_BLOCK_GROUPS, dtype=m.dtype)
    m4 = m.reshape(nb, SSM_BLOCK_GROUPS, h, p)
    return (m4[:, :, :, None, :] * eye[None, :, None, :, None]).reshape(nb, SSM_BLOCK_GROUPS * h, SSM_BLOCK_GROUPS * p)


def _block_diag_take(m, h, p):
    nb = m.shape[0]
    eye = jnp.eye(SSM_BLOCK_GROUPS, dtype=m.dtype)
    m5 = m.reshape(nb, SSM_BLOCK_GROUPS, h, SSM_BLOCK_GROUPS, p)
    return jnp.sum(m5 * eye[None, :, None, :, None], axis=3).reshape(nb * SSM_BLOCK_GROUPS, h, p)


def _scan_tile(br, bi, pw_re, pw_im, reverse):
    row = lax.broadcasted_iota(jnp.int32, br.shape, 0)
    xr, xi = br, bi
    for d in (1, 2, 4):
        ar, ai = pw_re[d - 1:d, :], pw_im[d - 1:d, :]
        if reverse:
            keep = row < SUBLANES - d
            sr = jnp.where(keep, pltpu.roll(xr, SUBLANES - d, 0), 0.0)
            si = jnp.where(keep, pltpu.roll(xi, SUBLANES - d, 0), 0.0)
        else:
            keep = row >= d
            sr = jnp.where(keep, pltpu.roll(xr, d, 0), 0.0)
            si = jnp.where(keep, pltpu.roll(xi, d, 0), 0.0)
        xr, xi = xr + ar * sr - ai * si, xi + ar * si + ai * sr
    return xr, xi


def _ssm_blocks(s):
    tt = min(SSM_TIME_BLOCK, s)
    assert s % tt == 0 and tt % SUBLANES == 0
    ch = SSM_BLOCK_GROUPS * SSM_GROUP
    st = SSM_BLOCK_GROUPS * SSM_STATE
    return tt, s // tt, SSM_WIDTH // ch, ch, st


def _ssm_fwd(proj, bc_re, bc_im, cc_re, cc_im, pw_re, pw_im, dvec):
    s = proj.shape[0]
    tt, nt, nb, ch, st = _ssm_blocks(s)
    n_tiles = tt // SUBLANES

    def body(u_ref, bre_ref, bim_ref, cre_ref, cim_ref, pr_ref, pi_ref, d_ref,
             xr_ref, xi_ref, y_ref, z_ref, carry_ref):
        @pl.when(pl.program_id(1) == 0)
        def _():
            carry_ref[...] = jnp.zeros_like(carry_ref)

        u = u_ref[...]
        ub = u.astype(BF16)
        xr_ref[...] = jnp.dot(ub, bre_ref[0], preferred_element_type=F32)
        xi_ref[...] = jnp.dot(ub, bim_ref[0], preferred_element_type=F32)
        pw_r, pw_i = pr_ref[...], pi_ref[...]

        def step(t, carry):
            cr, ci = carry
            rows = pl.ds(pl.multiple_of(t * SUBLANES, SUBLANES), SUBLANES)
            xr, xi = _scan_tile(xr_ref[rows, :], xi_ref[rows, :], pw_r, pw_i, False)
            xr, xi = xr + pw_r * cr - pw_i * ci, xi + pw_r * ci + pw_i * cr
            xr_ref[rows, :] = xr
            xi_ref[rows, :] = xi
            return (jnp.broadcast_to(xr[SUBLANES - 1:SUBLANES, :], (SUBLANES, st)),
                    jnp.broadcast_to(xi[SUBLANES - 1:SUBLANES, :], (SUBLANES, st)))

        cr, ci = lax.fori_loop(0, n_tiles, step, (carry_ref[0], carry_ref[1]), unroll=2)
        carry_ref[0] = cr
        carry_ref[1] = ci
        y = (lax.dot_general(xr_ref[...].astype(BF16), cre_ref[0], _NT, preferred_element_type=F32)
             + lax.dot_general(xi_ref[...].astype(BF16), cim_ref[0], _NT, preferred_element_type=F32)
             + d_ref[...] * u)
        y_ref[...] = y
        z_ref[...] = _gelu(y).astype(BF16)

    chan = pl.BlockSpec((tt, ch), lambda b, t: (t, b))
    state = pl.BlockSpec((tt, st), lambda b, t: (t, b))
    mat = pl.BlockSpec((1, ch, st), lambda b, t: (b, 0, 0))
    pw = pl.BlockSpec((SUBLANES, st), lambda b, t: (0, b))
    vec = pl.BlockSpec((1, ch), lambda b, t: (0, b))
    x_shape = jax.ShapeDtypeStruct((s, nb * st), F32)
    return pl.pallas_call(
        body, name="ssm_fwd", grid=(nb, nt), in_specs=[chan, mat, mat, mat, mat, pw, pw, vec],
        out_specs=(state, state, chan, chan),
        out_shape=(x_shape, x_shape, jax.ShapeDtypeStruct((s, nb * ch), F32), jax.ShapeDtypeStruct((s, nb * ch), BF16)),
        scratch_shapes=[pltpu.VMEM((2, SUBLANES, st), F32)],
        compiler_params=_params(("parallel", "arbitrary")))(proj, bc_re, bc_im, cc_re, cc_im, pw_re, pw_im, dvec)


def _ssm_bwd(dz, y, proj, x_re, x_im, bc_re, bc_im, cc_re, cc_im, pw_re, pw_im, dvec, ride=None):
    s = proj.shape[0]
    tt, nt, nb, ch, st = _ssm_blocks(s)
    n_tiles = tt // SUBLANES

    def body(dz_ref, y_ref, u_ref, xr_ref, xi_ref, bre_ref, bim_ref, cre_ref, cim_ref, pr_ref, pi_ref, d_ref,
             du_ref, dbr_ref, dbi_ref, dcr_ref, dci_ref, gar_ref, gai_ref, dd_ref,
             lr_ref, li_ref, carry_ref, acc_ref):
        @pl.when(pl.program_id(1) == 0)
        def _():
            for ref in (carry_ref, acc_ref, dbr_ref, dbi_ref, dcr_ref, dci_ref, dd_ref):
                ref[...] = jnp.zeros_like(ref)

        u = u_ref[...]
        dy = dz_ref[...] * _gelu_grad(y_ref[...])
        ub, dyb = u.astype(BF16), dy.astype(BF16)
        lr_ref[...] = jnp.dot(dyb, cre_ref[0], preferred_element_type=F32)
        li_ref[...] = jnp.dot(dyb, cim_ref[0], preferred_element_type=F32)
        pw_r, pw_i = pr_ref[...], -pi_ref[...]
        pwc_r = jnp.concatenate([pw_r[SUBLANES - 1 - i:SUBLANES - i, :] for i in range(SUBLANES)], axis=0)
        pwc_i = jnp.concatenate([pw_i[SUBLANES - 1 - i:SUBLANES - i, :] for i in range(SUBLANES)], axis=0)
        row = lax.broadcasted_iota(jnp.int32, (SUBLANES, st), 0)

        def step(t, carry):
            cr, ci, acc_r, acc_i = carry
            rows = pl.ds(pl.multiple_of((n_tiles - 1 - t) * SUBLANES, SUBLANES), SUBLANES)
            lr, li = _scan_tile(lr_ref[rows, :], li_ref[rows, :], pw_r, pw_i, True)
            lr, li = lr + pwc_r * cr - pwc_i * ci, li + pwc_r * ci + pwc_i * cr
            lr_ref[rows, :] = lr
            li_ref[rows, :] = li
            nr = jnp.where(row < SUBLANES - 1, pltpu.roll(lr, SUBLANES - 1, 0), cr)
            ni = jnp.where(row < SUBLANES - 1, pltpu.roll(li, SUBLANES - 1, 0), ci)
            xr, xi = xr_ref[rows, :], xi_ref[rows, :]
            return (jnp.broadcast_to(lr[0:1, :], (SUBLANES, st)), jnp.broadcast_to(li[0:1, :], (SUBLANES, st)),
                    acc_r + xr * nr + xi * ni, acc_i + xr * ni - xi * nr)

        cr, ci, acc_r, acc_i = lax.fori_loop(
            0, n_tiles, step, (carry_ref[0], carry_ref[1], acc_ref[0], acc_ref[1]), unroll=2)
        carry_ref[0], carry_ref[1] = cr, ci
        acc_ref[0], acc_ref[1] = acc_r, acc_i
        lrb, lib = lr_ref[...].astype(BF16), li_ref[...].astype(BF16)
        du_ref[...] = (lax.dot_general(lrb, bre_ref[0], _NT, preferred_element_type=F32)
                       + lax.dot_general(lib, bim_ref[0], _NT, preferred_element_type=F32) + dy * d_ref[...])
        dbr_ref[0] += lax.dot_general(ub, lrb, _TN, preferred_element_type=F32)
        dbi_ref[0] += lax.dot_general(ub, lib, _TN, preferred_element_type=F32)
        dcr_ref[0] += lax.dot_general(dyb, xr_ref[...].astype(BF16), _TN, preferred_element_type=F32)
        dci_ref[0] += lax.dot_general(dyb, xi_ref[...].astype(BF16), _TN, preferred_element_type=F32)
        dd_ref[...] += jnp.sum(dy * u, axis=0, keepdims=True)

        @pl.when(pl.program_id(1) == nt - 1)
        def _():
            gar_ref[...] = jnp.sum(acc_r, axis=0, keepdims=True)
            gai_ref[...] = jnp.sum(acc_i, axis=0, keepdims=True)

    chan = pl.BlockSpec((tt, ch), lambda b, t: (nt - 1 - t, b))
    state = pl.BlockSpec((tt, st), lambda b, t: (nt - 1 - t, b))
    mat = pl.BlockSpec((1, ch, st), lambda b, t: (b, 0, 0))
    pw = pl.BlockSpec((SUBLANES, st), lambda b, t: (0, b))
    vec = pl.BlockSpec((1, ch), lambda b, t: (0, b))
    svec = pl.BlockSpec((1, st), lambda b, t: (0, b))
    mat_shape = jax.ShapeDtypeStruct((nb, ch, st), F32)
    return _call_with_ride_along(
        body, ride, name="ssm_bwd", grid=(nb, nt), semantics=("parallel", "arbitrary"),
        in_specs=[chan, chan, chan, state, state, mat, mat, mat, mat, pw, pw, vec],
        out_specs=(chan, mat, mat, mat, mat, svec, svec, vec),
        out_shape=(jax.ShapeDtypeStruct((s, nb * ch), F32), mat_shape, mat_shape, mat_shape, mat_shape,
                   jax.ShapeDtypeStruct((1, nb * st), F32), jax.ShapeDtypeStruct((1, nb * st), F32),
                   jax.ShapeDtypeStruct((1, nb * ch), F32)),
        scratch_shapes=[pltpu.VMEM((tt, st), F32), pltpu.VMEM((tt, st), F32),
                        pltpu.VMEM((2, SUBLANES, st), F32), pltpu.VMEM((2, SUBLANES, st), F32)],
        operands=(dz, y, proj, x_re, x_im, bc_re, bc_im, cc_re, cc_im, pw_re, pw_im, dvec))


def _split_dot(x, tri2):
    hi = x.astype(BF16)
    lo = (x - hi.astype(F32)).astype(BF16)
    return jnp.dot(jnp.concatenate([hi, lo], axis=1), tri2, preferred_element_type=F32)


def _tri_and_ones(tk, rel):
    r_i = lax.broadcasted_iota(jnp.int32, (2 * tk, 2 * tk), 0) % tk
    c_i = lax.broadcasted_iota(jnp.int32, (2 * tk, 2 * tk), 1)
    return jnp.where((c_i >= tk) | rel(r_i, c_i), 1.0, 0.0).astype(BF16)


def _attn_scores(qs, kb, own):
    z = lax.dot_general(qs, kb, _NT, preferred_element_type=F32)
    log_beta = jnp.minimum(z, 0.0) - jnp.log(1.0 + jnp.exp(-jnp.abs(z)))
    log_stay = log_beta - z
    if not own:
        return log_beta, log_stay, None
    mask = lax.broadcasted_iota(jnp.int32, z.shape, 1) < lax.broadcasted_iota(jnp.int32, z.shape, 0)
    return log_beta, jnp.where(mask, log_stay, 0.0), mask


def _attn_blocks(s):
    tq, tk = min(ATT_TQ, s), min(ATT_TK, s)
    assert s % tq == 0 and tq % tk == 0
    return tq, tk, tq // tk


def _first_head(shape, axis):
    return lax.broadcasted_iota(jnp.int32, shape, axis) < HEAD_DIM


def _pair_rms(x, g):
    first = _first_head(x.shape, 1)
    sq = x * x
    ms_a = jnp.sum(jnp.where(first, sq, 0.0), axis=1, keepdims=True) * (1.0 / HEAD_DIM)
    ms_b = jnp.sum(jnp.where(first, 0.0, sq), axis=1, keepdims=True) * (1.0 / HEAD_DIM)
    r = jnp.where(first, lax.rsqrt(ms_a + RMS_EPS), lax.rsqrt(ms_b + RMS_EPS))
    return x * r * g, r


def _pair_rms_bwd(x, g, r, dn):
    first = _first_head(x.shape, 1)
    a = dn * g
    ax = a * x
    dot_a = jnp.sum(jnp.where(first, ax, 0.0), axis=1, keepdims=True) * (1.0 / HEAD_DIM)
    dot_b = jnp.sum(jnp.where(first, 0.0, ax), axis=1, keepdims=True) * (1.0 / HEAD_DIM)
    dx = r * a - x * (r * r * r * jnp.where(first, dot_a, dot_b))
    return dx, jnp.sum(dn * x * r, axis=0, keepdims=True)


_Q_BLOCK0 = 2 * SSM_WIDTH // (2 * HEAD_DIM)
_K_BLOCK0 = _Q_BLOCK0 + ATTN_WIDTH // (2 * HEAD_DIM)
_V_BLOCK0 = _K_BLOCK0 + ATTN_WIDTH // (2 * HEAD_DIM)
HEAD_PAIRS = ATTN_HEADS // 2


def _fill_keys(k_ref, v_ref, gk_ref, kn_ref, vb_ref, rows_per_step):
    def fill(c, _):
        rows = pl.ds(pl.multiple_of(c * rows_per_step, rows_per_step), rows_per_step)
        kn, _ = _pair_rms(k_ref[rows, :], gk_ref[...])
        kn_ref[rows, :] = kn.astype(BF16)
        vb_ref[rows, :] = v_ref[rows, :].astype(BF16)
        return 0

    lax.fori_loop(0, k_ref.shape[0] // rows_per_step, fill, 0)


def _call_with_ride_along(body, ride, *, name, grid, semantics, in_specs, out_specs, out_shape, scratch_shapes,
                          operands):
    if ride is None:
        return pl.pallas_call(body, name=name, grid=grid, in_specs=in_specs, out_specs=out_specs, out_shape=out_shape,
                              scratch_shapes=scratch_shapes, compiler_params=_params(semantics))(*operands)
    n, n_in, n_out = len(ride["bufs"]), len(in_specs), len(out_specs)
    into = list(ride.get("into", ()))
    first_out = n_in + n + len(into)
    last = math.prod(grid) - 1

    def wrapped(*refs):
        ins, x_refs = refs[:n_in], refs[n_in:n_in + n]
        outs, out_refs = refs[first_out:first_out + n_out], refs[first_out + n_out:first_out + n_out + n]
        scratch, sems = refs[first_out + n_out + n:-3], refs[-3:]
        step = 0
        for axis, size in enumerate(grid):
            step = step * size + pl.program_id(axis)

        def exchange():
            return _direct_exchange(x_refs, out_refs, *sems, scatter=ride["scatter"], row0s=ride["row0s"])

        @pl.when(step == 0)
        def _():
            exchange()[0]()

        body(*ins, *outs, *scratch)

        @pl.when(step == last)
        def _():
            exchange()[1]()

    any_spec = pl.BlockSpec(memory_space=pl.ANY)
    landing = [jax.ShapeDtypeStruct(b.shape, b.dtype) for b in into] or list(ride["into_shapes"])
    return pl.pallas_call(
        wrapped, name=name, grid=grid, in_specs=list(in_specs) + [any_spec] * (n + len(into)),
        out_specs=tuple(out_specs) + (any_spec,) * n, out_shape=tuple(out_shape) + tuple(landing),
        input_output_aliases={n_in + n + a: n_out + a for a in range(len(into))},
        scratch_shapes=list(scratch_shapes) + _exchange_sems(n),
        compiler_params=_params(("arbitrary",) * len(grid)))(*operands, *ride["bufs"], *into)


def _attn_fwd(proj, gq, gk, ride=None):
    s = proj.shape[0]
    tq, tk, r = _attn_blocks(s)
    pw = 2 * HEAD_DIM

    def body(q_ref, k_ref, v_ref, gq_ref, gk_ref, o_ref, b_ref, kn_ref, vb_ref):
        qi = pl.program_id(1)

        @pl.when(qi == 0)
        def _():
            _fill_keys(k_ref, v_ref, gk_ref, kn_ref, vb_ref, tq)

        qn, _ = _pair_rms(q_ref[...], gq_ref[...])
        qn = qn * (HEAD_DIM ** -0.5)
        first = _first_head((tq, pw), 1)
        tri = _tri_and_ones(tk, lambda row, col: row > col)
        qs = [jnp.where(first, qn, 0.0).astype(BF16), jnp.where(first, 0.0, qn).astype(BF16)]

        def tiles(kq, carry, own):
            o_acc, accs = carry[0], list(carry[1:])
            ks = pl.ds(pl.multiple_of(kq * tq, tq), tq)
            kb, vb = kn_ref[ks, :], vb_ref[ks, :]
            scores = [_attn_scores(q, kb, own) for q in qs]
            sums = [[_split_dot(sc[1][:, u * tk:(u + 1) * tk], tri) for u in range(r)] for sc in scores]
            ws = []
            for h in range(2):
                later = [None] * r
                for u in reversed(range(r)):
                    later[u] = accs[h] + sums[h][u][:, :tk]
                    accs[h] = accs[h] + sums[h][u][:, tk:]
                w = jnp.exp(scores[h][0] + jnp.concatenate(later, axis=1))
                if own:
                    w = jnp.where(scores[h][2], w, 0.0)
                ws.append(w.astype(BF16))
            po = [jnp.dot(w, vb, preferred_element_type=F32) for w in ws]
            return (o_acc + jnp.where(first, po[0], po[1]), *accs)

        zero = jnp.zeros((tq, tk), F32)
        carry = tiles(qi, (jnp.zeros((tq, pw), F32), zero, zero), True)
        o_acc, acc_a, acc_b = lax.fori_loop(0, qi, lambda it, c: tiles(qi - 1 - it, c, False), carry)
        o_ref[...] = o_acc
        b_ref[0] = acc_a[:, 0:1]
        b_ref[1] = acc_b[:, 0:1]

    def cols(block0):
        return pl.BlockSpec((s, pw), lambda hp, i: (0, block0 + hp))

    gain = pl.BlockSpec((1, pw), lambda hp, i: (0, 0))
    return _call_with_ride_along(
        body, ride, name="attn_fwd", grid=(HEAD_PAIRS, s // tq), semantics=("parallel", "arbitrary"),
        in_specs=[pl.BlockSpec((tq, pw), lambda hp, i: (i, _Q_BLOCK0 + hp)), cols(_K_BLOCK0), cols(_V_BLOCK0),
                  gain, gain],
        out_specs=(pl.BlockSpec((tq, pw), lambda hp, i: (i, hp)), pl.BlockSpec((2, tq, 1), lambda hp, i: (hp, i, 0))),
        out_shape=(jax.ShapeDtypeStruct((s, ATTN_WIDTH), F32), jax.ShapeDtypeStruct((ATTN_HEADS, s, 1), F32)),
        scratch_shapes=[pltpu.VMEM((s, pw), BF16), pltpu.VMEM((s, pw), BF16)],
        operands=(proj, proj, proj, gq, gk))


def _attn_bwd(proj, gq, gk, bsum, do, ride=None):
    s = proj.shape[0]
    tq, tk, r = _attn_blocks(s)
    nq = s // tq
    pw = 2 * HEAD_DIM

    def body(q_ref, k_ref, v_ref, gq_ref, gk_ref, b_ref, do_ref,
             dq_ref, dk_ref, dv_ref, dgq_ref, dgk_ref, kn_ref, vb_ref, dkt_ref, dvt_ref):
        qi = pl.program_id(1)

        @pl.when(qi == 0)
        def _():
            _fill_keys(k_ref, v_ref, gk_ref, kn_ref, vb_ref, tq)
            for ref in (dkt_ref, dvt_ref, dgq_ref):
                ref[...] = jnp.zeros_like(ref)

        scale = HEAD_DIM ** -0.5
        q = q_ref[...]
        qn, rq = _pair_rms(q, gq_ref[...])
        qf = qn * scale
        qft = qf.T
        dof = do_ref[...]
        doft = dof.T
        first = _first_head((tq, pw), 1)
        first_t = _first_head((pw, tq), 0)
        tri_upto = _tri_and_ones(tk, lambda row, col: row <= col)
        tri_before = _tri_and_ones(tk, lambda row, col: row < col)
        heads = ((first, first_t), (~first, ~first_t))
        qs = [jnp.where(sel, qf, 0.0).astype(BF16) for sel, _ in heads]
        qst = [jnp.where(sel_t, qft, 0.0).astype(BF16) for _, sel_t in heads]
        dob = [jnp.where(sel, dof, 0.0).astype(BF16) for sel, _ in heads]
        dobt = [jnp.where(sel_t, doft, 0.0).astype(BF16) for _, sel_t in heads]

        def tiles(kq, carry, own):
            dq_acc = carry[0]
            rest, g_prefix = list(carry[1:3]), list(carry[3:5])
            ks = pl.ds(pl.multiple_of(kq * tq, tq), tq)
            kb, vb = kn_ref[ks, :], vb_ref[ks, :]
            scores = [_attn_scores(q, kb, own) for q in qs]
            dws = [lax.dot_general(d, vb, _NT, preferred_element_type=F32) for d in dob]
            sums = [[_split_dot(sc[1][:, u * tk:(u + 1) * tk], tri_upto) for u in range(r)] for sc in scores]
            ws, gs = [], []
            for h in range(2):
                later = [None] * r
                for u in range(r):
                    later[u] = rest[h] - sums[h][u][:, :tk]
                    rest[h] = rest[h] - sums[h][u][:, tk:]
                w = jnp.exp(scores[h][0] + jnp.concatenate(later, axis=1))
                if own:
                    w = jnp.where(scores[h][2], w, 0.0)
                ws.append(w)
                gs.append(w * dws[h])
            gsums = [[_split_dot(g[:, u * tk:(u + 1) * tk], tri_before) for u in range(r)] for g in gs]
            dzs = []
            for h in range(2):
                before = [None] * r
                for u in range(r):
                    before[u] = g_prefix[h] + gsums[h][u][:, :tk]
                    g_prefix[h] = g_prefix[h] + gsums[h][u][:, tk:]
                dz = gs[h] - jnp.exp(scores[h][0]) * (gs[h] + jnp.concatenate(before, axis=1))
                if own:
                    dz = jnp.where(scores[h][2], dz, 0.0)
                dzs.append(dz.astype(BF16))
            pq = [jnp.dot(dz, kb, preferred_element_type=F32) for dz in dzs]
            dkt_ref[:, ks] += (jnp.dot(qst[0], dzs[0], preferred_element_type=F32)
                               + jnp.dot(qst[1], dzs[1], preferred_element_type=F32))
            dvt_ref[:, ks] += (jnp.dot(dobt[0], ws[0].astype(BF16), preferred_element_type=F32)
                               + jnp.dot(dobt[1], ws[1].astype(BF16), preferred_element_type=F32))
            return (dq_acc + jnp.where(first, pq[0], pq[1]), *rest, *g_prefix)

        zero = jnp.zeros((tq, tk), F32)
        carry = (jnp.zeros((tq, pw), F32), jnp.broadcast_to(b_ref[0], (tq, tk)), jnp.broadcast_to(b_ref[1], (tq, tk)),
                 zero, zero)
        carry = lax.fori_loop(0, qi, lambda kq, c: tiles(kq, c, False), carry)
        dq, dgq = _pair_rms_bwd(q, gq_ref[...], rq, tiles(qi, carry, True)[0] * scale)
        dq_ref[...] = dq
        dgq_ref[0] += dgq

        @pl.when(qi == nq - 1)
        def _():
            def finish(c, dgk):
                rows = pl.ds(pl.multiple_of(c * tq, tq), tq)
                k = k_ref[rows, :]
                _, rk = _pair_rms(k, gk_ref[...])
                dk, dgk_c = _pair_rms_bwd(k, gk_ref[...], rk, dkt_ref[:, rows].T)
                dk_ref[rows, :] = dk
                dv_ref[rows, :] = dvt_ref[:, rows].T
                return dgk + dgk_c

            dgk_ref[0] = lax.fori_loop(0, nq, finish, jnp.zeros((1, pw), F32))

    def cols(block0):
        return pl.BlockSpec((s, pw), lambda hp, i: (0, block0 + hp))

    gain = pl.BlockSpec((1, pw), lambda hp, i: (0, 0))
    q_rows = pl.BlockSpec((tq, pw), lambda hp, i: (i, hp))
    all_rows = pl.BlockSpec((s, pw), lambda hp, i: (0, hp))
    pair_gain = pl.BlockSpec((1, 1, pw), lambda hp, i: (hp, 0, 0))
    wide = jax.ShapeDtypeStruct((s, ATTN_WIDTH), F32)
    gains = jax.ShapeDtypeStruct((HEAD_PAIRS, 1, pw), F32)
    return _call_with_ride_along(
        body, ride, name="attn_bwd", grid=(HEAD_PAIRS, nq), semantics=("parallel", "arbitrary"),
        in_specs=[pl.BlockSpec((tq, pw), lambda hp, i: (i, _Q_BLOCK0 + hp)), cols(_K_BLOCK0), cols(_V_BLOCK0),
                  gain, gain, pl.BlockSpec((2, tq, 1), lambda hp, i: (hp, i, 0)), q_rows],
        out_specs=(q_rows, all_rows, all_rows, pair_gain, pair_gain),
        out_shape=(wide, wide, wide, gains, gains),
        scratch_shapes=[pltpu.VMEM((s, pw), BF16), pltpu.VMEM((s, pw), BF16),
                        pltpu.VMEM((pw, s), F32), pltpu.VMEM((pw, s), F32)],
        operands=(proj, proj, proj, gq, gk, bsum, do))


def _reduce_adamw(parts, w, m, v, *, name, row0s=(0,)):
    n, _, cols = parts.shape
    rows = w.shape[0]
    seg_rows = rows // len(row0s)
    tile_rows = 2 * SUBLANES
    tr = max(t for t in range(tile_rows, min(seg_rows, 512) + 1, tile_rows)
             if seg_rows % t == 0 and all(r0 % t == 0 for r0 in row0s))
    seg_tiles = seg_rows // tr

    def parts_tile(i):
        tile = row0s[0] // tr + i
        for l in range(1, len(row0s)):
            tile = jnp.where(i >= l * seg_tiles, row0s[l] // tr + i - l * seg_tiles, tile)
        return tile

    c1 = 1.0 - ADAM_B1 ** ADAM_STEP
    c2 = 1.0 - ADAM_B2 ** ADAM_STEP

    def body(p_ref, w_ref, m_ref, v_ref, g_ref, d_ref, nm_ref, nv_ref):
        g = p_ref[0].astype(F32)
        for i in range(1, n):
            g = g + p_ref[i].astype(F32)
        nm = ADAM_B1 * m_ref[...] + (1.0 - ADAM_B1) * g
        nv = ADAM_B2 * v_ref[...] + (1.0 - ADAM_B2) * (g * g)
        g_ref[...] = g
        nm_ref[...] = nm
        nv_ref[...] = nv
        d_ref[...] = -ADAM_LR * ((nm / c1) / (jnp.sqrt(nv / c2) + ADAM_EPS) + ADAM_WD * w_ref[...])

    row = pl.BlockSpec((tr, cols), lambda i: (i, 0))
    out = jax.ShapeDtypeStruct((rows, cols), F32)
    return pl.pallas_call(
        body, name=name, grid=(rows // tr,),
        in_specs=[pl.BlockSpec((n, tr, cols), lambda i: (0, parts_tile(i), 0)), row, row, row],
        out_specs=(row,) * 4, out_shape=(out,) * 4, compiler_params=_params(("parallel",)))(parts, w, m, v)


def _all_gather(shards, *, name):
    n_arr = len(shards)

    def body(*refs):
        x_refs, out_refs = refs[:n_arr], refs[n_arr:2 * n_arr]
        send_sems, recv_sems, local_sems = refs[2 * n_arr:]
        x, y, c = lax.axis_index("x"), lax.axis_index("y"), lax.axis_index("c")
        me, sibling = (x, y, c), (x, y, 1 - c)
        chips = [(1 - x, y), (x, 1 - y), (1 - x, 1 - y)]

        def slot(a, px, py, pc):
            return out_refs[a].at[4 * px + 2 * py + pc]

        def copy(a, k, block, to, src=None):
            return pltpu.make_async_remote_copy(
                src_ref=slot(a, *block) if src is None else src, dst_ref=slot(a, *block),
                send_sem=send_sems.at[a, k], recv_sem=recv_sems.at[a, k], device_id=to, device_id_type=MESH_IDS)

        arrays = range(n_arr)
        mine = [pltpu.make_async_copy(x_refs[a], slot(a, *me), local_sems.at[a]) for a in arrays]
        for cp in mine:
            cp.start()
        first = [copy(a, 0, me, sibling, src=x_refs[a]) for a in arrays]
        first += [copy(a, 1 + j, me, (*chip, c), src=x_refs[a]) for j, chip in enumerate(chips) for a in arrays]
        for cp in first:
            cp.start()
        passed = []
        for j, chip in enumerate(chips):
            for a in arrays:
                copy(a, 1 + j, (*chip, c), me).wait_recv()
                passed.append(copy(a, 4 + j, (*chip, c), sibling))
                passed[-1].start()
        for a in arrays:
            copy(a, 0, sibling, me).wait_recv()
        for j, chip in enumerate(chips):
            for a in arrays:
                copy(a, 4 + j, (*chip, 1 - c), me).wait_recv()
        for cp in first + passed:
            cp.wait_send()
        for cp in mine:
            cp.wait()

    any_spec = pl.BlockSpec(memory_space=pl.ANY)
    return pl.pallas_call(
        body, name=name, out_shape=tuple(jax.ShapeDtypeStruct((N_DEV,) + s.shape, s.dtype) for s in shards),
        in_specs=[any_spec] * n_arr, out_specs=(any_spec,) * n_arr,
        scratch_shapes=[pltpu.SemaphoreType.DMA((n_arr, 7)), pltpu.SemaphoreType.DMA((n_arr, 7)),
                        pltpu.SemaphoreType.DMA((n_arr,))],
    )(*shards)


def _direct_exchange(x_refs, out_refs, send_sems, recv_sems, local_sems, *, scatter, row0s):
    arrays = range(len(x_refs))
    x, y, c = lax.axis_index("x"), lax.axis_index("y"), lax.axis_index("c")
    me = 4 * x + 2 * y + c

    def src(a, slot):
        return x_refs[a].at[slot] if scatter else x_refs[a]

    def landing(a, sender):
        return out_refs[a].at[sender, pl.ds(row0s[a], x_refs[a].shape[-2]), :]

    def local(a):
        return pltpu.make_async_copy(src(a, me), landing(a, me), local_sems.at[a])

    def copy(a, k, arriving):
        px, py, pc = x ^ (k >> 2), y ^ ((k >> 1) & 1), c ^ (k & 1)
        peer = 4 * px + 2 * py + pc
        return pltpu.make_async_remote_copy(
            src_ref=src(a, peer), dst_ref=landing(a, peer if arriving else me),
            send_sem=send_sems.at[a, k - 1], recv_sem=recv_sems.at[a, k - 1],
            device_id=(px, py, pc), device_id_type=MESH_IDS)

    def start():
        for a in arrays:
            local(a).start()
        for k in range(1, N_DEV):
            for a in arrays:
                copy(a, k, False).start()

    def wait():
        for k in range(1, N_DEV):
            for a in arrays:
                copy(a, k, True).wait_recv()
        for k in range(1, N_DEV):
            for a in arrays:
                copy(a, k, False).wait_send()
        for a in arrays:
            local(a).wait()

    return start, wait


def _exchange_sems(n_arr):
    return [pltpu.SemaphoreType.DMA((n_arr, N_DEV - 1)), pltpu.SemaphoreType.DMA((n_arr, N_DEV - 1)),
            pltpu.SemaphoreType.DMA((n_arr,))]


def _pack_small(vals, tail=0.0):
    flat = jnp.concatenate([vals[n].reshape(-1) for n in SMALL] + [jnp.full((1,), tail, F32)])
    rows = -(-flat.shape[0] // (SUBLANES * PACK_COLS)) * SUBLANES
    return jnp.pad(flat, (0, rows * PACK_COLS - flat.shape[0])).reshape(rows, PACK_COLS)


def _unpack_small(flat, shapes):
    flat = flat.reshape(-1)
    out, off = {}, 0
    for n in SMALL:
        size = math.prod(shapes[n])
        out[n] = flat[off:off + size].reshape(shapes[n])
        off += size
    return out


def _pair_gain(g):
    return jnp.tile(g, 2)[None]


def _ssm_setup(sp):
    g, p = SSM_GROUPS, SSM_STATE
    a_re = sp["ssm_a_re"][:, None, :]
    a_im = sp["ssm_a_im"][:, None, :]
    log_dt = jnp.broadcast_to(sp["ssm_log_dt"][:, None, None], (g, 1, p))
    bt_re = sp["ssm_b_re"].transpose(0, 2, 1)
    bt_im = sp["ssm_b_im"].transpose(0, 2, 1)
    return a_re, a_im, log_dt, bt_re, bt_im


def _layer_fwd(h, p_l, w, sp, ride=None):
    hn = _rms_fwd(h, sp["mix_norm_g"][None], name="rms_mix")
    proj = _mm(hn, w["w_in"], name="mm_proj")

    pw_re, pw_im, bbt_re, bbt_im = _ssm_param_fwd(*_ssm_setup(sp))
    pw_re = pw_re.transpose(1, 0, 2).reshape(SUBLANES, SSM_LANES)
    pw_im = pw_im.transpose(1, 0, 2).reshape(SUBLANES, SSM_LANES)
    ssm_mats = (_block_diag(bbt_re).astype(BF16), _block_diag(bbt_im).astype(BF16),
                _block_diag(sp["ssm_c_re"]).astype(BF16), _block_diag(-sp["ssm_c_im"]).astype(BF16),
                pw_re, pw_im, sp["ssm_d"].reshape(1, SSM_WIDTH))
    x_re, x_im, y, z = _ssm_fwd(proj, *ssm_mats)
    zz = _mm(z, w["ssm_w_glu"], name="mm_glu")

    o, bsum, *landed = _attn_fwd(proj, _pair_gain(sp["q_norm_g"]), _pair_gain(sp["k_norm_g"]), ride)

    bglu = sp["ssm_b_glu"][None]
    ycat = _gate_fwd(zz, bglu, proj, o)
    h2, hn2 = _mm_residual_norm(ycat, w["w_out"], h, sp["ple_norm_g"][None], name="mm_out")
    pp = _mm(p_l, w["w_ple_proj"], name="mm_ple_proj")
    pgl, h3 = _mm_gated_add(hn2, w["w_ple_gate"], h2, pp, name="mm_ple_gate")
    saved = dict(h=h, hn=hn, proj=proj, ssm_mats=ssm_mats, x_re=x_re, x_im=x_im, y=y, z=z,
                 zz=zz, bsum=bsum, o=o, ycat=ycat, h2=h2, hn2=hn2,
                 pgl=pgl, pp=pp, p_l=p_l, bglu=bglu)
    return h3, saved, landed


def _layer_bwd(dh3, sv, w, sp, ship=None):
    g, hh, p = SSM_GROUPS, SSM_GROUP, SSM_STATE
    grads = {}
    ship = ship or (lambda stage, grads_so_far, parts: None)
    dgp, dpp = _ple_bwd(dh3, sv["pgl"], sv["pp"])
    grads["w_ple_gate"] = _mm(sv["hn2"], dgp, ta=True, out_dtype=BF16, name="mm_d_ple_gate")
    grads["w_ple_proj"] = _mm(sv["p_l"], dpp, ta=True, out_dtype=BF16, name="mm_d_ple_proj")
    dh2, dg2 = _mm_norm_grad(dgp, w["w_ple_gate"], sv["h2"], sp["ple_norm_g"][None], dh3, name="mm_dhn2")
    grads["ple_norm_g"] = dg2[0]
    grads["w_out"] = _mm(sv["ycat"], dh2, ta=True, out_dtype=BF16, name="mm_d_out")
    dyc = _mm(dh2, w["w_out"], tb=True, name="mm_dycat")
    dzz, dgs, do, dga, dbglu = _gate_bwd(dyc, sv["zz"], sv["bglu"], sv["proj"], sv["o"])
    grads["ssm_b_glu"] = dbglu[0]

    dq, dk, dv, dgq, dgk, *parts = _attn_bwd(sv["proj"], _pair_gain(sp["q_norm_g"]), _pair_gain(sp["k_norm_g"]),
                                             sv["bsum"], do, ship("attn", grads, None))
    grads["q_norm_g"] = jnp.sum(dgq.reshape(ATTN_HEADS, HEAD_DIM), axis=0)
    grads["k_norm_g"] = jnp.sum(dgk.reshape(ATTN_HEADS, HEAD_DIM), axis=0)

    grads["ssm_w_glu"] = _mm(sv["z"], dzz, ta=True, out_dtype=BF16, name="mm_d_glu")
    dz = _mm(dzz, w["ssm_w_glu"], tb=True, name="mm_dz")
    du, dbc_re, dbc_im, dcc_re, dcc_im, gab_re, gab_im, dd, *landed = _ssm_bwd(
        dz, sv["y"], sv["proj"], sv["x_re"], sv["x_im"], *sv["ssm_mats"], ride=ship("ssm", grads, parts))
    parts = parts[:len(parts) - len(landed)] + landed
    grads["ssm_d"] = dd.reshape(g, hh)
    grads["ssm_c_re"] = _block_diag_take(dcc_re, hh, p)
    grads["ssm_c_im"] = -_block_diag_take(dcc_im, hh, p)
    da_re, da_im, dldt, dbt_re, dbt_im = _ssm_param_bwd(
        *_ssm_setup(sp), gab_re.reshape(g, 1, p), gab_im.reshape(g, 1, p),
        _block_diag_take(dbc_re, hh, p), _block_diag_take(dbc_im, hh, p))
    grads["ssm_a_re"], grads["ssm_a_im"] = da_re[:, 0, :], da_im[:, 0, :]
    grads["ssm_log_dt"] = dldt[:, 0, 0]
    grads["ssm_b_re"], grads["ssm_b_im"] = dbt_re.transpose(0, 2, 1), dbt_im.transpose(0, 2, 1)

    dproj = jnp.concatenate([du, dgs, dq, dk, dv, dga], axis=1)
    grads["w_in"] = _mm(sv["hn"], dproj, ta=True, out_dtype=BF16, name="mm_d_in")
    dh, dg1, *landed = _mm_norm_grad(dproj, w["w_in"], sv["h"], sp["mix_norm_g"][None], dh2, name="mm_dhn",
                                     ride=ship("dhn", grads, parts))
    parts = landed + parts[len(landed):]
    grads["mix_norm_g"] = dg1[0]
    return dh, grads, parts


def _local_step(x, p, target, big, small, fetch=None, ship=None):
    h = x
    big = list(big)
    saved = []
    for l in range(DEPTH):
        sp = {n: small[n][l] for n in SMALL}
        h, sv, landed = _layer_fwd(h, p[l], big[l], sp, fetch[0] if fetch and l == 0 else None)
        if landed:
            big[DEPTH - 1] = fetch[1](landed)
        saved.append(sv)
    dh, loss_parts = _loss_head(h, target)
    grads = [None] * DEPTH
    shipped = []
    for l in reversed(range(DEPTH)):
        sp = {n: small[n][l] for n in SMALL}
        carry = None
        if ship and l == 0:
            def carry(stage, grads_so_far, parts):
                return ship(stage, grads[DEPTH - 1], grads_so_far, parts)
        dh, grads[l], landed = _layer_bwd(dh, saved[l], big[l], sp, carry)
        shipped = landed or shipped
    return jnp.sum(loss_parts), dh, grads, shipped


def kernel(x, p, mix_norm_g, w_in, ssm_a_re, ssm_a_im, ssm_log_dt, ssm_b_re, ssm_b_im, ssm_c_re, ssm_c_im, ssm_d, ssm_w_glu, ssm_b_glu, q_norm_g, k_norm_g, w_out, ple_norm_g, w_ple_gate, w_ple_proj, loss_target, m_mix_norm_g, m_w_in, m_ssm_a_re, m_ssm_a_im, m_ssm_log_dt, m_ssm_b_re, m_ssm_b_im, m_ssm_c_re, m_ssm_c_im, m_ssm_d, m_ssm_w_glu, m_ssm_b_glu, m_q_norm_g, m_k_norm_g, m_w_out, m_ple_norm_g, m_w_ple_gate, m_w_ple_proj, v_mix_norm_g, v_w_in, v_ssm_a_re, v_ssm_a_im, v_ssm_log_dt, v_ssm_b_re, v_ssm_b_im, v_ssm_c_re, v_ssm_c_im, v_ssm_d, v_ssm_w_glu, v_ssm_b_glu, v_q_norm_g, v_k_norm_g, v_w_out, v_ple_norm_g, v_w_ple_gate, v_w_ple_proj):
    given = dict(locals())
    wts = {n: given[n] for n in WEIGHTS}
    mom = {n: given["m_" + n] for n in WEIGHTS}
    var = {n: given["v_" + n] for n in WEIGHTS}

    def rows2d(a):
        return a.reshape(-1, a.shape[-1])

    assert DEPTH == 2
    groups = (("w_in",), ("ssm_w_glu", "w_ple_proj"), ("w_out", "w_ple_gate"))
    layer_rows = [sum(wts[n].shape[1] for n in names) for names in groups]
    widths = [wts[names[0]].shape[2] for names in groups]

    def shards_of(l):
        return [jnp.concatenate([wts[n][l].astype(BF16) for n in names], axis=0) for names in groups]

    def weights_of(gathered):
        full = {}
        for names, got in zip(groups, gathered):
            row0 = 0
            for n in names:
                r = wts[n].shape[1]
                blocks = got[:, row0:row0 + r, :]
                full[n] = blocks.reshape(N_DEV * r, -1) if n in ROW_SHARDED else blocks.transpose(1, 0, 2).reshape(r, -1)
                row0 += r
        return full

    def blocks_to_send(layer_grads, which):
        def blocks(n):
            g = layer_grads[n]
            if n in ROW_SHARDED:
                return g.reshape(N_DEV, -1, g.shape[1])
            return g.reshape(g.shape[0], N_DEV, -1).transpose(1, 0, 2)

        return [jnp.concatenate([blocks(n) for n in groups[gi]], axis=1) for gi in which]

    fetch = (dict(bufs=shards_of(1), scatter=False, row0s=[0] * len(groups),
                  into_shapes=[jax.ShapeDtypeStruct((N_DEV, r, c), BF16) for r, c in zip(layer_rows, widths)]),
             weights_of)

    def ship(stage, last_grads, first_grads, parts):
        if stage == "attn":
            return dict(bufs=blocks_to_send(last_grads, (0, 1, 2)), scatter=True, row0s=layer_rows, into_shapes=[
                jax.ShapeDtypeStruct((N_DEV, DEPTH * r, c), BF16) for r, c in zip(layer_rows, widths)])
        which = (1, 2) if stage == "ssm" else (0,)
        return dict(bufs=blocks_to_send(first_grads, which), scatter=True, row0s=[0] * len(which),
                    into=[parts[gi] for gi in which])

    first = weights_of(_all_gather(shards_of(0), name="gather_weights"))
    small = {n: wts[n] for n in SMALL}
    loss, dx, grads, parts = _local_step(x[0], p[:, 0], loss_target[0], [first, None], small, fetch, ship)

    big_out = {}
    for names, part, rows in zip(groups, parts, layer_rows):
        row0 = 0
        for n in names:
            res = _reduce_adamw(part, rows2d(wts[n]), rows2d(mom[n]), rows2d(var[n]),
                                row0s=[l * rows + row0 for l in range(DEPTH)], name="adamw_" + n)
            big_out[n] = [a.reshape(wts[n].shape) for a in res]
            row0 += wts[n].shape[1]

    small_shapes = {n: wts[n].shape for n in SMALL}
    local_small = _pack_small({n: jnp.stack([grads[l][n] for l in range(DEPTH)]) for n in SMALL}, tail=loss)
    (all_small,) = _all_gather([local_small], name="gather_small_grads")
    small_out = _reduce_adamw(all_small, _pack_small(small), _pack_small({n: mom[n] for n in SMALL}),
                              _pack_small({n: var[n] for n in SMALL}), name="adamw_replicated")
    loss = small_out[0].reshape(-1)[sum(math.prod(s) for s in small_shapes.values())]
    small_out = [_unpack_small(a, small_shapes) for a in small_out]

    outs = [loss, dx[None]]
    for k in range(4):
        outs += [big_out[n][k] if n in BIG else small_out[k][n] for n in WEIGHTS]
    return tuple(outs)
```

```python
import math

import jax
import jax.numpy as jnp
from jax import lax
from jax.experimental import pallas as pl
from jax.experimental.pallas import tpu as pltpu

F32 = jnp.float32
BF16 = jnp.bfloat16
MESH_IDS = pl.DeviceIdType.MESH

N_DEV = 8
D_MODEL = 1024
DEPTH = 2
PLE_DIM = 256
SSM_WIDTH = 512
SSM_GROUP = 16
SSM_GROUPS = 32
SSM_STATE = 64
SSM_LANES = SSM_GROUPS * SSM_STATE
ATTN_WIDTH = 512
ATTN_HEADS = 8
HEAD_DIM = 64
RMS_EPS = 1e-6
ADAM_LR = 0.001
ADAM_B1 = 0.9
ADAM_B2 = 0.999
ADAM_EPS = 1e-08
ADAM_WD = 0.01
ADAM_STEP = 10

VMEM_LIMIT = 56 * 1024 * 1024
ATT_TQ = 512
ATT_TK = 128
SSM_BLOCK_GROUPS = 8
SSM_TIME_BLOCK = 512
SUBLANES = 8
PACK_COLS = 1024

BIG = ("w_in", "ssm_w_glu", "w_out", "w_ple_gate", "w_ple_proj")
ROW_SHARDED = ("w_out", "w_ple_gate")
SMALL = ("mix_norm_g", "ssm_a_re", "ssm_a_im", "ssm_log_dt", "ssm_b_re", "ssm_b_im", "ssm_c_re",
         "ssm_c_im", "ssm_d", "ssm_b_glu", "q_norm_g", "k_norm_g", "ple_norm_g")
WEIGHTS = ("mix_norm_g", "w_in", "ssm_a_re", "ssm_a_im", "ssm_log_dt", "ssm_b_re", "ssm_b_im",
           "ssm_c_re", "ssm_c_im", "ssm_d", "ssm_w_glu", "ssm_b_glu", "q_norm_g", "k_norm_g",
           "w_out", "ple_norm_g", "w_ple_gate", "w_ple_proj")
_NT = (((1,), (1,)), ((), ()))
_TN = (((0,), (0,)), ((), ()))


def _params(sem=None):
    return pltpu.CompilerParams(dimension_semantics=sem, vmem_limit_bytes=VMEM_LIMIT)


def _sigmoid(x):
    return 0.5 * jnp.tanh(0.5 * x) + 0.5


_GELU_K = math.sqrt(2.0 / math.pi)
_GELU_C = 0.044715


def _gelu(x):
    return 0.5 * x * (1.0 + jnp.tanh(_GELU_K * (x + _GELU_C * x * x * x)))


def _gelu_grad(x):
    th = jnp.tanh(_GELU_K * (x + _GELU_C * x * x * x))
    return 0.5 * (1.0 + th) + 0.5 * x * (1.0 - th * th) * _GELU_K * (1.0 + 3.0 * _GELU_C * x * x)


def _mm(a, b, *, name, ta=False, tb=False, add=None, out_dtype=F32, tm=1024, tn=1024, tk=1024,
        finish=None, extras=(), outs=None, ride=None):
    m, k = (a.shape[1], a.shape[0]) if ta else a.shape
    n = b.shape[0] if tb else b.shape[1]
    tm, tn, tk = min(tm, m), min(tn, n), min(tk, k)
    assert m % tm == 0 and n % tn == 0 and k % tk == 0, (name, a.shape, b.shape)
    n_steps = k // tk
    dims = (((0 if ta else 1,), (1 if tb else 0,)), ((), ()))
    a_spec = (pl.BlockSpec((tk, tm), lambda i, j, kk: (kk, i)) if ta
              else pl.BlockSpec((tm, tk), lambda i, j, kk: (i, kk)))
    b_spec = (pl.BlockSpec((tn, tk), lambda i, j, kk: (j, kk)) if tb
              else pl.BlockSpec((tk, tn), lambda i, j, kk: (kk, j)))
    o_spec = pl.BlockSpec((tm, tn), lambda i, j, kk: (i, j))
    ins, in_specs = [a, b], [a_spec, b_spec]
    if finish is None:
        if add is not None:
            ins.append(add)
            in_specs.append(o_spec)

        def finish(res, extra_refs, out_refs):
            if add is not None:
                res = res + extra_refs[0][...].astype(F32)
            out_refs[0][...] = res.astype(out_dtype)

        out_specs, out_shapes = [o_spec], [jax.ShapeDtypeStruct((m, n), out_dtype)]
    else:
        assert tn == n and add is None
        for arr, kind in extras:
            ins.append(arr)
            cols = arr.shape[1]
            in_specs.append(pl.BlockSpec((tm, cols), lambda i, j, kk: (i, 0)) if kind == "rows"
                            else pl.BlockSpec((1, cols), lambda i, j, kk: (0, 0)))
        out_specs = [pl.BlockSpec((tm, cols), lambda i, j, kk: (i, 0)) if kind == "rows"
                     else pl.BlockSpec((1, cols), lambda i, j, kk: (0, 0)) for cols, _, kind in outs]
        out_shapes = [jax.ShapeDtypeStruct((m if kind == "rows" else 1, cols), dtype) for cols, dtype, kind in outs]
    n_in = len(ins)

    def body(*refs):
        a_ref, b_ref, acc_ref = refs[0], refs[1], refs[-1]
        kk = pl.program_id(2)

        @pl.when(kk == 0)
        def _():
            acc_ref[...] = jnp.zeros_like(acc_ref)

        acc_ref[...] += lax.dot_general(a_ref[...].astype(BF16), b_ref[...].astype(BF16), dims,
                                        preferred_element_type=F32)

        @pl.when(kk == n_steps - 1)
        def _():
            finish(acc_ref[...], refs[2:n_in], refs[n_in:-1])

    sums = outs is not None and any(kind == "sum" for _, _, kind in outs)
    res = _call_with_ride_along(
        body, ride, name=name, grid=(m // tm, n // tn, n_steps),
        semantics=("arbitrary",) * 3 if sums else ("parallel", "parallel", "arbitrary"),
        in_specs=in_specs, out_specs=tuple(out_specs), out_shape=tuple(out_shapes),
        scratch_shapes=[pltpu.VMEM((tm, tn), F32)], operands=ins)
    return res[0] if outs is None and ride is None else res


def _rms(hv, g):
    r = lax.rsqrt(jnp.mean(hv * hv, axis=-1, keepdims=True) + RMS_EPS)
    return hv * r * g, r


def _rms_grad(hv, g, dn):
    _, r = _rms(hv, g)
    a = dn * g
    dot = jnp.mean(a * hv, axis=-1, keepdims=True)
    return r * a - hv * (r * r * r * dot), jnp.sum(dn * hv * r, axis=0, keepdims=True)


def _mm_residual_norm(a, b, h, g, *, name):
    def finish(res, extra_refs, out_refs):
        h_ref, g_ref = extra_refs
        h2 = res + h_ref[...]
        out_refs[0][...] = h2
        out_refs[1][...] = _rms(h2, g_ref[...])[0].astype(BF16)

    d = b.shape[1]
    return _mm(a, b, name=name, finish=finish, extras=[(h, "rows"), (g, "vec")],
               outs=[(d, F32, "rows"), (d, BF16, "rows")])


def _mm_gated_add(a, b, h2, pp, *, name):
    def finish(res, extra_refs, out_refs):
        h_ref, p_ref = extra_refs
        out_refs[0][...] = res
        out_refs[1][...] = h_ref[...] + _sigmoid(res) * p_ref[...]

    d = b.shape[1]
    return _mm(a, b, name=name, finish=finish, extras=[(h2, "rows"), (pp, "rows")],
               outs=[(d, F32, "rows"), (d, F32, "rows")])


def _mm_norm_grad(dy, w, h, g, dres, *, name, ride=None):
    def finish(res, extra_refs, out_refs):
        h_ref, g_ref, r_ref = extra_refs
        dh, dg = _rms_grad(h_ref[...], g_ref[...], res)
        out_refs[0][...] = dh + r_ref[...]

        @pl.when(pl.program_id(0) == 0)
        def _():
            out_refs[1][...] = jnp.zeros_like(out_refs[1])

        out_refs[1][...] += dg

    d = w.shape[0]
    return _mm(dy, w, name=name, tb=True, finish=finish, extras=[(h, "rows"), (g, "vec"), (dres, "rows")],
               outs=[(d, F32, "rows"), (d, F32, "sum")], ride=ride)


def _rows_tile(rows, want=512):
    t = min(rows, want)
    assert rows % t == 0
    return t


def _rowwise(body, name, rows, ins, outs):
    widest = max(cols for _, cols, _ in ins)
    tm = _rows_tile(rows, 512 if widest > 128 else 4096)
    in_specs = []
    for _, cols, cb in ins:
        if cb is None:
            in_specs.append(pl.BlockSpec((1, cols), lambda i: (0, 0)))
        else:
            in_specs.append(pl.BlockSpec((tm, cols), lambda i, cb=cb: (i, cb)))
    out_specs, out_shapes = [], []
    for cols, dtype, is_acc in outs:
        if is_acc:
            out_specs.append(pl.BlockSpec((1, cols), lambda i: (0, 0)))
            out_shapes.append(jax.ShapeDtypeStruct((1, cols), dtype))
        else:
            out_specs.append(pl.BlockSpec((tm, cols), lambda i: (i, 0)))
            out_shapes.append(jax.ShapeDtypeStruct((rows, cols), dtype))
    any_acc = any(o[2] for o in outs)
    return pl.pallas_call(
        body, name=name, grid=(rows // tm,), in_specs=in_specs, out_specs=tuple(out_specs),
        out_shape=tuple(out_shapes),
        compiler_params=_params(("arbitrary",) if any_acc else ("parallel",)))(*[a for a, _, _ in ins])


def _accumulate(ref, value):
    @pl.when(pl.program_id(0) == 0)
    def _():
        ref[...] = jnp.zeros_like(ref)

    ref[...] += value


def _rms_fwd(h, g, *, name):
    rows, d = h.shape

    def body(h_ref, g_ref, o_ref):
        o_ref[...] = _rms(h_ref[...], g_ref[...])[0].astype(BF16)

    return _rowwise(body, name, rows, [(h, d, 0), (g, d, None)], [(d, BF16, False)])[0]


def _gate_fwd(zz, bglu, proj, o):
    rows = zz.shape[0]
    w = SSM_WIDTH

    def body(zz_ref, b_ref, gs_ref, o_ref, ga_ref, y_ref):
        zz_v = zz_ref[...] + b_ref[...]
        val, gate = zz_v[:, :w], zz_v[:, w:]
        gs, ga = gs_ref[...], ga_ref[...]
        y_ref[:, :w] = (val * _sigmoid(gate) * (gs * _sigmoid(gs))).astype(BF16)
        y_ref[:, w:] = (o_ref[...] * (ga * _sigmoid(ga))).astype(BF16)

    return _rowwise(body, "gate_fwd", rows,
                    [(zz, 2 * w, 0), (bglu, 2 * w, None), (proj, w, 1), (o, w, 0), (proj, w, 5)],
                    [(2 * w, BF16, False)])[0]


def _gate_bwd(dyc, zz, bglu, proj, o):
    rows = zz.shape[0]
    w = SSM_WIDTH

    def body(dy_ref, zz_ref, b_ref, gs_ref, o_ref, ga_ref, dzz_ref, dgs_ref, do_ref, dga_ref, db_ref):
        zz_v = zz_ref[...] + b_ref[...]
        val, gate = zz_v[:, :w], zz_v[:, w:]
        gs, ga = gs_ref[...], ga_ref[...]
        dys, dya = dy_ref[:, :w], dy_ref[:, w:]
        sg, ss, sa = _sigmoid(gate), _sigmoid(gs), _sigmoid(ga)
        glu = val * sg
        dglu = dys * (gs * ss)
        dgs_ref[...] = dys * glu * (ss * (1.0 + gs * (1.0 - ss)))
        dval = dglu * sg
        dgate = dglu * val * sg * (1.0 - sg)
        dzz_ref[:, :w] = dval.astype(BF16)
        dzz_ref[:, w:] = dgate.astype(BF16)
        do_ref[...] = dya * (ga * sa)
        dga_ref[...] = dya * o_ref[...] * (sa * (1.0 + ga * (1.0 - sa)))
        _accumulate(db_ref, jnp.concatenate([jnp.sum(dval, axis=0, keepdims=True),
                                             jnp.sum(dgate, axis=0, keepdims=True)], axis=1))

    return _rowwise(body, "gate_bwd", rows,
                    [(dyc, 2 * w, 0), (zz, 2 * w, 0), (bglu, 2 * w, None), (proj, w, 1), (o, w, 0), (proj, w, 5)],
                    [(2 * w, BF16, False), (w, F32, False), (w, F32, False), (w, F32, False), (2 * w, F32, True)])


def _ple_bwd(dh3, pgl, pp):
    rows, d = dh3.shape

    def body(dh_ref, g_ref, p_ref, dg_ref, dp_ref):
        sg = _sigmoid(g_ref[...])
        dh = dh_ref[...]
        dg_ref[...] = (dh * p_ref[...] * sg * (1.0 - sg)).astype(BF16)
        dp_ref[...] = (dh * sg).astype(BF16)

    return _rowwise(body, "ple_bwd", rows, [(dh3, d, 0), (pgl, d, 0), (pp, d, 0)],
                    [(d, BF16, False), (d, BF16, False)])


def _loss_head(h, target):
    rows, d = h.shape

    def body(h_ref, t_ref, dh_ref, l_ref):
        err = h_ref[...] - t_ref[...]
        dh_ref[...] = err * (1.0 / d)
        _accumulate(l_ref, jnp.sum(err * err, axis=0, keepdims=True) * (0.5 / d))

    return _rowwise(body, "loss_head", rows, [(h, d, 0), (target, d, 0)], [(d, F32, False), (d, F32, True)])


def _zoh(lr, li, dt):
    mag = jnp.exp(lr * dt)
    ab_re = mag * jnp.cos(li * dt)
    ab_im = mag * jnp.sin(li * dt)
    num_re = ab_re - 1.0
    den = lr * lr + li * li
    f_re = (num_re * lr + ab_im * li) / den
    f_im = (ab_im * lr - num_re * li) / den
    return ab_re, ab_im, f_re, f_im, den


def _ssm_param_fwd(a_re, a_im, log_dt, bt_re, bt_im):
    g, _, p = a_re.shape
    h = bt_re.shape[1]

    def body(lr_ref, li_ref, ldt_ref, br_ref, bi_ref, pr_ref, pi_ref, bbr_ref, bbi_ref):
        dt = jnp.exp(ldt_ref[...])
        ab_re, ab_im, f_re, f_im, _ = _zoh(lr_ref[...], li_ref[...], dt)
        cr, ci = ab_re, ab_im
        for k in range(SUBLANES):
            pr_ref[:, k:k + 1, :] = cr
            pi_ref[:, k:k + 1, :] = ci
            cr, ci = cr * ab_re - ci * ab_im, cr * ab_im + ci * ab_re
        br, bi = br_ref[...], bi_ref[...]
        bbr_ref[...] = f_re * br - f_im * bi
        bbi_ref[...] = f_re * bi + f_im * br

    pw = jax.ShapeDtypeStruct((g, SUBLANES, p), F32)
    large = jax.ShapeDtypeStruct((g, h, p), F32)
    return pl.pallas_call(body, name="ssm_param_fwd", out_shape=(pw, pw, large, large),
                          compiler_params=_params())(a_re, a_im, log_dt, bt_re, bt_im)


def _ssm_param_bwd(a_re, a_im, log_dt, bt_re, bt_im, gab_re, gab_im, gbb_re, gbb_im):
    g, _, p = a_re.shape
    h = bt_re.shape[1]

    def body(lr_ref, li_ref, ldt_ref, br_ref, bi_ref, gar_ref, gai_ref, gbr_ref, gbi_ref,
             dlr_ref, dli_ref, dldt_ref, dbr_ref, dbi_ref):
        lr, li = lr_ref[...], li_ref[...]
        dt = jnp.exp(ldt_ref[...])
        ab_re, ab_im, f_re, f_im, den = _zoh(lr, li, dt)
        br, bi = br_ref[...], bi_ref[...]
        gbr, gbi = gbr_ref[...], gbi_ref[...]
        dbr_ref[...] = f_re * gbr + f_im * gbi
        dbi_ref[...] = f_re * gbi - f_im * gbr
        gf_re = jnp.sum(br * gbr + bi * gbi, axis=1, keepdims=True)
        gf_im = jnp.sum(br * gbi - bi * gbr, axis=1, keepdims=True)
        il_re, il_im = lr / den, -li / den
        ga_re = gar_ref[...] + il_re * gf_re + il_im * gf_im
        ga_im = gai_ref[...] + il_re * gf_im - il_im * gf_re
        q_re = f_re * il_re - f_im * il_im
        q_im = f_re * il_im + f_im * il_re
        gl_re = -(q_re * gf_re + q_im * gf_im)
        gl_im = -(q_re * gf_im - q_im * gf_re)
        dlr_ref[...] = gl_re + dt * (ab_re * ga_re + ab_im * ga_im)
        dli_ref[...] = gl_im + dt * (ab_re * ga_im - ab_im * ga_re)
        la_re = lr * ab_re - li * ab_im
        la_im = lr * ab_im + li * ab_re
        dldt_ref[...] = jnp.sum((la_re * ga_re + la_im * ga_im) * dt, axis=2, keepdims=True)

    small = jax.ShapeDtypeStruct((g, 1, p), F32)
    one = jax.ShapeDtypeStruct((g, 1, 1), F32)
    large = jax.ShapeDtypeStruct((g, h, p), F32)
    return pl.pallas_call(body, name="ssm_param_bwd", out_shape=(small, small, one, large, large),
                          compiler_params=_params())(
                              a_re, a_im, log_dt, bt_re, bt_im, gab_re, gab_im, gbb_re, gbb_im)


def _block_diag(m):
    g, h, p = m.shape
    nb = g // SSM_BLOCK_GROUPS
    eye = jnp.eye(SSM_BLOCK_GROUPS, dtype=m.dtype)
    m4 = m.reshape(nb, SSM_BLOCK_GROUPS, h, p)
    return (m4[:, :, :, None, :] * eye[None, :, None, :, None]).reshape(nb, SSM_BLOCK_GROUPS * h, SSM_BLOCK_GROUPS * p)


def _block_diag_take(m, h, p):
    nb = m.shape[0]
    eye = jnp.eye(SSM_BLOCK_GROUPS, dtype=m.dtype)
    m5 = m.reshape(nb, SSM_BLOCK_GROUPS, h, SSM_BLOCK_GROUPS, p)
    return jnp.sum(m5 * eye[None, :, None, :, None], axis=3).reshape(nb * SSM_BLOCK_GROUPS, h, p)


def _scan_tile(br, bi, pw_re, pw_im, reverse):
    row = lax.broadcasted_iota(jnp.int32, br.shape, 0)
    xr, xi = br, bi
    for d in (1, 2, 4):
        ar, ai = pw_re[d - 1:d, :], pw_im[d - 1:d, :]
        if reverse:
            keep = row < SUBLANES - d
            sr = jnp.where(keep, pltpu.roll(xr, SUBLANES - d, 0), 0.0)
            si = jnp.where(keep, pltpu.roll(xi, SUBLANES - d, 0), 0.0)
        else:
            keep = row >= d
            sr = jnp.where(keep, pltpu.roll(xr, d, 0), 0.0)
            si = jnp.where(keep, pltpu.roll(xi, d, 0), 0.0)
        xr, xi = xr + ar * sr - ai * si, xi + ar * si + ai * sr
    return xr, xi


def _ssm_blocks(s):
    tt = min(SSM_TIME_BLOCK, s)
    assert s % tt == 0 and tt % SUBLANES == 0
    ch = SSM_BLOCK_GROUPS * SSM_GROUP
    st = SSM_BLOCK_GROUPS * SSM_STATE
    return tt, s // tt, SSM_WIDTH // ch, ch, st


def _ssm_fwd(proj, bc_re, bc_im, cc_re, cc_im, pw_re, pw_im, dvec):
    s = proj.shape[0]
    tt, nt, nb, ch, st = _ssm_blocks(s)
    n_tiles = tt // SUBLANES

    def body(u_ref, bre_ref, bim_ref, cre_ref, cim_ref, pr_ref, pi_ref, d_ref,
             xr_ref, xi_ref, y_ref, z_ref, carry_ref):
        @pl.when(pl.program_id(1) == 0)
        def _():
            carry_ref[...] = jnp.zeros_like(carry_ref)

        u = u_ref[...]
        ub = u.astype(BF16)
        xr_ref[...] = jnp.dot(ub, bre_ref[0], preferred_element_type=F32)
        xi_ref[...] = jnp.dot(ub, bim_ref[0], preferred_element_type=F32)
        pw_r, pw_i = pr_ref[...], pi_ref[...]

        def step(t, carry):
            cr, ci = carry
            rows = pl.ds(pl.multiple_of(t * SUBLANES, SUBLANES), SUBLANES)
            xr, xi = _scan_tile(xr_ref[rows, :], xi_ref[rows, :], pw_r, pw_i, False)
            xr, xi = xr + pw_r * cr - pw_i * ci, xi + pw_r * ci + pw_i * cr
            xr_ref[rows, :] = xr
            xi_ref[rows, :] = xi
            return (jnp.broadcast_to(xr[SUBLANES - 1:SUBLANES, :], (SUBLANES, st)),
                    jnp.broadcast_to(xi[SUBLANES - 1:SUBLANES, :], (SUBLANES, st)))

        cr, ci = lax.fori_loop(0, n_tiles, step, (carry_ref[0], carry_ref[1]), unroll=2)
        carry_ref[0] = cr
        carry_ref[1] = ci
        y = (lax.dot_general(xr_ref[...].astype(BF16), cre_ref[0], _NT, preferred_element_type=F32)
             + lax.dot_general(xi_ref[...].astype(BF16), cim_ref[0], _NT, preferred_element_type=F32)
             + d_ref[...] * u)
        y_ref[...] = y
        z_ref[...] = _gelu(y).astype(BF16)

    chan = pl.BlockSpec((tt, ch), lambda b, t: (t, b))
    state = pl.BlockSpec((tt, st), lambda b, t: (t, b))
    mat = pl.BlockSpec((1, ch, st), lambda b, t: (b, 0, 0))
    pw = pl.BlockSpec((SUBLANES, st), lambda b, t: (0, b))
    vec = pl.BlockSpec((1, ch), lambda b, t: (0, b))
    x_shape = jax.ShapeDtypeStruct((s, nb * st), F32)
    return pl.pallas_call(
        body, name="ssm_fwd", grid=(nb, nt), in_specs=[chan, mat, mat, mat, mat, pw, pw, vec],
        out_specs=(state, state, chan, chan),
        out_shape=(x_shape, x_shape, jax.ShapeDtypeStruct((s, nb * ch), F32), jax.ShapeDtypeStruct((s, nb * ch), BF16)),
        scratch_shapes=[pltpu.VMEM((2, SUBLANES, st), F32)],
        compiler_params=_params(("parallel", "arbitrary")))(proj, bc_re, bc_im, cc_re, cc_im, pw_re, pw_im, dvec)


def _ssm_bwd(dz, y, proj, x_re, x_im, bc_re, bc_im, cc_re, cc_im, pw_re, pw_im, dvec, ride=None):
    s = proj.shape[0]
    tt, nt, nb, ch, st = _ssm_blocks(s)
    n_tiles = tt // SUBLANES

    def body(dz_ref, y_ref, u_ref, xr_ref, xi_ref, bre_ref, bim_ref, cre_ref, cim_ref, pr_ref, pi_ref, d_ref,
             du_ref, dbr_ref, dbi_ref, dcr_ref, dci_ref, gar_ref, gai_ref, dd_ref,
             lr_ref, li_ref, carry_ref, acc_ref):
        @pl.when(pl.program_id(1) == 0)
        def _():
            for ref in (carry_ref, acc_ref, dbr_ref, dbi_ref, dcr_ref, dci_ref, dd_ref):
                ref[...] = jnp.zeros_like(ref)

        u = u_ref[...]
        dy = dz_ref[...] * _gelu_grad(y_ref[...])
        ub, dyb = u.astype(BF16), dy.astype(BF16)
        lr_ref[...] = jnp.dot(dyb, cre_ref[0], preferred_element_type=F32)
        li_ref[...] = jnp.dot(dyb, cim_ref[0], preferred_element_type=F32)
        pw_r, pw_i = pr_ref[...], -pi_ref[...]
        pwc_r = jnp.concatenate([pw_r[SUBLANES - 1 - i:SUBLANES - i, :] for i in range(SUBLANES)], axis=0)
        pwc_i = jnp.concatenate([pw_i[SUBLANES - 1 - i:SUBLANES - i, :] for i in range(SUBLANES)], axis=0)
        row = lax.broadcasted_iota(jnp.int32, (SUBLANES, st), 0)

        def step(t, carry):
            cr, ci, acc_r, acc_i = carry
            rows = pl.ds(pl.multiple_of((n_tiles - 1 - t) * SUBLANES, SUBLANES), SUBLANES)
            lr, li = _scan_tile(lr_ref[rows, :], li_ref[rows, :], pw_r, pw_i, True)
            lr, li = lr + pwc_r * cr - pwc_i * ci, li + pwc_r * ci + pwc_i * cr
            lr_ref[rows, :] = lr
            li_ref[rows, :] = li
            nr = jnp.where(row < SUBLANES - 1, pltpu.roll(lr, SUBLANES - 1, 0), cr)
            ni = jnp.where(row < SUBLANES - 1, pltpu.roll(li, SUBLANES - 1, 0), ci)
            xr, xi = xr_ref[rows, :], xi_ref[rows, :]
            return (jnp.broadcast_to(lr[0:1, :], (SUBLANES, st)), jnp.broadcast_to(li[0:1, :], (SUBLANES, st)),
                    acc_r + xr * nr + xi * ni, acc_i + xr * ni - xi * nr)

        cr, ci, acc_r, acc_i = lax.fori_loop(
            0, n_tiles, step, (carry_ref[0], carry_ref[1], acc_ref[0], acc_ref[1]), unroll=2)
        carry_ref[0], carry_ref[1] = cr, ci
        acc_ref[0], acc_ref[1] = acc_r, acc_i
        lrb, lib = lr_ref[...].astype(BF16), li_ref[...].astype(BF16)
        du_ref[...] = (lax.dot_general(lrb, bre_ref[0], _NT, preferred_element_type=F32)
                       + lax.dot_general(lib, bim_ref[0], _NT, preferred_element_type=F32) + dy * d_ref[...])
        dbr_ref[0] += lax.dot_general(ub, lrb, _TN, preferred_element_type=F32)
        dbi_ref[0] += lax.dot_general(ub, lib, _TN, preferred_element_type=F32)
        dcr_ref[0] += lax.dot_general(dyb, xr_ref[...].astype(BF16), _TN, preferred_element_type=F32)
        dci_ref[0] += lax.dot_general(dyb, xi_ref[...].astype(BF16), _TN, preferred_element_type=F32)
        dd_ref[...] += jnp.sum(dy * u, axis=0, keepdims=True)

        @pl.when(pl.program_id(1) == nt - 1)
        def _():
            gar_ref[...] = jnp.sum(acc_r, axis=0, keepdims=True)
            gai_ref[...] = jnp.sum(acc_i, axis=0, keepdims=True)

    chan = pl.BlockSpec((tt, ch), lambda b, t: (nt - 1 - t, b))
    state = pl.BlockSpec((tt, st), lambda b, t: (nt - 1 - t, b))
    mat = pl.BlockSpec((1, ch, st), lambda b, t: (b, 0, 0))
    pw = pl.BlockSpec((SUBLANES, st), lambda b, t: (0, b))
    vec = pl.BlockSpec((1, ch), lambda b, t: (0, b))
    svec = pl.BlockSpec((1, st), lambda b, t: (0, b))
    mat_shape = jax.ShapeDtypeStruct((nb, ch, st), F32)
    return _call_with_ride_along(
        body, ride, name="ssm_bwd", grid=(nb, nt), semantics=("parallel", "arbitrary"),
        in_specs=[chan, chan, chan, state, state, mat, mat, mat, mat, pw, pw, vec],
        out_specs=(chan, mat, mat, mat, mat, svec, svec, vec),
        out_shape=(jax.ShapeDtypeStruct((s, nb * ch), F32), mat_shape, mat_shape, mat_shape, mat_shape,
                   jax.ShapeDtypeStruct((1, nb * st), F32), jax.ShapeDtypeStruct((1, nb * st), F32),
                   jax.ShapeDtypeStruct((1, nb * ch), F32)),
        scratch_shapes=[pltpu.VMEM((tt, st), F32), pltpu.VMEM((tt, st), F32),
                        pltpu.VMEM((2, SUBLANES, st), F32), pltpu.VMEM((2, SUBLANES, st), F32)],
        operands=(dz, y, proj, x_re, x_im, bc_re, bc_im, cc_re, cc_im, pw_re, pw_im, dvec))


def _split_dot(x, tri2):
    hi = x.astype(BF16)
    lo = (x - hi.astype(F32)).astype(BF16)
    return jnp.dot(jnp.concatenate([hi, lo], axis=1), tri2, preferred_element_type=F32)


def _tri_and_ones(tk, rel):
    r_i = lax.broadcasted_iota(jnp.int32, (2 * tk, 2 * tk), 0) % tk
    c_i = lax.broadcasted_iota(jnp.int32, (2 * tk, 2 * tk), 1)
    return jnp.where((c_i >= tk) | rel(r_i, c_i), 1.0, 0.0).astype(BF16)


def _attn_scores(qs, kb, own):
    z = lax.dot_general(qs, kb, _NT, preferred_element_type=F32)
    log_beta = jnp.minimum(z, 0.0) - jnp.log(1.0 + jnp.exp(-jnp.abs(z)))
    log_stay = log_beta - z
    if not own:
        return log_beta, log_stay, None
    mask = lax.broadcasted_iota(jnp.int32, z.shape, 1) < lax.broadcasted_iota(jnp.int32, z.shape, 0)
    return log_beta, jnp.where(mask, log_stay, 0.0), mask


def _attn_blocks(s):
    tq, tk = min(ATT_TQ, s), min(ATT_TK, s)
    assert s % tq == 0 and tq % tk == 0
    return tq, tk, tq // tk


def _first_head(shape, axis):
    return lax.broadcasted_iota(jnp.int32, shape, axis) < HEAD_DIM


def _pair_rms(x, g):
    first = _first_head(x.shape, 1)
    sq = x * x
    ms_a = jnp.sum(jnp.where(first, sq, 0.0), axis=1, keepdims=True) * (1.0 / HEAD_DIM)
    ms_b = jnp.sum(jnp.where(first, 0.0, sq), axis=1, keepdims=True) * (1.0 / HEAD_DIM)
    r = jnp.where(first, lax.rsqrt(ms_a + RMS_EPS), lax.rsqrt(ms_b + RMS_EPS))
    return x * r * g, r


def _pair_rms_bwd(x, g, r, dn):
    first = _first_head(x.shape, 1)
    a = dn * g
    ax = a * x
    dot_a = jnp.sum(jnp.where(first, ax, 0.0), axis=1, keepdims=True) * (1.0 / HEAD_DIM)
    dot_b = jnp.sum(jnp.where(first, 0.0, ax), axis=1, keepdims=True) * (1.0 / HEAD_DIM)
    dx = r * a - x * (r * r * r * jnp.where(first, dot_a, dot_b))
    return dx, jnp.sum(dn * x * r, axis=0, keepdims=True)


_Q_BLOCK0 = 2 * SSM_WIDTH // (2 * HEAD_DIM)
_K_BLOCK0 = _Q_BLOCK0 + ATTN_WIDTH // (2 * HEAD_DIM)
_V_BLOCK0 = _K_BLOCK0 + ATTN_WIDTH // (2 * HEAD_DIM)
HEAD_PAIRS = ATTN_HEADS // 2


def _fill_keys(k_ref, v_ref, gk_ref, kn_ref, vb_ref, rows_per_step):
    def fill(c, _):
        rows = pl.ds(pl.multiple_of(c * rows_per_step, rows_per_step), rows_per_step)
        kn, _ = _pair_rms(k_ref[rows, :], gk_ref[...])
        kn_ref[rows, :] = kn.astype(BF16)
        vb_ref[rows, :] = v_ref[rows, :].astype(BF16)
        return 0

    lax.fori_loop(0, k_ref.shape[0] // rows_per_step, fill, 0)


def _call_with_ride_along(body, ride, *, name, grid, semantics, in_specs, out_specs, out_shape, scratch_shapes,
                          operands):
    if ride is None:
        return pl.pallas_call(body, name=name, grid=grid, in_specs=in_specs, out_specs=out_specs, out_shape=out_shape,
                              scratch_shapes=scratch_shapes, compiler_params=_params(semantics))(*operands)
    n, n_in, n_out = len(ride["bufs"]), len(in_specs), len(out_specs)
    into = list(ride.get("into", ()))
    first_out = n_in + n + len(into)
    last = math.prod(grid) - 1

    def wrapped(*refs):
        ins, x_refs = refs[:n_in], refs[n_in:n_in + n]
        outs, out_refs = refs[first_out:first_out + n_out], refs[first_out + n_out:first_out + n_out + n]
        scratch, sems = refs[first_out + n_out + n:-3], refs[-3:]
        step = 0
        for axis, size in enumerate(grid):
            step = step * size + pl.program_id(axis)

        def exchange():
            return _direct_exchange(x_refs, out_refs, *sems, scatter=ride["scatter"], row0s=ride["row0s"])

        @pl.when(step == 0)
        def _():
            exchange()[0]()

        body(*ins, *outs, *scratch)

        @pl.when(step == last)
        def _():
            exchange()[1]()

    any_spec = pl.BlockSpec(memory_space=pl.ANY)
    landing = [jax.ShapeDtypeStruct(b.shape, b.dtype) for b in into] or list(ride["into_shapes"])
    return pl.pallas_call(
        wrapped, name=name, grid=grid, in_specs=list(in_specs) + [any_spec] * (n + len(into)),
        out_specs=tuple(out_specs) + (any_spec,) * n, out_shape=tuple(out_shape) + tuple(landing),
        input_output_aliases={n_in + n + a: n_out + a for a in range(len(into))},
        scratch_shapes=list(scratch_shapes) + _exchange_sems(n),
        compiler_params=_params(("arbitrary",) * len(grid)))(*operands, *ride["bufs"], *into)


def _attn_fwd(proj, gq, gk, ride=None):
    s = proj.shape[0]
    tq, tk, r = _attn_blocks(s)
    pw = 2 * HEAD_DIM

    def body(q_ref, k_ref, v_ref, gq_ref, gk_ref, o_ref, b_ref, kn_ref, vb_ref):
        qi = pl.program_id(1)

        @pl.when(qi == 0)
        def _():
            _fill_keys(k_ref, v_ref, gk_ref, kn_ref, vb_ref, tq)

        qn, _ = _pair_rms(q_ref[...], gq_ref[...])
        qn = qn * (HEAD_DIM ** -0.5)
        first = _first_head((tq, pw), 1)
        tri = _tri_and_ones(tk, lambda row, col: row > col)
        qs = [jnp.where(first, qn, 0.0).astype(BF16), jnp.where(first, 0.0, qn).astype(BF16)]

        def tiles(kq, carry, own):
            o_acc, accs = carry[0], list(carry[1:])
            ks = pl.ds(pl.multiple_of(kq * tq, tq), tq)
            kb, vb = kn_ref[ks, :], vb_ref[ks, :]
            scores = [_attn_scores(q, kb, own) for q in qs]
            sums = [[_split_dot(sc[1][:, u * tk:(u + 1) * tk], tri) for u in range(r)] for sc in scores]
            ws = []
            for h in range(2):
                later = [None] * r
                for u in reversed(range(r)):
                    later[u] = accs[h] + sums[h][u][:, :tk]
                    accs[h] = accs[h] + sums[h][u][:, tk:]
                w = jnp.exp(scores[h][0] + jnp.concatenate(later, axis=1))
                if own:
                    w = jnp.where(scores[h][2], w, 0.0)
                ws.append(w.astype(BF16))
            po = [jnp.dot(w, vb, preferred_element_type=F32) for w in ws]
            return (o_acc + jnp.where(first, po[0], po[1]), *accs)

        zero = jnp.zeros((tq, tk), F32)
        carry = tiles(qi, (jnp.zeros((tq, pw), F32), zero, zero), True)
        o_acc, acc_a, acc_b = lax.fori_loop(0, qi, lambda it, c: tiles(qi - 1 - it, c, False), carry)
        o_ref[...] = o_acc
        b_ref[0] = acc_a[:, 0:1]
        b_ref[1] = acc_b[:, 0:1]

    def cols(block0):
        return pl.BlockSpec((s, pw), lambda hp, i: (0, block0 + hp))

    gain = pl.BlockSpec((1, pw), lambda hp, i: (0, 0))
    return _call_with_ride_along(
        body, ride, name="attn_fwd", grid=(HEAD_PAIRS, s // tq), semantics=("parallel", "arbitrary"),
        in_specs=[pl.BlockSpec((tq, pw), lambda hp, i: (i, _Q_BLOCK0 + hp)), cols(_K_BLOCK0), cols(_V_BLOCK0),
                  gain, gain],
        out_specs=(pl.BlockSpec((tq, pw), lambda hp, i: (i, hp)), pl.BlockSpec((2, tq, 1), lambda hp, i: (hp, i, 0))),
        out_shape=(jax.ShapeDtypeStruct((s, ATTN_WIDTH), F32), jax.ShapeDtypeStruct((ATTN_HEADS, s, 1), F32)),
        scratch_shapes=[pltpu.VMEM((s, pw), BF16), pltpu.VMEM((s, pw), BF16)],
        operands=(proj, proj, proj, gq, gk))


def _attn_bwd(proj, gq, gk, bsum, do, ride=None):
    s = proj.shape[0]
    tq, tk, r = _attn_blocks(s)
    nq = s // tq
    pw = 2 * HEAD_DIM

    def body(q_ref, k_ref, v_ref, gq_ref, gk_ref, b_ref, do_ref,
             dq_ref, dk_ref, dv_ref, dgq_ref, dgk_ref, kn_ref, vb_ref, dkt_ref, dvt_ref):
        qi = pl.program_id(1)

        @pl.when(qi == 0)
        def _():
            _fill_keys(k_ref, v_ref, gk_ref, kn_ref, vb_ref, tq)
            for ref in (dkt_ref, dvt_ref, dgq_ref):
                ref[...] = jnp.zeros_like(ref)

        scale = HEAD_DIM ** -0.5
        q = q_ref[...]
        qn, rq = _pair_rms(q, gq_ref[...])
        qf = qn * scale
        qft = qf.T
        dof = do_ref[...]
        doft = dof.T
        first = _first_head((tq, pw), 1)
        first_t = _first_head((pw, tq), 0)
        tri_upto = _tri_and_ones(tk, lambda row, col: row <= col)
        tri_before = _tri_and_ones(tk, lambda row, col: row < col)
        heads = ((first, first_t), (~first, ~first_t))
        qs = [jnp.where(sel, qf, 0.0).astype(BF16) for sel, _ in heads]
        qst = [jnp.where(sel_t, qft, 0.0).astype(BF16) for _, sel_t in heads]
        dob = [jnp.where(sel, dof, 0.0).astype(BF16) for sel, _ in heads]
        dobt = [jnp.where(sel_t, doft, 0.0).astype(BF16) for _, sel_t in heads]

        def tiles(kq, carry, own):
            dq_acc = carry[0]
            rest, g_prefix = list(carry[1:3]), list(carry[3:5])
            ks = pl.ds(pl.multiple_of(kq * tq, tq), tq)
            kb, vb = kn_ref[ks, :], vb_ref[ks, :]
            scores = [_attn_scores(q, kb, own) for q in qs]
            dws = [lax.dot_general(d, vb, _NT, preferred_element_type=F32) for d in dob]
            sums = [[_split_dot(sc[1][:, u * tk:(u + 1) * tk], tri_upto) for u in range(r)] for sc in scores]
            ws, gs = [], []
            for h in range(2):
                later = [None] * r
                for u in range(r):
                    later[u] = rest[h] - sums[h][u][:, :tk]
                    rest[h] = rest[h] - sums[h][u][:, tk:]
                w = jnp.exp(scores[h][0] + jnp.concatenate(later, axis=1))
                if own:
                    w = jnp.where(scores[h][2], w, 0.0)
                ws.append(w)
                gs.append(w * dws[h])
            gsums = [[_split_dot(g[:, u * tk:(u + 1) * tk], tri_before) for u in range(r)] for g in gs]
            dzs = []
            for h in range(2):
                before = [None] * r
                for u in range(r):
                    before[u] = g_prefix[h] + gsums[h][u][:, :tk]
                    g_prefix[h] = g_prefix[h] + gsums[h][u][:, tk:]
                dz = gs[h] - jnp.exp(scores[h][0]) * (gs[h] + jnp.concatenate(before, axis=1))
                if own:
                    dz = jnp.where(scores[h][2], dz, 0.0)
                dzs.append(dz.astype(BF16))
            pq = [jnp.dot(dz, kb, preferred_element_type=F32) for dz in dzs]
            dkt_ref[:, ks] += (jnp.dot(qst[0], dzs[0], preferred_element_type=F32)
                               + jnp.dot(qst[1], dzs[1], preferred_element_type=F32))
            dvt_ref[:, ks] += (jnp.dot(dobt[0], ws[0].astype(BF16), preferred_element_type=F32)
                               + jnp.dot(dobt[1], ws[1].astype(BF16), preferred_element_type=F32))
            return (dq_acc + jnp.where(first, pq[0], pq[1]), *rest, *g_prefix)

        zero = jnp.zeros((tq, tk), F32)
        carry = (jnp.zeros((tq, pw), F32), jnp.broadcast_to(b_ref[0], (tq, tk)), jnp.broadcast_to(b_ref[1], (tq, tk)),
                 zero, zero)
        carry = lax.fori_loop(0, qi, lambda kq, c: tiles(kq, c, False), carry)
        dq, dgq = _pair_rms_bwd(q, gq_ref[...], rq, tiles(qi, carry, True)[0] * scale)
        dq_ref[...] = dq
        dgq_ref[0] += dgq

        @pl.when(qi == nq - 1)
        def _():
            def finish(c, dgk):
                rows = pl.ds(pl.multiple_of(c * tq, tq), tq)
                k = k_ref[rows, :]
                _, rk = _pair_rms(k, gk_ref[...])
                dk, dgk_c = _pair_rms_bwd(k, gk_ref[...], rk, dkt_ref[:, rows].T)
                dk_ref[rows, :] = dk
                dv_ref[rows, :] = dvt_ref[:, rows].T
                return dgk + dgk_c

            dgk_ref[0] = lax.fori_loop(0, nq, finish, jnp.zeros((1, pw), F32))

    def cols(block0):
        return pl.BlockSpec((s, pw), lambda hp, i: (0, block0 + hp))

    gain = pl.BlockSpec((1, pw), lambda hp, i: (0, 0))
    q_rows = pl.BlockSpec((tq, pw), lambda hp, i: (i, hp))
    all_rows = pl.BlockSpec((s, pw), lambda hp, i: (0, hp))
    pair_gain = pl.BlockSpec((1, 1, pw), lambda hp, i: (hp, 0, 0))
    wide = jax.ShapeDtypeStruct((s, ATTN_WIDTH), F32)
    gains = jax.ShapeDtypeStruct((HEAD_PAIRS, 1, pw), F32)
    return _call_with_ride_along(
        body, ride, name="attn_bwd", grid=(HEAD_PAIRS, nq), semantics=("parallel", "arbitrary"),
        in_specs=[pl.BlockSpec((tq, pw), lambda hp, i: (i, _Q_BLOCK0 + hp)), cols(_K_BLOCK0), cols(_V_BLOCK0),
                  gain, gain, pl.BlockSpec((2, tq, 1), lambda hp, i: (hp, i, 0)), q_rows],
        out_specs=(q_rows, all_rows, all_rows, pair_gain, pair_gain),
        out_shape=(wide, wide, wide, gains, gains),
        scratch_shapes=[pltpu.VMEM((s, pw), BF16), pltpu.VMEM((s, pw), BF16),
                        pltpu.VMEM((pw, s), F32), pltpu.VMEM((pw, s), F32)],
        operands=(proj, proj, proj, gq, gk, bsum, do))


def _reduce_adamw(parts, w, m, v, *, name, row0s=(0,)):
    n, _, cols = parts.shape
    rows = w.shape[0]
    seg_rows = rows // len(row0s)
    tile_rows = 2 * SUBLANES
    tr = max(t for t in range(tile_rows, min(seg_rows, 512) + 1, tile_rows)
             if seg_rows % t == 0 and all(r0 % t == 0 for r0 in row0s))
    seg_tiles = seg_rows // tr

    def parts_tile(i):
        tile = row0s[0] // tr + i
        for l in range(1, len(row0s)):
            tile = jnp.where(i >= l * seg_tiles, row0s[l] // tr + i - l * seg_tiles, tile)
        return tile

    c1 = 1.0 - ADAM_B1 ** ADAM_STEP
    c2 = 1.0 - ADAM_B2 ** ADAM_STEP

    def body(p_ref, w_ref, m_ref, v_ref, g_ref, d_ref, nm_ref, nv_ref):
        g = p_ref[0].astype(F32)
        for i in range(1, n):
            g = g + p_ref[i].astype(F32)
        nm = ADAM_B1 * m_ref[...] + (1.0 - ADAM_B1) * g
        nv = ADAM_B2 * v_ref[...] + (1.0 - ADAM_B2) * (g * g)
        g_ref[...] = g
        nm_ref[...] = nm
        nv_ref[...] = nv
        d_ref[...] = -ADAM_LR * ((nm / c1) / (jnp.sqrt(nv / c2) + ADAM_EPS) + ADAM_WD * w_ref[...])

    row = pl.BlockSpec((tr, cols), lambda i: (i, 0))
    out = jax.ShapeDtypeStruct((rows, cols), F32)
    return pl.pallas_call(
        body, name=name, grid=(rows // tr,),
        in_specs=[pl.BlockSpec((n, tr, cols), lambda i: (0, parts_tile(i), 0)), row, row, row],
        out_specs=(row,) * 4, out_shape=(out,) * 4, compiler_params=_params(("parallel",)))(parts, w, m, v)


def _all_gather(shards, *, name):
    n_arr = len(shards)

    def body(*refs):
        x_refs, out_refs = refs[:n_arr], refs[n_arr:2 * n_arr]
        send_sems, recv_sems, local_sems = refs[2 * n_arr:]
        x, y, c = lax.axis_index("x"), lax.axis_index("y"), lax.axis_index("c")
        me, sibling = (x, y, c), (x, y, 1 - c)
        chips = [(1 - x, y), (x, 1 - y), (1 - x, 1 - y)]

        def slot(a, px, py, pc):
            return out_refs[a].at[4 * px + 2 * py + pc]

        def copy(a, k, block, to, src=None):
            return pltpu.make_async_remote_copy(
                src_ref=slot(a, *block) if src is None else src, dst_ref=slot(a, *block),
                send_sem=send_sems.at[a, k], recv_sem=recv_sems.at[a, k], device_id=to, device_id_type=MESH_IDS)

        arrays = range(n_arr)
        mine = [pltpu.make_async_copy(x_refs[a], slot(a, *me), local_sems.at[a]) for a in arrays]
        for cp in mine:
            cp.start()
        first = [copy(a, 0, me, sibling, src=x_refs[a]) for a in arrays]
        first += [copy(a, 1 + j, me, (*chip, c), src=x_refs[a]) for j, chip in enumerate(chips) for a in arrays]
        for cp in first:
            cp.start()
        passed = []
        for j, chip in enumerate(chips):
            for a in arrays:
                copy(a, 1 + j, (*chip, c), me).wait_recv()
                passed.append(copy(a, 4 + j, (*chip, c), sibling))
                passed[-1].start()
        for a in arrays:
            copy(a, 0, sibling, me).wait_recv()
        for j, chip in enumerate(chips):
            for a in arrays:
                copy(a, 4 + j, (*chip, 1 - c), me).wait_recv()
        for cp in first + passed:
            cp.wait_send()
        for cp in mine:
            cp.wait()

    any_spec = pl.BlockSpec(memory_space=pl.ANY)
    return pl.pallas_call(
        body, name=name, out_shape=tuple(jax.ShapeDtypeStruct((N_DEV,) + s.shape, s.dtype) for s in shards),
        in_specs=[any_spec] * n_arr, out_specs=(any_spec,) * n_arr,
        scratch_shapes=[pltpu.SemaphoreType.DMA((n_arr, 7)), pltpu.SemaphoreType.DMA((n_arr, 7)),
                        pltpu.SemaphoreType.DMA((n_arr,))],
    )(*shards)


def _direct_exchange(x_refs, out_refs, send_sems, recv_sems, local_sems, *, scatter, row0s):
    arrays = range(len(x_refs))
    x, y, c = lax.axis_index("x"), lax.axis_index("y"), lax.axis_index("c")
    me = 4 * x + 2 * y + c

    def src(a, slot):
        return x_refs[a].at[slot] if scatter else x_refs[a]

    def landing(a, sender):
        return out_refs[a].at[sender, pl.ds(row0s[a], x_refs[a].shape[-2]), :]

    def local(a):
        return pltpu.make_async_copy(src(a, me), landing(a, me), local_sems.at[a])

    def copy(a, k, arriving):
        px, py, pc = x ^ (k >> 2), y ^ ((k >> 1) & 1), c ^ (k & 1)
        peer = 4 * px + 2 * py + pc
        return pltpu.make_async_remote_copy(
            src_ref=src(a, peer), dst_ref=landing(a, peer if arriving else me),
            send_sem=send_sems.at[a, k - 1], recv_sem=recv_sems.at[a, k - 1],
            device_id=(px, py, pc), device_id_type=MESH_IDS)

    def start():
        for a in arrays:
            local(a).start()
        for k in range(1, N_DEV):
            for a in arrays:
                copy(a, k, False).start()

    def wait():
        for k in range(1, N_DEV):
            for a in arrays:
                copy(a, k, True).wait_recv()
        for k in range(1, N_DEV):
            for a in arrays:
                copy(a, k, False).wait_send()
        for a in arrays:
            local(a).wait()

    return start, wait


def _exchange_sems(n_arr):
    return [pltpu.SemaphoreType.DMA((n_arr, N_DEV - 1)), pltpu.SemaphoreType.DMA((n_arr, N_DEV - 1)),
            pltpu.SemaphoreType.DMA((n_arr,))]


def _pack_small(vals, tail=0.0):
    flat = jnp.concatenate([vals[n].reshape(-1) for n in SMALL] + [jnp.full((1,), tail, F32)])
    rows = -(-flat.shape[0] // (SUBLANES * PACK_COLS)) * SUBLANES
    return jnp.pad(flat, (0, rows * PACK_COLS - flat.shape[0])).reshape(rows, PACK_COLS)


def _unpack_small(flat, shapes):
    flat = flat.reshape(-1)
    out, off = {}, 0
    for n in SMALL:
        size = math.prod(shapes[n])
        out[n] = flat[off:off + size].reshape(shapes[n])
        off += size
    return out


def _pair_gain(g):
    return jnp.tile(g, 2)[None]


def _ssm_setup(sp):
    g, p = SSM_GROUPS, SSM_STATE
    a_re = sp["ssm_a_re"][:, None, :]
    a_im = sp["ssm_a_im"][:, None, :]
    log_dt = jnp.broadcast_to(sp["ssm_log_dt"][:, None, None], (g, 1, p))
    bt_re = sp["ssm_b_re"].transpose(0, 2, 1)
    bt_im = sp["ssm_b_im"].transpose(0, 2, 1)
    return a_re, a_im, log_dt, bt_re, bt_im


def _layer_fwd(h, p_l, w, sp, ride=None):
    hn = _rms_fwd(h, sp["mix_norm_g"][None], name="rms_mix")
    proj = _mm(hn, w["w_in"], name="mm_proj")

    pw_re, pw_im, bbt_re, bbt_im = _ssm_param_fwd(*_ssm_setup(sp))
    pw_re = pw_re.transpose(1, 0, 2).reshape(SUBLANES, SSM_LANES)
    pw_im = pw_im.transpose(1, 0, 2).reshape(SUBLANES, SSM_LANES)
    ssm_mats = (_block_diag(bbt_re).astype(BF16), _block_diag(bbt_im).astype(BF16),
                _block_diag(sp["ssm_c_re"]).astype(BF16), _block_diag(-sp["ssm_c_im"]).astype(BF16),
                pw_re, pw_im, sp["ssm_d"].reshape(1, SSM_WIDTH))
    x_re, x_im, y, z = _ssm_fwd(proj, *ssm_mats)
    zz = _mm(z, w["ssm_w_glu"], name="mm_glu")

    o, bsum, *landed = _attn_fwd(proj, _pair_gain(sp["q_norm_g"]), _pair_gain(sp["k_norm_g"]), ride)

    bglu = sp["ssm_b_glu"][None]
    ycat = _gate_fwd(zz, bglu, proj, o)
    h2, hn2 = _mm_residual_norm(ycat, w["w_out"], h, sp["ple_norm_g"][None], name="mm_out")
    pp = _mm(p_l, w["w_ple_proj"], name="mm_ple_proj")
    pgl, h3 = _mm_gated_add(hn2, w["w_ple_gate"], h2, pp, name="mm_ple_gate")
    saved = dict(h=h, hn=hn, proj=proj, ssm_mats=ssm_mats, x_re=x_re, x_im=x_im, y=y, z=z,
                 zz=zz, bsum=bsum, o=o, ycat=ycat, h2=h2, hn2=hn2,
                 pgl=pgl, pp=pp, p_l=p_l, bglu=bglu)
    return h3, saved, landed


def _layer_bwd(dh3, sv, w, sp, ship=None):
    g, hh, p = SSM_GROUPS, SSM_GROUP, SSM_STATE
    grads = {}
    ship = ship or (lambda stage, grads_so_far, parts: None)
    dgp, dpp = _ple_bwd(dh3, sv["pgl"], sv["pp"])
    grads["w_ple_gate"] = _mm(sv["hn2"], dgp, ta=True, out_dtype=BF16, name="mm_d_ple_gate")
    grads["w_ple_proj"] = _mm(sv["p_l"], dpp, ta=True, out_dtype=BF16, name="mm_d_ple_proj")
    dh2, dg2 = _mm_norm_grad(dgp, w["w_ple_gate"], sv["h2"], sp["ple_norm_g"][None], dh3, name="mm_dhn2")
    grads["ple_norm_g"] = dg2[0]
    grads["w_out"] = _mm(sv["ycat"], dh2, ta=True, out_dtype=BF16, name="mm_d_out")
    dyc = _mm(dh2, w["w_out"], tb=True, name="mm_dycat")
    dzz, dgs, do, dga, dbglu = _gate_bwd(dyc, sv["zz"], sv["bglu"], sv["proj"], sv["o"])
    grads["ssm_b_glu"] = dbglu[0]

    dq, dk, dv, dgq, dgk, *parts = _attn_bwd(sv["proj"], _pair_gain(sp["q_norm_g"]), _pair_gain(sp["k_norm_g"]),
                                             sv["bsum"], do, ship("attn", grads, None))
    grads["q_norm_g"] = jnp.sum(dgq.reshape(ATTN_HEADS, HEAD_DIM), axis=0)
    grads["k_norm_g"] = jnp.sum(dgk.reshape(ATTN_HEADS, HEAD_DIM), axis=0)

    grads["ssm_w_glu"] = _mm(sv["z"], dzz, ta=True, out_dtype=BF16, name="mm_d_glu")
    dz = _mm(dzz, w["ssm_w_glu"], tb=True, name="mm_dz")
    du, dbc_re, dbc_im, dcc_re, dcc_im, gab_re, gab_im, dd, *landed = _ssm_bwd(
        dz, sv["y"], sv["proj"], sv["x_re"], sv["x_im"], *sv["ssm_mats"], ride=ship("ssm", grads, parts))
    parts = parts[:len(parts) - len(landed)] + landed
    grads["ssm_d"] = dd.reshape(g, hh)
    grads["ssm_c_re"] = _block_diag_take(dcc_re, hh, p)
    grads["ssm_c_im"] = -_block_diag_take(dcc_im, hh, p)
    da_re, da_im, dldt, dbt_re, dbt_im = _ssm_param_bwd(
        *_ssm_setup(sp), gab_re.reshape(g, 1, p), gab_im.reshape(g, 1, p),
        _block_diag_take(dbc_re, hh, p), _block_diag_take(dbc_im, hh, p))
    grads["ssm_a_re"], grads["ssm_a_im"] = da_re[:, 0, :], da_im[:, 0, :]
    grads["ssm_log_dt"] = dldt[:, 0, 0]
    grads["ssm_b_re"], grads["ssm_b_im"] = dbt_re.transpose(0, 2, 1), dbt_im.transpose(0, 2, 1)

    dproj = jnp.concatenate([du, dgs, dq, dk, dv, dga], axis=1)
    grads["w_in"] = _mm(sv["hn"], dproj, ta=True, out_dtype=BF16, name="mm_d_in")
    dh, dg1, *landed = _mm_norm_grad(dproj, w["w_in"], sv["h"], sp["mix_norm_g"][None], dh2, name="mm_dhn",
                                     ride=ship("dhn", grads, parts))
    parts = landed + parts[len(landed):]
    grads["mix_norm_g"] = dg1[0]
    return dh, grads, parts


def _local_step(x, p, target, big, small, fetch=None, ship=None):
    h = x
    big = list(big)
    saved = []
    for l in range(DEPTH):
        sp = {n: small[n][l] for n in SMALL}
        h, sv, landed = _layer_fwd(h, p[l], big[l], sp, fetch[0] if fetch and l == 0 else None)
        if landed:
            big[DEPTH - 1] = fetch[1](landed)
        saved.append(sv)
    dh, loss_parts = _loss_head(h, target)
    grads = [None] * DEPTH
    shipped = []
    for l in reversed(range(DEPTH)):
        sp = {n: small[n][l] for n in SMALL}
        carry = None
        if ship and l == 0:
            def carry(stage, grads_so_far, parts):
                return ship(stage, grads[DEPTH - 1], grads_so_far, parts)
        dh, grads[l], landed = _layer_bwd(dh, saved[l], big[l], sp, carry)
        shipped = landed or shipped
    return jnp.sum(loss_parts), dh, grads, shipped


def kernel(x, p, mix_norm_g, w_in, ssm_a_re, ssm_a_im, ssm_log_dt, ssm_b_re, ssm_b_im, ssm_c_re, ssm_c_im, ssm_d, ssm_w_glu, ssm_b_glu, q_norm_g, k_norm_g, w_out, ple_norm_g, w_ple_gate, w_ple_proj, loss_target, m_mix_norm_g, m_w_in, m_ssm_a_re, m_ssm_a_im, m_ssm_log_dt, m_ssm_b_re, m_ssm_b_im, m_ssm_c_re, m_ssm_c_im, m_ssm_d, m_ssm_w_glu, m_ssm_b_glu, m_q_norm_g, m_k_norm_g, m_w_out, m_ple_norm_g, m_w_ple_gate, m_w_ple_proj, v_mix_norm_g, v_w_in, v_ssm_a_re, v_ssm_a_im, v_ssm_log_dt, v_ssm_b_re, v_ssm_b_im, v_ssm_c_re, v_ssm_c_im, v_ssm_d, v_ssm_w_glu, v_ssm_b_glu, v_q_norm_g, v_k_norm_g, v_w_out, v_ple_norm_g, v_w_ple_gate, v_w_ple_proj):
    given = dict(locals())
    wts = {n: given[n] for n in WEIGHTS}
    mom = {n: given["m_" + n] for n in WEIGHTS}
    var = {n: given["v_" + n] for n in WEIGHTS}

    def rows2d(a):
        return a.reshape(-1, a.shape[-1])

    assert DEPTH == 2
    groups = (("w_in",), ("ssm_w_glu", "w_ple_proj"), ("w_out", "w_ple_gate"))
    layer_rows = [sum(wts[n].shape[1] for n in names) for names in groups]
    widths = [wts[names[0]].shape[2] for names in groups]

    def shards_of(l):
        return [jnp.concatenate([wts[n][l].astype(BF16) for n in names], axis=0) for names in groups]

    def weights_of(gathered):
        full = {}
        for names, got in zip(groups, gathered):
            row0 = 0
            for n in names:
                r = wts[n].shape[1]
                blocks = got[:, row0:row0 + r, :]
                full[n] = blocks.reshape(N_DEV * r, -1) if n in ROW_SHARDED else blocks.transpose(1, 0, 2).reshape(r, -1)
                row0 += r
        return full

    def blocks_to_send(layer_grads, which):
        def blocks(n):
            g = layer_grads[n]
            if n in ROW_SHARDED:
                return g.reshape(N_DEV, -1, g.shape[1])
            return g.reshape(g.shape[0], N_DEV, -1).transpose(1, 0, 2)

        return [jnp.concatenate([blocks(n) for n in groups[gi]], axis=1) for gi in which]

    fetch = (dict(bufs=shards_of(1), scatter=False, row0s=[0] * len(groups),
                  into_shapes=[jax.ShapeDtypeStruct((N_DEV, r, c), BF16) for r, c in zip(layer_rows, widths)]),
             weights_of)

    def ship(stage, last_grads, first_grads, parts):
        if stage == "attn":
            return dict(bufs=blocks_to_send(last_grads, (0, 1, 2)), scatter=True, row0s=layer_rows, into_shapes=[
                jax.ShapeDtypeStruct((N_DEV, DEPTH * r, c), BF16) for r, c in zip(layer_rows, widths)])
        which = (1, 2) if stage == "ssm" else (0,)
        return dict(bufs=blocks_to_send(first_grads, which), scatter=True, row0s=[0] * len(which),
                    into=[parts[gi] for gi in which])

    first = weights_of(_all_gather(shards_of(0), name="gather_weights"))
    small = {n: wts[n] for n in SMALL}
    loss, dx, grads, parts = _local_step(x[0], p[:, 0], loss_target[0], [first, None], small, fetch, ship)

    big_out = {}
    for names, part, rows in zip(groups, parts, layer_rows):
        row0 = 0
        for n in names:
            res = _reduce_adamw(part, rows2d(wts[n]), rows2d(mom[n]), rows2d(var[n]),
                                row0s=[l * rows + row0 for l in range(DEPTH)], name="adamw_" + n)
            big_out[n] = [a.reshape(wts[n].shape) for a in res]
            row0 += wts[n].shape[1]

    small_shapes = {n: wts[n].shape for n in SMALL}
    local_small = _pack_small({n: jnp.stack([grads[l][n] for l in range(DEPTH)]) for n in SMALL}, tail=loss)
    (all_small,) = _all_gather([local_small], name="gather_small_grads")
    small_out = _reduce_adamw(all_small, _pack_small(small), _pack_small({n: mom[n] for n in SMALL}),
                              _pack_small({n: var[n] for n in SMALL}), name="adamw_replicated")
    loss = small_out[0].reshape(-1)[sum(math.prod(s) for s in small_shapes.values())]
    small_out = [_unpack_small(a, small_shapes) for a in small_out]

    outs = [loss, dx[None]]
    for k in range(4):
        outs += [big_out[n][k] if n in BIG else small_out[k][n] for n in WEIGHTS]
    return tuple(outs)
```

```python
import math

import jax
import jax.numpy as jnp
from jax import lax
from jax.experimental import pallas as pl
from jax.experimental.pallas import tpu as pltpu

F32 = jnp.float32
BF16 = jnp.bfloat16
MESH_IDS = pl.DeviceIdType.MESH

N_DEV = 8
D_MODEL = 1024
DEPTH = 2
PLE_DIM = 256
SSM_WIDTH = 512
SSM_GROUP = 16
SSM_GROUPS = 32
SSM_STATE = 64
SSM_LANES = SSM_GROUPS * SSM_STATE
ATTN_WIDTH = 512
ATTN_HEADS = 8
HEAD_DIM = 64
RMS_EPS = 1e-6
ADAM_LR = 0.001
ADAM_B1 = 0.9
ADAM_B2 = 0.999
ADAM_EPS = 1e-08
ADAM_WD = 0.01
ADAM_STEP = 10

VMEM_LIMIT = 56 * 1024 * 1024
ATT_TQ = 512
ATT_TK = 128
SSM_BLOCK_GROUPS = 8
SSM_TIME_BLOCK = 512
SUBLANES = 8
PACK_COLS = 1024

BIG = ("w_in", "ssm_w_glu", "w_out", "w_ple_gate", "w_ple_proj")
ROW_SHARDED = ("w_out", "w_ple_gate")
SMALL = ("mix_norm_g", "ssm_a_re", "ssm_a_im", "ssm_log_dt", "ssm_b_re", "ssm_b_im", "ssm_c_re",
         "ssm_c_im", "ssm_d", "ssm_b_glu", "q_norm_g", "k_norm_g", "ple_norm_g")
WEIGHTS = ("mix_norm_g", "w_in", "ssm_a_re", "ssm_a_im", "ssm_log_dt", "ssm_b_re", "ssm_b_im",
           "ssm_c_re", "ssm_c_im", "ssm_d", "ssm_w_glu", "ssm_b_glu", "q_norm_g", "k_norm_g",
           "w_out", "ple_norm_g", "w_ple_gate", "w_ple_proj")
_NT = (((1,), (1,)), ((), ()))
_TN = (((0,), (0,)), ((), ()))


def _params(sem=None):
    return pltpu.CompilerParams(dimension_semantics=sem, vmem_limit_bytes=VMEM_LIMIT)


def _sigmoid(x):
    return 0.5 * jnp.tanh(0.5 * x) + 0.5


_GELU_K = math.sqrt(2.0 / math.pi)
_GELU_C = 0.044715


def _gelu(x):
    return 0.5 * x * (1.0 + jnp.tanh(_GELU_K * (x + _GELU_C * x * x * x)))


def _gelu_grad(x):
    th = jnp.tanh(_GELU_K * (x + _GELU_C * x * x * x))
    return 0.5 * (1.0 + th) + 0.5 * x * (1.0 - th * th) * _GELU_K * (1.0 + 3.0 * _GELU_C * x * x)


def _mm(a, b, *, name, ta=False, tb=False, add=None, out_dtype=F32, tm=1024, tn=1024, tk=1024,
        finish=None, extras=(), outs=None, ride=None):
    m, k = (a.shape[1], a.shape[0]) if ta else a.shape
    n = b.shape[0] if tb else b.shape[1]
    tm, tn, tk = min(tm, m), min(tn, n), min(tk, k)
    assert m % tm == 0 and n % tn == 0 and k % tk == 0, (name, a.shape, b.shape)
    n_steps = k // tk
    dims = (((0 if ta else 1,), (1 if tb else 0,)), ((), ()))
    a_spec = (pl.BlockSpec((tk, tm), lambda i, j, kk: (kk, i)) if ta
              else pl.BlockSpec((tm, tk), lambda i, j, kk: (i, kk)))
    b_spec = (pl.BlockSpec((tn, tk), lambda i, j, kk: (j, kk)) if tb
              else pl.BlockSpec((tk, tn), lambda i, j, kk: (kk, j)))
    o_spec = pl.BlockSpec((tm, tn), lambda i, j, kk: (i, j))
    ins, in_specs = [a, b], [a_spec, b_spec]
    if finish is None:
        if add is not None:
            ins.append(add)
            in_specs.append(o_spec)

        def finish(res, extra_refs, out_refs):
            if add is not None:
                res = res + extra_refs[0][...].astype(F32)
            out_refs[0][...] = res.astype(out_dtype)

        out_specs, out_shapes = [o_spec], [jax.ShapeDtypeStruct((m, n), out_dtype)]
    else:
        assert tn == n and add is None
        for arr, kind in extras:
            ins.append(arr)
            cols = arr.shape[1]
            in_specs.append(pl.BlockSpec((tm, cols), lambda i, j, kk: (i, 0)) if kind == "rows"
                            else pl.BlockSpec((1, cols), lambda i, j, kk: (0, 0)))
        out_specs = [pl.BlockSpec((tm, cols), lambda i, j, kk: (i, 0)) if kind == "rows"
                     else pl.BlockSpec((1, cols), lambda i, j, kk: (0, 0)) for cols, _, kind in outs]
        out_shapes = [jax.ShapeDtypeStruct((m if kind == "rows" else 1, cols), dtype) for cols, dtype, kind in outs]
    n_in = len(ins)

    def body(*refs):
        a_ref, b_ref = refs[0], refs[1]
        product = lax.dot_general(a_ref[...].astype(BF16), b_ref[...].astype(BF16), dims, preferred_element_type=F32)
        if n_steps == 1:
            finish(product, refs[2:n_in], refs[n_in:])
            return
        acc_ref = refs[-1]
        kk = pl.program_id(2)

        @pl.when(kk == 0)
        def _():
            acc_ref[...] = product

        @pl.when(kk > 0)
        def _():
            acc_ref[...] += product

        @pl.when(kk == n_steps - 1)
        def _():
            finish(acc_ref[...], refs[2:n_in], refs[n_in:-1])

    sums = outs is not None and any(kind == "sum" for _, _, kind in outs)
    res = _call_with_ride_along(
        body, ride, name=name, grid=(m // tm, n // tn, n_steps),
        semantics=("arbitrary",) * 3 if sums else ("parallel", "parallel", "arbitrary"),
        in_specs=in_specs, out_specs=tuple(out_specs), out_shape=tuple(out_shapes),
        scratch_shapes=[pltpu.VMEM((tm, tn), F32)] if n_steps > 1 else [], operands=ins)
    return res[0] if outs is None and ride is None else res


def _rms(hv, g):
    r = lax.rsqrt(jnp.mean(hv * hv, axis=-1, keepdims=True) + RMS_EPS)
    return hv * r * g, r


def _rms_grad(hv, g, dn):
    _, r = _rms(hv, g)
    a = dn * g
    dot = jnp.mean(a * hv, axis=-1, keepdims=True)
    return r * a - hv * (r * r * r * dot), jnp.sum(dn * hv * r, axis=0, keepdims=True)


def _mm_residual_norm(a, b, h, g, *, name):
    def finish(res, extra_refs, out_refs):
        h_ref, g_ref = extra_refs
        h2 = res + h_ref[...]
        out_refs[0][...] = h2
        out_refs[1][...] = _rms(h2, g_ref[...])[0].astype(BF16)

    d = b.shape[1]
    return _mm(a, b, name=name, finish=finish, extras=[(h, "rows"), (g, "vec")],
               outs=[(d, F32, "rows"), (d, BF16, "rows")])


def _mm_gated_add(a, b, h2, pp, *, name):
    def finish(res, extra_refs, out_refs):
        h_ref, p_ref = extra_refs
        out_refs[0][...] = res
        out_refs[1][...] = h_ref[...] + _sigmoid(res) * p_ref[...]

    d = b.shape[1]
    return _mm(a, b, name=name, finish=finish, extras=[(h2, "rows"), (pp, "rows")],
               outs=[(d, F32, "rows"), (d, F32, "rows")])


def _mm_norm_grad(dy, w, h, g, dres, *, name, ride=None):
    def finish(res, extra_refs, out_refs):
        h_ref, g_ref, r_ref = extra_refs
        dh, dg = _rms_grad(h_ref[...], g_ref[...], res)
        out_refs[0][...] = dh + r_ref[...]

        @pl.when(pl.program_id(0) == 0)
        def _():
            out_refs[1][...] = jnp.zeros_like(out_refs[1])

        out_refs[1][...] += dg

    d = w.shape[0]
    return _mm(dy, w, name=name, tb=True, finish=finish, extras=[(h, "rows"), (g, "vec"), (dres, "rows")],
               outs=[(d, F32, "rows"), (d, F32, "sum")], ride=ride)


def _rows_tile(rows, want=512):
    t = min(rows, want)
    assert rows % t == 0
    return t


def _rowwise(body, name, rows, ins, outs):
    widest = max(cols for _, cols, _ in ins)
    tm = _rows_tile(rows, 512 if widest > 128 else 4096)
    in_specs = []
    for _, cols, cb in ins:
        if cb is None:
            in_specs.append(pl.BlockSpec((1, cols), lambda i: (0, 0)))
        else:
            in_specs.append(pl.BlockSpec((tm, cols), lambda i, cb=cb: (i, cb)))
    out_specs, out_shapes = [], []
    for cols, dtype, is_acc in outs:
        if is_acc:
            out_specs.append(pl.BlockSpec((1, cols), lambda i: (0, 0)))
            out_shapes.append(jax.ShapeDtypeStruct((1, cols), dtype))
        else:
            out_specs.append(pl.BlockSpec((tm, cols), lambda i: (i, 0)))
            out_shapes.append(jax.ShapeDtypeStruct((rows, cols), dtype))
    any_acc = any(o[2] for o in outs)
    return pl.pallas_call(
        body, name=name, grid=(rows // tm,), in_specs=in_specs, out_specs=tuple(out_specs),
        out_shape=tuple(out_shapes),
        compiler_params=_params(("arbitrary",) if any_acc else ("parallel",)))(*[a for a, _, _ in ins])


def _accumulate(ref, value):
    @pl.when(pl.program_id(0) == 0)
    def _():
        ref[...] = jnp.zeros_like(ref)

    ref[...] += value


def _rms_fwd(h, g, *, name):
    rows, d = h.shape

    def body(h_ref, g_ref, o_ref):
        o_ref[...] = _rms(h_ref[...], g_ref[...])[0].astype(BF16)

    return _rowwise(body, name, rows, [(h, d, 0), (g, d, None)], [(d, BF16, False)])[0]


def _gate_fwd(zz, bglu, proj, o):
    rows = zz.shape[0]
    w = SSM_WIDTH

    def body(zz_ref, b_ref, gs_ref, o_ref, ga_ref, y_ref):
        zz_v = zz_ref[...] + b_ref[...]
        val, gate = zz_v[:, :w], zz_v[:, w:]
        gs, ga = gs_ref[...], ga_ref[...]
        y_ref[:, :w] = (val * _sigmoid(gate) * (gs * _sigmoid(gs))).astype(BF16)
        y_ref[:, w:] = (o_ref[...] * (ga * _sigmoid(ga))).astype(BF16)

    return _rowwise(body, "gate_fwd", rows,
                    [(zz, 2 * w, 0), (bglu, 2 * w, None), (proj, w, 1), (o, w, 0), (proj, w, 5)],
                    [(2 * w, BF16, False)])[0]


def _gate_bwd(dyc, zz, bglu, proj, o):
    rows = zz.shape[0]
    w = SSM_WIDTH

    def body(dy_ref, zz_ref, b_ref, gs_ref, o_ref, ga_ref, dzz_ref, dgs_ref, do_ref, dga_ref, db_ref):
        zz_v = zz_ref[...] + b_ref[...]
        val, gate = zz_v[:, :w], zz_v[:, w:]
        gs, ga = gs_ref[...], ga_ref[...]
        dys, dya = dy_ref[:, :w], dy_ref[:, w:]
        sg, ss, sa = _sigmoid(gate), _sigmoid(gs), _sigmoid(ga)
        glu = val * sg
        dglu = dys * (gs * ss)
        dgs_ref[...] = dys * glu * (ss * (1.0 + gs * (1.0 - ss)))
        dval = dglu * sg
        dgate = dglu * val * sg * (1.0 - sg)
        dzz_ref[:, :w] = dval.astype(BF16)
        dzz_ref[:, w:] = dgate.astype(BF16)
        do_ref[...] = dya * (ga * sa)
        dga_ref[...] = dya * o_ref[...] * (sa * (1.0 + ga * (1.0 - sa)))
        _accumulate(db_ref, jnp.concatenate([jnp.sum(dval, axis=0, keepdims=True),
                                             jnp.sum(dgate, axis=0, keepdims=True)], axis=1))

    return _rowwise(body, "gate_bwd", rows,
                    [(dyc, 2 * w, 0), (zz, 2 * w, 0), (bglu, 2 * w, None), (proj, w, 1), (o, w, 0), (proj, w, 5)],
                    [(2 * w, BF16, False), (w, F32, False), (w, F32, False), (w, F32, False), (2 * w, F32, True)])


def _ple_bwd(dh3, pgl, pp):
    rows, d = dh3.shape

    def body(dh_ref, g_ref, p_ref, dg_ref, dp_ref):
        sg = _sigmoid(g_ref[...])
        dh = dh_ref[...]
        dg_ref[...] = (dh * p_ref[...] * sg * (1.0 - sg)).astype(BF16)
        dp_ref[...] = (dh * sg).astype(BF16)

    return _rowwise(body, "ple_bwd", rows, [(dh3, d, 0), (pgl, d, 0), (pp, d, 0)],
                    [(d, BF16, False), (d, BF16, False)])


def _loss_head(h, target):
    rows, d = h.shape

    def body(h_ref, t_ref, dh_ref, l_ref):
        err = h_ref[...] - t_ref[...]
        dh_ref[...] = err * (1.0 / d)
        _accumulate(l_ref, jnp.sum(err * err, axis=0, keepdims=True) * (0.5 / d))

    return _rowwise(body, "loss_head", rows, [(h, d, 0), (target, d, 0)], [(d, F32, False), (d, F32, True)])


def _zoh(lr, li, dt):
    mag = jnp.exp(lr * dt)
    ab_re = mag * jnp.cos(li * dt)
    ab_im = mag * jnp.sin(li * dt)
    num_re = ab_re - 1.0
    den = lr * lr + li * li
    f_re = (num_re * lr + ab_im * li) / den
    f_im = (ab_im * lr - num_re * li) / den
    return ab_re, ab_im, f_re, f_im, den


def _ssm_param_fwd(a_re, a_im, log_dt, bt_re, bt_im):
    g, _, p = a_re.shape
    h = bt_re.shape[1]

    def body(lr_ref, li_ref, ldt_ref, br_ref, bi_ref, pr_ref, pi_ref, bbr_ref, bbi_ref):
        dt = jnp.exp(ldt_ref[...])
        ab_re, ab_im, f_re, f_im, _ = _zoh(lr_ref[...], li_ref[...], dt)
        cr, ci = ab_re, ab_im
        for k in range(SUBLANES):
            pr_ref[:, k:k + 1, :] = cr
            pi_ref[:, k:k + 1, :] = ci
            cr, ci = cr * ab_re - ci * ab_im, cr * ab_im + ci * ab_re
        br, bi = br_ref[...], bi_ref[...]
        bbr_ref[...] = f_re * br - f_im * bi
        bbi_ref[...] = f_re * bi + f_im * br

    pw = jax.ShapeDtypeStruct((g, SUBLANES, p), F32)
    large = jax.ShapeDtypeStruct((g, h, p), F32)
    return pl.pallas_call(body, name="ssm_param_fwd", out_shape=(pw, pw, large, large),
                          compiler_params=_params())(a_re, a_im, log_dt, bt_re, bt_im)


def _ssm_param_bwd(a_re, a_im, log_dt, bt_re, bt_im, gab_re, gab_im, gbb_re, gbb_im):
    g, _, p = a_re.shape
    h = bt_re.shape[1]

    def body(lr_ref, li_ref, ldt_ref, br_ref, bi_ref, gar_ref, gai_ref, gbr_ref, gbi_ref,
             dlr_ref, dli_ref, dldt_ref, dbr_ref, dbi_ref):
        lr, li = lr_ref[...], li_ref[...]
        dt = jnp.exp(ldt_ref[...])
        ab_re, ab_im, f_re, f_im, den = _zoh(lr, li, dt)
        br, bi = br_ref[...], bi_ref[...]
        gbr, gbi = gbr_ref[...], gbi_ref[...]
        dbr_ref[...] = f_re * gbr + f_im * gbi
        dbi_ref[...] = f_re * gbi - f_im * gbr
        gf_re = jnp.sum(br * gbr + bi * gbi, axis=1, keepdims=True)
        gf_im = jnp.sum(br * gbi - bi * gbr, axis=1, keepdims=True)
        il_re, il_im = lr / den, -li / den
        ga_re = gar_ref[...] + il_re * gf_re + il_im * gf_im
        ga_im = gai_ref[...] + il_re * gf_im - il_im * gf_re
        q_re = f_re * il_re - f_im * il_im
        q_im = f_re * il_im + f_im * il_re
        gl_re = -(q_re * gf_re + q_im * gf_im)
        gl_im = -(q_re * gf_im - q_im * gf_re)
        dlr_ref[...] = gl_re + dt * (ab_re * ga_re + ab_im * ga_im)
        dli_ref[...] = gl_im + dt * (ab_re * ga_im - ab_im * ga_re)
        la_re = lr * ab_re - li * ab_im
        la_im = lr * ab_im + li * ab_re
        dldt_ref[...] = jnp.sum((la_re * ga_re + la_im * ga_im) * dt, axis=2, keepdims=True)

    small = jax.ShapeDtypeStruct((g, 1, p), F32)
    one = jax.ShapeDtypeStruct((g, 1, 1), F32)
    large = jax.ShapeDtypeStruct((g, h, p), F32)
    return pl.pallas_call(body, name="ssm_param_bwd", out_shape=(small, small, one, large, large),
                          compiler_params=_params())(
                              a_re, a_im, log_dt, bt_re, bt_im, gab_re, gab_im, gbb_re, gbb_im)


def _block_diag(m):
    g, h, p = m.shape
    nb = g // SSM_BLOCK_GROUPS
    eye = jnp.eye(SSM_BLOCK_GROUPS, dtype=m.dtype)
    m4 = m.reshape(nb, SSM_BLOCK_GROUPS, h, p)
    return (m4[:, :, :, None, :] * eye[None, :, None, :, None]).reshape(nb, SSM_BLOCK_GROUPS * h, SSM_BLOCK_GROUPS * p)


def _block_diag_take(m, h, p):
    nb = m.shape[0]
    eye = jnp.eye(SSM_BLOCK_GROUPS, dtype=m.dtype)
    m5 = m.reshape(nb, SSM_BLOCK_GROUPS, h, SSM_BLOCK_GROUPS, p)
    return jnp.sum(m5 * eye[None, :, None, :, None], axis=3).reshape(nb * SSM_BLOCK_GROUPS, h, p)


def _scan_tile(br, bi, pw_re, pw_im, reverse):
    row = lax.broadcasted_iota(jnp.int32, br.shape, 0)
    xr, xi = br, bi
    for d in (1, 2, 4):
        ar, ai = pw_re[d - 1:d, :], pw_im[d - 1:d, :]
        if reverse:
            keep = row < SUBLANES - d
            sr = jnp.where(keep, pltpu.roll(xr, SUBLANES - d, 0), 0.0)
            si = jnp.where(keep, pltpu.roll(xi, SUBLANES - d, 0), 0.0)
        else:
            keep = row >= d
            sr = jnp.where(keep, pltpu.roll(xr, d, 0), 0.0)
            si = jnp.where(keep, pltpu.roll(xi, d, 0), 0.0)
        xr, xi = xr + ar * sr - ai * si, xi + ar * si + ai * sr
    return xr, xi


def _ssm_blocks(s):
    tt = min(SSM_TIME_BLOCK, s)
    assert s % tt == 0 and tt % SUBLANES == 0
    ch = SSM_BLOCK_GROUPS * SSM_GROUP
    st = SSM_BLOCK_GROUPS * SSM_STATE
    return tt, s // tt, SSM_WIDTH // ch, ch, st


def _ssm_fwd(proj, bc_re, bc_im, cc_re, cc_im, pw_re, pw_im, dvec):
    s = proj.shape[0]
    tt, nt, nb, ch, st = _ssm_blocks(s)
    n_tiles = tt // SUBLANES

    def body(u_ref, bre_ref, bim_ref, cre_ref, cim_ref, pr_ref, pi_ref, d_ref,
             xr_ref, xi_ref, y_ref, z_ref, carry_ref):
        @pl.when(pl.program_id(1) == 0)
        def _():
            carry_ref[...] = jnp.zeros_like(carry_ref)

        u = u_ref[...]
        ub = u.astype(BF16)
        xr_ref[...] = jnp.dot(ub, bre_ref[0], preferred_element_type=F32)
        xi_ref[...] = jnp.dot(ub, bim_ref[0], preferred_element_type=F32)
        pw_r, pw_i = pr_ref[...], pi_ref[...]

        def step(t, carry):
            cr, ci = carry
            rows = pl.ds(pl.multiple_of(t * SUBLANES, SUBLANES), SUBLANES)
            xr, xi = _scan_tile(xr_ref[rows, :], xi_ref[rows, :], pw_r, pw_i, False)
            xr, xi = xr + pw_r * cr - pw_i * ci, xi + pw_r * ci + pw_i * cr
            xr_ref[rows, :] = xr
            xi_ref[rows, :] = xi
            return (jnp.broadcast_to(xr[SUBLANES - 1:SUBLANES, :], (SUBLANES, st)),
                    jnp.broadcast_to(xi[SUBLANES - 1:SUBLANES, :], (SUBLANES, st)))

        cr, ci = lax.fori_loop(0, n_tiles, step, (carry_ref[0], carry_ref[1]), unroll=2)
        carry_ref[0] = cr
        carry_ref[1] = ci
        y = (lax.dot_general(xr_ref[...].astype(BF16), cre_ref[0], _NT, preferred_element_type=F32)
             + lax.dot_general(xi_ref[...].astype(BF16), cim_ref[0], _NT, preferred_element_type=F32)
             + d_ref[...] * u)
        y_ref[...] = y
        z_ref[...] = _gelu(y).astype(BF16)

    chan = pl.BlockSpec((tt, ch), lambda b, t: (t, b))
    state = pl.BlockSpec((tt, st), lambda b, t: (t, b))
    mat = pl.BlockSpec((1, ch, st), lambda b, t: (b, 0, 0))
    pw = pl.BlockSpec((SUBLANES, st), lambda b, t: (0, b))
    vec = pl.BlockSpec((1, ch), lambda b, t: (0, b))
    x_shape = jax.ShapeDtypeStruct((s, nb * st), F32)
    return pl.pallas_call(
        body, name="ssm_fwd", grid=(nb, nt), in_specs=[chan, mat, mat, mat, mat, pw, pw, vec],
        out_specs=(state, state, chan, chan),
        out_shape=(x_shape, x_shape, jax.ShapeDtypeStruct((s, nb * ch), F32), jax.ShapeDtypeStruct((s, nb * ch), BF16)),
        scratch_shapes=[pltpu.VMEM((2, SUBLANES, st), F32)],
        compiler_params=_params(("parallel", "arbitrary")))(proj, bc_re, bc_im, cc_re, cc_im, pw_re, pw_im, dvec)


def _ssm_bwd(dz, y, proj, x_re, x_im, bc_re, bc_im, cc_re, cc_im, pw_re, pw_im, dvec, ride=None):
    s = proj.shape[0]
    tt, nt, nb, ch, st = _ssm_blocks(s)
    n_tiles = tt // SUBLANES

    def body(dz_ref, y_ref, u_ref, xr_ref, xi_ref, bre_ref, bim_ref, cre_ref, cim_ref, pr_ref, pi_ref, d_ref,
             du_ref, dbr_ref, dbi_ref, dcr_ref, dci_ref, gar_ref, gai_ref, dd_ref,
             lr_ref, li_ref, carry_ref, acc_ref):
        @pl.when(pl.program_id(1) == 0)
        def _():
            for ref in (carry_ref, acc_ref, dbr_ref, dbi_ref, dcr_ref, dci_ref, dd_ref):
                ref[...] = jnp.zeros_like(ref)

        u = u_ref[...]
        dy = dz_ref[...] * _gelu_grad(y_ref[...])
        ub, dyb = u.astype(BF16), dy.astype(BF16)
        lr_ref[...] = jnp.dot(dyb, cre_ref[0], preferred_element_type=F32)
        li_ref[...] = jnp.dot(dyb, cim_ref[0], preferred_element_type=F32)
        pw_r, pw_i = pr_ref[...], -pi_ref[...]
        pwc_r = jnp.concatenate([pw_r[SUBLANES - 1 - i:SUBLANES - i, :] for i in range(SUBLANES)], axis=0)
        pwc_i = jnp.concatenate([pw_i[SUBLANES - 1 - i:SUBLANES - i, :] for i in range(SUBLANES)], axis=0)
        row = lax.broadcasted_iota(jnp.int32, (SUBLANES, st), 0)

        def step(t, carry):
            cr, ci, acc_r, acc_i = carry
            rows = pl.ds(pl.multiple_of((n_tiles - 1 - t) * SUBLANES, SUBLANES), SUBLANES)
            lr, li = _scan_tile(lr_ref[rows, :], li_ref[rows, :], pw_r, pw_i, True)
            lr, li = lr + pwc_r * cr - pwc_i * ci, li + pwc_r * ci + pwc_i * cr
            lr_ref[rows, :] = lr
            li_ref[rows, :] = li
            nr = jnp.where(row < SUBLANES - 1, pltpu.roll(lr, SUBLANES - 1, 0), cr)
            ni = jnp.where(row < SUBLANES - 1, pltpu.roll(li, SUBLANES - 1, 0), ci)
            xr, xi = xr_ref[rows, :], xi_ref[rows, :]
            return (jnp.broadcast_to(lr[0:1, :], (SUBLANES, st)), jnp.broadcast_to(li[0:1, :], (SUBLANES, st)),
                    acc_r + xr * nr + xi * ni, acc_i + xr * ni - xi * nr)

        cr, ci, acc_r, acc_i = lax.fori_loop(
            0, n_tiles, step, (carry_ref[0], carry_ref[1], acc_ref[0], acc_ref[1]), unroll=2)
        carry_ref[0], carry_ref[1] = cr, ci
        acc_ref[0], acc_ref[1] = acc_r, acc_i
        lrb, lib = lr_ref[...].astype(BF16), li_ref[...].astype(BF16)
        du_ref[...] = (lax.dot_general(lrb, bre_ref[0], _NT, preferred_element_type=F32)
                       + lax.dot_general(lib, bim_ref[0], _NT, preferred_element_type=F32) + dy * d_ref[...])
        dbr_ref[0] += lax.dot_general(ub, lrb, _TN, preferred_element_type=F32)
        dbi_ref[0] += lax.dot_general(ub, lib, _TN, preferred_element_type=F32)
        dcr_ref[0] += lax.dot_general(dyb, xr_ref[...].astype(BF16), _TN, preferred_element_type=F32)
        dci_ref[0] += lax.dot_general(dyb, xi_ref[...].astype(BF16), _TN, preferred_element_type=F32)
        dd_ref[...] += jnp.sum(dy * u, axis=0, keepdims=True)

        @pl.when(pl.program_id(1) == nt - 1)
        def _():
            gar_ref[...] = jnp.sum(acc_r, axis=0, keepdims=True)
            gai_ref[...] = jnp.sum(acc_i, axis=0, keepdims=True)

    chan = pl.BlockSpec((tt, ch), lambda b, t: (nt - 1 - t, b))
    state = pl.BlockSpec((tt, st), lambda b, t: (nt - 1 - t, b))
    mat = pl.BlockSpec((1, ch, st), lambda b, t: (b, 0, 0))
    pw = pl.BlockSpec((SUBLANES, st), lambda b, t: (0, b))
    vec = pl.BlockSpec((1, ch), lambda b, t: (0, b))
    svec = pl.BlockSpec((1, st), lambda b, t: (0, b))
    mat_shape = jax.ShapeDtypeStruct((nb, ch, st), F32)
    return _call_with_ride_along(
        body, ride, name="ssm_bwd", grid=(nb, nt), semantics=("parallel", "arbitrary"),
        in_specs=[chan, chan, chan, state, state, mat, mat, mat, mat, pw, pw, vec],
        out_specs=(chan, mat, mat, mat, mat, svec, svec, vec),
        out_shape=(jax.ShapeDtypeStruct((s, nb * ch), F32), mat_shape, mat_shape, mat_shape, mat_shape,
                   jax.ShapeDtypeStruct((1, nb * st), F32), jax.ShapeDtypeStruct((1, nb * st), F32),
                   jax.ShapeDtypeStruct((1, nb * ch), F32)),
        scratch_shapes=[pltpu.VMEM((tt, st), F32), pltpu.VMEM((tt, st), F32),
                        pltpu.VMEM((2, SUBLANES, st), F32), pltpu.VMEM((2, SUBLANES, st), F32)],
        operands=(dz, y, proj, x_re, x_im, bc_re, bc_im, cc_re, cc_im, pw_re, pw_im, dvec))


def _split_dot(x, tri2):
    hi = x.astype(BF16)
    lo = (x - hi.astype(F32)).astype(BF16)
    return jnp.dot(jnp.concatenate([hi, lo], axis=1), tri2, preferred_element_type=F32)


def _tri_and_ones(tk, rel):
    r_i = lax.broadcasted_iota(jnp.int32, (2 * tk, 2 * tk), 0) % tk
    c_i = lax.broadcasted_iota(jnp.int32, (2 * tk, 2 * tk), 1)
    return jnp.where((c_i >= tk) | rel(r_i, c_i), 1.0, 0.0).astype(BF16)


def _attn_scores(qs, kb, own):
    z = lax.dot_general(qs, kb, _NT, preferred_element_type=F32)
    log_beta = jnp.minimum(z, 0.0) - jnp.log(1.0 + jnp.exp(-jnp.abs(z)))
    log_stay = log_beta - z
    if not own:
        return log_beta, log_stay, None
    mask = lax.broadcasted_iota(jnp.int32, z.shape, 1) < lax.broadcasted_iota(jnp.int32, z.shape, 0)
    return log_beta, jnp.where(mask, log_stay, 0.0), mask


def _attn_blocks(s):
    tq, tk = min(ATT_TQ, s), min(ATT_TK, s)
    assert s % tq == 0 and tq % tk == 0
    return tq, tk, tq // tk


def _first_head(shape, axis):
    return lax.broadcasted_iota(jnp.int32, shape, axis) < HEAD_DIM


def _pair_rms(x, g):
    first = _first_head(x.shape, 1)
    sq = x * x
    ms_a = jnp.sum(jnp.where(first, sq, 0.0), axis=1, keepdims=True) * (1.0 / HEAD_DIM)
    ms_b = jnp.sum(jnp.where(first, 0.0, sq), axis=1, keepdims=True) * (1.0 / HEAD_DIM)
    r = jnp.where(first, lax.rsqrt(ms_a + RMS_EPS), lax.rsqrt(ms_b + RMS_EPS))
    return x * r * g, r


def _pair_rms_bwd(x, g, r, dn):
    first = _first_head(x.shape, 1)
    a = dn * g
    ax = a * x
    dot_a = jnp.sum(jnp.where(first, ax, 0.0), axis=1, keepdims=True) * (1.0 / HEAD_DIM)
    dot_b = jnp.sum(jnp.where(first, 0.0, ax), axis=1, keepdims=True) * (1.0 / HEAD_DIM)
    dx = r * a - x * (r * r * r * jnp.where(first, dot_a, dot_b))
    return dx, jnp.sum(dn * x * r, axis=0, keepdims=True)


_Q_BLOCK0 = 2 * SSM_WIDTH // (2 * HEAD_DIM)
_K_BLOCK0 = _Q_BLOCK0 + ATTN_WIDTH // (2 * HEAD_DIM)
_V_BLOCK0 = _K_BLOCK0 + ATTN_WIDTH // (2 * HEAD_DIM)
HEAD_PAIRS = ATTN_HEADS // 2


def _fill_keys(k_ref, v_ref, gk_ref, kn_ref, vb_ref, rows_per_step):
    def fill(c, _):
        rows = pl.ds(pl.multiple_of(c * rows_per_step, rows_per_step), rows_per_step)
        kn, _ = _pair_rms(k_ref[rows, :], gk_ref[...])
        kn_ref[rows, :] = kn.astype(BF16)
        vb_ref[rows, :] = v_ref[rows, :].astype(BF16)
        return 0

    lax.fori_loop(0, k_ref.shape[0] // rows_per_step, fill, 0)


def _call_with_ride_along(body, ride, *, name, grid, semantics, in_specs, out_specs, out_shape, scratch_shapes,
                          operands):
    if ride is None:
        return pl.pallas_call(body, name=name, grid=grid, in_specs=in_specs, out_specs=out_specs, out_shape=out_shape,
                              scratch_shapes=scratch_shapes, compiler_params=_params(semantics))(*operands)
    n, n_in, n_out = len(ride["bufs"]), len(in_specs), len(out_specs)
    into = list(ride.get("into", ()))
    first_out = n_in + n + len(into)
    last = math.prod(grid) - 1

    def wrapped(*refs):
        ins, x_refs = refs[:n_in], refs[n_in:n_in + n]
        outs, out_refs = refs[first_out:first_out + n_out], refs[first_out + n_out:first_out + n_out + n]
        scratch, sems = refs[first_out + n_out + n:-3], refs[-3:]
        step = 0
        for axis, size in enumerate(grid):
            step = step * size + pl.program_id(axis)

        def exchange():
            return _direct_exchange(x_refs, out_refs, *sems, scatter=ride["scatter"], row0s=ride["row0s"])

        @pl.when(step == 0)
        def _():
            exchange()[0]()

        body(*ins, *outs, *scratch)

        @pl.when(step == last)
        def _():
            exchange()[1]()

    any_spec = pl.BlockSpec(memory_space=pl.ANY)
    landing = [jax.ShapeDtypeStruct(b.shape, b.dtype) for b in into] or list(ride["into_shapes"])
    return pl.pallas_call(
        wrapped, name=name, grid=grid, in_specs=list(in_specs) + [any_spec] * (n + len(into)),
        out_specs=tuple(out_specs) + (any_spec,) * n, out_shape=tuple(out_shape) + tuple(landing),
        input_output_aliases={n_in + n + a: n_out + a for a in range(len(into))},
        scratch_shapes=list(scratch_shapes) + _exchange_sems(n),
        compiler_params=_params(("arbitrary",) * len(grid)))(*operands, *ride["bufs"], *into)


def _attn_fwd(proj, gq, gk, ride=None):
    s = proj.shape[0]
    tq, tk, r = _attn_blocks(s)
    pw = 2 * HEAD_DIM

    def body(q_ref, k_ref, v_ref, gq_ref, gk_ref, o_ref, b_ref, kn_ref, vb_ref):
        qi = pl.program_id(1)

        @pl.when(qi == 0)
        def _():
            _fill_keys(k_ref, v_ref, gk_ref, kn_ref, vb_ref, tq)

        qn, _ = _pair_rms(q_ref[...], gq_ref[...])
        qn = qn * (HEAD_DIM ** -0.5)
        first = _first_head((tq, pw), 1)
        tri = _tri_and_ones(tk, lambda row, col: row > col)
        qs = [jnp.where(first, qn, 0.0).astype(BF16), jnp.where(first, 0.0, qn).astype(BF16)]

        def tiles(kq, carry, own):
            o_acc, accs = carry[0], list(carry[1:])
            ks = pl.ds(pl.multiple_of(kq * tq, tq), tq)
            kb, vb = kn_ref[ks, :], vb_ref[ks, :]
            scores = [_attn_scores(q, kb, own) for q in qs]
            sums = [[_split_dot(sc[1][:, u * tk:(u + 1) * tk], tri) for u in range(r)] for sc in scores]
            ws = []
            for h in range(2):
                later = [None] * r
                for u in reversed(range(r)):
                    later[u] = accs[h] + sums[h][u][:, :tk]
                    accs[h] = accs[h] + sums[h][u][:, tk:]
                w = jnp.exp(scores[h][0] + jnp.concatenate(later, axis=1))
                if own:
                    w = jnp.where(scores[h][2], w, 0.0)
                ws.append(w.astype(BF16))
            po = [jnp.dot(w, vb, preferred_element_type=F32) for w in ws]
            return (o_acc + jnp.where(first, po[0], po[1]), *accs)

        zero = jnp.zeros((tq, tk), F32)
        carry = tiles(qi, (jnp.zeros((tq, pw), F32), zero, zero), True)
        o_acc, acc_a, acc_b = lax.fori_loop(0, qi, lambda it, c: tiles(qi - 1 - it, c, False), carry)
        o_ref[...] = o_acc
        b_ref[0] = acc_a[:, 0:1]
        b_ref[1] = acc_b[:, 0:1]

    def cols(block0):
        return pl.BlockSpec((s, pw), lambda hp, i: (0, block0 + hp))

    gain = pl.BlockSpec((1, pw), lambda hp, i: (0, 0))
    return _call_with_ride_along(
        body, ride, name="attn_fwd", grid=(HEAD_PAIRS, s // tq), semantics=("parallel", "arbitrary"),
        in_specs=[pl.BlockSpec((tq, pw), lambda hp, i: (i, _Q_BLOCK0 + hp)), cols(_K_BLOCK0), cols(_V_BLOCK0),
                  gain, gain],
        out_specs=(pl.BlockSpec((tq, pw), lambda hp, i: (i, hp)), pl.BlockSpec((2, tq, 1), lambda hp, i: (hp, i, 0))),
        out_shape=(jax.ShapeDtypeStruct((s, ATTN_WIDTH), F32), jax.ShapeDtypeStruct((ATTN_HEADS, s, 1), F32)),
        scratch_shapes=[pltpu.VMEM((s, pw), BF16), pltpu.VMEM((s, pw), BF16)],
        operands=(proj, proj, proj, gq, gk))


def _attn_bwd(proj, gq, gk, bsum, do, ride=None):
    s = proj.shape[0]
    tq, tk, r = _attn_blocks(s)
    nq = s // tq
    pw = 2 * HEAD_DIM

    def body(q_ref, k_ref, v_ref, gq_ref, gk_ref, b_ref, do_ref,
             dq_ref, dk_ref, dv_ref, dgq_ref, dgk_ref, kn_ref, vb_ref, dkt_ref, dvt_ref):
        qi = pl.program_id(1)

        @pl.when(qi == 0)
        def _():
            _fill_keys(k_ref, v_ref, gk_ref, kn_ref, vb_ref, tq)
            for ref in (dkt_ref, dvt_ref, dgq_ref):
                ref[...] = jnp.zeros_like(ref)

        scale = HEAD_DIM ** -0.5
        q = q_ref[...]
        qn, rq = _pair_rms(q, gq_ref[...])
        qf = qn * scale
        qft = qf.T
        dof = do_ref[...]
        doft = dof.T
        first = _first_head((tq, pw), 1)
        first_t = _first_head((pw, tq), 0)
        tri_upto = _tri_and_ones(tk, lambda row, col: row <= col)
        tri_before = _tri_and_ones(tk, lambda row, col: row < col)
        heads = ((first, first_t), (~first, ~first_t))
        qs = [jnp.where(sel, qf, 0.0).astype(BF16) for sel, _ in heads]
        qst = [jnp.where(sel_t, qft, 0.0).astype(BF16) for _, sel_t in heads]
        dob = [jnp.where(sel, dof, 0.0).astype(BF16) for sel, _ in heads]
        dobt = [jnp.where(sel_t, doft, 0.0).astype(BF16) for _, sel_t in heads]

        def tiles(kq, carry, own):
            dq_acc = carry[0]
            rest, g_prefix = list(carry[1:3]), list(carry[3:5])
            ks = pl.ds(pl.multiple_of(kq * tq, tq), tq)
            kb, vb = kn_ref[ks, :], vb_ref[ks, :]
            scores = [_attn_scores(q, kb, own) for q in qs]
            dws = [lax.dot_general(d, vb, _NT, preferred_element_type=F32) for d in dob]
            sums = [[_split_dot(sc[1][:, u * tk:(u + 1) * tk], tri_upto) for u in range(r)] for sc in scores]
            ws, gs = [], []
            for h in range(2):
                later = [None] * r
                for u in range(r):
                    later[u] = rest[h] - sums[h][u][:, :tk]
                    rest[h] = rest[h] - sums[h][u][:, tk:]
                w = jnp.exp(scores[h][0] + jnp.concatenate(later, axis=1))
                if own:
                    w = jnp.where(scores[h][2], w, 0.0)
                ws.append(w)
                gs.append(w * dws[h])
            gsums = [[_split_dot(g[:, u * tk:(u + 1) * tk], tri_before) for u in range(r)] for g in gs]
            dzs = []
            for h in range(2):
                before = [None] * r
                for u in range(r):
                    before[u] = g_prefix[h] + gsums[h][u][:, :tk]
                    g_prefix[h] = g_prefix[h] + gsums[h][u][:, tk:]
                dz = gs[h] - jnp.exp(scores[h][0]) * (gs[h] + jnp.concatenate(before, axis=1))
                if own:
                    dz = jnp.where(scores[h][2], dz, 0.0)
                dzs.append(dz.astype(BF16))
            pq = [jnp.dot(dz, kb, preferred_element_type=F32) for dz in dzs]
            dkt_ref[:, ks] += (jnp.dot(qst[0], dzs[0], preferred_element_type=F32)
                               + jnp.dot(qst[1], dzs[1], preferred_element_type=F32))
            dvt_ref[:, ks] += (jnp.dot(dobt[0], ws[0].astype(BF16), preferred_element_type=F32)
                               + jnp.dot(dobt[1], ws[1].astype(BF16), preferred_element_type=F32))
            return (dq_acc + jnp.where(first, pq[0], pq[1]), *rest, *g_prefix)

        zero = jnp.zeros((tq, tk), F32)
        carry = (jnp.zeros((tq, pw), F32), jnp.broadcast_to(b_ref[0], (tq, tk)), jnp.broadcast_to(b_ref[1], (tq, tk)),
                 zero, zero)
        carry = lax.fori_loop(0, qi, lambda kq, c: tiles(kq, c, False), carry)
        dq, dgq = _pair_rms_bwd(q, gq_ref[...], rq, tiles(qi, carry, True)[0] * scale)
        dq_ref[...] = dq
        dgq_ref[0] += dgq

        @pl.when(qi == nq - 1)
        def _():
            def finish(c, dgk):
                rows = pl.ds(pl.multiple_of(c * tq, tq), tq)
                k = k_ref[rows, :]
                _, rk = _pair_rms(k, gk_ref[...])
                dk, dgk_c = _pair_rms_bwd(k, gk_ref[...], rk, dkt_ref[:, rows].T)
                dk_ref[rows, :] = dk
                dv_ref[rows, :] = dvt_ref[:, rows].T
                return dgk + dgk_c

            dgk_ref[0] = lax.fori_loop(0, nq, finish, jnp.zeros((1, pw), F32))

    def cols(block0):
        return pl.BlockSpec((s, pw), lambda hp, i: (0, block0 + hp))

    gain = pl.BlockSpec((1, pw), lambda hp, i: (0, 0))
    q_rows = pl.BlockSpec((tq, pw), lambda hp, i: (i, hp))
    all_rows = pl.BlockSpec((s, pw), lambda hp, i: (0, hp))
    pair_gain = pl.BlockSpec((1, 1, pw), lambda hp, i: (hp, 0, 0))
    wide = jax.ShapeDtypeStruct((s, ATTN_WIDTH), F32)
    gains = jax.ShapeDtypeStruct((HEAD_PAIRS, 1, pw), F32)
    return _call_with_ride_along(
        body, ride, name="attn_bwd", grid=(HEAD_PAIRS, nq), semantics=("parallel", "arbitrary"),
        in_specs=[pl.BlockSpec((tq, pw), lambda hp, i: (i, _Q_BLOCK0 + hp)), cols(_K_BLOCK0), cols(_V_BLOCK0),
                  gain, gain, pl.BlockSpec((2, tq, 1), lambda hp, i: (hp, i, 0)), q_rows],
        out_specs=(q_rows, all_rows, all_rows, pair_gain, pair_gain),
        out_shape=(wide, wide, wide, gains, gains),
        scratch_shapes=[pltpu.VMEM((s, pw), BF16), pltpu.VMEM((s, pw), BF16),
                        pltpu.VMEM((pw, s), F32), pltpu.VMEM((pw, s), F32)],
        operands=(proj, proj, proj, gq, gk, bsum, do))


def _reduce_adamw(parts, w, m, v, *, name, row0s=(0,)):
    n, _, cols = parts.shape
    rows = w.shape[0]
    seg_rows = rows // len(row0s)
    tile_rows = 2 * SUBLANES
    tr = max(t for t in range(tile_rows, min(seg_rows, 512) + 1, tile_rows)
             if seg_rows % t == 0 and all(r0 % t == 0 for r0 in row0s))
    seg_tiles = seg_rows // tr

    def parts_tile(i):
        tile = row0s[0] // tr + i
        for l in range(1, len(row0s)):
            tile = jnp.where(i >= l * seg_tiles, row0s[l] // tr + i - l * seg_tiles, tile)
        return tile

    c1 = 1.0 - ADAM_B1 ** ADAM_STEP
    c2 = 1.0 - ADAM_B2 ** ADAM_STEP

    def body(p_ref, w_ref, m_ref, v_ref, g_ref, d_ref, nm_ref, nv_ref):
        g = p_ref[0].astype(F32)
        for i in range(1, n):
            g = g + p_ref[i].astype(F32)
        nm = ADAM_B1 * m_ref[...] + (1.0 - ADAM_B1) * g
        nv = ADAM_B2 * v_ref[...] + (1.0 - ADAM_B2) * (g * g)
        g_ref[...] = g
        nm_ref[...] = nm
        nv_ref[...] = nv
        d_ref[...] = -ADAM_LR * ((nm / c1) / (jnp.sqrt(nv / c2) + ADAM_EPS) + ADAM_WD * w_ref[...])

    row = pl.BlockSpec((tr, cols), lambda i: (i, 0))
    out = jax.ShapeDtypeStruct((rows, cols), F32)
    return pl.pallas_call(
        body, name=name, grid=(rows // tr,),
        in_specs=[pl.BlockSpec((n, tr, cols), lambda i: (0, parts_tile(i), 0)), row, row, row],
        out_specs=(row,) * 4, out_shape=(out,) * 4, compiler_params=_params(("parallel",)))(parts, w, m, v)


def _all_gather(shards, *, name):
    n_arr = len(shards)

    def body(*refs):
        x_refs, out_refs = refs[:n_arr], refs[n_arr:2 * n_arr]
        send_sems, recv_sems, local_sems = refs[2 * n_arr:]
        x, y, c = lax.axis_index("x"), lax.axis_index("y"), lax.axis_index("c")
        me, sibling = (x, y, c), (x, y, 1 - c)
        chips = [(1 - x, y), (x, 1 - y), (1 - x, 1 - y)]

        def slot(a, px, py, pc):
            return out_refs[a].at[4 * px + 2 * py + pc]

        def copy(a, k, block, to, src=None):
            return pltpu.make_async_remote_copy(
                src_ref=slot(a, *block) if src is None else src, dst_ref=slot(a, *block),
                send_sem=send_sems.at[a, k], recv_sem=recv_sems.at[a, k], device_id=to, device_id_type=MESH_IDS)

        arrays = range(n_arr)
        mine = [pltpu.make_async_copy(x_refs[a], slot(a, *me), local_sems.at[a]) for a in arrays]
        for cp in mine:
            cp.start()
        first = [copy(a, 0, me, sibling, src=x_refs[a]) for a in arrays]
        first += [copy(a, 1 + j, me, (*chip, c), src=x_refs[a]) for j, chip in enumerate(chips) for a in arrays]
        for cp in first:
            cp.start()
        passed = []
        for j, chip in enumerate(chips):
            for a in arrays:
                copy(a, 1 + j, (*chip, c), me).wait_recv()
                passed.append(copy(a, 4 + j, (*chip, c), sibling))
                passed[-1].start()
        for a in arrays:
            copy(a, 0, sibling, me).wait_recv()
        for j, chip in enumerate(chips):
            for a in arrays:
                copy(a, 4 + j, (*chip, 1 - c), me).wait_recv()
        for cp in first + passed:
            cp.wait_send()
        for cp in mine:
            cp.wait()

    any_spec = pl.BlockSpec(memory_space=pl.ANY)
    return pl.pallas_call(
        body, name=name, out_shape=tuple(jax.ShapeDtypeStruct((N_DEV,) + s.shape, s.dtype) for s in shards),
        in_specs=[any_spec] * n_arr, out_specs=(any_spec,) * n_arr,
        scratch_shapes=[pltpu.SemaphoreType.DMA((n_arr, 7)), pltpu.SemaphoreType.DMA((n_arr, 7)),
                        pltpu.SemaphoreType.DMA((n_arr,))],
    )(*shards)


def _direct_exchange(x_refs, out_refs, send_sems, recv_sems, local_sems, *, scatter, row0s):
    arrays = range(len(x_refs))
    x, y, c = lax.axis_index("x"), lax.axis_index("y"), lax.axis_index("c")
    me = 4 * x + 2 * y + c

    def src(a, slot):
        return x_refs[a].at[slot] if scatter else x_refs[a]

    def landing(a, sender):
        return out_refs[a].at[sender, pl.ds(row0s[a], x_refs[a].shape[-2]), :]

    def local(a):
        return pltpu.make_async_copy(src(a, me), landing(a, me), local_sems.at[a])

    def copy(a, k, arriving):
        px, py, pc = x ^ (k >> 2), y ^ ((k >> 1) & 1), c ^ (k & 1)
        peer = 4 * px + 2 * py + pc
        return pltpu.make_async_remote_copy(
            src_ref=src(a, peer), dst_ref=landing(a, peer if arriving else me),
            send_sem=send_sems.at[a, k - 1], recv_sem=recv_sems.at[a, k - 1],
            device_id=(px, py, pc), device_id_type=MESH_IDS)

    def start():
        for a in arrays:
            local(a).start()
        for k in range(1, N_DEV):
            for a in arrays:
                copy(a, k, False).start()

    def wait():
        for k in range(1, N_DEV):
            for a in arrays:
                copy(a, k, True).wait_recv()
        for k in range(1, N_DEV):
            for a in arrays:
                copy(a, k, False).wait_send()
        for a in arrays:
            local(a).wait()

    return start, wait


def _exchange_sems(n_arr):
    return [pltpu.SemaphoreType.DMA((n_arr, N_DEV - 1)), pltpu.SemaphoreType.DMA((n_arr, N_DEV - 1)),
            pltpu.SemaphoreType.DMA((n_arr,))]


def _pack_small(vals, tail=0.0):
    flat = jnp.concatenate([vals[n].reshape(-1) for n in SMALL] + [jnp.full((1,), tail, F32)])
    rows = -(-flat.shape[0] // (SUBLANES * PACK_COLS)) * SUBLANES
    return jnp.pad(flat, (0, rows * PACK_COLS - flat.shape[0])).reshape(rows, PACK_COLS)


def _unpack_small(flat, shapes):
    flat = flat.reshape(-1)
    out, off = {}, 0
    for n in SMALL:
        size = math.prod(shapes[n])
        out[n] = flat[off:off + size].reshape(shapes[n])
        off += size
    return out


def _pair_gain(g):
    return jnp.tile(g, 2)[None]


def _ssm_setup(sp):
    g, p = SSM_GROUPS, SSM_STATE
    a_re = sp["ssm_a_re"][:, None, :]
    a_im = sp["ssm_a_im"][:, None, :]
    log_dt = jnp.broadcast_to(sp["ssm_log_dt"][:, None, None], (g, 1, p))
    bt_re = sp["ssm_b_re"].transpose(0, 2, 1)
    bt_im = sp["ssm_b_im"].transpose(0, 2, 1)
    return a_re, a_im, log_dt, bt_re, bt_im


def _layer_fwd(h, p_l, w, sp, ride=None):
    hn = _rms_fwd(h, sp["mix_norm_g"][None], name="rms_mix")
    proj = _mm(hn, w["w_in"], name="mm_proj")

    pw_re, pw_im, bbt_re, bbt_im = _ssm_param_fwd(*_ssm_setup(sp))
    pw_re = pw_re.transpose(1, 0, 2).reshape(SUBLANES, SSM_LANES)
    pw_im = pw_im.transpose(1, 0, 2).reshape(SUBLANES, SSM_LANES)
    ssm_mats = (_block_diag(bbt_re).astype(BF16), _block_diag(bbt_im).astype(BF16),
                _block_diag(sp["ssm_c_re"]).astype(BF16), _block_diag(-sp["ssm_c_im"]).astype(BF16),
                pw_re, pw_im, sp["ssm_d"].reshape(1, SSM_WIDTH))
    x_re, x_im, y, z = _ssm_fwd(proj, *ssm_mats)
    zz = _mm(z, w["ssm_w_glu"], name="mm_glu")

    o, bsum, *landed = _attn_fwd(proj, _pair_gain(sp["q_norm_g"]), _pair_gain(sp["k_norm_g"]), ride)

    bglu = sp["ssm_b_glu"][None]
    ycat = _gate_fwd(zz, bglu, proj, o)
    h2, hn2 = _mm_residual_norm(ycat, w["w_out"], h, sp["ple_norm_g"][None], name="mm_out")
    pp = _mm(p_l, w["w_ple_proj"], name="mm_ple_proj")
    pgl, h3 = _mm_gated_add(hn2, w["w_ple_gate"], h2, pp, name="mm_ple_gate")
    saved = dict(h=h, hn=hn, proj=proj, ssm_mats=ssm_mats, x_re=x_re, x_im=x_im, y=y, z=z,
                 zz=zz, bsum=bsum, o=o, ycat=ycat, h2=h2, hn2=hn2,
                 pgl=pgl, pp=pp, p_l=p_l, bglu=bglu)
    return h3, saved, landed


def _layer_bwd(dh3, sv, w, sp, ship=None):
    g, hh, p = SSM_GROUPS, SSM_GROUP, SSM_STATE
    grads = {}
    ship = ship or (lambda stage, grads_so_far, parts: None)
    dgp, dpp = _ple_bwd(dh3, sv["pgl"], sv["pp"])
    grads["w_ple_gate"] = _mm(sv["hn2"], dgp, ta=True, out_dtype=BF16, name="mm_d_ple_gate")
    grads["w_ple_proj"] = _mm(sv["p_l"], dpp, ta=True, out_dtype=BF16, name="mm_d_ple_proj")
    dh2, dg2 = _mm_norm_grad(dgp, w["w_ple_gate"], sv["h2"], sp["ple_norm_g"][None], dh3, name="mm_dhn2")
    grads["ple_norm_g"] = dg2[0]
    grads["w_out"] = _mm(sv["ycat"], dh2, ta=True, out_dtype=BF16, name="mm_d_out")
    dyc = _mm(dh2, w["w_out"], tb=True, name="mm_dycat")
    dzz, dgs, do, dga, dbglu = _gate_bwd(dyc, sv["zz"], sv["bglu"], sv["proj"], sv["o"])
    grads["ssm_b_glu"] = dbglu[0]

    dq, dk, dv, dgq, dgk, *parts = _attn_bwd(sv["proj"], _pair_gain(sp["q_norm_g"]), _pair_gain(sp["k_norm_g"]),
                                             sv["bsum"], do, ship("attn", grads, None))
    grads["q_norm_g"] = jnp.sum(dgq.reshape(ATTN_HEADS, HEAD_DIM), axis=0)
    grads["k_norm_g"] = jnp.sum(dgk.reshape(ATTN_HEADS, HEAD_DIM), axis=0)

    grads["ssm_w_glu"] = _mm(sv["z"], dzz, ta=True, out_dtype=BF16, name="mm_d_glu")
    dz = _mm(dzz, w["ssm_w_glu"], tb=True, name="mm_dz")
    du, dbc_re, dbc_im, dcc_re, dcc_im, gab_re, gab_im, dd, *landed = _ssm_bwd(
        dz, sv["y"], sv["proj"], sv["x_re"], sv["x_im"], *sv["ssm_mats"], ride=ship("ssm", grads, parts))
    parts = parts[:len(parts) - len(landed)] + landed
    grads["ssm_d"] = dd.reshape(g, hh)
    grads["ssm_c_re"] = _block_diag_take(dcc_re, hh, p)
    grads["ssm_c_im"] = -_block_diag_take(dcc_im, hh, p)
    da_re, da_im, dldt, dbt_re, dbt_im = _ssm_param_bwd(
        *_ssm_setup(sp), gab_re.reshape(g, 1, p), gab_im.reshape(g, 1, p),
        _block_diag_take(dbc_re, hh, p), _block_diag_take(dbc_im, hh, p))
    grads["ssm_a_re"], grads["ssm_a_im"] = da_re[:, 0, :], da_im[:, 0, :]
    grads["ssm_log_dt"] = dldt[:, 0, 0]
    grads["ssm_b_re"], grads["ssm_b_im"] = dbt_re.transpose(0, 2, 1), dbt_im.transpose(0, 2, 1)

    dproj = jnp.concatenate([du, dgs, dq, dk, dv, dga], axis=1)
    grads["w_in"] = _mm(sv["hn"], dproj, ta=True, out_dtype=BF16, name="mm_d_in")
    dh, dg1, *landed = _mm_norm_grad(dproj, w["w_in"], sv["h"], sp["mix_norm_g"][None], dh2, name="mm_dhn",
                                     ride=ship("dhn", grads, parts))
    parts = landed + parts[len(landed):]
    grads["mix_norm_g"] = dg1[0]
    return dh, grads, parts


def _local_step(x, p, target, big, small, fetch=None, ship=None):
    h = x
    big = list(big)
    saved = []
    for l in range(DEPTH):
        sp = {n: small[n][l] for n in SMALL}
        h, sv, landed = _layer_fwd(h, p[l], big[l], sp, fetch[0] if fetch and l == 0 else None)
        if landed:
            big[DEPTH - 1] = fetch[1](landed)
        saved.append(sv)
    dh, loss_parts = _loss_head(h, target)
    grads = [None] * DEPTH
    shipped = []
    for l in reversed(range(DEPTH)):
        sp = {n: small[n][l] for n in SMALL}
        carry = None
        if ship and l == 0:
            def carry(stage, grads_so_far, parts):
                return ship(stage, grads[DEPTH - 1], grads_so_far, parts)
        dh, grads[l], landed = _layer_bwd(dh, saved[l], big[l], sp, carry)
        shipped = landed or shipped
    return jnp.sum(loss_parts), dh, grads, shipped


def kernel(x, p, mix_norm_g, w_in, ssm_a_re, ssm_a_im, ssm_log_dt, ssm_b_re, ssm_b_im, ssm_c_re, ssm_c_im, ssm_d, ssm_w_glu, ssm_b_glu, q_norm_g, k_norm_g, w_out, ple_norm_g, w_ple_gate, w_ple_proj, loss_target, m_mix_norm_g, m_w_in, m_ssm_a_re, m_ssm_a_im, m_ssm_log_dt, m_ssm_b_re, m_ssm_b_im, m_ssm_c_re, m_ssm_c_im, m_ssm_d, m_ssm_w_glu, m_ssm_b_glu, m_q_norm_g, m_k_norm_g, m_w_out, m_ple_norm_g, m_w_ple_gate, m_w_ple_proj, v_mix_norm_g, v_w_in, v_ssm_a_re, v_ssm_a_im, v_ssm_log_dt, v_ssm_b_re, v_ssm_b_im, v_ssm_c_re, v_ssm_c_im, v_ssm_d, v_ssm_w_glu, v_ssm_b_glu, v_q_norm_g, v_k_norm_g, v_w_out, v_ple_norm_g, v_w_ple_gate, v_w_ple_proj):
    given = dict(locals())
    wts = {n: given[n] for n in WEIGHTS}
    mom = {n: given["m_" + n] for n in WEIGHTS}
    var = {n: given["v_" + n] for n in WEIGHTS}

    def rows2d(a):
        return a.reshape(-1, a.shape[-1])

    assert DEPTH == 2
    groups = (("w_in",), ("ssm_w_glu", "w_ple_proj"), ("w_out", "w_ple_gate"))
    layer_rows = [sum(wts[n].shape[1] for n in names) for names in groups]
    widths = [wts[names[0]].shape[2] for names in groups]

    def shards_of(l):
        return [jnp.concatenate([wts[n][l].astype(BF16) for n in names], axis=0) for names in groups]

    def weights_of(gathered):
        full = {}
        for names, got in zip(groups, gathered):
            row0 = 0
            for n in names:
                r = wts[n].shape[1]
                blocks = got[:, row0:row0 + r, :]
                full[n] = blocks.reshape(N_DEV * r, -1) if n in ROW_SHARDED else blocks.transpose(1, 0, 2).reshape(r, -1)
                row0 += r
        return full

    def blocks_to_send(layer_grads, which):
        def blocks(n):
            g = layer_grads[n]
            if n in ROW_SHARDED:
                return g.reshape(N_DEV, -1, g.shape[1])
            return g.reshape(g.shape[0], N_DEV, -1).transpose(1, 0, 2)

        return [jnp.concatenate([blocks(n) for n in groups[gi]], axis=1) for gi in which]

    fetch = (dict(bufs=shards_of(1), scatter=False, row0s=[0] * len(groups),
                  into_shapes=[jax.ShapeDtypeStruct((N_DEV, r, c), BF16) for r, c in zip(layer_rows, widths)]),
             weights_of)

    def ship(stage, last_grads, first_grads, parts):
        if stage == "attn":
            return dict(bufs=blocks_to_send(last_grads, (0, 1, 2)), scatter=True, row0s=layer_rows, into_shapes=[
                jax.ShapeDtypeStruct((N_DEV, DEPTH * r, c), BF16) for r, c in zip(layer_rows, widths)])
        which = (1, 2) if stage == "ssm" else (0,)
        return dict(bufs=blocks_to_send(first_grads, which), scatter=True, row0s=[0] * len(which),
                    into=[parts[gi] for gi in which])

    first = weights_of(_all_gather(shards_of(0), name="gather_weights"))
    small = {n: wts[n] for n in SMALL}
    loss, dx, grads, parts = _local_step(x[0], p[:, 0], loss_target[0], [first, None], small, fetch, ship)

    big_out = {}
    for names, part, rows in zip(groups, parts, layer_rows):
        row0 = 0
        for n in names:
            res = _reduce_adamw(part, rows2d(wts[n]), rows2d(mom[n]), rows2d(var[n]),
                                row0s=[l * rows + row0 for l in range(DEPTH)], name="adamw_" + n)
            big_out[n] = [a.reshape(wts[n].shape) for a in res]
            row0 += wts[n].shape[1]

    small_shapes = {n: wts[n].shape for n in SMALL}
    local_small = _pack_small({n: jnp.stack([grads[l][n] for l in range(DEPTH)]) for n in SMALL}, tail=loss)
    (all_small,) = _all_gather([local_small], name="gather_small_grads")
    small_out = _reduce_adamw(all_small, _pack_small(small), _pack_small({n: mom[n] for n in SMALL}),
                              _pack_small({n: var[n] for n in SMALL}), name="adamw_replicated")
    loss = small_out[0].reshape(-1)[sum(math.prod(s) for s in small_shapes.values())]
    small_out = [_unpack_small(a, small_shapes) for a in small_out]

    outs = [loss, dx[None]]
    for k in range(4):
        outs += [big_out[n][k] if n in BIG else small_out[k][n] for n in WEIGHTS]
    return tuple(outs)
```

```python
import math

import jax
import jax.numpy as jnp
from jax import lax
from jax.experimental import pallas as pl
from jax.experimental.pallas import tpu as pltpu

F32 = jnp.float32
BF16 = jnp.bfloat16
MESH_IDS = pl.DeviceIdType.MESH

N_DEV = 8
D_MODEL = 1024
DEPTH = 2
PLE_DIM = 256
SSM_WIDTH = 512
SSM_GROUP = 16
SSM_GROUPS = 32
SSM_STATE = 64
SSM_LANES = SSM_GROUPS * SSM_STATE
ATTN_WIDTH = 512
ATTN_HEADS = 8
HEAD_DIM = 64
RMS_EPS = 1e-6
ADAM_LR = 0.001
ADAM_B1 = 0.9
ADAM_B2 = 0.999
ADAM_EPS = 1e-08
ADAM_WD = 0.01
ADAM_STEP = 10

VMEM_LIMIT = 56 * 1024 * 1024
ATT_TQ = 512
ATT_TK = 128
SSM_BLOCK_GROUPS = 8
SSM_TIME_BLOCK = 1024
SUBLANES = 8
PACK_COLS = 1024

BIG = ("w_in", "ssm_w_glu", "w_out", "w_ple_gate", "w_ple_proj")
ROW_SHARDED = ("w_out", "w_ple_gate")
SMALL = ("mix_norm_g", "ssm_a_re", "ssm_a_im", "ssm_log_dt", "ssm_b_re", "ssm_b_im", "ssm_c_re",
         "ssm_c_im", "ssm_d", "ssm_b_glu", "q_norm_g", "k_norm_g", "ple_norm_g")
WEIGHTS = ("mix_norm_g", "w_in", "ssm_a_re", "ssm_a_im", "ssm_log_dt", "ssm_b_re", "ssm_b_im",
           "ssm_c_re", "ssm_c_im", "ssm_d", "ssm_w_glu", "ssm_b_glu", "q_norm_g", "k_norm_g",
           "w_out", "ple_norm_g", "w_ple_gate", "w_ple_proj")
_NT = (((1,), (1,)), ((), ()))
_TN = (((0,), (0,)), ((), ()))


def _params(sem=None):
    return pltpu.CompilerParams(dimension_semantics=sem, vmem_limit_bytes=VMEM_LIMIT)


def _sigmoid(x):
    return 0.5 * jnp.tanh(0.5 * x) + 0.5


_GELU_K = math.sqrt(2.0 / math.pi)
_GELU_C = 0.044715


def _gelu(x):
    return 0.5 * x * (1.0 + jnp.tanh(_GELU_K * (x + _GELU_C * x * x * x)))


def _gelu_grad(x):
    th = jnp.tanh(_GELU_K * (x + _GELU_C * x * x * x))
    return 0.5 * (1.0 + th) + 0.5 * x * (1.0 - th * th) * _GELU_K * (1.0 + 3.0 * _GELU_C * x * x)


def _mm(a, b, *, name, ta=False, tb=False, add=None, out_dtype=F32, tm=1024, tn=1024, tk=1024,
        finish=None, extras=(), outs=None, ride=None):
    m, k = (a.shape[1], a.shape[0]) if ta else a.shape
    n = b.shape[0] if tb else b.shape[1]
    tm, tn, tk = min(tm, m), min(tn, n), min(tk, k)
    assert m % tm == 0 and n % tn == 0 and k % tk == 0, (name, a.shape, b.shape)
    n_steps = k // tk
    dims = (((0 if ta else 1,), (1 if tb else 0,)), ((), ()))
    a_spec = (pl.BlockSpec((tk, tm), lambda i, j, kk: (kk, i)) if ta
              else pl.BlockSpec((tm, tk), lambda i, j, kk: (i, kk)))
    b_spec = (pl.BlockSpec((tn, tk), lambda i, j, kk: (j, kk)) if tb
              else pl.BlockSpec((tk, tn), lambda i, j, kk: (kk, j)))
    o_spec = pl.BlockSpec((tm, tn), lambda i, j, kk: (i, j))
    ins, in_specs = [a, b], [a_spec, b_spec]
    if finish is None:
        if add is not None:
            ins.append(add)
            in_specs.append(o_spec)

        def finish(res, extra_refs, out_refs):
            if add is not None:
                res = res + extra_refs[0][...].astype(F32)
            out_refs[0][...] = res.astype(out_dtype)

        out_specs, out_shapes = [o_spec], [jax.ShapeDtypeStruct((m, n), out_dtype)]
    else:
        assert tn == n and add is None
        for arr, kind in extras:
            ins.append(arr)
            cols = arr.shape[1]
            in_specs.append(pl.BlockSpec((tm, cols), lambda i, j, kk: (i, 0)) if kind == "rows"
                            else pl.BlockSpec((1, cols), lambda i, j, kk: (0, 0)))
        out_specs = [pl.BlockSpec((tm, cols), lambda i, j, kk: (i, 0)) if kind == "rows"
                     else pl.BlockSpec((1, cols), lambda i, j, kk: (0, 0)) for cols, _, kind in outs]
        out_shapes = [jax.ShapeDtypeStruct((m if kind == "rows" else 1, cols), dtype) for cols, dtype, kind in outs]
    n_in = len(ins)

    def body(*refs):
        a_ref, b_ref = refs[0], refs[1]
        product = lax.dot_general(a_ref[...].astype(BF16), b_ref[...].astype(BF16), dims, preferred_element_type=F32)
        if n_steps == 1:
            finish(product, refs[2:n_in], refs[n_in:])
            return
        acc_ref = refs[-1]
        kk = pl.program_id(2)

        @pl.when(kk == 0)
        def _():
            acc_ref[...] = product

        @pl.when(kk > 0)
        def _():
            acc_ref[...] += product

        @pl.when(kk == n_steps - 1)
        def _():
            finish(acc_ref[...], refs[2:n_in], refs[n_in:-1])

    sums = outs is not None and any(kind == "sum" for _, _, kind in outs)
    res = _call_with_ride_along(
        body, ride, name=name, grid=(m // tm, n // tn, n_steps),
        semantics=("arbitrary",) * 3 if sums else ("parallel", "parallel", "arbitrary"),
        in_specs=in_specs, out_specs=tuple(out_specs), out_shape=tuple(out_shapes),
        scratch_shapes=[pltpu.VMEM((tm, tn), F32)] if n_steps > 1 else [], operands=ins)
    return res[0] if outs is None and ride is None else res


def _rms(hv, g):
    r = lax.rsqrt(jnp.mean(hv * hv, axis=-1, keepdims=True) + RMS_EPS)
    return hv * r * g, r


def _rms_grad(hv, g, dn):
    _, r = _rms(hv, g)
    a = dn * g
    dot = jnp.mean(a * hv, axis=-1, keepdims=True)
    return r * a - hv * (r * r * r * dot), jnp.sum(dn * hv * r, axis=0, keepdims=True)


def _mm_residual_norm(a, b, h, g, *, name):
    def finish(res, extra_refs, out_refs):
        h_ref, g_ref = extra_refs
        h2 = res + h_ref[...]
        out_refs[0][...] = h2
        out_refs[1][...] = _rms(h2, g_ref[...])[0].astype(BF16)

    d = b.shape[1]
    return _mm(a, b, name=name, finish=finish, extras=[(h, "rows"), (g, "vec")],
               outs=[(d, F32, "rows"), (d, BF16, "rows")])


def _mm_gated_add(a, b, h2, pp, *, name):
    def finish(res, extra_refs, out_refs):
        h_ref, p_ref = extra_refs
        out_refs[0][...] = res
        out_refs[1][...] = h_ref[...] + _sigmoid(res) * p_ref[...]

    d = b.shape[1]
    return _mm(a, b, name=name, finish=finish, extras=[(h2, "rows"), (pp, "rows")],
               outs=[(d, F32, "rows"), (d, F32, "rows")])


def _mm_norm_grad(dy, w, h, g, dres, *, name, ride=None):
    def finish(res, extra_refs, out_refs):
        h_ref, g_ref, r_ref = extra_refs
        dh, dg = _rms_grad(h_ref[...], g_ref[...], res)
        out_refs[0][...] = dh + r_ref[...]

        @pl.when(pl.program_id(0) == 0)
        def _():
            out_refs[1][...] = jnp.zeros_like(out_refs[1])

        out_refs[1][...] += dg

    d = w.shape[0]
    return _mm(dy, w, name=name, tb=True, finish=finish, extras=[(h, "rows"), (g, "vec"), (dres, "rows")],
               outs=[(d, F32, "rows"), (d, F32, "sum")], ride=ride)


def _rows_tile(rows, want=512):
    t = min(rows, want)
    assert rows % t == 0
    return t


def _rowwise(body, name, rows, ins, outs):
    widest = max(cols for _, cols, _ in ins)
    tm = _rows_tile(rows, 512 if widest > 128 else 4096)
    in_specs = []
    for _, cols, cb in ins:
        if cb is None:
            in_specs.append(pl.BlockSpec((1, cols), lambda i: (0, 0)))
        else:
            in_specs.append(pl.BlockSpec((tm, cols), lambda i, cb=cb: (i, cb)))
    out_specs, out_shapes = [], []
    for cols, dtype, is_acc in outs:
        if is_acc:
            out_specs.append(pl.BlockSpec((1, cols), lambda i: (0, 0)))
            out_shapes.append(jax.ShapeDtypeStruct((1, cols), dtype))
        else:
            out_specs.append(pl.BlockSpec((tm, cols), lambda i: (i, 0)))
            out_shapes.append(jax.ShapeDtypeStruct((rows, cols), dtype))
    any_acc = any(o[2] for o in outs)
    return pl.pallas_call(
        body, name=name, grid=(rows // tm,), in_specs=in_specs, out_specs=tuple(out_specs),
        out_shape=tuple(out_shapes),
        compiler_params=_params(("arbitrary",) if any_acc else ("parallel",)))(*[a for a, _, _ in ins])


def _accumulate(ref, value):
    @pl.when(pl.program_id(0) == 0)
    def _():
        ref[...] = jnp.zeros_like(ref)

    ref[...] += value


def _rms_fwd(h, g, *, name):
    rows, d = h.shape

    def body(h_ref, g_ref, o_ref):
        o_ref[...] = _rms(h_ref[...], g_ref[...])[0].astype(BF16)

    return _rowwise(body, name, rows, [(h, d, 0), (g, d, None)], [(d, BF16, False)])[0]


def _gate_fwd(zz, bglu, proj, o):
    rows = zz.shape[0]
    w = SSM_WIDTH

    def body(zz_ref, b_ref, gs_ref, o_ref, ga_ref, y_ref):
        zz_v = zz_ref[...] + b_ref[...]
        val, gate = zz_v[:, :w], zz_v[:, w:]
        gs, ga = gs_ref[...], ga_ref[...]
        y_ref[:, :w] = (val * _sigmoid(gate) * (gs * _sigmoid(gs))).astype(BF16)
        y_ref[:, w:] = (o_ref[...] * (ga * _sigmoid(ga))).astype(BF16)

    return _rowwise(body, "gate_fwd", rows,
                    [(zz, 2 * w, 0), (bglu, 2 * w, None), (proj, w, 1), (o, w, 0), (proj, w, 5)],
                    [(2 * w, BF16, False)])[0]


def _gate_bwd(dyc, zz, bglu, proj, o):
    rows = zz.shape[0]
    w = SSM_WIDTH

    def body(dy_ref, zz_ref, b_ref, gs_ref, o_ref, ga_ref, dzz_ref, dgs_ref, do_ref, dga_ref, db_ref):
        zz_v = zz_ref[...] + b_ref[...]
        val, gate = zz_v[:, :w], zz_v[:, w:]
        gs, ga = gs_ref[...], ga_ref[...]
        dys, dya = dy_ref[:, :w], dy_ref[:, w:]
        sg, ss, sa = _sigmoid(gate), _sigmoid(gs), _sigmoid(ga)
        glu = val * sg
        dglu = dys * (gs * ss)
        dgs_ref[...] = dys * glu * (ss * (1.0 + gs * (1.0 - ss)))
        dval = dglu * sg
        dgate = dglu * val * sg * (1.0 - sg)
        dzz_ref[:, :w] = dval.astype(BF16)
        dzz_ref[:, w:] = dgate.astype(BF16)
        do_ref[...] = dya * (ga * sa)
        dga_ref[...] = dya * o_ref[...] * (sa * (1.0 + ga * (1.0 - sa)))
        _accumulate(db_ref, jnp.concatenate([jnp.sum(dval, axis=0, keepdims=True),
                                             jnp.sum(dgate, axis=0, keepdims=True)], axis=1))

    return _rowwise(body, "gate_bwd", rows,
                    [(dyc, 2 * w, 0), (zz, 2 * w, 0), (bglu, 2 * w, None), (proj, w, 1), (o, w, 0), (proj, w, 5)],
                    [(2 * w, BF16, False), (w, F32, False), (w, F32, False), (w, F32, False), (2 * w, F32, True)])


def _ple_bwd(dh3, pgl, pp):
    rows, d = dh3.shape

    def body(dh_ref, g_ref, p_ref, dg_ref, dp_ref):
        sg = _sigmoid(g_ref[...])
        dh = dh_ref[...]
        dg_ref[...] = (dh * p_ref[...] * sg * (1.0 - sg)).astype(BF16)
        dp_ref[...] = (dh * sg).astype(BF16)

    return _rowwise(body, "ple_bwd", rows, [(dh3, d, 0), (pgl, d, 0), (pp, d, 0)],
                    [(d, BF16, False), (d, BF16, False)])


def _loss_head(h, target):
    rows, d = h.shape

    def body(h_ref, t_ref, dh_ref, l_ref):
        err = h_ref[...] - t_ref[...]
        dh_ref[...] = err * (1.0 / d)
        _accumulate(l_ref, jnp.sum(err * err, axis=0, keepdims=True) * (0.5 / d))

    return _rowwise(body, "loss_head", rows, [(h, d, 0), (target, d, 0)], [(d, F32, False), (d, F32, True)])


def _zoh(lr, li, dt):
    mag = jnp.exp(lr * dt)
    ab_re = mag * jnp.cos(li * dt)
    ab_im = mag * jnp.sin(li * dt)
    num_re = ab_re - 1.0
    den = lr * lr + li * li
    f_re = (num_re * lr + ab_im * li) / den
    f_im = (ab_im * lr - num_re * li) / den
    return ab_re, ab_im, f_re, f_im, den


def _ssm_param_fwd(a_re, a_im, log_dt, bt_re, bt_im):
    g, _, p = a_re.shape
    h = bt_re.shape[1]

    def body(lr_ref, li_ref, ldt_ref, br_ref, bi_ref, pr_ref, pi_ref, bbr_ref, bbi_ref):
        dt = jnp.exp(ldt_ref[...])
        ab_re, ab_im, f_re, f_im, _ = _zoh(lr_ref[...], li_ref[...], dt)
        cr, ci = ab_re, ab_im
        for k in range(SUBLANES):
            pr_ref[:, k:k + 1, :] = cr
            pi_ref[:, k:k + 1, :] = ci
            cr, ci = cr * ab_re - ci * ab_im, cr * ab_im + ci * ab_re
        br, bi = br_ref[...], bi_ref[...]
        bbr_ref[...] = f_re * br - f_im * bi
        bbi_ref[...] = f_re * bi + f_im * br

    pw = jax.ShapeDtypeStruct((g, SUBLANES, p), F32)
    large = jax.ShapeDtypeStruct((g, h, p), F32)
    return pl.pallas_call(body, name="ssm_param_fwd", out_shape=(pw, pw, large, large),
                          compiler_params=_params())(a_re, a_im, log_dt, bt_re, bt_im)


def _ssm_param_bwd(a_re, a_im, log_dt, bt_re, bt_im, gab_re, gab_im, gbb_re, gbb_im):
    g, _, p = a_re.shape
    h = bt_re.shape[1]

    def body(lr_ref, li_ref, ldt_ref, br_ref, bi_ref, gar_ref, gai_ref, gbr_ref, gbi_ref,
             dlr_ref, dli_ref, dldt_ref, dbr_ref, dbi_ref):
        lr, li = lr_ref[...], li_ref[...]
        dt = jnp.exp(ldt_ref[...])
        ab_re, ab_im, f_re, f_im, den = _zoh(lr, li, dt)
        br, bi = br_ref[...], bi_ref[...]
        gbr, gbi = gbr_ref[...], gbi_ref[...]
        dbr_ref[...] = f_re * gbr + f_im * gbi
        dbi_ref[...] = f_re * gbi - f_im * gbr
        gf_re = jnp.sum(br * gbr + bi * gbi, axis=1, keepdims=True)
        gf_im = jnp.sum(br * gbi - bi * gbr, axis=1, keepdims=True)
        il_re, il_im = lr / den, -li / den
        ga_re = gar_ref[...] + il_re * gf_re + il_im * gf_im
        ga_im = gai_ref[...] + il_re * gf_im - il_im * gf_re
        q_re = f_re * il_re - f_im * il_im
        q_im = f_re * il_im + f_im * il_re
        gl_re = -(q_re * gf_re + q_im * gf_im)
        gl_im = -(q_re * gf_im - q_im * gf_re)
        dlr_ref[...] = gl_re + dt * (ab_re * ga_re + ab_im * ga_im)
        dli_ref[...] = gl_im + dt * (ab_re * ga_im - ab_im * ga_re)
        la_re = lr * ab_re - li * ab_im
        la_im = lr * ab_im + li * ab_re
        dldt_ref[...] = jnp.sum((la_re * ga_re + la_im * ga_im) * dt, axis=2, keepdims=True)

    small = jax.ShapeDtypeStruct((g, 1, p), F32)
    one = jax.ShapeDtypeStruct((g, 1, 1), F32)
    large = jax.ShapeDtypeStruct((g, h, p), F32)
    return pl.pallas_call(body, name="ssm_param_bwd", out_shape=(small, small, one, large, large),
                          compiler_params=_params())(
                              a_re, a_im, log_dt, bt_re, bt_im, gab_re, gab_im, gbb_re, gbb_im)


def _block_diag(m):
    g, h, p = m.shape
    nb = g // SSM_BLOCK_GROUPS
    eye = jnp.eye(SSM_BLOCK_GROUPS, dtype=m.dtype)
    m4 = m.reshape(nb, SSM_BLOCK_GROUPS, h, p)
    return (m4[:, :, :, None, :] * eye[None, :, None, :, None]).reshape(nb, SSM_BLOCK_GROUPS * h, SSM_BLOCK_GROUPS * p)


def _block_diag_take(m, h, p):
    nb = m.shape[0]
    eye = jnp.eye(SSM_BLOCK_GROUPS, dtype=m.dtype)
    m5 = m.reshape(nb, SSM_BLOCK_GROUPS, h, SSM_BLOCK_GROUPS, p)
    return jnp.sum(m5 * eye[None, :, None, :, None], axis=3).reshape(nb * SSM_BLOCK_GROUPS, h, p)


def _scan_tile(br, bi, pw_re, pw_im, reverse):
    row = lax.broadcasted_iota(jnp.int32, br.shape, 0)
    xr, xi = br, bi
    for d in (1, 2, 4):
        ar, ai = pw_re[d - 1:d, :], pw_im[d - 1:d, :]
        if reverse:
            keep = row < SUBLANES - d
            sr = jnp.where(keep, pltpu.roll(xr, SUBLANES - d, 0), 0.0)
            si = jnp.where(keep, pltpu.roll(xi, SUBLANES - d, 0), 0.0)
        else:
            keep = row >= d
            sr = jnp.where(keep, pltpu.roll(xr, d, 0), 0.0)
            si = jnp.where(keep, pltpu.roll(xi, d, 0), 0.0)
        xr, xi = xr + ar * sr - ai * si, xi + ar * si + ai * sr
    return xr, xi


def _ssm_blocks(s):
    tt = min(SSM_TIME_BLOCK, s)
    assert s % tt == 0 and tt % SUBLANES == 0
    ch = SSM_BLOCK_GROUPS * SSM_GROUP
    st = SSM_BLOCK_GROUPS * SSM_STATE
    return tt, s // tt, SSM_WIDTH // ch, ch, st


def _ssm_fwd(proj, bc_re, bc_im, cc_re, cc_im, pw_re, pw_im, dvec):
    s = proj.shape[0]
    tt, nt, nb, ch, st = _ssm_blocks(s)
    n_tiles = tt // SUBLANES

    def body(u_ref, bre_ref, bim_ref, cre_ref, cim_ref, pr_ref, pi_ref, d_ref,
             xr_ref, xi_ref, y_ref, z_ref, carry_ref):
        @pl.when(pl.program_id(1) == 0)
        def _():
            carry_ref[...] = jnp.zeros_like(carry_ref)

        u = u_ref[...]
        ub = u.astype(BF16)
        xr_ref[...] = jnp.dot(ub, bre_ref[0], preferred_element_type=F32)
        xi_ref[...] = jnp.dot(ub, bim_ref[0], preferred_element_type=F32)
        pw_r, pw_i = pr_ref[...], pi_ref[...]

        def step(t, carry):
            cr, ci = carry
            rows = pl.ds(pl.multiple_of(t * SUBLANES, SUBLANES), SUBLANES)
            xr, xi = _scan_tile(xr_ref[rows, :], xi_ref[rows, :], pw_r, pw_i, False)
            xr, xi = xr + pw_r * cr - pw_i * ci, xi + pw_r * ci + pw_i * cr
            xr_ref[rows, :] = xr
            xi_ref[rows, :] = xi
            return (jnp.broadcast_to(xr[SUBLANES - 1:SUBLANES, :], (SUBLANES, st)),
                    jnp.broadcast_to(xi[SUBLANES - 1:SUBLANES, :], (SUBLANES, st)))

        cr, ci = lax.fori_loop(0, n_tiles, step, (carry_ref[0], carry_ref[1]), unroll=2)
        carry_ref[0] = cr
        carry_ref[1] = ci
        y = (lax.dot_general(xr_ref[...].astype(BF16), cre_ref[0], _NT, preferred_element_type=F32)
             + lax.dot_general(xi_ref[...].astype(BF16), cim_ref[0], _NT, preferred_element_type=F32)
             + d_ref[...] * u)
        y_ref[...] = y
        z_ref[...] = _gelu(y).astype(BF16)

    chan = pl.BlockSpec((tt, ch), lambda b, t: (t, b))
    state = pl.BlockSpec((tt, st), lambda b, t: (t, b))
    mat = pl.BlockSpec((1, ch, st), lambda b, t: (b, 0, 0))
    pw = pl.BlockSpec((SUBLANES, st), lambda b, t: (0, b))
    vec = pl.BlockSpec((1, ch), lambda b, t: (0, b))
    x_shape = jax.ShapeDtypeStruct((s, nb * st), F32)
    return pl.pallas_call(
        body, name="ssm_fwd", grid=(nb, nt), in_specs=[chan, mat, mat, mat, mat, pw, pw, vec],
        out_specs=(state, state, chan, chan),
        out_shape=(x_shape, x_shape, jax.ShapeDtypeStruct((s, nb * ch), F32), jax.ShapeDtypeStruct((s, nb * ch), BF16)),
        scratch_shapes=[pltpu.VMEM((2, SUBLANES, st), F32)],
        compiler_params=_params(("parallel", "arbitrary")))(proj, bc_re, bc_im, cc_re, cc_im, pw_re, pw_im, dvec)


def _ssm_bwd(dz, y, proj, x_re, x_im, bc_re, bc_im, cc_re, cc_im, pw_re, pw_im, dvec, ride=None):
    s = proj.shape[0]
    tt, nt, nb, ch, st = _ssm_blocks(s)
    n_tiles = tt // SUBLANES

    def body(dz_ref, y_ref, u_ref, xr_ref, xi_ref, bre_ref, bim_ref, cre_ref, cim_ref, pr_ref, pi_ref, d_ref,
             du_ref, dbr_ref, dbi_ref, dcr_ref, dci_ref, gar_ref, gai_ref, dd_ref,
             lr_ref, li_ref, carry_ref, acc_ref):
        @pl.when(pl.program_id(1) == 0)
        def _():
            for ref in (carry_ref, acc_ref, dbr_ref, dbi_ref, dcr_ref, dci_ref, dd_ref):
                ref[...] = jnp.zeros_like(ref)

        u = u_ref[...]
        dy = dz_ref[...] * _gelu_grad(y_ref[...])
        ub, dyb = u.astype(BF16), dy.astype(BF16)
        lr_ref[...] = jnp.dot(dyb, cre_ref[0], preferred_element_type=F32)
        li_ref[...] = jnp.dot(dyb, cim_ref[0], preferred_element_type=F32)
        pw_r, pw_i = pr_ref[...], -pi_ref[...]
        pwc_r = jnp.concatenate([pw_r[SUBLANES - 1 - i:SUBLANES - i, :] for i in range(SUBLANES)], axis=0)
        pwc_i = jnp.concatenate([pw_i[SUBLANES - 1 - i:SUBLANES - i, :] for i in range(SUBLANES)], axis=0)
        row = lax.broadcasted_iota(jnp.int32, (SUBLANES, st), 0)

        def step(t, carry):
            cr, ci, acc_r, acc_i = carry
            rows = pl.ds(pl.multiple_of((n_tiles - 1 - t) * SUBLANES, SUBLANES), SUBLANES)
            lr, li = _scan_tile(lr_ref[rows, :], li_ref[rows, :], pw_r, pw_i, True)
            lr, li = lr + pwc_r * cr - pwc_i * ci, li + pwc_r * ci + pwc_i * cr
            lr_ref[rows, :] = lr
            li_ref[rows, :] = li
            nr = jnp.where(row < SUBLANES - 1, pltpu.roll(lr, SUBLANES - 1, 0), cr)
            ni = jnp.where(row < SUBLANES - 1, pltpu.roll(li, SUBLANES - 1, 0), ci)
            xr, xi = xr_ref[rows, :], xi_ref[rows, :]
            return (jnp.broadcast_to(lr[0:1, :], (SUBLANES, st)), jnp.broadcast_to(li[0:1, :], (SUBLANES, st)),
                    acc_r + xr * nr + xi * ni, acc_i + xr * ni - xi * nr)

        cr, ci, acc_r, acc_i = lax.fori_loop(
            0, n_tiles, step, (carry_ref[0], carry_ref[1], acc_ref[0], acc_ref[1]), unroll=2)
        carry_ref[0], carry_ref[1] = cr, ci
        acc_ref[0], acc_ref[1] = acc_r, acc_i
        lrb, lib = lr_ref[...].astype(BF16), li_ref[...].astype(BF16)
        du_ref[...] = (lax.dot_general(lrb, bre_ref[0], _NT, preferred_element_type=F32)
                       + lax.dot_general(lib, bim_ref[0], _NT, preferred_element_type=F32) + dy * d_ref[...])
        dbr_ref[0] += lax.dot_general(ub, lrb, _TN, preferred_element_type=F32)
        dbi_ref[0] += lax.dot_general(ub, lib, _TN, preferred_element_type=F32)
        dcr_ref[0] += lax.dot_general(dyb, xr_ref[...].astype(BF16), _TN, preferred_element_type=F32)
        dci_ref[0] += lax.dot_general(dyb, xi_ref[...].astype(BF16), _TN, preferred_element_type=F32)
        dd_ref[...] += jnp.sum(dy * u, axis=0, keepdims=True)

        @pl.when(pl.program_id(1) == nt - 1)
        def _():
            gar_ref[...] = jnp.sum(acc_r, axis=0, keepdims=True)
            gai_ref[...] = jnp.sum(acc_i, axis=0, keepdims=True)

    chan = pl.BlockSpec((tt, ch), lambda b, t: (nt - 1 - t, b))
    state = pl.BlockSpec((tt, st), lambda b, t: (nt - 1 - t, b))
    mat = pl.BlockSpec((1, ch, st), lambda b, t: (b, 0, 0))
    pw = pl.BlockSpec((SUBLANES, st), lambda b, t: (0, b))
    vec = pl.BlockSpec((1, ch), lambda b, t: (0, b))
    svec = pl.BlockSpec((1, st), lambda b, t: (0, b))
    mat_shape = jax.ShapeDtypeStruct((nb, ch, st), F32)
    return _call_with_ride_along(
        body, ride, name="ssm_bwd", grid=(nb, nt), semantics=("parallel", "arbitrary"),
        in_specs=[chan, chan, chan, state, state, mat, mat, mat, mat, pw, pw, vec],
        out_specs=(chan, mat, mat, mat, mat, svec, svec, vec),
        out_shape=(jax.ShapeDtypeStruct((s, nb * ch), F32), mat_shape, mat_shape, mat_shape, mat_shape,
                   jax.ShapeDtypeStruct((1, nb * st), F32), jax.ShapeDtypeStruct((1, nb * st), F32),
                   jax.ShapeDtypeStruct((1, nb * ch), F32)),
        scratch_shapes=[pltpu.VMEM((tt, st), F32), pltpu.VMEM((tt, st), F32),
                        pltpu.VMEM((2, SUBLANES, st), F32), pltpu.VMEM((2, SUBLANES, st), F32)],
        operands=(dz, y, proj, x_re, x_im, bc_re, bc_im, cc_re, cc_im, pw_re, pw_im, dvec))


def _split_dot(x, tri2):
    hi = x.astype(BF16)
    lo = (x - hi.astype(F32)).astype(BF16)
    return jnp.dot(jnp.concatenate([hi, lo], axis=1), tri2, preferred_element_type=F32)


def _tri_and_ones(tk, rel):
    r_i = lax.broadcasted_iota(jnp.int32, (2 * tk, 2 * tk), 0) % tk
    c_i = lax.broadcasted_iota(jnp.int32, (2 * tk, 2 * tk), 1)
    return jnp.where((c_i >= tk) | rel(r_i, c_i), 1.0, 0.0).astype(BF16)


def _attn_scores(qs, kb, own):
    z = lax.dot_general(qs, kb, _NT, preferred_element_type=F32)
    log_beta = jnp.minimum(z, 0.0) - jnp.log(1.0 + jnp.exp(-jnp.abs(z)))
    log_stay = log_beta - z
    if not own:
        return log_beta, log_stay, None
    mask = lax.broadcasted_iota(jnp.int32, z.shape, 1) < lax.broadcasted_iota(jnp.int32, z.shape, 0)
    return log_beta, jnp.where(mask, log_stay, 0.0), mask


def _attn_blocks(s):
    tq, tk = min(ATT_TQ, s), min(ATT_TK, s)
    assert s % tq == 0 and tq % tk == 0
    return tq, tk, tq // tk


def _first_head(shape, axis):
    return lax.broadcasted_iota(jnp.int32, shape, axis) < HEAD_DIM


def _pair_rms(x, g):
    first = _first_head(x.shape, 1)
    sq = x * x
    ms_a = jnp.sum(jnp.where(first, sq, 0.0), axis=1, keepdims=True) * (1.0 / HEAD_DIM)
    ms_b = jnp.sum(jnp.where(first, 0.0, sq), axis=1, keepdims=True) * (1.0 / HEAD_DIM)
    r = jnp.where(first, lax.rsqrt(ms_a + RMS_EPS), lax.rsqrt(ms_b + RMS_EPS))
    return x * r * g, r


def _pair_rms_bwd(x, g, r, dn):
    first = _first_head(x.shape, 1)
    a = dn * g
    ax = a * x
    dot_a = jnp.sum(jnp.where(first, ax, 0.0), axis=1, keepdims=True) * (1.0 / HEAD_DIM)
    dot_b = jnp.sum(jnp.where(first, 0.0, ax), axis=1, keepdims=True) * (1.0 / HEAD_DIM)
    dx = r * a - x * (r * r * r * jnp.where(first, dot_a, dot_b))
    return dx, jnp.sum(dn * x * r, axis=0, keepdims=True)


_Q_BLOCK0 = 2 * SSM_WIDTH // (2 * HEAD_DIM)
_K_BLOCK0 = _Q_BLOCK0 + ATTN_WIDTH // (2 * HEAD_DIM)
_V_BLOCK0 = _K_BLOCK0 + ATTN_WIDTH // (2 * HEAD_DIM)
HEAD_PAIRS = ATTN_HEADS // 2


def _fill_keys(k_ref, v_ref, gk_ref, kn_ref, vb_ref, rows_per_step):
    def fill(c, _):
        rows = pl.ds(pl.multiple_of(c * rows_per_step, rows_per_step), rows_per_step)
        kn, _ = _pair_rms(k_ref[rows, :], gk_ref[...])
        kn_ref[rows, :] = kn.astype(BF16)
        vb_ref[rows, :] = v_ref[rows, :].astype(BF16)
        return 0

    lax.fori_loop(0, k_ref.shape[0] // rows_per_step, fill, 0)


def _call_with_ride_along(body, ride, *, name, grid, semantics, in_specs, out_specs, out_shape, scratch_shapes,
                          operands):
    if ride is None:
        return pl.pallas_call(body, name=name, grid=grid, in_specs=in_specs, out_specs=out_specs, out_shape=out_shape,
                              scratch_shapes=scratch_shapes, compiler_params=_params(semantics))(*operands)
    n, n_in, n_out = len(ride["bufs"]), len(in_specs), len(out_specs)
    into = list(ride.get("into", ()))
    first_out = n_in + n + len(into)
    last = math.prod(grid) - 1

    def wrapped(*refs):
        ins, x_refs = refs[:n_in], refs[n_in:n_in + n]
        outs, out_refs = refs[first_out:first_out + n_out], refs[first_out + n_out:first_out + n_out + n]
        scratch, sems = refs[first_out + n_out + n:-3], refs[-3:]
        step = 0
        for axis, size in enumerate(grid):
            step = step * size + pl.program_id(axis)

        def exchange():
            return _direct_exchange(x_refs, out_refs, *sems, scatter=ride["scatter"], row0s=ride["row0s"])

        @pl.when(step == 0)
        def _():
            exchange()[0]()

        body(*ins, *outs, *scratch)

        @pl.when(step == last)
        def _():
            exchange()[1]()

    any_spec = pl.BlockSpec(memory_space=pl.ANY)
    landing = [jax.ShapeDtypeStruct(b.shape, b.dtype) for b in into] or list(ride["into_shapes"])
    return pl.pallas_call(
        wrapped, name=name, grid=grid, in_specs=list(in_specs) + [any_spec] * (n + len(into)),
        out_specs=tuple(out_specs) + (any_spec,) * n, out_shape=tuple(out_shape) + tuple(landing),
        input_output_aliases={n_in + n + a: n_out + a for a in range(len(into))},
        scratch_shapes=list(scratch_shapes) + _exchange_sems(n),
        compiler_params=_params(("arbitrary",) * len(grid)))(*operands, *ride["bufs"], *into)


def _attn_fwd(proj, gq, gk, ride=None):
    s = proj.shape[0]
    tq, tk, r = _attn_blocks(s)
    pw = 2 * HEAD_DIM

    def body(q_ref, k_ref, v_ref, gq_ref, gk_ref, o_ref, b_ref, kn_ref, vb_ref):
        qi = pl.program_id(1)

        @pl.when(qi == 0)
        def _():
            _fill_keys(k_ref, v_ref, gk_ref, kn_ref, vb_ref, tq)

        qn, _ = _pair_rms(q_ref[...], gq_ref[...])
        qn = qn * (HEAD_DIM ** -0.5)
        first = _first_head((tq, pw), 1)
        tri = _tri_and_ones(tk, lambda row, col: row > col)
        qs = [jnp.where(first, qn, 0.0).astype(BF16), jnp.where(first, 0.0, qn).astype(BF16)]

        def tiles(kq, carry, own):
            o_acc, accs = carry[0], list(carry[1:])
            ks = pl.ds(pl.multiple_of(kq * tq, tq), tq)
            kb, vb = kn_ref[ks, :], vb_ref[ks, :]
            scores = [_attn_scores(q, kb, own) for q in qs]
            sums = [[_split_dot(sc[1][:, u * tk:(u + 1) * tk], tri) for u in range(r)] for sc in scores]
            ws = []
            for h in range(2):
                later = [None] * r
                for u in reversed(range(r)):
                    later[u] = accs[h] + sums[h][u][:, :tk]
                    accs[h] = accs[h] + sums[h][u][:, tk:]
                w = jnp.exp(scores[h][0] + jnp.concatenate(later, axis=1))
                if own:
                    w = jnp.where(scores[h][2], w, 0.0)
                ws.append(w.astype(BF16))
            po = [jnp.dot(w, vb, preferred_element_type=F32) for w in ws]
            return (o_acc + jnp.where(first, po[0], po[1]), *accs)

        zero = jnp.zeros((tq, tk), F32)
        carry = tiles(qi, (jnp.zeros((tq, pw), F32), zero, zero), True)
        o_acc, acc_a, acc_b = lax.fori_loop(0, qi, lambda it, c: tiles(qi - 1 - it, c, False), carry)
        o_ref[...] = o_acc
        b_ref[0] = acc_a[:, 0:1]
        b_ref[1] = acc_b[:, 0:1]

    def cols(block0):
        return pl.BlockSpec((s, pw), lambda hp, i: (0, block0 + hp))

    gain = pl.BlockSpec((1, pw), lambda hp, i: (0, 0))
    return _call_with_ride_along(
        body, ride, name="attn_fwd", grid=(HEAD_PAIRS, s // tq), semantics=("parallel", "arbitrary"),
        in_specs=[pl.BlockSpec((tq, pw), lambda hp, i: (i, _Q_BLOCK0 + hp)), cols(_K_BLOCK0), cols(_V_BLOCK0),
                  gain, gain],
        out_specs=(pl.BlockSpec((tq, pw), lambda hp, i: (i, hp)), pl.BlockSpec((2, tq, 1), lambda hp, i: (hp, i, 0))),
        out_shape=(jax.ShapeDtypeStruct((s, ATTN_WIDTH), F32), jax.ShapeDtypeStruct((ATTN_HEADS, s, 1), F32)),
        scratch_shapes=[pltpu.VMEM((s, pw), BF16), pltpu.VMEM((s, pw), BF16)],
        operands=(proj, proj, proj, gq, gk))


def _attn_bwd(proj, gq, gk, bsum, do, ride=None):
    s = proj.shape[0]
    tq, tk, r = _attn_blocks(s)
    nq = s // tq
    pw = 2 * HEAD_DIM

    def body(q_ref, k_ref, v_ref, gq_ref, gk_ref, b_ref, do_ref,
             dq_ref, dk_ref, dv_ref, dgq_ref, dgk_ref, kn_ref, vb_ref, dkt_ref, dvt_ref):
        qi = pl.program_id(1)

        @pl.when(qi == 0)
        def _():
            _fill_keys(k_ref, v_ref, gk_ref, kn_ref, vb_ref, tq)
            for ref in (dkt_ref, dvt_ref, dgq_ref):
                ref[...] = jnp.zeros_like(ref)

        scale = HEAD_DIM ** -0.5
        q = q_ref[...]
        qn, rq = _pair_rms(q, gq_ref[...])
        qf = qn * scale
        qft = qf.T
        dof = do_ref[...]
        doft = dof.T
        first = _first_head((tq, pw), 1)
        first_t = _first_head((pw, tq), 0)
        tri_upto = _tri_and_ones(tk, lambda row, col: row <= col)
        tri_before = _tri_and_ones(tk, lambda row, col: row < col)
        heads = ((first, first_t), (~first, ~first_t))
        qs = [jnp.where(sel, qf, 0.0).astype(BF16) for sel, _ in heads]
        qst = [jnp.where(sel_t, qft, 0.0).astype(BF16) for _, sel_t in heads]
        dob = [jnp.where(sel, dof, 0.0).astype(BF16) for sel, _ in heads]
        dobt = [jnp.where(sel_t, doft, 0.0).astype(BF16) for _, sel_t in heads]

        def tiles(kq, carry, own):
            dq_acc = carry[0]
            rest, g_prefix = list(carry[1:3]), list(carry[3:5])
            ks = pl.ds(pl.multiple_of(kq * tq, tq), tq)
            kb, vb = kn_ref[ks, :], vb_ref[ks, :]
            scores = [_attn_scores(q, kb, own) for q in qs]
            dws = [lax.dot_general(d, vb, _NT, preferred_element_type=F32) for d in dob]
            sums = [[_split_dot(sc[1][:, u * tk:(u + 1) * tk], tri_upto) for u in range(r)] for sc in scores]
            ws, gs = [], []
            for h in range(2):
                later = [None] * r
                for u in range(r):
                    later[u] = rest[h] - sums[h][u][:, :tk]
                    rest[h] = rest[h] - sums[h][u][:, tk:]
                w = jnp.exp(scores[h][0] + jnp.concatenate(later, axis=1))
                if own:
                    w = jnp.where(scores[h][2], w, 0.0)
                ws.append(w)
                gs.append(w * dws[h])
            gsums = [[_split_dot(g[:, u * tk:(u + 1) * tk], tri_before) for u in range(r)] for g in gs]
            dzs = []
            for h in range(2):
                before = [None] * r
                for u in range(r):
                    before[u] = g_prefix[h] + gsums[h][u][:, :tk]
                    g_prefix[h] = g_prefix[h] + gsums[h][u][:, tk:]
                dz = gs[h] - jnp.exp(scores[h][0]) * (gs[h] + jnp.concatenate(before, axis=1))
                if own:
                    dz = jnp.where(scores[h][2], dz, 0.0)
                dzs.append(dz.astype(BF16))
            pq = [jnp.dot(dz, kb, preferred_element_type=F32) for dz in dzs]
            dkt_ref[:, ks] += (jnp.dot(qst[0], dzs[0], preferred_element_type=F32)
                               + jnp.dot(qst[1], dzs[1], preferred_element_type=F32))
            dvt_ref[:, ks] += (jnp.dot(dobt[0], ws[0].astype(BF16), preferred_element_type=F32)
                               + jnp.dot(dobt[1], ws[1].astype(BF16), preferred_element_type=F32))
            return (dq_acc + jnp.where(first, pq[0], pq[1]), *rest, *g_prefix)

        zero = jnp.zeros((tq, tk), F32)
        carry = (jnp.zeros((tq, pw), F32), jnp.broadcast_to(b_ref[0], (tq, tk)), jnp.broadcast_to(b_ref[1], (tq, tk)),
                 zero, zero)
        carry = lax.fori_loop(0, qi, lambda kq, c: tiles(kq, c, False), carry)
        dq, dgq = _pair_rms_bwd(q, gq_ref[...], rq, tiles(qi, carry, True)[0] * scale)
        dq_ref[...] = dq
        dgq_ref[0] += dgq

        @pl.when(qi == nq - 1)
        def _():
            def finish(c, dgk):
                rows = pl.ds(pl.multiple_of(c * tq, tq), tq)
                k = k_ref[rows, :]
                _, rk = _pair_rms(k, gk_ref[...])
                dk, dgk_c = _pair_rms_bwd(k, gk_ref[...], rk, dkt_ref[:, rows].T)
                dk_ref[rows, :] = dk
                dv_ref[rows, :] = dvt_ref[:, rows].T
                return dgk + dgk_c

            dgk_ref[0] = lax.fori_loop(0, nq, finish, jnp.zeros((1, pw), F32))

    def cols(block0):
        return pl.BlockSpec((s, pw), lambda hp, i: (0, block0 + hp))

    gain = pl.BlockSpec((1, pw), lambda hp, i: (0, 0))
    q_rows = pl.BlockSpec((tq, pw), lambda hp, i: (i, hp))
    all_rows = pl.BlockSpec((s, pw), lambda hp, i: (0, hp))
    pair_gain = pl.BlockSpec((1, 1, pw), lambda hp, i: (hp, 0, 0))
    wide = jax.ShapeDtypeStruct((s, ATTN_WIDTH), F32)
    gains = jax.ShapeDtypeStruct((HEAD_PAIRS, 1, pw), F32)
    return _call_with_ride_along(
        body, ride, name="attn_bwd", grid=(HEAD_PAIRS, nq), semantics=("parallel", "arbitrary"),
        in_specs=[pl.BlockSpec((tq, pw), lambda hp, i: (i, _Q_BLOCK0 + hp)), cols(_K_BLOCK0), cols(_V_BLOCK0),
                  gain, gain, pl.BlockSpec((2, tq, 1), lambda hp, i: (hp, i, 0)), q_rows],
        out_specs=(q_rows, all_rows, all_rows, pair_gain, pair_gain),
        out_shape=(wide, wide, wide, gains, gains),
        scratch_shapes=[pltpu.VMEM((s, pw), BF16), pltpu.VMEM((s, pw), BF16),
                        pltpu.VMEM((pw, s), F32), pltpu.VMEM((pw, s), F32)],
        operands=(proj, proj, proj, gq, gk, bsum, do))


def _reduce_adamw(parts, w, m, v, *, name, row0s=(0,)):
    n, _, cols = parts.shape
    rows = w.shape[0]
    seg_rows = rows // len(row0s)
    tile_rows = 2 * SUBLANES
    tr = max(t for t in range(tile_rows, min(seg_rows, 512) + 1, tile_rows)
             if seg_rows % t == 0 and all(r0 % t == 0 for r0 in row0s))
    seg_tiles = seg_rows // tr

    def parts_tile(i):
        tile = row0s[0] // tr + i
        for l in range(1, len(row0s)):
            tile = jnp.where(i >= l * seg_tiles, row0s[l] // tr + i - l * seg_tiles, tile)
        return tile

    c1 = 1.0 - ADAM_B1 ** ADAM_STEP
    c2 = 1.0 - ADAM_B2 ** ADAM_STEP

    def body(p_ref, w_ref, m_ref, v_ref, g_ref, d_ref, nm_ref, nv_ref):
        g = p_ref[0].astype(F32)
        for i in range(1, n):
            g = g + p_ref[i].astype(F32)
        nm = ADAM_B1 * m_ref[...] + (1.0 - ADAM_B1) * g
        nv = ADAM_B2 * v_ref[...] + (1.0 - ADAM_B2) * (g * g)
        g_ref[...] = g
        nm_ref[...] = nm
        nv_ref[...] = nv
        d_ref[...] = -ADAM_LR * ((nm / c1) / (jnp.sqrt(nv / c2) + ADAM_EPS) + ADAM_WD * w_ref[...])

    row = pl.BlockSpec((tr, cols), lambda i: (i, 0))
    out = jax.ShapeDtypeStruct((rows, cols), F32)
    return pl.pallas_call(
        body, name=name, grid=(rows // tr,),
        in_specs=[pl.BlockSpec((n, tr, cols), lambda i: (0, parts_tile(i), 0)), row, row, row],
        out_specs=(row,) * 4, out_shape=(out,) * 4, compiler_params=_params(("parallel",)))(parts, w, m, v)


def _all_gather(shards, *, name):
    n_arr = len(shards)

    def body(*refs):
        x_refs, out_refs = refs[:n_arr], refs[n_arr:2 * n_arr]
        send_sems, recv_sems, local_sems = refs[2 * n_arr:]
        x, y, c = lax.axis_index("x"), lax.axis_index("y"), lax.axis_index("c")
        me, sibling = (x, y, c), (x, y, 1 - c)
        chips = [(1 - x, y), (x, 1 - y), (1 - x, 1 - y)]

        def slot(a, px, py, pc):
            return out_refs[a].at[4 * px + 2 * py + pc]

        def copy(a, k, block, to, src=None):
            return pltpu.make_async_remote_copy(
                src_ref=slot(a, *block) if src is None else src, dst_ref=slot(a, *block),
                send_sem=send_sems.at[a, k], recv_sem=recv_sems.at[a, k], device_id=to, device_id_type=MESH_IDS)

        arrays = range(n_arr)
        mine = [pltpu.make_async_copy(x_refs[a], slot(a, *me), local_sems.at[a]) for a in arrays]
        for cp in mine:
            cp.start()
        first = [copy(a, 0, me, sibling, src=x_refs[a]) for a in arrays]
        first += [copy(a, 1 + j, me, (*chip, c), src=x_refs[a]) for j, chip in enumerate(chips) for a in arrays]
        for cp in first:
            cp.start()
        passed = []
        for j, chip in enumerate(chips):
            for a in arrays:
                copy(a, 1 + j, (*chip, c), me).wait_recv()
                passed.append(copy(a, 4 + j, (*chip, c), sibling))
                passed[-1].start()
        for a in arrays:
            copy(a, 0, sibling, me).wait_recv()
        for j, chip in enumerate(chips):
            for a in arrays:
                copy(a, 4 + j, (*chip, 1 - c), me).wait_recv()
        for cp in first + passed:
            cp.wait_send()
        for cp in mine:
            cp.wait()

    any_spec = pl.BlockSpec(memory_space=pl.ANY)
    return pl.pallas_call(
        body, name=name, out_shape=tuple(jax.ShapeDtypeStruct((N_DEV,) + s.shape, s.dtype) for s in shards),
        in_specs=[any_spec] * n_arr, out_specs=(any_spec,) * n_arr,
        scratch_shapes=[pltpu.SemaphoreType.DMA((n_arr, 7)), pltpu.SemaphoreType.DMA((n_arr, 7)),
                        pltpu.SemaphoreType.DMA((n_arr,))],
    )(*shards)


def _direct_exchange(x_refs, out_refs, send_sems, recv_sems, local_sems, *, scatter, row0s):
    arrays = range(len(x_refs))
    x, y, c = lax.axis_index("x"), lax.axis_index("y"), lax.axis_index("c")
    me = 4 * x + 2 * y + c

    def src(a, slot):
        return x_refs[a].at[slot] if scatter else x_refs[a]

    def landing(a, sender):
        return out_refs[a].at[sender, pl.ds(row0s[a], x_refs[a].shape[-2]), :]

    def local(a):
        return pltpu.make_async_copy(src(a, me), landing(a, me), local_sems.at[a])

    def copy(a, k, arriving):
        px, py, pc = x ^ (k >> 2), y ^ ((k >> 1) & 1), c ^ (k & 1)
        peer = 4 * px + 2 * py + pc
        return pltpu.make_async_remote_copy(
            src_ref=src(a, peer), dst_ref=landing(a, peer if arriving else me),
            send_sem=send_sems.at[a, k - 1], recv_sem=recv_sems.at[a, k - 1],
            device_id=(px, py, pc), device_id_type=MESH_IDS)

    def start():
        for a in arrays:
            local(a).start()
        for k in range(1, N_DEV):
            for a in arrays:
                copy(a, k, False).start()

    def wait():
        for k in range(1, N_DEV):
            for a in arrays:
                copy(a, k, True).wait_recv()
        for k in range(1, N_DEV):
            for a in arrays:
                copy(a, k, False).wait_send()
        for a in arrays:
            local(a).wait()

    return start, wait


def _exchange_sems(n_arr):
    return [pltpu.SemaphoreType.DMA((n_arr, N_DEV - 1)), pltpu.SemaphoreType.DMA((n_arr, N_DEV - 1)),
            pltpu.SemaphoreType.DMA((n_arr,))]


def _pack_small(vals, tail=0.0):
    flat = jnp.concatenate([vals[n].reshape(-1) for n in SMALL] + [jnp.full((1,), tail, F32)])
    rows = -(-flat.shape[0] // (SUBLANES * PACK_COLS)) * SUBLANES
    return jnp.pad(flat, (0, rows * PACK_COLS - flat.shape[0])).reshape(rows, PACK_COLS)


def _unpack_small(flat, shapes):
    flat = flat.reshape(-1)
    out, off = {}, 0
    for n in SMALL:
        size = math.prod(shapes[n])
        out[n] = flat[off:off + size].reshape(shapes[n])
        off += size
    return out


def _pair_gain(g):
    return jnp.tile(g, 2)[None]


def _ssm_setup(sp):
    g, p = SSM_GROUPS, SSM_STATE
    a_re = sp["ssm_a_re"][:, None, :]
    a_im = sp["ssm_a_im"][:, None, :]
    log_dt = jnp.broadcast_to(sp["ssm_log_dt"][:, None, None], (g, 1, p))
    bt_re = sp["ssm_b_re"].transpose(0, 2, 1)
    bt_im = sp["ssm_b_im"].transpose(0, 2, 1)
    return a_re, a_im, log_dt, bt_re, bt_im


def _layer_fwd(h, p_l, w, sp, ride=None):
    hn = _rms_fwd(h, sp["mix_norm_g"][None], name="rms_mix")
    proj = _mm(hn, w["w_in"], name="mm_proj")

    pw_re, pw_im, bbt_re, bbt_im = _ssm_param_fwd(*_ssm_setup(sp))
    pw_re = pw_re.transpose(1, 0, 2).reshape(SUBLANES, SSM_LANES)
    pw_im = pw_im.transpose(1, 0, 2).reshape(SUBLANES, SSM_LANES)
    ssm_mats = (_block_diag(bbt_re).astype(BF16), _block_diag(bbt_im).astype(BF16),
                _block_diag(sp["ssm_c_re"]).astype(BF16), _block_diag(-sp["ssm_c_im"]).astype(BF16),
                pw_re, pw_im, sp["ssm_d"].reshape(1, SSM_WIDTH))
    x_re, x_im, y, z = _ssm_fwd(proj, *ssm_mats)
    zz = _mm(z, w["ssm_w_glu"], name="mm_glu")

    o, bsum, *landed = _attn_fwd(proj, _pair_gain(sp["q_norm_g"]), _pair_gain(sp["k_norm_g"]), ride)

    bglu = sp["ssm_b_glu"][None]
    ycat = _gate_fwd(zz, bglu, proj, o)
    h2, hn2 = _mm_residual_norm(ycat, w["w_out"], h, sp["ple_norm_g"][None], name="mm_out")
    pp = _mm(p_l, w["w_ple_proj"], name="mm_ple_proj")
    pgl, h3 = _mm_gated_add(hn2, w["w_ple_gate"], h2, pp, name="mm_ple_gate")
    saved = dict(h=h, hn=hn, proj=proj, ssm_mats=ssm_mats, x_re=x_re, x_im=x_im, y=y, z=z,
                 zz=zz, bsum=bsum, o=o, ycat=ycat, h2=h2, hn2=hn2,
                 pgl=pgl, pp=pp, p_l=p_l, bglu=bglu)
    return h3, saved, landed


def _layer_bwd(dh3, sv, w, sp, ship=None):
    g, hh, p = SSM_GROUPS, SSM_GROUP, SSM_STATE
    grads = {}
    ship = ship or (lambda stage, grads_so_far, parts: None)
    dgp, dpp = _ple_bwd(dh3, sv["pgl"], sv["pp"])
    grads["w_ple_gate"] = _mm(sv["hn2"], dgp, ta=True, out_dtype=BF16, name="mm_d_ple_gate")
    grads["w_ple_proj"] = _mm(sv["p_l"], dpp, ta=True, out_dtype=BF16, name="mm_d_ple_proj")
    dh2, dg2 = _mm_norm_grad(dgp, w["w_ple_gate"], sv["h2"], sp["ple_norm_g"][None], dh3, name="mm_dhn2")
    grads["ple_norm_g"] = dg2[0]
    grads["w_out"] = _mm(sv["ycat"], dh2, ta=True, out_dtype=BF16, name="mm_d_out")
    dyc = _mm(dh2, w["w_out"], tb=True, name="mm_dycat")
    dzz, dgs, do, dga, dbglu = _gate_bwd(dyc, sv["zz"], sv["bglu"], sv["proj"], sv["o"])
    grads["ssm_b_glu"] = dbglu[0]

    dq, dk, dv, dgq, dgk, *parts = _attn_bwd(sv["proj"], _pair_gain(sp["q_norm_g"]), _pair_gain(sp["k_norm_g"]),
                                             sv["bsum"], do, ship("attn", grads, None))
    grads["q_norm_g"] = jnp.sum(dgq.reshape(ATTN_HEADS, HEAD_DIM), axis=0)
    grads["k_norm_g"] = jnp.sum(dgk.reshape(ATTN_HEADS, HEAD_DIM), axis=0)

    grads["ssm_w_glu"] = _mm(sv["z"], dzz, ta=True, out_dtype=BF16, name="mm_d_glu")
    dz = _mm(dzz, w["ssm_w_glu"], tb=True, name="mm_dz")
    du, dbc_re, dbc_im, dcc_re, dcc_im, gab_re, gab_im, dd, *landed = _ssm_bwd(
        dz, sv["y"], sv["proj"], sv["x_re"], sv["x_im"], *sv["ssm_mats"], ride=ship("ssm", grads, parts))
    parts = parts[:len(parts) - len(landed)] + landed
    grads["ssm_d"] = dd.reshape(g, hh)
    grads["ssm_c_re"] = _block_diag_take(dcc_re, hh, p)
    grads["ssm_c_im"] = -_block_diag_take(dcc_im, hh, p)
    da_re, da_im, dldt, dbt_re, dbt_im = _ssm_param_bwd(
        *_ssm_setup(sp), gab_re.reshape(g, 1, p), gab_im.reshape(g, 1, p),
        _block_diag_take(dbc_re, hh, p), _block_diag_take(dbc_im, hh, p))
    grads["ssm_a_re"], grads["ssm_a_im"] = da_re[:, 0, :], da_im[:, 0, :]
    grads["ssm_log_dt"] = dldt[:, 0, 0]
    grads["ssm_b_re"], grads["ssm_b_im"] = dbt_re.transpose(0, 2, 1), dbt_im.transpose(0, 2, 1)

    dproj = jnp.concatenate([du, dgs, dq, dk, dv, dga], axis=1)
    grads["w_in"] = _mm(sv["hn"], dproj, ta=True, out_dtype=BF16, name="mm_d_in")
    dh, dg1, *landed = _mm_norm_grad(dproj, w["w_in"], sv["h"], sp["mix_norm_g"][None], dh2, name="mm_dhn",
                                     ride=ship("dhn", grads, parts))
    parts = landed + parts[len(landed):]
    grads["mix_norm_g"] = dg1[0]
    return dh, grads, parts


def _local_step(x, p, target, big, small, fetch=None, ship=None):
    h = x
    big = list(big)
    saved = []
    for l in range(DEPTH):
        sp = {n: small[n][l] for n in SMALL}
        h, sv, landed = _layer_fwd(h, p[l], big[l], sp, fetch[0] if fetch and l == 0 else None)
        if landed:
            big[DEPTH - 1] = fetch[1](landed)
        saved.append(sv)
    dh, loss_parts = _loss_head(h, target)
    grads = [None] * DEPTH
    shipped = []
    for l in reversed(range(DEPTH)):
        sp = {n: small[n][l] for n in SMALL}
        carry = None
        if ship and l == 0:
            def carry(stage, grads_so_far, parts):
                return ship(stage, grads[DEPTH - 1], grads_so_far, parts)
        dh, grads[l], landed = _layer_bwd(dh, saved[l], big[l], sp, carry)
        shipped = landed or shipped
    return jnp.sum(loss_parts), dh, grads, shipped


def kernel(x, p, mix_norm_g, w_in, ssm_a_re, ssm_a_im, ssm_log_dt, ssm_b_re, ssm_b_im, ssm_c_re, ssm_c_im, ssm_d, ssm_w_glu, ssm_b_glu, q_norm_g, k_norm_g, w_out, ple_norm_g, w_ple_gate, w_ple_proj, loss_target, m_mix_norm_g, m_w_in, m_ssm_a_re, m_ssm_a_im, m_ssm_log_dt, m_ssm_b_re, m_ssm_b_im, m_ssm_c_re, m_ssm_c_im, m_ssm_d, m_ssm_w_glu, m_ssm_b_glu, m_q_norm_g, m_k_norm_g, m_w_out, m_ple_norm_g, m_w_ple_gate, m_w_ple_proj, v_mix_norm_g, v_w_in, v_ssm_a_re, v_ssm_a_im, v_ssm_log_dt, v_ssm_b_re, v_ssm_b_im, v_ssm_c_re, v_ssm_c_im, v_ssm_d, v_ssm_w_glu, v_ssm_b_glu, v_q_norm_g, v_k_norm_g, v_w_out, v_ple_norm_g, v_w_ple_gate, v_w_ple_proj):
    given = dict(locals())
    wts = {n: given[n] for n in WEIGHTS}
    mom = {n: given["m_" + n] for n in WEIGHTS}
    var = {n: given["v_" + n] for n in WEIGHTS}

    def rows2d(a):
        return a.reshape(-1, a.shape[-1])

    assert DEPTH == 2
    groups = (("w_in",), ("ssm_w_glu", "w_ple_proj"), ("w_out", "w_ple_gate"))
    layer_rows = [sum(wts[n].shape[1] for n in names) for names in groups]
    widths = [wts[names[0]].shape[2] for names in groups]

    def shards_of(l):
        return [jnp.concatenate([wts[n][l].astype(BF16) for n in names], axis=0) for names in groups]

    def weights_of(gathered):
        full = {}
        for names, got in zip(groups, gathered):
            row0 = 0
            for n in names:
                r = wts[n].shape[1]
                blocks = got[:, row0:row0 + r, :]
                full[n] = blocks.reshape(N_DEV * r, -1) if n in ROW_SHARDED else blocks.transpose(1, 0, 2).reshape(r, -1)
                row0 += r
        return full

    def blocks_to_send(layer_grads, which):
        def blocks(n):
            g = layer_grads[n]
            if n in ROW_SHARDED:
                return g.reshape(N_DEV, -1, g.shape[1])
            return g.reshape(g.shape[0], N_DEV, -1).transpose(1, 0, 2)

        return [jnp.concatenate([blocks(n) for n in groups[gi]], axis=1) for gi in which]

    fetch = (dict(bufs=shards_of(1), scatter=False, row0s=[0] * len(groups),
                  into_shapes=[jax.ShapeDtypeStruct((N_DEV, r, c), BF16) for r, c in zip(layer_rows, widths)]),
             weights_of)

    def ship(stage, last_grads, first_grads, parts):
        if stage == "attn":
            return dict(bufs=blocks_to_send(last_grads, (0, 1, 2)), scatter=True, row0s=layer_rows, into_shapes=[
                jax.ShapeDtypeStruct((N_DEV, DEPTH * r, c), BF16) for r, c in zip(layer_rows, widths)])
        which = (1, 2) if stage == "ssm" else (0,)
        return dict(bufs=blocks_to_send(first_grads, which), scatter=True, row0s=[0] * len(which),
                    into=[parts[gi] for gi in which])

    first = weights_of(_all_gather(shards_of(0), name="gather_weights"))
    small = {n: wts[n] for n in SMALL}
    loss, dx, grads, parts = _local_step(x[0], p[:, 0], loss_target[0], [first, None], small, fetch, ship)

    big_out = {}
    for names, part, rows in zip(groups, parts, layer_rows):
        row0 = 0
        for n in names:
            res = _reduce_adamw(part, rows2d(wts[n]), rows2d(mom[n]), rows2d(var[n]),
                                row0s=[l * rows + row0 for l in range(DEPTH)], name="adamw_" + n)
            big_out[n] = [a.reshape(wts[n].shape) for a in res]
            row0 += wts[n].shape[1]

    small_shapes = {n: wts[n].shape for n in SMALL}
    local_small = _pack_small({n: jnp.stack([grads[l][n] for l in range(DEPTH)]) for n in SMALL}, tail=loss)
    (all_small,) = _all_gather([local_small], name="gather_small_grads")
    small_out = _reduce_adamw(all_small, _pack_small(small), _pack_small({n: mom[n] for n in SMALL}),
                              _pack_small({n: var[n] for n in SMALL}), name="adamw_replicated")
    loss = small_out[0].reshape(-1)[sum(math.prod(s) for s in small_shapes.values())]
    small_out = [_unpack_small(a, small_shapes) for a in small_out]

    outs = [loss, dx[None]]
    for k in range(4):
        outs += [big_out[n][k] if n in BIG else small_out[k][n] for n in WEIGHTS]
    return tuple(outs)
```
